```python
import math
import jax, jax.numpy as jnp
from jax import lax
import numpy as np

D_MODEL = 2048
BATCH = 8
SEQ = 8192
DEPTH = 4

N_MEM = 256
HEAD_DIM = 128
MEM_HEADS = 4
MEM_WIDTH = MEM_HEADS * HEAD_DIM
LRU_WIDTH = D_MODEL - MEM_WIDTH
LRU_BLOCKS = LRU_WIDTH // HEAD_DIM
CONV_WIDTH = 4
LRU_C = 8.0
DIL_GROUPS = ((128, 1), (512, 4), (2048, 16))
DIL_WIDTH = D_MODEL - MEM_WIDTH
DIL_HEADS = DIL_WIDTH // HEAD_DIM
HEADS_PER_GROUP = DIL_HEADS // len(DIL_GROUPS)
D_FF = -(-8 * D_MODEL // (3 * 256)) * 256
Q_BLOCK = 128
N_A = DEPTH // 2
N_B = DEPTH - N_A
RMS_EPS = 1e-6
NEG_INF = -1e30

kernel_name = "yoco_rglru_dilated_swa_hybrid"


def rms_norm(x, g):
    xf = x.astype(jnp.float32)
    y = xf * lax.rsqrt(jnp.mean(xf * xf, axis=-1, keepdims=True) + RMS_EPS)
    return (y * g.astype(jnp.float32)).astype(x.dtype)


def swiglu_sublayer(h, pre_g, post_g, w_ffn_in, w_ffn_out):
    hn = rms_norm(h, pre_g)
    gu = hn @ w_ffn_in
    g, u = gu[..., :D_FF], gu[..., D_FF:]
    y = (jax.nn.silu(g) * u) @ w_ffn_out
    return h + rms_norm(y, post_g)


def memory_attention(q_flat, mem_n, w_mem_kv):
    bsz, s, _ = q_flat.shape
    q = q_flat.reshape(bsz, s, MEM_HEADS, HEAD_DIM)
    kv = mem_n @ w_mem_kv
    m = kv.shape[1]
    k = kv[..., :MEM_WIDTH].reshape(bsz, m, MEM_HEADS, HEAD_DIM)
    v = kv[..., MEM_WIDTH:].reshape(bsz, m, MEM_HEADS, HEAD_DIM)
    sc = jnp.einsum('bshd,bmhd->bhsm', q, k).astype(jnp.float32) * (HEAD_DIM ** -0.5)
    p = jax.nn.softmax(sc, axis=-1).astype(v.dtype)
    o = jnp.einsum('bhsm,bmhd->bshd', p, v)
    return o.reshape(bsz, s, MEM_WIDTH)


def causal_depthwise_conv(x, w, b):
    c = x.shape[-1]
    y = lax.conv_general_dilated(x, w[:, None, :], window_strides=(1,),
                                 padding=((CONV_WIDTH - 1, 0),),
                                 dimension_numbers=('NWC', 'WIO', 'NWC'),
                                 feature_group_count=c)
    return y + b


def _linear_recurrence_combine(c1, c2):
    a1, b1 = c1
    a2, b2 = c2
    return a1 * a2, a2 * b1 + b2


def rg_lru(x, w_a, b_a, w_x, b_x, lam):
    bsz, s, c = x.shape
    xb = x.reshape(bsz, s, LRU_BLOCKS, HEAD_DIM)
    r = jax.nn.sigmoid(jnp.einsum('bsnc,ncd->bsnd', xb, w_a).reshape(bsz, s, c) + b_a)
    i = jax.nn.sigmoid(jnp.einsum('bsnc,ncd->bsnd', xb, w_x).reshape(bsz, s, c) + b_x)
    log_a = -LRU_C * r.astype(jnp.float32) * jax.nn.softplus(-lam.astype(jnp.float32))
    a = jnp.exp(log_a)
    bterm = jnp.sqrt(-jnp.expm1(2.0 * log_a)) * (i * x).astype(jnp.float32)
    _, h = lax.associative_scan(_linear_recurrence_combine, (a, bterm), axis=1)
    return h.astype(x.dtype)


def alibi_slopes(n):
    return jnp.exp2(-8.0 * (jnp.arange(n, dtype=jnp.float32) + 1.0) / n)


def dilated_group_attention(q, k, v, window, dil, slopes):
    bsz, s, h, dh = q.shape
    span = dil * Q_BLOCK
    sp = -(-s // span) * span
    sub_len = sp // dil
    nb = sub_len // Q_BLOCK
    pad = ((0, 0), (0, sp - s), (0, 0), (0, 0))

    def to_blocks(t):
        t = jnp.pad(t, pad).reshape(bsz, sub_len, dil, h, dh).transpose(0, 2, 1, 3, 4)
        return t.reshape(bsz, dil, nb, Q_BLOCK, h, dh)

    def with_prev(t):
        prev = jnp.pad(t, ((0, 0), (0, 0), (1, 0), (0, 0), (0, 0), (0, 0)))[:, :, :-1]
        return jnp.concatenate([prev, t], axis=3)

    qb = to_blocks(q)
    kb = with_prev(to_blocks(k))
    vb = with_prev(to_blocks(v))
    sc = jnp.einsum('brnqhd,brnkhd->brnhqk', qb, kb).astype(jnp.float32) * (dh ** -0.5)

    qi = jnp.arange(Q_BLOCK)[:, None]
    ki = jnp.arange(2 * Q_BLOCK)[None, :]
    rel = qi + Q_BLOCK - ki
    key_pos = jnp.arange(nb)[:, None, None] * Q_BLOCK - Q_BLOCK + ki[None]
    n_keys = window // dil
    valid = (rel >= 0)[None] & (rel <= n_keys)[None] & (key_pos >= 0)
    bias = -slopes[:, None, None] * (rel * dil).astype(jnp.float32)[None]
    sc = jnp.where(valid[:, None], sc + bias, NEG_INF)

    mx = jnp.max(sc, axis=-1, keepdims=True)
    lse = mx + jnp.log(jnp.sum(jnp.exp(sc - mx), axis=-1, keepdims=True))
    p = jnp.exp(sc - lse).astype(v.dtype)
    out = jnp.einsum('brnhqk,brnkhd->brnqhd', p, vb)
    out = out.reshape(bsz, dil, sub_len, h, dh).transpose(0, 2, 1, 3, 4).reshape(bsz, sp, h, dh)[:, :s]
    lse = lse[..., 0].transpose(0, 1, 2, 4, 3).reshape(bsz, dil, sub_len, h)
    lse = lse.transpose(0, 2, 1, 3).reshape(bsz, sp, h)[:, :s]
    return out, lse


def dilated_mixture_attention(q, k_shared, v_shared):
    slopes = alibi_slopes(DIL_HEADS)
    outs, lses = [], []
    for g, (window, dil) in enumerate(DIL_GROUPS):
        hs = slice(g * HEADS_PER_GROUP, (g + 1) * HEADS_PER_GROUP)
        o, l = dilated_group_attention(q[:, :, hs], k_shared[:, :, hs], v_shared[:, :, hs],
                                       window, dil, slopes[hs])
        outs.append(o)
        lses.append(l)
    w = jax.nn.softmax(jnp.stack(lses, axis=0), axis=0)
    comb = jnp.concatenate([outs[g] * w[g][..., None].astype(outs[g].dtype)
                            for g in range(len(DIL_GROUPS))], axis=2)
    bsz, s = q.shape[:2]
    return comb.reshape(bsz, s, DIL_WIDTH)


def recurrent_layer(h, mem_n, pre_mix_g, post_mix_g, pre_ffn_g, post_ffn_g, w_in, conv_w, conv_b,
                    gate_a_w, gate_a_b, gate_x_w, gate_x_b, lam, w_mem_kv, w_out, w_ffn_in, w_ffn_out):
    hn = rms_norm(h, pre_mix_g)
    proj = hn @ w_in
    xb = proj[..., :LRU_WIDTH]
    gb = proj[..., LRU_WIDTH:2 * LRU_WIDTH]
    qm = proj[..., 2 * LRU_WIDTH:]
    xc = causal_depthwise_conv(xb, conv_w, conv_b)
    y = rg_lru(xc, gate_a_w, gate_a_b, gate_x_w, gate_x_b, lam) * jax.nn.gelu(gb)
    m = memory_attention(qm, mem_n, w_mem_kv)
    mix = jnp.concatenate([y, m], axis=-1) @ w_out
    h = h + rms_norm(mix, post_mix_g)
    return swiglu_sublayer(h, pre_ffn_g, post_ffn_g, w_ffn_in, w_ffn_out)


def dilated_layer(h, mem_n, k_shared, v_shared, pre_mix_g, post_mix_g, pre_ffn_g, post_ffn_g,
                  w_in, w_mem_kv, w_out, w_ffn_in, w_ffn_out):
    bsz, s, _ = h.shape
    hn = rms_norm(h, pre_mix_g)
    proj = hn @ w_in
    q = proj[..., :DIL_WIDTH].reshape(bsz, s, DIL_HEADS, HEAD_DIM)
    a = dilated_mixture_attention(q, k_shared, v_shared)
    m = memory_attention(proj[..., DIL_WIDTH:], mem_n, w_mem_kv)
    mix = jnp.concatenate([a, m], axis=-1) @ w_out
    h = h + rms_norm(mix, post_mix_g)
    return swiglu_sublayer(h, pre_ffn_g, post_ffn_g, w_ffn_in, w_ffn_out)


def _fwd_setup_inputs(seed: int = 0) -> dict:
    key = jax.random.key(seed)
    keys = iter(jax.random.split(key, 64))
    f32 = jnp.float32
    D = D_MODEL

    def nrm(shape, scale):
        return jax.random.normal(next(keys), shape, f32) * scale

    def gain(shape):
        return 1.0 + nrm(shape, 0.05)

    a0 = jax.random.uniform(next(keys), (N_A, LRU_WIDTH), f32, 0.9, 0.999)
    sig_l = a0 ** (1.0 / LRU_C)
    lam = jnp.log(sig_l) - jnp.log1p(-sig_l)

    return {
        "x": nrm((BATCH, SEQ, D), 1.0),
        "mem": nrm((BATCH, N_MEM, D), 1.0),
        "mem_norm_g": gain((D,)),
        "a_pre_mix_g": gain((N_A, D)),
        "a_post_mix_g": gain((N_A, D)),
        "a_pre_ffn_g": gain((N_A, D)),
        "a_post_ffn_g": gain((N_A, D)),
        "a_w_in": nrm((N_A, D, 2 * LRU_WIDTH + MEM_WIDTH), D ** -0.5),
        "a_conv_w": nrm((N_A, CONV_WIDTH, LRU_WIDTH), CONV_WIDTH ** -0.5),
        "a_conv_b": nrm((N_A, LRU_WIDTH), 0.02),
        "a_gate_a_w": nrm((N_A, LRU_BLOCKS, HEAD_DIM, HEAD_DIM), HEAD_DIM ** -0.5),
        "a_gate_a_b": nrm((N_A, LRU_WIDTH), 0.1),
        "a_gate_x_w": nrm((N_A, LRU_BLOCKS, HEAD_DIM, HEAD_DIM), HEAD_DIM ** -0.5),
        "a_gate_x_b": nrm((N_A, LRU_WIDTH), 0.1),
        "a_lambda": lam,
        "a_w_mem_kv": nrm((N_A, D, 2 * MEM_WIDTH), D ** -0.5),
        "a_w_out": nrm((N_A, LRU_WIDTH + MEM_WIDTH, D), (LRU_WIDTH + MEM_WIDTH) ** -0.5),
        "a_w_ffn_in": nrm((N_A, D, 2 * D_FF), D ** -0.5),
        "a_w_ffn_out": nrm((N_A, D_FF, D), D_FF ** -0.5),
        "kv_norm_g": gain((D,)),
        "w_kv_shared": nrm((D, 2 * DIL_WIDTH), D ** -0.5),
        "b_pre_mix_g": gain((N_B, D)),
        "b_post_mix_g": gain((N_B, D)),
        "b_pre_ffn_g": gain((N_B, D)),
        "b_post_ffn_g": gain((N_B, D)),
        "b_w_in": nrm((N_B, D, DIL_WIDTH + MEM_WIDTH), D ** -0.5),
        "b_w_mem_kv": nrm((N_B, D, 2 * MEM_WIDTH), D ** -0.5),
        "b_w_out": nrm((N_B, DIL_WIDTH + MEM_WIDTH, D), (DIL_WIDTH + MEM_WIDTH) ** -0.5),
        "b_w_ffn_in": nrm((N_B, D, 2 * D_FF), D ** -0.5),
        "b_w_ffn_out": nrm((N_B, D_FF, D), D_FF ** -0.5),
    }


def _fwd_reference(x, mem, mem_norm_g, a_pre_mix_g, a_post_mix_g, a_pre_ffn_g, a_post_ffn_g, a_w_in,
              a_conv_w, a_conv_b, a_gate_a_w, a_gate_a_b, a_gate_x_w, a_gate_x_b, a_lambda,
              a_w_mem_kv, a_w_out, a_w_ffn_in, a_w_ffn_out, kv_norm_g, w_kv_shared,
              b_pre_mix_g, b_post_mix_g, b_pre_ffn_g, b_post_ffn_g, b_w_in, b_w_mem_kv, b_w_out,
              b_w_ffn_in, b_w_ffn_out):
    bsz, s, _ = x.shape
    mem_n = rms_norm(mem, mem_norm_g)
    h = x
    k_shared = v_shared = None
    for l in range(DEPTH):
        if l < N_A:
            h = recurrent_layer(h, mem_n, a_pre_mix_g[l], a_post_mix_g[l], a_pre_ffn_g[l],
                                a_post_ffn_g[l], a_w_in[l], a_conv_w[l], a_conv_b[l],
                                a_gate_a_w[l], a_gate_a_b[l], a_gate_x_w[l], a_gate_x_b[l],
                                a_lambda[l], a_w_mem_kv[l], a_w_out[l], a_w_ffn_in[l],
                                a_w_ffn_out[l])
            if l == N_A - 1:
                kv = rms_norm(h, kv_norm_g) @ w_kv_shared
                k_shared = kv[..., :DIL_WIDTH].reshape(bsz, s, DIL_HEADS, HEAD_DIM)
                v_shared = kv[..., DIL_WIDTH:].reshape(bsz, s, DIL_HEADS, HEAD_DIM)
        else:
            j = l - N_A
            h = dilated_layer(h, mem_n, k_shared, v_shared, b_pre_mix_g[j], b_post_mix_g[j],
                              b_pre_ffn_g[j], b_post_ffn_g[j], b_w_in[j], b_w_mem_kv[j],
                              b_w_out[j], b_w_ffn_in[j], b_w_ffn_out[j])
    return h


import jax as _jax
import jax.numpy as _jnp

TWIN_FORMAT = 'train_step'
FWD_PARAMS = ['x', 'mem', 'mem_norm_g', 'a_pre_mix_g', 'a_post_mix_g', 'a_pre_ffn_g', 'a_post_ffn_g', 'a_w_in', 'a_conv_w', 'a_conv_b', 'a_gate_a_w', 'a_gate_a_b', 'a_gate_x_w', 'a_gate_x_b', 'a_lambda', 'a_w_mem_kv', 'a_w_out', 'a_w_ffn_in', 'a_w_ffn_out', 'kv_norm_g', 'w_kv_shared', 'b_pre_mix_g', 'b_post_mix_g', 'b_pre_ffn_g', 'b_post_ffn_g', 'b_w_in', 'b_w_mem_kv', 'b_w_out', 'b_w_ffn_in', 'b_w_ffn_out']
TWIN_WEIGHTS = ['mem_norm_g', 'a_pre_mix_g', 'a_post_mix_g', 'a_pre_ffn_g', 'a_post_ffn_g', 'a_w_in', 'a_conv_w', 'a_conv_b', 'a_gate_a_w', 'a_gate_a_b', 'a_gate_x_w', 'a_gate_x_b', 'a_lambda', 'a_w_mem_kv', 'a_w_out', 'a_w_ffn_in', 'a_w_ffn_out', 'kv_norm_g', 'w_kv_shared', 'b_pre_mix_g', 'b_post_mix_g', 'b_pre_ffn_g', 'b_post_ffn_g', 'b_w_in', 'b_w_mem_kv', 'b_w_out', 'b_w_ffn_in', 'b_w_ffn_out']
TWIN_DIFF_INPUT = 'x'
TWIN_INPUTS = ['x', 'mem', 'mem_norm_g', 'a_pre_mix_g', 'a_post_mix_g', 'a_pre_ffn_g', 'a_post_ffn_g', 'a_w_in', 'a_conv_w', 'a_conv_b', 'a_gate_a_w', 'a_gate_a_b', 'a_gate_x_w', 'a_gate_x_b', 'a_lambda', 'a_w_mem_kv', 'a_w_out', 'a_w_ffn_in', 'a_w_ffn_out', 'kv_norm_g', 'w_kv_shared', 'b_pre_mix_g', 'b_post_mix_g', 'b_pre_ffn_g', 'b_post_ffn_g', 'b_w_in', 'b_w_mem_kv', 'b_w_out', 'b_w_ffn_in', 'b_w_ffn_out', 'loss_target', 'm_mem_norm_g', 'm_a_pre_mix_g', 'm_a_post_mix_g', 'm_a_pre_ffn_g', 'm_a_post_ffn_g', 'm_a_w_in', 'm_a_conv_w', 'm_a_conv_b', 'm_a_gate_a_w', 'm_a_gate_a_b', 'm_a_gate_x_w', 'm_a_gate_x_b', 'm_a_lambda', 'm_a_w_mem_kv', 'm_a_w_out', 'm_a_w_ffn_in', 'm_a_w_ffn_out', 'm_kv_norm_g', 'm_w_kv_shared', 'm_b_pre_mix_g', 'm_b_post_mix_g', 'm_b_pre_ffn_g', 'm_b_post_ffn_g', 'm_b_w_in', 'm_b_w_mem_kv', 'm_b_w_out', 'm_b_w_ffn_in', 'm_b_w_ffn_out', 'v_mem_norm_g', 'v_a_pre_mix_g', 'v_a_post_mix_g', 'v_a_pre_ffn_g', 'v_a_post_ffn_g', 'v_a_w_in', 'v_a_conv_w', 'v_a_conv_b', 'v_a_gate_a_w', 'v_a_gate_a_b', 'v_a_gate_x_w', 'v_a_gate_x_b', 'v_a_lambda', 'v_a_w_mem_kv', 'v_a_w_out', 'v_a_w_ffn_in', 'v_a_w_ffn_out', 'v_kv_norm_g', 'v_w_kv_shared', 'v_b_pre_mix_g', 'v_b_post_mix_g', 'v_b_pre_ffn_g', 'v_b_post_ffn_g', 'v_b_w_in', 'v_b_w_mem_kv', 'v_b_w_out', 'v_b_w_ffn_in', 'v_b_w_ffn_out']
TWIN_OUTPUTS = ['loss', 'grad_x', 'grad_mem_norm_g', 'grad_a_pre_mix_g', 'grad_a_post_mix_g', 'grad_a_pre_ffn_g', 'grad_a_post_ffn_g', 'grad_a_w_in', 'grad_a_conv_w', 'grad_a_conv_b', 'grad_a_gate_a_w', 'grad_a_gate_a_b', 'grad_a_gate_x_w', 'grad_a_gate_x_b', 'grad_a_lambda', 'grad_a_w_mem_kv', 'grad_a_w_out', 'grad_a_w_ffn_in', 'grad_a_w_ffn_out', 'grad_kv_norm_g', 'grad_w_kv_shared', 'grad_b_pre_mix_g', 'grad_b_post_mix_g', 'grad_b_pre_ffn_g', 'grad_b_post_ffn_g', 'grad_b_w_in', 'grad_b_w_mem_kv', 'grad_b_w_out', 'grad_b_w_ffn_in', 'grad_b_w_ffn_out', 'delta_mem_norm_g', 'delta_a_pre_mix_g', 'delta_a_post_mix_g', 'delta_a_pre_ffn_g', 'delta_a_post_ffn_g', 'delta_a_w_in', 'delta_a_conv_w', 'delta_a_conv_b', 'delta_a_gate_a_w', 'delta_a_gate_a_b', 'delta_a_gate_x_w', 'delta_a_gate_x_b', 'delta_a_lambda', 'delta_a_w_mem_kv', 'delta_a_w_out', 'delta_a_w_ffn_in', 'delta_a_w_ffn_out', 'delta_kv_norm_g', 'delta_w_kv_shared', 'delta_b_pre_mix_g', 'delta_b_post_mix_g', 'delta_b_pre_ffn_g', 'delta_b_post_ffn_g', 'delta_b_w_in', 'delta_b_w_mem_kv', 'delta_b_w_out', 'delta_b_w_ffn_in', 'delta_b_w_ffn_out', 'new_m_mem_norm_g', 'new_m_a_pre_mix_g', 'new_m_a_post_mix_g', 'new_m_a_pre_ffn_g', 'new_m_a_post_ffn_g', 'new_m_a_w_in', 'new_m_a_conv_w', 'new_m_a_conv_b', 'new_m_a_gate_a_w', 'new_m_a_gate_a_b', 'new_m_a_gate_x_w', 'new_m_a_gate_x_b', 'new_m_a_lambda', 'new_m_a_w_mem_kv', 'new_m_a_w_out', 'new_m_a_w_ffn_in', 'new_m_a_w_ffn_out', 'new_m_kv_norm_g', 'new_m_w_kv_shared', 'new_m_b_pre_mix_g', 'new_m_b_post_mix_g', 'new_m_b_pre_ffn_g', 'new_m_b_post_ffn_g', 'new_m_b_w_in', 'new_m_b_w_mem_kv', 'new_m_b_w_out', 'new_m_b_w_ffn_in', 'new_m_b_w_ffn_out', 'new_v_mem_norm_g', 'new_v_a_pre_mix_g', 'new_v_a_post_mix_g', 'new_v_a_pre_ffn_g', 'new_v_a_post_ffn_g', 'new_v_a_w_in', 'new_v_a_conv_w', 'new_v_a_conv_b', 'new_v_a_gate_a_w', 'new_v_a_gate_a_b', 'new_v_a_gate_x_w', 'new_v_a_gate_x_b', 'new_v_a_lambda', 'new_v_a_w_mem_kv', 'new_v_a_w_out', 'new_v_a_w_ffn_in', 'new_v_a_w_ffn_out', 'new_v_kv_norm_g', 'new_v_w_kv_shared', 'new_v_b_pre_mix_g', 'new_v_b_post_mix_g', 'new_v_b_pre_ffn_g', 'new_v_b_post_ffn_g', 'new_v_b_w_in', 'new_v_b_w_mem_kv', 'new_v_b_w_out', 'new_v_b_w_ffn_in', 'new_v_b_w_ffn_out']
TWIN_LEAF_KINDS = {'loss': 'loss', 'grad_x': 'grad_x', 'grad_mem_norm_g': 'grad_w', 'grad_a_pre_mix_g': 'grad_w', 'grad_a_post_mix_g': 'grad_w', 'grad_a_pre_ffn_g': 'grad_w', 'grad_a_post_ffn_g': 'grad_w', 'grad_a_w_in': 'grad_w', 'grad_a_conv_w': 'grad_w', 'grad_a_conv_b': 'grad_w', 'grad_a_gate_a_w': 'grad_w', 'grad_a_gate_a_b': 'grad_w', 'grad_a_gate_x_w': 'grad_w', 'grad_a_gate_x_b': 'grad_w', 'grad_a_lambda': 'grad_w', 'grad_a_w_mem_kv': 'grad_w', 'grad_a_w_out': 'grad_w', 'grad_a_w_ffn_in': 'grad_w', 'grad_a_w_ffn_out': 'grad_w', 'grad_kv_norm_g': 'grad_w', 'grad_w_kv_shared': 'grad_w', 'grad_b_pre_mix_g': 'grad_w', 'grad_b_post_mix_g': 'grad_w', 'grad_b_pre_ffn_g': 'grad_w', 'grad_b_post_ffn_g': 'grad_w', 'grad_b_w_in': 'grad_w', 'grad_b_w_mem_kv': 'grad_w', 'grad_b_w_out': 'grad_w', 'grad_b_w_ffn_in': 'grad_w', 'grad_b_w_ffn_out': 'grad_w', 'delta_mem_norm_g': 'delta_w', 'delta_a_pre_mix_g': 'delta_w', 'delta_a_post_mix_g': 'delta_w', 'delta_a_pre_ffn_g': 'delta_w', 'delta_a_post_ffn_g': 'delta_w', 'delta_a_w_in': 'delta_w', 'delta_a_conv_w': 'delta_w', 'delta_a_conv_b': 'delta_w', 'delta_a_gate_a_w': 'delta_w', 'delta_a_gate_a_b': 'delta_w', 'delta_a_gate_x_w': 'delta_w', 'delta_a_gate_x_b': 'delta_w', 'delta_a_lambda': 'delta_w', 'delta_a_w_mem_kv': 'delta_w', 'delta_a_w_out': 'delta_w', 'delta_a_w_ffn_in': 'delta_w', 'delta_a_w_ffn_out': 'delta_w', 'delta_kv_norm_g': 'delta_w', 'delta_w_kv_shared': 'delta_w', 'delta_b_pre_mix_g': 'delta_w', 'delta_b_post_mix_g': 'delta_w', 'delta_b_pre_ffn_g': 'delta_w', 'delta_b_post_ffn_g': 'delta_w', 'delta_b_w_in': 'delta_w', 'delta_b_w_mem_kv': 'delta_w', 'delta_b_w_out': 'delta_w', 'delta_b_w_ffn_in': 'delta_w', 'delta_b_w_ffn_out': 'delta_w', 'new_m_mem_norm_g': 'new_m', 'new_m_a_pre_mix_g': 'new_m', 'new_m_a_post_mix_g': 'new_m', 'new_m_a_pre_ffn_g': 'new_m', 'new_m_a_post_ffn_g': 'new_m', 'new_m_a_w_in': 'new_m', 'new_m_a_conv_w': 'new_m', 'new_m_a_conv_b': 'new_m', 'new_m_a_gate_a_w': 'new_m', 'new_m_a_gate_a_b': 'new_m', 'new_m_a_gate_x_w': 'new_m', 'new_m_a_gate_x_b': 'new_m', 'new_m_a_lambda': 'new_m', 'new_m_a_w_mem_kv': 'new_m', 'new_m_a_w_out': 'new_m', 'new_m_a_w_ffn_in': 'new_m', 'new_m_a_w_ffn_out': 'new_m', 'new_m_kv_norm_g': 'new_m', 'new_m_w_kv_shared': 'new_m', 'new_m_b_pre_mix_g': 'new_m', 'new_m_b_post_mix_g': 'new_m', 'new_m_b_pre_ffn_g': 'new_m', 'new_m_b_post_ffn_g': 'new_m', 'new_m_b_w_in': 'new_m', 'new_m_b_w_mem_kv': 'new_m', 'new_m_b_w_out': 'new_m', 'new_m_b_w_ffn_in': 'new_m', 'new_m_b_w_ffn_out': 'new_m', 'new_v_mem_norm_g': 'new_v', 'new_v_a_pre_mix_g': 'new_v', 'new_v_a_post_mix_g': 'new_v', 'new_v_a_pre_ffn_g': 'new_v', 'new_v_a_post_ffn_g': 'new_v', 'new_v_a_w_in': 'new_v', 'new_v_a_conv_w': 'new_v', 'new_v_a_conv_b': 'new_v', 'new_v_a_gate_a_w': 'new_v', 'new_v_a_gate_a_b': 'new_v', 'new_v_a_gate_x_w': 'new_v', 'new_v_a_gate_x_b': 'new_v', 'new_v_a_lambda': 'new_v', 'new_v_a_w_mem_kv': 'new_v', 'new_v_a_w_out': 'new_v', 'new_v_a_w_ffn_in': 'new_v', 'new_v_a_w_ffn_out': 'new_v', 'new_v_kv_norm_g': 'new_v', 'new_v_w_kv_shared': 'new_v', 'new_v_b_pre_mix_g': 'new_v', 'new_v_b_post_mix_g': 'new_v', 'new_v_b_pre_ffn_g': 'new_v', 'new_v_b_post_ffn_g': 'new_v', 'new_v_b_w_in': 'new_v', 'new_v_b_w_mem_kv': 'new_v', 'new_v_b_w_out': 'new_v', 'new_v_b_w_ffn_in': 'new_v', 'new_v_b_w_ffn_out': 'new_v'}


def _forward(args):
    return _fwd_reference(*[args[k] for k in FWD_PARAMS])


def _output_shape():
    def fwd():
        inp = _fwd_setup_inputs(0)
        return _fwd_reference(*[inp[k] for k in FWD_PARAMS])
    out = _jax.eval_shape(fwd)
    return out.shape, out.dtype

N_MICROBATCH = 1
ADAM_LR = 0.001
ADAM_B1 = 0.9
ADAM_B2 = 0.999
ADAM_EPS = 1e-08
ADAM_WD = 0.01
ADAM_STEP = 10
PER_EXAMPLE_BATCH_AXIS = {'x': 0, 'mem': 0, 'loss_target': 0}
SHARED_INPUTS = []
_WEIGHT_DTYPES = {'mem_norm_g': _jnp.float32, 'a_pre_mix_g': _jnp.float32, 'a_post_mix_g': _jnp.float32, 'a_pre_ffn_g': _jnp.float32, 'a_post_ffn_g': _jnp.float32, 'a_w_in': _jnp.float32, 'a_conv_w': _jnp.float32, 'a_conv_b': _jnp.float32, 'a_gate_a_w': _jnp.float32, 'a_gate_a_b': _jnp.float32, 'a_gate_x_w': _jnp.float32, 'a_gate_x_b': _jnp.float32, 'a_lambda': _jnp.float32, 'a_w_mem_kv': _jnp.float32, 'a_w_out': _jnp.float32, 'a_w_ffn_in': _jnp.float32, 'a_w_ffn_out': _jnp.float32, 'kv_norm_g': _jnp.float32, 'w_kv_shared': _jnp.float32, 'b_pre_mix_g': _jnp.float32, 'b_post_mix_g': _jnp.float32, 'b_pre_ffn_g': _jnp.float32, 'b_post_ffn_g': _jnp.float32, 'b_w_in': _jnp.float32, 'b_w_mem_kv': _jnp.float32, 'b_w_out': _jnp.float32, 'b_w_ffn_in': _jnp.float32, 'b_w_ffn_out': _jnp.float32}
MOMENT_SCALE = {'mem_norm_g': 2.048595e+00, 'a_pre_mix_g': 3.503624e+00, 'a_post_mix_g': 3.309338e+01, 'a_pre_ffn_g': 1.923423e+00, 'a_post_ffn_g': 3.175415e+01, 'a_w_in': 2.574791e+00, 'a_conv_w': 5.778812e+00, 'a_conv_b': 8.042269e+01, 'a_gate_a_w': 2.590491e+00, 'a_gate_a_b': 1.871574e+00, 'a_gate_x_w': 4.664878e+00, 'a_gate_x_b': 2.128073e+00, 'a_lambda': 3.517820e+00, 'a_w_mem_kv': 1.059250e+00, 'a_w_out': 6.892460e+00, 'a_w_ffn_in': 7.967604e-01, 'a_w_ffn_out': 1.593984e+00, 'kv_norm_g': 6.372936e+00, 'w_kv_shared': 5.043925e+00, 'b_pre_mix_g': 4.523598e-01, 'b_post_mix_g': 3.242892e+01, 'b_pre_ffn_g': 1.398686e+00, 'b_post_ffn_g': 3.214665e+01, 'b_w_in': 4.478417e-01, 'b_w_mem_kv': 1.949964e+00, 'b_w_out': 4.517482e+00, 'b_w_ffn_in': 5.680268e-01, 'b_w_ffn_out': 1.234389e+00}


def _to_microbatches(a, axis):
    t = _jnp.moveaxis(a, axis, 0)
    t = t.reshape((N_MICROBATCH, t.shape[0] // N_MICROBATCH) + t.shape[1:])
    return _jnp.moveaxis(t, 1, axis + 1)


def setup_inputs(seed: int = 0) -> dict:
    inp = _fwd_setup_inputs(seed)
    key = _jax.random.fold_in(_jax.random.key(seed), 7919)
    shape, _ = _output_shape()
    out = dict(inp)
    out["loss_target"] = _jax.random.normal(_jax.random.fold_in(key, 0), shape, _jnp.float32)
    for i, name in enumerate(TWIN_WEIGHTS):
        w = inp[name].astype(_jnp.float32)
        if MOMENT_SCALE is None:
            s = _jnp.sqrt(_jnp.mean(_jnp.square(w)) + 1e-30)
        else:
            s = MOMENT_SCALE[name]
        km, kv = _jax.random.split(_jax.random.fold_in(key, i + 1))
        out[name] = w
        out["m_" + name] = s * _jax.random.normal(km, w.shape, _jnp.float32)
        out["v_" + name] = (s * s) * _jax.random.uniform(kv, w.shape, _jnp.float32, 0.5, 1.5)
    if N_MICROBATCH > 1:
        for name, axis in PER_EXAMPLE_BATCH_AXIS.items():
            out[name] = _to_microbatches(out[name], axis)
    return {'x': out['x'], 'mem': out['mem'], 'mem_norm_g': out['mem_norm_g'], 'a_pre_mix_g': out['a_pre_mix_g'], 'a_post_mix_g': out['a_post_mix_g'], 'a_pre_ffn_g': out['a_pre_ffn_g'], 'a_post_ffn_g': out['a_post_ffn_g'], 'a_w_in': out['a_w_in'], 'a_conv_w': out['a_conv_w'], 'a_conv_b': out['a_conv_b'], 'a_gate_a_w': out['a_gate_a_w'], 'a_gate_a_b': out['a_gate_a_b'], 'a_gate_x_w': out['a_gate_x_w'], 'a_gate_x_b': out['a_gate_x_b'], 'a_lambda': out['a_lambda'], 'a_w_mem_kv': out['a_w_mem_kv'], 'a_w_out': out['a_w_out'], 'a_w_ffn_in': out['a_w_ffn_in'], 'a_w_ffn_out': out['a_w_ffn_out'], 'kv_norm_g': out['kv_norm_g'], 'w_kv_shared': out['w_kv_shared'], 'b_pre_mix_g': out['b_pre_mix_g'], 'b_post_mix_g': out['b_post_mix_g'], 'b_pre_ffn_g': out['b_pre_ffn_g'], 'b_post_ffn_g': out['b_post_ffn_g'], 'b_w_in': out['b_w_in'], 'b_w_mem_kv': out['b_w_mem_kv'], 'b_w_out': out['b_w_out'], 'b_w_ffn_in': out['b_w_ffn_in'], 'b_w_ffn_out': out['b_w_ffn_out'], 'loss_target': out['loss_target'], 'm_mem_norm_g': out['m_mem_norm_g'], 'm_a_pre_mix_g': out['m_a_pre_mix_g'], 'm_a_post_mix_g': out['m_a_post_mix_g'], 'm_a_pre_ffn_g': out['m_a_pre_ffn_g'], 'm_a_post_ffn_g': out['m_a_post_ffn_g'], 'm_a_w_in': out['m_a_w_in'], 'm_a_conv_w': out['m_a_conv_w'], 'm_a_conv_b': out['m_a_conv_b'], 'm_a_gate_a_w': out['m_a_gate_a_w'], 'm_a_gate_a_b': out['m_a_gate_a_b'], 'm_a_gate_x_w': out['m_a_gate_x_w'], 'm_a_gate_x_b': out['m_a_gate_x_b'], 'm_a_lambda': out['m_a_lambda'], 'm_a_w_mem_kv': out['m_a_w_mem_kv'], 'm_a_w_out': out['m_a_w_out'], 'm_a_w_ffn_in': out['m_a_w_ffn_in'], 'm_a_w_ffn_out': out['m_a_w_ffn_out'], 'm_kv_norm_g': out['m_kv_norm_g'], 'm_w_kv_shared': out['m_w_kv_shared'], 'm_b_pre_mix_g': out['m_b_pre_mix_g'], 'm_b_post_mix_g': out['m_b_post_mix_g'], 'm_b_pre_ffn_g': out['m_b_pre_ffn_g'], 'm_b_post_ffn_g': out['m_b_post_ffn_g'], 'm_b_w_in': out['m_b_w_in'], 'm_b_w_mem_kv': out['m_b_w_mem_kv'], 'm_b_w_out': out['m_b_w_out'], 'm_b_w_ffn_in': out['m_b_w_ffn_in'], 'm_b_w_ffn_out': out['m_b_w_ffn_out'], 'v_mem_norm_g': out['v_mem_norm_g'], 'v_a_pre_mix_g': out['v_a_pre_mix_g'], 'v_a_post_mix_g': out['v_a_post_mix_g'], 'v_a_pre_ffn_g': out['v_a_pre_ffn_g'], 'v_a_post_ffn_g': out['v_a_post_ffn_g'], 'v_a_w_in': out['v_a_w_in'], 'v_a_conv_w': out['v_a_conv_w'], 'v_a_conv_b': out['v_a_conv_b'], 'v_a_gate_a_w': out['v_a_gate_a_w'], 'v_a_gate_a_b': out['v_a_gate_a_b'], 'v_a_gate_x_w': out['v_a_gate_x_w'], 'v_a_gate_x_b': out['v_a_gate_x_b'], 'v_a_lambda': out['v_a_lambda'], 'v_a_w_mem_kv': out['v_a_w_mem_kv'], 'v_a_w_out': out['v_a_w_out'], 'v_a_w_ffn_in': out['v_a_w_ffn_in'], 'v_a_w_ffn_out': out['v_a_w_ffn_out'], 'v_kv_norm_g': out['v_kv_norm_g'], 'v_w_kv_shared': out['v_w_kv_shared'], 'v_b_pre_mix_g': out['v_b_pre_mix_g'], 'v_b_post_mix_g': out['v_b_post_mix_g'], 'v_b_pre_ffn_g': out['v_b_pre_ffn_g'], 'v_b_post_ffn_g': out['v_b_post_ffn_g'], 'v_b_w_in': out['v_b_w_in'], 'v_b_w_mem_kv': out['v_b_w_mem_kv'], 'v_b_w_out': out['v_b_w_out'], 'v_b_w_ffn_in': out['v_b_w_ffn_in'], 'v_b_w_ffn_out': out['v_b_w_ffn_out']}


def _loss(weights, diff, rest, loss_target):
    with _jax.named_scope("forward"):
        args = {**rest, TWIN_DIFF_INPUT: diff, **{k: w.astype(_WEIGHT_DTYPES[k]) for k, w in weights.items()}}
        y = _forward(args)
    with _jax.named_scope("loss_head"):
        err = _jnp.square(y.astype(_jnp.float32) - loss_target)
        return 0.5 * _jnp.sum(_jnp.mean(err, axis=-1)) if err.ndim else 0.5 * err


def _adamw(w, g, m, v):
    m = ADAM_B1 * m + (1.0 - ADAM_B1) * g
    v = ADAM_B2 * v + (1.0 - ADAM_B2) * _jnp.square(g)
    m_hat = m / (1.0 - ADAM_B1 ** ADAM_STEP)
    v_hat = v / (1.0 - ADAM_B2 ** ADAM_STEP)
    delta = -ADAM_LR * (m_hat / (_jnp.sqrt(v_hat) + ADAM_EPS) + ADAM_WD * w)
    return delta, m, v


def reference(x, mem, mem_norm_g, a_pre_mix_g, a_post_mix_g, a_pre_ffn_g, a_post_ffn_g, a_w_in, a_conv_w, a_conv_b, a_gate_a_w, a_gate_a_b, a_gate_x_w, a_gate_x_b, a_lambda, a_w_mem_kv, a_w_out, a_w_ffn_in, a_w_ffn_out, kv_norm_g, w_kv_shared, b_pre_mix_g, b_post_mix_g, b_pre_ffn_g, b_post_ffn_g, b_w_in, b_w_mem_kv, b_w_out, b_w_ffn_in, b_w_ffn_out, loss_target, m_mem_norm_g, m_a_pre_mix_g, m_a_post_mix_g, m_a_pre_ffn_g, m_a_post_ffn_g, m_a_w_in, m_a_conv_w, m_a_conv_b, m_a_gate_a_w, m_a_gate_a_b, m_a_gate_x_w, m_a_gate_x_b, m_a_lambda, m_a_w_mem_kv, m_a_w_out, m_a_w_ffn_in, m_a_w_ffn_out, m_kv_norm_g, m_w_kv_shared, m_b_pre_mix_g, m_b_post_mix_g, m_b_pre_ffn_g, m_b_post_ffn_g, m_b_w_in, m_b_w_mem_kv, m_b_w_out, m_b_w_ffn_in, m_b_w_ffn_out, v_mem_norm_g, v_a_pre_mix_g, v_a_post_mix_g, v_a_pre_ffn_g, v_a_post_ffn_g, v_a_w_in, v_a_conv_w, v_a_conv_b, v_a_gate_a_w, v_a_gate_a_b, v_a_gate_x_w, v_a_gate_x_b, v_a_lambda, v_a_w_mem_kv, v_a_w_out, v_a_w_ffn_in, v_a_w_ffn_out, v_kv_norm_g, v_w_kv_shared, v_b_pre_mix_g, v_b_post_mix_g, v_b_pre_ffn_g, v_b_post_ffn_g, v_b_w_in, v_b_w_mem_kv, v_b_w_out, v_b_w_ffn_in, v_b_w_ffn_out):
    given = dict(x=x, mem=mem, mem_norm_g=mem_norm_g, a_pre_mix_g=a_pre_mix_g, a_post_mix_g=a_post_mix_g, a_pre_ffn_g=a_pre_ffn_g, a_post_ffn_g=a_post_ffn_g, a_w_in=a_w_in, a_conv_w=a_conv_w, a_conv_b=a_conv_b, a_gate_a_w=a_gate_a_w, a_gate_a_b=a_gate_a_b, a_gate_x_w=a_gate_x_w, a_gate_x_b=a_gate_x_b, a_lambda=a_lambda, a_w_mem_kv=a_w_mem_kv, a_w_out=a_w_out, a_w_ffn_in=a_w_ffn_in, a_w_ffn_out=a_w_ffn_out, kv_norm_g=kv_norm_g, w_kv_shared=w_kv_shared, b_pre_mix_g=b_pre_mix_g, b_post_mix_g=b_post_mix_g, b_pre_ffn_g=b_pre_ffn_g, b_post_ffn_g=b_post_ffn_g, b_w_in=b_w_in, b_w_mem_kv=b_w_mem_kv, b_w_out=b_w_out, b_w_ffn_in=b_w_ffn_in, b_w_ffn_out=b_w_ffn_out, loss_target=loss_target, m_mem_norm_g=m_mem_norm_g, m_a_pre_mix_g=m_a_pre_mix_g, m_a_post_mix_g=m_a_post_mix_g, m_a_pre_ffn_g=m_a_pre_ffn_g, m_a_post_ffn_g=m_a_post_ffn_g, m_a_w_in=m_a_w_in, m_a_conv_w=m_a_conv_w, m_a_conv_b=m_a_conv_b, m_a_gate_a_w=m_a_gate_a_w, m_a_gate_a_b=m_a_gate_a_b, m_a_gate_x_w=m_a_gate_x_w, m_a_gate_x_b=m_a_gate_x_b, m_a_lambda=m_a_lambda, m_a_w_mem_kv=m_a_w_mem_kv, m_a_w_out=m_a_w_out, m_a_w_ffn_in=m_a_w_ffn_in, m_a_w_ffn_out=m_a_w_ffn_out, m_kv_norm_g=m_kv_norm_g, m_w_kv_shared=m_w_kv_shared, m_b_pre_mix_g=m_b_pre_mix_g, m_b_post_mix_g=m_b_post_mix_g, m_b_pre_ffn_g=m_b_pre_ffn_g, m_b_post_ffn_g=m_b_post_ffn_g, m_b_w_in=m_b_w_in, m_b_w_mem_kv=m_b_w_mem_kv, m_b_w_out=m_b_w_out, m_b_w_ffn_in=m_b_w_ffn_in, m_b_w_ffn_out=m_b_w_ffn_out, v_mem_norm_g=v_mem_norm_g, v_a_pre_mix_g=v_a_pre_mix_g, v_a_post_mix_g=v_a_post_mix_g, v_a_pre_ffn_g=v_a_pre_ffn_g, v_a_post_ffn_g=v_a_post_ffn_g, v_a_w_in=v_a_w_in, v_a_conv_w=v_a_conv_w, v_a_conv_b=v_a_conv_b, v_a_gate_a_w=v_a_gate_a_w, v_a_gate_a_b=v_a_gate_a_b, v_a_gate_x_w=v_a_gate_x_w, v_a_gate_x_b=v_a_gate_x_b, v_a_lambda=v_a_lambda, v_a_w_mem_kv=v_a_w_mem_kv, v_a_w_out=v_a_w_out, v_a_w_ffn_in=v_a_w_ffn_in, v_a_w_ffn_out=v_a_w_ffn_out, v_kv_norm_g=v_kv_norm_g, v_w_kv_shared=v_w_kv_shared, v_b_pre_mix_g=v_b_pre_mix_g, v_b_post_mix_g=v_b_post_mix_g, v_b_pre_ffn_g=v_b_pre_ffn_g, v_b_post_ffn_g=v_b_post_ffn_g, v_b_w_in=v_b_w_in, v_b_w_mem_kv=v_b_w_mem_kv, v_b_w_out=v_b_w_out, v_b_w_ffn_in=v_b_w_ffn_in, v_b_w_ffn_out=v_b_w_ffn_out)
    weights = {n: given[n] for n in TWIN_WEIGHTS}
    shared = {n: given[n] for n in SHARED_INPUTS}
    per_example = {n: given[n] for n in ['x', 'mem']}
    grad_fn = _jax.value_and_grad(_loss, argnums=(0, 1))

    def one_microbatch(ex, loss_target):
        ex = dict(ex)
        diff = ex.pop(TWIN_DIFF_INPUT)
        return grad_fn(weights, diff, {**shared, **ex}, loss_target)

    if N_MICROBATCH == 1:
        loss, (grad_w, grad_x) = one_microbatch(per_example, given["loss_target"])
    else:
        def body(carry, xs):
            loss_sum, grad_sum = carry
            l_k, (gw_k, gx_k) = one_microbatch(xs[0], xs[1])
            with _jax.named_scope("update"):
                return (loss_sum + l_k, _jax.tree.map(_jnp.add, grad_sum, gw_k)), gx_k

        init = (_jnp.zeros((), _jnp.float32), _jax.tree.map(_jnp.zeros_like, weights))
        (loss, grad_w), grad_x = _jax.lax.scan(body, init, (per_example, given["loss_target"]))
    with _jax.named_scope("update"):
        delta_w, new_m, new_v = {}, {}, {}
        for n in TWIN_WEIGHTS:
            delta_w[n], new_m[n], new_v[n] = _adamw(weights[n], grad_w[n], given["m_" + n], given["v_" + n])
    return (loss, grad_x, *[grad_w[n] for n in TWIN_WEIGHTS], *[delta_w[n] for n in TWIN_WEIGHTS],
            *[new_m[n] for n in TWIN_WEIGHTS], *[new_v[n] for n in TWIN_WEIGHTS])
```

```python
import functools
import math

import jax
import jax.numpy as jnp
from jax import lax
from jax.experimental import pallas as pl
from jax.experimental.pallas import tpu as pltpu

D = 2048
HD = 128
MEM_W = 512
MEM_HEADS = 4
MIX_W = D - MEM_W
N_BLK = MIX_W // HD
D_FF = 5632
N_MEM = 256
RMS_EPS = 1e-6
NEG_INF = -1e30
LRU_C = 8.0
DIL_GROUPS = ((128, 1), (512, 4), (2048, 16))
Q_BLOCK = 128
SCALE = HD ** -0.5
N_CHIPS = 4

ADAM_LR = 0.001
ADAM_B1 = 0.9
ADAM_B2 = 0.999
ADAM_EPS = 1e-08
ADAM_WD = 0.01
ADAM_STEP = 10

MXU = jnp.bfloat16
F32 = jnp.float32
VMEM_LIMIT_BYTES = 56 * 1024 * 1024

BS = pl.BlockSpec
SDS = jax.ShapeDtypeStruct
MESH = pl.DeviceIdType.MESH


def _cp(*sem):
    return pltpu.CompilerParams(dimension_semantics=sem or None, vmem_limit_bytes=VMEM_LIMIT_BYTES)


def _dot(a, b, dn=((1,), (0,))):
    return lax.dot_general(a, b, (dn, ((), ())), preferred_element_type=F32)


def _div(i, n):
    return lax.div(i, jnp.int32(n))


def _rem(i, n):
    return lax.rem(i, jnp.int32(n))


NN = ((1,), (0,))
NT = ((1,), (1,))
TN = ((0,), (0,))


def _sigmoid(z):
    return 1.0 / (1.0 + jnp.exp(-z))


def _log1p_pos(u):
    return jnp.where(u < 1e-2, u * (1.0 - u * (0.5 - u * (1.0 / 3.0))), jnp.log(1.0 + u))


def _neg_expm1(z):
    return jnp.where(z > -1e-2, -z * (1.0 + z * (0.5 + z * (1.0 / 6.0))), 1.0 - jnp.exp(z))


def _softplus(z):
    return jnp.maximum(z, 0.0) + _log1p_pos(jnp.exp(-jnp.abs(z)))


_GELU_C = math.sqrt(2.0 / math.pi)


def _gelu_and_grad(x):
    x2 = x * x
    t = jnp.tanh(_GELU_C * (x + 0.044715 * x * x2))
    g = 0.5 * x * (1.0 + t)
    dg = 0.5 * (1.0 + t) + 0.5 * x * (1.0 - t * t) * _GELU_C * (1.0 + 3.0 * 0.044715 * x2)
    return g, dg


def _row_tile(rows):
    return min(256, rows)


def norm_cast(x, g, name):
    rows = x.shape[0]
    tr = _row_tile(rows)

    def body(x_ref, g_ref, o_ref):
        xv = x_ref[...]
        r = lax.rsqrt(jnp.mean(xv * xv, axis=-1, keepdims=True) + RMS_EPS)
        o_ref[...] = (xv * r * g_ref[...]).astype(o_ref.dtype)

    return pl.pallas_call(
        body, grid=(rows // tr,),
        in_specs=[BS((tr, D), lambda i: (i, 0)), BS((1, D), lambda i: (0, 0))],
        out_specs=BS((tr, D), lambda i: (i, 0)),
        out_shape=SDS((rows, D), MXU), compiler_params=_cp("parallel"), name=name,
    )(x, g.reshape(1, D))


def resid_norm(h, y, g, name):
    rows = h.shape[0]
    tr = _row_tile(rows)

    def body(h_ref, y_ref, g_ref, o_ref):
        yv = y_ref[...]
        r = lax.rsqrt(jnp.mean(yv * yv, axis=-1, keepdims=True) + RMS_EPS)
        o_ref[...] = h_ref[...] + yv * r * g_ref[...]

    return pl.pallas_call(
        body, grid=(rows // tr,),
        in_specs=[BS((tr, D), lambda i: (i, 0)), BS((tr, D), lambda i: (i, 0)), BS((1, D), lambda i: (0, 0))],
        out_specs=BS((tr, D), lambda i: (i, 0)),
        out_shape=SDS((rows, D), F32), compiler_params=_cp("parallel"), name=name,
    )(h, y, g.reshape(1, D))


def norm_bwd(x, g, dy, res, out_dtype, name):
    rows = x.shape[0]
    tr = _row_tile(rows)
    has_res = res is not None

    def body(*refs):
        if has_res:
            x_ref, g_ref, dy_ref, res_ref, dx_ref, dg_ref = refs
        else:
            x_ref, g_ref, dy_ref, dx_ref, dg_ref = refs
        xv = x_ref[...]
        dyv = dy_ref[...].astype(F32)
        r = lax.rsqrt(jnp.mean(xv * xv, axis=-1, keepdims=True) + RMS_EPS)
        xhat = xv * r
        dxhat = dyv * g_ref[...]
        dx = r * (dxhat - xhat * jnp.mean(dxhat * xhat, axis=-1, keepdims=True))
        if has_res:
            dx = dx + res_ref[...]
        dx_ref[...] = dx.astype(dx_ref.dtype)

        @pl.when(pl.program_id(0) == 0)
        def _():
            dg_ref[...] = jnp.zeros_like(dg_ref)

        dg_ref[...] += jnp.sum(dyv * xhat, axis=0, keepdims=True)

    row = BS((tr, D), lambda i: (i, 0))
    vec = BS((1, D), lambda i: (0, 0))
    ins = [x, g.reshape(1, D), dy] + ([res] if has_res else [])
    dx, dg = pl.pallas_call(
        body, grid=(rows // tr,),
        in_specs=[row, vec, row] + ([row] if has_res else []),
        out_specs=[row, vec],
        out_shape=[SDS((rows, D), out_dtype), SDS((1, D), F32)],
        compiler_params=_cp("arbitrary"), name=name,
    )(*ins)
    return dx, dg.reshape(D)


def loss_head(y, target, name):
    rows = y.shape[0]
    tr = _row_tile(rows)

    def body(y_ref, t_ref, dy_ref, acc_ref):
        err = y_ref[...] - t_ref[...]
        dy_ref[...] = err * (1.0 / D)

        @pl.when(pl.program_id(0) == 0)
        def _():
            acc_ref[...] = jnp.zeros_like(acc_ref)

        acc_ref[...] += jnp.sum(err * err, axis=0, keepdims=True)

    row = BS((tr, D), lambda i: (i, 0))
    dy, acc = pl.pallas_call(
        body, grid=(rows // tr,), in_specs=[row, row],
        out_specs=[row, BS((1, D), lambda i: (0, 0))],
        out_shape=[SDS((rows, D), F32), SDS((1, D), F32)],
        compiler_params=_cp("arbitrary"), name=name,
    )(y, target)
    return acc, dy


def _mm_call(ins, in_specs, pick, dn, grid, o_spec, out_sds, name):
    gk = grid[2]
    n_in = len(ins)

    def body(*refs):
        o_ref = refs[n_in]
        k = pl.program_id(2)

        def step(a_ref, b_ref):
            p = _dot(a_ref[...], b_ref[...], dn)
            if gk == 1:
                o_ref[...] = p.astype(o_ref.dtype)
            else:
                acc = o_ref if out_sds.dtype == F32 else refs[n_in + 1]

                @pl.when(k == 0)
                def _():
                    acc[...] = p

                @pl.when(k > 0)
                def _():
                    acc[...] += p

                if acc is not o_ref:
                    @pl.when(k == gk - 1)
                    def _():
                        o_ref[...] = acc[...].astype(o_ref.dtype)

        pick(refs[:n_in], k, step)

    scratch = []
    if gk > 1 and out_sds.dtype != F32:
        scratch = [pltpu.VMEM(o_spec.block_shape[-2:], F32)]
    return pl.pallas_call(
        body, grid=grid, in_specs=in_specs, out_specs=o_spec, out_shape=out_sds,
        scratch_shapes=scratch, compiler_params=_cp("parallel", "parallel", "arbitrary"), name=name,
    )(*ins)


def _pick2(refs, k, step):
    step(refs[0], refs[1])


def mm_nn(a, w, *, tm, tn, tk, out_dtype, name):
    m, kdim = a.shape
    if w.ndim == 3:
        c = w.shape[2]
        n = N_CHIPS * c
        per = c // tn
        b_spec = BS((None, tk, tn), lambda i, j, k: (_div(j, per), k, _rem(j, per)))
    else:
        n = w.shape[1]
        b_spec = BS((tk, tn), lambda i, j, k: (k, j))
    grid = (m // tm, n // tn, kdim // tk)
    return _mm_call([a, w], [BS((tm, tk), lambda i, j, k: (i, k)), b_spec], _pick2, NN, grid,
                    BS((tm, tn), lambda i, j, k: (i, j)), SDS((m, n), out_dtype), name)


def mm_nt(a_list, w, *, tm, tn, tk, out_dtype, name):
    m = a_list[0].shape[0]
    ka = a_list[0].shape[1]
    n_a = len(a_list)
    kdim = ka * n_a
    if w.ndim == 3:
        c = w.shape[2]
        n = w.shape[1]
        per = c // tk
        b_spec = BS((None, tn, tk), lambda i, j, k: (_div(k, per), j, _rem(k, per)))
    else:
        n = w.shape[0]
        b_spec = BS((tn, tk), lambda i, j, k: (j, k))
    gk = kdim // tk
    half = gk // n_a
    grid = (m // tm, n // tn, gk)
    if n_a == 1:
        a_specs = [BS((tm, tk), lambda i, j, k: (i, k))]
        pick = lambda refs, k, step: step(refs[0], refs[1])
    else:
        a_specs = [BS((tm, tk), lambda i, j, k: (i, jnp.minimum(k, half - 1))),
                   BS((tm, tk), lambda i, j, k: (i, jnp.maximum(k - half, 0)))]

        def pick(refs, k, step):
            @pl.when(k < half)
            def _():
                step(refs[0], refs[2])

            @pl.when(k >= half)
            def _():
                step(refs[1], refs[2])

    return _mm_call(list(a_list) + [w], a_specs + [b_spec], pick, NT, grid,
                    BS((tm, tn), lambda i, j, k: (i, j)), SDS((m, n), out_dtype), name)


def mm_tn(a, b_list, *, t1, tn, ts, col_shards, name):
    s, k1 = a.shape
    nb = b_list[0].shape[1]
    n_b = len(b_list)
    n = nb * n_b
    gn = n // tn
    half = gn // n_b
    grid = (k1 // t1, gn, s // ts)
    if col_shards:
        c = n // N_CHIPS
        per = c // tn
        o_spec = BS((None, t1, tn), lambda i, j, k: (_div(j, per), i, _rem(j, per)))
        out_sds = SDS((N_CHIPS, k1, c), F32)
    else:
        o_spec = BS((t1, tn), lambda i, j, k: (i, j))
        out_sds = SDS((k1, n), F32)
    a_spec = BS((ts, t1), lambda i, j, k: (k, i))
    if n_b == 1:
        b_specs = [BS((ts, tn), lambda i, j, k: (k, j))]
        pick = lambda refs, k, step: step(refs[0], refs[1])
    else:
        b_specs = [BS((ts, tn), lambda i, j, k: (k, jnp.minimum(j, half - 1))),
                   BS((ts, tn), lambda i, j, k: (k, jnp.maximum(j - half, 0)))]

        def pick(refs, k, step):
            j = pl.program_id(1)

            @pl.when(j < half)
            def _():
                step(refs[0], refs[1])

            @pl.when(j >= half)
            def _():
                step(refs[0], refs[2])

    return _mm_call([a] + list(b_list), [a_spec] + b_specs, pick, TN, grid, o_spec, out_sds, name)


def ffn_in_fwd(hn, w, name):
    s = hn.shape[0]
    tm = min(512, s)
    tn = D_FF // 4

    def body(a_ref, wg_ref, wu_ref, g_ref, u_ref, act_ref):
        a = a_ref[...]
        g = _dot(a, wg_ref[...])
        u = _dot(a, wu_ref[...])
        g_ref[...] = g
        u_ref[...] = u
        act_ref[...] = (g * _sigmoid(g) * u).astype(act_ref.dtype)

    tile = BS((tm, tn), lambda j, i: (i, j))
    return pl.pallas_call(
        body, grid=(4, s // tm),
        in_specs=[BS((tm, D), lambda j, i: (i, 0)),
                  BS((None, D, tn), lambda j, i: (_div(j, 2), 0, _rem(j, 2))),
                  BS((None, D, tn), lambda j, i: (2 + _div(j, 2), 0, _rem(j, 2)))],
        out_specs=[tile, tile, tile],
        out_shape=[SDS((s, D_FF), F32), SDS((s, D_FF), F32), SDS((s, D_FF), MXU)],
        compiler_params=_cp("parallel", "parallel"), name=name,
    )(hn, w, w)


def ffn_act_bwd(dy, w_out, g, u, name):
    s = dy.shape[0]
    tm = min(512, s)
    tn = D_FF // 4

    def body(dy_ref, w_ref, g_ref, u_ref, dg_ref, du_ref):
        dact = _dot(dy_ref[...], w_ref[...], NT)
        gv = g_ref[...]
        sg = _sigmoid(gv)
        dg_ref[...] = (dact * u_ref[...] * sg * (1.0 + gv * (1.0 - sg))).astype(dg_ref.dtype)
        du_ref[...] = (dact * gv * sg).astype(du_ref.dtype)

    tile = BS((tm, tn), lambda j, i: (i, j))
    return pl.pallas_call(
        body, grid=(4, s // tm),
        in_specs=[BS((tm, D), lambda j, i: (i, 0)), BS((tn, D), lambda j, i: (j, 0)), tile, tile],
        out_specs=[tile, tile],
        out_shape=[SDS((s, D_FF), MXU), SDS((s, D_FF), MXU)],
        compiler_params=_cp("parallel", "parallel"), name=name,
    )(dy, w_out, g, u)


LRU_T = 256
HALO = 8


def _shift_down(x, k, fill):
    rows = x.shape[0]
    idx = lax.broadcasted_iota(jnp.int32, x.shape, 0)
    return jnp.where(idx < k, fill, pltpu.roll(x, k, 0))


def _shift_up(x, k, fill):
    rows = x.shape[0]
    idx = lax.broadcasted_iota(jnp.int32, x.shape, 0)
    return jnp.where(idx >= rows - k, fill, pltpu.roll(x, rows - k, 0))


def _conv_taps(xcat):
    rows = xcat.shape[0]
    taps = []
    for k in range(4):
        off = HALO - 3 + k
        taps.append(xcat[off:off + LRU_T] if off == HALO else pltpu.roll(xcat, rows - off, 0)[:LRU_T])
    return taps


def _gates(xc, wa_ref, ba, wx_ref, bx, lam, za_ref, zx_ref):
    xm = xc.astype(MXU)
    for n in range(N_BLK):
        sl = slice(n * HD, (n + 1) * HD)
        za_ref[:, sl] = _dot(xm[:, sl], wa_ref[n])
        zx_ref[:, sl] = _dot(xm[:, sl], wx_ref[n])
    ra = _sigmoid(za_ref[...] + ba)
    ii = _sigmoid(zx_ref[...] + bx)
    sp = _softplus(-lam)
    log_a = -LRU_C * ra * sp
    a = jnp.exp(log_a)
    mult = jnp.sqrt(_neg_expm1(2.0 * log_a))
    return ra, ii, sp, a, mult


def lru_fwd(proj, conv_w, conv_b, wa, ba, wx, bx, lam, name):
    s = proj.shape[0]
    c = MIX_W
    nblk = s // LRU_T
    hpb = LRU_T // HALO

    def body(x_ref, halo_ref, cw_ref, cb_ref, wa_ref, ba_ref, wx_ref, bx_ref, lam_ref,
             xc_ref, h_ref, carry, za_ref, zx_ref):
        i = pl.program_id(0)

        @pl.when(i == 0)
        def _():
            carry[...] = jnp.zeros_like(carry)

        halo = jnp.where(i == 0, 0.0, halo_ref[...])
        xcat = jnp.concatenate([halo, x_ref[...]], axis=0)
        taps = _conv_taps(xcat)
        xc = cb_ref[...] + sum(cw_ref[k:k + 1, :] * taps[k] for k in range(4))
        xc_ref[...] = xc
        _, ii, _, a, mult = _gates(xc, wa_ref, ba_ref[...], wx_ref, bx_ref[...], lam_ref[...], za_ref, zx_ref)
        b = mult * (ii * xc)
        sh = 1
        while sh < LRU_T:
            b = a * _shift_down(b, sh, 0.0) + b
            a = a * _shift_down(a, sh, 1.0)
            sh *= 2
        h = b + a * carry[HALO - 1:HALO, :]
        h_ref[...] = h
        carry[...] = h[LRU_T - HALO:, :]

    def full(shape):
        return BS(shape, lambda i: (0,) * len(shape))

    blk = BS((LRU_T, c), lambda i: (i, 0))
    return pl.pallas_call(
        body, grid=(nblk,),
        in_specs=[blk, BS((HALO, c), lambda i: (jnp.maximum(i * hpb - 1, 0), 0)),
                  full((4, c)), full((1, c)), full((N_BLK, HD, HD)), full((1, c)),
                  full((N_BLK, HD, HD)), full((1, c)), full((1, c))],
        out_specs=[blk, blk],
        out_shape=[SDS((s, c), F32), SDS((s, c), F32)],
        scratch_shapes=[pltpu.VMEM((HALO, c), F32), pltpu.VMEM((LRU_T, c), F32), pltpu.VMEM((LRU_T, c), F32)],
        compiler_params=_cp("arbitrary"), name=name,
    )(proj, proj, conv_w, conv_b.reshape(1, c), wa.astype(MXU), ba.reshape(1, c), wx.astype(MXU),
      bx.reshape(1, c), lam.reshape(1, c))


def lru_mix_prep(h, proj, m, name):
    s = h.shape[0]
    tr = _row_tile(s)

    def body(h_ref, gb_ref, m_ref, o_ref):
        ge, _ = _gelu_and_grad(gb_ref[...])
        o_ref[:, :MIX_W] = (h_ref[...] * ge).astype(o_ref.dtype)
        o_ref[:, MIX_W:] = m_ref[...]

    return pl.pallas_call(
        body, grid=(s // tr,),
        in_specs=[BS((tr, MIX_W), lambda i: (i, 0)), BS((tr, MIX_W), lambda i: (i, 1)),
                  BS((tr, MEM_W), lambda i: (i, 0))],
        out_specs=BS((tr, D), lambda i: (i, 0)), out_shape=SDS((s, D), MXU),
        compiler_params=_cp("parallel"), name=name,
    )(h, proj, m)


def lru_bwd(dym, proj, xc, hl, dqm, conv_w, wa, ba, wx, bx, lam, name):
    s = proj.shape[0]
    c = MIX_W
    nblk = s // LRU_T
    hpb = LRU_T // HALO
    wa_m = wa.astype(MXU)
    wx_m = wx.astype(MXU)

    def body(dy_ref, x_ref, xhalo_ref, gb_ref, xc_ref, h_ref, hhalo_ref, dqm_ref,
             cw_ref, wa_ref, ba_ref, wx_ref, bx_ref, lam_ref,
             dproj_ref, dcw_ref, dcb_ref, dwa_ref, dba_ref, dwx_ref, dbx_ref, dlam_ref,
             g_next, a_next, dxc_next, za_ref, zx_ref, dxc_ref):
        i = pl.program_id(0)

        @pl.when(i == 0)
        def _():
            g_next[...] = jnp.zeros_like(g_next)
            a_next[...] = jnp.zeros_like(a_next)
            dxc_next[...] = jnp.zeros_like(dxc_next)
            for r in (dcw_ref, dcb_ref, dwa_ref, dba_ref, dwx_ref, dbx_ref, dlam_ref):
                r[...] = jnp.zeros_like(r)

        first = i == nblk - 1
        xc = xc_ref[...]
        lam = lam_ref[...]
        ra, ii, sp, a, mult = _gates(xc, wa_ref, ba_ref[...], wx_ref, bx_ref[...], lam, za_ref, zx_ref)
        hl_v = h_ref[...]
        ge, dge = _gelu_and_grad(gb_ref[...])
        dyl = dy_ref[...]
        dhl = dyl * ge
        dproj_ref[:, c:2 * c] = (dyl * hl_v * dge).astype(dproj_ref.dtype)
        dproj_ref[:, 2 * c:] = dqm_ref[...]

        an = _shift_up(a, 1, 0.0)
        last_row = lax.broadcasted_iota(jnp.int32, a.shape, 0) == LRU_T - 1
        an = jnp.where(last_row, a_next[0:1, :], an)
        gb_acc = dhl
        sh = 1
        while sh < LRU_T:
            gb_acc = an * _shift_up(gb_acc, sh, 0.0) + gb_acc
            an = an * _shift_up(an, sh, 1.0)
            sh *= 2
        g = gb_acc + an * g_next[0:1, :]
        g_next[...] = g[:HALO, :]
        a_next[...] = a[:HALO, :]

        hhalo = jnp.where(first, 0.0, hhalo_ref[...])
        h_prev = _shift_down(hl_v, 1, 0.0)
        first_row = lax.broadcasted_iota(jnp.int32, a.shape, 0) == 0
        h_prev = jnp.where(first_row, hhalo[HALO - 1:HALO, :], h_prev)
        da = g * h_prev
        ixc = ii * xc
        dmult = g * ixc
        dii = g * mult * xc
        dxc = g * mult * ii
        dlog_a = (da - dmult * a / mult) * a
        dra = dlog_a * (-LRU_C) * sp
        dlam_ref[...] += jnp.sum(dlog_a * ra, axis=0, keepdims=True) * (LRU_C * _sigmoid(-lam))
        dza = dra * ra * (1.0 - ra)
        dzx = dii * ii * (1.0 - ii)
        dba_ref[...] += jnp.sum(dza, axis=0, keepdims=True)
        dbx_ref[...] += jnp.sum(dzx, axis=0, keepdims=True)
        xm = xc.astype(MXU)
        dza_m = dza.astype(MXU)
        dzx_m = dzx.astype(MXU)
        for n in range(N_BLK):
            sl = slice(n * HD, (n + 1) * HD)
            dwa_ref[n] += _dot(xm[:, sl], dza_m[:, sl], TN)
            dwx_ref[n] += _dot(xm[:, sl], dzx_m[:, sl], TN)
            dxc_ref[:, sl] = _dot(dza_m[:, sl], wa_ref[n], NT) + _dot(dzx_m[:, sl], wx_ref[n], NT)
        dxc = dxc + dxc_ref[...]

        dcat = jnp.concatenate([dxc, dxc_next[...]], axis=0)
        rows = dcat.shape[0]
        dxb = cw_ref[3:4, :] * dxc
        for k in range(3):
            dxb = dxb + cw_ref[k:k + 1, :] * pltpu.roll(dcat, rows - (3 - k), 0)[:LRU_T]
        dproj_ref[:, :c] = dxb.astype(dproj_ref.dtype)
        dxc_next[...] = dxc[:HALO, :]

        xhalo = jnp.where(first, 0.0, xhalo_ref[...])
        taps = _conv_taps(jnp.concatenate([xhalo, x_ref[...]], axis=0))
        for k in range(4):
            dcw_ref[k:k + 1, :] += jnp.sum(dxc * taps[k], axis=0, keepdims=True)
        dcb_ref[...] += jnp.sum(dxc, axis=0, keepdims=True)

    def full(shape):
        return BS(shape, lambda i: (0,) * len(shape))

    def rev(i):
        return nblk - 1 - i

    blk0 = BS((LRU_T, c), lambda i: (rev(i), 0))
    blk1 = BS((LRU_T, c), lambda i: (rev(i), 1))
    halo = BS((HALO, c), lambda i: (jnp.maximum(rev(i) * hpb - 1, 0), 0))
    outs = pl.pallas_call(
        body, grid=(nblk,),
        in_specs=[blk0, blk0, halo, blk1, blk0, blk0, halo, BS((LRU_T, MEM_W), lambda i: (rev(i), 0)),
                  full((4, c)), full((N_BLK, HD, HD)), full((1, c)), full((N_BLK, HD, HD)), full((1, c)),
                  full((1, c))],
        out_specs=[BS((LRU_T, 2 * c + MEM_W), lambda i: (rev(i), 0)), full((4, c)), full((1, c)),
                   full((N_BLK, HD, HD)), full((1, c)), full((N_BLK, HD, HD)), full((1, c)), full((1, c))],
        out_shape=[SDS((s, 2 * c + MEM_W), MXU), SDS((4, c), F32), SDS((1, c), F32),
                   SDS((N_BLK, HD, HD), F32), SDS((1, c), F32), SDS((N_BLK, HD, HD), F32), SDS((1, c), F32),
                   SDS((1, c), F32)],
        scratch_shapes=[pltpu.VMEM((HALO, c), F32), pltpu.VMEM((HALO, c), F32), pltpu.VMEM((HALO, c), F32),
                        pltpu.VMEM((LRU_T, c), F32), pltpu.VMEM((LRU_T, c), F32), pltpu.VMEM((LRU_T, c), F32)],
        compiler_params=_cp("arbitrary"), name=name,
    )(dym, proj, proj, proj, xc, hl, hl, dqm, conv_w, wa_m, ba.reshape(1, c), wx_m, bx.reshape(1, c),
      lam.reshape(1, c))
    dproj, dcw, dcb, dwa, dba, dwx, dbx, dlam = outs
    return dproj, dcw, dcb.reshape(c), dwa, dba.reshape(c), dwx, dbx.reshape(c), dlam.reshape(c)


def _mem_probs(q, k):
    sc = _dot(q, k, NT) * SCALE
    e = jnp.exp(sc - jnp.max(sc, axis=-1, keepdims=True))
    return e / jnp.sum(e, axis=-1, keepdims=True)


def mem_attn_fwd(proj, q_col, kvm, name):
    s = proj.shape[0]
    tq = min(512, s)

    def body(q_ref, kv_ref, o_ref):
        q = q_ref[...].astype(MXU)
        for hh in range(MEM_HEADS):
            sl = slice(hh * HD, (hh + 1) * HD)
            p = _mem_probs(q[:, sl], kv_ref[:, sl])
            o_ref[:, sl] = _dot(p.astype(MXU), kv_ref[:, MEM_W + hh * HD:MEM_W + (hh + 1) * HD]).astype(o_ref.dtype)

    return pl.pallas_call(
        body, grid=(s // tq,),
        in_specs=[BS((tq, MEM_W), lambda i: (i, q_col)), BS((N_MEM, 2 * MEM_W), lambda i: (0, 0))],
        out_specs=BS((tq, MEM_W), lambda i: (i, 0)), out_shape=SDS((s, MEM_W), MXU),
        compiler_params=_cp("parallel"), name=name,
    )(proj, kvm)


def mem_attn_bwd(proj, q_col, kvm, dym, name):
    s = proj.shape[0]
    tq = min(512, s)

    def body(q_ref, kv_ref, do_ref, dq_ref, dkv_ref):
        @pl.when(pl.program_id(0) == 0)
        def _():
            dkv_ref[...] = jnp.zeros_like(dkv_ref)

        q = q_ref[...].astype(MXU)
        do = do_ref[...].astype(MXU)
        for hh in range(MEM_HEADS):
            sl = slice(hh * HD, (hh + 1) * HD)
            vsl = slice(MEM_W + hh * HD, MEM_W + (hh + 1) * HD)
            k = kv_ref[:, sl]
            p = _mem_probs(q[:, sl], k)
            dp = _dot(do[:, sl], kv_ref[:, vsl], NT)
            ds = (p * (dp - jnp.sum(p * dp, axis=-1, keepdims=True)) * SCALE).astype(MXU)
            dq_ref[:, sl] = _dot(ds, k).astype(dq_ref.dtype)
            dkv_ref[:, sl] += _dot(ds, q[:, sl], TN)
            dkv_ref[:, vsl] += _dot(p.astype(MXU), do[:, sl], TN)

    return pl.pallas_call(
        body, grid=(s // tq,),
        in_specs=[BS((tq, MEM_W), lambda i: (i, q_col)), BS((N_MEM, 2 * MEM_W), lambda i: (0, 0)),
                  BS((tq, MEM_W), lambda i: (i, MIX_W // MEM_W))],
        out_specs=[BS((tq, MEM_W), lambda i: (i, 0)), BS((N_MEM, 2 * MEM_W), lambda i: (0, 0))],
        out_shape=[SDS((s, MEM_W), MXU), SDS((N_MEM, 2 * MEM_W), F32)],
        compiler_params=_cp("arbitrary"), name=name,
    )(proj, kvm, dym)


def _dil_scores(q, kp, kc, n, slope_dil):
    qi = lax.broadcasted_iota(jnp.int32, (Q_BLOCK, Q_BLOCK), 0)
    ki = lax.broadcasted_iota(jnp.int32, (Q_BLOCK, Q_BLOCK), 1)
    rel_p = qi + Q_BLOCK - ki
    rel_c = qi - ki
    s_p = _dot(q, kp, NT) * SCALE - slope_dil * rel_p.astype(F32)
    s_c = _dot(q, kc, NT) * SCALE - slope_dil * rel_c.astype(F32)
    s_p = jnp.where((rel_p <= Q_BLOCK) & (n > 0), s_p, NEG_INF)
    s_c = jnp.where(rel_c >= 0, s_c, NEG_INF)
    return s_p, s_c


def _slope_dil(gi, hh):
    head = 4 * gi + hh
    return DIL_GROUPS[gi][1] * 2.0 ** (-8.0 * (head + 1.0) / N_BLK)


def dil_attn_fwd(proj, kv, gi, name):
    dil = DIL_GROUPS[gi][1]
    s, pw = proj.shape
    sub = s // dil
    nb = sub // Q_BLOCK
    qc, kc_ = pw // MEM_W, kv.shape[1] // MEM_W

    def body(q_ref, kp_ref, kc_ref, vp_ref, vc_ref, o_ref, lse_ref):
        n = pl.program_id(1)
        q = q_ref[...].astype(MXU)
        for hh in range(4):
            sl = slice(hh * HD, (hh + 1) * HD)
            s_p, s_c = _dil_scores(q[:, sl], kp_ref[:, sl], kc_ref[:, sl], n, _slope_dil(gi, hh))
            mx = jnp.maximum(jnp.max(s_p, axis=-1, keepdims=True), jnp.max(s_c, axis=-1, keepdims=True))
            den = jnp.sum(jnp.exp(s_p - mx), axis=-1, keepdims=True) + jnp.sum(jnp.exp(s_c - mx), axis=-1, keepdims=True)
            lse = mx + jnp.log(den)
            o_ref[:, sl] = (_dot(jnp.exp(s_p - lse).astype(MXU), vp_ref[:, sl])
                            + _dot(jnp.exp(s_c - lse).astype(MXU), vc_ref[:, sl]))
            lse_ref[:, sl] = jnp.broadcast_to(lse, (Q_BLOCK, HD))

    blk = (Q_BLOCK, MEM_W)
    prev = lambda n: jnp.maximum(n - 1, 0)
    out = BS(blk, lambda r, n: (n, r))
    return pl.pallas_call(
        body, grid=(dil, nb),
        in_specs=[BS(blk, lambda r, n: (n, r * qc + gi)),
                  BS(blk, lambda r, n: (prev(n), r * kc_ + gi)), BS(blk, lambda r, n: (n, r * kc_ + gi)),
                  BS(blk, lambda r, n: (prev(n), r * kc_ + 3 + gi)), BS(blk, lambda r, n: (n, r * kc_ + 3 + gi))],
        out_specs=[out, out],
        out_shape=[SDS((sub, dil * MEM_W), F32), SDS((sub, dil * MEM_W), F32)],
        compiler_params=_cp("parallel", "parallel"), name=name,
    )(proj.reshape(sub, dil * pw), *([kv.reshape(sub, dil * kv.shape[1])] * 4))


def dil_attn_bwd(proj, kv, lse, do, dd, gi, name):
    dil = DIL_GROUPS[gi][1]
    s, pw = proj.shape
    sub = s // dil
    nb = sub // Q_BLOCK
    qc, kc_ = pw // MEM_W, kv.shape[1] // MEM_W

    def body(q_ref, kp_ref, kc_ref, vp_ref, vc_ref, lse_ref, do_ref, dd_ref, dq_ref, dk_ref, dv_ref, ck, cv):
        n = pl.program_id(1)

        @pl.when(n == 0)
        def _():
            ck[...] = jnp.zeros_like(ck)
            cv[...] = jnp.zeros_like(cv)

        @pl.when(n < nb)
        def _():
            q = q_ref[...].astype(MXU)
            do_m = do_ref[...].astype(MXU)
            for hh in range(4):
                sl = slice(hh * HD, (hh + 1) * HD)
                s_p, s_c = _dil_scores(q[:, sl], kp_ref[:, sl], kc_ref[:, sl], n, _slope_dil(gi, hh))
                lse_h = lse_ref[:, sl]
                dd_h = dd_ref[:, sl]
                p_p = jnp.exp(s_p - lse_h)
                p_c = jnp.exp(s_c - lse_h)
                ds_p = (p_p * (_dot(do_m[:, sl], vp_ref[:, sl], NT) + dd_h) * SCALE).astype(MXU)
                ds_c = (p_c * (_dot(do_m[:, sl], vc_ref[:, sl], NT) + dd_h) * SCALE).astype(MXU)
                dq_ref[:, sl] = (_dot(ds_p, kp_ref[:, sl]) + _dot(ds_c, kc_ref[:, sl])).astype(dq_ref.dtype)
                dk_ref[:, sl] = ck[:, sl] + _dot(ds_p, q[:, sl], TN)
                dv_ref[:, sl] = cv[:, sl] + _dot(p_p.astype(MXU), do_m[:, sl], TN)
                ck[:, sl] = _dot(ds_c, q[:, sl], TN)
                cv[:, sl] = _dot(p_c.astype(MXU), do_m[:, sl], TN)

        @pl.when(n == nb)
        def _():
            dk_ref[...] = ck[...]
            dv_ref[...] = cv[...]

    blk = (Q_BLOCK, MEM_W)
    cur = lambda n: jnp.minimum(n, nb - 1)
    prev = lambda n: jnp.maximum(jnp.minimum(n, nb - 1) - 1, 0)
    done = lambda n: jnp.maximum(n - 1, 0)
    own = BS(blk, lambda r, n: (cur(n), r))
    kvv = kv.reshape(sub, dil * kv.shape[1])
    return pl.pallas_call(
        body, grid=(dil, nb + 1),
        in_specs=[BS(blk, lambda r, n: (cur(n), r * qc + gi)),
                  BS(blk, lambda r, n: (prev(n), r * kc_ + gi)), BS(blk, lambda r, n: (cur(n), r * kc_ + gi)),
                  BS(blk, lambda r, n: (prev(n), r * kc_ + 3 + gi)), BS(blk, lambda r, n: (cur(n), r * kc_ + 3 + gi)),
                  own, own, own],
        out_specs=[own, BS(blk, lambda r, n: (done(n), r)), BS(blk, lambda r, n: (done(n), r))],
        out_shape=[SDS((sub, dil * MEM_W), MXU), SDS((sub, dil * MEM_W), F32), SDS((sub, dil * MEM_W), F32)],
        scratch_shapes=[pltpu.VMEM(blk, F32), pltpu.VMEM(blk, F32)],
        compiler_params=_cp("parallel", "arbitrary"), name=name,
    )(proj.reshape(sub, dil * pw), kvv, kvv, kvv, kvv, lse, do, dd)


def _group_weights(lse_refs):
    l0, l1, l2 = (r[...] for r in lse_refs)
    mx = jnp.maximum(jnp.maximum(l0, l1), l2)
    e = [jnp.exp(l - mx) for l in (l0, l1, l2)]
    den = e[0] + e[1] + e[2]
    return [x / den for x in e]


def dil_mix_prep(o_list, lse_list, m, name):
    s = m.shape[0]
    tr = _row_tile(s)

    def body(o0, o1, o2, l0, l1, l2, m_ref, out_ref):
        w = _group_weights((l0, l1, l2))
        for g, o_ref in enumerate((o0, o1, o2)):
            out_ref[:, g * MEM_W:(g + 1) * MEM_W] = (o_ref[...] * w[g]).astype(out_ref.dtype)
        out_ref[:, MIX_W:] = m_ref[...]

    blk = BS((tr, MEM_W), lambda i: (i, 0))
    return pl.pallas_call(
        body, grid=(s // tr,), in_specs=[blk] * 7,
        out_specs=BS((tr, D), lambda i: (i, 0)), out_shape=SDS((s, D), MXU),
        compiler_params=_cp("parallel"), name=name,
    )(*o_list, *lse_list, m)


def dil_mix_bwd(dym, o_list, lse_list, name):
    s = dym.shape[0]
    tr = _row_tile(s)

    def body(da_ref, o0, o1, o2, l0, l1, l2, do0, do1, do2, dd0, dd1, dd2):
        w = _group_weights((l0, l1, l2))
        tot = None
        for g, (o_ref, do_ref) in enumerate(zip((o0, o1, o2), (do0, do1, do2))):
            da = da_ref[:, g * MEM_W:(g + 1) * MEM_W]
            do_ref[...] = da * w[g]
            x = da * o_ref[...]
            dw = jnp.concatenate(
                [jnp.broadcast_to(jnp.sum(x[:, hh * HD:(hh + 1) * HD], axis=-1, keepdims=True), (tr, HD))
                 for hh in range(4)], axis=1)
            tot = w[g] * dw if tot is None else tot + w[g] * dw
        for g, dd_ref in enumerate((dd0, dd1, dd2)):
            dd_ref[...] = -w[g] * tot

    blk = BS((tr, MEM_W), lambda i: (i, 0))
    outs = pl.pallas_call(
        body, grid=(s // tr,), in_specs=[BS((tr, MIX_W), lambda i: (i, 0))] + [blk] * 6,
        out_specs=[blk] * 6, out_shape=[SDS((s, MEM_W), F32)] * 6,
        compiler_params=_cp("parallel"), name=name,
    )(dym, *o_list, *lse_list)
    return outs[:3], outs[3:]


def sum_cast(parts, name):
    s = parts[0][0].shape[0]
    tr = _row_tile(s)
    flat = [a for p in parts for a in p]
    sizes = [len(p) for p in parts]

    def body(*refs):
        out_ref = refs[-1]
        pos = 0
        for j, n in enumerate(sizes):
            acc = refs[pos][...].astype(F32)
            for t in range(1, n):
                acc = acc + refs[pos + t][...].astype(F32)
            out_ref[:, j * MEM_W:(j + 1) * MEM_W] = acc.astype(out_ref.dtype)
            pos += n

    blk = BS((tr, MEM_W), lambda i: (i, 0))
    width = MEM_W * len(parts)
    return pl.pallas_call(
        body, grid=(s // tr,), in_specs=[blk] * len(flat),
        out_specs=BS((tr, width), lambda i: (i, 0)), out_shape=SDS((s, width), MXU),
        compiler_params=_cp("parallel"), name=name,
    )(*flat)


def add_n(arrs, name):
    rows, cols = arrs[0].shape
    tr = _row_tile(rows)

    def body(*refs):
        acc = refs[0][...]
        for r in refs[1:-1]:
            acc = acc + r[...]
        refs[-1][...] = acc

    blk = BS((tr, cols), lambda i: (i, 0))
    return pl.pallas_call(
        body, grid=(rows // tr,), in_specs=[blk] * len(arrs), out_specs=blk,
        out_shape=SDS((rows, cols), F32), compiler_params=_cp("parallel"), name=name,
    )(*arrs)


def _fwd_bwd(x, mem, target, small, big):
    s = x.shape[0]
    tm = min(1024, s)
    ts = min(512, s)
    gs, gb = {}, {}

    mem_n = norm_cast(mem, small["mem_norm_g"], "mem_norm")
    h = x
    saved = []
    kv = None
    for l in range(4):
        rec = l < 2
        p, j = ("a", l) if rec else ("b", l - 2)
        t = "rec" if rec else "dil"
        sv = {"h": h}
        kvm = mm_nn(mem_n, big[p + "_w_mem_kv"][j], tm=N_MEM, tn=2 * MEM_W, tk=D, out_dtype=MXU, name="mem_kv")
        hn = norm_cast(h, small[p + "_pre_mix_g"][j], "pre_norm")
        if rec:
            proj = mm_nn(hn, big["a_w_in"][j], tm=tm, tn=896, tk=D, out_dtype=F32, name="rec_in")
            xc, hl = lru_fwd(proj, small["a_conv_w"][j], small["a_conv_b"][j], small["a_gate_a_w"][j],
                             small["a_gate_a_b"][j], small["a_gate_x_w"][j], small["a_gate_x_b"][j],
                             small["a_lambda"][j], "lru_fwd")
            m = mem_attn_fwd(proj, 2 * MIX_W // MEM_W, kvm, "rec_mem_attn")
            ym = lru_mix_prep(hl, proj, m, "lru_mix_prep")
            sv.update(xc=xc, hl=hl)
        else:
            proj = mm_nn(hn, big["b_w_in"][j], tm=tm, tn=1024, tk=D, out_dtype=F32, name="dil_in")
            o_list, lse_list = [], []
            for gi in range(3):
                o, lse = dil_attn_fwd(proj, kv, gi, "dil_attn_fwd%d" % gi)
                o_list.append(o.reshape(s, MEM_W))
                lse_list.append(lse.reshape(s, MEM_W))
            m = mem_attn_fwd(proj, MIX_W // MEM_W, kvm, "dil_mem_attn")
            ym = dil_mix_prep(o_list, lse_list, m, "dil_mix_prep")
            sv.update(o=o_list, lse=lse_list)
        mix = mm_nn(ym, big[p + "_w_out"][j], tm=tm, tn=1024, tk=D, out_dtype=F32, name="mix_out")
        h1 = resid_norm(h, mix, small[p + "_post_mix_g"][j], "post_norm")
        hn2 = norm_cast(h1, small[p + "_pre_ffn_g"][j], "pre_norm")
        g, u, act = ffn_in_fwd(hn2, big[p + "_w_ffn_in"][j], "ffn_in")
        y2 = mm_nn(act, big[p + "_w_ffn_out"][j], tm=tm, tn=D, tk=D_FF // 4, out_dtype=F32, name="ffn_out")
        h = resid_norm(h1, y2, small[p + "_post_ffn_g"][j], "post_norm")
        sv.update(kvm=kvm, hn=hn, proj=proj, ym=ym, mix=mix, h1=h1, hn2=hn2, g=g, u=u, act=act, y2=y2)
        saved.append(sv)
        if l == 1:
            h_kv = h
            kvn = norm_cast(h, small["kv_norm_g"], "pre_norm")
            kv = mm_nn(kvn, big["w_kv_shared"], tm=tm, tn=768, tk=D, out_dtype=MXU, name="kv_proj")

    loss_parts, dh = loss_head(h, target, "loss_head")

    def stack2(name, j, val):
        gs.setdefault(name, [None, None])[j] = val

    def stack2b(name, j, val):
        gb.setdefault(name, [None, None])[j] = val

    dkv_parts = []
    dkvm = [None] * 4
    for l in (3, 2, 1, 0):
        rec = l < 2
        p, j = ("a", l) if rec else ("b", l - 2)
        sv = saved[l]
        if l == 1:
            dkv = sum_cast([(dkv_parts[0][c], dkv_parts[1][c]) for c in range(6)], "dkv_sum")
            dkvn = mm_nt([dkv], big["w_kv_shared"], tm=tm, tn=D, tk=768, out_dtype=F32, name="kv_proj_dx")
            gb["w_kv_shared"] = mm_tn(kvn, [dkv], t1=D, tn=768, ts=ts, col_shards=True, name="kv_proj_dw")
            dh, gs["kv_norm_g"] = norm_bwd(h_kv, small["kv_norm_g"], dkvn, dh, F32, "pre_norm_bwd")
        dy2, dg = norm_bwd(sv["y2"], small[p + "_post_ffn_g"][j], dh, None, MXU, "post_norm_bwd")
        stack2(p + "_post_ffn_g", j, dg)
        dgg, dgu = ffn_act_bwd(dy2, big[p + "_w_ffn_out"][j], sv["g"], sv["u"], "ffn_act_bwd")
        stack2b(p + "_w_ffn_out", j, mm_tn(sv["act"], [dy2], t1=D_FF // 4, tn=D, ts=ts, col_shards=False,
                                          name="ffn_out_dw"))
        dhn2 = mm_nt([dgg, dgu], big[p + "_w_ffn_in"][j], tm=tm, tn=D, tk=D_FF // 4, out_dtype=F32,
                     name="ffn_in_dx")
        stack2b(p + "_w_ffn_in", j, mm_tn(sv["hn2"], [dgg, dgu], t1=D, tn=D_FF // 4, ts=ts, col_shards=True,
                                         name="ffn_in_dw"))
        dh1, dg = norm_bwd(sv["h1"], small[p + "_pre_ffn_g"][j], dhn2, dh, F32, "pre_norm_bwd")
        stack2(p + "_pre_ffn_g", j, dg)
        dmix, dg = norm_bwd(sv["mix"], small[p + "_post_mix_g"][j], dh1, None, MXU, "post_norm_bwd")
        stack2(p + "_post_mix_g", j, dg)
        dym = mm_nt([dmix], big[p + "_w_out"][j], tm=tm, tn=1024, tk=D, out_dtype=F32, name="mix_out_dx")
        stack2b(p + "_w_out", j, mm_tn(sv["ym"], [dmix], t1=D, tn=1024, ts=ts, col_shards=False,
                                      name="mix_out_dw"))
        if rec:
            dqm, dkvm[l] = mem_attn_bwd(sv["proj"], 2 * MIX_W // MEM_W, sv["kvm"], dym, "rec_mem_attn_bwd")
            dproj, dcw, dcb, dwa, dba, dwx, dbx, dlam = lru_bwd(
                dym, sv["proj"], sv["xc"], sv["hl"], dqm, small["a_conv_w"][j], small["a_gate_a_w"][j],
                small["a_gate_a_b"][j], small["a_gate_x_w"][j], small["a_gate_x_b"][j], small["a_lambda"][j],
                "lru_bwd")
            for nm, val in (("a_conv_w", dcw), ("a_conv_b", dcb), ("a_gate_a_w", dwa), ("a_gate_a_b", dba),
                            ("a_gate_x_w", dwx), ("a_gate_x_b", dbx), ("a_lambda", dlam)):
                stack2(nm, j, val)
            dhn = mm_nt([dproj], big["a_w_in"][j], tm=tm, tn=D, tk=896, out_dtype=F32, name="rec_in_dx")
            stack2b("a_w_in", j, mm_tn(sv["hn"], [dproj], t1=D, tn=896, ts=ts, col_shards=True, name="rec_in_dw"))
        else:
            dqm, dkvm[l] = mem_attn_bwd(sv["proj"], MIX_W // MEM_W, sv["kvm"], dym, "dil_mem_attn_bwd")
            do_list, dd_list = dil_mix_bwd(dym, sv["o"], sv["lse"], "dil_mix_bwd")
            dq_list, dk_list, dv_list = [], [], []
            for gi in range(3):
                dil = DIL_GROUPS[gi][1]
                view = (s // dil, dil * MEM_W)
                dq, dk, dv = dil_attn_bwd(sv["proj"], kv, sv["lse"][gi].reshape(view), do_list[gi].reshape(view),
                                          dd_list[gi].reshape(view), gi, "dil_attn_bwd%d" % gi)
                dq_list.append(dq.reshape(s, MEM_W))
                dk_list.append(dk.reshape(s, MEM_W))
                dv_list.append(dv.reshape(s, MEM_W))
            dkv_parts.append(dk_list + dv_list)
            dproj = sum_cast([(a,) for a in dq_list + [dqm]], "dil_dproj")
            dhn = mm_nt([dproj], big["b_w_in"][j], tm=tm, tn=1024, tk=D, out_dtype=F32, name="dil_in_dx")
            stack2b("b_w_in", j, mm_tn(sv["hn"], [dproj], t1=D, tn=1024, ts=ts, col_shards=False, name="dil_in_dw"))
        dh, dg = norm_bwd(sv["h"], small[p + "_pre_mix_g"][j], dhn, dh1, F32, "pre_norm_bwd")
        stack2(p + "_pre_mix_g", j, dg)

    dmem_parts = []
    for l in range(4):
        p, j = ("a", l) if l < 2 else ("b", l - 2)
        dk_m = dkvm[l].astype(MXU)
        dmem_parts.append(mm_nt([dk_m], big[p + "_w_mem_kv"][j], tm=N_MEM, tn=D, tk=2 * MEM_W, out_dtype=F32,
                                name="mem_kv_dx"))
        stack2b(p + "_w_mem_kv", j, mm_tn(mem_n, [dk_m], t1=D, tn=2 * MEM_W, ts=N_MEM, col_shards=False,
                                         name="mem_kv_dw"))
    _, gs["mem_norm_g"] = norm_bwd(mem, small["mem_norm_g"], add_n(dmem_parts, "dmem_sum"), None, F32,
                                   "mem_norm_bwd")
    return loss_parts, dh, gs, gb


ANY = pl.BlockSpec(memory_space=pl.ANY)
CHIP_FLIPS = (1, 2, 3)


def _coords():
    return lax.axis_index("x"), lax.axis_index("y"), lax.axis_index("c")


def _flip(x, y, m):
    return x ^ (m >> 1), y ^ (m & 1)


def _remote(src, dst, send_sems, recv_sems, k, device):
    return pltpu.make_async_remote_copy(src_ref=src, dst_ref=dst, send_sem=send_sems.at[k], recv_sem=recv_sems.at[k],
                                        device_id=device, device_id_type=MESH)


def gather_shards(shards, name):
    n = len(shards)

    def body(*refs):
        ins, outs = refs[:n], refs[n:2 * n]
        send_sems, recv_sems, loc_sems = refs[2 * n:]
        x, y, c = _coords()
        me = 2 * x + y
        sib = (x, y, 1 - c)
        halves, locs, sends = [], [], []
        for i in range(n):
            hr = shards[i].shape[0] // 2
            mine = pl.ds(pl.multiple_of(c * hr, 8), hr)
            other = pl.ds(pl.multiple_of((1 - c) * hr, 8), hr)
            halves.append((mine, other))
            loc = pltpu.make_async_copy(ins[i], outs[i].at[me], loc_sems.at[i])
            loc.start()
            locs.append(loc)
            for j, m in enumerate(CHIP_FLIPS):
                cp = _remote(ins[i].at[mine], outs[i].at[me, mine], send_sems, recv_sems, 6 * i + j,
                             (*_flip(x, y, m), c))
                cp.start()
                sends.append(cp)
        for i in range(n):
            mine, _ = halves[i]
            for j, m in enumerate(CHIP_FLIPS):
                slot = outs[i].at[me ^ m, mine]
                _remote(slot, slot, send_sems, recv_sems, 6 * i + j, sib).wait_recv()
                fwd = _remote(slot, slot, send_sems, recv_sems, 6 * i + 3 + j, sib)
                fwd.start()
                sends.append(fwd)
        for i in range(n):
            _, other = halves[i]
            for j, m in enumerate(CHIP_FLIPS):
                slot = outs[i].at[me ^ m, other]
                _remote(slot, slot, send_sems, recv_sems, 6 * i + 3 + j, sib).wait_recv()
        for cp in sends:
            cp.wait_send()
        for loc in locs:
            loc.wait()

    return pl.pallas_call(
        body, in_specs=[ANY] * n, out_specs=[ANY] * n,
        out_shape=[SDS((N_CHIPS,) + sh.shape, sh.dtype) for sh in shards],
        scratch_shapes=[pltpu.SemaphoreType.DMA((6 * n,)), pltpu.SemaphoreType.DMA((6 * n,)),
                        pltpu.SemaphoreType.DMA((n,))],
        name=name,
    )(*shards)


def swap_halves(grads, name):
    n = len(grads)

    def body(*refs):
        ins, outs = refs[:n], refs[n:2 * n]
        send_sems, recv_sems = refs[2 * n:]
        x, y, c = _coords()
        cps = []
        for i in range(n):
            hr = grads[i].shape[1] // 2
            other = pl.ds(pl.multiple_of((1 - c) * hr, 8), hr)
            cp = _remote(ins[i].at[pl.ds(0, N_CHIPS), other], outs[i], send_sems, recv_sems, i, (x, y, 1 - c))
            cp.start()
            cps.append(cp)
        for cp in cps:
            cp.wait()

    return pl.pallas_call(
        body, in_specs=[ANY] * n, out_specs=[ANY] * n,
        out_shape=[SDS((N_CHIPS, g.shape[1] // 2, g.shape[2]), g.dtype) for g in grads],
        scratch_shapes=[pltpu.SemaphoreType.DMA((n,)), pltpu.SemaphoreType.DMA((n,))],
        name=name,
    )(*grads)


def _sum_rows_tile(rows, cols):
    for tr in (512, 256, 128, 64, 32, 16):
        if rows % tr == 0 and tr * cols * 4 <= 2 * 1024 * 1024:
            return tr
    raise ValueError((rows, cols))


def half_sum(g, got, c_arr, name):
    _, r, cols = g.shape
    hr = r // 2
    tr = _sum_rows_tile(hr, cols)

    def body(c_ref, g_ref, got_ref, o_ref):
        o_ref[...] = (g_ref[...] + got_ref[...]).astype(o_ref.dtype)

    grid_spec = pltpu.PrefetchScalarGridSpec(
        num_scalar_prefetch=1, grid=(N_CHIPS, hr // tr),
        in_specs=[BS((None, None, tr, cols), lambda s, i, c_ref: (s, c_ref[0], i, 0)),
                  BS((None, tr, cols), lambda s, i, c_ref: (s, i, 0))],
        out_specs=BS((None, tr, cols), lambda s, i, c_ref: (s, i, 0)))
    return pl.pallas_call(
        body, grid_spec=grid_spec, out_shape=SDS((N_CHIPS, hr, cols), jnp.bfloat16),
        compiler_params=_cp("parallel", "parallel"), name=name,
    )(c_arr, g.reshape(N_CHIPS, 2, hr, cols), got)


def exchange_parts(parts, name):
    n = len(parts)

    def body(*refs):
        ins, outs = refs[:n], refs[n:2 * n]
        send_sems, recv_sems, loc_sems = refs[2 * n:]
        x, y, c = _coords()
        me = 2 * x + y
        cps, locs = [], []
        for i in range(n):
            loc = pltpu.make_async_copy(ins[i].at[me], outs[i].at[me], loc_sems.at[i])
            loc.start()
            locs.append(loc)
            for j, m in enumerate(CHIP_FLIPS):
                cp = _remote(ins[i].at[me ^ m], outs[i].at[me], send_sems, recv_sems, 3 * i + j, (*_flip(x, y, m), c))
                cp.start()
                cps.append(cp)
        for cp in cps:
            cp.wait()
        for loc in locs:
            loc.wait()

    return pl.pallas_call(
        body, in_specs=[ANY] * n, out_specs=[ANY] * n,
        out_shape=[SDS(p.shape, p.dtype) for p in parts],
        scratch_shapes=[pltpu.SemaphoreType.DMA((3 * n,)), pltpu.SemaphoreType.DMA((3 * n,)),
                        pltpu.SemaphoreType.DMA((n,))],
        name=name,
    )(*parts)


def slot_sum(slots, name):
    _, hr, cols = slots.shape
    tr = _sum_rows_tile(hr, cols)

    def body(s_ref, o_ref):
        acc = s_ref[0].astype(F32)
        for p in range(1, N_CHIPS):
            acc = acc + s_ref[p].astype(F32)
        o_ref[...] = acc

    return pl.pallas_call(
        body, grid=(hr // tr,), in_specs=[BS((N_CHIPS, tr, cols), lambda i: (0, i, 0))],
        out_specs=BS((tr, cols), lambda i: (i, 0)), out_shape=SDS((hr, cols), F32),
        compiler_params=_cp("parallel"), name=name,
    )(slots)


def share_halves(fs, name):
    n = len(fs)

    def body(*refs):
        ins, outs = refs[:n], refs[n:2 * n]
        send_sems, recv_sems, loc_sems = refs[2 * n:]
        x, y, c = _coords()
        cps = []
        for i in range(n):
            hr = fs[i].shape[0]
            mine = pl.ds(pl.multiple_of(c * hr, 8), hr)
            loc = pltpu.make_async_copy(ins[i], outs[i].at[mine], loc_sems.at[i])
            loc.start()
            cp = _remote(ins[i], outs[i].at[mine], send_sems, recv_sems, i, (x, y, 1 - c))
            cp.start()
            cps += [loc, cp]
        for cp in cps:
            cp.wait()

    return pl.pallas_call(
        body, in_specs=[ANY] * n, out_specs=[ANY] * n,
        out_shape=[SDS((2 * f.shape[0], f.shape[1]), f.dtype) for f in fs],
        scratch_shapes=[pltpu.SemaphoreType.DMA((n,)), pltpu.SemaphoreType.DMA((n,)),
                        pltpu.SemaphoreType.DMA((n,))],
        name=name,
    )(*fs)


def reduce_scatter(grads, c_arr, tag):
    got = swap_halves(grads, "rs_swap_" + tag)
    parts = [half_sum(g, r, c_arr, "rs_half_sum") for g, r in zip(grads, got)]
    slots = exchange_parts(parts, "rs_exchange_" + tag)
    return share_halves([slot_sum(s, "rs_slot_sum") for s in slots], "rs_share_" + tag)


VM = pl.BlockSpec(memory_space=pltpu.VMEM)


def small_gather(v, name):
    def body(v_ref, out_ref, send_sems, recv_sems):
        x, y, c = _coords()
        me = 2 * x + y
        out_ref[me] = v_ref[...]
        cps = []
        for j, m in enumerate(CHIP_FLIPS):
            cp = _remote(v_ref, out_ref.at[me], send_sems, recv_sems, j, (*_flip(x, y, m), c))
            cp.start()
            cps.append(cp)
        for cp in cps:
            cp.wait()

    return pl.pallas_call(
        body, in_specs=[VM], out_specs=VM, out_shape=SDS((N_CHIPS,) + v.shape, v.dtype),
        scratch_shapes=[pltpu.SemaphoreType.DMA((3,)), pltpu.SemaphoreType.DMA((3,))],
        compiler_params=pltpu.CompilerParams(vmem_limit_bytes=VMEM_LIMIT_BYTES), name=name,
    )(v)


def small_allreduce(v, name):
    def body(v_ref, out_ref, sib_buf, slots, send_sems, recv_sems):
        x, y, c = _coords()
        me = 2 * x + y
        swap = _remote(v_ref, sib_buf, send_sems, recv_sems, 0, (x, y, 1 - c))
        swap.start()
        swap.wait()
        slots[me] = v_ref[...] + sib_buf[...]
        cps = []
        for j, m in enumerate(CHIP_FLIPS):
            cp = _remote(slots.at[me], slots.at[me], send_sems, recv_sems, 1 + j, (*_flip(x, y, m), c))
            cp.start()
            cps.append(cp)
        for cp in cps:
            cp.wait()
        out_ref[...] = (slots[0] + slots[1]) + (slots[2] + slots[3])

    return pl.pallas_call(
        body, in_specs=[VM], out_specs=VM, out_shape=SDS(v.shape, v.dtype),
        scratch_shapes=[pltpu.VMEM(v.shape, v.dtype), pltpu.VMEM((N_CHIPS,) + v.shape, v.dtype),
                        pltpu.SemaphoreType.DMA((4,)), pltpu.SemaphoreType.DMA((4,))],
        compiler_params=pltpu.CompilerParams(vmem_limit_bytes=VMEM_LIMIT_BYTES), name=name,
    )(v)


def adamw(w, g_list, m, v, name):
    nl, rows, cols = w.shape
    tr = _sum_rows_tile(rows, cols) if rows % 16 == 0 else rows
    bc1 = 1.0 - ADAM_B1 ** ADAM_STEP
    bc2 = 1.0 - ADAM_B2 ** ADAM_STEP

    def body(*refs):
        w_ref, m_ref, v_ref = refs[:3]
        g_refs = refs[3:3 + nl]
        go_ref, d_ref, mo_ref, vo_ref = refs[3 + nl:]
        layer = pl.program_id(0)
        for l in range(nl):
            @pl.when(layer == l)
            def _(l=l):
                g = g_refs[l][...]
                m_new = ADAM_B1 * m_ref[...] + (1.0 - ADAM_B1) * g
                v_new = ADAM_B2 * v_ref[...] + (1.0 - ADAM_B2) * (g * g)
                m_hat = m_new / bc1
                v_hat = v_new / bc2
                go_ref[...] = g
                d_ref[...] = -ADAM_LR * (m_hat / (jnp.sqrt(v_hat) + ADAM_EPS) + ADAM_WD * w_ref[...])
                mo_ref[...] = m_new
                vo_ref[...] = v_new

    stk = BS((None, tr, cols), lambda l, i: (l, i, 0))
    flat = BS((tr, cols), lambda l, i: (i, 0))
    out = SDS((nl, rows, cols), F32)
    return pl.pallas_call(
        body, grid=(nl, rows // tr), in_specs=[stk] * 3 + [flat] * nl, out_specs=[stk] * 4,
        out_shape=[out] * 4, compiler_params=_cp("parallel", "parallel"), name=name,
    )(w, m, v, *g_list)


WEIGHTS = ["mem_norm_g", "a_pre_mix_g", "a_post_mix_g", "a_pre_ffn_g", "a_post_ffn_g", "a_w_in", "a_conv_w",
           "a_conv_b", "a_gate_a_w", "a_gate_a_b", "a_gate_x_w", "a_gate_x_b", "a_lambda", "a_w_mem_kv", "a_w_out",
           "a_w_ffn_in", "a_w_ffn_out", "kv_norm_g", "w_kv_shared", "b_pre_mix_g", "b_post_mix_g", "b_pre_ffn_g",
           "b_post_ffn_g", "b_w_in", "b_w_mem_kv", "b_w_out", "b_w_ffn_in", "b_w_ffn_out"]
BIG = {"a_w_in": True, "a_w_mem_kv": False, "a_w_out": False, "a_w_ffn_in": True, "a_w_ffn_out": False,
       "w_kv_shared": True, "b_w_in": False, "b_w_mem_kv": False, "b_w_out": False, "b_w_ffn_in": True,
       "b_w_ffn_out": False}
SHARDED_SMALL = ["a_pre_mix_g", "a_post_mix_g", "a_pre_ffn_g", "a_post_ffn_g", "a_conv_w", "a_conv_b", "a_gate_a_b",
                 "a_gate_x_b", "a_lambda"]
REPL_SMALL = ["mem_norm_g", "kv_norm_g", "b_pre_mix_g", "b_post_mix_g", "b_pre_ffn_g", "b_post_ffn_g", "a_gate_a_w",
              "a_gate_x_w"]
LANES = 128


def _pack(arrs, row_multiple=8):
    flat = jnp.concatenate([a.reshape(-1) for a in arrs])
    pad = -flat.shape[0] % (LANES * row_multiple)
    if pad:
        flat = jnp.concatenate([flat, jnp.zeros((pad,), flat.dtype)])
    return flat.reshape(-1, LANES)


def _unpack(packed, shapes):
    flat = packed.reshape(-1)
    out, pos = [], 0
    for sh in shapes:
        size = math.prod(sh)
        out.append(flat[pos:pos + size].reshape(sh))
        pos += size
    return out


def kernel(x, mem, mem_norm_g, a_pre_mix_g, a_post_mix_g, a_pre_ffn_g, a_post_ffn_g, a_w_in, a_conv_w, a_conv_b,
           a_gate_a_w, a_gate_a_b, a_gate_x_w, a_gate_x_b, a_lambda, a_w_mem_kv, a_w_out, a_w_ffn_in, a_w_ffn_out,
           kv_norm_g, w_kv_shared, b_pre_mix_g, b_post_mix_g, b_pre_ffn_g, b_post_ffn_g, b_w_in, b_w_mem_kv, b_w_out,
           b_w_ffn_in, b_w_ffn_out, loss_target, m_mem_norm_g, m_a_pre_mix_g, m_a_post_mix_g, m_a_pre_ffn_g,
           m_a_post_ffn_g, m_a_w_in, m_a_conv_w, m_a_conv_b, m_a_gate_a_w, m_a_gate_a_b, m_a_gate_x_w, m_a_gate_x_b,
           m_a_lambda, m_a_w_mem_kv, m_a_w_out, m_a_w_ffn_in, m_a_w_ffn_out, m_kv_norm_g, m_w_kv_shared, m_b_pre_mix_g,
           m_b_post_mix_g, m_b_pre_ffn_g, m_b_post_ffn_g, m_b_w_in, m_b_w_mem_kv, m_b_w_out, m_b_w_ffn_in, m_b_w_ffn_out,
           v_mem_norm_g, v_a_pre_mix_g, v_a_post_mix_g, v_a_pre_ffn_g, v_a_post_ffn_g, v_a_w_in, v_a_conv_w, v_a_conv_b,
           v_a_gate_a_w, v_a_gate_a_b, v_a_gate_x_w, v_a_gate_x_b, v_a_lambda, v_a_w_mem_kv, v_a_w_out, v_a_w_ffn_in,
           v_a_w_ffn_out, v_kv_norm_g, v_w_kv_shared, v_b_pre_mix_g, v_b_post_mix_g, v_b_pre_ffn_g, v_b_post_ffn_g,
           v_b_w_in, v_b_w_mem_kv, v_b_w_out, v_b_w_ffn_in, v_b_w_ffn_out):
    a = dict(locals())
    xi, yi, ci = _coords()
    chip = 2 * xi + yi
    c_arr = jnp.reshape(ci, (1,)).astype(jnp.int32)

    got = small_gather(_pack([a[n] for n in SHARDED_SMALL]), "small_gather")
    per_chip = [_unpack(got[s], [a[n].shape for n in SHARDED_SMALL]) for s in range(N_CHIPS)]
    small = {n: jnp.concatenate([per_chip[s][k] for s in range(N_CHIPS)], axis=-1)
             for k, n in enumerate(SHARDED_SMALL)}
    small.update({n: a[n] for n in REPL_SMALL})

    def shard2d(n, j):
        w = a[n] if j is None else a[n][j]
        return w.astype(MXU)

    def gathered(n, g):
        return g if BIG[n] else g.reshape(-1, g.shape[-1])

    big = {n: [None, None] for n in BIG if n != "w_kv_shared"}
    layer_names = {"a": [n for n in BIG if n.startswith("a_")], "b": [n for n in BIG if n.startswith("b_")]}
    for p in ("a", "b"):
        for j in range(2):
            outs = gather_shards([shard2d(n, j) for n in layer_names[p]], "gather_" + p)
            for n, g in zip(layer_names[p], outs):
                big[n][j] = gathered(n, g)
    big["w_kv_shared"] = gather_shards([shard2d("w_kv_shared", None)], "gather_kv")[0]

    loss_parts, dx, gs, gb = _fwd_bwd(x[0], mem[0], loss_target[0], small, big)
    loss = lax.psum(jnp.sum(loss_parts) * (0.5 / D), ("x", "y", "c"))

    res = {}

    def as_slots(n, g):
        return g if BIG[n] else g.reshape(N_CHIPS, g.shape[0] // N_CHIPS, g.shape[1])

    reduced = {n: [None, None] for n in BIG if n != "w_kv_shared"}
    for p in ("a", "b"):
        for j in range(2):
            outs = reduce_scatter([as_slots(n, gb[n][j]) for n in layer_names[p]], c_arr, p)
            for n, g in zip(layer_names[p], outs):
                reduced[n][j] = g
    reduced["w_kv_shared"] = reduce_scatter([gb["w_kv_shared"]], c_arr, "kv")
    for n in BIG:
        shape = a[n].shape
        rows, cols = shape[-2], shape[-1]
        stk = (-1, rows, cols)
        outs = adamw(a[n].reshape(stk), reduced[n], a["m_" + n].reshape(stk), a["v_" + n].reshape(stk), "adamw")
        res[n] = [o.reshape(shape) for o in outs]

    def full(n):
        g = gs[n]
        return jnp.stack(g) if isinstance(g, list) else g

    order = SHARDED_SMALL + REPL_SMALL
    full_shapes = [full(n).shape for n in order]
    summed = _unpack(small_allreduce(_pack([full(n) for n in order]), "small_allreduce"), full_shapes)
    mine = []
    for n, g in zip(order, summed):
        if n in SHARDED_SMALL:
            width = a[n].shape[-1]
            g = lax.dynamic_slice_in_dim(g, chip * width, width, axis=g.ndim - 1)
        mine.append(g.reshape(a[n].shape))
    shapes = [a[n].shape for n in order]
    rm = 512
    outs = adamw(_pack([a[n] for n in order], rm)[None], [_pack(mine, rm)],
                 _pack([a["m_" + n] for n in order], rm)[None], _pack([a["v_" + n] for n in order], rm)[None],
                 "adamw_small")
    unpacked = [_unpack(o[0], shapes) for o in outs]
    for k, n in enumerate(order):
        res[n] = [u[k] for u in unpacked]

    return (loss, dx[None], *[res[n][0] for n in WEIGHTS], *[res[n][1] for n in WEIGHTS],
            *[res[n][2] for n in WEIGHTS], *[res[n][3] for n in WEIGHTS])
```

```python
import functools
import math

import jax
import jax.numpy as jnp
from jax import lax
from jax.experimental import pallas as pl
from jax.experimental.pallas import tpu as pltpu

D = 2048
HD = 128
MEM_W = 512
MEM_HEADS = 4
MIX_W = D - MEM_W
N_BLK = MIX_W // HD
D_FF = 5632
N_MEM = 256
RMS_EPS = 1e-6
NEG_INF = -1e30
LRU_C = 8.0
DIL_GROUPS = ((128, 1), (512, 4), (2048, 16))
Q_BLOCK = 128
SCALE = HD ** -0.5
N_CHIPS = 4

ADAM_LR = 0.001
ADAM_B1 = 0.9
ADAM_B2 = 0.999
ADAM_EPS = 1e-08
ADAM_WD = 0.01
ADAM_STEP = 10

MXU = jnp.bfloat16
F32 = jnp.float32
VMEM_LIMIT_BYTES = 56 * 1024 * 1024

BS = pl.BlockSpec
SDS = jax.ShapeDtypeStruct
MESH = pl.DeviceIdType.MESH


def _cp(*sem):
    return pltpu.CompilerParams(dimension_semantics=sem or None, vmem_limit_bytes=VMEM_LIMIT_BYTES)


def _dot(a, b, dn=((1,), (0,))):
    return lax.dot_general(a, b, (dn, ((), ())), preferred_element_type=F32)


def _div(i, n):
    return lax.div(i, jnp.int32(n))


def _rem(i, n):
    return lax.rem(i, jnp.int32(n))


NN = ((1,), (0,))
NT = ((1,), (1,))
TN = ((0,), (0,))


def _sigmoid(z):
    return 1.0 / (1.0 + jnp.exp(-z))


def _log1p_pos(u):
    return jnp.where(u < 1e-2, u * (1.0 - u * (0.5 - u * (1.0 / 3.0))), jnp.log(1.0 + u))


def _neg_expm1(z):
    return jnp.where(z > -1e-2, -z * (1.0 + z * (0.5 + z * (1.0 / 6.0))), 1.0 - jnp.exp(z))


def _softplus(z):
    return jnp.maximum(z, 0.0) + _log1p_pos(jnp.exp(-jnp.abs(z)))


_GELU_C = math.sqrt(2.0 / math.pi)


def _gelu_and_grad(x):
    x2 = x * x
    t = jnp.tanh(_GELU_C * (x + 0.044715 * x * x2))
    g = 0.5 * x * (1.0 + t)
    dg = 0.5 * (1.0 + t) + 0.5 * x * (1.0 - t * t) * _GELU_C * (1.0 + 3.0 * 0.044715 * x2)
    return g, dg


def _row_tile(rows):
    return min(256, rows)


def norm_cast(x, g, name):
    rows = x.shape[0]
    tr = _row_tile(rows)

    def body(x_ref, g_ref, o_ref):
        xv = x_ref[...]
        r = lax.rsqrt(jnp.mean(xv * xv, axis=-1, keepdims=True) + RMS_EPS)
        o_ref[...] = (xv * r * g_ref[...]).astype(o_ref.dtype)

    return pl.pallas_call(
        body, grid=(rows // tr,),
        in_specs=[BS((tr, D), lambda i: (i, 0)), BS((1, D), lambda i: (0, 0))],
        out_specs=BS((tr, D), lambda i: (i, 0)),
        out_shape=SDS((rows, D), MXU), compiler_params=_cp("parallel"), name=name,
    )(x, g.reshape(1, D))


def resid_norm(h, y, g, name):
    rows = h.shape[0]
    tr = _row_tile(rows)

    def body(h_ref, y_ref, g_ref, o_ref):
        yv = y_ref[...]
        r = lax.rsqrt(jnp.mean(yv * yv, axis=-1, keepdims=True) + RMS_EPS)
        o_ref[...] = h_ref[...] + yv * r * g_ref[...]

    return pl.pallas_call(
        body, grid=(rows // tr,),
        in_specs=[BS((tr, D), lambda i: (i, 0)), BS((tr, D), lambda i: (i, 0)), BS((1, D), lambda i: (0, 0))],
        out_specs=BS((tr, D), lambda i: (i, 0)),
        out_shape=SDS((rows, D), F32), compiler_params=_cp("parallel"), name=name,
    )(h, y, g.reshape(1, D))


def norm_bwd(x, g, dy, res, out_dtype, name):
    rows = x.shape[0]
    tr = _row_tile(rows)
    has_res = res is not None

    def body(*refs):
        if has_res:
            x_ref, g_ref, dy_ref, res_ref, dx_ref, dg_ref = refs
        else:
            x_ref, g_ref, dy_ref, dx_ref, dg_ref = refs
        xv = x_ref[...]
        dyv = dy_ref[...].astype(F32)
        r = lax.rsqrt(jnp.mean(xv * xv, axis=-1, keepdims=True) + RMS_EPS)
        xhat = xv * r
        dxhat = dyv * g_ref[...]
        dx = r * (dxhat - xhat * jnp.mean(dxhat * xhat, axis=-1, keepdims=True))
        if has_res:
            dx = dx + res_ref[...]
        dx_ref[...] = dx.astype(dx_ref.dtype)

        @pl.when(pl.program_id(0) == 0)
        def _():
            dg_ref[...] = jnp.zeros_like(dg_ref)

        dg_ref[...] += jnp.sum(dyv * xhat, axis=0, keepdims=True)

    row = BS((tr, D), lambda i: (i, 0))
    vec = BS((1, D), lambda i: (0, 0))
    ins = [x, g.reshape(1, D), dy] + ([res] if has_res else [])
    dx, dg = pl.pallas_call(
        body, grid=(rows // tr,),
        in_specs=[row, vec, row] + ([row] if has_res else []),
        out_specs=[row, vec],
        out_shape=[SDS((rows, D), out_dtype), SDS((1, D), F32)],
        compiler_params=_cp("arbitrary"), name=name,
    )(*ins)
    return dx, dg.reshape(D)


def loss_head(y, target, name):
    rows = y.shape[0]
    tr = _row_tile(rows)

    def body(y_ref, t_ref, dy_ref, acc_ref):
        err = y_ref[...] - t_ref[...]
        dy_ref[...] = err * (1.0 / D)

        @pl.when(pl.program_id(0) == 0)
        def _():
            acc_ref[...] = jnp.zeros_like(acc_ref)

        acc_ref[...] += jnp.sum(err * err, axis=0, keepdims=True)

    row = BS((tr, D), lambda i: (i, 0))
    dy, acc = pl.pallas_call(
        body, grid=(rows // tr,), in_specs=[row, row],
        out_specs=[row, BS((1, D), lambda i: (0, 0))],
        out_shape=[SDS((rows, D), F32), SDS((1, D), F32)],
        compiler_params=_cp("arbitrary"), name=name,
    )(y, target)
    return acc, dy


def _mm_call(ins, in_specs, pick, dn, grid, o_spec, out_sds, name):
    gk = grid[2]
    n_in = len(ins)

    def body(*refs):
        o_ref = refs[n_in]
        k = pl.program_id(2)

        def step(a_ref, b_ref):
            p = _dot(a_ref[...], b_ref[...], dn)
            if gk == 1:
                o_ref[...] = p.astype(o_ref.dtype)
            else:
                acc = o_ref if out_sds.dtype == F32 else refs[n_in + 1]

                @pl.when(k == 0)
                def _():
                    acc[...] = p

                @pl.when(k > 0)
                def _():
                    acc[...] += p

                if acc is not o_ref:
                    @pl.when(k == gk - 1)
                    def _():
                        o_ref[...] = acc[...].astype(o_ref.dtype)

        pick(refs[:n_in], k, step)

    scratch = []
    if gk > 1 and out_sds.dtype != F32:
        scratch = [pltpu.VMEM(o_spec.block_shape[-2:], F32)]
    return pl.pallas_call(
        body, grid=grid, in_specs=in_specs, out_specs=o_spec, out_shape=out_sds,
        scratch_shapes=scratch, compiler_params=_cp("parallel", "parallel", "arbitrary"), name=name,
    )(*ins)


def _pick2(refs, k, step):
    step(refs[0], refs[1])


def mm_nn(a, w, *, tm, tn, tk, out_dtype, name):
    m, kdim = a.shape
    if w.ndim == 3:
        c = w.shape[2]
        n = N_CHIPS * c
        per = c // tn
        b_spec = BS((None, tk, tn), lambda i, j, k: (_div(j, per), k, _rem(j, per)))
    else:
        n = w.shape[1]
        b_spec = BS((tk, tn), lambda i, j, k: (k, j))
    grid = (m // tm, n // tn, kdim // tk)
    return _mm_call([a, w], [BS((tm, tk), lambda i, j, k: (i, k)), b_spec], _pick2, NN, grid,
                    BS((tm, tn), lambda i, j, k: (i, j)), SDS((m, n), out_dtype), name)


def mm_nt(a_list, w, *, tm, tn, tk, out_dtype, name):
    m = a_list[0].shape[0]
    ka = a_list[0].shape[1]
    n_a = len(a_list)
    kdim = ka * n_a
    if w.ndim == 3:
        c = w.shape[2]
        n = w.shape[1]
        per = c // tk
        b_spec = BS((None, tn, tk), lambda i, j, k: (_div(k, per), j, _rem(k, per)))
    else:
        n = w.shape[0]
        b_spec = BS((tn, tk), lambda i, j, k: (j, k))
    gk = kdim // tk
    half = gk // n_a
    grid = (m // tm, n // tn, gk)
    if n_a == 1:
        a_specs = [BS((tm, tk), lambda i, j, k: (i, k))]
        pick = lambda refs, k, step: step(refs[0], refs[1])
    else:
        a_specs = [BS((tm, tk), lambda i, j, k: (i, jnp.minimum(k, half - 1))),
                   BS((tm, tk), lambda i, j, k: (i, jnp.maximum(k - half, 0)))]

        def pick(refs, k, step):
            @pl.when(k < half)
            def _():
                step(refs[0], refs[2])

            @pl.when(k >= half)
            def _():
                step(refs[1], refs[2])

    return _mm_call(list(a_list) + [w], a_specs + [b_spec], pick, NT, grid,
                    BS((tm, tn), lambda i, j, k: (i, j)), SDS((m, n), out_dtype), name)


def mm_tn(a, b_list, *, t1, tn, ts, col_shards, name):
    s, k1 = a.shape
    nb = b_list[0].shape[1]
    n_b = len(b_list)
    n = nb * n_b
    gn = n // tn
    half = gn // n_b
    grid = (k1 // t1, gn, s // ts)
    if col_shards:
        c = n // N_CHIPS
        per = c // tn
        o_spec = BS((None, t1, tn), lambda i, j, k: (_div(j, per), i, _rem(j, per)))
        out_sds = SDS((N_CHIPS, k1, c), F32)
    else:
        o_spec = BS((t1, tn), lambda i, j, k: (i, j))
        out_sds = SDS((k1, n), F32)
    a_spec = BS((ts, t1), lambda i, j, k: (k, i))
    if n_b == 1:
        b_specs = [BS((ts, tn), lambda i, j, k: (k, j))]
        pick = lambda refs, k, step: step(refs[0], refs[1])
    else:
        b_specs = [BS((ts, tn), lambda i, j, k: (k, jnp.minimum(j, half - 1))),
                   BS((ts, tn), lambda i, j, k: (k, jnp.maximum(j - half, 0)))]

        def pick(refs, k, step):
            j = pl.program_id(1)

            @pl.when(j < half)
            def _():
                step(refs[0], refs[1])

            @pl.when(j >= half)
            def _():
                step(refs[0], refs[2])

    return _mm_call([a] + list(b_list), [a_spec] + b_specs, pick, TN, grid, o_spec, out_sds, name)


def ffn_in_fwd(hn, w, name):
    s = hn.shape[0]
    tm = min(512, s)
    tn = D_FF // 4

    def body(a_ref, wg_ref, wu_ref, g_ref, u_ref, act_ref):
        a = a_ref[...]
        g = _dot(a, wg_ref[...])
        u = _dot(a, wu_ref[...])
        g_ref[...] = g
        u_ref[...] = u
        act_ref[...] = (g * _sigmoid(g) * u).astype(act_ref.dtype)

    tile = BS((tm, tn), lambda j, i: (i, j))
    return pl.pallas_call(
        body, grid=(4, s // tm),
        in_specs=[BS((tm, D), lambda j, i: (i, 0)),
                  BS((None, D, tn), lambda j, i: (_div(j, 2), 0, _rem(j, 2))),
                  BS((None, D, tn), lambda j, i: (2 + _div(j, 2), 0, _rem(j, 2)))],
        out_specs=[tile, tile, tile],
        out_shape=[SDS((s, D_FF), F32), SDS((s, D_FF), F32), SDS((s, D_FF), MXU)],
        compiler_params=_cp("parallel", "parallel"), name=name,
    )(hn, w, w)


def ffn_act_bwd(dy, w_out, g, u, name):
    s = dy.shape[0]
    tm = min(512, s)
    tn = D_FF // 4

    def body(dy_ref, w_ref, g_ref, u_ref, dg_ref, du_ref):
        dact = _dot(dy_ref[...], w_ref[...], NT)
        gv = g_ref[...]
        sg = _sigmoid(gv)
        dg_ref[...] = (dact * u_ref[...] * sg * (1.0 + gv * (1.0 - sg))).astype(dg_ref.dtype)
        du_ref[...] = (dact * gv * sg).astype(du_ref.dtype)

    tile = BS((tm, tn), lambda j, i: (i, j))
    return pl.pallas_call(
        body, grid=(4, s // tm),
        in_specs=[BS((tm, D), lambda j, i: (i, 0)), BS((tn, D), lambda j, i: (j, 0)), tile, tile],
        out_specs=[tile, tile],
        out_shape=[SDS((s, D_FF), MXU), SDS((s, D_FF), MXU)],
        compiler_params=_cp("parallel", "parallel"), name=name,
    )(dy, w_out, g, u)


LRU_T = 256
HALO = 8


def _shift_down(x, k, fill):
    rows = x.shape[0]
    idx = lax.broadcasted_iota(jnp.int32, x.shape, 0)
    return jnp.where(idx < k, fill, pltpu.roll(x, k, 0))


def _shift_up(x, k, fill):
    rows = x.shape[0]
    idx = lax.broadcasted_iota(jnp.int32, x.shape, 0)
    return jnp.where(idx >= rows - k, fill, pltpu.roll(x, rows - k, 0))


def _conv_taps(xcat):
    rows = xcat.shape[0]
    taps = []
    for k in range(4):
        off = HALO - 3 + k
        taps.append(xcat[off:off + LRU_T] if off == HALO else pltpu.roll(xcat, rows - off, 0)[:LRU_T])
    return taps


def _gates(xc, wa_ref, ba, wx_ref, bx, lam, za_ref, zx_ref):
    xm = xc.astype(MXU)
    for n in range(N_BLK):
        sl = slice(n * HD, (n + 1) * HD)
        za_ref[:, sl] = _dot(xm[:, sl], wa_ref[n])
        zx_ref[:, sl] = _dot(xm[:, sl], wx_ref[n])
    ra = _sigmoid(za_ref[...] + ba)
    ii = _sigmoid(zx_ref[...] + bx)
    sp = _softplus(-lam)
    log_a = -LRU_C * ra * sp
    a = jnp.exp(log_a)
    mult = jnp.sqrt(_neg_expm1(2.0 * log_a))
    return ra, ii, sp, a, mult


def lru_fwd(proj, conv_w, conv_b, wa, ba, wx, bx, lam, name):
    s = proj.shape[0]
    c = MIX_W
    nblk = s // LRU_T
    hpb = LRU_T // HALO

    def body(x_ref, halo_ref, cw_ref, cb_ref, wa_ref, ba_ref, wx_ref, bx_ref, lam_ref,
             xc_ref, h_ref, carry, za_ref, zx_ref):
        i = pl.program_id(0)

        @pl.when(i == 0)
        def _():
            carry[...] = jnp.zeros_like(carry)

        halo = jnp.where(i == 0, 0.0, halo_ref[...])
        xcat = jnp.concatenate([halo, x_ref[...]], axis=0)
        taps = _conv_taps(xcat)
        xc = cb_ref[...] + sum(cw_ref[k:k + 1, :] * taps[k] for k in range(4))
        xc_ref[...] = xc
        _, ii, _, a, mult = _gates(xc, wa_ref, ba_ref[...], wx_ref, bx_ref[...], lam_ref[...], za_ref, zx_ref)
        b = mult * (ii * xc)
        sh = 1
        while sh < LRU_T:
            b = a * _shift_down(b, sh, 0.0) + b
            a = a * _shift_down(a, sh, 1.0)
            sh *= 2
        h = b + a * carry[HALO - 1:HALO, :]
        h_ref[...] = h
        carry[...] = h[LRU_T - HALO:, :]

    def full(shape):
        return BS(shape, lambda i: (0,) * len(shape))

    blk = BS((LRU_T, c), lambda i: (i, 0))
    return pl.pallas_call(
        body, grid=(nblk,),
        in_specs=[blk, BS((HALO, c), lambda i: (jnp.maximum(i * hpb - 1, 0), 0)),
                  full((4, c)), full((1, c)), full((N_BLK, HD, HD)), full((1, c)),
                  full((N_BLK, HD, HD)), full((1, c)), full((1, c))],
        out_specs=[blk, blk],
        out_shape=[SDS((s, c), F32), SDS((s, c), F32)],
        scratch_shapes=[pltpu.VMEM((HALO, c), F32), pltpu.VMEM((LRU_T, c), F32), pltpu.VMEM((LRU_T, c), F32)],
        compiler_params=_cp("arbitrary"), name=name,
    )(proj, proj, conv_w, conv_b.reshape(1, c), wa.astype(MXU), ba.reshape(1, c), wx.astype(MXU),
      bx.reshape(1, c), lam.reshape(1, c))


def lru_mix_prep(h, proj, m, name):
    s = h.shape[0]
    tr = _row_tile(s)

    def body(h_ref, gb_ref, m_ref, o_ref):
        ge, _ = _gelu_and_grad(gb_ref[...])
        o_ref[:, :MIX_W] = (h_ref[...] * ge).astype(o_ref.dtype)
        o_ref[:, MIX_W:] = m_ref[...]

    return pl.pallas_call(
        body, grid=(s // tr,),
        in_specs=[BS((tr, MIX_W), lambda i: (i, 0)), BS((tr, MIX_W), lambda i: (i, 1)),
                  BS((tr, MEM_W), lambda i: (i, 0))],
        out_specs=BS((tr, D), lambda i: (i, 0)), out_shape=SDS((s, D), MXU),
        compiler_params=_cp("parallel"), name=name,
    )(h, proj, m)


def lru_bwd(dym, proj, xc, hl, dqm, conv_w, wa, ba, wx, bx, lam, name):
    s = proj.shape[0]
    c = MIX_W
    nblk = s // LRU_T
    hpb = LRU_T // HALO
    wa_m = wa.astype(MXU)
    wx_m = wx.astype(MXU)

    def body(dy_ref, x_ref, xhalo_ref, gb_ref, xc_ref, h_ref, hhalo_ref, dqm_ref,
             cw_ref, wa_ref, ba_ref, wx_ref, bx_ref, lam_ref,
             dproj_ref, dcw_ref, dcb_ref, dwa_ref, dba_ref, dwx_ref, dbx_ref, dlam_ref,
             g_next, a_next, dxc_next, za_ref, zx_ref, dxc_ref):
        i = pl.program_id(0)

        @pl.when(i == 0)
        def _():
            g_next[...] = jnp.zeros_like(g_next)
            a_next[...] = jnp.zeros_like(a_next)
            dxc_next[...] = jnp.zeros_like(dxc_next)
            for r in (dcw_ref, dcb_ref, dwa_ref, dba_ref, dwx_ref, dbx_ref, dlam_ref):
                r[...] = jnp.zeros_like(r)

        first = i == nblk - 1
        xc = xc_ref[...]
        lam = lam_ref[...]
        ra, ii, sp, a, mult = _gates(xc, wa_ref, ba_ref[...], wx_ref, bx_ref[...], lam, za_ref, zx_ref)
        hl_v = h_ref[...]
        ge, dge = _gelu_and_grad(gb_ref[...])
        dyl = dy_ref[...]
        dhl = dyl * ge
        dproj_ref[:, c:2 * c] = (dyl * hl_v * dge).astype(dproj_ref.dtype)
        dproj_ref[:, 2 * c:] = dqm_ref[...]

        an = _shift_up(a, 1, 0.0)
        last_row = lax.broadcasted_iota(jnp.int32, a.shape, 0) == LRU_T - 1
        an = jnp.where(last_row, a_next[0:1, :], an)
        gb_acc = dhl
        sh = 1
        while sh < LRU_T:
            gb_acc = an * _shift_up(gb_acc, sh, 0.0) + gb_acc
            an = an * _shift_up(an, sh, 1.0)
            sh *= 2
        g = gb_acc + an * g_next[0:1, :]
        g_next[...] = g[:HALO, :]
        a_next[...] = a[:HALO, :]

        hhalo = jnp.where(first, 0.0, hhalo_ref[...])
        h_prev = _shift_down(hl_v, 1, 0.0)
        first_row = lax.broadcasted_iota(jnp.int32, a.shape, 0) == 0
        h_prev = jnp.where(first_row, hhalo[HALO - 1:HALO, :], h_prev)
        da = g * h_prev
        ixc = ii * xc
        dmult = g * ixc
        dii = g * mult * xc
        dxc = g * mult * ii
        dlog_a = (da - dmult * a / mult) * a
        dra = dlog_a * (-LRU_C) * sp
        dlam_ref[...] += jnp.sum(dlog_a * ra, axis=0, keepdims=True) * (LRU_C * _sigmoid(-lam))
        dza = dra * ra * (1.0 - ra)
        dzx = dii * ii * (1.0 - ii)
        dba_ref[...] += jnp.sum(dza, axis=0, keepdims=True)
        dbx_ref[...] += jnp.sum(dzx, axis=0, keepdims=True)
        xm = xc.astype(MXU)
        dza_m = dza.astype(MXU)
        dzx_m = dzx.astype(MXU)
        for n in range(N_BLK):
            sl = slice(n * HD, (n + 1) * HD)
            dwa_ref[n] += _dot(xm[:, sl], dza_m[:, sl], TN)
            dwx_ref[n] += _dot(xm[:, sl], dzx_m[:, sl], TN)
            dxc_ref[:, sl] = _dot(dza_m[:, sl], wa_ref[n], NT) + _dot(dzx_m[:, sl], wx_ref[n], NT)
        dxc = dxc + dxc_ref[...]

        dcat = jnp.concatenate([dxc, dxc_next[...]], axis=0)
        rows = dcat.shape[0]
        dxb = cw_ref[3:4, :] * dxc
        for k in range(3):
            dxb = dxb + cw_ref[k:k + 1, :] * pltpu.roll(dcat, rows - (3 - k), 0)[:LRU_T]
        dproj_ref[:, :c] = dxb.astype(dproj_ref.dtype)
        dxc_next[...] = dxc[:HALO, :]

        xhalo = jnp.where(first, 0.0, xhalo_ref[...])
        taps = _conv_taps(jnp.concatenate([xhalo, x_ref[...]], axis=0))
        for k in range(4):
            dcw_ref[k:k + 1, :] += jnp.sum(dxc * taps[k], axis=0, keepdims=True)
        dcb_ref[...] += jnp.sum(dxc, axis=0, keepdims=True)

    def full(shape):
        return BS(shape, lambda i: (0,) * len(shape))

    def rev(i):
        return nblk - 1 - i

    blk0 = BS((LRU_T, c), lambda i: (rev(i), 0))
    blk1 = BS((LRU_T, c), lambda i: (rev(i), 1))
    halo = BS((HALO, c), lambda i: (jnp.maximum(rev(i) * hpb - 1, 0), 0))
    outs = pl.pallas_call(
        body, grid=(nblk,),
        in_specs=[blk0, blk0, halo, blk1, blk0, blk0, halo, BS((LRU_T, MEM_W), lambda i: (rev(i), 0)),
                  full((4, c)), full((N_BLK, HD, HD)), full((1, c)), full((N_BLK, HD, HD)), full((1, c)),
                  full((1, c))],
        out_specs=[BS((LRU_T, 2 * c + MEM_W), lambda i: (rev(i), 0)), full((4, c)), full((1, c)),
                   full((N_BLK, HD, HD)), full((1, c)), full((N_BLK, HD, HD)), full((1, c)), full((1, c))],
        out_shape=[SDS((s, 2 * c + MEM_W), MXU), SDS((4, c), F32), SDS((1, c), F32),
                   SDS((N_BLK, HD, HD), F32), SDS((1, c), F32), SDS((N_BLK, HD, HD), F32), SDS((1, c), F32),
                   SDS((1, c), F32)],
        scratch_shapes=[pltpu.VMEM((HALO, c), F32), pltpu.VMEM((HALO, c), F32), pltpu.VMEM((HALO, c), F32),
                        pltpu.VMEM((LRU_T, c), F32), pltpu.VMEM((LRU_T, c), F32), pltpu.VMEM((LRU_T, c), F32)],
        compiler_params=_cp("arbitrary"), name=name,
    )(dym, proj, proj, proj, xc, hl, hl, dqm, conv_w, wa_m, ba.reshape(1, c), wx_m, bx.reshape(1, c),
      lam.reshape(1, c))
    dproj, dcw, dcb, dwa, dba, dwx, dbx, dlam = outs
    return dproj, dcw, dcb.reshape(c), dwa, dba.reshape(c), dwx, dbx.reshape(c), dlam.reshape(c)


def _mem_probs(q, k):
    sc = _dot(q, k, NT) * SCALE
    e = jnp.exp(sc - jnp.max(sc, axis=-1, keepdims=True))
    return e / jnp.sum(e, axis=-1, keepdims=True)


def mem_attn_fwd(proj, q_col, kvm, name):
    s = proj.shape[0]
    tq = min(512, s)

    def body(q_ref, kv_ref, o_ref):
        q = q_ref[...].astype(MXU)
        for hh in range(MEM_HEADS):
            sl = slice(hh * HD, (hh + 1) * HD)
            p = _mem_probs(q[:, sl], kv_ref[:, sl])
            o_ref[:, sl] = _dot(p.astype(MXU), kv_ref[:, MEM_W + hh * HD:MEM_W + (hh + 1) * HD]).astype(o_ref.dtype)

    return pl.pallas_call(
        body, grid=(s // tq,),
        in_specs=[BS((tq, MEM_W), lambda i: (i, q_col)), BS((N_MEM, 2 * MEM_W), lambda i: (0, 0))],
        out_specs=BS((tq, MEM_W), lambda i: (i, 0)), out_shape=SDS((s, MEM_W), MXU),
        compiler_params=_cp("parallel"), name=name,
    )(proj, kvm)


def mem_attn_bwd(proj, q_col, kvm, dym, name):
    s = proj.shape[0]
    tq = min(512, s)

    def body(q_ref, kv_ref, do_ref, dq_ref, dkv_ref):
        @pl.when(pl.program_id(0) == 0)
        def _():
            dkv_ref[...] = jnp.zeros_like(dkv_ref)

        q = q_ref[...].astype(MXU)
        do = do_ref[...].astype(MXU)
        for hh in range(MEM_HEADS):
            sl = slice(hh * HD, (hh + 1) * HD)
            vsl = slice(MEM_W + hh * HD, MEM_W + (hh + 1) * HD)
            k = kv_ref[:, sl]
            p = _mem_probs(q[:, sl], k)
            dp = _dot(do[:, sl], kv_ref[:, vsl], NT)
            ds = (p * (dp - jnp.sum(p * dp, axis=-1, keepdims=True)) * SCALE).astype(MXU)
            dq_ref[:, sl] = _dot(ds, k).astype(dq_ref.dtype)
            dkv_ref[:, sl] += _dot(ds, q[:, sl], TN)
            dkv_ref[:, vsl] += _dot(p.astype(MXU), do[:, sl], TN)

    return pl.pallas_call(
        body, grid=(s // tq,),
        in_specs=[BS((tq, MEM_W), lambda i: (i, q_col)), BS((N_MEM, 2 * MEM_W), lambda i: (0, 0)),
                  BS((tq, MEM_W), lambda i: (i, MIX_W // MEM_W))],
        out_specs=[BS((tq, MEM_W), lambda i: (i, 0)), BS((N_MEM, 2 * MEM_W), lambda i: (0, 0))],
        out_shape=[SDS((s, MEM_W), MXU), SDS((N_MEM, 2 * MEM_W), F32)],
        compiler_params=_cp("arbitrary"), name=name,
    )(proj, kvm, dym)


def _dil_scores(q, kp, kc, n, slope_dil):
    qi = lax.broadcasted_iota(jnp.int32, (Q_BLOCK, Q_BLOCK), 0)
    ki = lax.broadcasted_iota(jnp.int32, (Q_BLOCK, Q_BLOCK), 1)
    rel_p = qi + Q_BLOCK - ki
    rel_c = qi - ki
    s_p = _dot(q, kp, NT) * SCALE - slope_dil * rel_p.astype(F32)
    s_c = _dot(q, kc, NT) * SCALE - slope_dil * rel_c.astype(F32)
    s_p = jnp.where((rel_p <= Q_BLOCK) & (n > 0), s_p, NEG_INF)
    s_c = jnp.where(rel_c >= 0, s_c, NEG_INF)
    return s_p, s_c


def _slope_dil(gi, hh):
    head = 4 * gi + hh
    return DIL_GROUPS[gi][1] * 2.0 ** (-8.0 * (head + 1.0) / N_BLK)


def dil_attn_fwd(proj, kv, gi, name):
    dil = DIL_GROUPS[gi][1]
    s, pw = proj.shape
    sub = s // dil
    nb = sub // Q_BLOCK
    qc, kc_ = pw // MEM_W, kv.shape[1] // MEM_W

    def body(q_ref, kp_ref, kc_ref, vp_ref, vc_ref, o_ref, lse_ref):
        n = pl.program_id(1)
        q = q_ref[...].astype(MXU)
        for hh in range(4):
            sl = slice(hh * HD, (hh + 1) * HD)
            s_p, s_c = _dil_scores(q[:, sl], kp_ref[:, sl], kc_ref[:, sl], n, _slope_dil(gi, hh))
            mx = jnp.maximum(jnp.max(s_p, axis=-1, keepdims=True), jnp.max(s_c, axis=-1, keepdims=True))
            den = jnp.sum(jnp.exp(s_p - mx), axis=-1, keepdims=True) + jnp.sum(jnp.exp(s_c - mx), axis=-1, keepdims=True)
            lse = mx + jnp.log(den)
            o_ref[:, sl] = (_dot(jnp.exp(s_p - lse).astype(MXU), vp_ref[:, sl])
                            + _dot(jnp.exp(s_c - lse).astype(MXU), vc_ref[:, sl]))
            lse_ref[:, sl] = jnp.broadcast_to(lse, (Q_BLOCK, HD))

    blk = (Q_BLOCK, MEM_W)
    prev = lambda n: jnp.maximum(n - 1, 0)
    out = BS(blk, lambda r, n: (n, r))
    return pl.pallas_call(
        body, grid=(dil, nb),
        in_specs=[BS(blk, lambda r, n: (n, r * qc + gi)),
                  BS(blk, lambda r, n: (prev(n), r * kc_ + gi)), BS(blk, lambda r, n: (n, r * kc_ + gi)),
                  BS(blk, lambda r, n: (prev(n), r * kc_ + 3 + gi)), BS(blk, lambda r, n: (n, r * kc_ + 3 + gi))],
        out_specs=[out, out],
        out_shape=[SDS((sub, dil * MEM_W), F32), SDS((sub, dil * MEM_W), F32)],
        compiler_params=_cp("parallel", "parallel"), name=name,
    )(proj.reshape(sub, dil * pw), *([kv.reshape(sub, dil * kv.shape[1])] * 4))


def dil_attn_bwd(proj, kv, lse, do, dd, gi, name):
    dil = DIL_GROUPS[gi][1]
    s, pw = proj.shape
    sub = s // dil
    nb = sub // Q_BLOCK
    qc, kc_ = pw // MEM_W, kv.shape[1] // MEM_W

    def body(q_ref, kp_ref, kc_ref, vp_ref, vc_ref, lse_ref, do_ref, dd_ref, dq_ref, dk_ref, dv_ref, ck, cv):
        n = pl.program_id(1)

        @pl.when(n == 0)
        def _():
            ck[...] = jnp.zeros_like(ck)
            cv[...] = jnp.zeros_like(cv)

        @pl.when(n < nb)
        def _():
            q = q_ref[...].astype(MXU)
            do_m = do_ref[...].astype(MXU)
            for hh in range(4):
                sl = slice(hh * HD, (hh + 1) * HD)
                s_p, s_c = _dil_scores(q[:, sl], kp_ref[:, sl], kc_ref[:, sl], n, _slope_dil(gi, hh))
                lse_h = lse_ref[:, sl]
                dd_h = dd_ref[:, sl]
                p_p = jnp.exp(s_p - lse_h)
                p_c = jnp.exp(s_c - lse_h)
                ds_p = (p_p * (_dot(do_m[:, sl], vp_ref[:, sl], NT) + dd_h) * SCALE).astype(MXU)
                ds_c = (p_c * (_dot(do_m[:, sl], vc_ref[:, sl], NT) + dd_h) * SCALE).astype(MXU)
                dq_ref[:, sl] = (_dot(ds_p, kp_ref[:, sl]) + _dot(ds_c, kc_ref[:, sl])).astype(dq_ref.dtype)
                dk_ref[:, sl] = ck[:, sl] + _dot(ds_p, q[:, sl], TN)
                dv_ref[:, sl] = cv[:, sl] + _dot(p_p.astype(MXU), do_m[:, sl], TN)
                ck[:, sl] = _dot(ds_c, q[:, sl], TN)
                cv[:, sl] = _dot(p_c.astype(MXU), do_m[:, sl], TN)

        @pl.when(n == nb)
        def _():
            dk_ref[...] = ck[...]
            dv_ref[...] = cv[...]

    blk = (Q_BLOCK, MEM_W)
    cur = lambda n: jnp.minimum(n, nb - 1)
    prev = lambda n: jnp.maximum(jnp.minimum(n, nb - 1) - 1, 0)
    done = lambda n: jnp.maximum(n - 1, 0)
    own = BS(blk, lambda r, n: (cur(n), r))
    kvv = kv.reshape(sub, dil * kv.shape[1])
    return pl.pallas_call(
        body, grid=(dil, nb + 1),
        in_specs=[BS(blk, lambda r, n: (cur(n), r * qc + gi)),
                  BS(blk, lambda r, n: (prev(n), r * kc_ + gi)), BS(blk, lambda r, n: (cur(n), r * kc_ + gi)),
                  BS(blk, lambda r, n: (prev(n), r * kc_ + 3 + gi)), BS(blk, lambda r, n: (cur(n), r * kc_ + 3 + gi)),
                  own, own, own],
        out_specs=[own, BS(blk, lambda r, n: (done(n), r)), BS(blk, lambda r, n: (done(n), r))],
        out_shape=[SDS((sub, dil * MEM_W), MXU), SDS((sub, dil * MEM_W), F32), SDS((sub, dil * MEM_W), F32)],
        scratch_shapes=[pltpu.VMEM(blk, F32), pltpu.VMEM(blk, F32)],
        compiler_params=_cp("parallel", "arbitrary"), name=name,
    )(proj.reshape(sub, dil * pw), kvv, kvv, kvv, kvv, lse, do, dd)


def _group_weights(lse_refs):
    l0, l1, l2 = (r[...] for r in lse_refs)
    mx = jnp.maximum(jnp.maximum(l0, l1), l2)
    e = [jnp.exp(l - mx) for l in (l0, l1, l2)]
    den = e[0] + e[1] + e[2]
    return [x / den for x in e]


def dil_mix_prep(o_list, lse_list, m, name):
    s = m.shape[0]
    tr = _row_tile(s)

    def body(o0, o1, o2, l0, l1, l2, m_ref, out_ref):
        w = _group_weights((l0, l1, l2))
        for g, o_ref in enumerate((o0, o1, o2)):
            out_ref[:, g * MEM_W:(g + 1) * MEM_W] = (o_ref[...] * w[g]).astype(out_ref.dtype)
        out_ref[:, MIX_W:] = m_ref[...]

    blk = BS((tr, MEM_W), lambda i: (i, 0))
    return pl.pallas_call(
        body, grid=(s // tr,), in_specs=[blk] * 7,
        out_specs=BS((tr, D), lambda i: (i, 0)), out_shape=SDS((s, D), MXU),
        compiler_params=_cp("parallel"), name=name,
    )(*o_list, *lse_list, m)


def dil_mix_bwd(dym, o_list, lse_list, name):
    s = dym.shape[0]
    tr = _row_tile(s)

    def body(da_ref, o0, o1, o2, l0, l1, l2, do0, do1, do2, dd0, dd1, dd2):
        w = _group_weights((l0, l1, l2))
        tot = None
        for g, (o_ref, do_ref) in enumerate(zip((o0, o1, o2), (do0, do1, do2))):
            da = da_ref[:, g * MEM_W:(g + 1) * MEM_W]
            do_ref[...] = da * w[g]
            x = da * o_ref[...]
            dw = jnp.concatenate(
                [jnp.broadcast_to(jnp.sum(x[:, hh * HD:(hh + 1) * HD], axis=-1, keepdims=True), (tr, HD))
                 for hh in range(4)], axis=1)
            tot = w[g] * dw if tot is None else tot + w[g] * dw
        for g, dd_ref in enumerate((dd0, dd1, dd2)):
            dd_ref[...] = -w[g] * tot

    blk = BS((tr, MEM_W), lambda i: (i, 0))
    outs = pl.pallas_call(
        body, grid=(s // tr,), in_specs=[BS((tr, MIX_W), lambda i: (i, 0))] + [blk] * 6,
        out_specs=[blk] * 6, out_shape=[SDS((s, MEM_W), F32)] * 6,
        compiler_params=_cp("parallel"), name=name,
    )(dym, *o_list, *lse_list)
    return outs[:3], outs[3:]


def sum_cast(parts, name):
    s = parts[0][0].shape[0]
    tr = _row_tile(s)
    flat = [a for p in parts for a in p]
    sizes = [len(p) for p in parts]

    def body(*refs):
        out_ref = refs[-1]
        pos = 0
        for j, n in enumerate(sizes):
            acc = refs[pos][...].astype(F32)
            for t in range(1, n):
                acc = acc + refs[pos + t][...].astype(F32)
            out_ref[:, j * MEM_W:(j + 1) * MEM_W] = acc.astype(out_ref.dtype)
            pos += n

    blk = BS((tr, MEM_W), lambda i: (i, 0))
    width = MEM_W * len(parts)
    return pl.pallas_call(
        body, grid=(s // tr,), in_specs=[blk] * len(flat),
        out_specs=BS((tr, width), lambda i: (i, 0)), out_shape=SDS((s, width), MXU),
        compiler_params=_cp("parallel"), name=name,
    )(*flat)


def add_n(arrs, name):
    rows, cols = arrs[0].shape
    tr = _row_tile(rows)

    def body(*refs):
        acc = refs[0][...]
        for r in refs[1:-1]:
            acc = acc + r[...]
        refs[-1][...] = acc

    blk = BS((tr, cols), lambda i: (i, 0))
    return pl.pallas_call(
        body, grid=(rows // tr,), in_specs=[blk] * len(arrs), out_specs=blk,
        out_shape=SDS((rows, cols), F32), compiler_params=_cp("parallel"), name=name,
    )(*arrs)


def _fwd_bwd(x, mem, target, small, big):
    s = x.shape[0]
    tm = min(1024, s)
    ts = min(512, s)
    gs, gb = {}, {}

    mem_n = norm_cast(mem, small["mem_norm_g"], "mem_norm")
    h = x
    saved = []
    kv = None
    for l in range(4):
        rec = l < 2
        p, j = ("a", l) if rec else ("b", l - 2)
        t = "rec" if rec else "dil"
        sv = {"h": h}
        kvm = mm_nn(mem_n, big[p + "_w_mem_kv"][j], tm=N_MEM, tn=2 * MEM_W, tk=D, out_dtype=MXU, name="mem_kv")
        hn = norm_cast(h, small[p + "_pre_mix_g"][j], "pre_norm")
        if rec:
            proj = mm_nn(hn, big["a_w_in"][j], tm=tm, tn=896, tk=D, out_dtype=F32, name="rec_in")
            xc, hl = lru_fwd(proj, small["a_conv_w"][j], small["a_conv_b"][j], small["a_gate_a_w"][j],
                             small["a_gate_a_b"][j], small["a_gate_x_w"][j], small["a_gate_x_b"][j],
                             small["a_lambda"][j], "lru_fwd")
            m = mem_attn_fwd(proj, 2 * MIX_W // MEM_W, kvm, "rec_mem_attn")
            ym = lru_mix_prep(hl, proj, m, "lru_mix_prep")
            sv.update(xc=xc, hl=hl)
        else:
            proj = mm_nn(hn, big["b_w_in"][j], tm=tm, tn=1024, tk=D, out_dtype=F32, name="dil_in")
            o_list, lse_list = [], []
            for gi in range(3):
                o, lse = dil_attn_fwd(proj, kv, gi, "dil_attn_fwd%d" % gi)
                o_list.append(o.reshape(s, MEM_W))
                lse_list.append(lse.reshape(s, MEM_W))
            m = mem_attn_fwd(proj, MIX_W // MEM_W, kvm, "dil_mem_attn")
            ym = dil_mix_prep(o_list, lse_list, m, "dil_mix_prep")
            sv.update(o=o_list, lse=lse_list)
        mix = mm_nn(ym, big[p + "_w_out"][j], tm=tm, tn=1024, tk=D, out_dtype=F32, name="mix_out")
        h1 = resid_norm(h, mix, small[p + "_post_mix_g"][j], "post_norm")
        hn2 = norm_cast(h1, small[p + "_pre_ffn_g"][j], "pre_norm")
        g, u, act = ffn_in_fwd(hn2, big[p + "_w_ffn_in"][j], "ffn_in")
        y2 = mm_nn(act, big[p + "_w_ffn_out"][j], tm=tm, tn=D, tk=D_FF // 4, out_dtype=F32, name="ffn_out")
        h = resid_norm(h1, y2, small[p + "_post_ffn_g"][j], "post_norm")
        sv.update(kvm=kvm, hn=hn, proj=proj, ym=ym, mix=mix, h1=h1, hn2=hn2, g=g, u=u, act=act, y2=y2)
        saved.append(sv)
        if l == 1:
            h_kv = h
            kvn = norm_cast(h, small["kv_norm_g"], "pre_norm")
            kv = mm_nn(kvn, big["w_kv_shared"], tm=tm, tn=768, tk=D, out_dtype=MXU, name="kv_proj")

    loss_parts, dh = loss_head(h, target, "loss_head")

    def stack2(name, j, val):
        gs.setdefault(name, [None, None])[j] = val

    def stack2b(name, j, val):
        gb.setdefault(name, [None, None])[j] = val

    dkv_parts = []
    dkvm = [None] * 4
    for l in (3, 2, 1, 0):
        rec = l < 2
        p, j = ("a", l) if rec else ("b", l - 2)
        sv = saved[l]
        if l == 1:
            dkv = sum_cast([(dkv_parts[0][c], dkv_parts[1][c]) for c in range(6)], "dkv_sum")
            dkvn = mm_nt([dkv], big["w_kv_shared"], tm=tm, tn=D, tk=768, out_dtype=F32, name="kv_proj_dx")
            gb["w_kv_shared"] = mm_tn(kvn, [dkv], t1=D, tn=768, ts=ts, col_shards=True, name="kv_proj_dw")
            dh, gs["kv_norm_g"] = norm_bwd(h_kv, small["kv_norm_g"], dkvn, dh, F32, "pre_norm_bwd")
        dy2, dg = norm_bwd(sv["y2"], small[p + "_post_ffn_g"][j], dh, None, MXU, "post_norm_bwd")
        stack2(p + "_post_ffn_g", j, dg)
        dgg, dgu = ffn_act_bwd(dy2, big[p + "_w_ffn_out"][j], sv["g"], sv["u"], "ffn_act_bwd")
        stack2b(p + "_w_ffn_out", j, mm_tn(sv["act"], [dy2], t1=D_FF // 4, tn=D, ts=ts, col_shards=False,
                                          name="ffn_out_dw"))
        dhn2 = mm_nt([dgg, dgu], big[p + "_w_ffn_in"][j], tm=tm, tn=D, tk=D_FF // 4, out_dtype=F32,
                     name="ffn_in_dx")
        stack2b(p + "_w_ffn_in", j, mm_tn(sv["hn2"], [dgg, dgu], t1=D, tn=D_FF // 4, ts=ts, col_shards=True,
                                         name="ffn_in_dw"))
        dh1, dg = norm_bwd(sv["h1"], small[p + "_pre_ffn_g"][j], dhn2, dh, F32, "pre_norm_bwd")
        stack2(p + "_pre_ffn_g", j, dg)
        dmix, dg = norm_bwd(sv["mix"], small[p + "_post_mix_g"][j], dh1, None, MXU, "post_norm_bwd")
        stack2(p + "_post_mix_g", j, dg)
        dym = mm_nt([dmix], big[p + "_w_out"][j], tm=tm, tn=1024, tk=D, out_dtype=F32, name="mix_out_dx")
        stack2b(p + "_w_out", j, mm_tn(sv["ym"], [dmix], t1=D, tn=1024, ts=ts, col_shards=False,
                                      name="mix_out_dw"))
        if rec:
            dqm, dkvm[l] = mem_attn_bwd(sv["proj"], 2 * MIX_W // MEM_W, sv["kvm"], dym, "rec_mem_attn_bwd")
            dproj, dcw, dcb, dwa, dba, dwx, dbx, dlam = lru_bwd(
                dym, sv["proj"], sv["xc"], sv["hl"], dqm, small["a_conv_w"][j], small["a_gate_a_w"][j],
                small["a_gate_a_b"][j], small["a_gate_x_w"][j], small["a_gate_x_b"][j], small["a_lambda"][j],
                "lru_bwd")
            for nm, val in (("a_conv_w", dcw), ("a_conv_b", dcb), ("a_gate_a_w", dwa), ("a_gate_a_b", dba),
                            ("a_gate_x_w", dwx), ("a_gate_x_b", dbx), ("a_lambda", dlam)):
                stack2(nm, j, val)
            dhn = mm_nt([dproj], big["a_w_in"][j], tm=tm, tn=D, tk=896, out_dtype=F32, name="rec_in_dx")
            stack2b("a_w_in", j, mm_tn(sv["hn"], [dproj], t1=D, tn=896, ts=ts, col_shards=True, name="rec_in_dw"))
        else:
            dqm, dkvm[l] = mem_attn_bwd(sv["proj"], MIX_W // MEM_W, sv["kvm"], dym, "dil_mem_attn_bwd")
            do_list, dd_list = dil_mix_bwd(dym, sv["o"], sv["lse"], "dil_mix_bwd")
            dq_list, dk_list, dv_list = [], [], []
            for gi in range(3):
                dil = DIL_GROUPS[gi][1]
                view = (s // dil, dil * MEM_W)
                dq, dk, dv = dil_attn_bwd(sv["proj"], kv, sv["lse"][gi].reshape(view), do_list[gi].reshape(view),
                                          dd_list[gi].reshape(view), gi, "dil_attn_bwd%d" % gi)
                dq_list.append(dq.reshape(s, MEM_W))
                dk_list.append(dk.reshape(s, MEM_W))
                dv_list.append(dv.reshape(s, MEM_W))
            dkv_parts.append(dk_list + dv_list)
            dproj = sum_cast([(a,) for a in dq_list + [dqm]], "dil_dproj")
            dhn = mm_nt([dproj], big["b_w_in"][j], tm=tm, tn=1024, tk=D, out_dtype=F32, name="dil_in_dx")
            stack2b("b_w_in", j, mm_tn(sv["hn"], [dproj], t1=D, tn=1024, ts=ts, col_shards=False, name="dil_in_dw"))
        dh, dg = norm_bwd(sv["h"], small[p + "_pre_mix_g"][j], dhn, dh1, F32, "pre_norm_bwd")
        stack2(p + "_pre_mix_g", j, dg)

    dmem_parts = []
    for l in range(4):
        p, j = ("a", l) if l < 2 else ("b", l - 2)
        dk_m = dkvm[l].astype(MXU)
        dmem_parts.append(mm_nt([dk_m], big[p + "_w_mem_kv"][j], tm=N_MEM, tn=D, tk=2 * MEM_W, out_dtype=F32,
                                name="mem_kv_dx"))
        stack2b(p + "_w_mem_kv", j, mm_tn(mem_n, [dk_m], t1=D, tn=2 * MEM_W, ts=N_MEM, col_shards=False,
                                         name="mem_kv_dw"))
    _, gs["mem_norm_g"] = norm_bwd(mem, small["mem_norm_g"], add_n(dmem_parts, "dmem_sum"), None, F32,
                                   "mem_norm_bwd")
    return loss_parts, dh, gs, gb


ANY = pl.BlockSpec(memory_space=pl.ANY)
CHIP_FLIPS = (1, 2, 3)


def _coords():
    return lax.axis_index("x"), lax.axis_index("y"), lax.axis_index("c")


def _flip(x, y, m):
    return x ^ (m >> 1), y ^ (m & 1)


def _remote(src, dst, send_sems, recv_sems, k, device):
    return pltpu.make_async_remote_copy(src_ref=src, dst_ref=dst, send_sem=send_sems.at[k], recv_sem=recv_sems.at[k],
                                        device_id=device, device_id_type=MESH)


def gather_shards(shards, name):
    n = len(shards)

    def body(*refs):
        ins, outs = refs[:n], refs[n:2 * n]
        send_sems, recv_sems = refs[2 * n:]
        x, y, c = _coords()
        me = 2 * x + y
        sib = (x, y, 1 - c)
        halves, sends = [], []
        for i in range(n):
            hr = shards[i].shape[0] // 2
            mine = pl.ds(pl.multiple_of(c * hr, 8), hr)
            other = pl.ds(pl.multiple_of((1 - c) * hr, 8), hr)
            halves.append((mine, other))
            own = _remote(ins[i], outs[i].at[me], send_sems, recv_sems, 7 * i + 6, sib)
            own.start()
            sends.append(own)
            for j, m in enumerate(CHIP_FLIPS):
                cp = _remote(ins[i].at[mine], outs[i].at[me, mine], send_sems, recv_sems, 7 * i + j,
                             (*_flip(x, y, m), c))
                cp.start()
                sends.append(cp)
        for i in range(n):
            mine, _ = halves[i]
            for j, m in enumerate(CHIP_FLIPS):
                slot = outs[i].at[me ^ m, mine]
                _remote(slot, slot, send_sems, recv_sems, 7 * i + j, sib).wait_recv()
                fwd = _remote(slot, slot, send_sems, recv_sems, 7 * i + 3 + j, sib)
                fwd.start()
                sends.append(fwd)
        for i in range(n):
            _, other = halves[i]
            for j, m in enumerate(CHIP_FLIPS):
                slot = outs[i].at[me ^ m, other]
                _remote(slot, slot, send_sems, recv_sems, 7 * i + 3 + j, sib).wait_recv()
            _remote(ins[i], outs[i].at[me], send_sems, recv_sems, 7 * i + 6, sib).wait_recv()
        for cp in sends:
            cp.wait_send()

    return pl.pallas_call(
        body, in_specs=[ANY] * n, out_specs=[ANY] * n,
        out_shape=[SDS((N_CHIPS,) + sh.shape, sh.dtype) for sh in shards],
        scratch_shapes=[pltpu.SemaphoreType.DMA((7 * n,)), pltpu.SemaphoreType.DMA((7 * n,))],
        name=name,
    )(*shards)


def swap_halves(grads, name):
    n = len(grads)

    def body(*refs):
        ins, outs = refs[:n], refs[n:2 * n]
        send_sems, recv_sems = refs[2 * n:]
        x, y, c = _coords()
        cps = []
        for i in range(n):
            hr = grads[i].shape[1] // 2
            other = pl.ds(pl.multiple_of((1 - c) * hr, 8), hr)
            cp = _remote(ins[i].at[pl.ds(0, N_CHIPS), other], outs[i], send_sems, recv_sems, i, (x, y, 1 - c))
            cp.start()
            cps.append(cp)
        for cp in cps:
            cp.wait()

    return pl.pallas_call(
        body, in_specs=[ANY] * n, out_specs=[ANY] * n,
        out_shape=[SDS((N_CHIPS, g.shape[1] // 2, g.shape[2]), g.dtype) for g in grads],
        scratch_shapes=[pltpu.SemaphoreType.DMA((n,)), pltpu.SemaphoreType.DMA((n,))],
        name=name,
    )(*grads)


def _sum_rows_tile(rows, cols):
    for tr in (512, 256, 128, 64, 32, 16):
        if rows % tr == 0 and tr * cols * 4 <= 2 * 1024 * 1024:
            return tr
    raise ValueError((rows, cols))


def half_sum(g, got, c_arr, name):
    _, r, cols = g.shape
    hr = r // 2
    tr = _sum_rows_tile(hr, cols)

    def body(c_ref, g_ref, got_ref, o_ref):
        o_ref[...] = (g_ref[...] + got_ref[...]).astype(o_ref.dtype)

    grid_spec = pltpu.PrefetchScalarGridSpec(
        num_scalar_prefetch=1, grid=(N_CHIPS, hr // tr),
        in_specs=[BS((None, None, tr, cols), lambda s, i, c_ref: (s, c_ref[0], i, 0)),
                  BS((None, tr, cols), lambda s, i, c_ref: (s, i, 0))],
        out_specs=BS((None, tr, cols), lambda s, i, c_ref: (s, i, 0)))
    return pl.pallas_call(
        body, grid_spec=grid_spec, out_shape=SDS((N_CHIPS, hr, cols), jnp.bfloat16),
        compiler_params=_cp("parallel", "parallel"), name=name,
    )(c_arr, g.reshape(N_CHIPS, 2, hr, cols), got)


def exchange_parts(parts, name):
    n = len(parts)

    def body(*refs):
        ins, outs = refs[:n], refs[n:2 * n]
        send_sems, recv_sems, loc_sems = refs[2 * n:]
        x, y, c = _coords()
        me = 2 * x + y
        cps, locs = [], []
        for i in range(n):
            loc = pltpu.make_async_copy(ins[i].at[me], outs[i].at[me], loc_sems.at[i])
            loc.start()
            locs.append(loc)
            for j, m in enumerate(CHIP_FLIPS):
                cp = _remote(ins[i].at[me ^ m], outs[i].at[me], send_sems, recv_sems, 3 * i + j, (*_flip(x, y, m), c))
                cp.start()
                cps.append(cp)
        for cp in cps:
            cp.wait()
        for loc in locs:
            loc.wait()

    return pl.pallas_call(
        body, in_specs=[ANY] * n, out_specs=[ANY] * n,
        out_shape=[SDS(p.shape, p.dtype) for p in parts],
        scratch_shapes=[pltpu.SemaphoreType.DMA((3 * n,)), pltpu.SemaphoreType.DMA((3 * n,)),
                        pltpu.SemaphoreType.DMA((n,))],
        name=name,
    )(*parts)


def slot_sum(slots, c_arr, name):
    _, hr, cols = slots.shape
    tr = _sum_rows_tile(hr, cols)
    nblk = hr // tr

    def body(c_ref, s_ref, o_ref):
        acc = s_ref[0].astype(F32)
        for p in range(1, N_CHIPS):
            acc = acc + s_ref[p].astype(F32)
        o_ref[...] = acc

    grid_spec = pltpu.PrefetchScalarGridSpec(
        num_scalar_prefetch=1, grid=(nblk,),
        in_specs=[BS((N_CHIPS, tr, cols), lambda i, c_ref: (0, i, 0))],
        out_specs=BS((tr, cols), lambda i, c_ref: (c_ref[0] * nblk + i, 0)))
    return pl.pallas_call(
        body, grid_spec=grid_spec, out_shape=SDS((2 * hr, cols), F32),
        compiler_params=_cp("parallel"), name=name,
    )(c_arr, slots)


def share_halves(bufs, name):
    n = len(bufs)

    def body(*refs):
        outs = refs[n:2 * n]
        send_sems, recv_sems = refs[2 * n:]
        x, y, c = _coords()
        cps = []
        for i in range(n):
            hr = bufs[i].shape[0] // 2
            mine = outs[i].at[pl.ds(pl.multiple_of(c * hr, 8), hr)]
            cp = _remote(mine, mine, send_sems, recv_sems, i, (x, y, 1 - c))
            cp.start()
            cps.append(cp)
        for cp in cps:
            cp.wait()

    return pl.pallas_call(
        body, in_specs=[ANY] * n, out_specs=[ANY] * n,
        out_shape=[SDS(b.shape, b.dtype) for b in bufs],
        input_output_aliases={i: i for i in range(n)},
        scratch_shapes=[pltpu.SemaphoreType.DMA((n,)), pltpu.SemaphoreType.DMA((n,))],
        name=name,
    )(*bufs)


def reduce_scatter(grads, c_arr, tag):
    got = swap_halves(grads, "rs_swap_" + tag)
    parts = [half_sum(g, r, c_arr, "rs_half_sum") for g, r in zip(grads, got)]
    slots = exchange_parts(parts, "rs_exchange_" + tag)
    return share_halves([slot_sum(s, c_arr, "rs_slot_sum") for s in slots], "rs_share_" + tag)


VM = pl.BlockSpec(memory_space=pltpu.VMEM)


def small_gather(v, name):
    def body(v_ref, out_ref, send_sems, recv_sems):
        x, y, c = _coords()
        me = 2 * x + y
        out_ref[me] = v_ref[...]
        cps = []
        for j, m in enumerate(CHIP_FLIPS):
            cp = _remote(v_ref, out_ref.at[me], send_sems, recv_sems, j, (*_flip(x, y, m), c))
            cp.start()
            cps.append(cp)
        for cp in cps:
            cp.wait()

    return pl.pallas_call(
        body, in_specs=[VM], out_specs=VM, out_shape=SDS((N_CHIPS,) + v.shape, v.dtype),
        scratch_shapes=[pltpu.SemaphoreType.DMA((3,)), pltpu.SemaphoreType.DMA((3,))],
        compiler_params=pltpu.CompilerParams(vmem_limit_bytes=VMEM_LIMIT_BYTES), name=name,
    )(v)


def small_allreduce(v, name):
    def body(v_ref, out_ref, sib_buf, slots, send_sems, recv_sems):
        x, y, c = _coords()
        me = 2 * x + y
        swap = _remote(v_ref, sib_buf, send_sems, recv_sems, 0, (x, y, 1 - c))
        swap.start()
        swap.wait()
        slots[me] = v_ref[...] + sib_buf[...]
        cps = []
        for j, m in enumerate(CHIP_FLIPS):
            cp = _remote(slots.at[me], slots.at[me], send_sems, recv_sems, 1 + j, (*_flip(x, y, m), c))
            cp.start()
            cps.append(cp)
        for cp in cps:
            cp.wait()
        out_ref[...] = (slots[0] + slots[1]) + (slots[2] + slots[3])

    return pl.pallas_call(
        body, in_specs=[VM], out_specs=VM, out_shape=SDS(v.shape, v.dtype),
        scratch_shapes=[pltpu.VMEM(v.shape, v.dtype), pltpu.VMEM((N_CHIPS,) + v.shape, v.dtype),
                        pltpu.SemaphoreType.DMA((4,)), pltpu.SemaphoreType.DMA((4,))],
        compiler_params=pltpu.CompilerParams(vmem_limit_bytes=VMEM_LIMIT_BYTES), name=name,
    )(v)


def adamw(w, g_list, m, v, name):
    nl, rows, cols = w.shape
    tr = _sum_rows_tile(rows, cols) if rows % 16 == 0 else rows
    bc1 = 1.0 - ADAM_B1 ** ADAM_STEP
    bc2 = 1.0 - ADAM_B2 ** ADAM_STEP

    def body(*refs):
        w_ref, m_ref, v_ref = refs[:3]
        g_refs = refs[3:3 + nl]
        go_ref, d_ref, mo_ref, vo_ref = refs[3 + nl:]
        layer = pl.program_id(0)
        for l in range(nl):
            @pl.when(layer == l)
            def _(l=l):
                g = g_refs[l][...]
                m_new = ADAM_B1 * m_ref[...] + (1.0 - ADAM_B1) * g
                v_new = ADAM_B2 * v_ref[...] + (1.0 - ADAM_B2) * (g * g)
                m_hat = m_new / bc1
                v_hat = v_new / bc2
                go_ref[...] = g
                d_ref[...] = -ADAM_LR * (m_hat / (jnp.sqrt(v_hat) + ADAM_EPS) + ADAM_WD * w_ref[...])
                mo_ref[...] = m_new
                vo_ref[...] = v_new

    stk = BS((None, tr, cols), lambda l, i: (l, i, 0))
    flat = BS((tr, cols), lambda l, i: (i, 0))
    out = SDS((nl, rows, cols), F32)
    return pl.pallas_call(
        body, grid=(nl, rows // tr), in_specs=[stk] * 3 + [flat] * nl, out_specs=[stk] * 4,
        out_shape=[out] * 4, compiler_params=_cp("parallel", "parallel"), name=name,
    )(w, m, v, *g_list)


WEIGHTS = ["mem_norm_g", "a_pre_mix_g", "a_post_mix_g", "a_pre_ffn_g", "a_post_ffn_g", "a_w_in", "a_conv_w",
           "a_conv_b", "a_gate_a_w", "a_gate_a_b", "a_gate_x_w", "a_gate_x_b", "a_lambda", "a_w_mem_kv", "a_w_out",
           "a_w_ffn_in", "a_w_ffn_out", "kv_norm_g", "w_kv_shared", "b_pre_mix_g", "b_post_mix_g", "b_pre_ffn_g",
           "b_post_ffn_g", "b_w_in", "b_w_mem_kv", "b_w_out", "b_w_ffn_in", "b_w_ffn_out"]
BIG = {"a_w_in": True, "a_w_mem_kv": False, "a_w_out": False, "a_w_ffn_in": True, "a_w_ffn_out": False,
       "w_kv_shared": True, "b_w_in": False, "b_w_mem_kv": False, "b_w_out": False, "b_w_ffn_in": True,
       "b_w_ffn_out": False}
SHARDED_SMALL = ["a_pre_mix_g", "a_post_mix_g", "a_pre_ffn_g", "a_post_ffn_g", "a_conv_w", "a_conv_b", "a_gate_a_b",
                 "a_gate_x_b", "a_lambda"]
REPL_SMALL = ["mem_norm_g", "kv_norm_g", "b_pre_mix_g", "b_post_mix_g", "b_pre_ffn_g", "b_post_ffn_g", "a_gate_a_w",
              "a_gate_x_w"]
LANES = 128


def _pack(arrs, row_multiple=8):
    flat = jnp.concatenate([a.reshape(-1) for a in arrs])
    pad = -flat.shape[0] % (LANES * row_multiple)
    if pad:
        flat = jnp.concatenate([flat, jnp.zeros((pad,), flat.dtype)])
    return flat.reshape(-1, LANES)


def _unpack(packed, shapes):
    flat = packed.reshape(-1)
    out, pos = [], 0
    for sh in shapes:
        size = math.prod(sh)
        out.append(flat[pos:pos + size].reshape(sh))
        pos += size
    return out


def kernel(x, mem, mem_norm_g, a_pre_mix_g, a_post_mix_g, a_pre_ffn_g, a_post_ffn_g, a_w_in, a_conv_w, a_conv_b,
           a_gate_a_w, a_gate_a_b, a_gate_x_w, a_gate_x_b, a_lambda, a_w_mem_kv, a_w_out, a_w_ffn_in, a_w_ffn_out,
           kv_norm_g, w_kv_shared, b_pre_mix_g, b_post_mix_g, b_pre_ffn_g, b_post_ffn_g, b_w_in, b_w_mem_kv, b_w_out,
           b_w_ffn_in, b_w_ffn_out, loss_target, m_mem_norm_g, m_a_pre_mix_g, m_a_post_mix_g, m_a_pre_ffn_g,
           m_a_post_ffn_g, m_a_w_in, m_a_conv_w, m_a_conv_b, m_a_gate_a_w, m_a_gate_a_b, m_a_gate_x_w, m_a_gate_x_b,
           m_a_lambda, m_a_w_mem_kv, m_a_w_out, m_a_w_ffn_in, m_a_w_ffn_out, m_kv_norm_g, m_w_kv_shared, m_b_pre_mix_g,
           m_b_post_mix_g, m_b_pre_ffn_g, m_b_post_ffn_g, m_b_w_in, m_b_w_mem_kv, m_b_w_out, m_b_w_ffn_in, m_b_w_ffn_out,
           v_mem_norm_g, v_a_pre_mix_g, v_a_post_mix_g, v_a_pre_ffn_g, v_a_post_ffn_g, v_a_w_in, v_a_conv_w, v_a_conv_b,
           v_a_gate_a_w, v_a_gate_a_b, v_a_gate_x_w, v_a_gate_x_b, v_a_lambda, v_a_w_mem_kv, v_a_w_out, v_a_w_ffn_in,
           v_a_w_ffn_out, v_kv_norm_g, v_w_kv_shared, v_b_pre_mix_g, v_b_post_mix_g, v_b_pre_ffn_g, v_b_post_ffn_g,
           v_b_w_in, v_b_w_mem_kv, v_b_w_out, v_b_w_ffn_in, v_b_w_ffn_out):
    a = dict(locals())
    xi, yi, ci = _coords()
    chip = 2 * xi + yi
    c_arr = jnp.reshape(ci, (1,)).astype(jnp.int32)

    got = small_gather(_pack([a[n] for n in SHARDED_SMALL]), "small_gather")
    per_chip = [_unpack(got[s], [a[n].shape for n in SHARDED_SMALL]) for s in range(N_CHIPS)]
    small = {n: jnp.concatenate([per_chip[s][k] for s in range(N_CHIPS)], axis=-1)
             for k, n in enumerate(SHARDED_SMALL)}
    small.update({n: a[n] for n in REPL_SMALL})

    def shard2d(n, j):
        w = a[n] if j is None else a[n][j]
        return w.astype(MXU)

    def gathered(n, g):
        return g if BIG[n] else g.reshape(-1, g.shape[-1])

    big = {n: [None, None] for n in BIG if n != "w_kv_shared"}
    layer_names = {"a": [n for n in BIG if n.startswith("a_")], "b": [n for n in BIG if n.startswith("b_")]}
    for p in ("a", "b"):
        for j in range(2):
            outs = gather_shards([shard2d(n, j) for n in layer_names[p]], "gather_" + p)
            for n, g in zip(layer_names[p], outs):
                big[n][j] = gathered(n, g)
    big["w_kv_shared"] = gather_shards([shard2d("w_kv_shared", None)], "gather_kv")[0]

    loss_parts, dx, gs, gb = _fwd_bwd(x[0], mem[0], loss_target[0], small, big)
    loss = lax.psum(jnp.sum(loss_parts) * (0.5 / D), ("x", "y", "c"))

    res = {}

    def as_slots(n, g):
        return g if BIG[n] else g.reshape(N_CHIPS, g.shape[0] // N_CHIPS, g.shape[1])

    reduced = {n: [None, None] for n in BIG if n != "w_kv_shared"}
    for p in ("a", "b"):
        for j in range(2):
            outs = reduce_scatter([as_slots(n, gb[n][j]) for n in layer_names[p]], c_arr, p)
            for n, g in zip(layer_names[p], outs):
                reduced[n][j] = g
    reduced["w_kv_shared"] = reduce_scatter([gb["w_kv_shared"]], c_arr, "kv")
    for n in BIG:
        shape = a[n].shape
        rows, cols = shape[-2], shape[-1]
        stk = (-1, rows, cols)
        outs = adamw(a[n].reshape(stk), reduced[n], a["m_" + n].reshape(stk), a["v_" + n].reshape(stk), "adamw")
        res[n] = [o.reshape(shape) for o in outs]

    def full(n):
        g = gs[n]
        return jnp.stack(g) if isinstance(g, list) else g

    order = SHARDED_SMALL + REPL_SMALL
    full_shapes = [full(n).shape for n in order]
    summed = _unpack(small_allreduce(_pack([full(n) for n in order]), "small_allreduce"), full_shapes)
    mine = []
    for n, g in zip(order, summed):
        if n in SHARDED_SMALL:
            width = a[n].shape[-1]
            g = lax.dynamic_slice_in_dim(g, chip * width, width, axis=g.ndim - 1)
        mine.append(g.reshape(a[n].shape))
    shapes = [a[n].shape for n in order]
    rm = 512
    outs = adamw(_pack([a[n] for n in order], rm)[None], [_pack(mine, rm)],
                 _pack([a["m_" + n] for n in order], rm)[None], _pack([a["v_" + n] for n in order], rm)[None],
                 "adamw_small")
    unpacked = [_unpack(o[0], shapes) for o in outs]
    for k, n in enumerate(order):
        res[n] = [u[k] for u in unpacked]

    return (loss, dx[None], *[res[n][0] for n in WEIGHTS], *[res[n][1] for n in WEIGHTS],
            *[res[n][2] for n in WEIGHTS], *[res[n][3] for n in WEIGHTS])
```

```python
import functools
import math

import jax
import jax.numpy as jnp
from jax import lax
from jax.experimental import pallas as pl
from jax.experimental.pallas import tpu as pltpu

D = 2048
HD = 128
MEM_W = 512
MEM_HEADS = 4
MIX_W = D - MEM_W
N_BLK = MIX_W // HD
D_FF = 5632
N_MEM = 256
RMS_EPS = 1e-6
NEG_INF = -1e30
LRU_C = 8.0
DIL_GROUPS = ((128, 1), (512, 4), (2048, 16))
Q_BLOCK = 128
SCALE = HD ** -0.5
N_CHIPS = 4

ADAM_LR = 0.001
ADAM_B1 = 0.9
ADAM_B2 = 0.999
ADAM_EPS = 1e-08
ADAM_WD = 0.01
ADAM_STEP = 10

MXU = jnp.bfloat16
F32 = jnp.float32
VMEM_LIMIT_BYTES = 56 * 1024 * 1024

BS = pl.BlockSpec
SDS = jax.ShapeDtypeStruct
MESH = pl.DeviceIdType.MESH


def _cp(*sem):
    return pltpu.CompilerParams(dimension_semantics=sem or None, vmem_limit_bytes=VMEM_LIMIT_BYTES)


def _dot(a, b, dn=((1,), (0,))):
    return lax.dot_general(a, b, (dn, ((), ())), preferred_element_type=F32)


def _div(i, n):
    return lax.div(i, jnp.int32(n))


def _rem(i, n):
    return lax.rem(i, jnp.int32(n))


NN = ((1,), (0,))
NT = ((1,), (1,))
TN = ((0,), (0,))


def _sigmoid(z):
    return 1.0 / (1.0 + jnp.exp(-z))


def _log1p_pos(u):
    return jnp.where(u < 1e-2, u * (1.0 - u * (0.5 - u * (1.0 / 3.0))), jnp.log(1.0 + u))


def _neg_expm1(z):
    return jnp.where(z > -1e-2, -z * (1.0 + z * (0.5 + z * (1.0 / 6.0))), 1.0 - jnp.exp(z))


def _softplus(z):
    return jnp.maximum(z, 0.0) + _log1p_pos(jnp.exp(-jnp.abs(z)))


_GELU_C = math.sqrt(2.0 / math.pi)


def _gelu_and_grad(x):
    x2 = x * x
    t = jnp.tanh(_GELU_C * (x + 0.044715 * x * x2))
    g = 0.5 * x * (1.0 + t)
    dg = 0.5 * (1.0 + t) + 0.5 * x * (1.0 - t * t) * _GELU_C * (1.0 + 3.0 * 0.044715 * x2)
    return g, dg


def _row_tile(rows):
    return min(256, rows)


def norm_cast(x, g, name):
    rows = x.shape[0]
    tr = _row_tile(rows)

    def body(x_ref, g_ref, o_ref):
        xv = x_ref[...]
        r = lax.rsqrt(jnp.mean(xv * xv, axis=-1, keepdims=True) + RMS_EPS)
        o_ref[...] = (xv * r * g_ref[...]).astype(o_ref.dtype)

    return pl.pallas_call(
        body, grid=(rows // tr,),
        in_specs=[BS((tr, D), lambda i: (i, 0)), BS((1, D), lambda i: (0, 0))],
        out_specs=BS((tr, D), lambda i: (i, 0)),
        out_shape=SDS((rows, D), MXU), compiler_params=_cp("parallel"), name=name,
    )(x, g.reshape(1, D))


def resid_norm(h, y, g, name):
    rows = h.shape[0]
    tr = _row_tile(rows)

    def body(h_ref, y_ref, g_ref, o_ref):
        yv = y_ref[...]
        r = lax.rsqrt(jnp.mean(yv * yv, axis=-1, keepdims=True) + RMS_EPS)
        o_ref[...] = h_ref[...] + yv * r * g_ref[...]

    return pl.pallas_call(
        body, grid=(rows // tr,),
        in_specs=[BS((tr, D), lambda i: (i, 0)), BS((tr, D), lambda i: (i, 0)), BS((1, D), lambda i: (0, 0))],
        out_specs=BS((tr, D), lambda i: (i, 0)),
        out_shape=SDS((rows, D), F32), compiler_params=_cp("parallel"), name=name,
    )(h, y, g.reshape(1, D))


def norm_bwd(x, g, dy, res, out_dtype, name):
    rows = x.shape[0]
    tr = _row_tile(rows)
    has_res = res is not None

    def body(*refs):
        if has_res:
            x_ref, g_ref, dy_ref, res_ref, dx_ref, dg_ref = refs
        else:
            x_ref, g_ref, dy_ref, dx_ref, dg_ref = refs
        xv = x_ref[...]
        dyv = dy_ref[...].astype(F32)
        r = lax.rsqrt(jnp.mean(xv * xv, axis=-1, keepdims=True) + RMS_EPS)
        xhat = xv * r
        dxhat = dyv * g_ref[...]
        dx = r * (dxhat - xhat * jnp.mean(dxhat * xhat, axis=-1, keepdims=True))
        if has_res:
            dx = dx + res_ref[...]
        dx_ref[...] = dx.astype(dx_ref.dtype)

        @pl.when(pl.program_id(0) == 0)
        def _():
            dg_ref[...] = jnp.zeros_like(dg_ref)

        dg_ref[...] += jnp.sum(dyv * xhat, axis=0, keepdims=True)

    row = BS((tr, D), lambda i: (i, 0))
    vec = BS((1, D), lambda i: (0, 0))
    ins = [x, g.reshape(1, D), dy] + ([res] if has_res else [])
    dx, dg = pl.pallas_call(
        body, grid=(rows // tr,),
        in_specs=[row, vec, row] + ([row] if has_res else []),
        out_specs=[row, vec],
        out_shape=[SDS((rows, D), out_dtype), SDS((1, D), F32)],
        compiler_params=_cp("arbitrary"), name=name,
    )(*ins)
    return dx, dg.reshape(D)


def loss_head(y, target, name):
    rows = y.shape[0]
    tr = _row_tile(rows)

    def body(y_ref, t_ref, dy_ref, acc_ref):
        err = y_ref[...] - t_ref[...]
        dy_ref[...] = err * (1.0 / D)

        @pl.when(pl.program_id(0) == 0)
        def _():
            acc_ref[...] = jnp.zeros_like(acc_ref)

        acc_ref[...] += jnp.sum(err * err, axis=0, keepdims=True)

    row = BS((tr, D), lambda i: (i, 0))
    dy, acc = pl.pallas_call(
        body, grid=(rows // tr,), in_specs=[row, row],
        out_specs=[row, BS((1, D), lambda i: (0, 0))],
        out_shape=[SDS((rows, D), F32), SDS((1, D), F32)],
        compiler_params=_cp("arbitrary"), name=name,
    )(y, target)
    return acc, dy


def _mm_call(ins, in_specs, pick, dn, grid, o_spec, out_sds, name):
    gk = grid[2]
    n_in = len(ins)

    def body(*refs):
        o_ref = refs[n_in]
        k = pl.program_id(2)

        def step(a_ref, b_ref):
            p = _dot(a_ref[...], b_ref[...], dn)
            if gk == 1:
                o_ref[...] = p.astype(o_ref.dtype)
            else:
                acc = o_ref if out_sds.dtype == F32 else refs[n_in + 1]

                @pl.when(k == 0)
                def _():
                    acc[...] = p

                @pl.when(k > 0)
                def _():
                    acc[...] += p

                if acc is not o_ref:
                    @pl.when(k == gk - 1)
                    def _():
                        o_ref[...] = acc[...].astype(o_ref.dtype)

        pick(refs[:n_in], k, step)

    scratch = []
    if gk > 1 and out_sds.dtype != F32:
        scratch = [pltpu.VMEM(o_spec.block_shape[-2:], F32)]
    return pl.pallas_call(
        body, grid=grid, in_specs=in_specs, out_specs=o_spec, out_shape=out_sds,
        scratch_shapes=scratch, compiler_params=_cp("parallel", "parallel", "arbitrary"), name=name,
    )(*ins)


def _pick2(refs, k, step):
    step(refs[0], refs[1])


def mm_nn(a, w, *, tm, tn, tk, out_dtype, name):
    m, kdim = a.shape
    if w.ndim == 3:
        c = w.shape[2]
        n = N_CHIPS * c
        per = c // tn
        b_spec = BS((None, tk, tn), lambda i, j, k: (_div(j, per), k, _rem(j, per)))
    else:
        n = w.shape[1]
        b_spec = BS((tk, tn), lambda i, j, k: (k, j))
    grid = (m // tm, n // tn, kdim // tk)
    return _mm_call([a, w], [BS((tm, tk), lambda i, j, k: (i, k)), b_spec], _pick2, NN, grid,
                    BS((tm, tn), lambda i, j, k: (i, j)), SDS((m, n), out_dtype), name)


def mm_nt(a_list, w, *, tm, tn, tk, out_dtype, name):
    m = a_list[0].shape[0]
    ka = a_list[0].shape[1]
    n_a = len(a_list)
    kdim = ka * n_a
    if w.ndim == 3:
        c = w.shape[2]
        n = w.shape[1]
        per = c // tk
        b_spec = BS((None, tn, tk), lambda i, j, k: (_div(k, per), j, _rem(k, per)))
    else:
        n = w.shape[0]
        b_spec = BS((tn, tk), lambda i, j, k: (j, k))
    gk = kdim // tk
    half = gk // n_a
    grid = (m // tm, n // tn, gk)
    if n_a == 1:
        a_specs = [BS((tm, tk), lambda i, j, k: (i, k))]
        pick = lambda refs, k, step: step(refs[0], refs[1])
    else:
        a_specs = [BS((tm, tk), lambda i, j, k: (i, jnp.minimum(k, half - 1))),
                   BS((tm, tk), lambda i, j, k: (i, jnp.maximum(k - half, 0)))]

        def pick(refs, k, step):
            @pl.when(k < half)
            def _():
                step(refs[0], refs[2])

            @pl.when(k >= half)
            def _():
                step(refs[1], refs[2])

    return _mm_call(list(a_list) + [w], a_specs + [b_spec], pick, NT, grid,
                    BS((tm, tn), lambda i, j, k: (i, j)), SDS((m, n), out_dtype), name)


def mm_tn(a, b_list, *, t1, tn, ts, col_shards, name):
    s, k1 = a.shape
    nb = b_list[0].shape[1]
    n_b = len(b_list)
    n = nb * n_b
    gn = n // tn
    half = gn // n_b
    grid = (k1 // t1, gn, s // ts)
    if col_shards:
        c = n // N_CHIPS
        per = c // tn
        o_spec = BS((None, t1, tn), lambda i, j, k: (_div(j, per), i, _rem(j, per)))
        out_sds = SDS((N_CHIPS, k1, c), F32)
    else:
        o_spec = BS((t1, tn), lambda i, j, k: (i, j))
        out_sds = SDS((k1, n), F32)
    a_spec = BS((ts, t1), lambda i, j, k: (k, i))
    if n_b == 1:
        b_specs = [BS((ts, tn), lambda i, j, k: (k, j))]
        pick = lambda refs, k, step: step(refs[0], refs[1])
    else:
        b_specs = [BS((ts, tn), lambda i, j, k: (k, jnp.minimum(j, half - 1))),
                   BS((ts, tn), lambda i, j, k: (k, jnp.maximum(j - half, 0)))]

        def pick(refs, k, step):
            j = pl.program_id(1)

            @pl.when(j < half)
            def _():
                step(refs[0], refs[1])

            @pl.when(j >= half)
            def _():
                step(refs[0], refs[2])

    return _mm_call([a] + list(b_list), [a_spec] + b_specs, pick, TN, grid, o_spec, out_sds, name)


def ffn_in_fwd(hn, w, name):
    s = hn.shape[0]
    tm = min(512, s)
    tn = D_FF // 4

    def body(a_ref, wg_ref, wu_ref, g_ref, u_ref, act_ref):
        a = a_ref[...]
        g = _dot(a, wg_ref[...])
        u = _dot(a, wu_ref[...])
        g_ref[...] = g
        u_ref[...] = u
        act_ref[...] = (g * _sigmoid(g) * u).astype(act_ref.dtype)

    tile = BS((tm, tn), lambda j, i: (i, j))
    return pl.pallas_call(
        body, grid=(4, s // tm),
        in_specs=[BS((tm, D), lambda j, i: (i, 0)),
                  BS((None, D, tn), lambda j, i: (_div(j, 2), 0, _rem(j, 2))),
                  BS((None, D, tn), lambda j, i: (2 + _div(j, 2), 0, _rem(j, 2)))],
        out_specs=[tile, tile, tile],
        out_shape=[SDS((s, D_FF), F32), SDS((s, D_FF), F32), SDS((s, D_FF), MXU)],
        compiler_params=_cp("parallel", "parallel"), name=name,
    )(hn, w, w)


def ffn_act_bwd(dy, w_out, g, u, name):
    s = dy.shape[0]
    tm = min(512, s)
    tn = D_FF // 4

    def body(dy_ref, w_ref, g_ref, u_ref, dg_ref, du_ref):
        dact = _dot(dy_ref[...], w_ref[...], NT)
        gv = g_ref[...]
        sg = _sigmoid(gv)
        dg_ref[...] = (dact * u_ref[...] * sg * (1.0 + gv * (1.0 - sg))).astype(dg_ref.dtype)
        du_ref[...] = (dact * gv * sg).astype(du_ref.dtype)

    tile = BS((tm, tn), lambda j, i: (i, j))
    return pl.pallas_call(
        body, grid=(4, s // tm),
        in_specs=[BS((tm, D), lambda j, i: (i, 0)), BS((tn, D), lambda j, i: (j, 0)), tile, tile],
        out_specs=[tile, tile],
        out_shape=[SDS((s, D_FF), MXU), SDS((s, D_FF), MXU)],
        compiler_params=_cp("parallel", "parallel"), name=name,
    )(dy, w_out, g, u)


LRU_T = 256
HALO = 8


def _shift_down(x, k, fill):
    rows = x.shape[0]
    idx = lax.broadcasted_iota(jnp.int32, x.shape, 0)
    return jnp.where(idx < k, fill, pltpu.roll(x, k, 0))


def _shift_up(x, k, fill):
    rows = x.shape[0]
    idx = lax.broadcasted_iota(jnp.int32, x.shape, 0)
    return jnp.where(idx >= rows - k, fill, pltpu.roll(x, rows - k, 0))


def _conv_taps(xcat):
    rows = xcat.shape[0]
    taps = []
    for k in range(4):
        off = HALO - 3 + k
        taps.append(xcat[off:off + LRU_T] if off == HALO else pltpu.roll(xcat, rows - off, 0)[:LRU_T])
    return taps


def _gates(xc, wa_ref, ba, wx_ref, bx, lam, za_ref, zx_ref):
    xm = xc.astype(MXU)
    for n in range(N_BLK):
        sl = slice(n * HD, (n + 1) * HD)
        za_ref[:, sl] = _dot(xm[:, sl], wa_ref[n])
        zx_ref[:, sl] = _dot(xm[:, sl], wx_ref[n])
    ra = _sigmoid(za_ref[...] + ba)
    ii = _sigmoid(zx_ref[...] + bx)
    sp = _softplus(-lam)
    log_a = -LRU_C * ra * sp
    a = jnp.exp(log_a)
    mult = jnp.sqrt(_neg_expm1(2.0 * log_a))
    return ra, ii, sp, a, mult


def lru_fwd(proj, conv_w, conv_b, wa, ba, wx, bx, lam, name):
    s = proj.shape[0]
    c = MIX_W
    nblk = s // LRU_T
    hpb = LRU_T // HALO

    def body(x_ref, halo_ref, cw_ref, cb_ref, wa_ref, ba_ref, wx_ref, bx_ref, lam_ref,
             xc_ref, h_ref, carry, za_ref, zx_ref):
        i = pl.program_id(0)

        @pl.when(i == 0)
        def _():
            carry[...] = jnp.zeros_like(carry)

        halo = jnp.where(i == 0, 0.0, halo_ref[...])
        xcat = jnp.concatenate([halo, x_ref[...]], axis=0)
        taps = _conv_taps(xcat)
        xc = cb_ref[...] + sum(cw_ref[k:k + 1, :] * taps[k] for k in range(4))
        xc_ref[...] = xc
        _, ii, _, a, mult = _gates(xc, wa_ref, ba_ref[...], wx_ref, bx_ref[...], lam_ref[...], za_ref, zx_ref)
        b = mult * (ii * xc)
        sh = 1
        while sh < LRU_T:
            b = a * _shift_down(b, sh, 0.0) + b
            a = a * _shift_down(a, sh, 1.0)
            sh *= 2
        h = b + a * carry[HALO - 1:HALO, :]
        h_ref[...] = h
        carry[...] = h[LRU_T - HALO:, :]

    def full(shape):
        return BS(shape, lambda i: (0,) * len(shape))

    blk = BS((LRU_T, c), lambda i: (i, 0))
    return pl.pallas_call(
        body, grid=(nblk,),
        in_specs=[blk, BS((HALO, c), lambda i: (jnp.maximum(i * hpb - 1, 0), 0)),
                  full((4, c)), full((1, c)), full((N_BLK, HD, HD)), full((1, c)),
                  full((N_BLK, HD, HD)), full((1, c)), full((1, c))],
        out_specs=[blk, blk],
        out_shape=[SDS((s, c), F32), SDS((s, c), F32)],
        scratch_shapes=[pltpu.VMEM((HALO, c), F32), pltpu.VMEM((LRU_T, c), F32), pltpu.VMEM((LRU_T, c), F32)],
        compiler_params=_cp("arbitrary"), name=name,
    )(proj, proj, conv_w, conv_b.reshape(1, c), wa.astype(MXU), ba.reshape(1, c), wx.astype(MXU),
      bx.reshape(1, c), lam.reshape(1, c))


def lru_mix_prep(h, proj, m, name):
    s = h.shape[0]
    tr = _row_tile(s)

    def body(h_ref, gb_ref, m_ref, o_ref):
        ge, _ = _gelu_and_grad(gb_ref[...])
        o_ref[:, :MIX_W] = (h_ref[...] * ge).astype(o_ref.dtype)
        o_ref[:, MIX_W:] = m_ref[...]

    return pl.pallas_call(
        body, grid=(s // tr,),
        in_specs=[BS((tr, MIX_W), lambda i: (i, 0)), BS((tr, MIX_W), lambda i: (i, 1)),
                  BS((tr, MEM_W), lambda i: (i, 0))],
        out_specs=BS((tr, D), lambda i: (i, 0)), out_shape=SDS((s, D), MXU),
        compiler_params=_cp("parallel"), name=name,
    )(h, proj, m)


def lru_bwd(dym, proj, xc, hl, dqm, conv_w, wa, ba, wx, bx, lam, name):
    s = proj.shape[0]
    c = MIX_W
    nblk = s // LRU_T
    hpb = LRU_T // HALO
    wa_m = wa.astype(MXU)
    wx_m = wx.astype(MXU)

    def body(dy_ref, x_ref, xhalo_ref, gb_ref, xc_ref, h_ref, hhalo_ref, dqm_ref,
             cw_ref, wa_ref, ba_ref, wx_ref, bx_ref, lam_ref,
             dproj_ref, dcw_ref, dcb_ref, dwa_ref, dba_ref, dwx_ref, dbx_ref, dlam_ref,
             g_next, a_next, dxc_next, za_ref, zx_ref, dxc_ref):
        i = pl.program_id(0)

        @pl.when(i == 0)
        def _():
            g_next[...] = jnp.zeros_like(g_next)
            a_next[...] = jnp.zeros_like(a_next)
            dxc_next[...] = jnp.zeros_like(dxc_next)
            for r in (dcw_ref, dcb_ref, dwa_ref, dba_ref, dwx_ref, dbx_ref, dlam_ref):
                r[...] = jnp.zeros_like(r)

        first = i == nblk - 1
        xc = xc_ref[...]
        lam = lam_ref[...]
        ra, ii, sp, a, mult = _gates(xc, wa_ref, ba_ref[...], wx_ref, bx_ref[...], lam, za_ref, zx_ref)
        hl_v = h_ref[...]
        ge, dge = _gelu_and_grad(gb_ref[...])
        dyl = dy_ref[...]
        dhl = dyl * ge
        dproj_ref[:, c:2 * c] = (dyl * hl_v * dge).astype(dproj_ref.dtype)
        dproj_ref[:, 2 * c:] = dqm_ref[...]

        an = _shift_up(a, 1, 0.0)
        last_row = lax.broadcasted_iota(jnp.int32, a.shape, 0) == LRU_T - 1
        an = jnp.where(last_row, a_next[0:1, :], an)
        gb_acc = dhl
        sh = 1
        while sh < LRU_T:
            gb_acc = an * _shift_up(gb_acc, sh, 0.0) + gb_acc
            an = an * _shift_up(an, sh, 1.0)
            sh *= 2
        g = gb_acc + an * g_next[0:1, :]
        g_next[...] = g[:HALO, :]
        a_next[...] = a[:HALO, :]

        hhalo = jnp.where(first, 0.0, hhalo_ref[...])
        h_prev = _shift_down(hl_v, 1, 0.0)
        first_row = lax.broadcasted_iota(jnp.int32, a.shape, 0) == 0
        h_prev = jnp.where(first_row, hhalo[HALO - 1:HALO, :], h_prev)
        da = g * h_prev
        ixc = ii * xc
        dmult = g * ixc
        dii = g * mult * xc
        dxc = g * mult * ii
        dlog_a = (da - dmult * a / mult) * a
        dra = dlog_a * (-LRU_C) * sp
        dlam_ref[...] += jnp.sum(dlog_a * ra, axis=0, keepdims=True) * (LRU_C * _sigmoid(-lam))
        dza = dra * ra * (1.0 - ra)
        dzx = dii * ii * (1.0 - ii)
        dba_ref[...] += jnp.sum(dza, axis=0, keepdims=True)
        dbx_ref[...] += jnp.sum(dzx, axis=0, keepdims=True)
        xm = xc.astype(MXU)
        dza_m = dza.astype(MXU)
        dzx_m = dzx.astype(MXU)
        for n in range(N_BLK):
            sl = slice(n * HD, (n + 1) * HD)
            dwa_ref[n] += _dot(xm[:, sl], dza_m[:, sl], TN)
            dwx_ref[n] += _dot(xm[:, sl], dzx_m[:, sl], TN)
            dxc_ref[:, sl] = _dot(dza_m[:, sl], wa_ref[n], NT) + _dot(dzx_m[:, sl], wx_ref[n], NT)
        dxc = dxc + dxc_ref[...]

        dcat = jnp.concatenate([dxc, dxc_next[...]], axis=0)
        rows = dcat.shape[0]
        dxb = cw_ref[3:4, :] * dxc
        for k in range(3):
            dxb = dxb + cw_ref[k:k + 1, :] * pltpu.roll(dcat, rows - (3 - k), 0)[:LRU_T]
        dproj_ref[:, :c] = dxb.astype(dproj_ref.dtype)
        dxc_next[...] = dxc[:HALO, :]

        xhalo = jnp.where(first, 0.0, xhalo_ref[...])
        taps = _conv_taps(jnp.concatenate([xhalo, x_ref[...]], axis=0))
        for k in range(4):
            dcw_ref[k:k + 1, :] += jnp.sum(dxc * taps[k], axis=0, keepdims=True)
        dcb_ref[...] += jnp.sum(dxc, axis=0, keepdims=True)

    def full(shape):
        return BS(shape, lambda i: (0,) * len(shape))

    def rev(i):
        return nblk - 1 - i

    blk0 = BS((LRU_T, c), lambda i: (rev(i), 0))
    blk1 = BS((LRU_T, c), lambda i: (rev(i), 1))
    halo = BS((HALO, c), lambda i: (jnp.maximum(rev(i) * hpb - 1, 0), 0))
    outs = pl.pallas_call(
        body, grid=(nblk,),
        in_specs=[blk0, blk0, halo, blk1, blk0, blk0, halo, BS((LRU_T, MEM_W), lambda i: (rev(i), 0)),
                  full((4, c)), full((N_BLK, HD, HD)), full((1, c)), full((N_BLK, HD, HD)), full((1, c)),
                  full((1, c))],
        out_specs=[BS((LRU_T, 2 * c + MEM_W), lambda i: (rev(i), 0)), full((4, c)), full((1, c)),
                   full((N_BLK, HD, HD)), full((1, c)), full((N_BLK, HD, HD)), full((1, c)), full((1, c))],
        out_shape=[SDS((s, 2 * c + MEM_W), MXU), SDS((4, c), F32), SDS((1, c), F32),
                   SDS((N_BLK, HD, HD), F32), SDS((1, c), F32), SDS((N_BLK, HD, HD), F32), SDS((1, c), F32),
                   SDS((1, c), F32)],
        scratch_shapes=[pltpu.VMEM((HALO, c), F32), pltpu.VMEM((HALO, c), F32), pltpu.VMEM((HALO, c), F32),
                        pltpu.VMEM((LRU_T, c), F32), pltpu.VMEM((LRU_T, c), F32), pltpu.VMEM((LRU_T, c), F32)],
        compiler_params=_cp("arbitrary"), name=name,
    )(dym, proj, proj, proj, xc, hl, hl, dqm, conv_w, wa_m, ba.reshape(1, c), wx_m, bx.reshape(1, c),
      lam.reshape(1, c))
    dproj, dcw, dcb, dwa, dba, dwx, dbx, dlam = outs
    return dproj, dcw, dcb.reshape(c), dwa, dba.reshape(c), dwx, dbx.reshape(c), dlam.reshape(c)


def _mem_probs(q, k):
    sc = _dot(q, k, NT) * SCALE
    e = jnp.exp(sc - jnp.max(sc, axis=-1, keepdims=True))
    return e / jnp.sum(e, axis=-1, keepdims=True)


def mem_attn_fwd(proj, q_col, kvm, name):
    s = proj.shape[0]
    tq = min(512, s)

    def body(q_ref, kv_ref, o_ref):
        q = q_ref[...].astype(MXU)
        for hh in range(MEM_HEADS):
            sl = slice(hh * HD, (hh + 1) * HD)
            p = _mem_probs(q[:, sl], kv_ref[:, sl])
            o_ref[:, sl] = _dot(p.astype(MXU), kv_ref[:, MEM_W + hh * HD:MEM_W + (hh + 1) * HD]).astype(o_ref.dtype)

    return pl.pallas_call(
        body, grid=(s // tq,),
        in_specs=[BS((tq, MEM_W), lambda i: (i, q_col)), BS((N_MEM, 2 * MEM_W), lambda i: (0, 0))],
        out_specs=BS((tq, MEM_W), lambda i: (i, 0)), out_shape=SDS((s, MEM_W), MXU),
        compiler_params=_cp("parallel"), name=name,
    )(proj, kvm)


def mem_attn_bwd(proj, q_col, kvm, dym, name):
    s = proj.shape[0]
    tq = min(512, s)

    def body(q_ref, kv_ref, do_ref, dq_ref, dkv_ref):
        @pl.when(pl.program_id(0) == 0)
        def _():
            dkv_ref[...] = jnp.zeros_like(dkv_ref)

        q = q_ref[...].astype(MXU)
        do = do_ref[...].astype(MXU)
        for hh in range(MEM_HEADS):
            sl = slice(hh * HD, (hh + 1) * HD)
            vsl = slice(MEM_W + hh * HD, MEM_W + (hh + 1) * HD)
            k = kv_ref[:, sl]
            p = _mem_probs(q[:, sl], k)
            dp = _dot(do[:, sl], kv_ref[:, vsl], NT)
            ds = (p * (dp - jnp.sum(p * dp, axis=-1, keepdims=True)) * SCALE).astype(MXU)
            dq_ref[:, sl] = _dot(ds, k).astype(dq_ref.dtype)
            dkv_ref[:, sl] += _dot(ds, q[:, sl], TN)
            dkv_ref[:, vsl] += _dot(p.astype(MXU), do[:, sl], TN)

    return pl.pallas_call(
        body, grid=(s // tq,),
        in_specs=[BS((tq, MEM_W), lambda i: (i, q_col)), BS((N_MEM, 2 * MEM_W), lambda i: (0, 0)),
                  BS((tq, MEM_W), lambda i: (i, MIX_W // MEM_W))],
        out_specs=[BS((tq, MEM_W), lambda i: (i, 0)), BS((N_MEM, 2 * MEM_W), lambda i: (0, 0))],
        out_shape=[SDS((s, MEM_W), MXU), SDS((N_MEM, 2 * MEM_W), F32)],
        compiler_params=_cp("arbitrary"), name=name,
    )(proj, kvm, dym)


def _dil_scores(q, kp, kc, n, slope_dil):
    qi = lax.broadcasted_iota(jnp.int32, (Q_BLOCK, Q_BLOCK), 0)
    ki = lax.broadcasted_iota(jnp.int32, (Q_BLOCK, Q_BLOCK), 1)
    rel_p = qi + Q_BLOCK - ki
    rel_c = qi - ki
    s_p = _dot(q, kp, NT) * SCALE - slope_dil * rel_p.astype(F32)
    s_c = _dot(q, kc, NT) * SCALE - slope_dil * rel_c.astype(F32)
    s_p = jnp.where((rel_p <= Q_BLOCK) & (n > 0), s_p, NEG_INF)
    s_c = jnp.where(rel_c >= 0, s_c, NEG_INF)
    return s_p, s_c


def _slope_dil(gi, hh):
    head = 4 * gi + hh
    return DIL_GROUPS[gi][1] * 2.0 ** (-8.0 * (head + 1.0) / N_BLK)


def dil_attn_fwd(proj, kv, gi, name):
    dil = DIL_GROUPS[gi][1]
    s, pw = proj.shape
    sub = s // dil
    nb = sub // Q_BLOCK
    qc, kc_ = pw // MEM_W, kv.shape[1] // MEM_W

    def body(q_ref, kp_ref, kc_ref, vp_ref, vc_ref, o_ref, lse_ref):
        n = pl.program_id(1)
        q = q_ref[...].astype(MXU)
        for hh in range(4):
            sl = slice(hh * HD, (hh + 1) * HD)
            s_p, s_c = _dil_scores(q[:, sl], kp_ref[:, sl], kc_ref[:, sl], n, _slope_dil(gi, hh))
            mx = jnp.maximum(jnp.max(s_p, axis=-1, keepdims=True), jnp.max(s_c, axis=-1, keepdims=True))
            den = jnp.sum(jnp.exp(s_p - mx), axis=-1, keepdims=True) + jnp.sum(jnp.exp(s_c - mx), axis=-1, keepdims=True)
            lse = mx + jnp.log(den)
            o_ref[:, sl] = (_dot(jnp.exp(s_p - lse).astype(MXU), vp_ref[:, sl])
                            + _dot(jnp.exp(s_c - lse).astype(MXU), vc_ref[:, sl]))
            lse_ref[:, sl] = jnp.broadcast_to(lse, (Q_BLOCK, HD))

    blk = (Q_BLOCK, MEM_W)
    prev = lambda n: jnp.maximum(n - 1, 0)
    out = BS(blk, lambda r, n: (n, r))
    return pl.pallas_call(
        body, grid=(dil, nb),
        in_specs=[BS(blk, lambda r, n: (n, r * qc + gi)),
                  BS(blk, lambda r, n: (prev(n), r * kc_ + gi)), BS(blk, lambda r, n: (n, r * kc_ + gi)),
                  BS(blk, lambda r, n: (prev(n), r * kc_ + 3 + gi)), BS(blk, lambda r, n: (n, r * kc_ + 3 + gi))],
        out_specs=[out, out],
        out_shape=[SDS((sub, dil * MEM_W), F32), SDS((sub, dil * MEM_W), F32)],
        compiler_params=_cp("parallel", "parallel"), name=name,
    )(proj.reshape(sub, dil * pw), *([kv.reshape(sub, dil * kv.shape[1])] * 4))


def dil_attn_bwd(proj, kv, lse, do, dd, gi, name):
    dil = DIL_GROUPS[gi][1]
    s, pw = proj.shape
    sub = s // dil
    nb = sub // Q_BLOCK
    qc, kc_ = pw // MEM_W, kv.shape[1] // MEM_W

    def body(q_ref, kp_ref, kc_ref, vp_ref, vc_ref, lse_ref, do_ref, dd_ref, dq_ref, dk_ref, dv_ref, ck, cv):
        n = pl.program_id(1)

        @pl.when(n == 0)
        def _():
            ck[...] = jnp.zeros_like(ck)
            cv[...] = jnp.zeros_like(cv)

        @pl.when(n < nb)
        def _():
            q = q_ref[...].astype(MXU)
            do_m = do_ref[...].astype(MXU)
            for hh in range(4):
                sl = slice(hh * HD, (hh + 1) * HD)
                s_p, s_c = _dil_scores(q[:, sl], kp_ref[:, sl], kc_ref[:, sl], n, _slope_dil(gi, hh))
                lse_h = lse_ref[:, sl]
                dd_h = dd_ref[:, sl]
                p_p = jnp.exp(s_p - lse_h)
                p_c = jnp.exp(s_c - lse_h)
                ds_p = (p_p * (_dot(do_m[:, sl], vp_ref[:, sl], NT) + dd_h) * SCALE).astype(MXU)
                ds_c = (p_c * (_dot(do_m[:, sl], vc_ref[:, sl], NT) + dd_h) * SCALE).astype(MXU)
                dq_ref[:, sl] = (_dot(ds_p, kp_ref[:, sl]) + _dot(ds_c, kc_ref[:, sl])).astype(dq_ref.dtype)
                dk_ref[:, sl] = ck[:, sl] + _dot(ds_p, q[:, sl], TN)
                dv_ref[:, sl] = cv[:, sl] + _dot(p_p.astype(MXU), do_m[:, sl], TN)
                ck[:, sl] = _dot(ds_c, q[:, sl], TN)
                cv[:, sl] = _dot(p_c.astype(MXU), do_m[:, sl], TN)

        @pl.when(n == nb)
        def _():
            dk_ref[...] = ck[...]
            dv_ref[...] = cv[...]

    blk = (Q_BLOCK, MEM_W)
    cur = lambda n: jnp.minimum(n, nb - 1)
    prev = lambda n: jnp.maximum(jnp.minimum(n, nb - 1) - 1, 0)
    done = lambda n: jnp.maximum(n - 1, 0)
    own = BS(blk, lambda r, n: (cur(n), r))
    kvv = kv.reshape(sub, dil * kv.shape[1])
    return pl.pallas_call(
        body, grid=(dil, nb + 1),
        in_specs=[BS(blk, lambda r, n: (cur(n), r * qc + gi)),
                  BS(blk, lambda r, n: (prev(n), r * kc_ + gi)), BS(blk, lambda r, n: (cur(n), r * kc_ + gi)),
                  BS(blk, lambda r, n: (prev(n), r * kc_ + 3 + gi)), BS(blk, lambda r, n: (cur(n), r * kc_ + 3 + gi)),
                  own, own, own],
        out_specs=[own, BS(blk, lambda r, n: (done(n), r)), BS(blk, lambda r, n: (done(n), r))],
        out_shape=[SDS((sub, dil * MEM_W), MXU), SDS((sub, dil * MEM_W), F32), SDS((sub, dil * MEM_W), F32)],
        scratch_shapes=[pltpu.VMEM(blk, F32), pltpu.VMEM(blk, F32)],
        compiler_params=_cp("parallel", "arbitrary"), name=name,
    )(proj.reshape(sub, dil * pw), kvv, kvv, kvv, kvv, lse, do, dd)


def _group_weights(lse_refs):
    l0, l1, l2 = (r[...] for r in lse_refs)
    mx = jnp.maximum(jnp.maximum(l0, l1), l2)
    e = [jnp.exp(l - mx) for l in (l0, l1, l2)]
    den = e[0] + e[1] + e[2]
    return [x / den for x in e]


def dil_mix_prep(o_list, lse_list, m, name):
    s = m.shape[0]
    tr = _row_tile(s)

    def body(o0, o1, o2, l0, l1, l2, m_ref, out_ref):
        w = _group_weights((l0, l1, l2))
        for g, o_ref in enumerate((o0, o1, o2)):
            out_ref[:, g * MEM_W:(g + 1) * MEM_W] = (o_ref[...] * w[g]).astype(out_ref.dtype)
        out_ref[:, MIX_W:] = m_ref[...]

    blk = BS((tr, MEM_W), lambda i: (i, 0))
    return pl.pallas_call(
        body, grid=(s // tr,), in_specs=[blk] * 7,
        out_specs=BS((tr, D), lambda i: (i, 0)), out_shape=SDS((s, D), MXU),
        compiler_params=_cp("parallel"), name=name,
    )(*o_list, *lse_list, m)


def dil_mix_bwd(dym, o_list, lse_list, name):
    s = dym.shape[0]
    tr = _row_tile(s)

    def body(da_ref, o0, o1, o2, l0, l1, l2, do0, do1, do2, dd0, dd1, dd2):
        w = _group_weights((l0, l1, l2))
        tot = None
        for g, (o_ref, do_ref) in enumerate(zip((o0, o1, o2), (do0, do1, do2))):
            da = da_ref[:, g * MEM_W:(g + 1) * MEM_W]
            do_ref[...] = da * w[g]
            x = da * o_ref[...]
            dw = jnp.concatenate(
                [jnp.broadcast_to(jnp.sum(x[:, hh * HD:(hh + 1) * HD], axis=-1, keepdims=True), (tr, HD))
                 for hh in range(4)], axis=1)
            tot = w[g] * dw if tot is None else tot + w[g] * dw
        for g, dd_ref in enumerate((dd0, dd1, dd2)):
            dd_ref[...] = -w[g] * tot

    blk = BS((tr, MEM_W), lambda i: (i, 0))
    outs = pl.pallas_call(
        body, grid=(s // tr,), in_specs=[BS((tr, MIX_W), lambda i: (i, 0))] + [blk] * 6,
        out_specs=[blk] * 6, out_shape=[SDS((s, MEM_W), F32)] * 6,
        compiler_params=_cp("parallel"), name=name,
    )(dym, *o_list, *lse_list)
    return outs[:3], outs[3:]


def sum_cast(parts, name):
    s = parts[0][0].shape[0]
    tr = _row_tile(s)
    flat = [a for p in parts for a in p]
    sizes = [len(p) for p in parts]

    def body(*refs):
        out_ref = refs[-1]
        pos = 0
        for j, n in enumerate(sizes):
            acc = refs[pos][...].astype(F32)
            for t in range(1, n):
                acc = acc + refs[pos + t][...].astype(F32)
            out_ref[:, j * MEM_W:(j + 1) * MEM_W] = acc.astype(out_ref.dtype)
            pos += n

    blk = BS((tr, MEM_W), lambda i: (i, 0))
    width = MEM_W * len(parts)
    return pl.pallas_call(
        body, grid=(s // tr,), in_specs=[blk] * len(flat),
        out_specs=BS((tr, width), lambda i: (i, 0)), out_shape=SDS((s, width), MXU),
        compiler_params=_cp("parallel"), name=name,
    )(*flat)


def add_n(arrs, name):
    rows, cols = arrs[0].shape
    tr = _row_tile(rows)

    def body(*refs):
        acc = refs[0][...]
        for r in refs[1:-1]:
            acc = acc + r[...]
        refs[-1][...] = acc

    blk = BS((tr, cols), lambda i: (i, 0))
    return pl.pallas_call(
        body, grid=(rows // tr,), in_specs=[blk] * len(arrs), out_specs=blk,
        out_shape=SDS((rows, cols), F32), compiler_params=_cp("parallel"), name=name,
    )(*arrs)


class _NoExchange:
    def hook(self, where, l, after):
        return None


def _fwd_bwd(x, mem, target, small, big, gs, gb, sched):
    s = x.shape[0]
    tm = min(1024, s)
    ts = min(512, s)

    def after_hook(arr, where, l, after):
        tok = sched.hook(where, l, after)
        return arr if tok is None else tie(arr, tok, "tie_%s_%d" % (where, l))

    h = x
    saved = []
    kv = None
    mem_n = None
    for l in range(4):
        rec = l < 2
        p, j = ("a", l) if rec else ("b", l - 2)
        sv = {"h": h}
        hn = norm_cast(h, small[p + "_pre_mix_g"][j], "pre_norm")
        hn = after_hook(hn, "fwd_begin", l, h)
        if mem_n is None:
            mem_n = norm_cast(mem, small["mem_norm_g"], "mem_norm")
        kvm = mm_nn(mem_n, big[p + "_w_mem_kv"][j], tm=N_MEM, tn=2 * MEM_W, tk=D, out_dtype=MXU, name="mem_kv")
        if rec:
            proj = mm_nn(hn, big["a_w_in"][j], tm=tm, tn=896, tk=D, out_dtype=F32, name="rec_in")
            xc, hl = lru_fwd(proj, small["a_conv_w"][j], small["a_conv_b"][j], small["a_gate_a_w"][j],
                             small["a_gate_a_b"][j], small["a_gate_x_w"][j], small["a_gate_x_b"][j],
                             small["a_lambda"][j], "lru_fwd")
            m = mem_attn_fwd(proj, 2 * MIX_W // MEM_W, kvm, "rec_mem_attn")
            ym = lru_mix_prep(hl, proj, m, "lru_mix_prep")
            sv.update(xc=xc, hl=hl)
        else:
            proj = mm_nn(hn, big["b_w_in"][j], tm=tm, tn=1024, tk=D, out_dtype=F32, name="dil_in")
            o_list, lse_list = [], []
            for gi in range(3):
                o, lse = dil_attn_fwd(proj, kv, gi, "dil_attn_fwd%d" % gi)
                o_list.append(o.reshape(s, MEM_W))
                lse_list.append(lse.reshape(s, MEM_W))
            m = mem_attn_fwd(proj, MIX_W // MEM_W, kvm, "dil_mem_attn")
            ym = dil_mix_prep(o_list, lse_list, m, "dil_mix_prep")
            sv.update(o=o_list, lse=lse_list)
        mix = mm_nn(ym, big[p + "_w_out"][j], tm=tm, tn=1024, tk=D, out_dtype=F32, name="mix_out")
        h1 = resid_norm(h, mix, small[p + "_post_mix_g"][j], "post_norm")
        hn2 = norm_cast(h1, small[p + "_pre_ffn_g"][j], "pre_norm")
        hn2 = after_hook(hn2, "fwd_mid", l, mix)
        g, u, act = ffn_in_fwd(hn2, big[p + "_w_ffn_in"][j], "ffn_in")
        y2 = mm_nn(act, big[p + "_w_ffn_out"][j], tm=tm, tn=D, tk=D_FF // 4, out_dtype=F32, name="ffn_out")
        h = resid_norm(h1, y2, small[p + "_post_ffn_g"][j], "post_norm")
        sched.hook("fwd_end", l, h)
        sv.update(kvm=kvm, hn=hn, proj=proj, ym=ym, mix=mix, h1=h1, hn2=hn2, g=g, u=u, act=act, y2=y2)
        saved.append(sv)
        if l == 1:
            h_kv = h
            kvn = norm_cast(h, small["kv_norm_g"], "pre_norm")
            kv = mm_nn(kvn, big["w_kv_shared"], tm=tm, tn=768, tk=D, out_dtype=MXU, name="kv_proj")

    loss_parts, dh = loss_head(h, target, "loss_head")

    def stack2(name, j, val):
        gs.setdefault(name, [None, None])[j] = val

    def stack2b(name, j, val):
        gb.setdefault(name, [None, None])[j] = val

    dkv_parts = []
    dmem_parts = []
    dkvm = [None] * 4
    for l in (3, 2, 1, 0):
        rec = l < 2
        p, j = ("a", l) if rec else ("b", l - 2)
        sv = saved[l]
        if l == 1:
            dkv = sum_cast([(dkv_parts[0][c], dkv_parts[1][c]) for c in range(6)], "dkv_sum")
            dkvn = mm_nt([dkv], big["w_kv_shared"], tm=tm, tn=D, tk=768, out_dtype=F32, name="kv_proj_dx")
            gb["w_kv_shared"] = mm_tn(kvn, [dkv], t1=D, tn=768, ts=ts, col_shards=True, name="kv_proj_dw")
            dh, gs["kv_norm_g"] = norm_bwd(h_kv, small["kv_norm_g"], dkvn, dh, F32, "pre_norm_bwd")
        dy2, dg = norm_bwd(sv["y2"], small[p + "_post_ffn_g"][j], dh, None, MXU, "post_norm_bwd")
        dy2 = after_hook(dy2, "bwd_begin", l, dh)
        stack2(p + "_post_ffn_g", j, dg)
        dgg, dgu = ffn_act_bwd(dy2, big[p + "_w_ffn_out"][j], sv["g"], sv["u"], "ffn_act_bwd")
        dgg = after_hook(dgg, "bwd_mid1", l, dgu)
        stack2b(p + "_w_ffn_out", j, mm_tn(sv["act"], [dy2], t1=D_FF // 4, tn=D, ts=ts, col_shards=False,
                                          name="ffn_out_dw"))
        dhn2 = mm_nt([dgg, dgu], big[p + "_w_ffn_in"][j], tm=tm, tn=D, tk=D_FF // 4, out_dtype=F32,
                     name="ffn_in_dx")
        stack2b(p + "_w_ffn_in", j, mm_tn(sv["hn2"], [dgg, dgu], t1=D, tn=D_FF // 4, ts=ts, col_shards=True,
                                         name="ffn_in_dw"))
        dhn2 = after_hook(dhn2, "bwd_mid2", l, gb[p + "_w_ffn_in"][j])
        dh1, dg = norm_bwd(sv["h1"], small[p + "_pre_ffn_g"][j], dhn2, dh, F32, "pre_norm_bwd")
        stack2(p + "_pre_ffn_g", j, dg)
        dmix, dg = norm_bwd(sv["mix"], small[p + "_post_mix_g"][j], dh1, None, MXU, "post_norm_bwd")
        stack2(p + "_post_mix_g", j, dg)
        dym = mm_nt([dmix], big[p + "_w_out"][j], tm=tm, tn=1024, tk=D, out_dtype=F32, name="mix_out_dx")
        stack2b(p + "_w_out", j, mm_tn(sv["ym"], [dmix], t1=D, tn=1024, ts=ts, col_shards=False,
                                      name="mix_out_dw"))
        if rec:
            dqm, dkvm[l] = mem_attn_bwd(sv["proj"], 2 * MIX_W // MEM_W, sv["kvm"], dym, "rec_mem_attn_bwd")
            dproj, dcw, dcb, dwa, dba, dwx, dbx, dlam = lru_bwd(
                dym, sv["proj"], sv["xc"], sv["hl"], dqm, small["a_conv_w"][j], small["a_gate_a_w"][j],
                small["a_gate_a_b"][j], small["a_gate_x_w"][j], small["a_gate_x_b"][j], small["a_lambda"][j],
                "lru_bwd")
            for nm, val in (("a_conv_w", dcw), ("a_conv_b", dcb), ("a_gate_a_w", dwa), ("a_gate_a_b", dba),
                            ("a_gate_x_w", dwx), ("a_gate_x_b", dbx), ("a_lambda", dlam)):
                stack2(nm, j, val)
            dhn = mm_nt([dproj], big["a_w_in"][j], tm=tm, tn=D, tk=896, out_dtype=F32, name="rec_in_dx")
            stack2b("a_w_in", j, mm_tn(sv["hn"], [dproj], t1=D, tn=896, ts=ts, col_shards=True, name="rec_in_dw"))
        else:
            dqm, dkvm[l] = mem_attn_bwd(sv["proj"], MIX_W // MEM_W, sv["kvm"], dym, "dil_mem_attn_bwd")
            do_list, dd_list = dil_mix_bwd(dym, sv["o"], sv["lse"], "dil_mix_bwd")
            dq_list, dk_list, dv_list = [], [], []
            for gi in range(3):
                dil = DIL_GROUPS[gi][1]
                view = (s // dil, dil * MEM_W)
                dq, dk, dv = dil_attn_bwd(sv["proj"], kv, sv["lse"][gi].reshape(view), do_list[gi].reshape(view),
                                          dd_list[gi].reshape(view), gi, "dil_attn_bwd%d" % gi)
                dq_list.append(dq.reshape(s, MEM_W))
                dk_list.append(dk.reshape(s, MEM_W))
                dv_list.append(dv.reshape(s, MEM_W))
            dkv_parts.append(dk_list + dv_list)
            dproj = sum_cast([(a,) for a in dq_list + [dqm]], "dil_dproj")
            dhn = mm_nt([dproj], big["b_w_in"][j], tm=tm, tn=1024, tk=D, out_dtype=F32, name="dil_in_dx")
            stack2b("b_w_in", j, mm_tn(sv["hn"], [dproj], t1=D, tn=1024, ts=ts, col_shards=False, name="dil_in_dw"))
        dk_m = dkvm[l].astype(MXU)
        dmem_parts.append(mm_nt([dk_m], big[p + "_w_mem_kv"][j], tm=N_MEM, tn=D, tk=2 * MEM_W, out_dtype=F32,
                                name="mem_kv_dx"))
        stack2b(p + "_w_mem_kv", j, mm_tn(mem_n, [dk_m], t1=D, tn=2 * MEM_W, ts=N_MEM, col_shards=False,
                                         name="mem_kv_dw"))
        dh, dg = norm_bwd(sv["h"], small[p + "_pre_mix_g"][j], dhn, dh1, F32, "pre_norm_bwd")
        stack2(p + "_pre_mix_g", j, dg)
        sched.hook("bwd_end", l, dh)

    _, gs["mem_norm_g"] = norm_bwd(mem, small["mem_norm_g"], add_n(dmem_parts, "dmem_sum"), None, F32,
                                   "mem_norm_bwd")
    return loss_parts, dh


ANY = pl.BlockSpec(memory_space=pl.ANY)
CHIP_FLIPS = (1, 2, 3)


def _coords():
    return lax.axis_index("x"), lax.axis_index("y"), lax.axis_index("c")


def _flip(x, y, m):
    return x ^ (m >> 1), y ^ (m & 1)


def _remote(src, dst, send_sems, recv_sems, k, device):
    return pltpu.make_async_remote_copy(src_ref=src, dst_ref=dst, send_sem=send_sems.at[k], recv_sem=recv_sems.at[k],
                                        device_id=device, device_id_type=MESH)


def gather_shards(shards, name):
    n = len(shards)

    def body(*refs):
        ins, outs = refs[:n], refs[n:2 * n]
        send_sems, recv_sems = refs[2 * n:]
        x, y, c = _coords()
        me = 2 * x + y
        sib = (x, y, 1 - c)
        halves, sends = [], []
        for i in range(n):
            hr = shards[i].shape[0] // 2
            mine = pl.ds(pl.multiple_of(c * hr, 8), hr)
            other = pl.ds(pl.multiple_of((1 - c) * hr, 8), hr)
            halves.append((mine, other))
            own = _remote(ins[i], outs[i].at[me], send_sems, recv_sems, 7 * i + 6, sib)
            own.start()
            sends.append(own)
            for j, m in enumerate(CHIP_FLIPS):
                cp = _remote(ins[i].at[mine], outs[i].at[me, mine], send_sems, recv_sems, 7 * i + j,
                             (*_flip(x, y, m), c))
                cp.start()
                sends.append(cp)
        for i in range(n):
            mine, _ = halves[i]
            for j, m in enumerate(CHIP_FLIPS):
                slot = outs[i].at[me ^ m, mine]
                _remote(slot, slot, send_sems, recv_sems, 7 * i + j, sib).wait_recv()
                fwd = _remote(slot, slot, send_sems, recv_sems, 7 * i + 3 + j, sib)
                fwd.start()
                sends.append(fwd)
        for i in range(n):
            _, other = halves[i]
            for j, m in enumerate(CHIP_FLIPS):
                slot = outs[i].at[me ^ m, other]
                _remote(slot, slot, send_sems, recv_sems, 7 * i + 3 + j, sib).wait_recv()
            _remote(ins[i], outs[i].at[me], send_sems, recv_sems, 7 * i + 6, sib).wait_recv()
        for cp in sends:
            cp.wait_send()

    return pl.pallas_call(
        body, in_specs=[ANY] * n, out_specs=[ANY] * n,
        out_shape=[SDS((N_CHIPS,) + sh.shape, sh.dtype) for sh in shards],
        scratch_shapes=[pltpu.SemaphoreType.DMA((7 * n,)), pltpu.SemaphoreType.DMA((7 * n,))],
        name=name,
    )(*shards)


def swap_halves(grads, name):
    n = len(grads)

    def body(*refs):
        ins, outs = refs[:n], refs[n:2 * n]
        send_sems, recv_sems = refs[2 * n:]
        x, y, c = _coords()
        cps = []
        for i in range(n):
            hr = grads[i].shape[1] // 2
            other = pl.ds(pl.multiple_of((1 - c) * hr, 8), hr)
            cp = _remote(ins[i].at[pl.ds(0, N_CHIPS), other], outs[i], send_sems, recv_sems, i, (x, y, 1 - c))
            cp.start()
            cps.append(cp)
        for cp in cps:
            cp.wait()

    return pl.pallas_call(
        body, in_specs=[ANY] * n, out_specs=[ANY] * n,
        out_shape=[SDS((N_CHIPS, g.shape[1] // 2, g.shape[2]), g.dtype) for g in grads],
        scratch_shapes=[pltpu.SemaphoreType.DMA((n,)), pltpu.SemaphoreType.DMA((n,))],
        name=name,
    )(*grads)


def _sum_rows_tile(rows, cols):
    for tr in (512, 256, 128, 64, 32, 16):
        if rows % tr == 0 and tr * cols * 4 <= 2 * 1024 * 1024:
            return tr
    raise ValueError((rows, cols))


def half_sum(g, got, c_arr, name):
    _, r, cols = g.shape
    hr = r // 2
    tr = _sum_rows_tile(hr, cols)

    def my_chip():
        return 2 * lax.axis_index("x") + lax.axis_index("y")

    def body(g_ref, got_ref, o_ref, own_ref):
        p = (g_ref[...] + got_ref[...]).astype(o_ref.dtype)
        o_ref[...] = p

        @pl.when(pl.program_id(1) == my_chip())
        def _():
            own_ref[...] = p

    out = SDS((N_CHIPS, hr, cols), jnp.bfloat16)
    return pl.pallas_call(
        body, grid=(hr // tr, N_CHIPS),
        in_specs=[BS((None, None, tr, cols), lambda i, s: (s, lax.axis_index("c"), i, 0)),
                  BS((None, tr, cols), lambda i, s: (s, i, 0))],
        out_specs=[BS((None, tr, cols), lambda i, s: (s, i, 0)),
                   BS((None, tr, cols), lambda i, s: (my_chip(), i, 0))],
        out_shape=[out, out], compiler_params=_cp("parallel", "arbitrary"), name=name,
    )(g.reshape(N_CHIPS, 2, hr, cols), got)


def exchange_parts(parts, name):
    n = len(parts)

    def body(*refs):
        ins, outs = refs[:n], refs[n:2 * n]
        send_sems, recv_sems, loc_sems = refs[2 * n:]
        x, y, c = _coords()
        me = 2 * x + y
        cps, locs = [], []
        for i in range(n):
            loc = pltpu.make_async_copy(ins[i].at[me], outs[i].at[me], loc_sems.at[i])
            loc.start()
            locs.append(loc)
            for j, m in enumerate(CHIP_FLIPS):
                cp = _remote(ins[i].at[me ^ m], outs[i].at[me], send_sems, recv_sems, 3 * i + j, (*_flip(x, y, m), c))
                cp.start()
                cps.append(cp)
        for cp in cps:
            cp.wait()
        for loc in locs:
            loc.wait()

    return pl.pallas_call(
        body, in_specs=[ANY] * n, out_specs=[ANY] * n,
        out_shape=[SDS(p.shape, p.dtype) for p in parts],
        scratch_shapes=[pltpu.SemaphoreType.DMA((3 * n,)), pltpu.SemaphoreType.DMA((3 * n,)),
                        pltpu.SemaphoreType.DMA((n,))],
        name=name,
    )(*parts)


def slot_sum(slots, c_arr, name):
    _, hr, cols = slots.shape
    tr = _sum_rows_tile(hr, cols)
    nblk = hr // tr

    def body(s_ref, o_ref):
        acc = s_ref[0].astype(F32)
        for p in range(1, N_CHIPS):
            acc = acc + s_ref[p].astype(F32)
        o_ref[...] = acc

    return pl.pallas_call(
        body, grid=(nblk,), in_specs=[BS((N_CHIPS, tr, cols), lambda i: (0, i, 0))],
        out_specs=BS((tr, cols), lambda i: (lax.axis_index("c") * nblk + i, 0)),
        out_shape=SDS((2 * hr, cols), F32), compiler_params=_cp("parallel"), name=name,
    )(slots)


def share_halves(bufs, name):
    n = len(bufs)

    def body(*refs):
        outs = refs[n:2 * n]
        send_sems, recv_sems = refs[2 * n:]
        x, y, c = _coords()
        cps = []
        for i in range(n):
            hr = bufs[i].shape[0] // 2
            mine = outs[i].at[pl.ds(pl.multiple_of(c * hr, 8), hr)]
            cp = _remote(mine, mine, send_sems, recv_sems, i, (x, y, 1 - c))
            cp.start()
            cps.append(cp)
        for cp in cps:
            cp.wait()

    return pl.pallas_call(
        body, in_specs=[ANY] * n, out_specs=[ANY] * n,
        out_shape=[SDS(b.shape, b.dtype) for b in bufs],
        input_output_aliases={i: i for i in range(n)},
        scratch_shapes=[pltpu.SemaphoreType.DMA((n,)), pltpu.SemaphoreType.DMA((n,))],
        name=name,
    )(*bufs)


HBM_SPEC = pl.BlockSpec(memory_space=pltpu.HBM)
SEM_SPEC = pl.BlockSpec(memory_space=pltpu.SEMAPHORE)
EFFECT = pltpu.SideEffectType.DATAFLOW_SIDE_EFFECTING


def split_start(name, bufs, plan, n_copies):
    nb = len(bufs)

    def body(*refs):
        send_sems, recv_sems = refs[nb], refs[nb + 1]
        for k, (src, dst, dev) in enumerate(plan(refs[:nb])):
            _remote(src, dst, send_sems, recv_sems, k, dev).start()
        refs[-1][...] = jnp.zeros_like(refs[-1])

    outs = pl.pallas_call(
        body, name=name,
        out_shape=(pltpu.SemaphoreType.DMA((n_copies,)), pltpu.SemaphoreType.DMA((n_copies,)),
                   *[pltpu.HBM(b.shape, b.dtype) for b in bufs], SDS((8, LANES), F32)),
        in_specs=[HBM_SPEC] * nb, out_specs=(SEM_SPEC, SEM_SPEC, *[HBM_SPEC] * nb, VM),
        input_output_aliases={i: 2 + i for i in range(nb)},
        compiler_params=pltpu.CompilerParams(has_side_effects=EFFECT),
    )(*[pltpu.with_memory_space_constraint(b, pltpu.HBM) for b in bufs])
    return outs[0], outs[1], list(outs[2:2 + nb]), outs[-1]


def split_wait(name, send_sems, recv_sems, bufs, after, plan):
    nb = len(bufs)

    def body(*refs):
        send_ref, recv_ref = refs[nb], refs[nb + 1]
        for k, (src, dst, dev) in enumerate(plan(refs[:nb])):
            cp = _remote(src, dst, send_ref, recv_ref, k, dev)
            cp.wait_send()
            cp.wait_recv()

    outs = pl.pallas_call(
        body, name=name, out_shape=[pltpu.HBM(b.shape, b.dtype) for b in bufs],
        in_specs=[HBM_SPEC] * nb + [SEM_SPEC, SEM_SPEC, ANY], out_specs=[HBM_SPEC] * nb,
        input_output_aliases={i: i for i in range(nb)},
        compiler_params=pltpu.CompilerParams(has_side_effects=EFFECT),
    )(*bufs, send_sems, recv_sems, after)
    return list(outs)


def tie(x, token, name):
    def body(x_ref, t_ref, o_ref):
        pass

    return pl.pallas_call(
        body, name=name, out_shape=SDS(x.shape, x.dtype), in_specs=[ANY, ANY], out_specs=ANY,
        input_output_aliases={0: 0},
    )(x, token)


def plan_gather_ici(n, rows):
    def plan(refs):
        x, y, c = _coords()
        me = 2 * x + y
        out = []
        for i in range(n):
            hr = rows[i] // 2
            mine = pl.ds(pl.multiple_of(c * hr, 8), hr)
            out.append((refs[i], refs[n + i].at[me], (x, y, 1 - c)))
            for m in CHIP_FLIPS:
                out.append((refs[i].at[mine], refs[n + i].at[me, mine], (*_flip(x, y, m), c)))
        return out
    return plan


def plan_gather_d2d(n, rows):
    def plan(refs):
        x, y, c = _coords()
        me = 2 * x + y
        out = []
        for i in range(n):
            hr = rows[i] // 2
            mine = pl.ds(pl.multiple_of(c * hr, 8), hr)
            for m in CHIP_FLIPS:
                slot = refs[i].at[me ^ m, mine]
                out.append((slot, slot, (x, y, 1 - c)))
        return out
    return plan


def plan_swap(n, rows):
    def plan(refs):
        x, y, c = _coords()
        out = []
        for i in range(n):
            hr = rows[i] // 2
            other = pl.ds(pl.multiple_of((1 - c) * hr, 8), hr)
            out.append((refs[i].at[pl.ds(0, N_CHIPS), other], refs[n + i], (x, y, 1 - c)))
        return out
    return plan


def plan_exchange(n):
    def plan(refs):
        x, y, c = _coords()
        me = 2 * x + y
        out = []
        for i in range(n):
            for m in CHIP_FLIPS:
                out.append((refs[i].at[me ^ m], refs[n + i].at[me], (*_flip(x, y, m), c)))
        return out
    return plan


def plan_share(n, rows):
    def plan(refs):
        x, y, c = _coords()
        out = []
        for i in range(n):
            hr = rows[i] // 2
            mine = refs[i].at[pl.ds(pl.multiple_of(c * hr, 8), hr)]
            out.append((mine, mine, (x, y, 1 - c)))
        return out
    return plan


def reduce_scatter(grads, c_arr, tag):
    got = swap_halves(grads, "rs_swap_" + tag)
    parts = [half_sum(g, r, c_arr, "rs_half_sum") for g, r in zip(grads, got)]
    slots = exchange_parts(parts, "rs_exchange_" + tag)
    return share_halves([slot_sum(s, c_arr, "rs_slot_sum") for s in slots], "rs_share_" + tag)


VM = pl.BlockSpec(memory_space=pltpu.VMEM)


def small_gather(v, name):
    def body(v_ref, out_ref, send_sems, recv_sems):
        x, y, c = _coords()
        me = 2 * x + y
        out_ref[me] = v_ref[...]
        cps = []
        for j, m in enumerate(CHIP_FLIPS):
            cp = _remote(v_ref, out_ref.at[me], send_sems, recv_sems, j, (*_flip(x, y, m), c))
            cp.start()
            cps.append(cp)
        for cp in cps:
            cp.wait()

    return pl.pallas_call(
        body, in_specs=[VM], out_specs=VM, out_shape=SDS((N_CHIPS,) + v.shape, v.dtype),
        scratch_shapes=[pltpu.SemaphoreType.DMA((3,)), pltpu.SemaphoreType.DMA((3,))],
        compiler_params=pltpu.CompilerParams(vmem_limit_bytes=VMEM_LIMIT_BYTES), name=name,
    )(v)


def small_allreduce(v, name):
    def body(v_ref, out_ref, sib_buf, slots, send_sems, recv_sems):
        x, y, c = _coords()
        me = 2 * x + y
        swap = _remote(v_ref, sib_buf, send_sems, recv_sems, 0, (x, y, 1 - c))
        swap.start()
        swap.wait()
        slots[me] = v_ref[...] + sib_buf[...]
        cps = []
        for j, m in enumerate(CHIP_FLIPS):
            cp = _remote(slots.at[me], slots.at[me], send_sems, recv_sems, 1 + j, (*_flip(x, y, m), c))
            cp.start()
            cps.append(cp)
        for cp in cps:
            cp.wait()
        out_ref[...] = (slots[0] + slots[1]) + (slots[2] + slots[3])

    return pl.pallas_call(
        body, in_specs=[VM], out_specs=VM, out_shape=SDS(v.shape, v.dtype),
        scratch_shapes=[pltpu.VMEM(v.shape, v.dtype), pltpu.VMEM((N_CHIPS,) + v.shape, v.dtype),
                        pltpu.SemaphoreType.DMA((4,)), pltpu.SemaphoreType.DMA((4,))],
        compiler_params=pltpu.CompilerParams(vmem_limit_bytes=VMEM_LIMIT_BYTES), name=name,
    )(v)


def adamw(w, g_list, m, v, name):
    nl, rows, cols = w.shape
    tr = _sum_rows_tile(rows, cols) if rows % 16 == 0 else rows
    bc1 = 1.0 - ADAM_B1 ** ADAM_STEP
    bc2 = 1.0 - ADAM_B2 ** ADAM_STEP

    def body(*refs):
        w_ref, m_ref, v_ref = refs[:3]
        g_refs = refs[3:3 + nl]
        go_ref, d_ref, mo_ref, vo_ref = refs[3 + nl:]
        layer = pl.program_id(0)
        for l in range(nl):
            @pl.when(layer == l)
            def _(l=l):
                g = g_refs[l][...]
                m_new = ADAM_B1 * m_ref[...] + (1.0 - ADAM_B1) * g
                v_new = ADAM_B2 * v_ref[...] + (1.0 - ADAM_B2) * (g * g)
                m_hat = m_new / bc1
                v_hat = v_new / bc2
                go_ref[...] = g
                d_ref[...] = -ADAM_LR * (m_hat / (jnp.sqrt(v_hat) + ADAM_EPS) + ADAM_WD * w_ref[...])
                mo_ref[...] = m_new
                vo_ref[...] = v_new

    stk = BS((None, tr, cols), lambda l, i: (l, i, 0))
    flat = BS((tr, cols), lambda l, i: (i, 0))
    out = SDS((nl, rows, cols), F32)
    return pl.pallas_call(
        body, grid=(nl, rows // tr), in_specs=[stk] * 3 + [flat] * nl, out_specs=[stk] * 4,
        out_shape=[out] * 4, compiler_params=_cp("parallel", "parallel"), name=name,
    )(w, m, v, *g_list)


WEIGHTS = ["mem_norm_g", "a_pre_mix_g", "a_post_mix_g", "a_pre_ffn_g", "a_post_ffn_g", "a_w_in", "a_conv_w",
           "a_conv_b", "a_gate_a_w", "a_gate_a_b", "a_gate_x_w", "a_gate_x_b", "a_lambda", "a_w_mem_kv", "a_w_out",
           "a_w_ffn_in", "a_w_ffn_out", "kv_norm_g", "w_kv_shared", "b_pre_mix_g", "b_post_mix_g", "b_pre_ffn_g",
           "b_post_ffn_g", "b_w_in", "b_w_mem_kv", "b_w_out", "b_w_ffn_in", "b_w_ffn_out"]
BIG = {"a_w_in": True, "a_w_mem_kv": False, "a_w_out": False, "a_w_ffn_in": True, "a_w_ffn_out": False,
       "w_kv_shared": True, "b_w_in": False, "b_w_mem_kv": False, "b_w_out": False, "b_w_ffn_in": True,
       "b_w_ffn_out": False}
SHARDED_SMALL = ["a_pre_mix_g", "a_post_mix_g", "a_pre_ffn_g", "a_post_ffn_g", "a_conv_w", "a_conv_b", "a_gate_a_b",
                 "a_gate_x_b", "a_lambda"]
REPL_SMALL = ["mem_norm_g", "kv_norm_g", "b_pre_mix_g", "b_post_mix_g", "b_pre_ffn_g", "b_post_ffn_g", "a_gate_a_w",
              "a_gate_x_w"]
LANES = 128


def _pack(arrs, row_multiple=8):
    flat = jnp.concatenate([a.reshape(-1) for a in arrs])
    pad = -flat.shape[0] % (LANES * row_multiple)
    if pad:
        flat = jnp.concatenate([flat, jnp.zeros((pad,), flat.dtype)])
    return flat.reshape(-1, LANES)


def _unpack(packed, shapes):
    flat = packed.reshape(-1)
    out, pos = [], 0
    for sh in shapes:
        size = math.prod(sh)
        out.append(flat[pos:pos + size].reshape(sh))
        pos += size
    return out


def kernel(x, mem, mem_norm_g, a_pre_mix_g, a_post_mix_g, a_pre_ffn_g, a_post_ffn_g, a_w_in, a_conv_w, a_conv_b,
           a_gate_a_w, a_gate_a_b, a_gate_x_w, a_gate_x_b, a_lambda, a_w_mem_kv, a_w_out, a_w_ffn_in, a_w_ffn_out,
           kv_norm_g, w_kv_shared, b_pre_mix_g, b_post_mix_g, b_pre_ffn_g, b_post_ffn_g, b_w_in, b_w_mem_kv, b_w_out,
           b_w_ffn_in, b_w_ffn_out, loss_target, m_mem_norm_g, m_a_pre_mix_g, m_a_post_mix_g, m_a_pre_ffn_g,
           m_a_post_ffn_g, m_a_w_in, m_a_conv_w, m_a_conv_b, m_a_gate_a_w, m_a_gate_a_b, m_a_gate_x_w, m_a_gate_x_b,
           m_a_lambda, m_a_w_mem_kv, m_a_w_out, m_a_w_ffn_in, m_a_w_ffn_out, m_kv_norm_g, m_w_kv_shared, m_b_pre_mix_g,
           m_b_post_mix_g, m_b_pre_ffn_g, m_b_post_ffn_g, m_b_w_in, m_b_w_mem_kv, m_b_w_out, m_b_w_ffn_in, m_b_w_ffn_out,
           v_mem_norm_g, v_a_pre_mix_g, v_a_post_mix_g, v_a_pre_ffn_g, v_a_post_ffn_g, v_a_w_in, v_a_conv_w, v_a_conv_b,
           v_a_gate_a_w, v_a_gate_a_b, v_a_gate_x_w, v_a_gate_x_b, v_a_lambda, v_a_w_mem_kv, v_a_w_out, v_a_w_ffn_in,
           v_a_w_ffn_out, v_kv_norm_g, v_w_kv_shared, v_b_pre_mix_g, v_b_post_mix_g, v_b_pre_ffn_g, v_b_post_ffn_g,
           v_b_w_in, v_b_w_mem_kv, v_b_w_out, v_b_w_ffn_in, v_b_w_ffn_out):
    a = dict(locals())
    xi, yi, ci = _coords()
    chip = 2 * xi + yi
    c_arr = jnp.stack([ci, chip]).astype(jnp.int32)

    got = small_gather(_pack([a[n] for n in SHARDED_SMALL]), "small_gather")
    per_chip = [_unpack(got[s], [a[n].shape for n in SHARDED_SMALL]) for s in range(N_CHIPS)]
    small = {n: jnp.concatenate([per_chip[s][k] for s in range(N_CHIPS)], axis=-1)
             for k, n in enumerate(SHARDED_SMALL)}
    small.update({n: a[n] for n in REPL_SMALL})

    names_a = [n for n in BIG if n.startswith("a_")]
    names_b = [n for n in BIG if n.startswith("b_")]
    groups = [[(n, 0) for n in names_a], [(n, 1) for n in names_a] + [("w_kv_shared", None)],
              [(n, 0) for n in names_b], [(n, 1) for n in names_b]]
    big = {n: [None, None] for n in BIG if n != "w_kv_shared"}
    gs, gb = {}, {}
    reduced = {n: [None, None] for n in BIG if n != "w_kv_shared"}

    def put(store, n, j, val):
        if j is None:
            store[n] = val
        else:
            store[n][j] = val

    class Exchange:
        def __init__(self):
            self.state = {}

        def gather_ici(self, g):
            shards = [(a[n] if j is None else a[n][j]).astype(MXU) for n, j in groups[g]]
            rows = [sh.shape[0] for sh in shards]
            lands = [lax.empty((N_CHIPS,) + sh.shape, sh.dtype) for sh in shards]
            plan = plan_gather_ici(len(shards), rows)
            ss, rs, bufs, tok = split_start("gather_ici_%d" % g, shards + lands, plan, 4 * len(shards))
            self.state["g", g] = (ss, rs, bufs, plan, rows)
            return tok

        def gather_d2d(self, g, after):
            ss, rs, bufs, plan, rows = self.state.pop(("g", g))
            n = len(rows)
            outs = split_wait("gather_ici_wait_%d" % g, ss, rs, bufs, after, plan)[n:]
            plan = plan_gather_d2d(n, rows)
            ss, rs, bufs, tok = split_start("gather_d2d_%d" % g, outs, plan, 3 * n)
            self.state["g", g] = (ss, rs, bufs, plan)
            return tok

        def gather_done(self, g, after):
            ss, rs, bufs, plan = self.state.pop(("g", g))
            outs = split_wait("gather_d2d_wait_%d" % g, ss, rs, bufs, after, plan)
            for (n, j), w in zip(groups[g], outs):
                put(big, n, j, w if BIG[n] else w.reshape(-1, w.shape[-1]))

        def rs_swap(self, g):
            grads = []
            for n, j in groups[g]:
                gr = gb[n] if j is None else gb[n][j]
                grads.append(gr if BIG[n] else gr.reshape(N_CHIPS, gr.shape[0] // N_CHIPS, gr.shape[1]))
            rows = [gr.shape[1] for gr in grads]
            lands = [lax.empty((N_CHIPS, gr.shape[1] // 2, gr.shape[2]), F32) for gr in grads]
            plan = plan_swap(len(grads), rows)
            ss, rs, bufs, tok = split_start("rs_swap_%d" % g, grads + lands, plan, len(grads))
            self.state["r", g] = (ss, rs, bufs, plan, rows)
            return tok

        def rs_exchange(self, g, after):
            ss, rs, bufs, plan, rows = self.state.pop(("r", g))
            n = len(rows)
            bufs = split_wait("rs_swap_wait_%d" % g, ss, rs, bufs, after, plan)
            sums = [half_sum(gr, got, c_arr, "rs_half_sum") for gr, got in zip(bufs[:n], bufs[n:])]
            plan = plan_exchange(n)
            ss, rs, bufs, tok = split_start("rs_exchange_%d" % g, [p for p, _ in sums] + [s for _, s in sums], plan,
                                            3 * n)
            self.state["r", g] = (ss, rs, bufs, plan, rows)
            return tok

        def rs_share(self, g, after):
            ss, rs, bufs, plan, rows = self.state.pop(("r", g))
            n = len(rows)
            slots = split_wait("rs_exchange_wait_%d" % g, ss, rs, bufs, after, plan)[n:]
            fulls = [slot_sum(s, c_arr, "rs_slot_sum") for s in slots]
            plan = plan_share(n, rows)
            ss, rs, bufs, tok = split_start("rs_share_%d" % g, fulls, plan, n)
            self.state["r", g] = (ss, rs, bufs, plan)
            return tok

        def rs_done(self, g, after):
            ss, rs, bufs, plan = self.state.pop(("r", g))
            outs = split_wait("rs_share_wait_%d" % g, ss, rs, bufs, after, plan)
            for (n, j), r in zip(groups[g], outs):
                put(reduced, n, j, r)

        def hook(self, where, l, after):
            if where == "fwd_begin":
                if l == 0:
                    tok = self.gather_ici(0)
                    tok = self.gather_d2d(0, tok)
                    self.gather_done(0, tok)
                return self.gather_ici(l + 1) if l < 3 else None
            if where == "fwd_mid":
                return self.gather_d2d(l + 1, after) if l < 3 else None
            if where == "fwd_end":
                if l < 3:
                    self.gather_done(l + 1, after)
                return None
            if where == "bwd_begin":
                return self.rs_swap(l + 1) if l < 3 else None
            if where == "bwd_mid1":
                return self.rs_exchange(l + 1, after) if l < 3 else None
            if where == "bwd_mid2":
                return self.rs_share(l + 1, after) if l < 3 else None
            if where == "bwd_end":
                if l < 3:
                    self.rs_done(l + 1, after)
                if l == 0:
                    tok = self.rs_swap(0)
                    tok = self.rs_exchange(0, tok)
                    tok = self.rs_share(0, tok)
                    self.rs_done(0, tok)
                return None
            raise ValueError(where)

    loss_parts, dx = _fwd_bwd(x[0], mem[0], loss_target[0], small, big, gs, gb, Exchange())
    loss = lax.psum(jnp.sum(loss_parts) * (0.5 / D), ("x", "y", "c"))

    res = {}
    reduced["w_kv_shared"] = [reduced["w_kv_shared"]]
    for n in BIG:
        shape = a[n].shape
        rows, cols = shape[-2], shape[-1]
        stk = (-1, rows, cols)
        outs = adamw(a[n].reshape(stk), reduced[n], a["m_" + n].reshape(stk), a["v_" + n].reshape(stk), "adamw")
        res[n] = [o.reshape(shape) for o in outs]

    def full(n):
        g = gs[n]
        return jnp.stack(g) if isinstance(g, list) else g

    order = SHARDED_SMALL + REPL_SMALL
    full_shapes = [full(n).shape for n in order]
    summed = _unpack(small_allreduce(_pack([full(n) for n in order]), "small_allreduce"), full_shapes)
    mine = []
    for n, g in zip(order, summed):
        if n in SHARDED_SMALL:
            width = a[n].shape[-1]
            g = lax.dynamic_slice_in_dim(g, chip * width, width, axis=g.ndim - 1)
        mine.append(g.reshape(a[n].shape))
    shapes = [a[n].shape for n in order]
    rm = 512
    outs = adamw(_pack([a[n] for n in order], rm)[None], [_pack(mine, rm)],
                 _pack([a["m_" + n] for n in order], rm)[None], _pack([a["v_" + n] for n in order], rm)[None],
                 "adamw_small")
    unpacked = [_unpack(o[0], shapes) for o in outs]
    for k, n in enumerate(order):
        res[n] = [u[k] for u in unpacked]

    return (loss, dx[None], *[res[n][0] for n in WEIGHTS], *[res[n][1] for n in WEIGHTS],
            *[res[n][2] for n in WEIGHTS], *[res[n][3] for n in WEIGHTS])
```

```python
import functools
import math

import jax
import jax.numpy as jnp
from jax import lax
from jax.experimental import pallas as pl
from jax.experimental.pallas import tpu as pltpu

D = 2048
HD = 128
MEM_W = 512
MEM_HEADS = 4
MIX_W = D - MEM_W
N_BLK = MIX_W // HD
D_FF = 5632
N_MEM = 256
RMS_EPS = 1e-6
NEG_INF = -1e30
LRU_C = 8.0
DIL_GROUPS = ((128, 1), (512, 4), (2048, 16))
Q_BLOCK = 128
SCALE = HD ** -0.5
N_CHIPS = 4

ADAM_LR = 0.001
ADAM_B1 = 0.9
ADAM_B2 = 0.999
ADAM_EPS = 1e-08
ADAM_WD = 0.01
ADAM_STEP = 10

MXU = jnp.bfloat16
F32 = jnp.float32
VMEM_LIMIT_BYTES = 56 * 1024 * 1024

BS = pl.BlockSpec
SDS = jax.ShapeDtypeStruct
MESH = pl.DeviceIdType.MESH


def _cp(*sem):
    return pltpu.CompilerParams(dimension_semantics=sem or None, vmem_limit_bytes=VMEM_LIMIT_BYTES)


def _dot(a, b, dn=((1,), (0,))):
    return lax.dot_general(a, b, (dn, ((), ())), preferred_element_type=F32)


def _div(i, n):
    return lax.div(i, jnp.int32(n))


def _rem(i, n):
    return lax.rem(i, jnp.int32(n))


NN = ((1,), (0,))
NT = ((1,), (1,))
TN = ((0,), (0,))


def _sigmoid(z):
    return 1.0 / (1.0 + jnp.exp(-z))


def _log1p_pos(u):
    return jnp.where(u < 1e-2, u * (1.0 - u * (0.5 - u * (1.0 / 3.0))), jnp.log(1.0 + u))


def _neg_expm1(z):
    return jnp.where(z > -1e-2, -z * (1.0 + z * (0.5 + z * (1.0 / 6.0))), 1.0 - jnp.exp(z))


def _softplus(z):
    return jnp.maximum(z, 0.0) + _log1p_pos(jnp.exp(-jnp.abs(z)))


_GELU_C = math.sqrt(2.0 / math.pi)


def _gelu_and_grad(x):
    x2 = x * x
    t = jnp.tanh(_GELU_C * (x + 0.044715 * x * x2))
    g = 0.5 * x * (1.0 + t)
    dg = 0.5 * (1.0 + t) + 0.5 * x * (1.0 - t * t) * _GELU_C * (1.0 + 3.0 * 0.044715 * x2)
    return g, dg


def _row_tile(rows):
    return min(256, rows)


def norm_cast(x, g, name):
    rows = x.shape[0]
    tr = _row_tile(rows)

    def body(x_ref, g_ref, o_ref):
        xv = x_ref[...]
        r = lax.rsqrt(jnp.mean(xv * xv, axis=-1, keepdims=True) + RMS_EPS)
        o_ref[...] = (xv * r * g_ref[...]).astype(o_ref.dtype)

    return pl.pallas_call(
        body, grid=(rows // tr,),
        in_specs=[BS((tr, D), lambda i: (i, 0)), BS((1, D), lambda i: (0, 0))],
        out_specs=BS((tr, D), lambda i: (i, 0)),
        out_shape=SDS((rows, D), MXU), compiler_params=_cp("parallel"), name=name,
    )(x, g.reshape(1, D))


def resid_norm(h, y, g, name):
    rows = h.shape[0]
    tr = _row_tile(rows)

    def body(h_ref, y_ref, g_ref, o_ref):
        yv = y_ref[...]
        r = lax.rsqrt(jnp.mean(yv * yv, axis=-1, keepdims=True) + RMS_EPS)
        o_ref[...] = h_ref[...] + yv * r * g_ref[...]

    return pl.pallas_call(
        body, grid=(rows // tr,),
        in_specs=[BS((tr, D), lambda i: (i, 0)), BS((tr, D), lambda i: (i, 0)), BS((1, D), lambda i: (0, 0))],
        out_specs=BS((tr, D), lambda i: (i, 0)),
        out_shape=SDS((rows, D), F32), compiler_params=_cp("parallel"), name=name,
    )(h, y, g.reshape(1, D))


def norm_bwd(x, g, dy, res, out_dtype, name):
    rows = x.shape[0]
    tr = _row_tile(rows)
    has_res = res is not None

    def body(*refs):
        if has_res:
            x_ref, g_ref, dy_ref, res_ref, dx_ref, dg_ref = refs
        else:
            x_ref, g_ref, dy_ref, dx_ref, dg_ref = refs
        xv = x_ref[...]
        dyv = dy_ref[...].astype(F32)
        r = lax.rsqrt(jnp.mean(xv * xv, axis=-1, keepdims=True) + RMS_EPS)
        xhat = xv * r
        dxhat = dyv * g_ref[...]
        dx = r * (dxhat - xhat * jnp.mean(dxhat * xhat, axis=-1, keepdims=True))
        if has_res:
            dx = dx + res_ref[...]
        dx_ref[...] = dx.astype(dx_ref.dtype)

        @pl.when(pl.program_id(0) == 0)
        def _():
            dg_ref[...] = jnp.zeros_like(dg_ref)

        dg_ref[...] += jnp.sum(dyv * xhat, axis=0, keepdims=True)

    row = BS((tr, D), lambda i: (i, 0))
    vec = BS((1, D), lambda i: (0, 0))
    ins = [x, g.reshape(1, D), dy] + ([res] if has_res else [])
    dx, dg = pl.pallas_call(
        body, grid=(rows // tr,),
        in_specs=[row, vec, row] + ([row] if has_res else []),
        out_specs=[row, vec],
        out_shape=[SDS((rows, D), out_dtype), SDS((1, D), F32)],
        compiler_params=_cp("arbitrary"), name=name,
    )(*ins)
    return dx, dg.reshape(D)


def loss_head(y, target, name):
    rows = y.shape[0]
    tr = _row_tile(rows)

    def body(y_ref, t_ref, dy_ref, acc_ref):
        err = y_ref[...] - t_ref[...]
        dy_ref[...] = err * (1.0 / D)

        @pl.when(pl.program_id(0) == 0)
        def _():
            acc_ref[...] = jnp.zeros_like(acc_ref)

        acc_ref[...] += jnp.sum(err * err, axis=0, keepdims=True)

    row = BS((tr, D), lambda i: (i, 0))
    dy, acc = pl.pallas_call(
        body, grid=(rows // tr,), in_specs=[row, row],
        out_specs=[row, BS((1, D), lambda i: (0, 0))],
        out_shape=[SDS((rows, D), F32), SDS((1, D), F32)],
        compiler_params=_cp("arbitrary"), name=name,
    )(y, target)
    return acc, dy


def _mm_call(ins, in_specs, pick, dn, grid, o_spec, out_sds, name):
    gk = grid[2]
    n_in = len(ins)

    def body(*refs):
        o_ref = refs[n_in]
        k = pl.program_id(2)

        def step(a_ref, b_ref):
            p = _dot(a_ref[...], b_ref[...], dn)
            if gk == 1:
                o_ref[...] = p.astype(o_ref.dtype)
            else:
                acc = o_ref if out_sds.dtype == F32 else refs[n_in + 1]

                @pl.when(k == 0)
                def _():
                    acc[...] = p

                @pl.when(k > 0)
                def _():
                    acc[...] += p

                if acc is not o_ref:
                    @pl.when(k == gk - 1)
                    def _():
                        o_ref[...] = acc[...].astype(o_ref.dtype)

        pick(refs[:n_in], k, step)

    scratch = []
    if gk > 1 and out_sds.dtype != F32:
        scratch = [pltpu.VMEM(o_spec.block_shape[-2:], F32)]
    return pl.pallas_call(
        body, grid=grid, in_specs=in_specs, out_specs=o_spec, out_shape=out_sds,
        scratch_shapes=scratch, compiler_params=_cp("parallel", "parallel", "arbitrary"), name=name,
    )(*ins)


def _pick2(refs, k, step):
    step(refs[0], refs[1])


def mm_nn(a, w, *, tm, tn, tk, out_dtype, name):
    m, kdim = a.shape
    if w.ndim == 3:
        c = w.shape[2]
        n = N_CHIPS * c
        per = c // tn
        b_spec = BS((None, tk, tn), lambda i, j, k: (_div(j, per), k, _rem(j, per)))
    else:
        n = w.shape[1]
        b_spec = BS((tk, tn), lambda i, j, k: (k, j))
    grid = (m // tm, n // tn, kdim // tk)
    return _mm_call([a, w], [BS((tm, tk), lambda i, j, k: (i, k)), b_spec], _pick2, NN, grid,
                    BS((tm, tn), lambda i, j, k: (i, j)), SDS((m, n), out_dtype), name)


def mm_nt(a_list, w, *, tm, tn, tk, out_dtype, name):
    m = a_list[0].shape[0]
    ka = a_list[0].shape[1]
    n_a = len(a_list)
    kdim = ka * n_a
    if w.ndim == 3:
        c = w.shape[2]
        n = w.shape[1]
        per = c // tk
        b_spec = BS((None, tn, tk), lambda i, j, k: (_div(k, per), j, _rem(k, per)))
    else:
        n = w.shape[0]
        b_spec = BS((tn, tk), lambda i, j, k: (j, k))
    gk = kdim // tk
    half = gk // n_a
    grid = (m // tm, n // tn, gk)
    if n_a == 1:
        a_specs = [BS((tm, tk), lambda i, j, k: (i, k))]
        pick = lambda refs, k, step: step(refs[0], refs[1])
    else:
        a_specs = [BS((tm, tk), lambda i, j, k: (i, jnp.minimum(k, half - 1))),
                   BS((tm, tk), lambda i, j, k: (i, jnp.maximum(k - half, 0)))]

        def pick(refs, k, step):
            @pl.when(k < half)
            def _():
                step(refs[0], refs[2])

            @pl.when(k >= half)
            def _():
                step(refs[1], refs[2])

    return _mm_call(list(a_list) + [w], a_specs + [b_spec], pick, NT, grid,
                    BS((tm, tn), lambda i, j, k: (i, j)), SDS((m, n), out_dtype), name)


def mm_tn(a, b_list, *, t1, tn, ts, col_shards, name):
    s, k1 = a.shape
    nb = b_list[0].shape[1]
    n_b = len(b_list)
    n = nb * n_b
    gn = n // tn
    half = gn // n_b
    grid = (k1 // t1, gn, s // ts)
    if col_shards:
        c = n // N_CHIPS
        per = c // tn
        o_spec = BS((None, t1, tn), lambda i, j, k: (_div(j, per), i, _rem(j, per)))
        out_sds = SDS((N_CHIPS, k1, c), F32)
    else:
        o_spec = BS((t1, tn), lambda i, j, k: (i, j))
        out_sds = SDS((k1, n), F32)
    a_spec = BS((ts, t1), lambda i, j, k: (k, i))
    if n_b == 1:
        b_specs = [BS((ts, tn), lambda i, j, k: (k, j))]
        pick = lambda refs, k, step: step(refs[0], refs[1])
    else:
        b_specs = [BS((ts, tn), lambda i, j, k: (k, jnp.minimum(j, half - 1))),
                   BS((ts, tn), lambda i, j, k: (k, jnp.maximum(j - half, 0)))]

        def pick(refs, k, step):
            j = pl.program_id(1)

            @pl.when(j < half)
            def _():
                step(refs[0], refs[1])

            @pl.when(j >= half)
            def _():
                step(refs[0], refs[2])

    return _mm_call([a] + list(b_list), [a_spec] + b_specs, pick, TN, grid, o_spec, out_sds, name)


def ffn_in_fwd(hn, w, name):
    s = hn.shape[0]
    tm = min(512, s)
    tn = D_FF // 4

    def body(a_ref, wg_ref, wu_ref, g_ref, u_ref, act_ref):
        a = a_ref[...]
        g = _dot(a, wg_ref[...])
        u = _dot(a, wu_ref[...])
        g_ref[...] = g
        u_ref[...] = u
        act_ref[...] = (g * _sigmoid(g) * u).astype(act_ref.dtype)

    tile = BS((tm, tn), lambda j, i: (i, j))
    return pl.pallas_call(
        body, grid=(4, s // tm),
        in_specs=[BS((tm, D), lambda j, i: (i, 0)),
                  BS((None, D, tn), lambda j, i: (_div(j, 2), 0, _rem(j, 2))),
                  BS((None, D, tn), lambda j, i: (2 + _div(j, 2), 0, _rem(j, 2)))],
        out_specs=[tile, tile, tile],
        out_shape=[SDS((s, D_FF), F32), SDS((s, D_FF), F32), SDS((s, D_FF), MXU)],
        compiler_params=_cp("parallel", "parallel"), name=name,
    )(hn, w, w)


def ffn_act_bwd(dy, w_out, g, u, name):
    s = dy.shape[0]
    tm = min(512, s)
    tn = D_FF // 4

    def body(dy_ref, w_ref, g_ref, u_ref, dg_ref, du_ref):
        dact = _dot(dy_ref[...], w_ref[...], NT)
        gv = g_ref[...]
        sg = _sigmoid(gv)
        dg_ref[...] = (dact * u_ref[...] * sg * (1.0 + gv * (1.0 - sg))).astype(dg_ref.dtype)
        du_ref[...] = (dact * gv * sg).astype(du_ref.dtype)

    tile = BS((tm, tn), lambda j, i: (i, j))
    return pl.pallas_call(
        body, grid=(4, s // tm),
        in_specs=[BS((tm, D), lambda j, i: (i, 0)), BS((tn, D), lambda j, i: (j, 0)), tile, tile],
        out_specs=[tile, tile],
        out_shape=[SDS((s, D_FF), MXU), SDS((s, D_FF), MXU)],
        compiler_params=_cp("parallel", "parallel"), name=name,
    )(dy, w_out, g, u)


LRU_T = 256
HALO = 8


def _shift_down(x, k, fill):
    rows = x.shape[0]
    idx = lax.broadcasted_iota(jnp.int32, x.shape, 0)
    return jnp.where(idx < k, fill, pltpu.roll(x, k, 0))


def _shift_up(x, k, fill):
    rows = x.shape[0]
    idx = lax.broadcasted_iota(jnp.int32, x.shape, 0)
    return jnp.where(idx >= rows - k, fill, pltpu.roll(x, rows - k, 0))


def _conv_taps(xcat):
    rows = xcat.shape[0]
    taps = []
    for k in range(4):
        off = HALO - 3 + k
        taps.append(xcat[off:off + LRU_T] if off == HALO else pltpu.roll(xcat, rows - off, 0)[:LRU_T])
    return taps


def _gates(xc, wa_ref, ba, wx_ref, bx, lam, za_ref, zx_ref):
    xm = xc.astype(MXU)
    for n in range(N_BLK):
        sl = slice(n * HD, (n + 1) * HD)
        za_ref[:, sl] = _dot(xm[:, sl], wa_ref[n])
        zx_ref[:, sl] = _dot(xm[:, sl], wx_ref[n])
    ra = _sigmoid(za_ref[...] + ba)
    ii = _sigmoid(zx_ref[...] + bx)
    sp = _softplus(-lam)
    log_a = -LRU_C * ra * sp
    a = jnp.exp(log_a)
    mult = jnp.sqrt(_neg_expm1(2.0 * log_a))
    return ra, ii, sp, a, mult


def lru_fwd(proj, conv_w, conv_b, wa, ba, wx, bx, lam, name):
    s = proj.shape[0]
    c = MIX_W
    nblk = s // LRU_T
    hpb = LRU_T // HALO

    def body(x_ref, halo_ref, cw_ref, cb_ref, wa_ref, ba_ref, wx_ref, bx_ref, lam_ref,
             xc_ref, h_ref, carry, za_ref, zx_ref):
        i = pl.program_id(0)

        @pl.when(i == 0)
        def _():
            carry[...] = jnp.zeros_like(carry)

        halo = jnp.where(i == 0, 0.0, halo_ref[...])
        xcat = jnp.concatenate([halo, x_ref[...]], axis=0)
        taps = _conv_taps(xcat)
        xc = cb_ref[...] + sum(cw_ref[k:k + 1, :] * taps[k] for k in range(4))
        xc_ref[...] = xc
        _, ii, _, a, mult = _gates(xc, wa_ref, ba_ref[...], wx_ref, bx_ref[...], lam_ref[...], za_ref, zx_ref)
        b = mult * (ii * xc)
        sh = 1
        while sh < LRU_T:
            b = a * _shift_down(b, sh, 0.0) + b
            a = a * _shift_down(a, sh, 1.0)
            sh *= 2
        h = b + a * carry[HALO - 1:HALO, :]
        h_ref[...] = h
        carry[...] = h[LRU_T - HALO:, :]

    def full(shape):
        return BS(shape, lambda i: (0,) * len(shape))

    blk = BS((LRU_T, c), lambda i: (i, 0))
    return pl.pallas_call(
        body, grid=(nblk,),
        in_specs=[blk, BS((HALO, c), lambda i: (jnp.maximum(i * hpb - 1, 0), 0)),
                  full((4, c)), full((1, c)), full((N_BLK, HD, HD)), full((1, c)),
                  full((N_BLK, HD, HD)), full((1, c)), full((1, c))],
        out_specs=[blk, blk],
        out_shape=[SDS((s, c), F32), SDS((s, c), F32)],
        scratch_shapes=[pltpu.VMEM((HALO, c), F32), pltpu.VMEM((LRU_T, c), F32), pltpu.VMEM((LRU_T, c), F32)],
        compiler_params=_cp("arbitrary"), name=name,
    )(proj, proj, conv_w, conv_b.reshape(1, c), wa.astype(MXU), ba.reshape(1, c), wx.astype(MXU),
      bx.reshape(1, c), lam.reshape(1, c))


def lru_mix_prep(h, proj, m, name):
    s = h.shape[0]
    tr = _row_tile(s)

    def body(h_ref, gb_ref, m_ref, o_ref):
        ge, _ = _gelu_and_grad(gb_ref[...])
        o_ref[:, :MIX_W] = (h_ref[...] * ge).astype(o_ref.dtype)
        o_ref[:, MIX_W:] = m_ref[...]

    return pl.pallas_call(
        body, grid=(s // tr,),
        in_specs=[BS((tr, MIX_W), lambda i: (i, 0)), BS((tr, MIX_W), lambda i: (i, 1)),
                  BS((tr, MEM_W), lambda i: (i, 0))],
        out_specs=BS((tr, D), lambda i: (i, 0)), out_shape=SDS((s, D), MXU),
        compiler_params=_cp("parallel"), name=name,
    )(h, proj, m)


def lru_bwd(dym, proj, xc, hl, dqm, conv_w, wa, ba, wx, bx, lam, name):
    s = proj.shape[0]
    c = MIX_W
    nblk = s // LRU_T
    hpb = LRU_T // HALO
    wa_m = wa.astype(MXU)
    wx_m = wx.astype(MXU)

    def body(dy_ref, x_ref, xhalo_ref, gb_ref, xc_ref, h_ref, hhalo_ref, dqm_ref,
             cw_ref, wa_ref, ba_ref, wx_ref, bx_ref, lam_ref,
             dproj_ref, dcw_ref, dcb_ref, dwa_ref, dba_ref, dwx_ref, dbx_ref, dlam_ref,
             g_next, a_next, dxc_next, za_ref, zx_ref, dxc_ref):
        i = pl.program_id(0)

        @pl.when(i == 0)
        def _():
            g_next[...] = jnp.zeros_like(g_next)
            a_next[...] = jnp.zeros_like(a_next)
            dxc_next[...] = jnp.zeros_like(dxc_next)
            for r in (dcw_ref, dcb_ref, dwa_ref, dba_ref, dwx_ref, dbx_ref, dlam_ref):
                r[...] = jnp.zeros_like(r)

        first = i == nblk - 1
        xc = xc_ref[...]
        lam = lam_ref[...]
        ra, ii, sp, a, mult = _gates(xc, wa_ref, ba_ref[...], wx_ref, bx_ref[...], lam, za_ref, zx_ref)
        hl_v = h_ref[...]
        ge, dge = _gelu_and_grad(gb_ref[...])
        dyl = dy_ref[...]
        dhl = dyl * ge
        dproj_ref[:, c:2 * c] = (dyl * hl_v * dge).astype(dproj_ref.dtype)
        dproj_ref[:, 2 * c:] = dqm_ref[...]

        an = _shift_up(a, 1, 0.0)
        last_row = lax.broadcasted_iota(jnp.int32, a.shape, 0) == LRU_T - 1
        an = jnp.where(last_row, a_next[0:1, :], an)
        gb_acc = dhl
        sh = 1
        while sh < LRU_T:
            gb_acc = an * _shift_up(gb_acc, sh, 0.0) + gb_acc
            an = an * _shift_up(an, sh, 1.0)
            sh *= 2
        g = gb_acc + an * g_next[0:1, :]
        g_next[...] = g[:HALO, :]
        a_next[...] = a[:HALO, :]

        hhalo = jnp.where(first, 0.0, hhalo_ref[...])
        h_prev = _shift_down(hl_v, 1, 0.0)
        first_row = lax.broadcasted_iota(jnp.int32, a.shape, 0) == 0
        h_prev = jnp.where(first_row, hhalo[HALO - 1:HALO, :], h_prev)
        da = g * h_prev
        ixc = ii * xc
        dmult = g * ixc
        dii = g * mult * xc
        dxc = g * mult * ii
        dlog_a = (da - dmult * a / mult) * a
        dra = dlog_a * (-LRU_C) * sp
        dlam_ref[...] += jnp.sum(dlog_a * ra, axis=0, keepdims=True) * (LRU_C * _sigmoid(-lam))
        dza = dra * ra * (1.0 - ra)
        dzx = dii * ii * (1.0 - ii)
        dba_ref[...] += jnp.sum(dza, axis=0, keepdims=True)
        dbx_ref[...] += jnp.sum(dzx, axis=0, keepdims=True)
        xm = xc.astype(MXU)
        dza_m = dza.astype(MXU)
        dzx_m = dzx.astype(MXU)
        for n in range(N_BLK):
            sl = slice(n * HD, (n + 1) * HD)
            dwa_ref[n] += _dot(xm[:, sl], dza_m[:, sl], TN)
            dwx_ref[n] += _dot(xm[:, sl], dzx_m[:, sl], TN)
            dxc_ref[:, sl] = _dot(dza_m[:, sl], wa_ref[n], NT) + _dot(dzx_m[:, sl], wx_ref[n], NT)
        dxc = dxc + dxc_ref[...]

        dcat = jnp.concatenate([dxc, dxc_next[...]], axis=0)
        rows = dcat.shape[0]
        dxb = cw_ref[3:4, :] * dxc
        for k in range(3):
            dxb = dxb + cw_ref[k:k + 1, :] * pltpu.roll(dcat, rows - (3 - k), 0)[:LRU_T]
        dproj_ref[:, :c] = dxb.astype(dproj_ref.dtype)
        dxc_next[...] = dxc[:HALO, :]

        xhalo = jnp.where(first, 0.0, xhalo_ref[...])
        taps = _conv_taps(jnp.concatenate([xhalo, x_ref[...]], axis=0))
        for k in range(4):
            dcw_ref[k:k + 1, :] += jnp.sum(dxc * taps[k], axis=0, keepdims=True)
        dcb_ref[...] += jnp.sum(dxc, axis=0, keepdims=True)

    def full(shape):
        return BS(shape, lambda i: (0,) * len(shape))

    def rev(i):
        return nblk - 1 - i

    blk0 = BS((LRU_T, c), lambda i: (rev(i), 0))
    blk1 = BS((LRU_T, c), lambda i: (rev(i), 1))
    halo = BS((HALO, c), lambda i: (jnp.maximum(rev(i) * hpb - 1, 0), 0))
    outs = pl.pallas_call(
        body, grid=(nblk,),
        in_specs=[blk0, blk0, halo, blk1, blk0, blk0, halo, BS((LRU_T, MEM_W), lambda i: (rev(i), 0)),
                  full((4, c)), full((N_BLK, HD, HD)), full((1, c)), full((N_BLK, HD, HD)), full((1, c)),
                  full((1, c))],
        out_specs=[BS((LRU_T, 2 * c + MEM_W), lambda i: (rev(i), 0)), full((4, c)), full((1, c)),
                   full((N_BLK, HD, HD)), full((1, c)), full((N_BLK, HD, HD)), full((1, c)), full((1, c))],
        out_shape=[SDS((s, 2 * c + MEM_W), MXU), SDS((4, c), F32), SDS((1, c), F32),
                   SDS((N_BLK, HD, HD), F32), SDS((1, c), F32), SDS((N_BLK, HD, HD), F32), SDS((1, c), F32),
                   SDS((1, c), F32)],
        scratch_shapes=[pltpu.VMEM((HALO, c), F32), pltpu.VMEM((HALO, c), F32), pltpu.VMEM((HALO, c), F32),
                        pltpu.VMEM((LRU_T, c), F32), pltpu.VMEM((LRU_T, c), F32), pltpu.VMEM((LRU_T, c), F32)],
        compiler_params=_cp("arbitrary"), name=name,
    )(dym, proj, proj, proj, xc, hl, hl, dqm, conv_w, wa_m, ba.reshape(1, c), wx_m, bx.reshape(1, c),
      lam.reshape(1, c))
    dproj, dcw, dcb, dwa, dba, dwx, dbx, dlam = outs
    return dproj, dcw, dcb.reshape(c), dwa, dba.reshape(c), dwx, dbx.reshape(c), dlam.reshape(c)


def _mem_probs(q, k):
    sc = _dot(q, k, NT) * SCALE
    e = jnp.exp(sc - jnp.max(sc, axis=-1, keepdims=True))
    return e / jnp.sum(e, axis=-1, keepdims=True)


def mem_attn_fwd(proj, q_col, kvm, name):
    s = proj.shape[0]
    tq = min(512, s)

    def body(q_ref, kv_ref, o_ref):
        q = q_ref[...].astype(MXU)
        for hh in range(MEM_HEADS):
            sl = slice(hh * HD, (hh + 1) * HD)
            p = _mem_probs(q[:, sl], kv_ref[:, sl])
            o_ref[:, sl] = _dot(p.astype(MXU), kv_ref[:, MEM_W + hh * HD:MEM_W + (hh + 1) * HD]).astype(o_ref.dtype)

    return pl.pallas_call(
        body, grid=(s // tq,),
        in_specs=[BS((tq, MEM_W), lambda i: (i, q_col)), BS((N_MEM, 2 * MEM_W), lambda i: (0, 0))],
        out_specs=BS((tq, MEM_W), lambda i: (i, 0)), out_shape=SDS((s, MEM_W), MXU),
        compiler_params=_cp("parallel"), name=name,
    )(proj, kvm)


def mem_attn_bwd(proj, q_col, kvm, dym, name):
    s = proj.shape[0]
    tq = min(512, s)

    def body(q_ref, kv_ref, do_ref, dq_ref, dkv_ref):
        @pl.when(pl.program_id(0) == 0)
        def _():
            dkv_ref[...] = jnp.zeros_like(dkv_ref)

        q = q_ref[...].astype(MXU)
        do = do_ref[...].astype(MXU)
        for hh in range(MEM_HEADS):
            sl = slice(hh * HD, (hh + 1) * HD)
            vsl = slice(MEM_W + hh * HD, MEM_W + (hh + 1) * HD)
            k = kv_ref[:, sl]
            p = _mem_probs(q[:, sl], k)
            dp = _dot(do[:, sl], kv_ref[:, vsl], NT)
            ds = (p * (dp - jnp.sum(p * dp, axis=-1, keepdims=True)) * SCALE).astype(MXU)
            dq_ref[:, sl] = _dot(ds, k).astype(dq_ref.dtype)
            dkv_ref[:, sl] += _dot(ds, q[:, sl], TN)
            dkv_ref[:, vsl] += _dot(p.astype(MXU), do[:, sl], TN)

    return pl.pallas_call(
        body, grid=(s // tq,),
        in_specs=[BS((tq, MEM_W), lambda i: (i, q_col)), BS((N_MEM, 2 * MEM_W), lambda i: (0, 0)),
                  BS((tq, MEM_W), lambda i: (i, MIX_W // MEM_W))],
        out_specs=[BS((tq, MEM_W), lambda i: (i, 0)), BS((N_MEM, 2 * MEM_W), lambda i: (0, 0))],
        out_shape=[SDS((s, MEM_W), MXU), SDS((N_MEM, 2 * MEM_W), F32)],
        compiler_params=_cp("arbitrary"), name=name,
    )(proj, kvm, dym)


def _dil_scores(q, kp, kc, n, slope_dil):
    qi = lax.broadcasted_iota(jnp.int32, (Q_BLOCK, Q_BLOCK), 0)
    ki = lax.broadcasted_iota(jnp.int32, (Q_BLOCK, Q_BLOCK), 1)
    rel_p = qi + Q_BLOCK - ki
    rel_c = qi - ki
    s_p = _dot(q, kp, NT) * SCALE - slope_dil * rel_p.astype(F32)
    s_c = _dot(q, kc, NT) * SCALE - slope_dil * rel_c.astype(F32)
    s_p = jnp.where((rel_p <= Q_BLOCK) & (n > 0), s_p, NEG_INF)
    s_c = jnp.where(rel_c >= 0, s_c, NEG_INF)
    return s_p, s_c


def _slope_dil(gi, hh):
    head = 4 * gi + hh
    return DIL_GROUPS[gi][1] * 2.0 ** (-8.0 * (head + 1.0) / N_BLK)


def dil_attn_fwd(proj, kv, gi, name):
    dil = DIL_GROUPS[gi][1]
    s, pw = proj.shape
    sub = s // dil
    nb = sub // Q_BLOCK
    qc, kc_ = pw // MEM_W, kv.shape[1] // MEM_W

    def body(q_ref, kp_ref, kc_ref, vp_ref, vc_ref, o_ref, lse_ref):
        n = pl.program_id(1)
        q = q_ref[...].astype(MXU)
        for hh in range(4):
            sl = slice(hh * HD, (hh + 1) * HD)
            s_p, s_c = _dil_scores(q[:, sl], kp_ref[:, sl], kc_ref[:, sl], n, _slope_dil(gi, hh))
            mx = jnp.maximum(jnp.max(s_p, axis=-1, keepdims=True), jnp.max(s_c, axis=-1, keepdims=True))
            den = jnp.sum(jnp.exp(s_p - mx), axis=-1, keepdims=True) + jnp.sum(jnp.exp(s_c - mx), axis=-1, keepdims=True)
            lse = mx + jnp.log(den)
            o_ref[:, sl] = (_dot(jnp.exp(s_p - lse).astype(MXU), vp_ref[:, sl])
                            + _dot(jnp.exp(s_c - lse).astype(MXU), vc_ref[:, sl]))
            lse_ref[:, sl] = jnp.broadcast_to(lse, (Q_BLOCK, HD))

    blk = (Q_BLOCK, MEM_W)
    prev = lambda n: jnp.maximum(n - 1, 0)
    out = BS(blk, lambda r, n: (n, r))
    return pl.pallas_call(
        body, grid=(dil, nb),
        in_specs=[BS(blk, lambda r, n: (n, r * qc + gi)),
                  BS(blk, lambda r, n: (prev(n), r * kc_ + gi)), BS(blk, lambda r, n: (n, r * kc_ + gi)),
                  BS(blk, lambda r, n: (prev(n), r * kc_ + 3 + gi)), BS(blk, lambda r, n: (n, r * kc_ + 3 + gi))],
        out_specs=[out, out],
        out_shape=[SDS((sub, dil * MEM_W), F32), SDS((sub, dil * MEM_W), F32)],
        compiler_params=_cp("parallel", "parallel"), name=name,
    )(proj.reshape(sub, dil * pw), *([kv.reshape(sub, dil * kv.shape[1])] * 4))


def dil_attn_bwd(proj, kv, lse, do, dd, gi, name):
    dil = DIL_GROUPS[gi][1]
    s, pw = proj.shape
    sub = s // dil
    nb = sub // Q_BLOCK
    qc, kc_ = pw // MEM_W, kv.shape[1] // MEM_W

    def body(q_ref, kp_ref, kc_ref, vp_ref, vc_ref, lse_ref, do_ref, dd_ref, dq_ref, dk_ref, dv_ref, ck, cv):
        n = pl.program_id(1)

        @pl.when(n == 0)
        def _():
            ck[...] = jnp.zeros_like(ck)
            cv[...] = jnp.zeros_like(cv)

        @pl.when(n < nb)
        def _():
            q = q_ref[...].astype(MXU)
            do_m = do_ref[...].astype(MXU)
            for hh in range(4):
                sl = slice(hh * HD, (hh + 1) * HD)
                s_p, s_c = _dil_scores(q[:, sl], kp_ref[:, sl], kc_ref[:, sl], n, _slope_dil(gi, hh))
                lse_h = lse_ref[:, sl]
                dd_h = dd_ref[:, sl]
                p_p = jnp.exp(s_p - lse_h)
                p_c = jnp.exp(s_c - lse_h)
                ds_p = (p_p * (_dot(do_m[:, sl], vp_ref[:, sl], NT) + dd_h) * SCALE).astype(MXU)
                ds_c = (p_c * (_dot(do_m[:, sl], vc_ref[:, sl], NT) + dd_h) * SCALE).astype(MXU)
                dq_ref[:, sl] = (_dot(ds_p, kp_ref[:, sl]) + _dot(ds_c, kc_ref[:, sl])).astype(dq_ref.dtype)
                dk_ref[:, sl] = ck[:, sl] + _dot(ds_p, q[:, sl], TN)
                dv_ref[:, sl] = cv[:, sl] + _dot(p_p.astype(MXU), do_m[:, sl], TN)
                ck[:, sl] = _dot(ds_c, q[:, sl], TN)
                cv[:, sl] = _dot(p_c.astype(MXU), do_m[:, sl], TN)

        @pl.when(n == nb)
        def _():
            dk_ref[...] = ck[...]
            dv_ref[...] = cv[...]

    blk = (Q_BLOCK, MEM_W)
    cur = lambda n: jnp.minimum(n, nb - 1)
    prev = lambda n: jnp.maximum(jnp.minimum(n, nb - 1) - 1, 0)
    done = lambda n: jnp.maximum(n - 1, 0)
    own = BS(blk, lambda r, n: (cur(n), r))
    kvv = kv.reshape(sub, dil * kv.shape[1])
    return pl.pallas_call(
        body, grid=(dil, nb + 1),
        in_specs=[BS(blk, lambda r, n: (cur(n), r * qc + gi)),
                  BS(blk, lambda r, n: (prev(n), r * kc_ + gi)), BS(blk, lambda r, n: (cur(n), r * kc_ + gi)),
                  BS(blk, lambda r, n: (prev(n), r * kc_ + 3 + gi)), BS(blk, lambda r, n: (cur(n), r * kc_ + 3 + gi)),
                  own, own, own],
        out_specs=[own, BS(blk, lambda r, n: (done(n), r)), BS(blk, lambda r, n: (done(n), r))],
        out_shape=[SDS((sub, dil * MEM_W), MXU), SDS((sub, dil * MEM_W), F32), SDS((sub, dil * MEM_W), F32)],
        scratch_shapes=[pltpu.VMEM(blk, F32), pltpu.VMEM(blk, F32)],
        compiler_params=_cp("parallel", "arbitrary"), name=name,
    )(proj.reshape(sub, dil * pw), kvv, kvv, kvv, kvv, lse, do, dd)


def _group_weights(lse_refs):
    l0, l1, l2 = (r[...] for r in lse_refs)
    mx = jnp.maximum(jnp.maximum(l0, l1), l2)
    e = [jnp.exp(l - mx) for l in (l0, l1, l2)]
    den = e[0] + e[1] + e[2]
    return [x / den for x in e]


def dil_mix_prep(o_list, lse_list, m, name):
    s = m.shape[0]
    tr = _row_tile(s)

    def body(o0, o1, o2, l0, l1, l2, m_ref, out_ref):
        w = _group_weights((l0, l1, l2))
        for g, o_ref in enumerate((o0, o1, o2)):
            out_ref[:, g * MEM_W:(g + 1) * MEM_W] = (o_ref[...] * w[g]).astype(out_ref.dtype)
        out_ref[:, MIX_W:] = m_ref[...]

    blk = BS((tr, MEM_W), lambda i: (i, 0))
    return pl.pallas_call(
        body, grid=(s // tr,), in_specs=[blk] * 7,
        out_specs=BS((tr, D), lambda i: (i, 0)), out_shape=SDS((s, D), MXU),
        compiler_params=_cp("parallel"), name=name,
    )(*o_list, *lse_list, m)


def dil_mix_bwd(dym, o_list, lse_list, name):
    s = dym.shape[0]
    tr = _row_tile(s)

    def body(da_ref, o0, o1, o2, l0, l1, l2, do0, do1, do2, dd0, dd1, dd2):
        w = _group_weights((l0, l1, l2))
        tot = None
        for g, (o_ref, do_ref) in enumerate(zip((o0, o1, o2), (do0, do1, do2))):
            da = da_ref[:, g * MEM_W:(g + 1) * MEM_W]
            do_ref[...] = da * w[g]
            x = da * o_ref[...]
            dw = jnp.concatenate(
                [jnp.broadcast_to(jnp.sum(x[:, hh * HD:(hh + 1) * HD], axis=-1, keepdims=True), (tr, HD))
                 for hh in range(4)], axis=1)
            tot = w[g] * dw if tot is None else tot + w[g] * dw
        for g, dd_ref in enumerate((dd0, dd1, dd2)):
            dd_ref[...] = -w[g] * tot

    blk = BS((tr, MEM_W), lambda i: (i, 0))
    outs = pl.pallas_call(
        body, grid=(s // tr,), in_specs=[BS((tr, MIX_W), lambda i: (i, 0))] + [blk] * 6,
        out_specs=[blk] * 6, out_shape=[SDS((s, MEM_W), F32)] * 6,
        compiler_params=_cp("parallel"), name=name,
    )(dym, *o_list, *lse_list)
    return outs[:3], outs[3:]


def sum_cast(parts, name):
    s = parts[0][0].shape[0]
    tr = _row_tile(s)
    flat = [a for p in parts for a in p]
    sizes = [len(p) for p in parts]

    def body(*refs):
        out_ref = refs[-1]
        pos = 0
        for j, n in enumerate(sizes):
            acc = refs[pos][...].astype(F32)
            for t in range(1, n):
                acc = acc + refs[pos + t][...].astype(F32)
            out_ref[:, j * MEM_W:(j + 1) * MEM_W] = acc.astype(out_ref.dtype)
            pos += n

    blk = BS((tr, MEM_W), lambda i: (i, 0))
    width = MEM_W * len(parts)
    return pl.pallas_call(
        body, grid=(s // tr,), in_specs=[blk] * len(flat),
        out_specs=BS((tr, width), lambda i: (i, 0)), out_shape=SDS((s, width), MXU),
        compiler_params=_cp("parallel"), name=name,
    )(*flat)


def add_n(arrs, name):
    rows, cols = arrs[0].shape
    tr = _row_tile(rows)

    def body(*refs):
        acc = refs[0][...]
        for r in refs[1:-1]:
            acc = acc + r[...]
        refs[-1][...] = acc

    blk = BS((tr, cols), lambda i: (i, 0))
    return pl.pallas_call(
        body, grid=(rows // tr,), in_specs=[blk] * len(arrs), out_specs=blk,
        out_shape=SDS((rows, cols), F32), compiler_params=_cp("parallel"), name=name,
    )(*arrs)


class _NoExchange:
    def hook(self, where, l, after):
        return []


def _fwd_bwd(x, mem, target, small, big, gs, gb, sched):
    s = x.shape[0]
    tm = min(1024, s)
    ts = min(2048, s)

    def after_hook(arr, where, l, after):
        toks = sched.hook(where, l, after)
        return tie(arr, toks, "tie_%s_%d" % (where, l)) if toks else arr

    h = x
    saved = []
    kv = None
    mem_n = None
    for l in range(4):
        rec = l < 2
        p, j = ("a", l) if rec else ("b", l - 2)
        sv = {"h": h}
        hn = norm_cast(h, small[p + "_pre_mix_g"][j], "pre_norm")
        hn = after_hook(hn, "fwd_begin", l, h)
        if mem_n is None:
            mem_n = norm_cast(mem, small["mem_norm_g"], "mem_norm")
        kvm = mm_nn(mem_n, big[p + "_w_mem_kv"][j], tm=N_MEM, tn=2 * MEM_W, tk=D, out_dtype=MXU, name="mem_kv")
        if rec:
            proj = mm_nn(hn, big["a_w_in"][j], tm=tm, tn=896, tk=D, out_dtype=F32, name="rec_in")
            xc, hl = lru_fwd(proj, small["a_conv_w"][j], small["a_conv_b"][j], small["a_gate_a_w"][j],
                             small["a_gate_a_b"][j], small["a_gate_x_w"][j], small["a_gate_x_b"][j],
                             small["a_lambda"][j], "lru_fwd")
            m = mem_attn_fwd(proj, 2 * MIX_W // MEM_W, kvm, "rec_mem_attn")
            ym = lru_mix_prep(hl, proj, m, "lru_mix_prep")
            sv.update(xc=xc, hl=hl)
        else:
            proj = mm_nn(hn, big["b_w_in"][j], tm=tm, tn=1024, tk=D, out_dtype=F32, name="dil_in")
            o_list, lse_list = [], []
            for gi in range(3):
                o, lse = dil_attn_fwd(proj, kv, gi, "dil_attn_fwd%d" % gi)
                o_list.append(o.reshape(s, MEM_W))
                lse_list.append(lse.reshape(s, MEM_W))
            m = mem_attn_fwd(proj, MIX_W // MEM_W, kvm, "dil_mem_attn")
            ym = dil_mix_prep(o_list, lse_list, m, "dil_mix_prep")
            sv.update(o=o_list, lse=lse_list)
        ym = after_hook(ym, "fwd_q1", l, m)
        mix = mm_nn(ym, big[p + "_w_out"][j], tm=tm, tn=1024, tk=D, out_dtype=F32, name="mix_out")
        h1 = resid_norm(h, mix, small[p + "_post_mix_g"][j], "post_norm")
        hn2 = norm_cast(h1, small[p + "_pre_ffn_g"][j], "pre_norm")
        hn2 = after_hook(hn2, "fwd_mid", l, mix)
        g, u, act = ffn_in_fwd(hn2, big[p + "_w_ffn_in"][j], "ffn_in")
        act = after_hook(act, "fwd_q3", l, u)
        y2 = mm_nn(act, big[p + "_w_ffn_out"][j], tm=tm, tn=D, tk=D_FF // 4, out_dtype=F32, name="ffn_out")
        h = resid_norm(h1, y2, small[p + "_post_ffn_g"][j], "post_norm")
        sched.hook("fwd_end", l, h)
        sv.update(kvm=kvm, hn=hn, proj=proj, ym=ym, mix=mix, h1=h1, hn2=hn2, g=g, u=u, act=act, y2=y2)
        saved.append(sv)
        if l == 1:
            h_kv = h
            kvn = norm_cast(h, small["kv_norm_g"], "pre_norm")
            kv = mm_nn(kvn, big["w_kv_shared"], tm=tm, tn=768, tk=D, out_dtype=MXU, name="kv_proj")

    loss_parts, dh = loss_head(h, target, "loss_head")

    def stack2(name, j, val):
        gs.setdefault(name, [None, None])[j] = val

    def stack2b(name, j, val):
        gb.setdefault(name, [None, None])[j] = val

    dkv_parts = []
    dmem_parts = []
    dkvm = [None] * 4
    for l in (3, 2, 1, 0):
        rec = l < 2
        p, j = ("a", l) if rec else ("b", l - 2)
        sv = saved[l]
        if l == 1:
            dkv = sum_cast([(dkv_parts[0][c], dkv_parts[1][c]) for c in range(6)], "dkv_sum")
            dkvn = mm_nt([dkv], big["w_kv_shared"], tm=tm, tn=D, tk=768, out_dtype=F32, name="kv_proj_dx")
            gb["w_kv_shared"] = mm_tn(kvn, [dkv], t1=D, tn=768, ts=ts, col_shards=True, name="kv_proj_dw")
            dh, gs["kv_norm_g"] = norm_bwd(h_kv, small["kv_norm_g"], dkvn, dh, F32, "pre_norm_bwd")
        dy2, dg = norm_bwd(sv["y2"], small[p + "_post_ffn_g"][j], dh, None, MXU, "post_norm_bwd")
        dy2 = after_hook(dy2, "bwd_begin", l, dh)
        stack2(p + "_post_ffn_g", j, dg)
        dgg, dgu = ffn_act_bwd(dy2, big[p + "_w_ffn_out"][j], sv["g"], sv["u"], "ffn_act_bwd")
        dgg = after_hook(dgg, "bwd_mid1", l, dgu)
        stack2b(p + "_w_ffn_out", j, mm_tn(sv["act"], [dy2], t1=D_FF // 4, tn=D, ts=ts // 2, col_shards=False,
                                          name="ffn_out_dw"))
        dhn2 = mm_nt([dgg, dgu], big[p + "_w_ffn_in"][j], tm=tm, tn=D, tk=D_FF // 4, out_dtype=F32,
                     name="ffn_in_dx")
        stack2b(p + "_w_ffn_in", j, mm_tn(sv["hn2"], [dgg, dgu], t1=D // 2, tn=D_FF // 4, ts=ts, col_shards=True,
                                         name="ffn_in_dw"))
        dhn2 = after_hook(dhn2, "bwd_mid2", l, gb[p + "_w_ffn_in"][j])
        dh1, dg = norm_bwd(sv["h1"], small[p + "_pre_ffn_g"][j], dhn2, dh, F32, "pre_norm_bwd")
        stack2(p + "_pre_ffn_g", j, dg)
        dmix, dg = norm_bwd(sv["mix"], small[p + "_post_mix_g"][j], dh1, None, MXU, "post_norm_bwd")
        stack2(p + "_post_mix_g", j, dg)
        dym = mm_nt([dmix], big[p + "_w_out"][j], tm=tm, tn=1024, tk=D, out_dtype=F32, name="mix_out_dx")
        stack2b(p + "_w_out", j, mm_tn(sv["ym"], [dmix], t1=D, tn=1024, ts=ts, col_shards=False,
                                      name="mix_out_dw"))
        dym = after_hook(dym, "bwd_m1", l, gb[p + "_w_out"][j])
        if rec:
            dqm, dkvm[l] = mem_attn_bwd(sv["proj"], 2 * MIX_W // MEM_W, sv["kvm"], dym, "rec_mem_attn_bwd")
            dproj, dcw, dcb, dwa, dba, dwx, dbx, dlam = lru_bwd(
                dym, sv["proj"], sv["xc"], sv["hl"], dqm, small["a_conv_w"][j], small["a_gate_a_w"][j],
                small["a_gate_a_b"][j], small["a_gate_x_w"][j], small["a_gate_x_b"][j], small["a_lambda"][j],
                "lru_bwd")
            for nm, val in (("a_conv_w", dcw), ("a_conv_b", dcb), ("a_gate_a_w", dwa), ("a_gate_a_b", dba),
                            ("a_gate_x_w", dwx), ("a_gate_x_b", dbx), ("a_lambda", dlam)):
                stack2(nm, j, val)
            dhn = mm_nt([dproj], big["a_w_in"][j], tm=tm, tn=D, tk=896, out_dtype=F32, name="rec_in_dx")
            stack2b("a_w_in", j, mm_tn(sv["hn"], [dproj], t1=D, tn=896, ts=ts, col_shards=True, name="rec_in_dw"))
        else:
            dqm, dkvm[l] = mem_attn_bwd(sv["proj"], MIX_W // MEM_W, sv["kvm"], dym, "dil_mem_attn_bwd")
            do_list, dd_list = dil_mix_bwd(dym, sv["o"], sv["lse"], "dil_mix_bwd")
            dq_list, dk_list, dv_list = [], [], []
            for gi in range(3):
                dil = DIL_GROUPS[gi][1]
                view = (s // dil, dil * MEM_W)
                dq, dk, dv = dil_attn_bwd(sv["proj"], kv, sv["lse"][gi].reshape(view), do_list[gi].reshape(view),
                                          dd_list[gi].reshape(view), gi, "dil_attn_bwd%d" % gi)
                dq_list.append(dq.reshape(s, MEM_W))
                dk_list.append(dk.reshape(s, MEM_W))
                dv_list.append(dv.reshape(s, MEM_W))
            dkv_parts.append(dk_list + dv_list)
            dproj = sum_cast([(a,) for a in dq_list + [dqm]], "dil_dproj")
            dhn = mm_nt([dproj], big["b_w_in"][j], tm=tm, tn=1024, tk=D, out_dtype=F32, name="dil_in_dx")
            stack2b("b_w_in", j, mm_tn(sv["hn"], [dproj], t1=D, tn=1024, ts=ts, col_shards=False, name="dil_in_dw"))
        dk_m = dkvm[l].astype(MXU)
        dmem_parts.append(mm_nt([dk_m], big[p + "_w_mem_kv"][j], tm=N_MEM, tn=D, tk=2 * MEM_W, out_dtype=F32,
                                name="mem_kv_dx"))
        stack2b(p + "_w_mem_kv", j, mm_tn(mem_n, [dk_m], t1=D, tn=2 * MEM_W, ts=N_MEM, col_shards=False,
                                         name="mem_kv_dw"))
        dh, dg = norm_bwd(sv["h"], small[p + "_pre_mix_g"][j], dhn, dh1, F32, "pre_norm_bwd")
        stack2(p + "_pre_mix_g", j, dg)
        dh = after_hook(dh, "bwd_end", l, dh)

    _, gs["mem_norm_g"] = norm_bwd(mem, small["mem_norm_g"], add_n(dmem_parts, "dmem_sum"), None, F32,
                                   "mem_norm_bwd")
    return loss_parts, dh


ANY = pl.BlockSpec(memory_space=pl.ANY)
CHIP_FLIPS = (1, 2, 3)


def _coords():
    return lax.axis_index("x"), lax.axis_index("y"), lax.axis_index("c")


def _flip(x, y, m):
    return x ^ (m >> 1), y ^ (m & 1)


def _remote(src, dst, send_sems, recv_sems, k, device):
    return pltpu.make_async_remote_copy(src_ref=src, dst_ref=dst, send_sem=send_sems.at[k], recv_sem=recv_sems.at[k],
                                        device_id=device, device_id_type=MESH)


def gather_shards(shards, name):
    n = len(shards)

    def body(*refs):
        ins, outs = refs[:n], refs[n:2 * n]
        send_sems, recv_sems = refs[2 * n:]
        x, y, c = _coords()
        me = 2 * x + y
        sib = (x, y, 1 - c)
        halves, sends = [], []
        for i in range(n):
            hr = shards[i].shape[0] // 2
            mine = pl.ds(pl.multiple_of(c * hr, 8), hr)
            other = pl.ds(pl.multiple_of((1 - c) * hr, 8), hr)
            halves.append((mine, other))
            own = _remote(ins[i], outs[i].at[me], send_sems, recv_sems, 7 * i + 6, sib)
            own.start()
            sends.append(own)
            for j, m in enumerate(CHIP_FLIPS):
                cp = _remote(ins[i].at[mine], outs[i].at[me, mine], send_sems, recv_sems, 7 * i + j,
                             (*_flip(x, y, m), c))
                cp.start()
                sends.append(cp)
        for i in range(n):
            mine, _ = halves[i]
            for j, m in enumerate(CHIP_FLIPS):
                slot = outs[i].at[me ^ m, mine]
                _remote(slot, slot, send_sems, recv_sems, 7 * i + j, sib).wait_recv()
                fwd = _remote(slot, slot, send_sems, recv_sems, 7 * i + 3 + j, sib)
                fwd.start()
                sends.append(fwd)
        for i in range(n):
            _, other = halves[i]
            for j, m in enumerate(CHIP_FLIPS):
                slot = outs[i].at[me ^ m, other]
                _remote(slot, slot, send_sems, recv_sems, 7 * i + 3 + j, sib).wait_recv()
            _remote(ins[i], outs[i].at[me], send_sems, recv_sems, 7 * i + 6, sib).wait_recv()
        for cp in sends:
            cp.wait_send()

    return pl.pallas_call(
        body, in_specs=[ANY] * n, out_specs=[ANY] * n,
        out_shape=[SDS((N_CHIPS,) + sh.shape, sh.dtype) for sh in shards],
        scratch_shapes=[pltpu.SemaphoreType.DMA((7 * n,)), pltpu.SemaphoreType.DMA((7 * n,))],
        name=name,
    )(*shards)


def swap_halves(grads, name):
    n = len(grads)

    def body(*refs):
        ins, outs = refs[:n], refs[n:2 * n]
        send_sems, recv_sems = refs[2 * n:]
        x, y, c = _coords()
        cps = []
        for i in range(n):
            hr = grads[i].shape[1] // 2
            other = pl.ds(pl.multiple_of((1 - c) * hr, 8), hr)
            cp = _remote(ins[i].at[pl.ds(0, N_CHIPS), other], outs[i], send_sems, recv_sems, i, (x, y, 1 - c))
            cp.start()
            cps.append(cp)
        for cp in cps:
            cp.wait()

    return pl.pallas_call(
        body, in_specs=[ANY] * n, out_specs=[ANY] * n,
        out_shape=[SDS((N_CHIPS, g.shape[1] // 2, g.shape[2]), g.dtype) for g in grads],
        scratch_shapes=[pltpu.SemaphoreType.DMA((n,)), pltpu.SemaphoreType.DMA((n,))],
        name=name,
    )(*grads)


def _sum_rows_tile(rows, cols):
    for tr in (512, 256, 128, 64, 32, 16):
        if rows % tr == 0 and tr * cols * 4 <= 2 * 1024 * 1024:
            return tr
    raise ValueError((rows, cols))


def half_sum(g, got, c_arr, name):
    _, r, cols = g.shape
    hr = r // 2
    tr = _sum_rows_tile(hr, cols)

    def my_chip():
        return 2 * lax.axis_index("x") + lax.axis_index("y")

    def body(g_ref, got_ref, o_ref, own_ref):
        p = (g_ref[...] + got_ref[...]).astype(o_ref.dtype)
        o_ref[...] = p

        @pl.when(pl.program_id(1) == my_chip())
        def _():
            own_ref[...] = p

    out = SDS((N_CHIPS, hr, cols), jnp.bfloat16)
    return pl.pallas_call(
        body, grid=(hr // tr, N_CHIPS),
        in_specs=[BS((None, None, tr, cols), lambda i, s: (s, lax.axis_index("c"), i, 0)),
                  BS((None, tr, cols), lambda i, s: (s, i, 0))],
        out_specs=[BS((None, tr, cols), lambda i, s: (s, i, 0)),
                   BS((None, tr, cols), lambda i, s: (my_chip(), i, 0))],
        out_shape=[out, out], compiler_params=_cp("parallel", "arbitrary"), name=name,
    )(g.reshape(N_CHIPS, 2, hr, cols), got)


def exchange_parts(parts, name):
    n = len(parts)

    def body(*refs):
        ins, outs = refs[:n], refs[n:2 * n]
        send_sems, recv_sems, loc_sems = refs[2 * n:]
        x, y, c = _coords()
        me = 2 * x + y
        cps, locs = [], []
        for i in range(n):
            loc = pltpu.make_async_copy(ins[i].at[me], outs[i].at[me], loc_sems.at[i])
            loc.start()
            locs.append(loc)
            for j, m in enumerate(CHIP_FLIPS):
                cp = _remote(ins[i].at[me ^ m], outs[i].at[me], send_sems, recv_sems, 3 * i + j, (*_flip(x, y, m), c))
                cp.start()
                cps.append(cp)
        for cp in cps:
            cp.wait()
        for loc in locs:
            loc.wait()

    return pl.pallas_call(
        body, in_specs=[ANY] * n, out_specs=[ANY] * n,
        out_shape=[SDS(p.shape, p.dtype) for p in parts],
        scratch_shapes=[pltpu.SemaphoreType.DMA((3 * n,)), pltpu.SemaphoreType.DMA((3 * n,)),
                        pltpu.SemaphoreType.DMA((n,))],
        name=name,
    )(*parts)


def slot_sum(slots, c_arr, name):
    _, hr, cols = slots.shape
    tr = _sum_rows_tile(hr, cols)
    nblk = hr // tr

    def body(s_ref, o_ref):
        acc = s_ref[0].astype(F32)
        for p in range(1, N_CHIPS):
            acc = acc + s_ref[p].astype(F32)
        o_ref[...] = acc

    return pl.pallas_call(
        body, grid=(nblk,), in_specs=[BS((N_CHIPS, tr, cols), lambda i: (0, i, 0))],
        out_specs=BS((tr, cols), lambda i: (lax.axis_index("c") * nblk + i, 0)),
        out_shape=SDS((2 * hr, cols), F32), compiler_params=_cp("parallel"), name=name,
    )(slots)


def share_halves(bufs, name):
    n = len(bufs)

    def body(*refs):
        outs = refs[n:2 * n]
        send_sems, recv_sems = refs[2 * n:]
        x, y, c = _coords()
        cps = []
        for i in range(n):
            hr = bufs[i].shape[0] // 2
            mine = outs[i].at[pl.ds(pl.multiple_of(c * hr, 8), hr)]
            cp = _remote(mine, mine, send_sems, recv_sems, i, (x, y, 1 - c))
            cp.start()
            cps.append(cp)
        for cp in cps:
            cp.wait()

    return pl.pallas_call(
        body, in_specs=[ANY] * n, out_specs=[ANY] * n,
        out_shape=[SDS(b.shape, b.dtype) for b in bufs],
        input_output_aliases={i: i for i in range(n)},
        scratch_shapes=[pltpu.SemaphoreType.DMA((n,)), pltpu.SemaphoreType.DMA((n,))],
        name=name,
    )(*bufs)


HBM_SPEC = pl.BlockSpec(memory_space=pltpu.HBM)
SEM_SPEC = pl.BlockSpec(memory_space=pltpu.SEMAPHORE)
EFFECT = pltpu.SideEffectType.DATAFLOW_SIDE_EFFECTING


def split_start(name, bufs, plan, n_copies):
    nb = len(bufs)

    def body(*refs):
        send_sems, recv_sems = refs[nb], refs[nb + 1]
        for k, (src, dst, dev) in enumerate(plan(refs[:nb])):
            _remote(src, dst, send_sems, recv_sems, k, dev).start()
        refs[-1][...] = jnp.zeros_like(refs[-1])

    outs = pl.pallas_call(
        body, name=name,
        out_shape=(pltpu.SemaphoreType.DMA((n_copies,)), pltpu.SemaphoreType.DMA((n_copies,)),
                   *[pltpu.HBM(b.shape, b.dtype) for b in bufs], SDS((8, LANES), F32)),
        in_specs=[HBM_SPEC] * nb, out_specs=(SEM_SPEC, SEM_SPEC, *[HBM_SPEC] * nb, VM),
        input_output_aliases={i: 2 + i for i in range(nb)},
        compiler_params=pltpu.CompilerParams(has_side_effects=EFFECT),
    )(*[pltpu.with_memory_space_constraint(b, pltpu.HBM) for b in bufs])
    return outs[0], outs[1], list(outs[2:2 + nb]), outs[-1]


def split_wait(name, send_sems, recv_sems, bufs, after, plan):
    nb = len(bufs)

    def body(*refs):
        send_ref, recv_ref = refs[nb], refs[nb + 1]
        for k, (src, dst, dev) in enumerate(plan(refs[:nb])):
            cp = _remote(src, dst, send_ref, recv_ref, k, dev)
            cp.wait_send()
            cp.wait_recv()

    outs = pl.pallas_call(
        body, name=name, out_shape=[pltpu.HBM(b.shape, b.dtype) for b in bufs],
        in_specs=[HBM_SPEC] * nb + [SEM_SPEC, SEM_SPEC, ANY], out_specs=[HBM_SPEC] * nb,
        input_output_aliases={i: i for i in range(nb)},
        compiler_params=pltpu.CompilerParams(has_side_effects=EFFECT),
    )(*bufs, send_sems, recv_sems, after)
    return list(outs)


def tie(x, tokens, name):
    def body(*refs):
        pass

    return pl.pallas_call(
        body, name=name, out_shape=SDS(x.shape, x.dtype), in_specs=[ANY] * (1 + len(tokens)), out_specs=ANY,
        input_output_aliases={0: 0},
    )(x, *tokens)


def plan_gather_ici(n, rows):
    def plan(refs):
        x, y, c = _coords()
        me = 2 * x + y
        out = []
        for i in range(n):
            hr = rows[i] // 2
            mine = pl.ds(pl.multiple_of(c * hr, 8), hr)
            out.append((refs[i], refs[n + i].at[me], (x, y, 1 - c)))
            for m in CHIP_FLIPS:
                out.append((refs[i].at[mine], refs[n + i].at[me, mine], (*_flip(x, y, m), c)))
        return out
    return plan


def plan_gather_d2d(n, rows):
    def plan(refs):
        x, y, c = _coords()
        me = 2 * x + y
        out = []
        for i in range(n):
            hr = rows[i] // 2
            mine = pl.ds(pl.multiple_of(c * hr, 8), hr)
            for m in CHIP_FLIPS:
                slot = refs[i].at[me ^ m, mine]
                out.append((slot, slot, (x, y, 1 - c)))
        return out
    return plan


def plan_swap(n, rows):
    def plan(refs):
        x, y, c = _coords()
        out = []
        for i in range(n):
            hr = rows[i] // 2
            other = pl.ds(pl.multiple_of((1 - c) * hr, 8), hr)
            out.append((refs[i].at[pl.ds(0, N_CHIPS), other], refs[n + i], (x, y, 1 - c)))
        return out
    return plan


def plan_exchange(n):
    def plan(refs):
        x, y, c = _coords()
        me = 2 * x + y
        out = []
        for i in range(n):
            for m in CHIP_FLIPS:
                out.append((refs[i].at[me ^ m], refs[n + i].at[me], (*_flip(x, y, m), c)))
        return out
    return plan


def plan_share(n, rows):
    def plan(refs):
        x, y, c = _coords()
        out = []
        for i in range(n):
            hr = rows[i] // 2
            mine = refs[i].at[pl.ds(pl.multiple_of(c * hr, 8), hr)]
            out.append((mine, mine, (x, y, 1 - c)))
        return out
    return plan


def reduce_scatter(grads, c_arr, tag):
    got = swap_halves(grads, "rs_swap_" + tag)
    parts = [half_sum(g, r, c_arr, "rs_half_sum") for g, r in zip(grads, got)]
    slots = exchange_parts(parts, "rs_exchange_" + tag)
    return share_halves([slot_sum(s, c_arr, "rs_slot_sum") for s in slots], "rs_share_" + tag)


VM = pl.BlockSpec(memory_space=pltpu.VMEM)


def small_gather(v, name):
    def body(v_ref, out_ref, send_sems, recv_sems):
        x, y, c = _coords()
        me = 2 * x + y
        out_ref[me] = v_ref[...]
        cps = []
        for j, m in enumerate(CHIP_FLIPS):
            cp = _remote(v_ref, out_ref.at[me], send_sems, recv_sems, j, (*_flip(x, y, m), c))
            cp.start()
            cps.append(cp)
        for cp in cps:
            cp.wait()

    return pl.pallas_call(
        body, in_specs=[VM], out_specs=VM, out_shape=SDS((N_CHIPS,) + v.shape, v.dtype),
        scratch_shapes=[pltpu.SemaphoreType.DMA((3,)), pltpu.SemaphoreType.DMA((3,))],
        compiler_params=pltpu.CompilerParams(vmem_limit_bytes=VMEM_LIMIT_BYTES), name=name,
    )(v)


def small_allreduce(v, name):
    def body(v_ref, out_ref, sib_buf, slots, send_sems, recv_sems):
        x, y, c = _coords()
        me = 2 * x + y
        swap = _remote(v_ref, sib_buf, send_sems, recv_sems, 0, (x, y, 1 - c))
        swap.start()
        swap.wait()
        slots[me] = v_ref[...] + sib_buf[...]
        cps = []
        for j, m in enumerate(CHIP_FLIPS):
            cp = _remote(slots.at[me], slots.at[me], send_sems, recv_sems, 1 + j, (*_flip(x, y, m), c))
            cp.start()
            cps.append(cp)
        for cp in cps:
            cp.wait()
        out_ref[...] = (slots[0] + slots[1]) + (slots[2] + slots[3])

    return pl.pallas_call(
        body, in_specs=[VM], out_specs=VM, out_shape=SDS(v.shape, v.dtype),
        scratch_shapes=[pltpu.VMEM(v.shape, v.dtype), pltpu.VMEM((N_CHIPS,) + v.shape, v.dtype),
                        pltpu.SemaphoreType.DMA((4,)), pltpu.SemaphoreType.DMA((4,))],
        compiler_params=pltpu.CompilerParams(vmem_limit_bytes=VMEM_LIMIT_BYTES), name=name,
    )(v)


def adamw(w, g_list, m, v, name):
    nl, rows, cols = w.shape
    tr = _sum_rows_tile(rows, cols) if rows % 16 == 0 else rows
    bc1 = 1.0 - ADAM_B1 ** ADAM_STEP
    bc2 = 1.0 - ADAM_B2 ** ADAM_STEP

    def body(*refs):
        w_ref, m_ref, v_ref = refs[:3]
        g_refs = refs[3:3 + nl]
        go_ref, d_ref, mo_ref, vo_ref = refs[3 + nl:]
        layer = pl.program_id(0)
        for l in range(nl):
            @pl.when(layer == l)
            def _(l=l):
                g = g_refs[l][...]
                m_new = ADAM_B1 * m_ref[...] + (1.0 - ADAM_B1) * g
                v_new = ADAM_B2 * v_ref[...] + (1.0 - ADAM_B2) * (g * g)
                m_hat = m_new / bc1
                v_hat = v_new / bc2
                go_ref[...] = g
                d_ref[...] = -ADAM_LR * (m_hat / (jnp.sqrt(v_hat) + ADAM_EPS) + ADAM_WD * w_ref[...])
                mo_ref[...] = m_new
                vo_ref[...] = v_new

    stk = BS((None, tr, cols), lambda l, i: (l, i, 0))
    flat = BS((tr, cols), lambda l, i: (i, 0))
    out = SDS((nl, rows, cols), F32)
    return pl.pallas_call(
        body, grid=(nl, rows // tr), in_specs=[stk] * 3 + [flat] * nl, out_specs=[stk] * 4,
        out_shape=[out] * 4, compiler_params=_cp("parallel", "parallel"), name=name,
    )(w, m, v, *g_list)


WEIGHTS = ["mem_norm_g", "a_pre_mix_g", "a_post_mix_g", "a_pre_ffn_g", "a_post_ffn_g", "a_w_in", "a_conv_w",
           "a_conv_b", "a_gate_a_w", "a_gate_a_b", "a_gate_x_w", "a_gate_x_b", "a_lambda", "a_w_mem_kv", "a_w_out",
           "a_w_ffn_in", "a_w_ffn_out", "kv_norm_g", "w_kv_shared", "b_pre_mix_g", "b_post_mix_g", "b_pre_ffn_g",
           "b_post_ffn_g", "b_w_in", "b_w_mem_kv", "b_w_out", "b_w_ffn_in", "b_w_ffn_out"]
BIG = {"a_w_in": True, "a_w_mem_kv": False, "a_w_out": False, "a_w_ffn_in": True, "a_w_ffn_out": False,
       "w_kv_shared": True, "b_w_in": False, "b_w_mem_kv": False, "b_w_out": False, "b_w_ffn_in": True,
       "b_w_ffn_out": False}
SHARDED_SMALL = ["a_pre_mix_g", "a_post_mix_g", "a_pre_ffn_g", "a_post_ffn_g", "a_conv_w", "a_conv_b", "a_gate_a_b",
                 "a_gate_x_b", "a_lambda"]
REPL_SMALL = ["mem_norm_g", "kv_norm_g", "b_pre_mix_g", "b_post_mix_g", "b_pre_ffn_g", "b_post_ffn_g", "a_gate_a_w",
              "a_gate_x_w"]
LANES = 128


def _pack(arrs, row_multiple=8):
    flat = jnp.concatenate([a.reshape(-1) for a in arrs])
    pad = -flat.shape[0] % (LANES * row_multiple)
    if pad:
        flat = jnp.concatenate([flat, jnp.zeros((pad,), flat.dtype)])
    return flat.reshape(-1, LANES)


def _unpack(packed, shapes):
    flat = packed.reshape(-1)
    out, pos = [], 0
    for sh in shapes:
        size = math.prod(sh)
        out.append(flat[pos:pos + size].reshape(sh))
        pos += size
    return out


def kernel(x, mem, mem_norm_g, a_pre_mix_g, a_post_mix_g, a_pre_ffn_g, a_post_ffn_g, a_w_in, a_conv_w, a_conv_b,
           a_gate_a_w, a_gate_a_b, a_gate_x_w, a_gate_x_b, a_lambda, a_w_mem_kv, a_w_out, a_w_ffn_in, a_w_ffn_out,
           kv_norm_g, w_kv_shared, b_pre_mix_g, b_post_mix_g, b_pre_ffn_g, b_post_ffn_g, b_w_in, b_w_mem_kv, b_w_out,
           b_w_ffn_in, b_w_ffn_out, loss_target, m_mem_norm_g, m_a_pre_mix_g, m_a_post_mix_g, m_a_pre_ffn_g,
           m_a_post_ffn_g, m_a_w_in, m_a_conv_w, m_a_conv_b, m_a_gate_a_w, m_a_gate_a_b, m_a_gate_x_w, m_a_gate_x_b,
           m_a_lambda, m_a_w_mem_kv, m_a_w_out, m_a_w_ffn_in, m_a_w_ffn_out, m_kv_norm_g, m_w_kv_shared, m_b_pre_mix_g,
           m_b_post_mix_g, m_b_pre_ffn_g, m_b_post_ffn_g, m_b_w_in, m_b_w_mem_kv, m_b_w_out, m_b_w_ffn_in, m_b_w_ffn_out,
           v_mem_norm_g, v_a_pre_mix_g, v_a_post_mix_g, v_a_pre_ffn_g, v_a_post_ffn_g, v_a_w_in, v_a_conv_w, v_a_conv_b,
           v_a_gate_a_w, v_a_gate_a_b, v_a_gate_x_w, v_a_gate_x_b, v_a_lambda, v_a_w_mem_kv, v_a_w_out, v_a_w_ffn_in,
           v_a_w_ffn_out, v_kv_norm_g, v_w_kv_shared, v_b_pre_mix_g, v_b_post_mix_g, v_b_pre_ffn_g, v_b_post_ffn_g,
           v_b_w_in, v_b_w_mem_kv, v_b_w_out, v_b_w_ffn_in, v_b_w_ffn_out):
    a = dict(locals())
    xi, yi, ci = _coords()
    chip = 2 * xi + yi
    c_arr = jnp.stack([ci, chip]).astype(jnp.int32)

    got = small_gather(_pack([a[n] for n in SHARDED_SMALL]), "small_gather")
    per_chip = [_unpack(got[s], [a[n].shape for n in SHARDED_SMALL]) for s in range(N_CHIPS)]
    small = {n: jnp.concatenate([per_chip[s][k] for s in range(N_CHIPS)], axis=-1)
             for k, n in enumerate(SHARDED_SMALL)}
    small.update({n: a[n] for n in REPL_SMALL})

    groups = []
    for l in range(4):
        p, j = ("a", l) if l < 2 else ("b", l - 2)
        groups.append([(p + "_" + n, j) for n in ("w_in", "w_mem_kv", "w_out")])
        groups.append([(p + "_" + n, j) for n in ("w_ffn_in", "w_ffn_out")])
    groups[3].append(("w_kv_shared", None))
    big = {n: [None, None] for n in BIG if n != "w_kv_shared"}
    gs, gb = {}, {}
    reduced = {n: [None, None] for n in BIG if n != "w_kv_shared"}

    def put(store, n, j, val):
        if j is None:
            store[n] = val
        else:
            store[n][j] = val

    class Exchange:
        def __init__(self):
            self.state = {}

        def gather_ici(self, g):
            shards = [(a[n] if j is None else a[n][j]).astype(MXU) for n, j in groups[g]]
            rows = [sh.shape[0] for sh in shards]
            lands = [lax.empty((N_CHIPS,) + sh.shape, sh.dtype) for sh in shards]
            plan = plan_gather_ici(len(shards), rows)
            ss, rs, bufs, tok = split_start("gather_ici_%d" % g, shards + lands, plan, 4 * len(shards))
            self.state["g", g] = (ss, rs, bufs, plan, rows)
            return tok

        def gather_d2d(self, g, after):
            ss, rs, bufs, plan, rows = self.state.pop(("g", g))
            n = len(rows)
            outs = split_wait("gather_ici_wait_%d" % g, ss, rs, bufs, after, plan)[n:]
            plan = plan_gather_d2d(n, rows)
            ss, rs, bufs, tok = split_start("gather_d2d_%d" % g, outs, plan, 3 * n)
            self.state["g", g] = (ss, rs, bufs, plan)
            return tok

        def gather_done(self, g, after):
            ss, rs, bufs, plan = self.state.pop(("g", g))
            outs = split_wait("gather_d2d_wait_%d" % g, ss, rs, bufs, after, plan)
            for (n, j), w in zip(groups[g], outs):
                put(big, n, j, w if BIG[n] else w.reshape(-1, w.shape[-1]))

        def rs_swap(self, g):
            grads = []
            for n, j in groups[g]:
                gr = gb[n] if j is None else gb[n][j]
                grads.append(gr if BIG[n] else gr.reshape(N_CHIPS, gr.shape[0] // N_CHIPS, gr.shape[1]))
            rows = [gr.shape[1] for gr in grads]
            lands = [lax.empty((N_CHIPS, gr.shape[1] // 2, gr.shape[2]), F32) for gr in grads]
            plan = plan_swap(len(grads), rows)
            ss, rs, bufs, tok = split_start("rs_swap_%d" % g, grads + lands, plan, len(grads))
            self.state["r", g] = (ss, rs, bufs, plan, rows)
            return tok

        def rs_exchange(self, g, after):
            ss, rs, bufs, plan, rows = self.state.pop(("r", g))
            n = len(rows)
            bufs = split_wait("rs_swap_wait_%d" % g, ss, rs, bufs, after, plan)
            sums = [half_sum(gr, got, c_arr, "rs_half_sum") for gr, got in zip(bufs[:n], bufs[n:])]
            plan = plan_exchange(n)
            ss, rs, bufs, tok = split_start("rs_exchange_%d" % g, [p for p, _ in sums] + [s for _, s in sums], plan,
                                            3 * n)
            self.state["r", g] = (ss, rs, bufs, plan, rows)
            return tok

        def rs_share(self, g, after):
            ss, rs, bufs, plan, rows = self.state.pop(("r", g))
            n = len(rows)
            slots = split_wait("rs_exchange_wait_%d" % g, ss, rs, bufs, after, plan)[n:]
            fulls = [slot_sum(s, c_arr, "rs_slot_sum") for s in slots]
            plan = plan_share(n, rows)
            ss, rs, bufs, tok = split_start("rs_share_%d" % g, fulls, plan, n)
            self.state["r", g] = (ss, rs, bufs, plan)
            return tok

        def rs_done(self, g, after):
            ss, rs, bufs, plan = self.state.pop(("r", g))
            outs = split_wait("rs_share_wait_%d" % g, ss, rs, bufs, after, plan)
            for (n, j), r in zip(groups[g], outs):
                put(reduced, n, j, r)

        def hook(self, where, l, after):
            mix, ffn = 2 * l, 2 * l + 1
            toks = []
            if where == "fwd_begin":
                if l == 0:
                    tok = self.gather_ici(mix)
                    tok = self.gather_d2d(mix, tok)
                    self.gather_done(mix, tok)
                toks.append(self.gather_ici(ffn))
            elif where == "fwd_q1":
                toks.append(self.gather_d2d(ffn, after))
            elif where == "fwd_mid":
                self.gather_done(ffn, after)
                if l < 3:
                    toks.append(self.gather_ici(mix + 2))
            elif where == "fwd_q3":
                if l < 3:
                    toks.append(self.gather_d2d(mix + 2, after))
            elif where == "fwd_end":
                if l < 3:
                    self.gather_done(mix + 2, after)
            elif where == "bwd_begin":
                if l < 3:
                    self.rs_done(ffn + 2, after)
                    toks.append(self.rs_exchange(mix + 2, after))
            elif where == "bwd_mid1":
                if l < 3:
                    toks.append(self.rs_share(mix + 2, after))
            elif where == "bwd_mid2":
                if l < 3:
                    self.rs_done(mix + 2, after)
                toks.append(self.rs_swap(ffn))
            elif where == "bwd_m1":
                toks.append(self.rs_exchange(ffn, after))
            elif where == "bwd_end":
                toks.append(self.rs_share(ffn, after))
                toks.append(self.rs_swap(mix))
                if l == 0:
                    self.rs_done(ffn, toks[0])
                    tok = self.rs_exchange(mix, toks[1])
                    tok = self.rs_share(mix, tok)
                    self.rs_done(mix, tok)
                    toks = []
            else:
                raise ValueError(where)
            return toks

    loss_parts, dx = _fwd_bwd(x[0], mem[0], loss_target[0], small, big, gs, gb, Exchange())
    loss = lax.psum(jnp.sum(loss_parts) * (0.5 / D), ("x", "y", "c"))

    res = {}
    reduced["w_kv_shared"] = [reduced["w_kv_shared"]]
    for n in BIG:
        shape = a[n].shape
        rows, cols = shape[-2], shape[-1]
        stk = (-1, rows, cols)
        outs = adamw(a[n].reshape(stk), reduced[n], a["m_" + n].reshape(stk), a["v_" + n].reshape(stk), "adamw")
        res[n] = [o.reshape(shape) for o in outs]

    def full(n):
        g = gs[n]
        return jnp.stack(g) if isinstance(g, list) else g

    order = SHARDED_SMALL + REPL_SMALL
    full_shapes = [full(n).shape for n in order]
    summed = _unpack(small_allreduce(_pack([full(n) for n in order]), "small_allreduce"), full_shapes)
    mine = []
    for n, g in zip(order, summed):
        if n in SHARDED_SMALL:
            width = a[n].shape[-1]
            g = lax.dynamic_slice_in_dim(g, chip * width, width, axis=g.ndim - 1)
        mine.append(g.reshape(a[n].shape))
    shapes = [a[n].shape for n in order]
    rm = 512
    outs = adamw(_pack([a[n] for n in order], rm)[None], [_pack(mine, rm)],
                 _pack([a["m_" + n] for n in order], rm)[None], _pack([a["v_" + n] for n in order], rm)[None],
                 "adamw_small")
    unpacked = [_unpack(o[0], shapes) for o in outs]
    for k, n in enumerate(order):
        res[n] = [u[k] for u in unpacked]

    return (loss, dx[None], *[res[n][0] for n in WEIGHTS], *[res[n][1] for n in WEIGHTS],
            *[res[n][2] for n in WEIGHTS], *[res[n][3] for n in WEIGHTS])
```

```python
import functools
import math

import jax
import jax.numpy as jnp
from jax import lax
from jax.experimental import pallas as pl
from jax.experimental.pallas import tpu as pltpu

D = 2048
HD = 128
MEM_W = 512
MEM_HEADS = 4
MIX_W = D - MEM_W
N_BLK = MIX_W // HD
D_FF = 5632
N_MEM = 256
RMS_EPS = 1e-6
NEG_INF = -1e30
LRU_C = 8.0
DIL_GROUPS = ((128, 1), (512, 4), (2048, 16))
Q_BLOCK = 128
SCALE = HD ** -0.5
N_CHIPS = 4
MXU_COLS = 256

ADAM_LR = 0.001
ADAM_B1 = 0.9
ADAM_B2 = 0.999
ADAM_EPS = 1e-08
ADAM_WD = 0.01
ADAM_STEP = 10

MXU = jnp.bfloat16
F32 = jnp.float32
VMEM_LIMIT_BYTES = 56 * 1024 * 1024

BS = pl.BlockSpec
SDS = jax.ShapeDtypeStruct
MESH = pl.DeviceIdType.MESH


def _cp(*sem):
    return pltpu.CompilerParams(dimension_semantics=sem or None, vmem_limit_bytes=VMEM_LIMIT_BYTES)


def _dot(a, b, dn=((1,), (0,))):
    return lax.dot_general(a, b, (dn, ((), ())), preferred_element_type=F32)


def _div(i, n):
    return lax.div(i, jnp.int32(n))


def _rem(i, n):
    return lax.rem(i, jnp.int32(n))


NN = ((1,), (0,))
NT = ((1,), (1,))
TN = ((0,), (0,))


def _sigmoid(z):
    return 1.0 / (1.0 + jnp.exp(-z))


def _log1p_pos(u):
    return jnp.where(u < 1e-2, u * (1.0 - u * (0.5 - u * (1.0 / 3.0))), jnp.log(1.0 + u))


def _neg_expm1(z):
    return jnp.where(z > -1e-2, -z * (1.0 + z * (0.5 + z * (1.0 / 6.0))), 1.0 - jnp.exp(z))


def _softplus(z):
    return jnp.maximum(z, 0.0) + _log1p_pos(jnp.exp(-jnp.abs(z)))


_GELU_C = math.sqrt(2.0 / math.pi)


def _gelu_and_grad(x):
    x2 = x * x
    t = jnp.tanh(_GELU_C * (x + 0.044715 * x * x2))
    g = 0.5 * x * (1.0 + t)
    dg = 0.5 * (1.0 + t) + 0.5 * x * (1.0 - t * t) * _GELU_C * (1.0 + 3.0 * 0.044715 * x2)
    return g, dg


def _row_tile(rows):
    return min(256, rows)


def norm_cast(x, g, name):
    rows = x.shape[0]
    tr = _row_tile(rows)

    def body(x_ref, g_ref, o_ref):
        xv = x_ref[...]
        r = lax.rsqrt(jnp.mean(xv * xv, axis=-1, keepdims=True) + RMS_EPS)
        o_ref[...] = (xv * r * g_ref[...]).astype(o_ref.dtype)

    return pl.pallas_call(
        body, grid=(rows // tr,),
        in_specs=[BS((tr, D), lambda i: (i, 0)), BS((1, D), lambda i: (0, 0))],
        out_specs=BS((tr, D), lambda i: (i, 0)),
        out_shape=SDS((rows, D), MXU), compiler_params=_cp("parallel"), name=name,
    )(x, g.reshape(1, D))


def resid_norm(h, y, g, name):
    rows = h.shape[0]
    tr = _row_tile(rows)

    def body(h_ref, y_ref, g_ref, o_ref):
        yv = y_ref[...]
        r = lax.rsqrt(jnp.mean(yv * yv, axis=-1, keepdims=True) + RMS_EPS)
        o_ref[...] = h_ref[...] + yv * r * g_ref[...]

    return pl.pallas_call(
        body, grid=(rows // tr,),
        in_specs=[BS((tr, D), lambda i: (i, 0)), BS((tr, D), lambda i: (i, 0)), BS((1, D), lambda i: (0, 0))],
        out_specs=BS((tr, D), lambda i: (i, 0)),
        out_shape=SDS((rows, D), F32), compiler_params=_cp("parallel"), name=name,
    )(h, y, g.reshape(1, D))


def resid_norm_next(h, y, g, g_next, name):
    rows = h.shape[0]
    tr = _row_tile(rows)

    def body(h_ref, y_ref, g_ref, gn_ref, o_ref, n_ref):
        yv = y_ref[...]
        r = lax.rsqrt(jnp.mean(yv * yv, axis=-1, keepdims=True) + RMS_EPS)
        hv = h_ref[...] + yv * r * g_ref[...]
        o_ref[...] = hv
        r2 = lax.rsqrt(jnp.mean(hv * hv, axis=-1, keepdims=True) + RMS_EPS)
        n_ref[...] = (hv * r2 * gn_ref[...]).astype(n_ref.dtype)

    row = BS((tr, D), lambda i: (i, 0))
    vec = BS((1, D), lambda i: (0, 0))
    return pl.pallas_call(
        body, grid=(rows // tr,), in_specs=[row, row, vec, vec], out_specs=[row, row],
        out_shape=[SDS((rows, D), F32), SDS((rows, D), MXU)], compiler_params=_cp("parallel"), name=name,
    )(h, y, g.reshape(1, D), g_next.reshape(1, D))


def norm_bwd(x, g, dy, res, out_dtype, name):
    rows = x.shape[0]
    tr = _row_tile(rows)
    has_res = res is not None

    def body(*refs):
        if has_res:
            x_ref, g_ref, dy_ref, res_ref, dx_ref, dg_ref = refs
        else:
            x_ref, g_ref, dy_ref, dx_ref, dg_ref = refs
        xv = x_ref[...]
        dyv = dy_ref[...].astype(F32)
        r = lax.rsqrt(jnp.mean(xv * xv, axis=-1, keepdims=True) + RMS_EPS)
        xhat = xv * r
        dxhat = dyv * g_ref[...]
        dx = r * (dxhat - xhat * jnp.mean(dxhat * xhat, axis=-1, keepdims=True))
        if has_res:
            dx = dx + res_ref[...]
        dx_ref[...] = dx.astype(dx_ref.dtype)

        @pl.when(pl.program_id(0) == 0)
        def _():
            dg_ref[...] = jnp.zeros_like(dg_ref)

        dg_ref[...] += jnp.sum(dyv * xhat, axis=0, keepdims=True)

    row = BS((tr, D), lambda i: (i, 0))
    vec = BS((1, D), lambda i: (0, 0))
    ins = [x, g.reshape(1, D), dy] + ([res] if has_res else [])
    dx, dg = pl.pallas_call(
        body, grid=(rows // tr,),
        in_specs=[row, vec, row] + ([row] if has_res else []),
        out_specs=[row, vec],
        out_shape=[SDS((rows, D), out_dtype), SDS((1, D), F32)],
        compiler_params=_cp("arbitrary"), name=name,
    )(*ins)
    return dx, dg.reshape(D)


def loss_head(y, target, name):
    rows = y.shape[0]
    tr = _row_tile(rows)

    def body(y_ref, t_ref, dy_ref, acc_ref):
        err = y_ref[...] - t_ref[...]
        dy_ref[...] = err * (1.0 / D)

        @pl.when(pl.program_id(0) == 0)
        def _():
            acc_ref[...] = jnp.zeros_like(acc_ref)

        acc_ref[...] += jnp.sum(err * err, axis=0, keepdims=True)

    row = BS((tr, D), lambda i: (i, 0))
    dy, acc = pl.pallas_call(
        body, grid=(rows // tr,), in_specs=[row, row],
        out_specs=[row, BS((1, D), lambda i: (0, 0))],
        out_shape=[SDS((rows, D), F32), SDS((1, D), F32)],
        compiler_params=_cp("arbitrary"), name=name,
    )(y, target)
    return acc, dy


def _mm_call(ins, in_specs, pick, dn, grid, o_spec, out_sds, name):
    gk = grid[2]
    n_in = len(ins)

    def body(*refs):
        o_ref = refs[n_in]
        k = pl.program_id(2)

        def step(a_ref, b_ref):
            p = _dot(a_ref[...], b_ref[...], dn)
            if gk == 1:
                o_ref[...] = p.astype(o_ref.dtype)
            else:
                acc = o_ref if out_sds.dtype == F32 else refs[n_in + 1]

                @pl.when(k == 0)
                def _():
                    acc[...] = p

                @pl.when(k > 0)
                def _():
                    acc[...] += p

                if acc is not o_ref:
                    @pl.when(k == gk - 1)
                    def _():
                        o_ref[...] = acc[...].astype(o_ref.dtype)

        pick(refs[:n_in], k, step)

    scratch = []
    if gk > 1 and out_sds.dtype != F32:
        scratch = [pltpu.VMEM(o_spec.block_shape[-2:], F32)]
    return pl.pallas_call(
        body, grid=grid, in_specs=in_specs, out_specs=o_spec, out_shape=out_sds,
        scratch_shapes=scratch, compiler_params=_cp("parallel", "parallel", "arbitrary"), name=name,
    )(*ins)


def _pick2(refs, k, step):
    step(refs[0], refs[1])


def mm_nn(a, w, *, tm, tn, tk, out_dtype, name):
    m, kdim = a.shape
    if w.ndim == 3:
        c = w.shape[2]
        n = N_CHIPS * c
        per = c // tn
        b_spec = BS((None, tk, tn), lambda i, j, k: (_div(j, per), k, _rem(j, per)))
    else:
        n = w.shape[1]
        b_spec = BS((tk, tn), lambda i, j, k: (k, j))
    grid = (m // tm, n // tn, kdim // tk)
    return _mm_call([a, w], [BS((tm, tk), lambda i, j, k: (i, k)), b_spec], _pick2, NN, grid,
                    BS((tm, tn), lambda i, j, k: (i, j)), SDS((m, n), out_dtype), name)


def mm_nt(a_list, w, *, tm, tn, tk, out_dtype, name):
    m = a_list[0].shape[0]
    ka = a_list[0].shape[1]
    n_a = len(a_list)
    kdim = ka * n_a
    if w.ndim == 3:
        c = w.shape[2]
        n = w.shape[1]
        per = c // tk
        b_spec = BS((None, tn, tk), lambda i, j, k: (_div(k, per), j, _rem(k, per)))
    else:
        n = w.shape[0]
        b_spec = BS((tn, tk), lambda i, j, k: (j, k))
    gk = kdim // tk
    half = gk // n_a
    grid = (m // tm, n // tn, gk)
    if n_a == 1:
        a_specs = [BS((tm, tk), lambda i, j, k: (i, k))]
        pick = lambda refs, k, step: step(refs[0], refs[1])
    else:
        a_specs = [BS((tm, tk), lambda i, j, k: (i, jnp.minimum(k, half - 1))),
                   BS((tm, tk), lambda i, j, k: (i, jnp.maximum(k - half, 0)))]

        def pick(refs, k, step):
            @pl.when(k < half)
            def _():
                step(refs[0], refs[2])

            @pl.when(k >= half)
            def _():
                step(refs[1], refs[2])

    return _mm_call(list(a_list) + [w], a_specs + [b_spec], pick, NT, grid,
                    BS((tm, tn), lambda i, j, k: (i, j)), SDS((m, n), out_dtype), name)


def mm_tn(a, b_list, *, t1, tn, ts, col_shards, name):
    s, k1 = a.shape
    nb = b_list[0].shape[1]
    n_b = len(b_list)
    n = nb * n_b
    gn = n // tn
    half = gn // n_b
    grid = (k1 // t1, gn, s // ts)
    if col_shards:
        c = n // N_CHIPS
        per = c // tn
        o_spec = BS((None, t1, tn), lambda i, j, k: (_div(j, per), i, _rem(j, per)))
        out_sds = SDS((N_CHIPS, k1, c), F32)
    else:
        o_spec = BS((t1, tn), lambda i, j, k: (i, j))
        out_sds = SDS((k1, n), F32)
    a_spec = BS((ts, t1), lambda i, j, k: (k, i))
    if n_b == 1:
        b_specs = [BS((ts, tn), lambda i, j, k: (k, j))]
        pick = lambda refs, k, step: step(refs[0], refs[1])
    else:
        b_specs = [BS((ts, tn), lambda i, j, k: (k, jnp.minimum(j, half - 1))),
                   BS((ts, tn), lambda i, j, k: (k, jnp.maximum(j - half, 0)))]

        def pick(refs, k, step):
            j = pl.program_id(1)

            @pl.when(j < half)
            def _():
                step(refs[0], refs[1])

            @pl.when(j >= half)
            def _():
                step(refs[0], refs[2])

    return _mm_call([a] + list(b_list), [a_spec] + b_specs, pick, TN, grid, o_spec, out_sds, name)


def ffn_in_fwd(hn, w, name):
    s = hn.shape[0]
    tm = min(512, s)
    tn = D_FF // 4

    def body(a_ref, wg_ref, wu_ref, g_ref, u_ref, act_ref):
        a = a_ref[...]
        g = _dot(a, wg_ref[...])
        u = _dot(a, wu_ref[...])
        g_ref[...] = g
        u_ref[...] = u
        act_ref[...] = (g * _sigmoid(g) * u).astype(act_ref.dtype)

    tile = BS((tm, tn), lambda j, i: (i, j))
    return pl.pallas_call(
        body, grid=(4, s // tm),
        in_specs=[BS((tm, D), lambda j, i: (i, 0)),
                  BS((None, D, tn), lambda j, i: (_div(j, 2), 0, _rem(j, 2))),
                  BS((None, D, tn), lambda j, i: (2 + _div(j, 2), 0, _rem(j, 2)))],
        out_specs=[tile, tile, tile],
        out_shape=[SDS((s, D_FF), F32), SDS((s, D_FF), F32), SDS((s, D_FF), MXU)],
        compiler_params=_cp("parallel", "parallel"), name=name,
    )(hn, w, w)


def ffn_act_bwd(dy, w_out, g, u, name):
    s = dy.shape[0]
    tm = min(512, s)
    tn = D_FF // 4

    def body(dy_ref, w_ref, g_ref, u_ref, dg_ref, du_ref):
        dy = dy_ref[...]
        for c0 in range(0, tn, MXU_COLS):
            sl = slice(c0, min(c0 + MXU_COLS, tn))
            dact = _dot(dy, w_ref[sl, :], NT)
            gv = g_ref[:, sl]
            sg = _sigmoid(gv)
            dg_ref[:, sl] = (dact * u_ref[:, sl] * sg * (1.0 + gv * (1.0 - sg))).astype(dg_ref.dtype)
            du_ref[:, sl] = (dact * gv * sg).astype(du_ref.dtype)

    tile = BS((tm, tn), lambda j, i: (i, j))
    return pl.pallas_call(
        body, grid=(4, s // tm),
        in_specs=[BS((tm, D), lambda j, i: (i, 0)), BS((tn, D), lambda j, i: (j, 0)), tile, tile],
        out_specs=[tile, tile],
        out_shape=[SDS((s, D_FF), MXU), SDS((s, D_FF), MXU)],
        compiler_params=_cp("parallel", "parallel"), name=name,
    )(dy, w_out, g, u)


LRU_T = 256
HALO = 8


def _shift_down(x, k, fill):
    rows = x.shape[0]
    idx = lax.broadcasted_iota(jnp.int32, x.shape, 0)
    return jnp.where(idx < k, fill, pltpu.roll(x, k, 0))


def _shift_up(x, k, fill):
    rows = x.shape[0]
    idx = lax.broadcasted_iota(jnp.int32, x.shape, 0)
    return jnp.where(idx >= rows - k, fill, pltpu.roll(x, rows - k, 0))


def _conv_taps(xcat):
    rows = xcat.shape[0]
    taps = []
    for k in range(4):
        off = HALO - 3 + k
        taps.append(xcat[off:off + LRU_T] if off == HALO else pltpu.roll(xcat, rows - off, 0)[:LRU_T])
    return taps


def _gates(xc, wa_ref, ba, wx_ref, bx, lam, za_ref, zx_ref):
    xm = xc.astype(MXU)
    for n in range(N_BLK):
        sl = slice(n * HD, (n + 1) * HD)
        za_ref[:, sl] = _dot(xm[:, sl], wa_ref[n])
        zx_ref[:, sl] = _dot(xm[:, sl], wx_ref[n])
    ra = _sigmoid(za_ref[...] + ba)
    ii = _sigmoid(zx_ref[...] + bx)
    sp = _softplus(-lam)
    log_a = -LRU_C * ra * sp
    a = jnp.exp(log_a)
    mult = jnp.sqrt(_neg_expm1(2.0 * log_a))
    return ra, ii, sp, a, mult


def lru_fwd(proj, conv_w, conv_b, wa, ba, wx, bx, lam, name):
    s = proj.shape[0]
    c = MIX_W
    nblk = s // LRU_T
    hpb = LRU_T // HALO

    def body(x_ref, halo_ref, cw_ref, cb_ref, wa_ref, ba_ref, wx_ref, bx_ref, lam_ref,
             xc_ref, h_ref, carry, za_ref, zx_ref):
        i = pl.program_id(0)

        @pl.when(i == 0)
        def _():
            carry[...] = jnp.zeros_like(carry)

        halo = jnp.where(i == 0, 0.0, halo_ref[...])
        xcat = jnp.concatenate([halo, x_ref[...]], axis=0)
        taps = _conv_taps(xcat)
        xc = cb_ref[...] + sum(cw_ref[k:k + 1, :] * taps[k] for k in range(4))
        xc_ref[...] = xc
        _, ii, _, a, mult = _gates(xc, wa_ref, ba_ref[...], wx_ref, bx_ref[...], lam_ref[...], za_ref, zx_ref)
        b = mult * (ii * xc)
        sh = 1
        while sh < LRU_T:
            b = a * _shift_down(b, sh, 0.0) + b
            a = a * _shift_down(a, sh, 1.0)
            sh *= 2
        h = b + a * carry[HALO - 1:HALO, :]
        h_ref[...] = h
        carry[...] = h[LRU_T - HALO:, :]

    def full(shape):
        return BS(shape, lambda i: (0,) * len(shape))

    blk = BS((LRU_T, c), lambda i: (i, 0))
    return pl.pallas_call(
        body, grid=(nblk,),
        in_specs=[blk, BS((HALO, c), lambda i: (jnp.maximum(i * hpb - 1, 0), 0)),
                  full((4, c)), full((1, c)), full((N_BLK, HD, HD)), full((1, c)),
                  full((N_BLK, HD, HD)), full((1, c)), full((1, c))],
        out_specs=[blk, blk],
        out_shape=[SDS((s, c), F32), SDS((s, c), F32)],
        scratch_shapes=[pltpu.VMEM((HALO, c), F32), pltpu.VMEM((LRU_T, c), F32), pltpu.VMEM((LRU_T, c), F32)],
        compiler_params=_cp("arbitrary"), name=name,
    )(proj, proj, conv_w, conv_b.reshape(1, c), wa.astype(MXU), ba.reshape(1, c), wx.astype(MXU),
      bx.reshape(1, c), lam.reshape(1, c))


def lru_mix_prep(h, proj, m, name):
    s = h.shape[0]
    tr = _row_tile(s)

    def body(h_ref, gb_ref, m_ref, o_ref):
        ge, _ = _gelu_and_grad(gb_ref[...])
        o_ref[:, :MIX_W] = (h_ref[...] * ge).astype(o_ref.dtype)
        o_ref[:, MIX_W:] = m_ref[...]

    return pl.pallas_call(
        body, grid=(s // tr,),
        in_specs=[BS((tr, MIX_W), lambda i: (i, 0)), BS((tr, MIX_W), lambda i: (i, 1)),
                  BS((tr, MEM_W), lambda i: (i, 0))],
        out_specs=BS((tr, D), lambda i: (i, 0)), out_shape=SDS((s, D), MXU),
        compiler_params=_cp("parallel"), name=name,
    )(h, proj, m)


def lru_bwd(dym, proj, xc, hl, dqm, conv_w, wa, ba, wx, bx, lam, name):
    s = proj.shape[0]
    c = MIX_W
    nblk = s // LRU_T
    hpb = LRU_T // HALO
    wa_m = wa.astype(MXU)
    wx_m = wx.astype(MXU)

    def body(dy_ref, x_ref, xhalo_ref, gb_ref, xc_ref, h_ref, hhalo_ref, dqm_ref,
             cw_ref, wa_ref, ba_ref, wx_ref, bx_ref, lam_ref,
             dproj_ref, dcw_ref, dcb_ref, dwa_ref, dba_ref, dwx_ref, dbx_ref, dlam_ref,
             g_next, a_next, dxc_next, za_ref, zx_ref, dxc_ref):
        i = pl.program_id(0)

        @pl.when(i == 0)
        def _():
            g_next[...] = jnp.zeros_like(g_next)
            a_next[...] = jnp.zeros_like(a_next)
            dxc_next[...] = jnp.zeros_like(dxc_next)
            for r in (dcw_ref, dcb_ref, dwa_ref, dba_ref, dwx_ref, dbx_ref, dlam_ref):
                r[...] = jnp.zeros_like(r)

        first = i == nblk - 1
        xc = xc_ref[...]
        lam = lam_ref[...]
        ra, ii, sp, a, mult = _gates(xc, wa_ref, ba_ref[...], wx_ref, bx_ref[...], lam, za_ref, zx_ref)
        hl_v = h_ref[...]
        ge, dge = _gelu_and_grad(gb_ref[...])
        dyl = dy_ref[...]
        dhl = dyl * ge
        dproj_ref[:, c:2 * c] = (dyl * hl_v * dge).astype(dproj_ref.dtype)
        dproj_ref[:, 2 * c:] = dqm_ref[...]

        an = _shift_up(a, 1, 0.0)
        last_row = lax.broadcasted_iota(jnp.int32, a.shape, 0) == LRU_T - 1
        an = jnp.where(last_row, a_next[0:1, :], an)
        gb_acc = dhl
        sh = 1
        while sh < LRU_T:
            gb_acc = an * _shift_up(gb_acc, sh, 0.0) + gb_acc
            an = an * _shift_up(an, sh, 1.0)
            sh *= 2
        g = gb_acc + an * g_next[0:1, :]
        g_next[...] = g[:HALO, :]
        a_next[...] = a[:HALO, :]

        hhalo = jnp.where(first, 0.0, hhalo_ref[...])
        h_prev = _shift_down(hl_v, 1, 0.0)
        first_row = lax.broadcasted_iota(jnp.int32, a.shape, 0) == 0
        h_prev = jnp.where(first_row, hhalo[HALO - 1:HALO, :], h_prev)
        da = g * h_prev
        ixc = ii * xc
        dmult = g * ixc
        dii = g * mult * xc
        dxc = g * mult * ii
        dlog_a = (da - dmult * a / mult) * a
        dra = dlog_a * (-LRU_C) * sp
        dlam_ref[...] += jnp.sum(dlog_a * ra, axis=0, keepdims=True) * (LRU_C * _sigmoid(-lam))
        dza = dra * ra * (1.0 - ra)
        dzx = dii * ii * (1.0 - ii)
        dba_ref[...] += jnp.sum(dza, axis=0, keepdims=True)
        dbx_ref[...] += jnp.sum(dzx, axis=0, keepdims=True)
        xm = xc.astype(MXU)
        dza_m = dza.astype(MXU)
        dzx_m = dzx.astype(MXU)
        for n in range(N_BLK):
            sl = slice(n * HD, (n + 1) * HD)
            dwa_ref[n] += _dot(xm[:, sl], dza_m[:, sl], TN)
            dwx_ref[n] += _dot(xm[:, sl], dzx_m[:, sl], TN)
            dxc_ref[:, sl] = _dot(dza_m[:, sl], wa_ref[n], NT) + _dot(dzx_m[:, sl], wx_ref[n], NT)
        dxc = dxc + dxc_ref[...]

        dcat = jnp.concatenate([dxc, dxc_next[...]], axis=0)
        rows = dcat.shape[0]
        dxb = cw_ref[3:4, :] * dxc
        for k in range(3):
            dxb = dxb + cw_ref[k:k + 1, :] * pltpu.roll(dcat, rows - (3 - k), 0)[:LRU_T]
        dproj_ref[:, :c] = dxb.astype(dproj_ref.dtype)
        dxc_next[...] = dxc[:HALO, :]

        xhalo = jnp.where(first, 0.0, xhalo_ref[...])
        taps = _conv_taps(jnp.concatenate([xhalo, x_ref[...]], axis=0))
        for k in range(4):
            dcw_ref[k:k + 1, :] += jnp.sum(dxc * taps[k], axis=0, keepdims=True)
        dcb_ref[...] += jnp.sum(dxc, axis=0, keepdims=True)

    def full(shape):
        return BS(shape, lambda i: (0,) * len(shape))

    def rev(i):
        return nblk - 1 - i

    blk0 = BS((LRU_T, c), lambda i: (rev(i), 0))
    blk1 = BS((LRU_T, c), lambda i: (rev(i), 1))
    halo = BS((HALO, c), lambda i: (jnp.maximum(rev(i) * hpb - 1, 0), 0))
    outs = pl.pallas_call(
        body, grid=(nblk,),
        in_specs=[blk0, blk0, halo, blk1, blk0, blk0, halo, BS((LRU_T, MEM_W), lambda i: (rev(i), 0)),
                  full((4, c)), full((N_BLK, HD, HD)), full((1, c)), full((N_BLK, HD, HD)), full((1, c)),
                  full((1, c))],
        out_specs=[BS((LRU_T, 2 * c + MEM_W), lambda i: (rev(i), 0)), full((4, c)), full((1, c)),
                   full((N_BLK, HD, HD)), full((1, c)), full((N_BLK, HD, HD)), full((1, c)), full((1, c))],
        out_shape=[SDS((s, 2 * c + MEM_W), MXU), SDS((4, c), F32), SDS((1, c), F32),
                   SDS((N_BLK, HD, HD), F32), SDS((1, c), F32), SDS((N_BLK, HD, HD), F32), SDS((1, c), F32),
                   SDS((1, c), F32)],
        scratch_shapes=[pltpu.VMEM((HALO, c), F32), pltpu.VMEM((HALO, c), F32), pltpu.VMEM((HALO, c), F32),
                        pltpu.VMEM((LRU_T, c), F32), pltpu.VMEM((LRU_T, c), F32), pltpu.VMEM((LRU_T, c), F32)],
        compiler_params=_cp("arbitrary"), name=name,
    )(dym, proj, proj, proj, xc, hl, hl, dqm, conv_w, wa_m, ba.reshape(1, c), wx_m, bx.reshape(1, c),
      lam.reshape(1, c))
    dproj, dcw, dcb, dwa, dba, dwx, dbx, dlam = outs
    return dproj, dcw, dcb.reshape(c), dwa, dba.reshape(c), dwx, dbx.reshape(c), dlam.reshape(c)


def _mem_probs(q, k):
    sc = _dot(q, k, NT) * SCALE
    e = jnp.exp(sc - jnp.max(sc, axis=-1, keepdims=True))
    return e / jnp.sum(e, axis=-1, keepdims=True)


def mem_attn_fwd(proj, q_col, kvm, name):
    s = proj.shape[0]
    tq = min(512, s)

    def body(q_ref, kv_ref, o_ref):
        q = q_ref[...].astype(MXU)
        for hh in range(MEM_HEADS):
            sl = slice(hh * HD, (hh + 1) * HD)
            p = _mem_probs(q[:, sl], kv_ref[:, sl])
            o_ref[:, sl] = _dot(p.astype(MXU), kv_ref[:, MEM_W + hh * HD:MEM_W + (hh + 1) * HD]).astype(o_ref.dtype)

    return pl.pallas_call(
        body, grid=(s // tq,),
        in_specs=[BS((tq, MEM_W), lambda i: (i, q_col)), BS((N_MEM, 2 * MEM_W), lambda i: (0, 0))],
        out_specs=BS((tq, MEM_W), lambda i: (i, 0)), out_shape=SDS((s, MEM_W), MXU),
        compiler_params=_cp("parallel"), name=name,
    )(proj, kvm)


def mem_attn_bwd(proj, q_col, kvm, dym, name):
    s = proj.shape[0]
    tq = min(512, s)

    def body(q_ref, kv_ref, do_ref, dq_ref, dkv_ref):
        @pl.when(pl.program_id(0) == 0)
        def _():
            dkv_ref[...] = jnp.zeros_like(dkv_ref)

        q = q_ref[...].astype(MXU)
        do = do_ref[...].astype(MXU)
        for hh in range(MEM_HEADS):
            sl = slice(hh * HD, (hh + 1) * HD)
            vsl = slice(MEM_W + hh * HD, MEM_W + (hh + 1) * HD)
            k = kv_ref[:, sl]
            p = _mem_probs(q[:, sl], k)
            dp = _dot(do[:, sl], kv_ref[:, vsl], NT)
            ds = (p * (dp - jnp.sum(p * dp, axis=-1, keepdims=True)) * SCALE).astype(MXU)
            dq_ref[:, sl] = _dot(ds, k).astype(dq_ref.dtype)
            dkv_ref[:, sl] += _dot(ds, q[:, sl], TN)
            dkv_ref[:, vsl] += _dot(p.astype(MXU), do[:, sl], TN)

    return pl.pallas_call(
        body, grid=(s // tq,),
        in_specs=[BS((tq, MEM_W), lambda i: (i, q_col)), BS((N_MEM, 2 * MEM_W), lambda i: (0, 0)),
                  BS((tq, MEM_W), lambda i: (i, MIX_W // MEM_W))],
        out_specs=[BS((tq, MEM_W), lambda i: (i, 0)), BS((N_MEM, 2 * MEM_W), lambda i: (0, 0))],
        out_shape=[SDS((s, MEM_W), MXU), SDS((N_MEM, 2 * MEM_W), F32)],
        compiler_params=_cp("arbitrary"), name=name,
    )(proj, kvm, dym)


def _dil_scores(q, kp, kc, n, slope_dil):
    qi = lax.broadcasted_iota(jnp.int32, (Q_BLOCK, Q_BLOCK), 0)
    ki = lax.broadcasted_iota(jnp.int32, (Q_BLOCK, Q_BLOCK), 1)
    rel_p = qi + Q_BLOCK - ki
    rel_c = qi - ki
    s_p = _dot(q, kp, NT) * SCALE - slope_dil * rel_p.astype(F32)
    s_c = _dot(q, kc, NT) * SCALE - slope_dil * rel_c.astype(F32)
    s_p = jnp.where((rel_p <= Q_BLOCK) & (n > 0), s_p, NEG_INF)
    s_c = jnp.where(rel_c >= 0, s_c, NEG_INF)
    return s_p, s_c


def _slope_dil(gi, hh):
    head = 4 * gi + hh
    return DIL_GROUPS[gi][1] * 2.0 ** (-8.0 * (head + 1.0) / N_BLK)


def dil_attn_fwd(proj, kv, gi, name):
    dil = DIL_GROUPS[gi][1]
    s, pw = proj.shape
    sub = s // dil
    nb = sub // Q_BLOCK
    qc, kc_ = pw // MEM_W, kv.shape[1] // MEM_W

    def body(q_ref, kp_ref, kc_ref, vp_ref, vc_ref, o_ref, lse_ref):
        n = pl.program_id(1)
        q = q_ref[...].astype(MXU)
        for hh in range(4):
            sl = slice(hh * HD, (hh + 1) * HD)
            s_p, s_c = _dil_scores(q[:, sl], kp_ref[:, sl], kc_ref[:, sl], n, _slope_dil(gi, hh))
            mx = jnp.maximum(jnp.max(s_p, axis=-1, keepdims=True), jnp.max(s_c, axis=-1, keepdims=True))
            den = jnp.sum(jnp.exp(s_p - mx), axis=-1, keepdims=True) + jnp.sum(jnp.exp(s_c - mx), axis=-1, keepdims=True)
            lse = mx + jnp.log(den)
            o_ref[:, sl] = (_dot(jnp.exp(s_p - lse).astype(MXU), vp_ref[:, sl])
                            + _dot(jnp.exp(s_c - lse).astype(MXU), vc_ref[:, sl]))
            lse_ref[:, sl] = jnp.broadcast_to(lse, (Q_BLOCK, HD))

    blk = (Q_BLOCK, MEM_W)
    prev = lambda n: jnp.maximum(n - 1, 0)
    out = BS(blk, lambda r, n: (n, r))
    return pl.pallas_call(
        body, grid=(dil, nb),
        in_specs=[BS(blk, lambda r, n: (n, r * qc + gi)),
                  BS(blk, lambda r, n: (prev(n), r * kc_ + gi)), BS(blk, lambda r, n: (n, r * kc_ + gi)),
                  BS(blk, lambda r, n: (prev(n), r * kc_ + 3 + gi)), BS(blk, lambda r, n: (n, r * kc_ + 3 + gi))],
        out_specs=[out, out],
        out_shape=[SDS((sub, dil * MEM_W), F32), SDS((sub, dil * MEM_W), F32)],
        compiler_params=_cp("parallel", "parallel"), name=name,
    )(proj.reshape(sub, dil * pw), *([kv.reshape(sub, dil * kv.shape[1])] * 4))


def dil_attn_bwd(proj, kv, lse, do, dd, gi, name):
    dil = DIL_GROUPS[gi][1]
    s, pw = proj.shape
    sub = s // dil
    nb = sub // Q_BLOCK
    qc, kc_ = pw // MEM_W, kv.shape[1] // MEM_W

    def body(q_ref, kp_ref, kc_ref, vp_ref, vc_ref, lse_ref, do_ref, dd_ref, dq_ref, dk_ref, dv_ref, ck, cv):
        n = pl.program_id(1)

        @pl.when(n == 0)
        def _():
            ck[...] = jnp.zeros_like(ck)
            cv[...] = jnp.zeros_like(cv)

        @pl.when(n < nb)
        def _():
            q = q_ref[...].astype(MXU)
            do_m = do_ref[...].astype(MXU)
            for hh in range(4):
                sl = slice(hh * HD, (hh + 1) * HD)
                s_p, s_c = _dil_scores(q[:, sl], kp_ref[:, sl], kc_ref[:, sl], n, _slope_dil(gi, hh))
                lse_h = lse_ref[:, sl]
                dd_h = dd_ref[:, sl]
                p_p = jnp.exp(s_p - lse_h)
                p_c = jnp.exp(s_c - lse_h)
                ds_p = (p_p * (_dot(do_m[:, sl], vp_ref[:, sl], NT) + dd_h) * SCALE).astype(MXU)
                ds_c = (p_c * (_dot(do_m[:, sl], vc_ref[:, sl], NT) + dd_h) * SCALE).astype(MXU)
                dq_ref[:, sl] = (_dot(ds_p, kp_ref[:, sl]) + _dot(ds_c, kc_ref[:, sl])).astype(dq_ref.dtype)
                dk_ref[:, sl] = ck[:, sl] + _dot(ds_p, q[:, sl], TN)
                dv_ref[:, sl] = cv[:, sl] + _dot(p_p.astype(MXU), do_m[:, sl], TN)
                ck[:, sl] = _dot(ds_c, q[:, sl], TN)
                cv[:, sl] = _dot(p_c.astype(MXU), do_m[:, sl], TN)

        @pl.when(n == nb)
        def _():
            dk_ref[...] = ck[...]
            dv_ref[...] = cv[...]

    blk = (Q_BLOCK, MEM_W)
    cur = lambda n: jnp.minimum(n, nb - 1)
    prev = lambda n: jnp.maximum(jnp.minimum(n, nb - 1) - 1, 0)
    done = lambda n: jnp.maximum(n - 1, 0)
    own = BS(blk, lambda r, n: (cur(n), r))
    kvv = kv.reshape(sub, dil * kv.shape[1])
    return pl.pallas_call(
        body, grid=(dil, nb + 1),
        in_specs=[BS(blk, lambda r, n: (cur(n), r * qc + gi)),
                  BS(blk, lambda r, n: (prev(n), r * kc_ + gi)), BS(blk, lambda r, n: (cur(n), r * kc_ + gi)),
                  BS(blk, lambda r, n: (prev(n), r * kc_ + 3 + gi)), BS(blk, lambda r, n: (cur(n), r * kc_ + 3 + gi)),
                  own, own, own],
        out_specs=[own, BS(blk, lambda r, n: (done(n), r)), BS(blk, lambda r, n: (done(n), r))],
        out_shape=[SDS((sub, dil * MEM_W), MXU), SDS((sub, dil * MEM_W), F32), SDS((sub, dil * MEM_W), F32)],
        scratch_shapes=[pltpu.VMEM(blk, F32), pltpu.VMEM(blk, F32)],
        compiler_params=_cp("parallel", "arbitrary"), name=name,
    )(proj.reshape(sub, dil * pw), kvv, kvv, kvv, kvv, lse, do, dd)


def _group_weights(lse_refs):
    l0, l1, l2 = (r[...] for r in lse_refs)
    mx = jnp.maximum(jnp.maximum(l0, l1), l2)
    e = [jnp.exp(l - mx) for l in (l0, l1, l2)]
    den = e[0] + e[1] + e[2]
    return [x / den for x in e]


def dil_mix_prep(o_list, lse_list, m, name):
    s = m.shape[0]
    tr = _row_tile(s)

    def body(o0, o1, o2, l0, l1, l2, m_ref, out_ref):
        w = _group_weights((l0, l1, l2))
        for g, o_ref in enumerate((o0, o1, o2)):
            out_ref[:, g * MEM_W:(g + 1) * MEM_W] = (o_ref[...] * w[g]).astype(out_ref.dtype)
        out_ref[:, MIX_W:] = m_ref[...]

    blk = BS((tr, MEM_W), lambda i: (i, 0))
    return pl.pallas_call(
        body, grid=(s // tr,), in_specs=[blk] * 7,
        out_specs=BS((tr, D), lambda i: (i, 0)), out_shape=SDS((s, D), MXU),
        compiler_params=_cp("parallel"), name=name,
    )(*o_list, *lse_list, m)


def dil_mix_bwd(dym, o_list, lse_list, name):
    s = dym.shape[0]
    tr = _row_tile(s)

    def body(da_ref, o0, o1, o2, l0, l1, l2, do0, do1, do2, dd0, dd1, dd2):
        w = _group_weights((l0, l1, l2))
        tot = None
        for g, (o_ref, do_ref) in enumerate(zip((o0, o1, o2), (do0, do1, do2))):
            da = da_ref[:, g * MEM_W:(g + 1) * MEM_W]
            do_ref[...] = da * w[g]
            x = da * o_ref[...]
            dw = jnp.concatenate(
                [jnp.broadcast_to(jnp.sum(x[:, hh * HD:(hh + 1) * HD], axis=-1, keepdims=True), (tr, HD))
                 for hh in range(4)], axis=1)
            tot = w[g] * dw if tot is None else tot + w[g] * dw
        for g, dd_ref in enumerate((dd0, dd1, dd2)):
            dd_ref[...] = -w[g] * tot

    blk = BS((tr, MEM_W), lambda i: (i, 0))
    outs = pl.pallas_call(
        body, grid=(s // tr,), in_specs=[BS((tr, MIX_W), lambda i: (i, 0))] + [blk] * 6,
        out_specs=[blk] * 6, out_shape=[SDS((s, MEM_W), F32)] * 6,
        compiler_params=_cp("parallel"), name=name,
    )(dym, *o_list, *lse_list)
    return outs[:3], outs[3:]


def sum_cast(parts, name):
    s = parts[0][0].shape[0]
    tr = _row_tile(s)
    flat = [a for p in parts for a in p]
    sizes = [len(p) for p in parts]

    def body(*refs):
        out_ref = refs[-1]
        pos = 0
        for j, n in enumerate(sizes):
            acc = refs[pos][...].astype(F32)
            for t in range(1, n):
                acc = acc + refs[pos + t][...].astype(F32)
            out_ref[:, j * MEM_W:(j + 1) * MEM_W] = acc.astype(out_ref.dtype)
            pos += n

    blk = BS((tr, MEM_W), lambda i: (i, 0))
    width = MEM_W * len(parts)
    return pl.pallas_call(
        body, grid=(s // tr,), in_specs=[blk] * len(flat),
        out_specs=BS((tr, width), lambda i: (i, 0)), out_shape=SDS((s, width), MXU),
        compiler_params=_cp("parallel"), name=name,
    )(*flat)


def add_n(arrs, name):
    rows, cols = arrs[0].shape
    tr = _row_tile(rows)

    def body(*refs):
        acc = refs[0][...]
        for r in refs[1:-1]:
            acc = acc + r[...]
        refs[-1][...] = acc

    blk = BS((tr, cols), lambda i: (i, 0))
    return pl.pallas_call(
        body, grid=(rows // tr,), in_specs=[blk] * len(arrs), out_specs=blk,
        out_shape=SDS((rows, cols), F32), compiler_params=_cp("parallel"), name=name,
    )(*arrs)


class _NoExchange:
    def hook(self, where, l, after):
        return []


def _fwd_bwd(x, mem, target, small, big, gs, gb, sched):
    s = x.shape[0]
    tm = min(1024, s)
    ts = min(2048, s)

    def after_hook(arr, where, l, after):
        toks = sched.hook(where, l, after)
        return tie(arr, toks, "tie_%s_%d" % (where, l)) if toks else arr

    h = x
    saved = []
    kv = None
    mem_n = None
    hn = norm_cast(h, small["a_pre_mix_g"][0], "pre_norm")
    for l in range(4):
        rec = l < 2
        p, j = ("a", l) if rec else ("b", l - 2)
        sv = {"h": h}
        hn = after_hook(hn, "fwd_begin", l, h)
        if mem_n is None:
            mem_n = norm_cast(mem, small["mem_norm_g"], "mem_norm")
        kvm = mm_nn(mem_n, big[p + "_w_mem_kv"][j], tm=N_MEM, tn=2 * MEM_W, tk=D, out_dtype=MXU, name="mem_kv")
        if rec:
            proj = mm_nn(hn, big["a_w_in"][j], tm=tm, tn=896, tk=D, out_dtype=F32, name="rec_in")
            xc, hl = lru_fwd(proj, small["a_conv_w"][j], small["a_conv_b"][j], small["a_gate_a_w"][j],
                             small["a_gate_a_b"][j], small["a_gate_x_w"][j], small["a_gate_x_b"][j],
                             small["a_lambda"][j], "lru_fwd")
            m = mem_attn_fwd(proj, 2 * MIX_W // MEM_W, kvm, "rec_mem_attn")
            ym = lru_mix_prep(hl, proj, m, "lru_mix_prep")
            sv.update(xc=xc, hl=hl)
        else:
            proj = mm_nn(hn, big["b_w_in"][j], tm=tm, tn=1024, tk=D, out_dtype=F32, name="dil_in")
            o_list, lse_list = [], []
            for gi in range(3):
                o, lse = dil_attn_fwd(proj, kv, gi, "dil_attn_fwd%d" % gi)
                o_list.append(o.reshape(s, MEM_W))
                lse_list.append(lse.reshape(s, MEM_W))
            m = mem_attn_fwd(proj, MIX_W // MEM_W, kvm, "dil_mem_attn")
            ym = dil_mix_prep(o_list, lse_list, m, "dil_mix_prep")
            sv.update(o=o_list, lse=lse_list)
        ym = after_hook(ym, "fwd_q1", l, ym)
        mix = mm_nn(ym, big[p + "_w_out"][j], tm=tm, tn=1024, tk=D, out_dtype=F32, name="mix_out")
        h1, hn2 = resid_norm_next(h, mix, small[p + "_post_mix_g"][j], small[p + "_pre_ffn_g"][j], "post_pre_norm")
        hn2 = after_hook(hn2, "fwd_mid", l, mix)
        g, u, act = ffn_in_fwd(hn2, big[p + "_w_ffn_in"][j], "ffn_in")
        act = after_hook(act, "fwd_q3", l, u)
        y2 = mm_nn(act, big[p + "_w_ffn_out"][j], tm=tm // 2, tn=D, tk=D_FF // 2, out_dtype=F32, name="ffn_out")
        sv.update(kvm=kvm, hn=hn, proj=proj, ym=ym, mix=mix, h1=h1, hn2=hn2, g=g, u=u, act=act, y2=y2)
        saved.append(sv)
        if l < 3:
            pn, jn = ("a", l + 1) if l + 1 < 2 else ("b", l - 1)
            h, hn = resid_norm_next(h1, y2, small[p + "_post_ffn_g"][j], small[pn + "_pre_mix_g"][jn],
                                    "post_pre_norm")
        else:
            h = resid_norm(h1, y2, small[p + "_post_ffn_g"][j], "post_norm")
        sched.hook("fwd_end", l, h)
        if l == 1:
            h_kv = h
            kvn = norm_cast(h, small["kv_norm_g"], "pre_norm")
            kv = mm_nn(kvn, big["w_kv_shared"], tm=tm, tn=768, tk=D, out_dtype=MXU, name="kv_proj")

    loss_parts, dh = loss_head(h, target, "loss_head")

    def stack2(name, j, val):
        gs.setdefault(name, [None, None])[j] = val

    def stack2b(name, j, val):
        gb.setdefault(name, [None, None])[j] = val

    dkv_parts = []
    dmem_parts = []
    dkvm = [None] * 4
    for l in (3, 2, 1, 0):
        rec = l < 2
        p, j = ("a", l) if rec else ("b", l - 2)
        sv = saved[l]
        if l == 1:
            dkv = sum_cast([(dkv_parts[0][c], dkv_parts[1][c]) for c in range(6)], "dkv_sum")
            dkvn = mm_nt([dkv], big["w_kv_shared"], tm=tm, tn=D, tk=768, out_dtype=F32, name="kv_proj_dx")
            gb["w_kv_shared"] = mm_tn(kvn, [dkv], t1=D, tn=768, ts=ts, col_shards=True, name="kv_proj_dw")
            dh, gs["kv_norm_g"] = norm_bwd(h_kv, small["kv_norm_g"], dkvn, dh, F32, "pre_norm_bwd")
        dy2, dg = norm_bwd(sv["y2"], small[p + "_post_ffn_g"][j], dh, None, MXU, "post_norm_bwd")
        dy2 = after_hook(dy2, "bwd_begin", l, dh)
        stack2(p + "_post_ffn_g", j, dg)
        dgg, dgu = ffn_act_bwd(dy2, big[p + "_w_ffn_out"][j], sv["g"], sv["u"], "ffn_act_bwd")
        dgg = after_hook(dgg, "bwd_mid1", l, dgu)
        stack2b(p + "_w_ffn_out", j, mm_tn(sv["act"], [dy2], t1=D_FF // 4, tn=D, ts=ts // 2, col_shards=False,
                                          name="ffn_out_dw"))
        dhn2 = mm_nt([dgg, dgu], big[p + "_w_ffn_in"][j], tm=tm // 2, tn=D, tk=D_FF // 2, out_dtype=F32,
                     name="ffn_in_dx")
        stack2b(p + "_w_ffn_in", j, mm_tn(sv["hn2"], [dgg, dgu], t1=D // 2, tn=D_FF // 4, ts=ts, col_shards=True,
                                         name="ffn_in_dw"))
        dhn2 = after_hook(dhn2, "bwd_mid2", l, gb[p + "_w_ffn_in"][j])
        dh1, dg = norm_bwd(sv["h1"], small[p + "_pre_ffn_g"][j], dhn2, dh, F32, "pre_norm_bwd")
        stack2(p + "_pre_ffn_g", j, dg)
        dmix, dg = norm_bwd(sv["mix"], small[p + "_post_mix_g"][j], dh1, None, MXU, "post_norm_bwd")
        stack2(p + "_post_mix_g", j, dg)
        dym = mm_nt([dmix], big[p + "_w_out"][j], tm=tm, tn=1024, tk=D, out_dtype=F32, name="mix_out_dx")
        stack2b(p + "_w_out", j, mm_tn(sv["ym"], [dmix], t1=D, tn=1024, ts=ts, col_shards=False,
                                      name="mix_out_dw"))
        dym = after_hook(dym, "bwd_m1", l, gb[p + "_w_out"][j])
        if rec:
            dqm, dkvm[l] = mem_attn_bwd(sv["proj"], 2 * MIX_W // MEM_W, sv["kvm"], dym, "rec_mem_attn_bwd")
            dproj, dcw, dcb, dwa, dba, dwx, dbx, dlam = lru_bwd(
                dym, sv["proj"], sv["xc"], sv["hl"], dqm, small["a_conv_w"][j], small["a_gate_a_w"][j],
                small["a_gate_a_b"][j], small["a_gate_x_w"][j], small["a_gate_x_b"][j], small["a_lambda"][j],
                "lru_bwd")
            for nm, val in (("a_conv_w", dcw), ("a_conv_b", dcb), ("a_gate_a_w", dwa), ("a_gate_a_b", dba),
                            ("a_gate_x_w", dwx), ("a_gate_x_b", dbx), ("a_lambda", dlam)):
                stack2(nm, j, val)
            dhn = mm_nt([dproj], big["a_w_in"][j], tm=tm, tn=D, tk=896, out_dtype=F32, name="rec_in_dx")
            stack2b("a_w_in", j, mm_tn(sv["hn"], [dproj], t1=D, tn=896, ts=ts, col_shards=True, name="rec_in_dw"))
        else:
            dqm, dkvm[l] = mem_attn_bwd(sv["proj"], MIX_W // MEM_W, sv["kvm"], dym, "dil_mem_attn_bwd")
            do_list, dd_list = dil_mix_bwd(dym, sv["o"], sv["lse"], "dil_mix_bwd")
            dq_list, dk_list, dv_list = [], [], []
            for gi in range(3):
                dil = DIL_GROUPS[gi][1]
                view = (s // dil, dil * MEM_W)
                dq, dk, dv = dil_attn_bwd(sv["proj"], kv, sv["lse"][gi].reshape(view), do_list[gi].reshape(view),
                                          dd_list[gi].reshape(view), gi, "dil_attn_bwd%d" % gi)
                dq_list.append(dq.reshape(s, MEM_W))
                dk_list.append(dk.reshape(s, MEM_W))
                dv_list.append(dv.reshape(s, MEM_W))
            dkv_parts.append(dk_list + dv_list)
            dproj = sum_cast([(a,) for a in dq_list + [dqm]], "dil_dproj")
            dhn = mm_nt([dproj], big["b_w_in"][j], tm=tm, tn=1024, tk=D, out_dtype=F32, name="dil_in_dx")
            stack2b("b_w_in", j, mm_tn(sv["hn"], [dproj], t1=D, tn=1024, ts=ts, col_shards=False, name="dil_in_dw"))
        dk_m = dkvm[l].astype(MXU)
        dmem_parts.append(mm_nt([dk_m], big[p + "_w_mem_kv"][j], tm=N_MEM, tn=D, tk=2 * MEM_W, out_dtype=F32,
                                name="mem_kv_dx"))
        stack2b(p + "_w_mem_kv", j, mm_tn(mem_n, [dk_m], t1=D, tn=2 * MEM_W, ts=N_MEM, col_shards=False,
                                         name="mem_kv_dw"))
        dh, dg = norm_bwd(sv["h"], small[p + "_pre_mix_g"][j], dhn, dh1, F32, "pre_norm_bwd")
        stack2(p + "_pre_mix_g", j, dg)
        dh = after_hook(dh, "bwd_end", l, dh)

    _, gs["mem_norm_g"] = norm_bwd(mem, small["mem_norm_g"], add_n(dmem_parts, "dmem_sum"), None, F32,
                                   "mem_norm_bwd")
    return loss_parts, dh


ANY = pl.BlockSpec(memory_space=pl.ANY)
CHIP_FLIPS = (1, 2, 3)


def _coords():
    return lax.axis_index("x"), lax.axis_index("y"), lax.axis_index("c")


def _flip(x, y, m):
    return x ^ (m >> 1), y ^ (m & 1)


def _remote(src, dst, send_sems, recv_sems, k, device):
    return pltpu.make_async_remote_copy(src_ref=src, dst_ref=dst, send_sem=send_sems.at[k], recv_sem=recv_sems.at[k],
                                        device_id=device, device_id_type=MESH)


def gather_shards(shards, name):
    n = len(shards)

    def body(*refs):
        ins, outs = refs[:n], refs[n:2 * n]
        send_sems, recv_sems = refs[2 * n:]
        x, y, c = _coords()
        me = 2 * x + y
        sib = (x, y, 1 - c)
        halves, sends = [], []
        for i in range(n):
            hr = shards[i].shape[0] // 2
            mine = pl.ds(pl.multiple_of(c * hr, 8), hr)
            other = pl.ds(pl.multiple_of((1 - c) * hr, 8), hr)
            halves.append((mine, other))
            own = _remote(ins[i], outs[i].at[me], send_sems, recv_sems, 7 * i + 6, sib)
            own.start()
            sends.append(own)
            for j, m in enumerate(CHIP_FLIPS):
                cp = _remote(ins[i].at[mine], outs[i].at[me, mine], send_sems, recv_sems, 7 * i + j,
                             (*_flip(x, y, m), c))
                cp.start()
                sends.append(cp)
        for i in range(n):
            mine, _ = halves[i]
            for j, m in enumerate(CHIP_FLIPS):
                slot = outs[i].at[me ^ m, mine]
                _remote(slot, slot, send_sems, recv_sems, 7 * i + j, sib).wait_recv()
                fwd = _remote(slot, slot, send_sems, recv_sems, 7 * i + 3 + j, sib)
                fwd.start()
                sends.append(fwd)
        for i in range(n):
            _, other = halves[i]
            for j, m in enumerate(CHIP_FLIPS):
                slot = outs[i].at[me ^ m, other]
                _remote(slot, slot, send_sems, recv_sems, 7 * i + 3 + j, sib).wait_recv()
            _remote(ins[i], outs[i].at[me], send_sems, recv_sems, 7 * i + 6, sib).wait_recv()
        for cp in sends:
            cp.wait_send()

    return pl.pallas_call(
        body, in_specs=[ANY] * n, out_specs=[ANY] * n,
        out_shape=[SDS((N_CHIPS,) + sh.shape, sh.dtype) for sh in shards],
        scratch_shapes=[pltpu.SemaphoreType.DMA((7 * n,)), pltpu.SemaphoreType.DMA((7 * n,))],
        name=name,
    )(*shards)


def swap_halves(grads, name):
    n = len(grads)

    def body(*refs):
        ins, outs = refs[:n], refs[n:2 * n]
        send_sems, recv_sems = refs[2 * n:]
        x, y, c = _coords()
        cps = []
        for i in range(n):
            hr = grads[i].shape[1] // 2
            other = pl.ds(pl.multiple_of((1 - c) * hr, 8), hr)
            cp = _remote(ins[i].at[pl.ds(0, N_CHIPS), other], outs[i], send_sems, recv_sems, i, (x, y, 1 - c))
            cp.start()
            cps.append(cp)
        for cp in cps:
            cp.wait()

    return pl.pallas_call(
        body, in_specs=[ANY] * n, out_specs=[ANY] * n,
        out_shape=[SDS((N_CHIPS, g.shape[1] // 2, g.shape[2]), g.dtype) for g in grads],
        scratch_shapes=[pltpu.SemaphoreType.DMA((n,)), pltpu.SemaphoreType.DMA((n,))],
        name=name,
    )(*grads)


def _sum_rows_tile(rows, cols):
    for tr in (512, 256, 128, 64, 32, 16):
        if rows % tr == 0 and tr * cols * 4 <= 2 * 1024 * 1024:
            return tr
    raise ValueError((rows, cols))


def half_sum(g, got, c_arr, name):
    _, r, cols = g.shape
    hr = r // 2
    tr = _sum_rows_tile(hr, cols)

    def my_chip():
        return 2 * lax.axis_index("x") + lax.axis_index("y")

    def body(g_ref, got_ref, o_ref, own_ref):
        p = (g_ref[...] + got_ref[...]).astype(o_ref.dtype)
        o_ref[...] = p

        @pl.when(pl.program_id(1) == my_chip())
        def _():
            own_ref[...] = p

    out = SDS((N_CHIPS, hr, cols), jnp.bfloat16)
    return pl.pallas_call(
        body, grid=(hr // tr, N_CHIPS),
        in_specs=[BS((None, None, tr, cols), lambda i, s: (s, lax.axis_index("c"), i, 0)),
                  BS((None, tr, cols), lambda i, s: (s, i, 0))],
        out_specs=[BS((None, tr, cols), lambda i, s: (s, i, 0)),
                   BS((None, tr, cols), lambda i, s: (my_chip(), i, 0))],
        out_shape=[out, out], compiler_params=_cp("parallel", "arbitrary"), name=name,
    )(g.reshape(N_CHIPS, 2, hr, cols), got)


def exchange_parts(parts, name):
    n = len(parts)

    def body(*refs):
        ins, outs = refs[:n], refs[n:2 * n]
        send_sems, recv_sems, loc_sems = refs[2 * n:]
        x, y, c = _coords()
        me = 2 * x + y
        cps, locs = [], []
        for i in range(n):
            loc = pltpu.make_async_copy(ins[i].at[me], outs[i].at[me], loc_sems.at[i])
            loc.start()
            locs.append(loc)
            for j, m in enumerate(CHIP_FLIPS):
                cp = _remote(ins[i].at[me ^ m], outs[i].at[me], send_sems, recv_sems, 3 * i + j, (*_flip(x, y, m), c))
                cp.start()
                cps.append(cp)
        for cp in cps:
            cp.wait()
        for loc in locs:
            loc.wait()

    return pl.pallas_call(
        body, in_specs=[ANY] * n, out_specs=[ANY] * n,
        out_shape=[SDS(p.shape, p.dtype) for p in parts],
        scratch_shapes=[pltpu.SemaphoreType.DMA((3 * n,)), pltpu.SemaphoreType.DMA((3 * n,)),
                        pltpu.SemaphoreType.DMA((n,))],
        name=name,
    )(*parts)


def slot_sum(slots, c_arr, name):
    _, hr, cols = slots.shape
    tr = _sum_rows_tile(hr, cols)
    nblk = hr // tr

    def body(s_ref, o_ref):
        acc = s_ref[0].astype(F32)
        for p in range(1, N_CHIPS):
            acc = acc + s_ref[p].astype(F32)
        o_ref[...] = acc

    return pl.pallas_call(
        body, grid=(nblk,), in_specs=[BS((N_CHIPS, tr, cols), lambda i: (0, i, 0))],
        out_specs=BS((tr, cols), lambda i: (lax.axis_index("c") * nblk + i, 0)),
        out_shape=SDS((2 * hr, cols), F32), compiler_params=_cp("parallel"), name=name,
    )(slots)


def share_halves(bufs, name):
    n = len(bufs)

    def body(*refs):
        outs = refs[n:2 * n]
        send_sems, recv_sems = refs[2 * n:]
        x, y, c = _coords()
        cps = []
        for i in range(n):
            hr = bufs[i].shape[0] // 2
            mine = outs[i].at[pl.ds(pl.multiple_of(c * hr, 8), hr)]
            cp = _remote(mine, mine, send_sems, recv_sems, i, (x, y, 1 - c))
            cp.start()
            cps.append(cp)
        for cp in cps:
            cp.wait()

    return pl.pallas_call(
        body, in_specs=[ANY] * n, out_specs=[ANY] * n,
        out_shape=[SDS(b.shape, b.dtype) for b in bufs],
        input_output_aliases={i: i for i in range(n)},
        scratch_shapes=[pltpu.SemaphoreType.DMA((n,)), pltpu.SemaphoreType.DMA((n,))],
        name=name,
    )(*bufs)


HBM_SPEC = pl.BlockSpec(memory_space=pltpu.HBM)
SEM_SPEC = pl.BlockSpec(memory_space=pltpu.SEMAPHORE)
EFFECT = pltpu.SideEffectType.DATAFLOW_SIDE_EFFECTING


def split_start(name, bufs, plan, n_copies):
    nb = len(bufs)

    def body(*refs):
        send_sems, recv_sems = refs[nb], refs[nb + 1]
        for k, (src, dst, dev) in enumerate(plan(refs[:nb])):
            _remote(src, dst, send_sems, recv_sems, k, dev).start()
        refs[-1][...] = jnp.zeros_like(refs[-1])

    outs = pl.pallas_call(
        body, name=name,
        out_shape=(pltpu.SemaphoreType.DMA((n_copies,)), pltpu.SemaphoreType.DMA((n_copies,)),
                   *[pltpu.HBM(b.shape, b.dtype) for b in bufs], SDS((8, LANES), F32)),
        in_specs=[HBM_SPEC] * nb, out_specs=(SEM_SPEC, SEM_SPEC, *[HBM_SPEC] * nb, VM),
        input_output_aliases={i: 2 + i for i in range(nb)},
        compiler_params=pltpu.CompilerParams(has_side_effects=EFFECT),
    )(*[pltpu.with_memory_space_constraint(b, pltpu.HBM) for b in bufs])
    return outs[0], outs[1], list(outs[2:2 + nb]), outs[-1]


def split_wait(name, send_sems, recv_sems, bufs, after, plan):
    nb = len(bufs)

    def body(*refs):
        send_ref, recv_ref = refs[nb], refs[nb + 1]
        for k, (src, dst, dev) in enumerate(plan(refs[:nb])):
            cp = _remote(src, dst, send_ref, recv_ref, k, dev)
            cp.wait_send()
            cp.wait_recv()

    outs = pl.pallas_call(
        body, name=name, out_shape=[pltpu.HBM(b.shape, b.dtype) for b in bufs],
        in_specs=[HBM_SPEC] * nb + [SEM_SPEC, SEM_SPEC, ANY], out_specs=[HBM_SPEC] * nb,
        input_output_aliases={i: i for i in range(nb)},
        compiler_params=pltpu.CompilerParams(has_side_effects=EFFECT),
    )(*bufs, send_sems, recv_sems, after)
    return list(outs)


def tie(x, tokens, name):
    def body(*refs):
        pass

    return pl.pallas_call(
        body, name=name, out_shape=SDS(x.shape, x.dtype), in_specs=[ANY] * (1 + len(tokens)), out_specs=ANY,
        input_output_aliases={0: 0},
    )(x, *tokens)


def plan_gather_ici(n, rows):
    def plan(refs):
        x, y, c = _coords()
        me = 2 * x + y
        out = []
        for i in range(n):
            hr = rows[i] // 2
            mine = pl.ds(pl.multiple_of(c * hr, 8), hr)
            out.append((refs[i], refs[n + i].at[me], (x, y, 1 - c)))
            for m in CHIP_FLIPS:
                out.append((refs[i].at[mine], refs[n + i].at[me, mine], (*_flip(x, y, m), c)))
        return out
    return plan


def plan_gather_d2d(n, rows):
    def plan(refs):
        x, y, c = _coords()
        me = 2 * x + y
        out = []
        for i in range(n):
            hr = rows[i] // 2
            mine = pl.ds(pl.multiple_of(c * hr, 8), hr)
            for m in CHIP_FLIPS:
                slot = refs[i].at[me ^ m, mine]
                out.append((slot, slot, (x, y, 1 - c)))
        return out
    return plan


def plan_swap(n, rows):
    def plan(refs):
        x, y, c = _coords()
        out = []
        for i in range(n):
            hr = rows[i] // 2
            other = pl.ds(pl.multiple_of((1 - c) * hr, 8), hr)
            out.append((refs[i].at[pl.ds(0, N_CHIPS), other], refs[n + i], (x, y, 1 - c)))
        return out
    return plan


def plan_exchange(n):
    def plan(refs):
        x, y, c = _coords()
        me = 2 * x + y
        out = []
        for i in range(n):
            for m in CHIP_FLIPS:
                out.append((refs[i].at[me ^ m], refs[n + i].at[me], (*_flip(x, y, m), c)))
        return out
    return plan


def plan_share(n, rows):
    def plan(refs):
        x, y, c = _coords()
        out = []
        for i in range(n):
            hr = rows[i] // 2
            mine = refs[i].at[pl.ds(pl.multiple_of(c * hr, 8), hr)]
            out.append((mine, mine, (x, y, 1 - c)))
        return out
    return plan


def reduce_scatter(grads, c_arr, tag):
    got = swap_halves(grads, "rs_swap_" + tag)
    parts = [half_sum(g, r, c_arr, "rs_half_sum") for g, r in zip(grads, got)]
    slots = exchange_parts(parts, "rs_exchange_" + tag)
    return share_halves([slot_sum(s, c_arr, "rs_slot_sum") for s in slots], "rs_share_" + tag)


VM = pl.BlockSpec(memory_space=pltpu.VMEM)


def small_gather(v, name):
    def body(v_ref, out_ref, send_sems, recv_sems):
        x, y, c = _coords()
        me = 2 * x + y
        out_ref[me] = v_ref[...]
        cps = []
        for j, m in enumerate(CHIP_FLIPS):
            cp = _remote(v_ref, out_ref.at[me], send_sems, recv_sems, j, (*_flip(x, y, m), c))
            cp.start()
            cps.append(cp)
        for cp in cps:
            cp.wait()

    return pl.pallas_call(
        body, in_specs=[VM], out_specs=VM, out_shape=SDS((N_CHIPS,) + v.shape, v.dtype),
        scratch_shapes=[pltpu.SemaphoreType.DMA((3,)), pltpu.SemaphoreType.DMA((3,))],
        compiler_params=pltpu.CompilerParams(vmem_limit_bytes=VMEM_LIMIT_BYTES), name=name,
    )(v)


def small_allreduce(v, name):
    def body(v_ref, out_ref, sib_buf, slots, send_sems, recv_sems):
        x, y, c = _coords()
        me = 2 * x + y
        swap = _remote(v_ref, sib_buf, send_sems, recv_sems, 0, (x, y, 1 - c))
        swap.start()
        swap.wait()
        slots[me] = v_ref[...] + sib_buf[...]
        cps = []
        for j, m in enumerate(CHIP_FLIPS):
            cp = _remote(slots.at[me], slots.at[me], send_sems, recv_sems, 1 + j, (*_flip(x, y, m), c))
            cp.start()
            cps.append(cp)
        for cp in cps:
            cp.wait()
        out_ref[...] = (slots[0] + slots[1]) + (slots[2] + slots[3])

    return pl.pallas_call(
        body, in_specs=[VM], out_specs=VM, out_shape=SDS(v.shape, v.dtype),
        scratch_shapes=[pltpu.VMEM(v.shape, v.dtype), pltpu.VMEM((N_CHIPS,) + v.shape, v.dtype),
                        pltpu.SemaphoreType.DMA((4,)), pltpu.SemaphoreType.DMA((4,))],
        compiler_params=pltpu.CompilerParams(vmem_limit_bytes=VMEM_LIMIT_BYTES), name=name,
    )(v)


def adamw(w, g_list, m, v, name):
    nl, rows, cols = w.shape
    tr = _sum_rows_tile(rows, cols) if rows % 16 == 0 else rows
    bc1 = 1.0 - ADAM_B1 ** ADAM_STEP
    bc2 = 1.0 - ADAM_B2 ** ADAM_STEP

    def body(*refs):
        w_ref, m_ref, v_ref = refs[:3]
        g_refs = refs[3:3 + nl]
        go_ref, d_ref, mo_ref, vo_ref = refs[3 + nl:]
        layer = pl.program_id(0)
        for l in range(nl):
            @pl.when(layer == l)
            def _(l=l):
                g = g_refs[l][...]
                m_new = ADAM_B1 * m_ref[...] + (1.0 - ADAM_B1) * g
                v_new = ADAM_B2 * v_ref[...] + (1.0 - ADAM_B2) * (g * g)
                m_hat = m_new / bc1
                v_hat = v_new / bc2
                go_ref[...] = g
                d_ref[...] = -ADAM_LR * (m_hat / (jnp.sqrt(v_hat) + ADAM_EPS) + ADAM_WD * w_ref[...])
                mo_ref[...] = m_new
                vo_ref[...] = v_new

    stk = BS((None, tr, cols), lambda l, i: (l, i, 0))
    flat = BS((tr, cols), lambda l, i: (i, 0))
    out = SDS((nl, rows, cols), F32)
    return pl.pallas_call(
        body, grid=(nl, rows // tr), in_specs=[stk] * 3 + [flat] * nl, out_specs=[stk] * 4,
        out_shape=[out] * 4, compiler_params=_cp("parallel", "parallel"), name=name,
    )(w, m, v, *g_list)


WEIGHTS = ["mem_norm_g", "a_pre_mix_g", "a_post_mix_g", "a_pre_ffn_g", "a_post_ffn_g", "a_w_in", "a_conv_w",
           "a_conv_b", "a_gate_a_w", "a_gate_a_b", "a_gate_x_w", "a_gate_x_b", "a_lambda", "a_w_mem_kv", "a_w_out",
           "a_w_ffn_in", "a_w_ffn_out", "kv_norm_g", "w_kv_shared", "b_pre_mix_g", "b_post_mix_g", "b_pre_ffn_g",
           "b_post_ffn_g", "b_w_in", "b_w_mem_kv", "b_w_out", "b_w_ffn_in", "b_w_ffn_out"]
BIG = {"a_w_in": True, "a_w_mem_kv": False, "a_w_out": False, "a_w_ffn_in": True, "a_w_ffn_out": False,
       "w_kv_shared": True, "b_w_in": False, "b_w_mem_kv": False, "b_w_out": False, "b_w_ffn_in": True,
       "b_w_ffn_out": False}
SHARDED_SMALL = ["a_pre_mix_g", "a_post_mix_g", "a_pre_ffn_g", "a_post_ffn_g", "a_conv_w", "a_conv_b", "a_gate_a_b",
                 "a_gate_x_b", "a_lambda"]
REPL_SMALL = ["mem_norm_g", "kv_norm_g", "b_pre_mix_g", "b_post_mix_g", "b_pre_ffn_g", "b_post_ffn_g", "a_gate_a_w",
              "a_gate_x_w"]
LANES = 128


def _pack(arrs, row_multiple=8):
    flat = jnp.concatenate([a.reshape(-1) for a in arrs])
    pad = -flat.shape[0] % (LANES * row_multiple)
    if pad:
        flat = jnp.concatenate([flat, jnp.zeros((pad,), flat.dtype)])
    return flat.reshape(-1, LANES)


def _unpack(packed, shapes):
    flat = packed.reshape(-1)
    out, pos = [], 0
    for sh in shapes:
        size = math.prod(sh)
        out.append(flat[pos:pos + size].reshape(sh))
        pos += size
    return out


def kernel(x, mem, mem_norm_g, a_pre_mix_g, a_post_mix_g, a_pre_ffn_g, a_post_ffn_g, a_w_in, a_conv_w, a_conv_b,
           a_gate_a_w, a_gate_a_b, a_gate_x_w, a_gate_x_b, a_lambda, a_w_mem_kv, a_w_out, a_w_ffn_in, a_w_ffn_out,
           kv_norm_g, w_kv_shared, b_pre_mix_g, b_post_mix_g, b_pre_ffn_g, b_post_ffn_g, b_w_in, b_w_mem_kv, b_w_out,
           b_w_ffn_in, b_w_ffn_out, loss_target, m_mem_norm_g, m_a_pre_mix_g, m_a_post_mix_g, m_a_pre_ffn_g,
           m_a_post_ffn_g, m_a_w_in, m_a_conv_w, m_a_conv_b, m_a_gate_a_w, m_a_gate_a_b, m_a_gate_x_w, m_a_gate_x_b,
           m_a_lambda, m_a_w_mem_kv, m_a_w_out, m_a_w_ffn_in, m_a_w_ffn_out, m_kv_norm_g, m_w_kv_shared, m_b_pre_mix_g,
           m_b_post_mix_g, m_b_pre_ffn_g, m_b_post_ffn_g, m_b_w_in, m_b_w_mem_kv, m_b_w_out, m_b_w_ffn_in, m_b_w_ffn_out,
           v_mem_norm_g, v_a_pre_mix_g, v_a_post_mix_g, v_a_pre_ffn_g, v_a_post_ffn_g, v_a_w_in, v_a_conv_w, v_a_conv_b,
           v_a_gate_a_w, v_a_gate_a_b, v_a_gate_x_w, v_a_gate_x_b, v_a_lambda, v_a_w_mem_kv, v_a_w_out, v_a_w_ffn_in,
           v_a_w_ffn_out, v_kv_norm_g, v_w_kv_shared, v_b_pre_mix_g, v_b_post_mix_g, v_b_pre_ffn_g, v_b_post_ffn_g,
           v_b_w_in, v_b_w_mem_kv, v_b_w_out, v_b_w_ffn_in, v_b_w_ffn_out):
    a = dict(locals())
    xi, yi, ci = _coords()
    chip = 2 * xi + yi
    c_arr = jnp.stack([ci, chip]).astype(jnp.int32)

    got = small_gather(_pack([a[n] for n in SHARDED_SMALL]), "small_gather")
    per_chip = [_unpack(got[s], [a[n].shape for n in SHARDED_SMALL]) for s in range(N_CHIPS)]
    small = {n: jnp.concatenate([per_chip[s][k] for s in range(N_CHIPS)], axis=-1)
             for k, n in enumerate(SHARDED_SMALL)}
    small.update({n: a[n] for n in REPL_SMALL})

    groups = []
    for l in range(4):
        p, j = ("a", l) if l < 2 else ("b", l - 2)
        groups.append([(p + "_" + n, j) for n in ("w_in", "w_mem_kv", "w_out")])
        groups.append([(p + "_" + n, j) for n in ("w_ffn_in", "w_ffn_out")])
    groups[3].append(("w_kv_shared", None))
    big = {n: [None, None] for n in BIG if n != "w_kv_shared"}
    gs, gb = {}, {}
    reduced = {n: [None, None] for n in BIG if n != "w_kv_shared"}

    def put(store, n, j, val):
        if j is None:
            store[n] = val
        else:
            store[n][j] = val

    class Exchange:
        def __init__(self):
            self.state = {}

        def gather_ici(self, g):
            shards = [(a[n] if j is None else a[n][j]).astype(MXU) for n, j in groups[g]]
            rows = [sh.shape[0] for sh in shards]
            lands = [lax.empty((N_CHIPS,) + sh.shape, sh.dtype) for sh in shards]
            plan = plan_gather_ici(len(shards), rows)
            ss, rs, bufs, tok = split_start("gather_ici_%d" % g, shards + lands, plan, 4 * len(shards))
            self.state["g", g] = (ss, rs, bufs, plan, rows)
            return tok

        def gather_d2d(self, g, after):
            ss, rs, bufs, plan, rows = self.state.pop(("g", g))
            n = len(rows)
            outs = split_wait("gather_ici_wait_%d" % g, ss, rs, bufs, after, plan)[n:]
            plan = plan_gather_d2d(n, rows)
            ss, rs, bufs, tok = split_start("gather_d2d_%d" % g, outs, plan, 3 * n)
            self.state["g", g] = (ss, rs, bufs, plan)
            return tok

        def gather_done(self, g, after):
            ss, rs, bufs, plan = self.state.pop(("g", g))
            outs = split_wait("gather_d2d_wait_%d" % g, ss, rs, bufs, after, plan)
            for (n, j), w in zip(groups[g], outs):
                put(big, n, j, w if BIG[n] else w.reshape(-1, w.shape[-1]))

        def rs_swap(self, g):
            grads = []
            for n, j in groups[g]:
                gr = gb[n] if j is None else gb[n][j]
                grads.append(gr if BIG[n] else gr.reshape(N_CHIPS, gr.shape[0] // N_CHIPS, gr.shape[1]))
            rows = [gr.shape[1] for gr in grads]
            lands = [lax.empty((N_CHIPS, gr.shape[1] // 2, gr.shape[2]), F32) for gr in grads]
            plan = plan_swap(len(grads), rows)
            ss, rs, bufs, tok = split_start("rs_swap_%d" % g, grads + lands, plan, len(grads))
            self.state["r", g] = (ss, rs, bufs, plan, rows)
            return tok

        def rs_exchange(self, g, after):
            ss, rs, bufs, plan, rows = self.state.pop(("r", g))
            n = len(rows)
            bufs = split_wait("rs_swap_wait_%d" % g, ss, rs, bufs, after, plan)
            sums = [half_sum(gr, got, c_arr, "rs_half_sum") for gr, got in zip(bufs[:n], bufs[n:])]
            plan = plan_exchange(n)
            ss, rs, bufs, tok = split_start("rs_exchange_%d" % g, [p for p, _ in sums] + [s for _, s in sums], plan,
                                            3 * n)
            self.state["r", g] = (ss, rs, bufs, plan, rows)
            return tok

        def rs_share(self, g, after):
            ss, rs, bufs, plan, rows = self.state.pop(("r", g))
            n = len(rows)
            slots = split_wait("rs_exchange_wait_%d" % g, ss, rs, bufs, after, plan)[n:]
            fulls = [slot_sum(s, c_arr, "rs_slot_sum") for s in slots]
            plan = plan_share(n, rows)
            ss, rs, bufs, tok = split_start("rs_share_%d" % g, fulls, plan, n)
            self.state["r", g] = (ss, rs, bufs, plan)
            return tok

        def rs_done(self, g, after):
            ss, rs, bufs, plan = self.state.pop(("r", g))
            outs = split_wait("rs_share_wait_%d" % g, ss, rs, bufs, after, plan)
            for (n, j), r in zip(groups[g], outs):
                put(reduced, n, j, r)

        def hook(self, where, l, after):
            mix, ffn = 2 * l, 2 * l + 1
            toks = []
            if where == "fwd_begin":
                if l == 0:
                    tok = self.gather_ici(mix)
                    tok = self.gather_d2d(mix, tok)
                    self.gather_done(mix, tok)
                toks.append(self.gather_ici(ffn))
            elif where == "fwd_q1":
                toks.append(self.gather_d2d(ffn, after))
            elif where == "fwd_mid":
                self.gather_done(ffn, after)
                if l < 3:
                    toks.append(self.gather_ici(mix + 2))
            elif where == "fwd_q3":
                if l < 3:
                    toks.append(self.gather_d2d(mix + 2, after))
            elif where == "fwd_end":
                if l < 3:
                    self.gather_done(mix + 2, after)
            elif where == "bwd_begin":
                if l < 3:
                    self.rs_done(ffn + 2, after)
                    toks.append(self.rs_exchange(mix + 2, after))
            elif where == "bwd_mid1":
                if l < 3:
                    toks.append(self.rs_share(mix + 2, after))
            elif where == "bwd_mid2":
                if l < 3:
                    self.rs_done(mix + 2, after)
                toks.append(self.rs_swap(ffn))
            elif where == "bwd_m1":
                toks.append(self.rs_exchange(ffn, after))
            elif where == "bwd_end":
                toks.append(self.rs_share(ffn, after))
                toks.append(self.rs_swap(mix))
                if l == 0:
                    self.rs_done(ffn, toks[0])
                    tok = self.rs_exchange(mix, toks[1])
                    tok = self.rs_share(mix, tok)
                    self.rs_done(mix, tok)
                    toks = []
            else:
                raise ValueError(where)
            return toks

    loss_parts, dx = _fwd_bwd(x[0], mem[0], loss_target[0], small, big, gs, gb, Exchange())
    loss = lax.psum(jnp.sum(loss_parts) * (0.5 / D), ("x", "y", "c"))

    res = {}
    reduced["w_kv_shared"] = [reduced["w_kv_shared"]]
    for n in BIG:
        shape = a[n].shape
        rows, cols = shape[-2], shape[-1]
        stk = (-1, rows, cols)
        outs = adamw(a[n].reshape(stk), reduced[n], a["m_" + n].reshape(stk), a["v_" + n].reshape(stk), "adamw")
        res[n] = [o.reshape(shape) for o in outs]

    def full(n):
        g = gs[n]
        return jnp.stack(g) if isinstance(g, list) else g

    order = SHARDED_SMALL + REPL_SMALL
    full_shapes = [full(n).shape for n in order]
    summed = _unpack(small_allreduce(_pack([full(n) for n in order]), "small_allreduce"), full_shapes)
    mine = []
    for n, g in zip(order, summed):
        if n in SHARDED_SMALL:
            width = a[n].shape[-1]
            g = lax.dynamic_slice_in_dim(g, chip * width, width, axis=g.ndim - 1)
        mine.append(g.reshape(a[n].shape))
    shapes = [a[n].shape for n in order]
    rm = 512
    outs = adamw(_pack([a[n] for n in order], rm)[None], [_pack(mine, rm)],
                 _pack([a["m_" + n] for n in order], rm)[None], _pack([a["v_" + n] for n in order], rm)[None],
                 "adamw_small")
    unpacked = [_unpack(o[0], shapes) for o in outs]
    for k, n in enumerate(order):
        res[n] = [u[k] for u in unpacked]

    return (loss, dx[None], *[res[n][0] for n in WEIGHTS], *[res[n][1] for n in WEIGHTS],
            *[res[n][2] for n in WEIGHTS], *[res[n][3] for n in WEIGHTS])
```

```python
import functools
import math

import jax
import jax.numpy as jnp
from jax import lax
from jax.experimental import pallas as pl
from jax.experimental.pallas import tpu as pltpu

D = 2048
HD = 128
MEM_W = 512
MEM_HEADS = 4
MIX_W = D - MEM_W
N_BLK = MIX_W // HD
D_FF = 5632
N_MEM = 256
RMS_EPS = 1e-6
NEG_INF = -1e30
LRU_C = 8.0
DIL_GROUPS = ((128, 1), (512, 4), (2048, 16))
Q_BLOCK = 128
SCALE = HD ** -0.5
N_CHIPS = 4
MXU_COLS = 256

ADAM_LR = 0.001
ADAM_B1 = 0.9
ADAM_B2 = 0.999
ADAM_EPS = 1e-08
ADAM_WD = 0.01
ADAM_STEP = 10

MXU = jnp.bfloat16
F32 = jnp.float32
VMEM_LIMIT_BYTES = 56 * 1024 * 1024

BS = pl.BlockSpec
SDS = jax.ShapeDtypeStruct
MESH = pl.DeviceIdType.MESH


def _cp(*sem):
    return pltpu.CompilerParams(dimension_semantics=sem or None, vmem_limit_bytes=VMEM_LIMIT_BYTES)


def _dot(a, b, dn=((1,), (0,))):
    return lax.dot_general(a, b, (dn, ((), ())), preferred_element_type=F32)


def _div(i, n):
    return lax.div(i, jnp.int32(n))


def _rem(i, n):
    return lax.rem(i, jnp.int32(n))


NN = ((1,), (0,))
NT = ((1,), (1,))
TN = ((0,), (0,))


def _sigmoid(z):
    return 1.0 / (1.0 + jnp.exp(-z))


def _log1p_pos(u):
    return jnp.where(u < 1e-2, u * (1.0 - u * (0.5 - u * (1.0 / 3.0))), jnp.log(1.0 + u))


def _neg_expm1(z):
    return jnp.where(z > -1e-2, -z * (1.0 + z * (0.5 + z * (1.0 / 6.0))), 1.0 - jnp.exp(z))


def _softplus(z):
    return jnp.maximum(z, 0.0) + _log1p_pos(jnp.exp(-jnp.abs(z)))


_GELU_C = math.sqrt(2.0 / math.pi)


def _gelu_and_grad(x):
    x2 = x * x
    t = jnp.tanh(_GELU_C * (x + 0.044715 * x * x2))
    g = 0.5 * x * (1.0 + t)
    dg = 0.5 * (1.0 + t) + 0.5 * x * (1.0 - t * t) * _GELU_C * (1.0 + 3.0 * 0.044715 * x2)
    return g, dg


def _row_tile(rows):
    return min(256, rows)


def norm_cast(x, g, name):
    rows = x.shape[0]
    tr = _row_tile(rows)

    def body(x_ref, g_ref, o_ref):
        xv = x_ref[...]
        r = lax.rsqrt(jnp.mean(xv * xv, axis=-1, keepdims=True) + RMS_EPS)
        o_ref[...] = (xv * r * g_ref[...]).astype(o_ref.dtype)

    return pl.pallas_call(
        body, grid=(rows // tr,),
        in_specs=[BS((tr, D), lambda i: (i, 0)), BS((1, D), lambda i: (0, 0))],
        out_specs=BS((tr, D), lambda i: (i, 0)),
        out_shape=SDS((rows, D), MXU), compiler_params=_cp("parallel"), name=name,
    )(x, g.reshape(1, D))


def resid_norm(h, y, g, name):
    rows = h.shape[0]
    tr = _row_tile(rows)

    def body(h_ref, y_ref, g_ref, o_ref):
        yv = y_ref[...]
        r = lax.rsqrt(jnp.mean(yv * yv, axis=-1, keepdims=True) + RMS_EPS)
        o_ref[...] = h_ref[...] + yv * r * g_ref[...]

    return pl.pallas_call(
        body, grid=(rows // tr,),
        in_specs=[BS((tr, D), lambda i: (i, 0)), BS((tr, D), lambda i: (i, 0)), BS((1, D), lambda i: (0, 0))],
        out_specs=BS((tr, D), lambda i: (i, 0)),
        out_shape=SDS((rows, D), F32), compiler_params=_cp("parallel"), name=name,
    )(h, y, g.reshape(1, D))


def resid_norm_next(h, y, g, g_next, name):
    rows = h.shape[0]
    tr = _row_tile(rows)

    def body(h_ref, y_ref, g_ref, gn_ref, o_ref, n_ref):
        yv = y_ref[...]
        r = lax.rsqrt(jnp.mean(yv * yv, axis=-1, keepdims=True) + RMS_EPS)
        hv = h_ref[...] + yv * r * g_ref[...]
        o_ref[...] = hv
        r2 = lax.rsqrt(jnp.mean(hv * hv, axis=-1, keepdims=True) + RMS_EPS)
        n_ref[...] = (hv * r2 * gn_ref[...]).astype(n_ref.dtype)

    row = BS((tr, D), lambda i: (i, 0))
    vec = BS((1, D), lambda i: (0, 0))
    return pl.pallas_call(
        body, grid=(rows // tr,), in_specs=[row, row, vec, vec], out_specs=[row, row],
        out_shape=[SDS((rows, D), F32), SDS((rows, D), MXU)], compiler_params=_cp("parallel"), name=name,
    )(h, y, g.reshape(1, D), g_next.reshape(1, D))


def norm_bwd(x, g, dy, res, out_dtype, name):
    rows = x.shape[0]
    tr = _row_tile(rows)
    has_res = res is not None

    def body(*refs):
        if has_res:
            x_ref, g_ref, dy_ref, res_ref, dx_ref, dg_ref = refs
        else:
            x_ref, g_ref, dy_ref, dx_ref, dg_ref = refs
        xv = x_ref[...]
        dyv = dy_ref[...].astype(F32)
        r = lax.rsqrt(jnp.mean(xv * xv, axis=-1, keepdims=True) + RMS_EPS)
        xhat = xv * r
        dxhat = dyv * g_ref[...]
        dx = r * (dxhat - xhat * jnp.mean(dxhat * xhat, axis=-1, keepdims=True))
        if has_res:
            dx = dx + res_ref[...]
        dx_ref[...] = dx.astype(dx_ref.dtype)

        @pl.when(pl.program_id(0) == 0)
        def _():
            dg_ref[...] = jnp.zeros_like(dg_ref)

        dg_ref[...] += jnp.sum(dyv * xhat, axis=0, keepdims=True)

    row = BS((tr, D), lambda i: (i, 0))
    vec = BS((1, D), lambda i: (0, 0))
    ins = [x, g.reshape(1, D), dy] + ([res] if has_res else [])
    dx, dg = pl.pallas_call(
        body, grid=(rows // tr,),
        in_specs=[row, vec, row] + ([row] if has_res else []),
        out_specs=[row, vec],
        out_shape=[SDS((rows, D), out_dtype), SDS((1, D), F32)],
        compiler_params=_cp("arbitrary"), name=name,
    )(*ins)
    return dx, dg.reshape(D)


def loss_head(y, target, name):
    rows = y.shape[0]
    tr = _row_tile(rows)

    def body(y_ref, t_ref, dy_ref, acc_ref):
        err = y_ref[...] - t_ref[...]
        dy_ref[...] = err * (1.0 / D)

        @pl.when(pl.program_id(0) == 0)
        def _():
            acc_ref[...] = jnp.zeros_like(acc_ref)

        acc_ref[...] += jnp.sum(err * err, axis=0, keepdims=True)

    row = BS((tr, D), lambda i: (i, 0))
    dy, acc = pl.pallas_call(
        body, grid=(rows // tr,), in_specs=[row, row],
        out_specs=[row, BS((1, D), lambda i: (0, 0))],
        out_shape=[SDS((rows, D), F32), SDS((1, D), F32)],
        compiler_params=_cp("arbitrary"), name=name,
    )(y, target)
    return acc, dy


def _mm_call(ins, in_specs, pick, dn, grid, o_spec, out_sds, name):
    gk = grid[2]
    n_in = len(ins)

    def body(*refs):
        o_ref = refs[n_in]
        k = pl.program_id(2)

        def step(a_ref, b_ref):
            p = _dot(a_ref[...], b_ref[...], dn)
            if gk == 1:
                o_ref[...] = p.astype(o_ref.dtype)
            else:
                acc = o_ref if out_sds.dtype == F32 else refs[n_in + 1]

                @pl.when(k == 0)
                def _():
                    acc[...] = p

                @pl.when(k > 0)
                def _():
                    acc[...] += p

                if acc is not o_ref:
                    @pl.when(k == gk - 1)
                    def _():
                        o_ref[...] = acc[...].astype(o_ref.dtype)

        pick(refs[:n_in], k, step)

    scratch = []
    if gk > 1 and out_sds.dtype != F32:
        scratch = [pltpu.VMEM(o_spec.block_shape[-2:], F32)]
    return pl.pallas_call(
        body, grid=grid, in_specs=in_specs, out_specs=o_spec, out_shape=out_sds,
        scratch_shapes=scratch, compiler_params=_cp("parallel", "parallel", "arbitrary"), name=name,
    )(*ins)


def _pick2(refs, k, step):
    step(refs[0], refs[1])


def mm_nn(a, w, *, tm, tn, tk, out_dtype, name):
    m, kdim = a.shape
    if w.ndim == 3:
        c = w.shape[2]
        n = N_CHIPS * c
        per = c // tn
        b_spec = BS((None, tk, tn), lambda i, j, k: (_div(j, per), k, _rem(j, per)))
    else:
        n = w.shape[1]
        b_spec = BS((tk, tn), lambda i, j, k: (k, j))
    grid = (m // tm, n // tn, kdim // tk)
    return _mm_call([a, w], [BS((tm, tk), lambda i, j, k: (i, k)), b_spec], _pick2, NN, grid,
                    BS((tm, tn), lambda i, j, k: (i, j)), SDS((m, n), out_dtype), name)


def mm_nt(a_list, w, *, tm, tn, tk, out_dtype, name):
    m = a_list[0].shape[0]
    ka = a_list[0].shape[1]
    n_a = len(a_list)
    kdim = ka * n_a
    if w.ndim == 3:
        c = w.shape[2]
        n = w.shape[1]
        per = c // tk
        b_spec = BS((None, tn, tk), lambda i, j, k: (_div(k, per), j, _rem(k, per)))
    else:
        n = w.shape[0]
        b_spec = BS((tn, tk), lambda i, j, k: (j, k))
    gk = kdim // tk
    half = gk // n_a
    grid = (m // tm, n // tn, gk)
    if n_a == 1:
        a_specs = [BS((tm, tk), lambda i, j, k: (i, k))]
        pick = lambda refs, k, step: step(refs[0], refs[1])
    else:
        a_specs = [BS((tm, tk), lambda i, j, k: (i, jnp.minimum(k, half - 1))),
                   BS((tm, tk), lambda i, j, k: (i, jnp.maximum(k - half, 0)))]

        def pick(refs, k, step):
            @pl.when(k < half)
            def _():
                step(refs[0], refs[2])

            @pl.when(k >= half)
            def _():
                step(refs[1], refs[2])

    return _mm_call(list(a_list) + [w], a_specs + [b_spec], pick, NT, grid,
                    BS((tm, tn), lambda i, j, k: (i, j)), SDS((m, n), out_dtype), name)


def mm_tn(a, b_list, *, t1, tn, ts, col_shards, name):
    s, k1 = a.shape
    nb = b_list[0].shape[1]
    n_b = len(b_list)
    n = nb * n_b
    gn = n // tn
    half = gn // n_b
    grid = (k1 // t1, gn, s // ts)
    if col_shards:
        c = n // N_CHIPS
        per = c // tn
        o_spec = BS((None, t1, tn), lambda i, j, k: (_div(j, per), i, _rem(j, per)))
        out_sds = SDS((N_CHIPS, k1, c), F32)
    else:
        o_spec = BS((t1, tn), lambda i, j, k: (i, j))
        out_sds = SDS((k1, n), F32)
    a_spec = BS((ts, t1), lambda i, j, k: (k, i))
    if n_b == 1:
        b_specs = [BS((ts, tn), lambda i, j, k: (k, j))]
        pick = lambda refs, k, step: step(refs[0], refs[1])
    else:
        b_specs = [BS((ts, tn), lambda i, j, k: (k, jnp.minimum(j, half - 1))),
                   BS((ts, tn), lambda i, j, k: (k, jnp.maximum(j - half, 0)))]

        def pick(refs, k, step):
            j = pl.program_id(1)

            @pl.when(j < half)
            def _():
                step(refs[0], refs[1])

            @pl.when(j >= half)
            def _():
                step(refs[0], refs[2])

    return _mm_call([a] + list(b_list), [a_spec] + b_specs, pick, TN, grid, o_spec, out_sds, name)


def ffn_in_fwd(hn, w, name):
    s = hn.shape[0]
    tm = min(512, s)
    tn = D_FF // 4

    def body(a_ref, wg_ref, wu_ref, dag_ref, dau_ref, act_ref):
        a = a_ref[...]
        g = _dot(a, wg_ref[...])
        u = _dot(a, wu_ref[...])
        sg = _sigmoid(g)
        silu = g * sg
        dag_ref[...] = u * sg * (1.0 + g * (1.0 - sg))
        dau_ref[...] = silu
        act_ref[...] = (silu * u).astype(act_ref.dtype)

    tile = BS((tm, tn), lambda j, i: (i, j))
    return pl.pallas_call(
        body, grid=(4, s // tm),
        in_specs=[BS((tm, D), lambda j, i: (i, 0)),
                  BS((None, D, tn), lambda j, i: (_div(j, 2), 0, _rem(j, 2))),
                  BS((None, D, tn), lambda j, i: (2 + _div(j, 2), 0, _rem(j, 2)))],
        out_specs=[tile, tile, tile],
        out_shape=[SDS((s, D_FF), F32), SDS((s, D_FF), F32), SDS((s, D_FF), MXU)],
        compiler_params=_cp("parallel", "parallel"), name=name,
    )(hn, w, w)


def ffn_act_bwd(dy, w_out, dag, dau, name):
    s = dy.shape[0]
    tm = min(512, s)
    tn = D_FF // 4

    def body(dy_ref, w_ref, dag_ref, dau_ref, dg_ref, du_ref):
        dact = _dot(dy_ref[...], w_ref[...], NT)
        dg_ref[...] = (dact * dag_ref[...]).astype(dg_ref.dtype)
        du_ref[...] = (dact * dau_ref[...]).astype(du_ref.dtype)

    tile = BS((tm, tn), lambda j, i: (i, j))
    return pl.pallas_call(
        body, grid=(4, s // tm),
        in_specs=[BS((tm, D), lambda j, i: (i, 0)), BS((tn, D), lambda j, i: (j, 0)), tile, tile],
        out_specs=[tile, tile],
        out_shape=[SDS((s, D_FF), MXU), SDS((s, D_FF), MXU)],
        compiler_params=_cp("parallel", "parallel"), name=name,
    )(dy, w_out, dag, dau)


LRU_T = 256
HALO = 8


def _shift_down(x, k, fill):
    rows = x.shape[0]
    idx = lax.broadcasted_iota(jnp.int32, x.shape, 0)
    return jnp.where(idx < k, fill, pltpu.roll(x, k, 0))


def _shift_up(x, k, fill):
    rows = x.shape[0]
    idx = lax.broadcasted_iota(jnp.int32, x.shape, 0)
    return jnp.where(idx >= rows - k, fill, pltpu.roll(x, rows - k, 0))


def _scan_block(a, b, carry, reverse):
    rows, cols = a.shape
    sub = 8
    in_group = lax.broadcasted_iota(jnp.int32, a.shape, 0) % sub
    for sh in (1, 2, 4):
        if reverse:
            a_s, b_s, ok = pltpu.roll(a, rows - sh, 0), pltpu.roll(b, rows - sh, 0), in_group < sub - sh
        else:
            a_s, b_s, ok = pltpu.roll(a, sh, 0), pltpu.roll(b, sh, 0), in_group >= sh
        b = jnp.where(ok, a * b_s + b, b)
        a = jnp.where(ok, a * a_s, a)
    groups = list(range(rows // sub))
    edge = 0 if reverse else sub - 1
    carry_in = {}
    for v in (reversed(groups) if reverse else groups):
        carry_in[v] = carry
        row = sub * v + edge
        carry = b[row:row + 1, :] + a[row:row + 1, :] * carry
    cin = jnp.concatenate([jnp.broadcast_to(carry_in[v], (sub, cols)) for v in groups], axis=0)
    return b + a * cin


def _conv_taps(xcat):
    rows = xcat.shape[0]
    taps = []
    for k in range(4):
        off = HALO - 3 + k
        taps.append(xcat[off:off + LRU_T] if off == HALO else pltpu.roll(xcat, rows - off, 0)[:LRU_T])
    return taps


def _gates(xc, wa_ref, ba, wx_ref, bx, lam, za_ref, zx_ref):
    xm = xc.astype(MXU)
    for n in range(N_BLK):
        sl = slice(n * HD, (n + 1) * HD)
        za_ref[:, sl] = _dot(xm[:, sl], wa_ref[n])
        zx_ref[:, sl] = _dot(xm[:, sl], wx_ref[n])
    ra = _sigmoid(za_ref[...] + ba)
    ii = _sigmoid(zx_ref[...] + bx)
    sp = _softplus(-lam)
    log_a = -LRU_C * ra * sp
    a = jnp.exp(log_a)
    mult = jnp.sqrt(_neg_expm1(2.0 * log_a))
    return ra, ii, sp, a, mult


def lru_fwd(proj, conv_w, conv_b, wa, ba, wx, bx, lam, name):
    s = proj.shape[0]
    c = MIX_W
    nblk = s // LRU_T
    hpb = LRU_T // HALO

    def body(x_ref, halo_ref, cw_ref, cb_ref, wa_ref, ba_ref, wx_ref, bx_ref, lam_ref,
             xc_ref, h_ref, carry, za_ref, zx_ref):
        i = pl.program_id(0)

        @pl.when(i == 0)
        def _():
            carry[...] = jnp.zeros_like(carry)

        halo = jnp.where(i == 0, 0.0, halo_ref[...])
        xcat = jnp.concatenate([halo, x_ref[...]], axis=0)
        taps = _conv_taps(xcat)
        xc = cb_ref[...] + sum(cw_ref[k:k + 1, :] * taps[k] for k in range(4))
        xc_ref[...] = xc
        _, ii, _, a, mult = _gates(xc, wa_ref, ba_ref[...], wx_ref, bx_ref[...], lam_ref[...], za_ref, zx_ref)
        h = _scan_block(a, mult * (ii * xc), carry[HALO - 1:HALO, :], False)
        h_ref[...] = h
        carry[...] = h[LRU_T - HALO:, :]

    def full(shape):
        return BS(shape, lambda i: (0,) * len(shape))

    blk = BS((LRU_T, c), lambda i: (i, 0))
    return pl.pallas_call(
        body, grid=(nblk,),
        in_specs=[blk, BS((HALO, c), lambda i: (jnp.maximum(i * hpb - 1, 0), 0)),
                  full((4, c)), full((1, c)), full((N_BLK, HD, HD)), full((1, c)),
                  full((N_BLK, HD, HD)), full((1, c)), full((1, c))],
        out_specs=[blk, blk],
        out_shape=[SDS((s, c), F32), SDS((s, c), F32)],
        scratch_shapes=[pltpu.VMEM((HALO, c), F32), pltpu.VMEM((LRU_T, c), F32), pltpu.VMEM((LRU_T, c), F32)],
        compiler_params=_cp("arbitrary"), name=name,
    )(proj, proj, conv_w, conv_b.reshape(1, c), wa.astype(MXU), ba.reshape(1, c), wx.astype(MXU),
      bx.reshape(1, c), lam.reshape(1, c))


def lru_mix_prep(h, proj, m, name):
    s = h.shape[0]
    tr = _row_tile(s)

    def body(h_ref, gb_ref, m_ref, o_ref):
        ge, _ = _gelu_and_grad(gb_ref[...])
        o_ref[:, :MIX_W] = (h_ref[...] * ge).astype(o_ref.dtype)
        o_ref[:, MIX_W:] = m_ref[...]

    return pl.pallas_call(
        body, grid=(s // tr,),
        in_specs=[BS((tr, MIX_W), lambda i: (i, 0)), BS((tr, MIX_W), lambda i: (i, 1)),
                  BS((tr, MEM_W), lambda i: (i, 0))],
        out_specs=BS((tr, D), lambda i: (i, 0)), out_shape=SDS((s, D), MXU),
        compiler_params=_cp("parallel"), name=name,
    )(h, proj, m)


def lru_bwd(dym, proj, xc, hl, dqm, conv_w, wa, ba, wx, bx, lam, name):
    s = proj.shape[0]
    c = MIX_W
    nblk = s // LRU_T
    hpb = LRU_T // HALO
    wa_m = wa.astype(MXU)
    wx_m = wx.astype(MXU)

    def body(dy_ref, x_ref, xhalo_ref, gb_ref, xc_ref, h_ref, hhalo_ref, dqm_ref,
             cw_ref, wa_ref, ba_ref, wx_ref, bx_ref, lam_ref,
             dproj_ref, dcw_ref, dcb_ref, dwa_ref, dba_ref, dwx_ref, dbx_ref, dlam_ref,
             g_next, a_next, dxc_next, za_ref, zx_ref, dxc_ref):
        i = pl.program_id(0)

        @pl.when(i == 0)
        def _():
            g_next[...] = jnp.zeros_like(g_next)
            a_next[...] = jnp.zeros_like(a_next)
            dxc_next[...] = jnp.zeros_like(dxc_next)
            for r in (dcw_ref, dcb_ref, dwa_ref, dba_ref, dwx_ref, dbx_ref, dlam_ref):
                r[...] = jnp.zeros_like(r)

        first = i == nblk - 1
        xc = xc_ref[...]
        lam = lam_ref[...]
        ra, ii, sp, a, mult = _gates(xc, wa_ref, ba_ref[...], wx_ref, bx_ref[...], lam, za_ref, zx_ref)
        hl_v = h_ref[...]
        ge, dge = _gelu_and_grad(gb_ref[...])
        dyl = dy_ref[...]
        dhl = dyl * ge
        dproj_ref[:, c:2 * c] = (dyl * hl_v * dge).astype(dproj_ref.dtype)
        dproj_ref[:, 2 * c:] = dqm_ref[...]

        an = _shift_up(a, 1, 0.0)
        last_row = lax.broadcasted_iota(jnp.int32, a.shape, 0) == LRU_T - 1
        an = jnp.where(last_row, a_next[0:1, :], an)
        g = _scan_block(an, dhl, g_next[0:1, :], True)
        g_next[...] = g[:HALO, :]
        a_next[...] = a[:HALO, :]

        hhalo = jnp.where(first, 0.0, hhalo_ref[...])
        h_prev = _shift_down(hl_v, 1, 0.0)
        first_row = lax.broadcasted_iota(jnp.int32, a.shape, 0) == 0
        h_prev = jnp.where(first_row, hhalo[HALO - 1:HALO, :], h_prev)
        da = g * h_prev
        ixc = ii * xc
        dmult = g * ixc
        dii = g * mult * xc
        dxc = g * mult * ii
        dlog_a = (da - dmult * a / mult) * a
        dra = dlog_a * (-LRU_C) * sp
        dlam_ref[...] += jnp.sum(dlog_a * ra, axis=0, keepdims=True) * (LRU_C * _sigmoid(-lam))
        dza = dra * ra * (1.0 - ra)
        dzx = dii * ii * (1.0 - ii)
        dba_ref[...] += jnp.sum(dza, axis=0, keepdims=True)
        dbx_ref[...] += jnp.sum(dzx, axis=0, keepdims=True)
        xm = xc.astype(MXU)
        dza_m = dza.astype(MXU)
        dzx_m = dzx.astype(MXU)
        for n in range(N_BLK):
            sl = slice(n * HD, (n + 1) * HD)
            dwa_ref[n] += _dot(xm[:, sl], dza_m[:, sl], TN)
            dwx_ref[n] += _dot(xm[:, sl], dzx_m[:, sl], TN)
            dxc_ref[:, sl] = _dot(dza_m[:, sl], wa_ref[n], NT) + _dot(dzx_m[:, sl], wx_ref[n], NT)
        dxc = dxc + dxc_ref[...]

        dcat = jnp.concatenate([dxc, dxc_next[...]], axis=0)
        rows = dcat.shape[0]
        dxb = cw_ref[3:4, :] * dxc
        for k in range(3):
            dxb = dxb + cw_ref[k:k + 1, :] * pltpu.roll(dcat, rows - (3 - k), 0)[:LRU_T]
        dproj_ref[:, :c] = dxb.astype(dproj_ref.dtype)
        dxc_next[...] = dxc[:HALO, :]

        xhalo = jnp.where(first, 0.0, xhalo_ref[...])
        taps = _conv_taps(jnp.concatenate([xhalo, x_ref[...]], axis=0))
        for k in range(4):
            dcw_ref[k:k + 1, :] += jnp.sum(dxc * taps[k], axis=0, keepdims=True)
        dcb_ref[...] += jnp.sum(dxc, axis=0, keepdims=True)

    def full(shape):
        return BS(shape, lambda i: (0,) * len(shape))

    def rev(i):
        return nblk - 1 - i

    blk0 = BS((LRU_T, c), lambda i: (rev(i), 0))
    blk1 = BS((LRU_T, c), lambda i: (rev(i), 1))
    halo = BS((HALO, c), lambda i: (jnp.maximum(rev(i) * hpb - 1, 0), 0))
    outs = pl.pallas_call(
        body, grid=(nblk,),
        in_specs=[blk0, blk0, halo, blk1, blk0, blk0, halo, BS((LRU_T, MEM_W), lambda i: (rev(i), 0)),
                  full((4, c)), full((N_BLK, HD, HD)), full((1, c)), full((N_BLK, HD, HD)), full((1, c)),
                  full((1, c))],
        out_specs=[BS((LRU_T, 2 * c + MEM_W), lambda i: (rev(i), 0)), full((4, c)), full((1, c)),
                   full((N_BLK, HD, HD)), full((1, c)), full((N_BLK, HD, HD)), full((1, c)), full((1, c))],
        out_shape=[SDS((s, 2 * c + MEM_W), MXU), SDS((4, c), F32), SDS((1, c), F32),
                   SDS((N_BLK, HD, HD), F32), SDS((1, c), F32), SDS((N_BLK, HD, HD), F32), SDS((1, c), F32),
                   SDS((1, c), F32)],
        scratch_shapes=[pltpu.VMEM((HALO, c), F32), pltpu.VMEM((HALO, c), F32), pltpu.VMEM((HALO, c), F32),
                        pltpu.VMEM((LRU_T, c), F32), pltpu.VMEM((LRU_T, c), F32), pltpu.VMEM((LRU_T, c), F32)],
        compiler_params=_cp("arbitrary"), name=name,
    )(dym, proj, proj, proj, xc, hl, hl, dqm, conv_w, wa_m, ba.reshape(1, c), wx_m, bx.reshape(1, c),
      lam.reshape(1, c))
    dproj, dcw, dcb, dwa, dba, dwx, dbx, dlam = outs
    return dproj, dcw, dcb.reshape(c), dwa, dba.reshape(c), dwx, dbx.reshape(c), dlam.reshape(c)


def _mem_probs(q, k):
    sc = _dot(q, k, NT) * SCALE
    e = jnp.exp(sc - jnp.max(sc, axis=-1, keepdims=True))
    return e / jnp.sum(e, axis=-1, keepdims=True)


def mem_attn_fwd(proj, q_col, kvm, name):
    s = proj.shape[0]
    tq = min(512, s)

    def body(q_ref, kv_ref, o_ref):
        q = q_ref[...].astype(MXU)
        for hh in range(MEM_HEADS):
            sl = slice(hh * HD, (hh + 1) * HD)
            p = _mem_probs(q[:, sl], kv_ref[:, sl])
            o_ref[:, sl] = _dot(p.astype(MXU), kv_ref[:, MEM_W + hh * HD:MEM_W + (hh + 1) * HD]).astype(o_ref.dtype)

    return pl.pallas_call(
        body, grid=(s // tq,),
        in_specs=[BS((tq, MEM_W), lambda i: (i, q_col)), BS((N_MEM, 2 * MEM_W), lambda i: (0, 0))],
        out_specs=BS((tq, MEM_W), lambda i: (i, 0)), out_shape=SDS((s, MEM_W), MXU),
        compiler_params=_cp("parallel"), name=name,
    )(proj, kvm)


def mem_attn_bwd(proj, q_col, kvm, dym, name):
    s = proj.shape[0]
    tq = min(512, s)

    def body(q_ref, kv_ref, do_ref, dq_ref, dkv_ref):
        @pl.when(pl.program_id(0) == 0)
        def _():
            dkv_ref[...] = jnp.zeros_like(dkv_ref)

        q = q_ref[...].astype(MXU)
        do = do_ref[...].astype(MXU)
        for hh in range(MEM_HEADS):
            sl = slice(hh * HD, (hh + 1) * HD)
            vsl = slice(MEM_W + hh * HD, MEM_W + (hh + 1) * HD)
            k = kv_ref[:, sl]
            p = _mem_probs(q[:, sl], k)
            dp = _dot(do[:, sl], kv_ref[:, vsl], NT)
            ds = (p * (dp - jnp.sum(p * dp, axis=-1, keepdims=True)) * SCALE).astype(MXU)
            dq_ref[:, sl] = _dot(ds, k).astype(dq_ref.dtype)
            dkv_ref[:, sl] += _dot(ds, q[:, sl], TN)
            dkv_ref[:, vsl] += _dot(p.astype(MXU), do[:, sl], TN)

    return pl.pallas_call(
        body, grid=(s // tq,),
        in_specs=[BS((tq, MEM_W), lambda i: (i, q_col)), BS((N_MEM, 2 * MEM_W), lambda i: (0, 0)),
                  BS((tq, MEM_W), lambda i: (i, MIX_W // MEM_W))],
        out_specs=[BS((tq, MEM_W), lambda i: (i, 0)), BS((N_MEM, 2 * MEM_W), lambda i: (0, 0))],
        out_shape=[SDS((s, MEM_W), MXU), SDS((N_MEM, 2 * MEM_W), F32)],
        compiler_params=_cp("arbitrary"), name=name,
    )(proj, kvm, dym)


def _dil_scores(q, kp, kc, n, slope_dil):
    qi = lax.broadcasted_iota(jnp.int32, (Q_BLOCK, Q_BLOCK), 0)
    ki = lax.broadcasted_iota(jnp.int32, (Q_BLOCK, Q_BLOCK), 1)
    rel_p = qi + Q_BLOCK - ki
    rel_c = qi - ki
    s_p = _dot(q, kp, NT) * SCALE - slope_dil * rel_p.astype(F32)
    s_c = _dot(q, kc, NT) * SCALE - slope_dil * rel_c.astype(F32)
    s_p = jnp.where((rel_p <= Q_BLOCK) & (n > 0), s_p, NEG_INF)
    s_c = jnp.where(rel_c >= 0, s_c, NEG_INF)
    return s_p, s_c


def _slope_dil(gi, hh):
    head = 4 * gi + hh
    return DIL_GROUPS[gi][1] * 2.0 ** (-8.0 * (head + 1.0) / N_BLK)


def _dil_operands(proj, kv, gi):
    dil = DIL_GROUPS[gi][1]
    if dil == 1:
        return proj, kv, kv, (lambda r: gi), (lambda r: gi), (lambda r: MIX_W // MEM_W + gi)
    sub = proj.shape[0] // dil

    def view(a, col):
        return a[:, col:col + MEM_W].reshape(sub, dil * MEM_W)

    same = lambda r: r
    return view(proj, gi * MEM_W), view(kv, gi * MEM_W), view(kv, MIX_W + gi * MEM_W), same, same, same


def dil_attn_fwd(proj, kv, gi, name):
    dil = DIL_GROUPS[gi][1]
    s, pw = proj.shape
    sub = s // dil
    nb = sub // Q_BLOCK
    qc, kc_ = pw // MEM_W, kv.shape[1] // MEM_W

    def body(q_ref, kp_ref, kc_ref, vp_ref, vc_ref, o_ref, lse_ref):
        n = pl.program_id(1)
        q = q_ref[...].astype(MXU)
        for hh in range(4):
            sl = slice(hh * HD, (hh + 1) * HD)
            s_p, s_c = _dil_scores(q[:, sl], kp_ref[:, sl], kc_ref[:, sl], n, _slope_dil(gi, hh))
            mx = jnp.maximum(jnp.max(s_p, axis=-1, keepdims=True), jnp.max(s_c, axis=-1, keepdims=True))
            den = jnp.sum(jnp.exp(s_p - mx), axis=-1, keepdims=True) + jnp.sum(jnp.exp(s_c - mx), axis=-1, keepdims=True)
            lse = mx + jnp.log(den)
            o_ref[:, sl] = (_dot(jnp.exp(s_p - lse).astype(MXU), vp_ref[:, sl])
                            + _dot(jnp.exp(s_c - lse).astype(MXU), vc_ref[:, sl]))
            lse_ref[:, sl] = jnp.broadcast_to(lse, (Q_BLOCK, HD))

    blk = (Q_BLOCK, MEM_W)
    prev = lambda n: jnp.maximum(n - 1, 0)
    out = BS(blk, lambda r, n: (n, r))
    qv, kview, vview, qcol, kcol, vcol = _dil_operands(proj, kv, gi)
    return pl.pallas_call(
        body, grid=(dil, nb),
        in_specs=[BS(blk, lambda r, n: (n, qcol(r))),
                  BS(blk, lambda r, n: (prev(n), kcol(r))), BS(blk, lambda r, n: (n, kcol(r))),
                  BS(blk, lambda r, n: (prev(n), vcol(r))), BS(blk, lambda r, n: (n, vcol(r)))],
        out_specs=[out, out],
        out_shape=[SDS((sub, dil * MEM_W), F32), SDS((sub, dil * MEM_W), F32)],
        compiler_params=_cp("parallel", "parallel"), name=name,
    )(qv, kview, kview, vview, vview)


def dil_attn_bwd(proj, kv, lse, do, dd, gi, name):
    dil = DIL_GROUPS[gi][1]
    s, pw = proj.shape
    sub = s // dil
    nb = sub // Q_BLOCK
    qc, kc_ = pw // MEM_W, kv.shape[1] // MEM_W

    def body(q_ref, kp_ref, kc_ref, vp_ref, vc_ref, lse_ref, do_ref, dd_ref, dq_ref, dk_ref, dv_ref, ck, cv):
        n = pl.program_id(1)

        @pl.when(n == 0)
        def _():
            ck[...] = jnp.zeros_like(ck)
            cv[...] = jnp.zeros_like(cv)

        @pl.when(n < nb)
        def _():
            q = q_ref[...].astype(MXU)
            do_m = do_ref[...].astype(MXU)
            for hh in range(4):
                sl = slice(hh * HD, (hh + 1) * HD)
                s_p, s_c = _dil_scores(q[:, sl], kp_ref[:, sl], kc_ref[:, sl], n, _slope_dil(gi, hh))
                lse_h = lse_ref[:, sl]
                dd_h = dd_ref[:, sl]
                p_p = jnp.exp(s_p - lse_h)
                p_c = jnp.exp(s_c - lse_h)
                ds_p = (p_p * (_dot(do_m[:, sl], vp_ref[:, sl], NT) + dd_h) * SCALE).astype(MXU)
                ds_c = (p_c * (_dot(do_m[:, sl], vc_ref[:, sl], NT) + dd_h) * SCALE).astype(MXU)
                dq_ref[:, sl] = (_dot(ds_p, kp_ref[:, sl]) + _dot(ds_c, kc_ref[:, sl])).astype(dq_ref.dtype)
                dk_ref[:, sl] = ck[:, sl] + _dot(ds_p, q[:, sl], TN)
                dv_ref[:, sl] = cv[:, sl] + _dot(p_p.astype(MXU), do_m[:, sl], TN)
                ck[:, sl] = _dot(ds_c, q[:, sl], TN)
                cv[:, sl] = _dot(p_c.astype(MXU), do_m[:, sl], TN)

        @pl.when(n == nb)
        def _():
            dk_ref[...] = ck[...]
            dv_ref[...] = cv[...]

    blk = (Q_BLOCK, MEM_W)
    cur = lambda n: jnp.minimum(n, nb - 1)
    prev = lambda n: jnp.maximum(jnp.minimum(n, nb - 1) - 1, 0)
    done = lambda n: jnp.maximum(n - 1, 0)
    own = BS(blk, lambda r, n: (cur(n), r))
    qv, kview, vview, qcol, kcol, vcol = _dil_operands(proj, kv, gi)
    return pl.pallas_call(
        body, grid=(dil, nb + 1),
        in_specs=[BS(blk, lambda r, n: (cur(n), qcol(r))),
                  BS(blk, lambda r, n: (prev(n), kcol(r))), BS(blk, lambda r, n: (cur(n), kcol(r))),
                  BS(blk, lambda r, n: (prev(n), vcol(r))), BS(blk, lambda r, n: (cur(n), vcol(r))),
                  own, own, own],
        out_specs=[own, BS(blk, lambda r, n: (done(n), r)), BS(blk, lambda r, n: (done(n), r))],
        out_shape=[SDS((sub, dil * MEM_W), MXU), SDS((sub, dil * MEM_W), F32), SDS((sub, dil * MEM_W), F32)],
        scratch_shapes=[pltpu.VMEM(blk, F32), pltpu.VMEM(blk, F32)],
        compiler_params=_cp("parallel", "arbitrary"), name=name,
    )(qv, kview, kview, vview, vview, lse, do, dd)


def _group_weights(lse_refs):
    l0, l1, l2 = (r[...] for r in lse_refs)
    mx = jnp.maximum(jnp.maximum(l0, l1), l2)
    e = [jnp.exp(l - mx) for l in (l0, l1, l2)]
    den = e[0] + e[1] + e[2]
    return [x / den for x in e]


def dil_mix_prep(o_list, lse_list, m, name):
    s = m.shape[0]
    tr = _row_tile(s)

    def body(o0, o1, o2, l0, l1, l2, m_ref, out_ref):
        w = _group_weights((l0, l1, l2))
        for g, o_ref in enumerate((o0, o1, o2)):
            out_ref[:, g * MEM_W:(g + 1) * MEM_W] = (o_ref[...] * w[g]).astype(out_ref.dtype)
        out_ref[:, MIX_W:] = m_ref[...]

    blk = BS((tr, MEM_W), lambda i: (i, 0))
    return pl.pallas_call(
        body, grid=(s // tr,), in_specs=[blk] * 7,
        out_specs=BS((tr, D), lambda i: (i, 0)), out_shape=SDS((s, D), MXU),
        compiler_params=_cp("parallel"), name=name,
    )(*o_list, *lse_list, m)


def dil_mix_bwd(dym, o_list, lse_list, name):
    s = dym.shape[0]
    tr = _row_tile(s)

    def body(da_ref, o0, o1, o2, l0, l1, l2, do0, do1, do2, dd0, dd1, dd2):
        w = _group_weights((l0, l1, l2))
        tot = None
        for g, (o_ref, do_ref) in enumerate(zip((o0, o1, o2), (do0, do1, do2))):
            da = da_ref[:, g * MEM_W:(g + 1) * MEM_W]
            do_ref[...] = da * w[g]
            x = da * o_ref[...]
            dw = jnp.concatenate(
                [jnp.broadcast_to(jnp.sum(x[:, hh * HD:(hh + 1) * HD], axis=-1, keepdims=True), (tr, HD))
                 for hh in range(4)], axis=1)
            tot = w[g] * dw if tot is None else tot + w[g] * dw
        for g, dd_ref in enumerate((dd0, dd1, dd2)):
            dd_ref[...] = -w[g] * tot

    blk = BS((tr, MEM_W), lambda i: (i, 0))
    outs = pl.pallas_call(
        body, grid=(s // tr,), in_specs=[BS((tr, MIX_W), lambda i: (i, 0))] + [blk] * 6,
        out_specs=[blk] * 6, out_shape=[SDS((s, MEM_W), F32)] * 6,
        compiler_params=_cp("parallel"), name=name,
    )(dym, *o_list, *lse_list)
    return outs[:3], outs[3:]


def sum_cast(parts, name):
    s = parts[0][0].shape[0]
    tr = _row_tile(s)
    flat = [a for p in parts for a in p]
    sizes = [len(p) for p in parts]

    def body(*refs):
        out_ref = refs[-1]
        pos = 0
        for j, n in enumerate(sizes):
            acc = refs[pos][...].astype(F32)
            for t in range(1, n):
                acc = acc + refs[pos + t][...].astype(F32)
            out_ref[:, j * MEM_W:(j + 1) * MEM_W] = acc.astype(out_ref.dtype)
            pos += n

    blk = BS((tr, MEM_W), lambda i: (i, 0))
    width = MEM_W * len(parts)
    return pl.pallas_call(
        body, grid=(s // tr,), in_specs=[blk] * len(flat),
        out_specs=BS((tr, width), lambda i: (i, 0)), out_shape=SDS((s, width), MXU),
        compiler_params=_cp("parallel"), name=name,
    )(*flat)


def add_n(arrs, name):
    rows, cols = arrs[0].shape
    tr = _row_tile(rows)

    def body(*refs):
        acc = refs[0][...]
        for r in refs[1:-1]:
            acc = acc + r[...]
        refs[-1][...] = acc

    blk = BS((tr, cols), lambda i: (i, 0))
    return pl.pallas_call(
        body, grid=(rows // tr,), in_specs=[blk] * len(arrs), out_specs=blk,
        out_shape=SDS((rows, cols), F32), compiler_params=_cp("parallel"), name=name,
    )(*arrs)


class _NoExchange:
    def hook(self, where, l, after):
        return []


def _fwd_bwd(x, mem, target, small, big, gs, gb, sched):
    s = x.shape[0]
    tm = min(1024, s)
    ts = min(2048, s)

    def after_hook(arr, where, l, after):
        toks = sched.hook(where, l, after)
        return tie(arr, toks, "tie_%s_%d" % (where, l)) if toks else arr

    h = x
    saved = []
    kv = None
    mem_n = None
    hn = norm_cast(h, small["a_pre_mix_g"][0], "pre_norm")
    for l in range(4):
        rec = l < 2
        p, j = ("a", l) if rec else ("b", l - 2)
        sv = {"h": h}
        hn = after_hook(hn, "fwd_begin", l, h)
        if mem_n is None:
            mem_n = norm_cast(mem, small["mem_norm_g"], "mem_norm")
        kvm = mm_nn(mem_n, big[p + "_w_mem_kv"][j], tm=N_MEM, tn=2 * MEM_W, tk=D, out_dtype=MXU, name="mem_kv")
        if rec:
            proj = mm_nn(hn, big["a_w_in"][j], tm=tm, tn=896, tk=D, out_dtype=F32, name="rec_in")
            xc, hl = lru_fwd(proj, small["a_conv_w"][j], small["a_conv_b"][j], small["a_gate_a_w"][j],
                             small["a_gate_a_b"][j], small["a_gate_x_w"][j], small["a_gate_x_b"][j],
                             small["a_lambda"][j], "lru_fwd")
            m = mem_attn_fwd(proj, 2 * MIX_W // MEM_W, kvm, "rec_mem_attn")
            ym = lru_mix_prep(hl, proj, m, "lru_mix_prep")
            sv.update(xc=xc, hl=hl)
        else:
            proj = mm_nn(hn, big["b_w_in"][j], tm=tm, tn=1024, tk=D, out_dtype=F32, name="dil_in")
            o_list, lse_list = [], []
            for gi in range(3):
                o, lse = dil_attn_fwd(proj, kv, gi, "dil_attn_fwd%d" % gi)
                o_list.append(o.reshape(s, MEM_W))
                lse_list.append(lse.reshape(s, MEM_W))
            m = mem_attn_fwd(proj, MIX_W // MEM_W, kvm, "dil_mem_attn")
            ym = dil_mix_prep(o_list, lse_list, m, "dil_mix_prep")
            sv.update(o=o_list, lse=lse_list)
        ym = after_hook(ym, "fwd_q1", l, ym)
        mix = mm_nn(ym, big[p + "_w_out"][j], tm=tm, tn=1024, tk=D, out_dtype=F32, name="mix_out")
        h1, hn2 = resid_norm_next(h, mix, small[p + "_post_mix_g"][j], small[p + "_pre_ffn_g"][j], "post_pre_norm")
        hn2 = after_hook(hn2, "fwd_mid", l, mix)
        g, u, act = ffn_in_fwd(hn2, big[p + "_w_ffn_in"][j], "ffn_in")
        act = after_hook(act, "fwd_q3", l, u)
        y2 = mm_nn(act, big[p + "_w_ffn_out"][j], tm=tm // 2, tn=D, tk=D_FF // 2, out_dtype=F32, name="ffn_out")
        sv.update(kvm=kvm, hn=hn, proj=proj, ym=ym, mix=mix, h1=h1, hn2=hn2, g=g, u=u, act=act, y2=y2)
        saved.append(sv)
        if l < 3:
            pn, jn = ("a", l + 1) if l + 1 < 2 else ("b", l - 1)
            h, hn = resid_norm_next(h1, y2, small[p + "_post_ffn_g"][j], small[pn + "_pre_mix_g"][jn],
                                    "post_pre_norm")
        else:
            h = resid_norm(h1, y2, small[p + "_post_ffn_g"][j], "post_norm")
        sched.hook("fwd_end", l, h)
        if l == 1:
            h_kv = h
            kvn = norm_cast(h, small["kv_norm_g"], "pre_norm")
            kv = mm_nn(kvn, big["w_kv_shared"], tm=tm, tn=768, tk=D, out_dtype=MXU, name="kv_proj")

    loss_parts, dh = loss_head(h, target, "loss_head")

    def stack2(name, j, val):
        gs.setdefault(name, [None, None])[j] = val

    def stack2b(name, j, val):
        gb.setdefault(name, [None, None])[j] = val

    dkv_parts = []
    dmem_parts = []
    dkvm = [None] * 4
    for l in (3, 2, 1, 0):
        rec = l < 2
        p, j = ("a", l) if rec else ("b", l - 2)
        sv = saved[l]
        if l == 1:
            dkv = sum_cast([(dkv_parts[0][c], dkv_parts[1][c]) for c in range(6)], "dkv_sum")
            dkvn = mm_nt([dkv], big["w_kv_shared"], tm=tm, tn=D, tk=768, out_dtype=F32, name="kv_proj_dx")
            gb["w_kv_shared"] = mm_tn(kvn, [dkv], t1=D, tn=768, ts=ts, col_shards=True, name="kv_proj_dw")
            dh, gs["kv_norm_g"] = norm_bwd(h_kv, small["kv_norm_g"], dkvn, dh, F32, "pre_norm_bwd")
        dy2, dg = norm_bwd(sv["y2"], small[p + "_post_ffn_g"][j], dh, None, MXU, "post_norm_bwd")
        dy2 = after_hook(dy2, "bwd_begin", l, dh)
        stack2(p + "_post_ffn_g", j, dg)
        dgg, dgu = ffn_act_bwd(dy2, big[p + "_w_ffn_out"][j], sv["g"], sv["u"], "ffn_act_bwd")
        dgg = after_hook(dgg, "bwd_mid1", l, dgu)
        stack2b(p + "_w_ffn_out", j, mm_tn(sv["act"], [dy2], t1=D_FF // 4, tn=D, ts=ts // 2, col_shards=False,
                                          name="ffn_out_dw"))
        dhn2 = mm_nt([dgg, dgu], big[p + "_w_ffn_in"][j], tm=tm // 2, tn=D, tk=D_FF // 2, out_dtype=F32,
                     name="ffn_in_dx")
        stack2b(p + "_w_ffn_in", j, mm_tn(sv["hn2"], [dgg, dgu], t1=D // 2, tn=D_FF // 4, ts=ts, col_shards=True,
                                         name="ffn_in_dw"))
        dhn2 = after_hook(dhn2, "bwd_mid2", l, gb[p + "_w_ffn_in"][j])
        dh1, dg = norm_bwd(sv["h1"], small[p + "_pre_ffn_g"][j], dhn2, dh, F32, "pre_norm_bwd")
        stack2(p + "_pre_ffn_g", j, dg)
        dmix, dg = norm_bwd(sv["mix"], small[p + "_post_mix_g"][j], dh1, None, MXU, "post_norm_bwd")
        stack2(p + "_post_mix_g", j, dg)
        dym = mm_nt([dmix], big[p + "_w_out"][j], tm=tm, tn=1024, tk=D, out_dtype=F32, name="mix_out_dx")
        stack2b(p + "_w_out", j, mm_tn(sv["ym"], [dmix], t1=D, tn=1024, ts=ts, col_shards=False,
                                      name="mix_out_dw"))
        dym = after_hook(dym, "bwd_m1", l, gb[p + "_w_out"][j])
        if rec:
            dqm, dkvm[l] = mem_attn_bwd(sv["proj"], 2 * MIX_W // MEM_W, sv["kvm"], dym, "rec_mem_attn_bwd")
            dproj, dcw, dcb, dwa, dba, dwx, dbx, dlam = lru_bwd(
                dym, sv["proj"], sv["xc"], sv["hl"], dqm, small["a_conv_w"][j], small["a_gate_a_w"][j],
                small["a_gate_a_b"][j], small["a_gate_x_w"][j], small["a_gate_x_b"][j], small["a_lambda"][j],
                "lru_bwd")
            for nm, val in (("a_conv_w", dcw), ("a_conv_b", dcb), ("a_gate_a_w", dwa), ("a_gate_a_b", dba),
                            ("a_gate_x_w", dwx), ("a_gate_x_b", dbx), ("a_lambda", dlam)):
                stack2(nm, j, val)
            dhn = mm_nt([dproj], big["a_w_in"][j], tm=tm, tn=D, tk=896, out_dtype=F32, name="rec_in_dx")
            stack2b("a_w_in", j, mm_tn(sv["hn"], [dproj], t1=D, tn=896, ts=ts, col_shards=True, name="rec_in_dw"))
        else:
            dqm, dkvm[l] = mem_attn_bwd(sv["proj"], MIX_W // MEM_W, sv["kvm"], dym, "dil_mem_attn_bwd")
            do_list, dd_list = dil_mix_bwd(dym, sv["o"], sv["lse"], "dil_mix_bwd")
            dq_list, dk_list, dv_list = [], [], []
            for gi in range(3):
                dil = DIL_GROUPS[gi][1]
                view = (s // dil, dil * MEM_W)
                dq, dk, dv = dil_attn_bwd(sv["proj"], kv, sv["lse"][gi].reshape(view), do_list[gi].reshape(view),
                                          dd_list[gi].reshape(view), gi, "dil_attn_bwd%d" % gi)
                dq_list.append(dq.reshape(s, MEM_W))
                dk_list.append(dk.reshape(s, MEM_W))
                dv_list.append(dv.reshape(s, MEM_W))
            dkv_parts.append(dk_list + dv_list)
            dproj = sum_cast([(a,) for a in dq_list + [dqm]], "dil_dproj")
            dhn = mm_nt([dproj], big["b_w_in"][j], tm=tm, tn=1024, tk=D, out_dtype=F32, name="dil_in_dx")
            stack2b("b_w_in", j, mm_tn(sv["hn"], [dproj], t1=D, tn=1024, ts=ts, col_shards=False, name="dil_in_dw"))
        dk_m = dkvm[l].astype(MXU)
        dmem_parts.append(mm_nt([dk_m], big[p + "_w_mem_kv"][j], tm=N_MEM, tn=D, tk=2 * MEM_W, out_dtype=F32,
                                name="mem_kv_dx"))
        stack2b(p + "_w_mem_kv", j, mm_tn(mem_n, [dk_m], t1=D, tn=2 * MEM_W, ts=N_MEM, col_shards=False,
                                         name="mem_kv_dw"))
        dh, dg = norm_bwd(sv["h"], small[p + "_pre_mix_g"][j], dhn, dh1, F32, "pre_norm_bwd")
        stack2(p + "_pre_mix_g", j, dg)
        dh = after_hook(dh, "bwd_end", l, dh)

    _, gs["mem_norm_g"] = norm_bwd(mem, small["mem_norm_g"], add_n(dmem_parts, "dmem_sum"), None, F32,
                                   "mem_norm_bwd")
    return loss_parts, dh


ANY = pl.BlockSpec(memory_space=pl.ANY)
CHIP_FLIPS = (1, 2, 3)


def _coords():
    return lax.axis_index("x"), lax.axis_index("y"), lax.axis_index("c")


def _flip(x, y, m):
    return x ^ (m >> 1), y ^ (m & 1)


def _remote(src, dst, send_sems, recv_sems, k, device):
    return pltpu.make_async_remote_copy(src_ref=src, dst_ref=dst, send_sem=send_sems.at[k], recv_sem=recv_sems.at[k],
                                        device_id=device, device_id_type=MESH)


def gather_shards(shards, name):
    n = len(shards)

    def body(*refs):
        ins, outs = refs[:n], refs[n:2 * n]
        send_sems, recv_sems = refs[2 * n:]
        x, y, c = _coords()
        me = 2 * x + y
        sib = (x, y, 1 - c)
        halves, sends = [], []
        for i in range(n):
            hr = shards[i].shape[0] // 2
            mine = pl.ds(pl.multiple_of(c * hr, 8), hr)
            other = pl.ds(pl.multiple_of((1 - c) * hr, 8), hr)
            halves.append((mine, other))
            own = _remote(ins[i], outs[i].at[me], send_sems, recv_sems, 7 * i + 6, sib)
            own.start()
            sends.append(own)
            for j, m in enumerate(CHIP_FLIPS):
                cp = _remote(ins[i].at[mine], outs[i].at[me, mine], send_sems, recv_sems, 7 * i + j,
                             (*_flip(x, y, m), c))
                cp.start()
                sends.append(cp)
        for i in range(n):
            mine, _ = halves[i]
            for j, m in enumerate(CHIP_FLIPS):
                slot = outs[i].at[me ^ m, mine]
                _remote(slot, slot, send_sems, recv_sems, 7 * i + j, sib).wait_recv()
                fwd = _remote(slot, slot, send_sems, recv_sems, 7 * i + 3 + j, sib)
                fwd.start()
                sends.append(fwd)
        for i in range(n):
            _, other = halves[i]
            for j, m in enumerate(CHIP_FLIPS):
                slot = outs[i].at[me ^ m, other]
                _remote(slot, slot, send_sems, recv_sems, 7 * i + 3 + j, sib).wait_recv()
            _remote(ins[i], outs[i].at[me], send_sems, recv_sems, 7 * i + 6, sib).wait_recv()
        for cp in sends:
            cp.wait_send()

    return pl.pallas_call(
        body, in_specs=[ANY] * n, out_specs=[ANY] * n,
        out_shape=[SDS((N_CHIPS,) + sh.shape, sh.dtype) for sh in shards],
        scratch_shapes=[pltpu.SemaphoreType.DMA((7 * n,)), pltpu.SemaphoreType.DMA((7 * n,))],
        name=name,
    )(*shards)


def swap_halves(grads, name):
    n = len(grads)

    def body(*refs):
        ins, outs = refs[:n], refs[n:2 * n]
        send_sems, recv_sems = refs[2 * n:]
        x, y, c = _coords()
        cps = []
        for i in range(n):
            hr = grads[i].shape[1] // 2
            other = pl.ds(pl.multiple_of((1 - c) * hr, 8), hr)
            cp = _remote(ins[i].at[pl.ds(0, N_CHIPS), other], outs[i], send_sems, recv_sems, i, (x, y, 1 - c))
            cp.start()
            cps.append(cp)
        for cp in cps:
            cp.wait()

    return pl.pallas_call(
        body, in_specs=[ANY] * n, out_specs=[ANY] * n,
        out_shape=[SDS((N_CHIPS, g.shape[1] // 2, g.shape[2]), g.dtype) for g in grads],
        scratch_shapes=[pltpu.SemaphoreType.DMA((n,)), pltpu.SemaphoreType.DMA((n,))],
        name=name,
    )(*grads)


def _sum_rows_tile(rows, cols):
    for tr in (512, 256, 128, 64, 32, 16):
        if rows % tr == 0 and tr * cols * 4 <= 2 * 1024 * 1024:
            return tr
    raise ValueError((rows, cols))


def half_sum(g, got, c_arr, name):
    _, r, cols = g.shape
    hr = r // 2
    tr = _sum_rows_tile(hr, cols)

    def my_chip():
        return 2 * lax.axis_index("x") + lax.axis_index("y")

    def body(g_ref, got_ref, o_ref, own_ref):
        p = (g_ref[...] + got_ref[...]).astype(o_ref.dtype)
        o_ref[...] = p

        @pl.when(pl.program_id(1) == my_chip())
        def _():
            own_ref[...] = p

    out = SDS((N_CHIPS, hr, cols), jnp.bfloat16)
    return pl.pallas_call(
        body, grid=(hr // tr, N_CHIPS),
        in_specs=[BS((None, None, tr, cols), lambda i, s: (s, lax.axis_index("c"), i, 0)),
                  BS((None, tr, cols), lambda i, s: (s, i, 0))],
        out_specs=[BS((None, tr, cols), lambda i, s: (s, i, 0)),
                   BS((None, tr, cols), lambda i, s: (my_chip(), i, 0))],
        out_shape=[out, out], compiler_params=_cp("parallel", "arbitrary"), name=name,
    )(g.reshape(N_CHIPS, 2, hr, cols), got)


def exchange_parts(parts, name):
    n = len(parts)

    def body(*refs):
        ins, outs = refs[:n], refs[n:2 * n]
        send_sems, recv_sems, loc_sems = refs[2 * n:]
        x, y, c = _coords()
        me = 2 * x + y
        cps, locs = [], []
        for i in range(n):
            loc = pltpu.make_async_copy(ins[i].at[me], outs[i].at[me], loc_sems.at[i])
            loc.start()
            locs.append(loc)
            for j, m in enumerate(CHIP_FLIPS):
                cp = _remote(ins[i].at[me ^ m], outs[i].at[me], send_sems, recv_sems, 3 * i + j, (*_flip(x, y, m), c))
                cp.start()
                cps.append(cp)
        for cp in cps:
            cp.wait()
        for loc in locs:
            loc.wait()

    return pl.pallas_call(
        body, in_specs=[ANY] * n, out_specs=[ANY] * n,
        out_shape=[SDS(p.shape, p.dtype) for p in parts],
        scratch_shapes=[pltpu.SemaphoreType.DMA((3 * n,)), pltpu.SemaphoreType.DMA((3 * n,)),
                        pltpu.SemaphoreType.DMA((n,))],
        name=name,
    )(*parts)


def slot_sum(slots, c_arr, name):
    _, hr, cols = slots.shape
    tr = _sum_rows_tile(hr, cols)
    nblk = hr // tr

    def body(s_ref, o_ref):
        acc = s_ref[0].astype(F32)
        for p in range(1, N_CHIPS):
            acc = acc + s_ref[p].astype(F32)
        o_ref[...] = acc

    return pl.pallas_call(
        body, grid=(nblk,), in_specs=[BS((N_CHIPS, tr, cols), lambda i: (0, i, 0))],
        out_specs=BS((tr, cols), lambda i: (lax.axis_index("c") * nblk + i, 0)),
        out_shape=SDS((2 * hr, cols), F32), compiler_params=_cp("parallel"), name=name,
    )(slots)


def share_halves(bufs, name):
    n = len(bufs)

    def body(*refs):
        outs = refs[n:2 * n]
        send_sems, recv_sems = refs[2 * n:]
        x, y, c = _coords()
        cps = []
        for i in range(n):
            hr = bufs[i].shape[0] // 2
            mine = outs[i].at[pl.ds(pl.multiple_of(c * hr, 8), hr)]
            cp = _remote(mine, mine, send_sems, recv_sems, i, (x, y, 1 - c))
            cp.start()
            cps.append(cp)
        for cp in cps:
            cp.wait()

    return pl.pallas_call(
        body, in_specs=[ANY] * n, out_specs=[ANY] * n,
        out_shape=[SDS(b.shape, b.dtype) for b in bufs],
        input_output_aliases={i: i for i in range(n)},
        scratch_shapes=[pltpu.SemaphoreType.DMA((n,)), pltpu.SemaphoreType.DMA((n,))],
        name=name,
    )(*bufs)


HBM_SPEC = pl.BlockSpec(memory_space=pltpu.HBM)
SEM_SPEC = pl.BlockSpec(memory_space=pltpu.SEMAPHORE)
EFFECT = pltpu.SideEffectType.DATAFLOW_SIDE_EFFECTING


def split_start(name, bufs, plan, n_copies):
    nb = len(bufs)

    def body(*refs):
        send_sems, recv_sems = refs[nb], refs[nb + 1]
        for k, (src, dst, dev) in enumerate(plan(refs[:nb])):
            _remote(src, dst, send_sems, recv_sems, k, dev).start()
        refs[-1][...] = jnp.zeros_like(refs[-1])

    outs = pl.pallas_call(
        body, name=name,
        out_shape=(pltpu.SemaphoreType.DMA((n_copies,)), pltpu.SemaphoreType.DMA((n_copies,)),
                   *[pltpu.HBM(b.shape, b.dtype) for b in bufs], SDS((8, LANES), F32)),
        in_specs=[HBM_SPEC] * nb, out_specs=(SEM_SPEC, SEM_SPEC, *[HBM_SPEC] * nb, VM),
        input_output_aliases={i: 2 + i for i in range(nb)},
        compiler_params=pltpu.CompilerParams(has_side_effects=EFFECT),
    )(*[pltpu.with_memory_space_constraint(b, pltpu.HBM) for b in bufs])
    return outs[0], outs[1], list(outs[2:2 + nb]), outs[-1]


def split_wait(name, send_sems, recv_sems, bufs, after, plan):
    nb = len(bufs)

    def body(*refs):
        send_ref, recv_ref = refs[nb], refs[nb + 1]
        for k, (src, dst, dev) in enumerate(plan(refs[:nb])):
            cp = _remote(src, dst, send_ref, recv_ref, k, dev)
            cp.wait_send()
            cp.wait_recv()

    outs = pl.pallas_call(
        body, name=name, out_shape=[pltpu.HBM(b.shape, b.dtype) for b in bufs],
        in_specs=[HBM_SPEC] * nb + [SEM_SPEC, SEM_SPEC, ANY], out_specs=[HBM_SPEC] * nb,
        input_output_aliases={i: i for i in range(nb)},
        compiler_params=pltpu.CompilerParams(has_side_effects=EFFECT),
    )(*bufs, send_sems, recv_sems, after)
    return list(outs)


def tie(x, tokens, name):
    def body(*refs):
        pass

    return pl.pallas_call(
        body, name=name, out_shape=SDS(x.shape, x.dtype), in_specs=[ANY] * (1 + len(tokens)), out_specs=ANY,
        input_output_aliases={0: 0},
    )(x, *tokens)


def plan_gather_ici(n, rows):
    def plan(refs):
        x, y, c = _coords()
        me = 2 * x + y
        out = []
        for i in range(n):
            hr = rows[i] // 2
            mine = pl.ds(pl.multiple_of(c * hr, 8), hr)
            out.append((refs[i], refs[n + i].at[me], (x, y, 1 - c)))
            for m in CHIP_FLIPS:
                out.append((refs[i].at[mine], refs[n + i].at[me, mine], (*_flip(x, y, m), c)))
        return out
    return plan


def plan_gather_d2d(n, rows):
    def plan(refs):
        x, y, c = _coords()
        me = 2 * x + y
        out = []
        for i in range(n):
            hr = rows[i] // 2
            mine = pl.ds(pl.multiple_of(c * hr, 8), hr)
            for m in CHIP_FLIPS:
                slot = refs[i].at[me ^ m, mine]
                out.append((slot, slot, (x, y, 1 - c)))
        return out
    return plan


def plan_swap(n, rows):
    def plan(refs):
        x, y, c = _coords()
        out = []
        for i in range(n):
            hr = rows[i] // 2
            other = pl.ds(pl.multiple_of((1 - c) * hr, 8), hr)
            out.append((refs[i].at[pl.ds(0, N_CHIPS), other], refs[n + i], (x, y, 1 - c)))
        return out
    return plan


def plan_exchange(n):
    def plan(refs):
        x, y, c = _coords()
        me = 2 * x + y
        out = []
        for i in range(n):
            for m in CHIP_FLIPS:
                out.append((refs[i].at[me ^ m], refs[n + i].at[me], (*_flip(x, y, m), c)))
        return out
    return plan


def plan_share(n, rows):
    def plan(refs):
        x, y, c = _coords()
        out = []
        for i in range(n):
            hr = rows[i] // 2
            mine = refs[i].at[pl.ds(pl.multiple_of(c * hr, 8), hr)]
            out.append((mine, mine, (x, y, 1 - c)))
        return out
    return plan


def reduce_scatter(grads, c_arr, tag):
    got = swap_halves(grads, "rs_swap_" + tag)
    parts = [half_sum(g, r, c_arr, "rs_half_sum") for g, r in zip(grads, got)]
    slots = exchange_parts(parts, "rs_exchange_" + tag)
    return share_halves([slot_sum(s, c_arr, "rs_slot_sum") for s in slots], "rs_share_" + tag)


VM = pl.BlockSpec(memory_space=pltpu.VMEM)


def small_gather(v, name):
    def body(v_ref, out_ref, send_sems, recv_sems):
        x, y, c = _coords()
        me = 2 * x + y
        out_ref[me] = v_ref[...]
        cps = []
        for j, m in enumerate(CHIP_FLIPS):
            cp = _remote(v_ref, out_ref.at[me], send_sems, recv_sems, j, (*_flip(x, y, m), c))
            cp.start()
            cps.append(cp)
        for cp in cps:
            cp.wait()

    return pl.pallas_call(
        body, in_specs=[VM], out_specs=VM, out_shape=SDS((N_CHIPS,) + v.shape, v.dtype),
        scratch_shapes=[pltpu.SemaphoreType.DMA((3,)), pltpu.SemaphoreType.DMA((3,))],
        compiler_params=pltpu.CompilerParams(vmem_limit_bytes=VMEM_LIMIT_BYTES), name=name,
    )(v)


def small_allreduce(v, name):
    def body(v_ref, out_ref, sib_buf, slots, send_sems, recv_sems):
        x, y, c = _coords()
        me = 2 * x + y
        swap = _remote(v_ref, sib_buf, send_sems, recv_sems, 0, (x, y, 1 - c))
        swap.start()
        swap.wait()
        slots[me] = v_ref[...] + sib_buf[...]
        cps = []
        for j, m in enumerate(CHIP_FLIPS):
            cp = _remote(slots.at[me], slots.at[me], send_sems, recv_sems, 1 + j, (*_flip(x, y, m), c))
            cp.start()
            cps.append(cp)
        for cp in cps:
            cp.wait()
        out_ref[...] = (slots[0] + slots[1]) + (slots[2] + slots[3])

    return pl.pallas_call(
        body, in_specs=[VM], out_specs=VM, out_shape=SDS(v.shape, v.dtype),
        scratch_shapes=[pltpu.VMEM(v.shape, v.dtype), pltpu.VMEM((N_CHIPS,) + v.shape, v.dtype),
                        pltpu.SemaphoreType.DMA((4,)), pltpu.SemaphoreType.DMA((4,))],
        compiler_params=pltpu.CompilerParams(vmem_limit_bytes=VMEM_LIMIT_BYTES), name=name,
    )(v)


def adamw(w, g_list, m, v, name):
    nl, rows, cols = w.shape
    tr = _sum_rows_tile(rows, cols) if rows % 16 == 0 else rows
    bc1 = 1.0 - ADAM_B1 ** ADAM_STEP
    bc2 = 1.0 - ADAM_B2 ** ADAM_STEP

    def body(*refs):
        w_ref, m_ref, v_ref = refs[:3]
        g_refs = refs[3:3 + nl]
        go_ref, d_ref, mo_ref, vo_ref = refs[3 + nl:]
        layer = pl.program_id(0)
        for l in range(nl):
            @pl.when(layer == l)
            def _(l=l):
                g = g_refs[l][...]
                m_new = ADAM_B1 * m_ref[...] + (1.0 - ADAM_B1) * g
                v_new = ADAM_B2 * v_ref[...] + (1.0 - ADAM_B2) * (g * g)
                m_hat = m_new / bc1
                v_hat = v_new / bc2
                go_ref[...] = g
                d_ref[...] = -ADAM_LR * (m_hat / (jnp.sqrt(v_hat) + ADAM_EPS) + ADAM_WD * w_ref[...])
                mo_ref[...] = m_new
                vo_ref[...] = v_new

    stk = BS((None, tr, cols), lambda l, i: (l, i, 0))
    flat = BS((tr, cols), lambda l, i: (i, 0))
    out = SDS((nl, rows, cols), F32)
    return pl.pallas_call(
        body, grid=(nl, rows // tr), in_specs=[stk] * 3 + [flat] * nl, out_specs=[stk] * 4,
        out_shape=[out] * 4, compiler_params=_cp("parallel", "parallel"), name=name,
    )(w, m, v, *g_list)


WEIGHTS = ["mem_norm_g", "a_pre_mix_g", "a_post_mix_g", "a_pre_ffn_g", "a_post_ffn_g", "a_w_in", "a_conv_w",
           "a_conv_b", "a_gate_a_w", "a_gate_a_b", "a_gate_x_w", "a_gate_x_b", "a_lambda", "a_w_mem_kv", "a_w_out",
           "a_w_ffn_in", "a_w_ffn_out", "kv_norm_g", "w_kv_shared", "b_pre_mix_g", "b_post_mix_g", "b_pre_ffn_g",
           "b_post_ffn_g", "b_w_in", "b_w_mem_kv", "b_w_out", "b_w_ffn_in", "b_w_ffn_out"]
BIG = {"a_w_in": True, "a_w_mem_kv": False, "a_w_out": False, "a_w_ffn_in": True, "a_w_ffn_out": False,
       "w_kv_shared": True, "b_w_in": False, "b_w_mem_kv": False, "b_w_out": False, "b_w_ffn_in": True,
       "b_w_ffn_out": False}
SHARDED_SMALL = ["a_pre_mix_g", "a_post_mix_g", "a_pre_ffn_g", "a_post_ffn_g", "a_conv_w", "a_conv_b", "a_gate_a_b",
                 "a_gate_x_b", "a_lambda"]
REPL_SMALL = ["mem_norm_g", "kv_norm_g", "b_pre_mix_g", "b_post_mix_g", "b_pre_ffn_g", "b_post_ffn_g", "a_gate_a_w",
              "a_gate_x_w"]
LANES = 128


def _pack(arrs, row_multiple=8):
    flat = jnp.concatenate([a.reshape(-1) for a in arrs])
    pad = -flat.shape[0] % (LANES * row_multiple)
    if pad:
        flat = jnp.concatenate([flat, jnp.zeros((pad,), flat.dtype)])
    return flat.reshape(-1, LANES)


def _unpack(packed, shapes):
    flat = packed.reshape(-1)
    out, pos = [], 0
    for sh in shapes:
        size = math.prod(sh)
        out.append(flat[pos:pos + size].reshape(sh))
        pos += size
    return out


def kernel(x, mem, mem_norm_g, a_pre_mix_g, a_post_mix_g, a_pre_ffn_g, a_post_ffn_g, a_w_in, a_conv_w, a_conv_b,
           a_gate_a_w, a_gate_a_b, a_gate_x_w, a_gate_x_b, a_lambda, a_w_mem_kv, a_w_out, a_w_ffn_in, a_w_ffn_out,
           kv_norm_g, w_kv_shared, b_pre_mix_g, b_post_mix_g, b_pre_ffn_g, b_post_ffn_g, b_w_in, b_w_mem_kv, b_w_out,
           b_w_ffn_in, b_w_ffn_out, loss_target, m_mem_norm_g, m_a_pre_mix_g, m_a_post_mix_g, m_a_pre_ffn_g,
           m_a_post_ffn_g, m_a_w_in, m_a_conv_w, m_a_conv_b, m_a_gate_a_w, m_a_gate_a_b, m_a_gate_x_w, m_a_gate_x_b,
           m_a_lambda, m_a_w_mem_kv, m_a_w_out, m_a_w_ffn_in, m_a_w_ffn_out, m_kv_norm_g, m_w_kv_shared, m_b_pre_mix_g,
           m_b_post_mix_g, m_b_pre_ffn_g, m_b_post_ffn_g, m_b_w_in, m_b_w_mem_kv, m_b_w_out, m_b_w_ffn_in, m_b_w_ffn_out,
           v_mem_norm_g, v_a_pre_mix_g, v_a_post_mix_g, v_a_pre_ffn_g, v_a_post_ffn_g, v_a_w_in, v_a_conv_w, v_a_conv_b,
           v_a_gate_a_w, v_a_gate_a_b, v_a_gate_x_w, v_a_gate_x_b, v_a_lambda, v_a_w_mem_kv, v_a_w_out, v_a_w_ffn_in,
           v_a_w_ffn_out, v_kv_norm_g, v_w_kv_shared, v_b_pre_mix_g, v_b_post_mix_g, v_b_pre_ffn_g, v_b_post_ffn_g,
           v_b_w_in, v_b_w_mem_kv, v_b_w_out, v_b_w_ffn_in, v_b_w_ffn_out):
    a = dict(locals())
    xi, yi, ci = _coords()
    chip = 2 * xi + yi
    c_arr = jnp.stack([ci, chip]).astype(jnp.int32)

    got = small_gather(_pack([a[n] for n in SHARDED_SMALL]), "small_gather")
    per_chip = [_unpack(got[s], [a[n].shape for n in SHARDED_SMALL]) for s in range(N_CHIPS)]
    small = {n: jnp.concatenate([per_chip[s][k] for s in range(N_CHIPS)], axis=-1)
             for k, n in enumerate(SHARDED_SMALL)}
    small.update({n: a[n] for n in REPL_SMALL})

    groups = []
    for l in range(4):
        p, j = ("a", l) if l < 2 else ("b", l - 2)
        groups.append([(p + "_" + n, j) for n in ("w_in", "w_mem_kv", "w_out")])
        groups.append([(p + "_" + n, j) for n in ("w_ffn_in", "w_ffn_out")])
    groups[3].append(("w_kv_shared", None))
    big = {n: [None, None] for n in BIG if n != "w_kv_shared"}
    gs, gb = {}, {}
    reduced = {n: [None, None] for n in BIG if n != "w_kv_shared"}

    def put(store, n, j, val):
        if j is None:
            store[n] = val
        else:
            store[n][j] = val

    class Exchange:
        def __init__(self):
            self.state = {}

        def gather_ici(self, g):
            shards = [(a[n] if j is None else a[n][j]).astype(MXU) for n, j in groups[g]]
            rows = [sh.shape[0] for sh in shards]
            lands = [lax.empty((N_CHIPS,) + sh.shape, sh.dtype) for sh in shards]
            plan = plan_gather_ici(len(shards), rows)
            ss, rs, bufs, tok = split_start("gather_ici_%d" % g, shards + lands, plan, 4 * len(shards))
            self.state["g", g] = (ss, rs, bufs, plan, rows)
            return tok

        def gather_d2d(self, g, after):
            ss, rs, bufs, plan, rows = self.state.pop(("g", g))
            n = len(rows)
            outs = split_wait("gather_ici_wait_%d" % g, ss, rs, bufs, after, plan)[n:]
            plan = plan_gather_d2d(n, rows)
            ss, rs, bufs, tok = split_start("gather_d2d_%d" % g, outs, plan, 3 * n)
            self.state["g", g] = (ss, rs, bufs, plan)
            return tok

        def gather_done(self, g, after):
            ss, rs, bufs, plan = self.state.pop(("g", g))
            outs = split_wait("gather_d2d_wait_%d" % g, ss, rs, bufs, after, plan)
            for (n, j), w in zip(groups[g], outs):
                put(big, n, j, w if BIG[n] else w.reshape(-1, w.shape[-1]))

        def rs_swap(self, g):
            grads = []
            for n, j in groups[g]:
                gr = gb[n] if j is None else gb[n][j]
                grads.append(gr if BIG[n] else gr.reshape(N_CHIPS, gr.shape[0] // N_CHIPS, gr.shape[1]))
            rows = [gr.shape[1] for gr in grads]
            lands = [lax.empty((N_CHIPS, gr.shape[1] // 2, gr.shape[2]), F32) for gr in grads]
            plan = plan_swap(len(grads), rows)
            ss, rs, bufs, tok = split_start("rs_swap_%d" % g, grads + lands, plan, len(grads))
            self.state["r", g] = (ss, rs, bufs, plan, rows)
            return tok

        def rs_exchange(self, g, after):
            ss, rs, bufs, plan, rows = self.state.pop(("r", g))
            n = len(rows)
            bufs = split_wait("rs_swap_wait_%d" % g, ss, rs, bufs, after, plan)
            sums = [half_sum(gr, got, c_arr, "rs_half_sum") for gr, got in zip(bufs[:n], bufs[n:])]
            plan = plan_exchange(n)
            ss, rs, bufs, tok = split_start("rs_exchange_%d" % g, [p for p, _ in sums] + [s for _, s in sums], plan,
                                            3 * n)
            self.state["r", g] = (ss, rs, bufs, plan, rows)
            return tok

        def rs_share(self, g, after):
            ss, rs, bufs, plan, rows = self.state.pop(("r", g))
            n = len(rows)
            slots = split_wait("rs_exchange_wait_%d" % g, ss, rs, bufs, after, plan)[n:]
            fulls = [slot_sum(s, c_arr, "rs_slot_sum") for s in slots]
            plan = plan_share(n, rows)
            ss, rs, bufs, tok = split_start("rs_share_%d" % g, fulls, plan, n)
            self.state["r", g] = (ss, rs, bufs, plan)
            return tok

        def rs_done(self, g, after):
            ss, rs, bufs, plan = self.state.pop(("r", g))
            outs = split_wait("rs_share_wait_%d" % g, ss, rs, bufs, after, plan)
            for (n, j), r in zip(groups[g], outs):
                put(reduced, n, j, r)

        def hook(self, where, l, after):
            mix, ffn = 2 * l, 2 * l + 1
            toks = []
            if where == "fwd_begin":
                if l == 0:
                    tok = self.gather_ici(mix)
                    tok = self.gather_d2d(mix, tok)
                    self.gather_done(mix, tok)
                toks.append(self.gather_ici(ffn))
            elif where == "fwd_q1":
                toks.append(self.gather_d2d(ffn, after))
            elif where == "fwd_mid":
                self.gather_done(ffn, after)
                if l < 3:
                    toks.append(self.gather_ici(mix + 2))
            elif where == "fwd_q3":
                if l < 3:
                    toks.append(self.gather_d2d(mix + 2, after))
            elif where == "fwd_end":
                if l < 3:
                    self.gather_done(mix + 2, after)
            elif where == "bwd_begin":
                if l < 3:
                    self.rs_done(ffn + 2, after)
                    toks.append(self.rs_exchange(mix + 2, after))
            elif where == "bwd_mid1":
                if l < 3:
                    toks.append(self.rs_share(mix + 2, after))
            elif where == "bwd_mid2":
                if l < 3:
                    self.rs_done(mix + 2, after)
                toks.append(self.rs_swap(ffn))
            elif where == "bwd_m1":
                toks.append(self.rs_exchange(ffn, after))
            elif where == "bwd_end":
                toks.append(self.rs_share(ffn, after))
                toks.append(self.rs_swap(mix))
                if l == 0:
                    self.rs_done(ffn, toks[0])
                    tok = self.rs_exchange(mix, toks[1])
                    tok = self.rs_share(mix, tok)
                    self.rs_done(mix, tok)
                    toks = []
            else:
                raise ValueError(where)
            return toks

    loss_parts, dx = _fwd_bwd(x[0], mem[0], loss_target[0], small, big, gs, gb, Exchange())
    loss = lax.psum(jnp.sum(loss_parts) * (0.5 / D), ("x", "y", "c"))

    res = {}
    reduced["w_kv_shared"] = [reduced["w_kv_shared"]]
    for n in BIG:
        shape = a[n].shape
        rows, cols = shape[-2], shape[-1]
        stk = (-1, rows, cols)
        outs = adamw(a[n].reshape(stk), reduced[n], a["m_" + n].reshape(stk), a["v_" + n].reshape(stk), "adamw")
        res[n] = [o.reshape(shape) for o in outs]

    def full(n):
        g = gs[n]
        return jnp.stack(g) if isinstance(g, list) else g

    order = SHARDED_SMALL + REPL_SMALL
    full_shapes = [full(n).shape for n in order]
    summed = _unpack(small_allreduce(_pack([full(n) for n in order]), "small_allreduce"), full_shapes)
    mine = []
    for n, g in zip(order, summed):
        if n in SHARDED_SMALL:
            width = a[n].shape[-1]
            g = lax.dynamic_slice_in_dim(g, chip * width, width, axis=g.ndim - 1)
        mine.append(g.reshape(a[n].shape))
    shapes = [a[n].shape for n in order]
    rm = 512
    outs = adamw(_pack([a[n] for n in order], rm)[None], [_pack(mine, rm)],
                 _pack([a["m_" + n] for n in order], rm)[None], _pack([a["v_" + n] for n in order], rm)[None],
                 "adamw_small")
    unpacked = [_unpack(o[0], shapes) for o in outs]
    for k, n in enumerate(order):
        res[n] = [u[k] for u in unpacked]

    return (loss, dx[None], *[res[n][0] for n in WEIGHTS], *[res[n][1] for n in WEIGHTS],
            *[res[n][2] for n in WEIGHTS], *[res[n][3] for n in WEIGHTS])
```

```python
import functools
import math

import jax
import jax.numpy as jnp
from jax import lax
from jax.experimental import pallas as pl
from jax.experimental.pallas import tpu as pltpu

D = 2048
HD = 128
MEM_W = 512
MEM_HEADS = 4
MIX_W = D - MEM_W
N_BLK = MIX_W // HD
D_FF = 5632
N_MEM = 256
RMS_EPS = 1e-6
NEG_INF = -1e30
LRU_C = 8.0
DIL_GROUPS = ((128, 1), (512, 4), (2048, 16))
Q_BLOCK = 128
SCALE = HD ** -0.5
N_CHIPS = 4
MXU_COLS = 256
ACC_CHUNK = 2 * MXU_COLS

ADAM_LR = 0.001
ADAM_B1 = 0.9
ADAM_B2 = 0.999
ADAM_EPS = 1e-08
ADAM_WD = 0.01
ADAM_STEP = 10

MXU = jnp.bfloat16
F32 = jnp.float32
VMEM_LIMIT_BYTES = 56 * 1024 * 1024

BS = pl.BlockSpec
SDS = jax.ShapeDtypeStruct
MESH = pl.DeviceIdType.MESH


def _cp(*sem):
    return pltpu.CompilerParams(dimension_semantics=sem or None, vmem_limit_bytes=VMEM_LIMIT_BYTES)


def _dot(a, b, dn=((1,), (0,))):
    return lax.dot_general(a, b, (dn, ((), ())), preferred_element_type=F32)


def _div(i, n):
    return lax.div(i, jnp.int32(n))


def _rem(i, n):
    return lax.rem(i, jnp.int32(n))


NN = ((1,), (0,))
NT = ((1,), (1,))
TN = ((0,), (0,))


def _sigmoid(z):
    return 1.0 / (1.0 + jnp.exp(-z))


def _log1p_pos(u):
    return jnp.where(u < 1e-2, u * (1.0 - u * (0.5 - u * (1.0 / 3.0))), jnp.log(1.0 + u))


def _neg_expm1(z):
    return jnp.where(z > -1e-2, -z * (1.0 + z * (0.5 + z * (1.0 / 6.0))), 1.0 - jnp.exp(z))


def _softplus(z):
    return jnp.maximum(z, 0.0) + _log1p_pos(jnp.exp(-jnp.abs(z)))


_GELU_C = math.sqrt(2.0 / math.pi)


def _gelu_and_grad(x):
    x2 = x * x
    t = jnp.tanh(_GELU_C * (x + 0.044715 * x * x2))
    g = 0.5 * x * (1.0 + t)
    dg = 0.5 * (1.0 + t) + 0.5 * x * (1.0 - t * t) * _GELU_C * (1.0 + 3.0 * 0.044715 * x2)
    return g, dg


def _row_tile(rows):
    return min(256, rows)


def norm_cast(x, g, name):
    rows = x.shape[0]
    tr = _row_tile(rows)

    def body(x_ref, g_ref, o_ref):
        xv = x_ref[...]
        r = lax.rsqrt(jnp.mean(xv * xv, axis=-1, keepdims=True) + RMS_EPS)
        o_ref[...] = (xv * r * g_ref[...]).astype(o_ref.dtype)

    return pl.pallas_call(
        body, grid=(rows // tr,),
        in_specs=[BS((tr, D), lambda i: (i, 0)), BS((1, D), lambda i: (0, 0))],
        out_specs=BS((tr, D), lambda i: (i, 0)),
        out_shape=SDS((rows, D), MXU), compiler_params=_cp("parallel"), name=name,
    )(x, g.reshape(1, D))


def resid_norm(h, y, g, name):
    rows = h.shape[0]
    tr = _row_tile(rows)

    def body(h_ref, y_ref, g_ref, o_ref):
        yv = y_ref[...]
        r = lax.rsqrt(jnp.mean(yv * yv, axis=-1, keepdims=True) + RMS_EPS)
        o_ref[...] = h_ref[...] + yv * r * g_ref[...]

    return pl.pallas_call(
        body, grid=(rows // tr,),
        in_specs=[BS((tr, D), lambda i: (i, 0)), BS((tr, D), lambda i: (i, 0)), BS((1, D), lambda i: (0, 0))],
        out_specs=BS((tr, D), lambda i: (i, 0)),
        out_shape=SDS((rows, D), F32), compiler_params=_cp("parallel"), name=name,
    )(h, y, g.reshape(1, D))


def resid_norm_next(h, y, g, g_next, name):
    rows = h.shape[0]
    tr = _row_tile(rows)

    def body(h_ref, y_ref, g_ref, gn_ref, o_ref, n_ref):
        yv = y_ref[...]
        r = lax.rsqrt(jnp.mean(yv * yv, axis=-1, keepdims=True) + RMS_EPS)
        hv = h_ref[...] + yv * r * g_ref[...]
        o_ref[...] = hv
        r2 = lax.rsqrt(jnp.mean(hv * hv, axis=-1, keepdims=True) + RMS_EPS)
        n_ref[...] = (hv * r2 * gn_ref[...]).astype(n_ref.dtype)

    row = BS((tr, D), lambda i: (i, 0))
    vec = BS((1, D), lambda i: (0, 0))
    return pl.pallas_call(
        body, grid=(rows // tr,), in_specs=[row, row, vec, vec], out_specs=[row, row],
        out_shape=[SDS((rows, D), F32), SDS((rows, D), MXU)], compiler_params=_cp("parallel"), name=name,
    )(h, y, g.reshape(1, D), g_next.reshape(1, D))


def norm_bwd(x, g, dy, res, out_dtype, name):
    rows = x.shape[0]
    tr = _row_tile(rows)
    has_res = res is not None

    def body(*refs):
        if has_res:
            x_ref, g_ref, dy_ref, res_ref, dx_ref, dg_ref = refs
        else:
            x_ref, g_ref, dy_ref, dx_ref, dg_ref = refs
        xv = x_ref[...]
        dyv = dy_ref[...].astype(F32)
        r = lax.rsqrt(jnp.mean(xv * xv, axis=-1, keepdims=True) + RMS_EPS)
        xhat = xv * r
        dxhat = dyv * g_ref[...]
        dx = r * (dxhat - xhat * jnp.mean(dxhat * xhat, axis=-1, keepdims=True))
        if has_res:
            dx = dx + res_ref[...]
        dx_ref[...] = dx.astype(dx_ref.dtype)

        @pl.when(pl.program_id(0) == 0)
        def _():
            dg_ref[...] = jnp.zeros_like(dg_ref)

        dg_ref[...] += jnp.sum(dyv * xhat, axis=0, keepdims=True)

    row = BS((tr, D), lambda i: (i, 0))
    vec = BS((1, D), lambda i: (0, 0))
    ins = [x, g.reshape(1, D), dy] + ([res] if has_res else [])
    dx, dg = pl.pallas_call(
        body, grid=(rows // tr,),
        in_specs=[row, vec, row] + ([row] if has_res else []),
        out_specs=[row, vec],
        out_shape=[SDS((rows, D), out_dtype), SDS((1, D), F32)],
        compiler_params=_cp("arbitrary"), name=name,
    )(*ins)
    return dx, dg.reshape(D)


def loss_head(y, target, name):
    rows = y.shape[0]
    tr = _row_tile(rows)

    def body(y_ref, t_ref, dy_ref, acc_ref):
        err = y_ref[...] - t_ref[...]
        dy_ref[...] = err * (1.0 / D)

        @pl.when(pl.program_id(0) == 0)
        def _():
            acc_ref[...] = jnp.zeros_like(acc_ref)

        acc_ref[...] += jnp.sum(err * err, axis=0, keepdims=True)

    row = BS((tr, D), lambda i: (i, 0))
    dy, acc = pl.pallas_call(
        body, grid=(rows // tr,), in_specs=[row, row],
        out_specs=[row, BS((1, D), lambda i: (0, 0))],
        out_shape=[SDS((rows, D), F32), SDS((1, D), F32)],
        compiler_params=_cp("arbitrary"), name=name,
    )(y, target)
    return acc, dy


def _mm_call(ins, in_specs, pick, dn, grid, o_spec, out_sds, name):
    gk = grid[2]
    n_in = len(ins)

    def body(*refs):
        o_ref = refs[n_in]
        k = pl.program_id(2)

        def step(a_ref, b_ref):
            if gk == 1:
                o_ref[...] = _dot(a_ref[...], b_ref[...], dn).astype(o_ref.dtype)
                return
            acc = o_ref if out_sds.dtype == F32 else refs[n_in + 1]
            width = acc.shape[-1]
            if dn == TN or width <= ACC_CHUNK:
                chunks = [(0, width)]
            else:
                chunks = [(c0, min(c0 + ACC_CHUNK, width)) for c0 in range(0, width, ACC_CHUNK)]

            def sweep(first):
                a = a_ref[...]
                pending = None
                for c0, c1 in chunks:
                    p = _dot(a, b_ref[c0:c1, :] if dn == NT else b_ref[:, c0:c1], dn)
                    if pending is not None:
                        put(first, *pending)
                    pending = (c0, c1, p)
                put(first, *pending)

            def put(first, c0, c1, p):
                if first:
                    acc[:, c0:c1] = p
                else:
                    acc[:, c0:c1] += p

            @pl.when(k == 0)
            def _():
                sweep(True)

            @pl.when(k > 0)
            def _():
                sweep(False)

            if acc is not o_ref:
                @pl.when(k == gk - 1)
                def _():
                    o_ref[...] = acc[...].astype(o_ref.dtype)

        pick(refs[:n_in], k, step)

    scratch = []
    if gk > 1 and out_sds.dtype != F32:
        scratch = [pltpu.VMEM(o_spec.block_shape[-2:], F32)]
    return pl.pallas_call(
        body, grid=grid, in_specs=in_specs, out_specs=o_spec, out_shape=out_sds,
        scratch_shapes=scratch, compiler_params=_cp("parallel", "parallel", "arbitrary"), name=name,
    )(*ins)


def _pick2(refs, k, step):
    step(refs[0], refs[1])


def mm_nn(a, w, *, tm, tn, tk, out_dtype, name):
    m, kdim = a.shape
    if w.ndim == 3:
        c = w.shape[2]
        n = N_CHIPS * c
        per = c // tn
        b_spec = BS((None, tk, tn), lambda i, j, k: (_div(j, per), k, _rem(j, per)))
    else:
        n = w.shape[1]
        b_spec = BS((tk, tn), lambda i, j, k: (k, j))
    grid = (m // tm, n // tn, kdim // tk)
    return _mm_call([a, w], [BS((tm, tk), lambda i, j, k: (i, k)), b_spec], _pick2, NN, grid,
                    BS((tm, tn), lambda i, j, k: (i, j)), SDS((m, n), out_dtype), name)


def mm_nt(a_list, w, *, tm, tn, tk, out_dtype, name):
    m = a_list[0].shape[0]
    ka = a_list[0].shape[1]
    n_a = len(a_list)
    kdim = ka * n_a
    if w.ndim == 3:
        c = w.shape[2]
        n = w.shape[1]
        per = c // tk
        b_spec = BS((None, tn, tk), lambda i, j, k: (_div(k, per), j, _rem(k, per)))
    else:
        n = w.shape[0]
        b_spec = BS((tn, tk), lambda i, j, k: (j, k))
    gk = kdim // tk
    half = gk // n_a
    grid = (m // tm, n // tn, gk)
    if n_a == 1:
        a_specs = [BS((tm, tk), lambda i, j, k: (i, k))]
        pick = lambda refs, k, step: step(refs[0], refs[1])
    else:
        a_specs = [BS((tm, tk), lambda i, j, k: (i, jnp.minimum(k, half - 1))),
                   BS((tm, tk), lambda i, j, k: (i, jnp.maximum(k - half, 0)))]

        def pick(refs, k, step):
            @pl.when(k < half)
            def _():
                step(refs[0], refs[2])

            @pl.when(k >= half)
            def _():
                step(refs[1], refs[2])

    return _mm_call(list(a_list) + [w], a_specs + [b_spec], pick, NT, grid,
                    BS((tm, tn), lambda i, j, k: (i, j)), SDS((m, n), out_dtype), name)


def mm_tn(a, b_list, *, t1, tn, ts, col_shards, name):
    s, k1 = a.shape
    nb = b_list[0].shape[1]
    n_b = len(b_list)
    n = nb * n_b
    gn = n // tn
    half = gn // n_b
    grid = (k1 // t1, gn, s // ts)
    if col_shards:
        c = n // N_CHIPS
        per = c // tn
        o_spec = BS((None, t1, tn), lambda i, j, k: (_div(j, per), i, _rem(j, per)))
        out_sds = SDS((N_CHIPS, k1, c), F32)
    else:
        o_spec = BS((t1, tn), lambda i, j, k: (i, j))
        out_sds = SDS((k1, n), F32)
    a_spec = BS((ts, t1), lambda i, j, k: (k, i))
    if n_b == 1:
        b_specs = [BS((ts, tn), lambda i, j, k: (k, j))]
        pick = lambda refs, k, step: step(refs[0], refs[1])
    else:
        b_specs = [BS((ts, tn), lambda i, j, k: (k, jnp.minimum(j, half - 1))),
                   BS((ts, tn), lambda i, j, k: (k, jnp.maximum(j - half, 0)))]

        def pick(refs, k, step):
            j = pl.program_id(1)

            @pl.when(j < half)
            def _():
                step(refs[0], refs[1])

            @pl.when(j >= half)
            def _():
                step(refs[0], refs[2])

    return _mm_call([a] + list(b_list), [a_spec] + b_specs, pick, TN, grid, o_spec, out_sds, name)


def ffn_in_fwd(hn, w, name):
    s = hn.shape[0]
    tm = min(512, s)
    tn = D_FF // 4

    def tail(dag_ref, dau_ref, act_ref, c0, c1, g, u):
        sg = _sigmoid(g)
        silu = g * sg
        dag_ref[:, c0:c1] = u * sg * (1.0 + g * (1.0 - sg))
        dau_ref[:, c0:c1] = silu
        act_ref[:, c0:c1] = (silu * u).astype(act_ref.dtype)

    def body(a_ref, wg_ref, wu_ref, dag_ref, dau_ref, act_ref):
        a = a_ref[...]
        pending = None
        for c0 in range(0, tn, ACC_CHUNK):
            c1 = min(c0 + ACC_CHUNK, tn)
            g = _dot(a, wg_ref[:, c0:c1])
            u = _dot(a, wu_ref[:, c0:c1])
            if pending is not None:
                tail(dag_ref, dau_ref, act_ref, *pending)
            pending = (c0, c1, g, u)
        tail(dag_ref, dau_ref, act_ref, *pending)

    tile = BS((tm, tn), lambda j, i: (i, j))
    return pl.pallas_call(
        body, grid=(4, s // tm),
        in_specs=[BS((tm, D), lambda j, i: (i, 0)),
                  BS((None, D, tn), lambda j, i: (_div(j, 2), 0, _rem(j, 2))),
                  BS((None, D, tn), lambda j, i: (2 + _div(j, 2), 0, _rem(j, 2)))],
        out_specs=[tile, tile, tile],
        out_shape=[SDS((s, D_FF), F32), SDS((s, D_FF), F32), SDS((s, D_FF), MXU)],
        compiler_params=_cp("parallel", "parallel"), name=name,
    )(hn, w, w)


def ffn_act_bwd(dy, w_out, dag, dau, name):
    s = dy.shape[0]
    tm = min(512, s)
    tn = D_FF // 4

    def body(dy_ref, w_ref, dag_ref, dau_ref, dg_ref, du_ref):
        def tail(c0, c1, dact):
            dg_ref[:, c0:c1] = (dact * dag_ref[:, c0:c1]).astype(dg_ref.dtype)
            du_ref[:, c0:c1] = (dact * dau_ref[:, c0:c1]).astype(du_ref.dtype)

        dy = dy_ref[...]
        pending = None
        for c0 in range(0, tn, ACC_CHUNK):
            c1 = min(c0 + ACC_CHUNK, tn)
            dact = _dot(dy, w_ref[c0:c1, :], NT)
            if pending is not None:
                tail(*pending)
            pending = (c0, c1, dact)
        tail(*pending)

    tile = BS((tm, tn), lambda j, i: (i, j))
    return pl.pallas_call(
        body, grid=(4, s // tm),
        in_specs=[BS((tm, D), lambda j, i: (i, 0)), BS((tn, D), lambda j, i: (j, 0)), tile, tile],
        out_specs=[tile, tile],
        out_shape=[SDS((s, D_FF), MXU), SDS((s, D_FF), MXU)],
        compiler_params=_cp("parallel", "parallel"), name=name,
    )(dy, w_out, dag, dau)


LRU_T = 256
HALO = 8


def _shift_down(x, k, fill):
    rows = x.shape[0]
    idx = lax.broadcasted_iota(jnp.int32, x.shape, 0)
    return jnp.where(idx < k, fill, pltpu.roll(x, k, 0))


def _shift_up(x, k, fill):
    rows = x.shape[0]
    idx = lax.broadcasted_iota(jnp.int32, x.shape, 0)
    return jnp.where(idx >= rows - k, fill, pltpu.roll(x, rows - k, 0))


def _scan_block(a, b, carry, reverse):
    rows, cols = a.shape
    sub = 8
    in_group = lax.broadcasted_iota(jnp.int32, a.shape, 0) % sub
    for sh in (1, 2, 4):
        if reverse:
            a_s, b_s, ok = pltpu.roll(a, rows - sh, 0), pltpu.roll(b, rows - sh, 0), in_group < sub - sh
        else:
            a_s, b_s, ok = pltpu.roll(a, sh, 0), pltpu.roll(b, sh, 0), in_group >= sh
        b = jnp.where(ok, a * b_s + b, b)
        a = jnp.where(ok, a * a_s, a)
    groups = list(range(rows // sub))
    edge = 0 if reverse else sub - 1
    carry_in = {}
    for v in (reversed(groups) if reverse else groups):
        carry_in[v] = carry
        row = sub * v + edge
        carry = b[row:row + 1, :] + a[row:row + 1, :] * carry
    cin = jnp.concatenate([jnp.broadcast_to(carry_in[v], (sub, cols)) for v in groups], axis=0)
    return b + a * cin


def _conv_taps(xcat):
    rows = xcat.shape[0]
    taps = []
    for k in range(4):
        off = HALO - 3 + k
        taps.append(xcat[off:off + LRU_T] if off == HALO else pltpu.roll(xcat, rows - off, 0)[:LRU_T])
    return taps


def _gates(xc, wa_ref, ba, wx_ref, bx, lam, za_ref, zx_ref):
    xm = xc.astype(MXU)
    for n in range(N_BLK):
        sl = slice(n * HD, (n + 1) * HD)
        za_ref[:, sl] = _dot(xm[:, sl], wa_ref[n])
        zx_ref[:, sl] = _dot(xm[:, sl], wx_ref[n])
    ra = _sigmoid(za_ref[...] + ba)
    ii = _sigmoid(zx_ref[...] + bx)
    sp = _softplus(-lam)
    log_a = -LRU_C * ra * sp
    a = jnp.exp(log_a)
    mult = jnp.sqrt(_neg_expm1(2.0 * log_a))
    return ra, ii, sp, a, mult


def lru_fwd(proj, conv_w, conv_b, wa, ba, wx, bx, lam, name):
    s = proj.shape[0]
    c = MIX_W
    nblk = s // LRU_T
    hpb = LRU_T // HALO

    def body(x_ref, halo_ref, cw_ref, cb_ref, wa_ref, ba_ref, wx_ref, bx_ref, lam_ref,
             xc_ref, h_ref, carry, za_ref, zx_ref):
        i = pl.program_id(0)

        @pl.when(i == 0)
        def _():
            carry[...] = jnp.zeros_like(carry)

        halo = jnp.where(i == 0, 0.0, halo_ref[...])
        xcat = jnp.concatenate([halo, x_ref[...]], axis=0)
        taps = _conv_taps(xcat)
        xc = cb_ref[...] + sum(cw_ref[k:k + 1, :] * taps[k] for k in range(4))
        xc_ref[...] = xc
        _, ii, _, a, mult = _gates(xc, wa_ref, ba_ref[...], wx_ref, bx_ref[...], lam_ref[...], za_ref, zx_ref)
        h = _scan_block(a, mult * (ii * xc), carry[HALO - 1:HALO, :], False)
        h_ref[...] = h
        carry[...] = h[LRU_T - HALO:, :]

    def full(shape):
        return BS(shape, lambda i: (0,) * len(shape))

    blk = BS((LRU_T, c), lambda i: (i, 0))
    return pl.pallas_call(
        body, grid=(nblk,),
        in_specs=[blk, BS((HALO, c), lambda i: (jnp.maximum(i * hpb - 1, 0), 0)),
                  full((4, c)), full((1, c)), full((N_BLK, HD, HD)), full((1, c)),
                  full((N_BLK, HD, HD)), full((1, c)), full((1, c))],
        out_specs=[blk, blk],
        out_shape=[SDS((s, c), F32), SDS((s, c), F32)],
        scratch_shapes=[pltpu.VMEM((HALO, c), F32), pltpu.VMEM((LRU_T, c), F32), pltpu.VMEM((LRU_T, c), F32)],
        compiler_params=_cp("arbitrary"), name=name,
    )(proj, proj, conv_w, conv_b.reshape(1, c), wa.astype(MXU), ba.reshape(1, c), wx.astype(MXU),
      bx.reshape(1, c), lam.reshape(1, c))


def lru_mix_prep(h, proj, m, name):
    s = h.shape[0]
    tr = _row_tile(s)

    def body(h_ref, gb_ref, m_ref, o_ref):
        ge, _ = _gelu_and_grad(gb_ref[...])
        o_ref[:, :MIX_W] = (h_ref[...] * ge).astype(o_ref.dtype)
        o_ref[:, MIX_W:] = m_ref[...]

    return pl.pallas_call(
        body, grid=(s // tr,),
        in_specs=[BS((tr, MIX_W), lambda i: (i, 0)), BS((tr, MIX_W), lambda i: (i, 1)),
                  BS((tr, MEM_W), lambda i: (i, 0))],
        out_specs=BS((tr, D), lambda i: (i, 0)), out_shape=SDS((s, D), MXU),
        compiler_params=_cp("parallel"), name=name,
    )(h, proj, m)


def lru_bwd(dym, proj, xc, hl, dqm, conv_w, wa, ba, wx, bx, lam, name):
    s = proj.shape[0]
    c = MIX_W
    nblk = s // LRU_T
    hpb = LRU_T // HALO
    wa_m = wa.astype(MXU)
    wx_m = wx.astype(MXU)

    def body(dy_ref, x_ref, xhalo_ref, gb_ref, xc_ref, h_ref, hhalo_ref, dqm_ref,
             cw_ref, wa_ref, ba_ref, wx_ref, bx_ref, lam_ref,
             dproj_ref, dcw_ref, dcb_ref, dwa_ref, dba_ref, dwx_ref, dbx_ref, dlam_ref,
             g_next, a_next, dxc_next, za_ref, zx_ref, dxc_ref):
        i = pl.program_id(0)

        @pl.when(i == 0)
        def _():
            g_next[...] = jnp.zeros_like(g_next)
            a_next[...] = jnp.zeros_like(a_next)
            dxc_next[...] = jnp.zeros_like(dxc_next)
            for r in (dcw_ref, dcb_ref, dwa_ref, dba_ref, dwx_ref, dbx_ref, dlam_ref):
                r[...] = jnp.zeros_like(r)

        first = i == nblk - 1
        xc = xc_ref[...]
        lam = lam_ref[...]
        ra, ii, sp, a, mult = _gates(xc, wa_ref, ba_ref[...], wx_ref, bx_ref[...], lam, za_ref, zx_ref)
        hl_v = h_ref[...]
        ge, dge = _gelu_and_grad(gb_ref[...])
        dyl = dy_ref[...]
        dhl = dyl * ge
        dproj_ref[:, c:2 * c] = (dyl * hl_v * dge).astype(dproj_ref.dtype)
        dproj_ref[:, 2 * c:] = dqm_ref[...]

        an = _shift_up(a, 1, 0.0)
        last_row = lax.broadcasted_iota(jnp.int32, a.shape, 0) == LRU_T - 1
        an = jnp.where(last_row, a_next[0:1, :], an)
        g = _scan_block(an, dhl, g_next[0:1, :], True)
        g_next[...] = g[:HALO, :]
        a_next[...] = a[:HALO, :]

        hhalo = jnp.where(first, 0.0, hhalo_ref[...])
        h_prev = _shift_down(hl_v, 1, 0.0)
        first_row = lax.broadcasted_iota(jnp.int32, a.shape, 0) == 0
        h_prev = jnp.where(first_row, hhalo[HALO - 1:HALO, :], h_prev)
        da = g * h_prev
        ixc = ii * xc
        dmult = g * ixc
        dii = g * mult * xc
        dxc = g * mult * ii
        dlog_a = (da - dmult * a / mult) * a
        dra = dlog_a * (-LRU_C) * sp
        dlam_ref[...] += jnp.sum(dlog_a * ra, axis=0, keepdims=True) * (LRU_C * _sigmoid(-lam))
        dza = dra * ra * (1.0 - ra)
        dzx = dii * ii * (1.0 - ii)
        dba_ref[...] += jnp.sum(dza, axis=0, keepdims=True)
        dbx_ref[...] += jnp.sum(dzx, axis=0, keepdims=True)
        xm = xc.astype(MXU)
        dza_m = dza.astype(MXU)
        dzx_m = dzx.astype(MXU)
        for n in range(N_BLK):
            sl = slice(n * HD, (n + 1) * HD)
            dwa_ref[n] += _dot(xm[:, sl], dza_m[:, sl], TN)
            dwx_ref[n] += _dot(xm[:, sl], dzx_m[:, sl], TN)
            dxc_ref[:, sl] = _dot(dza_m[:, sl], wa_ref[n], NT) + _dot(dzx_m[:, sl], wx_ref[n], NT)
        dxc = dxc + dxc_ref[...]

        dcat = jnp.concatenate([dxc, dxc_next[...]], axis=0)
        rows = dcat.shape[0]
        dxb = cw_ref[3:4, :] * dxc
        for k in range(3):
            dxb = dxb + cw_ref[k:k + 1, :] * pltpu.roll(dcat, rows - (3 - k), 0)[:LRU_T]
        dproj_ref[:, :c] = dxb.astype(dproj_ref.dtype)
        dxc_next[...] = dxc[:HALO, :]

        xhalo = jnp.where(first, 0.0, xhalo_ref[...])
        taps = _conv_taps(jnp.concatenate([xhalo, x_ref[...]], axis=0))
        for k in range(4):
            dcw_ref[k:k + 1, :] += jnp.sum(dxc * taps[k], axis=0, keepdims=True)
        dcb_ref[...] += jnp.sum(dxc, axis=0, keepdims=True)

    def full(shape):
        return BS(shape, lambda i: (0,) * len(shape))

    def rev(i):
        return nblk - 1 - i

    blk0 = BS((LRU_T, c), lambda i: (rev(i), 0))
    blk1 = BS((LRU_T, c), lambda i: (rev(i), 1))
    halo = BS((HALO, c), lambda i: (jnp.maximum(rev(i) * hpb - 1, 0), 0))
    outs = pl.pallas_call(
        body, grid=(nblk,),
        in_specs=[blk0, blk0, halo, blk1, blk0, blk0, halo, BS((LRU_T, MEM_W), lambda i: (rev(i), 0)),
                  full((4, c)), full((N_BLK, HD, HD)), full((1, c)), full((N_BLK, HD, HD)), full((1, c)),
                  full((1, c))],
        out_specs=[BS((LRU_T, 2 * c + MEM_W), lambda i: (rev(i), 0)), full((4, c)), full((1, c)),
                   full((N_BLK, HD, HD)), full((1, c)), full((N_BLK, HD, HD)), full((1, c)), full((1, c))],
        out_shape=[SDS((s, 2 * c + MEM_W), MXU), SDS((4, c), F32), SDS((1, c), F32),
                   SDS((N_BLK, HD, HD), F32), SDS((1, c), F32), SDS((N_BLK, HD, HD), F32), SDS((1, c), F32),
                   SDS((1, c), F32)],
        scratch_shapes=[pltpu.VMEM((HALO, c), F32), pltpu.VMEM((HALO, c), F32), pltpu.VMEM((HALO, c), F32),
                        pltpu.VMEM((LRU_T, c), F32), pltpu.VMEM((LRU_T, c), F32), pltpu.VMEM((LRU_T, c), F32)],
        compiler_params=_cp("arbitrary"), name=name,
    )(dym, proj, proj, proj, xc, hl, hl, dqm, conv_w, wa_m, ba.reshape(1, c), wx_m, bx.reshape(1, c),
      lam.reshape(1, c))
    dproj, dcw, dcb, dwa, dba, dwx, dbx, dlam = outs
    return dproj, dcw, dcb.reshape(c), dwa, dba.reshape(c), dwx, dbx.reshape(c), dlam.reshape(c)


def _mem_probs(q, kv):
    heads = [slice(hh * HD, (hh + 1) * HD) for hh in range(MEM_HEADS)]
    sc = [_dot(q[:, sl], kv[:, sl], NT) * SCALE for sl in heads]
    e = [jnp.exp(s - jnp.max(s, axis=-1, keepdims=True)) for s in sc]
    return [x / jnp.sum(x, axis=-1, keepdims=True) for x in e]


def mem_attn_fwd(proj, q_col, kvm, name):
    s = proj.shape[0]
    tq = min(512, s)

    def body(q_ref, kv_ref, o_ref):
        q = q_ref[...].astype(MXU)
        kv = kv_ref[...]
        p = _mem_probs(q, kv)
        outs = [_dot(p[hh].astype(MXU), kv[:, MEM_W + hh * HD:MEM_W + (hh + 1) * HD]) for hh in range(MEM_HEADS)]
        o_ref[...] = jnp.concatenate(outs, axis=1).astype(o_ref.dtype)

    return pl.pallas_call(
        body, grid=(s // tq,),
        in_specs=[BS((tq, MEM_W), lambda i: (i, q_col)), BS((N_MEM, 2 * MEM_W), lambda i: (0, 0))],
        out_specs=BS((tq, MEM_W), lambda i: (i, 0)), out_shape=SDS((s, MEM_W), MXU),
        compiler_params=_cp("parallel"), name=name,
    )(proj, kvm)


def mem_attn_bwd(proj, q_col, kvm, dym, name):
    s = proj.shape[0]
    tq = min(512, s)

    def body(q_ref, kv_ref, do_ref, dq_ref, dkv_ref):
        @pl.when(pl.program_id(0) == 0)
        def _():
            dkv_ref[...] = jnp.zeros_like(dkv_ref)

        q = q_ref[...].astype(MXU)
        do = do_ref[...].astype(MXU)
        kv = kv_ref[...]
        heads = [slice(hh * HD, (hh + 1) * HD) for hh in range(MEM_HEADS)]
        p = _mem_probs(q, kv)
        dp = [_dot(do[:, sl], kv[:, MEM_W + hh * HD:MEM_W + (hh + 1) * HD], NT) for hh, sl in enumerate(heads)]
        ds = [(pp * (d - jnp.sum(pp * d, axis=-1, keepdims=True)) * SCALE).astype(MXU) for pp, d in zip(p, dp)]
        dq = [_dot(x, kv[:, sl]) for x, sl in zip(ds, heads)]
        dk = [_dot(x, q[:, sl], TN) for x, sl in zip(ds, heads)]
        dv = [_dot(pp.astype(MXU), do[:, sl], TN) for pp, sl in zip(p, heads)]
        dq_ref[...] = jnp.concatenate(dq, axis=1).astype(dq_ref.dtype)
        dkv_ref[...] += jnp.concatenate(dk + dv, axis=1)

    return pl.pallas_call(
        body, grid=(s // tq,),
        in_specs=[BS((tq, MEM_W), lambda i: (i, q_col)), BS((N_MEM, 2 * MEM_W), lambda i: (0, 0)),
                  BS((tq, MEM_W), lambda i: (i, MIX_W // MEM_W))],
        out_specs=[BS((tq, MEM_W), lambda i: (i, 0)), BS((N_MEM, 2 * MEM_W), lambda i: (0, 0))],
        out_shape=[SDS((s, MEM_W), MXU), SDS((N_MEM, 2 * MEM_W), F32)],
        compiler_params=_cp("arbitrary"), name=name,
    )(proj, kvm, dym)


def _dil_scores(q, kp, kc, n, slope_dil):
    qi = lax.broadcasted_iota(jnp.int32, (Q_BLOCK, Q_BLOCK), 0)
    ki = lax.broadcasted_iota(jnp.int32, (Q_BLOCK, Q_BLOCK), 1)
    rel_p = qi + Q_BLOCK - ki
    rel_c = qi - ki
    s_p = _dot(q, kp, NT) * SCALE - slope_dil * rel_p.astype(F32)
    s_c = _dot(q, kc, NT) * SCALE - slope_dil * rel_c.astype(F32)
    s_p = jnp.where((rel_p <= Q_BLOCK) & (n > 0), s_p, NEG_INF)
    s_c = jnp.where(rel_c >= 0, s_c, NEG_INF)
    return s_p, s_c


def _slope_dil(gi, hh):
    head = 4 * gi + hh
    return DIL_GROUPS[gi][1] * 2.0 ** (-8.0 * (head + 1.0) / N_BLK)


def _dil_operands(proj, kv, gi):
    dil = DIL_GROUPS[gi][1]
    if dil == 1:
        return proj, kv, kv, (lambda r: gi), (lambda r: gi), (lambda r: MIX_W // MEM_W + gi)
    sub = proj.shape[0] // dil

    def view(a, col):
        return a[:, col:col + MEM_W].reshape(sub, dil * MEM_W)

    same = lambda r: r
    return view(proj, gi * MEM_W), view(kv, gi * MEM_W), view(kv, MIX_W + gi * MEM_W), same, same, same


def dil_attn_fwd(proj, kv, gi, name):
    dil = DIL_GROUPS[gi][1]
    s, pw = proj.shape
    sub = s // dil
    nb = sub // Q_BLOCK
    qc, kc_ = pw // MEM_W, kv.shape[1] // MEM_W

    def body(q_ref, kp_ref, kc_ref, vp_ref, vc_ref, o_ref, lse_ref):
        n = pl.program_id(1)
        q = q_ref[...].astype(MXU)
        kp, kc, vp, vc = kp_ref[...], kc_ref[...], vp_ref[...], vc_ref[...]
        heads = [slice(hh * HD, (hh + 1) * HD) for hh in range(4)]
        sc = [_dil_scores(q[:, sl], kp[:, sl], kc[:, sl], n, _slope_dil(gi, hh)) for hh, sl in enumerate(heads)]
        mx = [jnp.maximum(jnp.max(s_p, axis=-1, keepdims=True), jnp.max(s_c, axis=-1, keepdims=True))
              for s_p, s_c in sc]
        den = [jnp.sum(jnp.exp(s_p - m), axis=-1, keepdims=True) + jnp.sum(jnp.exp(s_c - m), axis=-1, keepdims=True)
               for (s_p, s_c), m in zip(sc, mx)]
        lse = [m + jnp.log(d) for m, d in zip(mx, den)]
        pr = [(jnp.exp(s_p - l).astype(MXU), jnp.exp(s_c - l).astype(MXU)) for (s_p, s_c), l in zip(sc, lse)]
        outs = [_dot(p_p, vp[:, sl]) + _dot(p_c, vc[:, sl]) for (p_p, p_c), sl in zip(pr, heads)]
        o_ref[...] = jnp.concatenate(outs, axis=1)
        lse_ref[...] = jnp.concatenate([jnp.broadcast_to(l, (Q_BLOCK, HD)) for l in lse], axis=1)

    blk = (Q_BLOCK, MEM_W)
    prev = lambda n: jnp.maximum(n - 1, 0)
    out = BS(blk, lambda r, n: (n, r))
    qv, kview, vview, qcol, kcol, vcol = _dil_operands(proj, kv, gi)
    return pl.pallas_call(
        body, grid=(dil, nb),
        in_specs=[BS(blk, lambda r, n: (n, qcol(r))),
                  BS(blk, lambda r, n: (prev(n), kcol(r))), BS(blk, lambda r, n: (n, kcol(r))),
                  BS(blk, lambda r, n: (prev(n), vcol(r))), BS(blk, lambda r, n: (n, vcol(r)))],
        out_specs=[out, out],
        out_shape=[SDS((sub, dil * MEM_W), F32), SDS((sub, dil * MEM_W), F32)],
        compiler_params=_cp("parallel", "parallel"), name=name,
    )(qv, kview, kview, vview, vview)


def dil_attn_bwd(proj, kv, lse, do, dd, gi, name):
    dil = DIL_GROUPS[gi][1]
    s, pw = proj.shape
    sub = s // dil
    nb = sub // Q_BLOCK
    qc, kc_ = pw // MEM_W, kv.shape[1] // MEM_W

    def body(q_ref, kp_ref, kc_ref, vp_ref, vc_ref, lse_ref, do_ref, dd_ref, dq_ref, dk_ref, dv_ref, ck, cv):
        n = pl.program_id(1)

        @pl.when(n == 0)
        def _():
            ck[...] = jnp.zeros_like(ck)
            cv[...] = jnp.zeros_like(cv)

        @pl.when(n < nb)
        def _():
            q = q_ref[...].astype(MXU)
            do_m = do_ref[...].astype(MXU)
            kp, kc, vp, vc = kp_ref[...], kc_ref[...], vp_ref[...], vc_ref[...]
            lse_v, dd_v, ck_v, cv_v = lse_ref[...], dd_ref[...], ck[...], cv[...]
            heads = [slice(hh * HD, (hh + 1) * HD) for hh in range(4)]
            sc = [_dil_scores(q[:, sl], kp[:, sl], kc[:, sl], n, _slope_dil(gi, hh)) for hh, sl in enumerate(heads)]
            dp = [(_dot(do_m[:, sl], vp[:, sl], NT), _dot(do_m[:, sl], vc[:, sl], NT)) for sl in heads]
            pr = [(jnp.exp(s_p - lse_v[:, sl]), jnp.exp(s_c - lse_v[:, sl])) for (s_p, s_c), sl in zip(sc, heads)]
            ds = [((p_p * (dp_p + dd_v[:, sl]) * SCALE).astype(MXU), (p_c * (dp_c + dd_v[:, sl]) * SCALE).astype(MXU))
                  for (p_p, p_c), (dp_p, dp_c), sl in zip(pr, dp, heads)]
            pm = [(p_p.astype(MXU), p_c.astype(MXU)) for p_p, p_c in pr]
            dq = [_dot(ds_p, kp[:, sl]) + _dot(ds_c, kc[:, sl]) for (ds_p, ds_c), sl in zip(ds, heads)]
            dk = [ck_v[:, sl] + _dot(ds_p, q[:, sl], TN) for (ds_p, _), sl in zip(ds, heads)]
            dv = [cv_v[:, sl] + _dot(p_p, do_m[:, sl], TN) for (p_p, _), sl in zip(pm, heads)]
            ck_new = [_dot(ds_c, q[:, sl], TN) for (_, ds_c), sl in zip(ds, heads)]
            cv_new = [_dot(p_c, do_m[:, sl], TN) for (_, p_c), sl in zip(pm, heads)]
            dq_ref[...] = jnp.concatenate(dq, axis=1).astype(dq_ref.dtype)
            dk_ref[...] = jnp.concatenate(dk, axis=1)
            dv_ref[...] = jnp.concatenate(dv, axis=1)
            ck[...] = jnp.concatenate(ck_new, axis=1)
            cv[...] = jnp.concatenate(cv_new, axis=1)

        @pl.when(n == nb)
        def _():
            dk_ref[...] = ck[...]
            dv_ref[...] = cv[...]

    blk = (Q_BLOCK, MEM_W)
    cur = lambda n: jnp.minimum(n, nb - 1)
    prev = lambda n: jnp.maximum(jnp.minimum(n, nb - 1) - 1, 0)
    done = lambda n: jnp.maximum(n - 1, 0)
    own = BS(blk, lambda r, n: (cur(n), r))
    qv, kview, vview, qcol, kcol, vcol = _dil_operands(proj, kv, gi)
    return pl.pallas_call(
        body, grid=(dil, nb + 1),
        in_specs=[BS(blk, lambda r, n: (cur(n), qcol(r))),
                  BS(blk, lambda r, n: (prev(n), kcol(r))), BS(blk, lambda r, n: (cur(n), kcol(r))),
                  BS(blk, lambda r, n: (prev(n), vcol(r))), BS(blk, lambda r, n: (cur(n), vcol(r))),
                  own, own, own],
        out_specs=[own, BS(blk, lambda r, n: (done(n), r)), BS(blk, lambda r, n: (done(n), r))],
        out_shape=[SDS((sub, dil * MEM_W), MXU), SDS((sub, dil * MEM_W), F32), SDS((sub, dil * MEM_W), F32)],
        scratch_shapes=[pltpu.VMEM(blk, F32), pltpu.VMEM(blk, F32)],
        compiler_params=_cp("parallel", "arbitrary"), name=name,
    )(qv, kview, kview, vview, vview, lse, do, dd)


def _group_weights(lse_refs):
    l0, l1, l2 = (r[...] for r in lse_refs)
    mx = jnp.maximum(jnp.maximum(l0, l1), l2)
    e = [jnp.exp(l - mx) for l in (l0, l1, l2)]
    den = e[0] + e[1] + e[2]
    return [x / den for x in e]


def dil_mix_prep(o_list, lse_list, m, name):
    s = m.shape[0]
    tr = _row_tile(s)

    def body(o0, o1, o2, l0, l1, l2, m_ref, out_ref):
        w = _group_weights((l0, l1, l2))
        for g, o_ref in enumerate((o0, o1, o2)):
            out_ref[:, g * MEM_W:(g + 1) * MEM_W] = (o_ref[...] * w[g]).astype(out_ref.dtype)
        out_ref[:, MIX_W:] = m_ref[...]

    blk = BS((tr, MEM_W), lambda i: (i, 0))
    return pl.pallas_call(
        body, grid=(s // tr,), in_specs=[blk] * 7,
        out_specs=BS((tr, D), lambda i: (i, 0)), out_shape=SDS((s, D), MXU),
        compiler_params=_cp("parallel"), name=name,
    )(*o_list, *lse_list, m)


def dil_mix_bwd(dym, o_list, lse_list, name):
    s = dym.shape[0]
    tr = _row_tile(s)

    def body(da_ref, o0, o1, o2, l0, l1, l2, do0, do1, do2, dd0, dd1, dd2):
        w = _group_weights((l0, l1, l2))
        tot = None
        for g, (o_ref, do_ref) in enumerate(zip((o0, o1, o2), (do0, do1, do2))):
            da = da_ref[:, g * MEM_W:(g + 1) * MEM_W]
            do_ref[...] = da * w[g]
            x = da * o_ref[...]
            dw = jnp.concatenate(
                [jnp.broadcast_to(jnp.sum(x[:, hh * HD:(hh + 1) * HD], axis=-1, keepdims=True), (tr, HD))
                 for hh in range(4)], axis=1)
            tot = w[g] * dw if tot is None else tot + w[g] * dw
        for g, dd_ref in enumerate((dd0, dd1, dd2)):
            dd_ref[...] = -w[g] * tot

    blk = BS((tr, MEM_W), lambda i: (i, 0))
    outs = pl.pallas_call(
        body, grid=(s // tr,), in_specs=[BS((tr, MIX_W), lambda i: (i, 0))] + [blk] * 6,
        out_specs=[blk] * 6, out_shape=[SDS((s, MEM_W), F32)] * 6,
        compiler_params=_cp("parallel"), name=name,
    )(dym, *o_list, *lse_list)
    return outs[:3], outs[3:]


def sum_cast(parts, name):
    s = parts[0][0].shape[0]
    tr = _row_tile(s)
    flat = [a for p in parts for a in p]
    sizes = [len(p) for p in parts]

    def body(*refs):
        out_ref = refs[-1]
        pos = 0
        for j, n in enumerate(sizes):
            acc = refs[pos][...].astype(F32)
            for t in range(1, n):
                acc = acc + refs[pos + t][...].astype(F32)
            out_ref[:, j * MEM_W:(j + 1) * MEM_W] = acc.astype(out_ref.dtype)
            pos += n

    blk = BS((tr, MEM_W), lambda i: (i, 0))
    width = MEM_W * len(parts)
    return pl.pallas_call(
        body, grid=(s // tr,), in_specs=[blk] * len(flat),
        out_specs=BS((tr, width), lambda i: (i, 0)), out_shape=SDS((s, width), MXU),
        compiler_params=_cp("parallel"), name=name,
    )(*flat)


def add_n(arrs, name):
    rows, cols = arrs[0].shape
    tr = _row_tile(rows)

    def body(*refs):
        acc = refs[0][...]
        for r in refs[1:-1]:
            acc = acc + r[...]
        refs[-1][...] = acc

    blk = BS((tr, cols), lambda i: (i, 0))
    return pl.pallas_call(
        body, grid=(rows // tr,), in_specs=[blk] * len(arrs), out_specs=blk,
        out_shape=SDS((rows, cols), F32), compiler_params=_cp("parallel"), name=name,
    )(*arrs)


class _NoExchange:
    def hook(self, where, l, after):
        return []


def _fwd_bwd(x, mem, target, small, big, gs, gb, sched):
    s = x.shape[0]
    tm = min(1024, s)
    ts = min(2048, s)

    def after_hook(arr, where, l, after):
        toks = sched.hook(where, l, after)
        return tie(arr, toks, "tie_%s_%d" % (where, l)) if toks else arr

    h = x
    saved = []
    kv = None
    mem_n = None
    hn = norm_cast(h, small["a_pre_mix_g"][0], "pre_norm")
    for l in range(4):
        rec = l < 2
        p, j = ("a", l) if rec else ("b", l - 2)
        sv = {"h": h}
        hn = after_hook(hn, "fwd_begin", l, h)
        if mem_n is None:
            mem_n = norm_cast(mem, small["mem_norm_g"], "mem_norm")
        kvm = mm_nn(mem_n, big[p + "_w_mem_kv"][j], tm=N_MEM, tn=2 * MEM_W, tk=D, out_dtype=MXU, name="mem_kv")
        if rec:
            proj = mm_nn(hn, big["a_w_in"][j], tm=tm, tn=896, tk=D, out_dtype=F32, name="rec_in")
            xc, hl = lru_fwd(proj, small["a_conv_w"][j], small["a_conv_b"][j], small["a_gate_a_w"][j],
                             small["a_gate_a_b"][j], small["a_gate_x_w"][j], small["a_gate_x_b"][j],
                             small["a_lambda"][j], "lru_fwd")
            m = mem_attn_fwd(proj, 2 * MIX_W // MEM_W, kvm, "rec_mem_attn")
            ym = lru_mix_prep(hl, proj, m, "lru_mix_prep")
            sv.update(xc=xc, hl=hl)
        else:
            proj = mm_nn(hn, big["b_w_in"][j], tm=tm, tn=1024, tk=D, out_dtype=F32, name="dil_in")
            o_list, lse_list = [], []
            for gi in range(3):
                o, lse = dil_attn_fwd(proj, kv, gi, "dil_attn_fwd%d" % gi)
                o_list.append(o.reshape(s, MEM_W))
                lse_list.append(lse.reshape(s, MEM_W))
            m = mem_attn_fwd(proj, MIX_W // MEM_W, kvm, "dil_mem_attn")
            ym = dil_mix_prep(o_list, lse_list, m, "dil_mix_prep")
            sv.update(o=o_list, lse=lse_list)
        ym = after_hook(ym, "fwd_q1", l, ym)
        mix = mm_nn(ym, big[p + "_w_out"][j], tm=tm, tn=1024, tk=D, out_dtype=F32, name="mix_out")
        h1, hn2 = resid_norm_next(h, mix, small[p + "_post_mix_g"][j], small[p + "_pre_ffn_g"][j], "post_pre_norm")
        hn2 = after_hook(hn2, "fwd_mid", l, mix)
        g, u, act = ffn_in_fwd(hn2, big[p + "_w_ffn_in"][j], "ffn_in")
        act = after_hook(act, "fwd_q3", l, u)
        y2 = mm_nn(act, big[p + "_w_ffn_out"][j], tm=tm // 2, tn=D, tk=D_FF // 2, out_dtype=F32, name="ffn_out")
        sv.update(kvm=kvm, hn=hn, proj=proj, ym=ym, mix=mix, h1=h1, hn2=hn2, g=g, u=u, act=act, y2=y2)
        saved.append(sv)
        if l < 3:
            pn, jn = ("a", l + 1) if l + 1 < 2 else ("b", l - 1)
            h, hn = resid_norm_next(h1, y2, small[p + "_post_ffn_g"][j], small[pn + "_pre_mix_g"][jn],
                                    "post_pre_norm")
        else:
            h = resid_norm(h1, y2, small[p + "_post_ffn_g"][j], "post_norm")
        sched.hook("fwd_end", l, h)
        if l == 1:
            h_kv = h
            kvn = norm_cast(h, small["kv_norm_g"], "pre_norm")
            kv = mm_nn(kvn, big["w_kv_shared"], tm=tm, tn=768, tk=D, out_dtype=MXU, name="kv_proj")

    loss_parts, dh = loss_head(h, target, "loss_head")

    def stack2(name, j, val):
        gs.setdefault(name, [None, None])[j] = val

    def stack2b(name, j, val):
        gb.setdefault(name, [None, None])[j] = val

    dkv_parts = []
    dmem_parts = []
    dkvm = [None] * 4
    for l in (3, 2, 1, 0):
        rec = l < 2
        p, j = ("a", l) if rec else ("b", l - 2)
        sv = saved[l]
        if l == 1:
            dkv = sum_cast([(dkv_parts[0][c], dkv_parts[1][c]) for c in range(6)], "dkv_sum")
            dkvn = mm_nt([dkv], big["w_kv_shared"], tm=tm, tn=D, tk=768, out_dtype=F32, name="kv_proj_dx")
            gb["w_kv_shared"] = mm_tn(kvn, [dkv], t1=D, tn=768, ts=ts, col_shards=True, name="kv_proj_dw")
            dh, gs["kv_norm_g"] = norm_bwd(h_kv, small["kv_norm_g"], dkvn, dh, F32, "pre_norm_bwd")
        dy2, dg = norm_bwd(sv["y2"], small[p + "_post_ffn_g"][j], dh, None, MXU, "post_norm_bwd")
        dy2 = after_hook(dy2, "bwd_begin", l, dh)
        stack2(p + "_post_ffn_g", j, dg)
        dgg, dgu = ffn_act_bwd(dy2, big[p + "_w_ffn_out"][j], sv["g"], sv["u"], "ffn_act_bwd")
        dgg = after_hook(dgg, "bwd_mid1", l, dgu)
        stack2b(p + "_w_ffn_out", j, mm_tn(sv["act"], [dy2], t1=D_FF // 4, tn=D, ts=ts // 2, col_shards=False,
                                          name="ffn_out_dw"))
        dhn2 = mm_nt([dgg, dgu], big[p + "_w_ffn_in"][j], tm=tm // 2, tn=D, tk=D_FF // 2, out_dtype=F32,
                     name="ffn_in_dx")
        stack2b(p + "_w_ffn_in", j, mm_tn(sv["hn2"], [dgg, dgu], t1=D // 2, tn=D_FF // 4, ts=ts, col_shards=True,
                                         name="ffn_in_dw"))
        dhn2 = after_hook(dhn2, "bwd_mid2", l, gb[p + "_w_ffn_in"][j])
        dh1, dg = norm_bwd(sv["h1"], small[p + "_pre_ffn_g"][j], dhn2, dh, F32, "pre_norm_bwd")
        stack2(p + "_pre_ffn_g", j, dg)
        dmix, dg = norm_bwd(sv["mix"], small[p + "_post_mix_g"][j], dh1, None, MXU, "post_norm_bwd")
        stack2(p + "_post_mix_g", j, dg)
        dym = mm_nt([dmix], big[p + "_w_out"][j], tm=tm, tn=1024, tk=D, out_dtype=F32, name="mix_out_dx")
        stack2b(p + "_w_out", j, mm_tn(sv["ym"], [dmix], t1=D, tn=1024, ts=ts, col_shards=False,
                                      name="mix_out_dw"))
        dym = after_hook(dym, "bwd_m1", l, gb[p + "_w_out"][j])
        if rec:
            dqm, dkvm[l] = mem_attn_bwd(sv["proj"], 2 * MIX_W // MEM_W, sv["kvm"], dym, "rec_mem_attn_bwd")
            dproj, dcw, dcb, dwa, dba, dwx, dbx, dlam = lru_bwd(
                dym, sv["proj"], sv["xc"], sv["hl"], dqm, small["a_conv_w"][j], small["a_gate_a_w"][j],
                small["a_gate_a_b"][j], small["a_gate_x_w"][j], small["a_gate_x_b"][j], small["a_lambda"][j],
                "lru_bwd")
            for nm, val in (("a_conv_w", dcw), ("a_conv_b", dcb), ("a_gate_a_w", dwa), ("a_gate_a_b", dba),
                            ("a_gate_x_w", dwx), ("a_gate_x_b", dbx), ("a_lambda", dlam)):
                stack2(nm, j, val)
            dhn = mm_nt([dproj], big["a_w_in"][j], tm=tm, tn=D, tk=896, out_dtype=F32, name="rec_in_dx")
            stack2b("a_w_in", j, mm_tn(sv["hn"], [dproj], t1=D, tn=896, ts=ts, col_shards=True, name="rec_in_dw"))
        else:
            dqm, dkvm[l] = mem_attn_bwd(sv["proj"], MIX_W // MEM_W, sv["kvm"], dym, "dil_mem_attn_bwd")
            do_list, dd_list = dil_mix_bwd(dym, sv["o"], sv["lse"], "dil_mix_bwd")
            dq_list, dk_list, dv_list = [], [], []
            for gi in range(3):
                dil = DIL_GROUPS[gi][1]
                view = (s // dil, dil * MEM_W)
                dq, dk, dv = dil_attn_bwd(sv["proj"], kv, sv["lse"][gi].reshape(view), do_list[gi].reshape(view),
                                          dd_list[gi].reshape(view), gi, "dil_attn_bwd%d" % gi)
                dq_list.append(dq.reshape(s, MEM_W))
                dk_list.append(dk.reshape(s, MEM_W))
                dv_list.append(dv.reshape(s, MEM_W))
            dkv_parts.append(dk_list + dv_list)
            dproj = sum_cast([(a,) for a in dq_list + [dqm]], "dil_dproj")
            dhn = mm_nt([dproj], big["b_w_in"][j], tm=tm, tn=1024, tk=D, out_dtype=F32, name="dil_in_dx")
            stack2b("b_w_in", j, mm_tn(sv["hn"], [dproj], t1=D, tn=1024, ts=ts, col_shards=False, name="dil_in_dw"))
        dk_m = dkvm[l].astype(MXU)
        dmem_parts.append(mm_nt([dk_m], big[p + "_w_mem_kv"][j], tm=N_MEM, tn=D, tk=2 * MEM_W, out_dtype=F32,
                                name="mem_kv_dx"))
        stack2b(p + "_w_mem_kv", j, mm_tn(mem_n, [dk_m], t1=D, tn=2 * MEM_W, ts=N_MEM, col_shards=False,
                                         name="mem_kv_dw"))
        dh, dg = norm_bwd(sv["h"], small[p + "_pre_mix_g"][j], dhn, dh1, F32, "pre_norm_bwd")
        stack2(p + "_pre_mix_g", j, dg)
        dh = after_hook(dh, "bwd_end", l, dh)

    _, gs["mem_norm_g"] = norm_bwd(mem, small["mem_norm_g"], add_n(dmem_parts, "dmem_sum"), None, F32,
                                   "mem_norm_bwd")
    return loss_parts, dh


ANY = pl.BlockSpec(memory_space=pl.ANY)
CHIP_FLIPS = (1, 2, 3)


def _coords():
    return lax.axis_index("x"), lax.axis_index("y"), lax.axis_index("c")


def _flip(x, y, m):
    return x ^ (m >> 1), y ^ (m & 1)


def _remote(src, dst, send_sems, recv_sems, k, device):
    return pltpu.make_async_remote_copy(src_ref=src, dst_ref=dst, send_sem=send_sems.at[k], recv_sem=recv_sems.at[k],
                                        device_id=device, device_id_type=MESH)


def gather_shards(shards, name):
    n = len(shards)

    def body(*refs):
        ins, outs = refs[:n], refs[n:2 * n]
        send_sems, recv_sems = refs[2 * n:]
        x, y, c = _coords()
        me = 2 * x + y
        sib = (x, y, 1 - c)
        halves, sends = [], []
        for i in range(n):
            hr = shards[i].shape[0] // 2
            mine = pl.ds(pl.multiple_of(c * hr, 8), hr)
            other = pl.ds(pl.multiple_of((1 - c) * hr, 8), hr)
            halves.append((mine, other))
            own = _remote(ins[i], outs[i].at[me], send_sems, recv_sems, 7 * i + 6, sib)
            own.start()
            sends.append(own)
            for j, m in enumerate(CHIP_FLIPS):
                cp = _remote(ins[i].at[mine], outs[i].at[me, mine], send_sems, recv_sems, 7 * i + j,
                             (*_flip(x, y, m), c))
                cp.start()
                sends.append(cp)
        for i in range(n):
            mine, _ = halves[i]
            for j, m in enumerate(CHIP_FLIPS):
                slot = outs[i].at[me ^ m, mine]
                _remote(slot, slot, send_sems, recv_sems, 7 * i + j, sib).wait_recv()
                fwd = _remote(slot, slot, send_sems, recv_sems, 7 * i + 3 + j, sib)
                fwd.start()
                sends.append(fwd)
        for i in range(n):
            _, other = halves[i]
            for j, m in enumerate(CHIP_FLIPS):
                slot = outs[i].at[me ^ m, other]
                _remote(slot, slot, send_sems, recv_sems, 7 * i + 3 + j, sib).wait_recv()
            _remote(ins[i], outs[i].at[me], send_sems, recv_sems, 7 * i + 6, sib).wait_recv()
        for cp in sends:
            cp.wait_send()

    return pl.pallas_call(
        body, in_specs=[ANY] * n, out_specs=[ANY] * n,
        out_shape=[SDS((N_CHIPS,) + sh.shape, sh.dtype) for sh in shards],
        scratch_shapes=[pltpu.SemaphoreType.DMA((7 * n,)), pltpu.SemaphoreType.DMA((7 * n,))],
        name=name,
    )(*shards)


def swap_halves(grads, name):
    n = len(grads)

    def body(*refs):
        ins, outs = refs[:n], refs[n:2 * n]
        send_sems, recv_sems = refs[2 * n:]
        x, y, c = _coords()
        cps = []
        for i in range(n):
            hr = grads[i].shape[1] // 2
            other = pl.ds(pl.multiple_of((1 - c) * hr, 8), hr)
            cp = _remote(ins[i].at[pl.ds(0, N_CHIPS), other], outs[i], send_sems, recv_sems, i, (x, y, 1 - c))
            cp.start()
            cps.append(cp)
        for cp in cps:
            cp.wait()

    return pl.pallas_call(
        body, in_specs=[ANY] * n, out_specs=[ANY] * n,
        out_shape=[SDS((N_CHIPS, g.shape[1] // 2, g.shape[2]), g.dtype) for g in grads],
        scratch_shapes=[pltpu.SemaphoreType.DMA((n,)), pltpu.SemaphoreType.DMA((n,))],
        name=name,
    )(*grads)


def _sum_rows_tile(rows, cols):
    for tr in (512, 256, 128, 64, 32, 16):
        if rows % tr == 0 and tr * cols * 4 <= 2 * 1024 * 1024:
            return tr
    raise ValueError((rows, cols))


def half_sum(g, got, c_arr, name):
    _, r, cols = g.shape
    hr = r // 2
    tr = _sum_rows_tile(hr, cols)

    def my_chip():
        return 2 * lax.axis_index("x") + lax.axis_index("y")

    def body(g_ref, got_ref, o_ref, own_ref):
        p = (g_ref[...] + got_ref[...]).astype(o_ref.dtype)
        o_ref[...] = p

        @pl.when(pl.program_id(1) == my_chip())
        def _():
            own_ref[...] = p

    out = SDS((N_CHIPS, hr, cols), jnp.bfloat16)
    return pl.pallas_call(
        body, grid=(hr // tr, N_CHIPS),
        in_specs=[BS((None, None, tr, cols), lambda i, s: (s, lax.axis_index("c"), i, 0)),
                  BS((None, tr, cols), lambda i, s: (s, i, 0))],
        out_specs=[BS((None, tr, cols), lambda i, s: (s, i, 0)),
                   BS((None, tr, cols), lambda i, s: (my_chip(), i, 0))],
        out_shape=[out, out], compiler_params=_cp("parallel", "arbitrary"), name=name,
    )(g.reshape(N_CHIPS, 2, hr, cols), got)


def exchange_parts(parts, name):
    n = len(parts)

    def body(*refs):
        ins, outs = refs[:n], refs[n:2 * n]
        send_sems, recv_sems, loc_sems = refs[2 * n:]
        x, y, c = _coords()
        me = 2 * x + y
        cps, locs = [], []
        for i in range(n):
            loc = pltpu.make_async_copy(ins[i].at[me], outs[i].at[me], loc_sems.at[i])
            loc.start()
            locs.append(loc)
            for j, m in enumerate(CHIP_FLIPS):
                cp = _remote(ins[i].at[me ^ m], outs[i].at[me], send_sems, recv_sems, 3 * i + j, (*_flip(x, y, m), c))
                cp.start()
                cps.append(cp)
        for cp in cps:
            cp.wait()
        for loc in locs:
            loc.wait()

    return pl.pallas_call(
        body, in_specs=[ANY] * n, out_specs=[ANY] * n,
        out_shape=[SDS(p.shape, p.dtype) for p in parts],
        scratch_shapes=[pltpu.SemaphoreType.DMA((3 * n,)), pltpu.SemaphoreType.DMA((3 * n,)),
                        pltpu.SemaphoreType.DMA((n,))],
        name=name,
    )(*parts)


def slot_sum(slots, c_arr, name):
    _, hr, cols = slots.shape
    tr = _sum_rows_tile(hr, cols)
    nblk = hr // tr

    def body(s_ref, o_ref):
        acc = s_ref[0].astype(F32)
        for p in range(1, N_CHIPS):
            acc = acc + s_ref[p].astype(F32)
        o_ref[...] = acc

    return pl.pallas_call(
        body, grid=(nblk,), in_specs=[BS((N_CHIPS, tr, cols), lambda i: (0, i, 0))],
        out_specs=BS((tr, cols), lambda i: (lax.axis_index("c") * nblk + i, 0)),
        out_shape=SDS((2 * hr, cols), F32), compiler_params=_cp("parallel"), name=name,
    )(slots)


def share_halves(bufs, name):
    n = len(bufs)

    def body(*refs):
        outs = refs[n:2 * n]
        send_sems, recv_sems = refs[2 * n:]
        x, y, c = _coords()
        cps = []
        for i in range(n):
            hr = bufs[i].shape[0] // 2
            mine = outs[i].at[pl.ds(pl.multiple_of(c * hr, 8), hr)]
            cp = _remote(mine, mine, send_sems, recv_sems, i, (x, y, 1 - c))
            cp.start()
            cps.append(cp)
        for cp in cps:
            cp.wait()

    return pl.pallas_call(
        body, in_specs=[ANY] * n, out_specs=[ANY] * n,
        out_shape=[SDS(b.shape, b.dtype) for b in bufs],
        input_output_aliases={i: i for i in range(n)},
        scratch_shapes=[pltpu.SemaphoreType.DMA((n,)), pltpu.SemaphoreType.DMA((n,))],
        name=name,
    )(*bufs)


HBM_SPEC = pl.BlockSpec(memory_space=pltpu.HBM)
SEM_SPEC = pl.BlockSpec(memory_space=pltpu.SEMAPHORE)
EFFECT = pltpu.SideEffectType.DATAFLOW_SIDE_EFFECTING


def split_start(name, bufs, plan, n_copies):
    nb = len(bufs)

    def body(*refs):
        send_sems, recv_sems = refs[nb], refs[nb + 1]
        for k, (src, dst, dev) in enumerate(plan(refs[:nb])):
            _remote(src, dst, send_sems, recv_sems, k, dev).start()
        refs[-1][...] = jnp.zeros_like(refs[-1])

    outs = pl.pallas_call(
        body, name=name,
        out_shape=(pltpu.SemaphoreType.DMA((n_copies,)), pltpu.SemaphoreType.DMA((n_copies,)),
                   *[pltpu.HBM(b.shape, b.dtype) for b in bufs], SDS((8, LANES), F32)),
        in_specs=[HBM_SPEC] * nb, out_specs=(SEM_SPEC, SEM_SPEC, *[HBM_SPEC] * nb, VM),
        input_output_aliases={i: 2 + i for i in range(nb)},
        compiler_params=pltpu.CompilerParams(has_side_effects=EFFECT),
    )(*[pltpu.with_memory_space_constraint(b, pltpu.HBM) for b in bufs])
    return outs[0], outs[1], list(outs[2:2 + nb]), outs[-1]


def split_wait(name, send_sems, recv_sems, bufs, after, plan):
    nb = len(bufs)

    def body(*refs):
        send_ref, recv_ref = refs[nb], refs[nb + 1]
        for k, (src, dst, dev) in enumerate(plan(refs[:nb])):
            cp = _remote(src, dst, send_ref, recv_ref, k, dev)
            cp.wait_send()
            cp.wait_recv()

    outs = pl.pallas_call(
        body, name=name, out_shape=[pltpu.HBM(b.shape, b.dtype) for b in bufs],
        in_specs=[HBM_SPEC] * nb + [SEM_SPEC, SEM_SPEC, ANY], out_specs=[HBM_SPEC] * nb,
        input_output_aliases={i: i for i in range(nb)},
        compiler_params=pltpu.CompilerParams(has_side_effects=EFFECT),
    )(*bufs, send_sems, recv_sems, after)
    return list(outs)


def tie(x, tokens, name):
    def body(*refs):
        pass

    return pl.pallas_call(
        body, name=name, out_shape=SDS(x.shape, x.dtype), in_specs=[ANY] * (1 + len(tokens)), out_specs=ANY,
        input_output_aliases={0: 0},
    )(x, *tokens)


def plan_gather_ici(n, rows):
    def plan(refs):
        x, y, c = _coords()
        me = 2 * x + y
        out = []
        for i in range(n):
            hr = rows[i] // 2
            mine = pl.ds(pl.multiple_of(c * hr, 8), hr)
            out.append((refs[i], refs[n + i].at[me], (x, y, 1 - c)))
            for m in CHIP_FLIPS:
                out.append((refs[i].at[mine], refs[n + i].at[me, mine], (*_flip(x, y, m), c)))
        return out
    return plan


def plan_gather_d2d(n, rows):
    def plan(refs):
        x, y, c = _coords()
        me = 2 * x + y
        out = []
        for i in range(n):
            hr = rows[i] // 2
            mine = pl.ds(pl.multiple_of(c * hr, 8), hr)
            for m in CHIP_FLIPS:
                slot = refs[i].at[me ^ m, mine]
                out.append((slot, slot, (x, y, 1 - c)))
        return out
    return plan


def plan_swap(n, rows):
    def plan(refs):
        x, y, c = _coords()
        out = []
        for i in range(n):
            hr = rows[i] // 2
            other = pl.ds(pl.multiple_of((1 - c) * hr, 8), hr)
            out.append((refs[i].at[pl.ds(0, N_CHIPS), other], refs[n + i], (x, y, 1 - c)))
        return out
    return plan


def plan_exchange(n):
    def plan(refs):
        x, y, c = _coords()
        me = 2 * x + y
        out = []
        for i in range(n):
            for m in CHIP_FLIPS:
                out.append((refs[i].at[me ^ m], refs[n + i].at[me], (*_flip(x, y, m), c)))
        return out
    return plan


def plan_share(n, rows):
    def plan(refs):
        x, y, c = _coords()
        out = []
        for i in range(n):
            hr = rows[i] // 2
            mine = refs[i].at[pl.ds(pl.multiple_of(c * hr, 8), hr)]
            out.append((mine, mine, (x, y, 1 - c)))
        return out
    return plan


def reduce_scatter(grads, c_arr, tag):
    got = swap_halves(grads, "rs_swap_" + tag)
    parts = [half_sum(g, r, c_arr, "rs_half_sum") for g, r in zip(grads, got)]
    slots = exchange_parts(parts, "rs_exchange_" + tag)
    return share_halves([slot_sum(s, c_arr, "rs_slot_sum") for s in slots], "rs_share_" + tag)


VM = pl.BlockSpec(memory_space=pltpu.VMEM)


def small_gather(v, name):
    def body(v_ref, out_ref, send_sems, recv_sems):
        x, y, c = _coords()
        me = 2 * x + y
        out_ref[me] = v_ref[...]
        cps = []
        for j, m in enumerate(CHIP_FLIPS):
            cp = _remote(v_ref, out_ref.at[me], send_sems, recv_sems, j, (*_flip(x, y, m), c))
            cp.start()
            cps.append(cp)
        for cp in cps:
            cp.wait()

    return pl.pallas_call(
        body, in_specs=[VM], out_specs=VM, out_shape=SDS((N_CHIPS,) + v.shape, v.dtype),
        scratch_shapes=[pltpu.SemaphoreType.DMA((3,)), pltpu.SemaphoreType.DMA((3,))],
        compiler_params=pltpu.CompilerParams(vmem_limit_bytes=VMEM_LIMIT_BYTES), name=name,
    )(v)


def small_allreduce(v, name):
    def body(v_ref, out_ref, sib_buf, slots, send_sems, recv_sems):
        x, y, c = _coords()
        me = 2 * x + y
        swap = _remote(v_ref, sib_buf, send_sems, recv_sems, 0, (x, y, 1 - c))
        swap.start()
        swap.wait()
        slots[me] = v_ref[...] + sib_buf[...]
        cps = []
        for j, m in enumerate(CHIP_FLIPS):
            cp = _remote(slots.at[me], slots.at[me], send_sems, recv_sems, 1 + j, (*_flip(x, y, m), c))
            cp.start()
            cps.append(cp)
        for cp in cps:
            cp.wait()
        out_ref[...] = (slots[0] + slots[1]) + (slots[2] + slots[3])

    return pl.pallas_call(
        body, in_specs=[VM], out_specs=VM, out_shape=SDS(v.shape, v.dtype),
        scratch_shapes=[pltpu.VMEM(v.shape, v.dtype), pltpu.VMEM((N_CHIPS,) + v.shape, v.dtype),
                        pltpu.SemaphoreType.DMA((4,)), pltpu.SemaphoreType.DMA((4,))],
        compiler_params=pltpu.CompilerParams(vmem_limit_bytes=VMEM_LIMIT_BYTES), name=name,
    )(v)


def adamw(w, g_list, m, v, name):
    nl, rows, cols = w.shape
    tr = _sum_rows_tile(rows, cols) if rows % 16 == 0 else rows
    bc1 = 1.0 - ADAM_B1 ** ADAM_STEP
    bc2 = 1.0 - ADAM_B2 ** ADAM_STEP

    def body(*refs):
        w_ref, m_ref, v_ref = refs[:3]
        g_refs = refs[3:3 + nl]
        go_ref, d_ref, mo_ref, vo_ref = refs[3 + nl:]
        layer = pl.program_id(0)
        for l in range(nl):
            @pl.when(layer == l)
            def _(l=l):
                g = g_refs[l][...]
                m_new = ADAM_B1 * m_ref[...] + (1.0 - ADAM_B1) * g
                v_new = ADAM_B2 * v_ref[...] + (1.0 - ADAM_B2) * (g * g)
                m_hat = m_new / bc1
                v_hat = v_new / bc2
                go_ref[...] = g
                d_ref[...] = -ADAM_LR * (m_hat / (jnp.sqrt(v_hat) + ADAM_EPS) + ADAM_WD * w_ref[...])
                mo_ref[...] = m_new
                vo_ref[...] = v_new

    stk = BS((None, tr, cols), lambda l, i: (l, i, 0))
    flat = BS((tr, cols), lambda l, i: (i, 0))
    out = SDS((nl, rows, cols), F32)
    return pl.pallas_call(
        body, grid=(nl, rows // tr), in_specs=[stk] * 3 + [flat] * nl, out_specs=[stk] * 4,
        out_shape=[out] * 4, compiler_params=_cp("parallel", "parallel"), name=name,
    )(w, m, v, *g_list)


WEIGHTS = ["mem_norm_g", "a_pre_mix_g", "a_post_mix_g", "a_pre_ffn_g", "a_post_ffn_g", "a_w_in", "a_conv_w",
           "a_conv_b", "a_gate_a_w", "a_gate_a_b", "a_gate_x_w", "a_gate_x_b", "a_lambda", "a_w_mem_kv", "a_w_out",
           "a_w_ffn_in", "a_w_ffn_out", "kv_norm_g", "w_kv_shared", "b_pre_mix_g", "b_post_mix_g", "b_pre_ffn_g",
           "b_post_ffn_g", "b_w_in", "b_w_mem_kv", "b_w_out", "b_w_ffn_in", "b_w_ffn_out"]
BIG = {"a_w_in": True, "a_w_mem_kv": False, "a_w_out": False, "a_w_ffn_in": True, "a_w_ffn_out": False,
       "w_kv_shared": True, "b_w_in": False, "b_w_mem_kv": False, "b_w_out": False, "b_w_ffn_in": True,
       "b_w_ffn_out": False}
SHARDED_SMALL = ["a_pre_mix_g", "a_post_mix_g", "a_pre_ffn_g", "a_post_ffn_g", "a_conv_w", "a_conv_b", "a_gate_a_b",
                 "a_gate_x_b", "a_lambda"]
REPL_SMALL = ["mem_norm_g", "kv_norm_g", "b_pre_mix_g", "b_post_mix_g", "b_pre_ffn_g", "b_post_ffn_g", "a_gate_a_w",
              "a_gate_x_w"]
LANES = 128


def _pack(arrs, row_multiple=8):
    flat = jnp.concatenate([a.reshape(-1) for a in arrs])
    pad = -flat.shape[0] % (LANES * row_multiple)
    if pad:
        flat = jnp.concatenate([flat, jnp.zeros((pad,), flat.dtype)])
    return flat.reshape(-1, LANES)


def _unpack(packed, shapes):
    flat = packed.reshape(-1)
    out, pos = [], 0
    for sh in shapes:
        size = math.prod(sh)
        out.append(flat[pos:pos + size].reshape(sh))
        pos += size
    return out


def kernel(x, mem, mem_norm_g, a_pre_mix_g, a_post_mix_g, a_pre_ffn_g, a_post_ffn_g, a_w_in, a_conv_w, a_conv_b,
           a_gate_a_w, a_gate_a_b, a_gate_x_w, a_gate_x_b, a_lambda, a_w_mem_kv, a_w_out, a_w_ffn_in, a_w_ffn_out,
           kv_norm_g, w_kv_shared, b_pre_mix_g, b_post_mix_g, b_pre_ffn_g, b_post_ffn_g, b_w_in, b_w_mem_kv, b_w_out,
           b_w_ffn_in, b_w_ffn_out, loss_target, m_mem_norm_g, m_a_pre_mix_g, m_a_post_mix_g, m_a_pre_ffn_g,
           m_a_post_ffn_g, m_a_w_in, m_a_conv_w, m_a_conv_b, m_a_gate_a_w, m_a_gate_a_b, m_a_gate_x_w, m_a_gate_x_b,
           m_a_lambda, m_a_w_mem_kv, m_a_w_out, m_a_w_ffn_in, m_a_w_ffn_out, m_kv_norm_g, m_w_kv_shared, m_b_pre_mix_g,
           m_b_post_mix_g, m_b_pre_ffn_g, m_b_post_ffn_g, m_b_w_in, m_b_w_mem_kv, m_b_w_out, m_b_w_ffn_in, m_b_w_ffn_out,
           v_mem_norm_g, v_a_pre_mix_g, v_a_post_mix_g, v_a_pre_ffn_g, v_a_post_ffn_g, v_a_w_in, v_a_conv_w, v_a_conv_b,
           v_a_gate_a_w, v_a_gate_a_b, v_a_gate_x_w, v_a_gate_x_b, v_a_lambda, v_a_w_mem_kv, v_a_w_out, v_a_w_ffn_in,
           v_a_w_ffn_out, v_kv_norm_g, v_w_kv_shared, v_b_pre_mix_g, v_b_post_mix_g, v_b_pre_ffn_g, v_b_post_ffn_g,
           v_b_w_in, v_b_w_mem_kv, v_b_w_out, v_b_w_ffn_in, v_b_w_ffn_out):
    a = dict(locals())
    xi, yi, ci = _coords()
    chip = 2 * xi + yi
    c_arr = jnp.stack([ci, chip]).astype(jnp.int32)

    got = small_gather(_pack([a[n] for n in SHARDED_SMALL]), "small_gather")
    per_chip = [_unpack(got[s], [a[n].shape for n in SHARDED_SMALL]) for s in range(N_CHIPS)]
    small = {n: jnp.concatenate([per_chip[s][k] for s in range(N_CHIPS)], axis=-1)
             for k, n in enumerate(SHARDED_SMALL)}
    small.update({n: a[n] for n in REPL_SMALL})

    groups = []
    for l in range(4):
        p, j = ("a", l) if l < 2 else ("b", l - 2)
        groups.append([(p + "_" + n, j) for n in ("w_in", "w_mem_kv", "w_out")])
        groups.append([(p + "_" + n, j) for n in ("w_ffn_in", "w_ffn_out")])
    groups[3].append(("w_kv_shared", None))
    big = {n: [None, None] for n in BIG if n != "w_kv_shared"}
    gs, gb = {}, {}
    reduced = {n: [None, None] for n in BIG if n != "w_kv_shared"}

    def put(store, n, j, val):
        if j is None:
            store[n] = val
        else:
            store[n][j] = val

    class Exchange:
        def __init__(self):
            self.state = {}

        def gather_ici(self, g):
            shards = [(a[n] if j is None else a[n][j]).astype(MXU) for n, j in groups[g]]
            rows = [sh.shape[0] for sh in shards]
            lands = [lax.empty((N_CHIPS,) + sh.shape, sh.dtype) for sh in shards]
            plan = plan_gather_ici(len(shards), rows)
            ss, rs, bufs, tok = split_start("gather_ici_%d" % g, shards + lands, plan, 4 * len(shards))
            self.state["g", g] = (ss, rs, bufs, plan, rows)
            return tok

        def gather_d2d(self, g, after):
            ss, rs, bufs, plan, rows = self.state.pop(("g", g))
            n = len(rows)
            outs = split_wait("gather_ici_wait_%d" % g, ss, rs, bufs, after, plan)[n:]
            plan = plan_gather_d2d(n, rows)
            ss, rs, bufs, tok = split_start("gather_d2d_%d" % g, outs, plan, 3 * n)
            self.state["g", g] = (ss, rs, bufs, plan)
            return tok

        def gather_done(self, g, after):
            ss, rs, bufs, plan = self.state.pop(("g", g))
            outs = split_wait("gather_d2d_wait_%d" % g, ss, rs, bufs, after, plan)
            for (n, j), w in zip(groups[g], outs):
                put(big, n, j, w if BIG[n] else w.reshape(-1, w.shape[-1]))

        def rs_swap(self, g):
            grads = []
            for n, j in groups[g]:
                gr = gb[n] if j is None else gb[n][j]
                grads.append(gr if BIG[n] else gr.reshape(N_CHIPS, gr.shape[0] // N_CHIPS, gr.shape[1]))
            rows = [gr.shape[1] for gr in grads]
            lands = [lax.empty((N_CHIPS, gr.shape[1] // 2, gr.shape[2]), F32) for gr in grads]
            plan = plan_swap(len(grads), rows)
            ss, rs, bufs, tok = split_start("rs_swap_%d" % g, grads + lands, plan, len(grads))
            self.state["r", g] = (ss, rs, bufs, plan, rows)
            return tok

        def rs_exchange(self, g, after):
            ss, rs, bufs, plan, rows = self.state.pop(("r", g))
            n = len(rows)
            bufs = split_wait("rs_swap_wait_%d" % g, ss, rs, bufs, after, plan)
            sums = [half_sum(gr, got, c_arr, "rs_half_sum") for gr, got in zip(bufs[:n], bufs[n:])]
            plan = plan_exchange(n)
            ss, rs, bufs, tok = split_start("rs_exchange_%d" % g, [p for p, _ in sums] + [s for _, s in sums], plan,
                                            3 * n)
            self.state["r", g] = (ss, rs, bufs, plan, rows)
            return tok

        def rs_share(self, g, after):
            ss, rs, bufs, plan, rows = self.state.pop(("r", g))
            n = len(rows)
            slots = split_wait("rs_exchange_wait_%d" % g, ss, rs, bufs, after, plan)[n:]
            fulls = [slot_sum(s, c_arr, "rs_slot_sum") for s in slots]
            plan = plan_share(n, rows)
            ss, rs, bufs, tok = split_start("rs_share_%d" % g, fulls, plan, n)
            self.state["r", g] = (ss, rs, bufs, plan)
            return tok

        def rs_done(self, g, after):
            ss, rs, bufs, plan = self.state.pop(("r", g))
            outs = split_wait("rs_share_wait_%d" % g, ss, rs, bufs, after, plan)
            for (n, j), r in zip(groups[g], outs):
                put(reduced, n, j, r)

        def hook(self, where, l, after):
            mix, ffn = 2 * l, 2 * l + 1
            toks = []
            if where == "fwd_begin":
                if l == 0:
                    tok = self.gather_ici(mix)
                    tok = self.gather_d2d(mix, tok)
                    self.gather_done(mix, tok)
                toks.append(self.gather_ici(ffn))
            elif where == "fwd_q1":
                toks.append(self.gather_d2d(ffn, after))
            elif where == "fwd_mid":
                self.gather_done(ffn, after)
                if l < 3:
                    toks.append(self.gather_ici(mix + 2))
            elif where == "fwd_q3":
                if l < 3:
                    toks.append(self.gather_d2d(mix + 2, after))
            elif where == "fwd_end":
                if l < 3:
                    self.gather_done(mix + 2, after)
            elif where == "bwd_begin":
                if l < 3:
                    self.rs_done(ffn + 2, after)
                    toks.append(self.rs_exchange(mix + 2, after))
            elif where == "bwd_mid1":
                if l < 3:
                    toks.append(self.rs_share(mix + 2, after))
            elif where == "bwd_mid2":
                if l < 3:
                    self.rs_done(mix + 2, after)
                toks.append(self.rs_swap(ffn))
            elif where == "bwd_m1":
                toks.append(self.rs_exchange(ffn, after))
            elif where == "bwd_end":
                toks.append(self.rs_share(ffn, after))
                toks.append(self.rs_swap(mix))
                if l == 0:
                    self.rs_done(ffn, toks[0])
                    tok = self.rs_exchange(mix, toks[1])
                    tok = self.rs_share(mix, adamw_big([n for n in BIG if n.startswith("b_")], tok))
                    self.rs_done(mix, tok)
                    toks = []
            else:
                raise ValueError(where)
            return toks

    res = {}

    def adamw_big(names, token=None):
        last = None
        for n in names:
            shape = a[n].shape
            rows, cols = shape[-2], shape[-1]
            stk = (-1, rows, cols)
            grads = reduced[n] if isinstance(reduced[n], list) else [reduced[n]]
            if token is not None:
                grads = [tie(grads[0], [token], "tie_adamw_" + n)] + grads[1:]
            outs = adamw(a[n].reshape(stk), grads, a["m_" + n].reshape(stk), a["v_" + n].reshape(stk), "adamw")
            res[n] = [o.reshape(shape) for o in outs]
            last = outs[1]
            token = last if token is not None else None
        return last

    loss_parts, dx = _fwd_bwd(x[0], mem[0], loss_target[0], small, big, gs, gb, Exchange())
    loss = lax.psum(jnp.sum(loss_parts) * (0.5 / D), ("x", "y", "c"))
    adamw_big([n for n in BIG if n not in res])

    def full(n):
        g = gs[n]
        return jnp.stack(g) if isinstance(g, list) else g

    order = SHARDED_SMALL + REPL_SMALL
    full_shapes = [full(n).shape for n in order]
    summed = _unpack(small_allreduce(_pack([full(n) for n in order]), "small_allreduce"), full_shapes)
    mine = []
    for n, g in zip(order, summed):
        if n in SHARDED_SMALL:
            width = a[n].shape[-1]
            g = lax.dynamic_slice_in_dim(g, chip * width, width, axis=g.ndim - 1)
        mine.append(g.reshape(a[n].shape))
    shapes = [a[n].shape for n in order]
    rm = 512
    outs = adamw(_pack([a[n] for n in order], rm)[None], [_pack(mine, rm)],
                 _pack([a["m_" + n] for n in order], rm)[None], _pack([a["v_" + n] for n in order], rm)[None],
                 "adamw_small")
    unpacked = [_unpack(o[0], shapes) for o in outs]
    for k, n in enumerate(order):
        res[n] = [u[k] for u in unpacked]

    return (loss, dx[None], *[res[n][0] for n in WEIGHTS], *[res[n][1] for n in WEIGHTS],
            *[res[n][2] for n in WEIGHTS], *[res[n][3] for n in WEIGHTS])
```

```python
import functools
import math

import jax
import jax.numpy as jnp
from jax import lax
from jax.experimental import pallas as pl
from jax.experimental.pallas import tpu as pltpu

D = 2048
HD = 128
MEM_W = 512
MEM_HEADS = 4
MIX_W = D - MEM_W
N_BLK = MIX_W // HD
D_FF = 5632
N_MEM = 256
RMS_EPS = 1e-6
NEG_INF = -1e30
LRU_C = 8.0
DIL_GROUPS = ((128, 1), (512, 4), (2048, 16))
Q_BLOCK = 128
SCALE = HD ** -0.5
N_CHIPS = 4
MXU_COLS = 256
ACC_CHUNK = 2 * MXU_COLS

ADAM_LR = 0.001
ADAM_B1 = 0.9
ADAM_B2 = 0.999
ADAM_EPS = 1e-08
ADAM_WD = 0.01
ADAM_STEP = 10

MXU = jnp.bfloat16
F32 = jnp.float32
VMEM_LIMIT_BYTES = 56 * 1024 * 1024

BS = pl.BlockSpec
SDS = jax.ShapeDtypeStruct
MESH = pl.DeviceIdType.MESH


def _cp(*sem):
    return pltpu.CompilerParams(dimension_semantics=sem or None, vmem_limit_bytes=VMEM_LIMIT_BYTES)


def _dot(a, b, dn=((1,), (0,))):
    return lax.dot_general(a, b, (dn, ((), ())), preferred_element_type=F32)


def _div(i, n):
    return lax.div(i, jnp.int32(n))


def _rem(i, n):
    return lax.rem(i, jnp.int32(n))


NN = ((1,), (0,))
NT = ((1,), (1,))
TN = ((0,), (0,))


def _sigmoid(z):
    return 0.5 * jnp.tanh(0.5 * z) + 0.5


def _log1p_pos(u):
    return jnp.where(u < 1e-2, u * (1.0 - u * (0.5 - u * (1.0 / 3.0))), jnp.log(1.0 + u))


def _neg_expm1(z):
    return jnp.where(z > -1e-2, -z * (1.0 + z * (0.5 + z * (1.0 / 6.0))), 1.0 - jnp.exp(z))


def _softplus(z):
    return jnp.maximum(z, 0.0) + _log1p_pos(jnp.exp(-jnp.abs(z)))


_GELU_C = math.sqrt(2.0 / math.pi)


def _gelu_and_grad(x):
    x2 = x * x
    t = jnp.tanh(_GELU_C * (x + 0.044715 * x * x2))
    g = 0.5 * x * (1.0 + t)
    dg = 0.5 * (1.0 + t) + 0.5 * x * (1.0 - t * t) * _GELU_C * (1.0 + 3.0 * 0.044715 * x2)
    return g, dg


def _row_tile(rows):
    return min(256, rows)


def norm_cast(x, g, name):
    rows = x.shape[0]
    tr = _row_tile(rows)

    def body(x_ref, g_ref, o_ref):
        xv = x_ref[...]
        r = lax.rsqrt(jnp.mean(xv * xv, axis=-1, keepdims=True) + RMS_EPS)
        o_ref[...] = (xv * r * g_ref[...]).astype(o_ref.dtype)

    return pl.pallas_call(
        body, grid=(rows // tr,),
        in_specs=[BS((tr, D), lambda i: (i, 0)), BS((1, D), lambda i: (0, 0))],
        out_specs=BS((tr, D), lambda i: (i, 0)),
        out_shape=SDS((rows, D), MXU), compiler_params=_cp("parallel"), name=name,
    )(x, g.reshape(1, D))


def resid_norm(h, y, g, name):
    rows = h.shape[0]
    tr = _row_tile(rows)

    def body(h_ref, y_ref, g_ref, o_ref):
        yv = y_ref[...]
        r = lax.rsqrt(jnp.mean(yv * yv, axis=-1, keepdims=True) + RMS_EPS)
        o_ref[...] = h_ref[...] + yv * r * g_ref[...]

    return pl.pallas_call(
        body, grid=(rows // tr,),
        in_specs=[BS((tr, D), lambda i: (i, 0)), BS((tr, D), lambda i: (i, 0)), BS((1, D), lambda i: (0, 0))],
        out_specs=BS((tr, D), lambda i: (i, 0)),
        out_shape=SDS((rows, D), F32), compiler_params=_cp("parallel"), name=name,
    )(h, y, g.reshape(1, D))


def resid_norm_next(h, y, g, g_next, name):
    rows = h.shape[0]
    tr = _row_tile(rows)

    def body(h_ref, y_ref, g_ref, gn_ref, o_ref, n_ref):
        yv = y_ref[...]
        r = lax.rsqrt(jnp.mean(yv * yv, axis=-1, keepdims=True) + RMS_EPS)
        hv = h_ref[...] + yv * r * g_ref[...]
        o_ref[...] = hv
        r2 = lax.rsqrt(jnp.mean(hv * hv, axis=-1, keepdims=True) + RMS_EPS)
        n_ref[...] = (hv * r2 * gn_ref[...]).astype(n_ref.dtype)

    row = BS((tr, D), lambda i: (i, 0))
    vec = BS((1, D), lambda i: (0, 0))
    return pl.pallas_call(
        body, grid=(rows // tr,), in_specs=[row, row, vec, vec], out_specs=[row, row],
        out_shape=[SDS((rows, D), F32), SDS((rows, D), MXU)], compiler_params=_cp("parallel"), name=name,
    )(h, y, g.reshape(1, D), g_next.reshape(1, D))


def norm_bwd(x, g, dy, res, out_dtype, name):
    rows = x.shape[0]
    tr = _row_tile(rows)
    has_res = res is not None

    def body(*refs):
        if has_res:
            x_ref, g_ref, dy_ref, res_ref, dx_ref, dg_ref = refs
        else:
            x_ref, g_ref, dy_ref, dx_ref, dg_ref = refs
        xv = x_ref[...]
        dyv = dy_ref[...].astype(F32)
        r = lax.rsqrt(jnp.mean(xv * xv, axis=-1, keepdims=True) + RMS_EPS)
        xhat = xv * r
        dxhat = dyv * g_ref[...]
        dx = r * (dxhat - xhat * jnp.mean(dxhat * xhat, axis=-1, keepdims=True))
        if has_res:
            dx = dx + res_ref[...]
        dx_ref[...] = dx.astype(dx_ref.dtype)

        @pl.when(pl.program_id(0) == 0)
        def _():
            dg_ref[...] = jnp.zeros_like(dg_ref)

        dg_ref[...] += jnp.sum(dyv * xhat, axis=0, keepdims=True)

    row = BS((tr, D), lambda i: (i, 0))
    vec = BS((1, D), lambda i: (0, 0))
    ins = [x, g.reshape(1, D), dy] + ([res] if has_res else [])
    dx, dg = pl.pallas_call(
        body, grid=(rows // tr,),
        in_specs=[row, vec, row] + ([row] if has_res else []),
        out_specs=[row, vec],
        out_shape=[SDS((rows, D), out_dtype), SDS((1, D), F32)],
        compiler_params=_cp("arbitrary"), name=name,
    )(*ins)
    return dx, dg.reshape(D)


def loss_head(y, target, name):
    rows = y.shape[0]
    tr = _row_tile(rows)

    def body(y_ref, t_ref, dy_ref, acc_ref):
        err = y_ref[...] - t_ref[...]
        dy_ref[...] = err * (1.0 / D)

        @pl.when(pl.program_id(0) == 0)
        def _():
            acc_ref[...] = jnp.zeros_like(acc_ref)

        acc_ref[...] += jnp.sum(err * err, axis=0, keepdims=True)

    row = BS((tr, D), lambda i: (i, 0))
    dy, acc = pl.pallas_call(
        body, grid=(rows // tr,), in_specs=[row, row],
        out_specs=[row, BS((1, D), lambda i: (0, 0))],
        out_shape=[SDS((rows, D), F32), SDS((1, D), F32)],
        compiler_params=_cp("arbitrary"), name=name,
    )(y, target)
    return acc, dy


def _mm_call(ins, in_specs, pick, dn, grid, o_spec, out_sds, name):
    gk = grid[2]
    n_in = len(ins)

    def body(*refs):
        o_ref = refs[n_in]
        k = pl.program_id(2)

        def step(a_ref, b_ref):
            if gk == 1:
                o_ref[...] = _dot(a_ref[...], b_ref[...], dn).astype(o_ref.dtype)
                return
            acc = o_ref if out_sds.dtype == F32 else refs[n_in + 1]
            width = acc.shape[-1]
            if dn == TN or width <= ACC_CHUNK:
                chunks = [(0, width)]
            else:
                chunks = [(c0, min(c0 + ACC_CHUNK, width)) for c0 in range(0, width, ACC_CHUNK)]

            def sweep(first):
                a = a_ref[...]
                pending = None
                for c0, c1 in chunks:
                    p = _dot(a, b_ref[c0:c1, :] if dn == NT else b_ref[:, c0:c1], dn)
                    if pending is not None:
                        put(first, *pending)
                    pending = (c0, c1, p)
                put(first, *pending)

            def put(first, c0, c1, p):
                if first:
                    acc[:, c0:c1] = p
                else:
                    acc[:, c0:c1] += p

            @pl.when(k == 0)
            def _():
                sweep(True)

            @pl.when(k > 0)
            def _():
                sweep(False)

            if acc is not o_ref:
                @pl.when(k == gk - 1)
                def _():
                    o_ref[...] = acc[...].astype(o_ref.dtype)

        pick(refs[:n_in], k, step)

    scratch = []
    if gk > 1 and out_sds.dtype != F32:
        scratch = [pltpu.VMEM(o_spec.block_shape[-2:], F32)]
    return pl.pallas_call(
        body, grid=grid, in_specs=in_specs, out_specs=o_spec, out_shape=out_sds,
        scratch_shapes=scratch, compiler_params=_cp("parallel", "parallel", "arbitrary"), name=name,
    )(*ins)


def _pick2(refs, k, step):
    step(refs[0], refs[1])


def mm_nn(a, w, *, tm, tn, tk, out_dtype, name):
    m, kdim = a.shape
    if w.ndim == 3:
        c = w.shape[2]
        n = N_CHIPS * c
        per = c // tn
        b_spec = BS((None, tk, tn), lambda i, j, k: (_div(j, per), k, _rem(j, per)))
    else:
        n = w.shape[1]
        b_spec = BS((tk, tn), lambda i, j, k: (k, j))
    grid = (m // tm, n // tn, kdim // tk)
    return _mm_call([a, w], [BS((tm, tk), lambda i, j, k: (i, k)), b_spec], _pick2, NN, grid,
                    BS((tm, tn), lambda i, j, k: (i, j)), SDS((m, n), out_dtype), name)


def mm_nt(a_list, w, *, tm, tn, tk, out_dtype, name):
    m = a_list[0].shape[0]
    ka = a_list[0].shape[1]
    n_a = len(a_list)
    kdim = ka * n_a
    if w.ndim == 3:
        c = w.shape[2]
        n = w.shape[1]
        per = c // tk
        b_spec = BS((None, tn, tk), lambda i, j, k: (_div(k, per), j, _rem(k, per)))
    else:
        n = w.shape[0]
        b_spec = BS((tn, tk), lambda i, j, k: (j, k))
    gk = kdim // tk
    half = gk // n_a
    grid = (m // tm, n // tn, gk)
    if n_a == 1:
        a_specs = [BS((tm, tk), lambda i, j, k: (i, k))]
        pick = lambda refs, k, step: step(refs[0], refs[1])
    else:
        a_specs = [BS((tm, tk), lambda i, j, k: (i, jnp.minimum(k, half - 1))),
                   BS((tm, tk), lambda i, j, k: (i, jnp.maximum(k - half, 0)))]

        def pick(refs, k, step):
            @pl.when(k < half)
            def _():
                step(refs[0], refs[2])

            @pl.when(k >= half)
            def _():
                step(refs[1], refs[2])

    return _mm_call(list(a_list) + [w], a_specs + [b_spec], pick, NT, grid,
                    BS((tm, tn), lambda i, j, k: (i, j)), SDS((m, n), out_dtype), name)


def mm_tn(a, b_list, *, t1, tn, ts, col_shards, name):
    s, k1 = a.shape
    nb = b_list[0].shape[1]
    n_b = len(b_list)
    n = nb * n_b
    gn = n // tn
    half = gn // n_b
    grid = (k1 // t1, gn, s // ts)
    if col_shards:
        c = n // N_CHIPS
        per = c // tn
        o_spec = BS((None, t1, tn), lambda i, j, k: (_div(j, per), i, _rem(j, per)))
        out_sds = SDS((N_CHIPS, k1, c), MXU)
    else:
        o_spec = BS((t1, tn), lambda i, j, k: (i, j))
        out_sds = SDS((k1, n), MXU)
    a_spec = BS((ts, t1), lambda i, j, k: (k, i))
    if n_b == 1:
        b_specs = [BS((ts, tn), lambda i, j, k: (k, j))]
        pick = lambda refs, k, step: step(refs[0], refs[1])
    else:
        b_specs = [BS((ts, tn), lambda i, j, k: (k, jnp.minimum(j, half - 1))),
                   BS((ts, tn), lambda i, j, k: (k, jnp.maximum(j - half, 0)))]

        def pick(refs, k, step):
            j = pl.program_id(1)

            @pl.when(j < half)
            def _():
                step(refs[0], refs[1])

            @pl.when(j >= half)
            def _():
                step(refs[0], refs[2])

    return _mm_call([a] + list(b_list), [a_spec] + b_specs, pick, TN, grid, o_spec, out_sds, name)


def ffn_in_fwd(hn, w, name):
    s = hn.shape[0]
    tm = min(512, s)
    tn = D_FF // 4

    def tail(dag_ref, dau_ref, act_ref, c0, c1, g, u):
        sg = _sigmoid(g)
        silu = g * sg
        dag_ref[:, c0:c1] = u * sg * (1.0 + g * (1.0 - sg))
        dau_ref[:, c0:c1] = silu
        act_ref[:, c0:c1] = (silu * u).astype(act_ref.dtype)

    def body(a_ref, wg_ref, wu_ref, dag_ref, dau_ref, act_ref):
        a = a_ref[...]
        pending = None
        for c0 in range(0, tn, ACC_CHUNK):
            c1 = min(c0 + ACC_CHUNK, tn)
            g = _dot(a, wg_ref[:, c0:c1])
            u = _dot(a, wu_ref[:, c0:c1])
            if pending is not None:
                tail(dag_ref, dau_ref, act_ref, *pending)
            pending = (c0, c1, g, u)
        tail(dag_ref, dau_ref, act_ref, *pending)

    tile = BS((tm, tn), lambda j, i: (i, j))
    return pl.pallas_call(
        body, grid=(4, s // tm),
        in_specs=[BS((tm, D), lambda j, i: (i, 0)),
                  BS((None, D, tn), lambda j, i: (_div(j, 2), 0, _rem(j, 2))),
                  BS((None, D, tn), lambda j, i: (2 + _div(j, 2), 0, _rem(j, 2)))],
        out_specs=[tile, tile, tile],
        out_shape=[SDS((s, D_FF), F32), SDS((s, D_FF), F32), SDS((s, D_FF), MXU)],
        compiler_params=_cp("parallel", "parallel"), name=name,
    )(hn, w, w)


def ffn_act_bwd(dy, w_out, dag, dau, name):
    s = dy.shape[0]
    tm = min(512, s)
    tn = D_FF // 4

    def body(dy_ref, w_ref, dag_ref, dau_ref, dg_ref, du_ref):
        def tail(c0, c1, dact):
            dg_ref[:, c0:c1] = (dact * dag_ref[:, c0:c1]).astype(dg_ref.dtype)
            du_ref[:, c0:c1] = (dact * dau_ref[:, c0:c1]).astype(du_ref.dtype)

        dy = dy_ref[...]
        pending = None
        for c0 in range(0, tn, ACC_CHUNK):
            c1 = min(c0 + ACC_CHUNK, tn)
            dact = _dot(dy, w_ref[c0:c1, :], NT)
            if pending is not None:
                tail(*pending)
            pending = (c0, c1, dact)
        tail(*pending)

    tile = BS((tm, tn), lambda j, i: (i, j))
    return pl.pallas_call(
        body, grid=(4, s // tm),
        in_specs=[BS((tm, D), lambda j, i: (i, 0)), BS((tn, D), lambda j, i: (j, 0)), tile, tile],
        out_specs=[tile, tile],
        out_shape=[SDS((s, D_FF), MXU), SDS((s, D_FF), MXU)],
        compiler_params=_cp("parallel", "parallel"), name=name,
    )(dy, w_out, dag, dau)


LRU_T = 256
HALO = 8


def _shift_down(x, k, fill):
    rows = x.shape[0]
    idx = lax.broadcasted_iota(jnp.int32, x.shape, 0)
    return jnp.where(idx < k, fill, pltpu.roll(x, k, 0))


def _shift_up(x, k, fill):
    rows = x.shape[0]
    idx = lax.broadcasted_iota(jnp.int32, x.shape, 0)
    return jnp.where(idx >= rows - k, fill, pltpu.roll(x, rows - k, 0))


def _scan_block(a, b, carry, reverse):
    rows, cols = a.shape
    sub = 8
    in_group = lax.broadcasted_iota(jnp.int32, a.shape, 0) % sub
    for sh in (1, 2, 4):
        if reverse:
            a_s, b_s, ok = pltpu.roll(a, rows - sh, 0), pltpu.roll(b, rows - sh, 0), in_group < sub - sh
        else:
            a_s, b_s, ok = pltpu.roll(a, sh, 0), pltpu.roll(b, sh, 0), in_group >= sh
        b = jnp.where(ok, a * b_s + b, b)
        a = jnp.where(ok, a * a_s, a)
    groups = list(range(rows // sub))
    edge = 0 if reverse else sub - 1
    carry_in = {}
    for v in (reversed(groups) if reverse else groups):
        carry_in[v] = carry
        row = sub * v + edge
        carry = b[row:row + 1, :] + a[row:row + 1, :] * carry
    cin = jnp.concatenate([jnp.broadcast_to(carry_in[v], (sub, cols)) for v in groups], axis=0)
    return b + a * cin


def _conv_taps(xcat):
    rows = xcat.shape[0]
    taps = []
    for k in range(4):
        off = HALO - 3 + k
        taps.append(xcat[off:off + LRU_T] if off == HALO else pltpu.roll(xcat, rows - off, 0)[:LRU_T])
    return taps


def _gates(xc, wa_ref, ba, wx_ref, bx, lam, za_ref, zx_ref):
    xm = xc.astype(MXU)
    for n in range(N_BLK):
        sl = slice(n * HD, (n + 1) * HD)
        za_ref[:, sl] = _dot(xm[:, sl], wa_ref[n])
        zx_ref[:, sl] = _dot(xm[:, sl], wx_ref[n])
    ra = _sigmoid(za_ref[...] + ba)
    ii = _sigmoid(zx_ref[...] + bx)
    sp = _softplus(-lam)
    log_a = -LRU_C * ra * sp
    a = jnp.exp(log_a)
    mult = jnp.sqrt(_neg_expm1(2.0 * log_a))
    return ra, ii, sp, a, mult


def lru_fwd(proj, conv_w, conv_b, wa, ba, wx, bx, lam, name):
    s = proj.shape[0]
    c = MIX_W
    nblk = s // LRU_T
    hpb = LRU_T // HALO

    def body(x_ref, halo_ref, cw_ref, cb_ref, wa_ref, ba_ref, wx_ref, bx_ref, lam_ref,
             xc_ref, h_ref, carry, za_ref, zx_ref):
        i = pl.program_id(0)

        @pl.when(i == 0)
        def _():
            carry[...] = jnp.zeros_like(carry)

        halo = jnp.where(i == 0, 0.0, halo_ref[...])
        xcat = jnp.concatenate([halo, x_ref[...]], axis=0)
        taps = _conv_taps(xcat)
        xc = cb_ref[...] + sum(cw_ref[k:k + 1, :] * taps[k] for k in range(4))
        xc_ref[...] = xc
        _, ii, _, a, mult = _gates(xc, wa_ref, ba_ref[...], wx_ref, bx_ref[...], lam_ref[...], za_ref, zx_ref)
        h = _scan_block(a, mult * (ii * xc), carry[HALO - 1:HALO, :], False)
        h_ref[...] = h
        carry[...] = h[LRU_T - HALO:, :]

    def full(shape):
        return BS(shape, lambda i: (0,) * len(shape))

    blk = BS((LRU_T, c), lambda i: (i, 0))
    return pl.pallas_call(
        body, grid=(nblk,),
        in_specs=[blk, BS((HALO, c), lambda i: (jnp.maximum(i * hpb - 1, 0), 0)),
                  full((4, c)), full((1, c)), full((N_BLK, HD, HD)), full((1, c)),
                  full((N_BLK, HD, HD)), full((1, c)), full((1, c))],
        out_specs=[blk, blk],
        out_shape=[SDS((s, c), F32), SDS((s, c), F32)],
        scratch_shapes=[pltpu.VMEM((HALO, c), F32), pltpu.VMEM((LRU_T, c), F32), pltpu.VMEM((LRU_T, c), F32)],
        compiler_params=_cp("arbitrary"), name=name,
    )(proj, proj, conv_w, conv_b.reshape(1, c), wa.astype(MXU), ba.reshape(1, c), wx.astype(MXU),
      bx.reshape(1, c), lam.reshape(1, c))


def lru_mix_prep(h, proj, m, name):
    s = h.shape[0]
    tr = _row_tile(s)

    def body(h_ref, gb_ref, m_ref, o_ref):
        ge, _ = _gelu_and_grad(gb_ref[...])
        o_ref[:, :MIX_W] = (h_ref[...] * ge).astype(o_ref.dtype)
        o_ref[:, MIX_W:] = m_ref[...]

    return pl.pallas_call(
        body, grid=(s // tr,),
        in_specs=[BS((tr, MIX_W), lambda i: (i, 0)), BS((tr, MIX_W), lambda i: (i, 1)),
                  BS((tr, MEM_W), lambda i: (i, 0))],
        out_specs=BS((tr, D), lambda i: (i, 0)), out_shape=SDS((s, D), MXU),
        compiler_params=_cp("parallel"), name=name,
    )(h, proj, m)


def lru_bwd(dym, proj, xc, hl, dqm, conv_w, wa, ba, wx, bx, lam, name):
    s = proj.shape[0]
    c = MIX_W
    nblk = s // LRU_T
    hpb = LRU_T // HALO
    wa_m = wa.astype(MXU)
    wx_m = wx.astype(MXU)

    def body(dy_ref, x_ref, xhalo_ref, gb_ref, xc_ref, h_ref, hhalo_ref, dqm_ref,
             cw_ref, wa_ref, ba_ref, wx_ref, bx_ref, lam_ref,
             dproj_ref, dcw_ref, dcb_ref, dwa_ref, dba_ref, dwx_ref, dbx_ref, dlam_ref,
             g_next, a_next, dxc_next, za_ref, zx_ref, dxc_ref):
        i = pl.program_id(0)

        @pl.when(i == 0)
        def _():
            g_next[...] = jnp.zeros_like(g_next)
            a_next[...] = jnp.zeros_like(a_next)
            dxc_next[...] = jnp.zeros_like(dxc_next)
            for r in (dcw_ref, dcb_ref, dwa_ref, dba_ref, dwx_ref, dbx_ref, dlam_ref):
                r[...] = jnp.zeros_like(r)

        first = i == nblk - 1
        xc = xc_ref[...]
        lam = lam_ref[...]
        ra, ii, sp, a, mult = _gates(xc, wa_ref, ba_ref[...], wx_ref, bx_ref[...], lam, za_ref, zx_ref)
        hl_v = h_ref[...]
        ge, dge = _gelu_and_grad(gb_ref[...])
        dyl = dy_ref[...]
        dhl = dyl * ge
        dproj_ref[:, c:2 * c] = (dyl * hl_v * dge).astype(dproj_ref.dtype)
        dproj_ref[:, 2 * c:] = dqm_ref[...]

        an = _shift_up(a, 1, 0.0)
        last_row = lax.broadcasted_iota(jnp.int32, a.shape, 0) == LRU_T - 1
        an = jnp.where(last_row, a_next[0:1, :], an)
        g = _scan_block(an, dhl, g_next[0:1, :], True)
        g_next[...] = g[:HALO, :]
        a_next[...] = a[:HALO, :]

        hhalo = jnp.where(first, 0.0, hhalo_ref[...])
        h_prev = _shift_down(hl_v, 1, 0.0)
        first_row = lax.broadcasted_iota(jnp.int32, a.shape, 0) == 0
        h_prev = jnp.where(first_row, hhalo[HALO - 1:HALO, :], h_prev)
        da = g * h_prev
        ixc = ii * xc
        dmult = g * ixc
        dii = g * mult * xc
        dxc = g * mult * ii
        dlog_a = (da - dmult * a / mult) * a
        dra = dlog_a * (-LRU_C) * sp
        dlam_ref[...] += jnp.sum(dlog_a * ra, axis=0, keepdims=True) * (LRU_C * _sigmoid(-lam))
        dza = dra * ra * (1.0 - ra)
        dzx = dii * ii * (1.0 - ii)
        dba_ref[...] += jnp.sum(dza, axis=0, keepdims=True)
        dbx_ref[...] += jnp.sum(dzx, axis=0, keepdims=True)
        xm = xc.astype(MXU)
        dza_m = dza.astype(MXU)
        dzx_m = dzx.astype(MXU)
        for n in range(N_BLK):
            sl = slice(n * HD, (n + 1) * HD)
            dwa_ref[n] += _dot(xm[:, sl], dza_m[:, sl], TN)
            dwx_ref[n] += _dot(xm[:, sl], dzx_m[:, sl], TN)
            dxc_ref[:, sl] = _dot(dza_m[:, sl], wa_ref[n], NT) + _dot(dzx_m[:, sl], wx_ref[n], NT)
        dxc = dxc + dxc_ref[...]

        dcat = jnp.concatenate([dxc, dxc_next[...]], axis=0)
        rows = dcat.shape[0]
        dxb = cw_ref[3:4, :] * dxc
        for k in range(3):
            dxb = dxb + cw_ref[k:k + 1, :] * pltpu.roll(dcat, rows - (3 - k), 0)[:LRU_T]
        dproj_ref[:, :c] = dxb.astype(dproj_ref.dtype)
        dxc_next[...] = dxc[:HALO, :]

        xhalo = jnp.where(first, 0.0, xhalo_ref[...])
        taps = _conv_taps(jnp.concatenate([xhalo, x_ref[...]], axis=0))
        for k in range(4):
            dcw_ref[k:k + 1, :] += jnp.sum(dxc * taps[k], axis=0, keepdims=True)
        dcb_ref[...] += jnp.sum(dxc, axis=0, keepdims=True)

    def full(shape):
        return BS(shape, lambda i: (0,) * len(shape))

    def rev(i):
        return nblk - 1 - i

    blk0 = BS((LRU_T, c), lambda i: (rev(i), 0))
    blk1 = BS((LRU_T, c), lambda i: (rev(i), 1))
    halo = BS((HALO, c), lambda i: (jnp.maximum(rev(i) * hpb - 1, 0), 0))
    outs = pl.pallas_call(
        body, grid=(nblk,),
        in_specs=[blk0, blk0, halo, blk1, blk0, blk0, halo, BS((LRU_T, MEM_W), lambda i: (rev(i), 0)),
                  full((4, c)), full((N_BLK, HD, HD)), full((1, c)), full((N_BLK, HD, HD)), full((1, c)),
                  full((1, c))],
        out_specs=[BS((LRU_T, 2 * c + MEM_W), lambda i: (rev(i), 0)), full((4, c)), full((1, c)),
                   full((N_BLK, HD, HD)), full((1, c)), full((N_BLK, HD, HD)), full((1, c)), full((1, c))],
        out_shape=[SDS((s, 2 * c + MEM_W), MXU), SDS((4, c), F32), SDS((1, c), F32),
                   SDS((N_BLK, HD, HD), F32), SDS((1, c), F32), SDS((N_BLK, HD, HD), F32), SDS((1, c), F32),
                   SDS((1, c), F32)],
        scratch_shapes=[pltpu.VMEM((HALO, c), F32), pltpu.VMEM((HALO, c), F32), pltpu.VMEM((HALO, c), F32),
                        pltpu.VMEM((LRU_T, c), F32), pltpu.VMEM((LRU_T, c), F32), pltpu.VMEM((LRU_T, c), F32)],
        compiler_params=_cp("arbitrary"), name=name,
    )(dym, proj, proj, proj, xc, hl, hl, dqm, conv_w, wa_m, ba.reshape(1, c), wx_m, bx.reshape(1, c),
      lam.reshape(1, c))
    dproj, dcw, dcb, dwa, dba, dwx, dbx, dlam = outs
    return dproj, dcw, dcb.reshape(c), dwa, dba.reshape(c), dwx, dbx.reshape(c), dlam.reshape(c)


def _mem_probs(q, kv):
    heads = [slice(hh * HD, (hh + 1) * HD) for hh in range(MEM_HEADS)]
    sc = [_dot(q[:, sl], kv[:, sl], NT) * SCALE for sl in heads]
    e = [jnp.exp(s - jnp.max(s, axis=-1, keepdims=True)) for s in sc]
    return [x / jnp.sum(x, axis=-1, keepdims=True) for x in e]


def mem_attn_fwd(proj, q_col, kvm, name):
    s = proj.shape[0]
    tq = min(512, s)

    def body(q_ref, kv_ref, o_ref):
        q = q_ref[...].astype(MXU)
        kv = kv_ref[...]
        p = _mem_probs(q, kv)
        outs = [_dot(p[hh].astype(MXU), kv[:, MEM_W + hh * HD:MEM_W + (hh + 1) * HD]) for hh in range(MEM_HEADS)]
        o_ref[...] = jnp.concatenate(outs, axis=1).astype(o_ref.dtype)

    return pl.pallas_call(
        body, grid=(s // tq,),
        in_specs=[BS((tq, MEM_W), lambda i: (i, q_col)), BS((N_MEM, 2 * MEM_W), lambda i: (0, 0))],
        out_specs=BS((tq, MEM_W), lambda i: (i, 0)), out_shape=SDS((s, MEM_W), MXU),
        compiler_params=_cp("parallel"), name=name,
    )(proj, kvm)


def mem_attn_bwd(proj, q_col, kvm, dym, name):
    s = proj.shape[0]
    tq = min(512, s)

    def body(q_ref, kv_ref, do_ref, dq_ref, dkv_ref):
        @pl.when(pl.program_id(0) == 0)
        def _():
            dkv_ref[...] = jnp.zeros_like(dkv_ref)

        q = q_ref[...].astype(MXU)
        do = do_ref[...].astype(MXU)
        kv = kv_ref[...]
        heads = [slice(hh * HD, (hh + 1) * HD) for hh in range(MEM_HEADS)]
        p = _mem_probs(q, kv)
        dp = [_dot(do[:, sl], kv[:, MEM_W + hh * HD:MEM_W + (hh + 1) * HD], NT) for hh, sl in enumerate(heads)]
        ds = [(pp * (d - jnp.sum(pp * d, axis=-1, keepdims=True)) * SCALE).astype(MXU) for pp, d in zip(p, dp)]
        dq = [_dot(x, kv[:, sl]) for x, sl in zip(ds, heads)]
        dk = [_dot(x, q[:, sl], TN) for x, sl in zip(ds, heads)]
        dv = [_dot(pp.astype(MXU), do[:, sl], TN) for pp, sl in zip(p, heads)]
        dq_ref[...] = jnp.concatenate(dq, axis=1).astype(dq_ref.dtype)
        dkv_ref[...] += jnp.concatenate(dk + dv, axis=1)

    return pl.pallas_call(
        body, grid=(s // tq,),
        in_specs=[BS((tq, MEM_W), lambda i: (i, q_col)), BS((N_MEM, 2 * MEM_W), lambda i: (0, 0)),
                  BS((tq, MEM_W), lambda i: (i, MIX_W // MEM_W))],
        out_specs=[BS((tq, MEM_W), lambda i: (i, 0)), BS((N_MEM, 2 * MEM_W), lambda i: (0, 0))],
        out_shape=[SDS((s, MEM_W), MXU), SDS((N_MEM, 2 * MEM_W), F32)],
        compiler_params=_cp("arbitrary"), name=name,
    )(proj, kvm, dym)


def _dil_scores(q, kp, kc, n, slope_dil):
    qi = lax.broadcasted_iota(jnp.int32, (Q_BLOCK, Q_BLOCK), 0)
    ki = lax.broadcasted_iota(jnp.int32, (Q_BLOCK, Q_BLOCK), 1)
    rel_p = qi + Q_BLOCK - ki
    rel_c = qi - ki
    s_p = _dot(q, kp, NT) * SCALE - slope_dil * rel_p.astype(F32)
    s_c = _dot(q, kc, NT) * SCALE - slope_dil * rel_c.astype(F32)
    s_p = jnp.where((rel_p <= Q_BLOCK) & (n > 0), s_p, NEG_INF)
    s_c = jnp.where(rel_c >= 0, s_c, NEG_INF)
    return s_p, s_c


def _slope_dil(gi, hh):
    head = 4 * gi + hh
    return DIL_GROUPS[gi][1] * 2.0 ** (-8.0 * (head + 1.0) / N_BLK)


def _dil_operands(proj, kv, gi):
    dil = DIL_GROUPS[gi][1]
    if dil == 1:
        return proj, kv, kv, (lambda r: gi), (lambda r: gi), (lambda r: MIX_W // MEM_W + gi)
    sub = proj.shape[0] // dil

    def view(a, col):
        return a[:, col:col + MEM_W].reshape(sub, dil * MEM_W)

    same = lambda r: r
    return view(proj, gi * MEM_W), view(kv, gi * MEM_W), view(kv, MIX_W + gi * MEM_W), same, same, same


def dil_attn_fwd(proj, kv, gi, name):
    dil = DIL_GROUPS[gi][1]
    s, pw = proj.shape
    sub = s // dil
    nb = sub // Q_BLOCK
    qc, kc_ = pw // MEM_W, kv.shape[1] // MEM_W

    def body(q_ref, kp_ref, kc_ref, vp_ref, vc_ref, o_ref, lse_ref):
        n = pl.program_id(1)
        q = q_ref[...].astype(MXU)
        kp, kc, vp, vc = kp_ref[...], kc_ref[...], vp_ref[...], vc_ref[...]
        heads = [slice(hh * HD, (hh + 1) * HD) for hh in range(4)]
        sc = [_dil_scores(q[:, sl], kp[:, sl], kc[:, sl], n, _slope_dil(gi, hh)) for hh, sl in enumerate(heads)]
        mx = [jnp.maximum(jnp.max(s_p, axis=-1, keepdims=True), jnp.max(s_c, axis=-1, keepdims=True))
              for s_p, s_c in sc]
        den = [jnp.sum(jnp.exp(s_p - m), axis=-1, keepdims=True) + jnp.sum(jnp.exp(s_c - m), axis=-1, keepdims=True)
               for (s_p, s_c), m in zip(sc, mx)]
        lse = [m + jnp.log(d) for m, d in zip(mx, den)]
        pr = [(jnp.exp(s_p - l).astype(MXU), jnp.exp(s_c - l).astype(MXU)) for (s_p, s_c), l in zip(sc, lse)]
        outs = [_dot(p_p, vp[:, sl]) + _dot(p_c, vc[:, sl]) for (p_p, p_c), sl in zip(pr, heads)]
        o_ref[...] = jnp.concatenate(outs, axis=1)
        lse_ref[...] = jnp.concatenate([jnp.broadcast_to(l, (Q_BLOCK, HD)) for l in lse], axis=1)

    blk = (Q_BLOCK, MEM_W)
    prev = lambda n: jnp.maximum(n - 1, 0)
    out = BS(blk, lambda r, n: (n, r))
    qv, kview, vview, qcol, kcol, vcol = _dil_operands(proj, kv, gi)
    return pl.pallas_call(
        body, grid=(dil, nb),
        in_specs=[BS(blk, lambda r, n: (n, qcol(r))),
                  BS(blk, lambda r, n: (prev(n), kcol(r))), BS(blk, lambda r, n: (n, kcol(r))),
                  BS(blk, lambda r, n: (prev(n), vcol(r))), BS(blk, lambda r, n: (n, vcol(r)))],
        out_specs=[out, out],
        out_shape=[SDS((sub, dil * MEM_W), F32), SDS((sub, dil * MEM_W), F32)],
        compiler_params=_cp("parallel", "parallel"), name=name,
    )(qv, kview, kview, vview, vview)


def dil_attn_bwd(proj, kv, lse, do, dd, gi, name):
    dil = DIL_GROUPS[gi][1]
    s, pw = proj.shape
    sub = s // dil
    nb = sub // Q_BLOCK
    qc, kc_ = pw // MEM_W, kv.shape[1] // MEM_W

    def body(q_ref, kp_ref, kc_ref, vp_ref, vc_ref, lse_ref, do_ref, dd_ref, dq_ref, dk_ref, dv_ref, ck, cv):
        n = pl.program_id(1)

        @pl.when(n == 0)
        def _():
            ck[...] = jnp.zeros_like(ck)
            cv[...] = jnp.zeros_like(cv)

        @pl.when(n < nb)
        def _():
            q = q_ref[...].astype(MXU)
            do_m = do_ref[...].astype(MXU)
            kp, kc, vp, vc = kp_ref[...], kc_ref[...], vp_ref[...], vc_ref[...]
            lse_v, dd_v, ck_v, cv_v = lse_ref[...], dd_ref[...], ck[...], cv[...]
            heads = [slice(hh * HD, (hh + 1) * HD) for hh in range(4)]
            sc = [_dil_scores(q[:, sl], kp[:, sl], kc[:, sl], n, _slope_dil(gi, hh)) for hh, sl in enumerate(heads)]
            dp = [(_dot(do_m[:, sl], vp[:, sl], NT), _dot(do_m[:, sl], vc[:, sl], NT)) for sl in heads]
            pr = [(jnp.exp(s_p - lse_v[:, sl]), jnp.exp(s_c - lse_v[:, sl])) for (s_p, s_c), sl in zip(sc, heads)]
            ds = [((p_p * (dp_p + dd_v[:, sl]) * SCALE).astype(MXU), (p_c * (dp_c + dd_v[:, sl]) * SCALE).astype(MXU))
                  for (p_p, p_c), (dp_p, dp_c), sl in zip(pr, dp, heads)]
            pm = [(p_p.astype(MXU), p_c.astype(MXU)) for p_p, p_c in pr]
            dq = [_dot(ds_p, kp[:, sl]) + _dot(ds_c, kc[:, sl]) for (ds_p, ds_c), sl in zip(ds, heads)]
            dk = [ck_v[:, sl] + _dot(ds_p, q[:, sl], TN) for (ds_p, _), sl in zip(ds, heads)]
            dv = [cv_v[:, sl] + _dot(p_p, do_m[:, sl], TN) for (p_p, _), sl in zip(pm, heads)]
            ck_new = [_dot(ds_c, q[:, sl], TN) for (_, ds_c), sl in zip(ds, heads)]
            cv_new = [_dot(p_c, do_m[:, sl], TN) for (_, p_c), sl in zip(pm, heads)]
            dq_ref[...] = jnp.concatenate(dq, axis=1).astype(dq_ref.dtype)
            dk_ref[...] = jnp.concatenate(dk, axis=1)
            dv_ref[...] = jnp.concatenate(dv, axis=1)
            ck[...] = jnp.concatenate(ck_new, axis=1)
            cv[...] = jnp.concatenate(cv_new, axis=1)

        @pl.when(n == nb)
        def _():
            dk_ref[...] = ck[...]
            dv_ref[...] = cv[...]

    blk = (Q_BLOCK, MEM_W)
    cur = lambda n: jnp.minimum(n, nb - 1)
    prev = lambda n: jnp.maximum(jnp.minimum(n, nb - 1) - 1, 0)
    done = lambda n: jnp.maximum(n - 1, 0)
    own = BS(blk, lambda r, n: (cur(n), r))
    qv, kview, vview, qcol, kcol, vcol = _dil_operands(proj, kv, gi)
    return pl.pallas_call(
        body, grid=(dil, nb + 1),
        in_specs=[BS(blk, lambda r, n: (cur(n), qcol(r))),
                  BS(blk, lambda r, n: (prev(n), kcol(r))), BS(blk, lambda r, n: (cur(n), kcol(r))),
                  BS(blk, lambda r, n: (prev(n), vcol(r))), BS(blk, lambda r, n: (cur(n), vcol(r))),
                  own, own, own],
        out_specs=[own, BS(blk, lambda r, n: (done(n), r)), BS(blk, lambda r, n: (done(n), r))],
        out_shape=[SDS((sub, dil * MEM_W), MXU), SDS((sub, dil * MEM_W), F32), SDS((sub, dil * MEM_W), F32)],
        scratch_shapes=[pltpu.VMEM(blk, F32), pltpu.VMEM(blk, F32)],
        compiler_params=_cp("parallel", "arbitrary"), name=name,
    )(qv, kview, kview, vview, vview, lse, do, dd)


def _group_weights(lse_refs):
    l0, l1, l2 = (r[...] for r in lse_refs)
    mx = jnp.maximum(jnp.maximum(l0, l1), l2)
    e = [jnp.exp(l - mx) for l in (l0, l1, l2)]
    den = e[0] + e[1] + e[2]
    return [x / den for x in e]


def dil_mix_prep(o_list, lse_list, m, name):
    s = m.shape[0]
    tr = _row_tile(s)

    def body(o0, o1, o2, l0, l1, l2, m_ref, out_ref):
        w = _group_weights((l0, l1, l2))
        for g, o_ref in enumerate((o0, o1, o2)):
            out_ref[:, g * MEM_W:(g + 1) * MEM_W] = (o_ref[...] * w[g]).astype(out_ref.dtype)
        out_ref[:, MIX_W:] = m_ref[...]

    blk = BS((tr, MEM_W), lambda i: (i, 0))
    return pl.pallas_call(
        body, grid=(s // tr,), in_specs=[blk] * 7,
        out_specs=BS((tr, D), lambda i: (i, 0)), out_shape=SDS((s, D), MXU),
        compiler_params=_cp("parallel"), name=name,
    )(*o_list, *lse_list, m)


def dil_mix_bwd(dym, o_list, lse_list, name):
    s = dym.shape[0]
    tr = _row_tile(s)

    def body(da_ref, o0, o1, o2, l0, l1, l2, do0, do1, do2, dd0, dd1, dd2):
        w = _group_weights((l0, l1, l2))
        tot = None
        for g, (o_ref, do_ref) in enumerate(zip((o0, o1, o2), (do0, do1, do2))):
            da = da_ref[:, g * MEM_W:(g + 1) * MEM_W]
            do_ref[...] = da * w[g]
            x = da * o_ref[...]
            dw = jnp.concatenate(
                [jnp.broadcast_to(jnp.sum(x[:, hh * HD:(hh + 1) * HD], axis=-1, keepdims=True), (tr, HD))
                 for hh in range(4)], axis=1)
            tot = w[g] * dw if tot is None else tot + w[g] * dw
        for g, dd_ref in enumerate((dd0, dd1, dd2)):
            dd_ref[...] = -w[g] * tot

    blk = BS((tr, MEM_W), lambda i: (i, 0))
    outs = pl.pallas_call(
        body, grid=(s // tr,), in_specs=[BS((tr, MIX_W), lambda i: (i, 0))] + [blk] * 6,
        out_specs=[blk] * 6, out_shape=[SDS((s, MEM_W), F32)] * 6,
        compiler_params=_cp("parallel"), name=name,
    )(dym, *o_list, *lse_list)
    return outs[:3], outs[3:]


def sum_cast(parts, name):
    s = parts[0][0].shape[0]
    tr = _row_tile(s)
    flat = [a for p in parts for a in p]
    sizes = [len(p) for p in parts]

    def body(*refs):
        out_ref = refs[-1]
        pos = 0
        for j, n in enumerate(sizes):
            acc = refs[pos][...].astype(F32)
            for t in range(1, n):
                acc = acc + refs[pos + t][...].astype(F32)
            out_ref[:, j * MEM_W:(j + 1) * MEM_W] = acc.astype(out_ref.dtype)
            pos += n

    blk = BS((tr, MEM_W), lambda i: (i, 0))
    width = MEM_W * len(parts)
    return pl.pallas_call(
        body, grid=(s // tr,), in_specs=[blk] * len(flat),
        out_specs=BS((tr, width), lambda i: (i, 0)), out_shape=SDS((s, width), MXU),
        compiler_params=_cp("parallel"), name=name,
    )(*flat)


def add_n(arrs, name):
    rows, cols = arrs[0].shape
    tr = _row_tile(rows)

    def body(*refs):
        acc = refs[0][...]
        for r in refs[1:-1]:
            acc = acc + r[...]
        refs[-1][...] = acc

    blk = BS((tr, cols), lambda i: (i, 0))
    return pl.pallas_call(
        body, grid=(rows // tr,), in_specs=[blk] * len(arrs), out_specs=blk,
        out_shape=SDS((rows, cols), F32), compiler_params=_cp("parallel"), name=name,
    )(*arrs)


class _NoExchange:
    def hook(self, where, l, after):
        return []


def _fwd_bwd(x, mem, target, small, big, gs, gb, sched):
    s = x.shape[0]
    tm = min(1024, s)
    ts = min(2048, s)

    def after_hook(arr, where, l, after):
        toks = sched.hook(where, l, after)
        return tie(arr, toks, "tie_%s_%d" % (where, l)) if toks else arr

    h = x
    saved = []
    kv = None
    mem_n = None
    hn = norm_cast(h, small["a_pre_mix_g"][0], "pre_norm")
    for l in range(4):
        rec = l < 2
        p, j = ("a", l) if rec else ("b", l - 2)
        sv = {"h": h}
        hn = after_hook(hn, "fwd_begin", l, h)
        if mem_n is None:
            mem_n = norm_cast(mem, small["mem_norm_g"], "mem_norm")
        kvm = mm_nn(mem_n, big[p + "_w_mem_kv"][j], tm=N_MEM, tn=2 * MEM_W, tk=D, out_dtype=MXU, name="mem_kv")
        if rec:
            proj = mm_nn(hn, big["a_w_in"][j], tm=tm, tn=896, tk=D, out_dtype=F32, name="rec_in")
            xc, hl = lru_fwd(proj, small["a_conv_w"][j], small["a_conv_b"][j], small["a_gate_a_w"][j],
                             small["a_gate_a_b"][j], small["a_gate_x_w"][j], small["a_gate_x_b"][j],
                             small["a_lambda"][j], "lru_fwd")
            m = mem_attn_fwd(proj, 2 * MIX_W // MEM_W, kvm, "rec_mem_attn")
            ym = lru_mix_prep(hl, proj, m, "lru_mix_prep")
            sv.update(xc=xc, hl=hl)
        else:
            proj = mm_nn(hn, big["b_w_in"][j], tm=tm, tn=1024, tk=D, out_dtype=F32, name="dil_in")
            o_list, lse_list = [], []
            for gi in range(3):
                o, lse = dil_attn_fwd(proj, kv, gi, "dil_attn_fwd%d" % gi)
                o_list.append(o.reshape(s, MEM_W))
                lse_list.append(lse.reshape(s, MEM_W))
            m = mem_attn_fwd(proj, MIX_W // MEM_W, kvm, "dil_mem_attn")
            ym = dil_mix_prep(o_list, lse_list, m, "dil_mix_prep")
            sv.update(o=o_list, lse=lse_list)
        ym = after_hook(ym, "fwd_q1", l, ym)
        mix = mm_nn(ym, big[p + "_w_out"][j], tm=tm, tn=1024, tk=D, out_dtype=F32, name="mix_out")
        h1, hn2 = resid_norm_next(h, mix, small[p + "_post_mix_g"][j], small[p + "_pre_ffn_g"][j], "post_pre_norm")
        hn2 = after_hook(hn2, "fwd_mid", l, mix)
        g, u, act = ffn_in_fwd(hn2, big[p + "_w_ffn_in"][j], "ffn_in")
        act = after_hook(act, "fwd_q3", l, u)
        y2 = mm_nn(act, big[p + "_w_ffn_out"][j], tm=tm // 2, tn=D, tk=D_FF // 2, out_dtype=F32, name="ffn_out")
        sv.update(kvm=kvm, hn=hn, proj=proj, ym=ym, mix=mix, h1=h1, hn2=hn2, g=g, u=u, act=act, y2=y2)
        saved.append(sv)
        if l < 3:
            pn, jn = ("a", l + 1) if l + 1 < 2 else ("b", l - 1)
            h, hn = resid_norm_next(h1, y2, small[p + "_post_ffn_g"][j], small[pn + "_pre_mix_g"][jn],
                                    "post_pre_norm")
        else:
            h = resid_norm(h1, y2, small[p + "_post_ffn_g"][j], "post_norm")
        sched.hook("fwd_end", l, h)
        if l == 1:
            h_kv = h
            kvn = norm_cast(h, small["kv_norm_g"], "pre_norm")
            kv = mm_nn(kvn, big["w_kv_shared"], tm=tm, tn=768, tk=D, out_dtype=MXU, name="kv_proj")

    loss_parts, dh = loss_head(h, target, "loss_head")

    def stack2(name, j, val):
        gs.setdefault(name, [None, None])[j] = val

    def stack2b(name, j, val):
        gb.setdefault(name, [None, None])[j] = val

    dkv_parts = []
    dmem_parts = []
    dkvm = [None] * 4
    for l in (3, 2, 1, 0):
        rec = l < 2
        p, j = ("a", l) if rec else ("b", l - 2)
        sv = saved[l]
        if l == 1:
            dkv = sum_cast([(dkv_parts[0][c], dkv_parts[1][c]) for c in range(6)], "dkv_sum")
            dkvn = mm_nt([dkv], big["w_kv_shared"], tm=tm, tn=D, tk=768, out_dtype=F32, name="kv_proj_dx")
            gb["w_kv_shared"] = mm_tn(kvn, [dkv], t1=D, tn=768, ts=ts, col_shards=True, name="kv_proj_dw")
            dh, gs["kv_norm_g"] = norm_bwd(h_kv, small["kv_norm_g"], dkvn, dh, F32, "pre_norm_bwd")
        dy2, dg = norm_bwd(sv["y2"], small[p + "_post_ffn_g"][j], dh, None, MXU, "post_norm_bwd")
        dy2 = after_hook(dy2, "bwd_begin", l, dh)
        stack2(p + "_post_ffn_g", j, dg)
        dgg, dgu = ffn_act_bwd(dy2, big[p + "_w_ffn_out"][j], sv["g"], sv["u"], "ffn_act_bwd")
        dgg = after_hook(dgg, "bwd_mid1", l, dgu)
        stack2b(p + "_w_ffn_out", j, mm_tn(sv["act"], [dy2], t1=D_FF // 4, tn=D, ts=ts // 2, col_shards=False,
                                          name="ffn_out_dw"))
        dhn2 = mm_nt([dgg, dgu], big[p + "_w_ffn_in"][j], tm=tm // 2, tn=D, tk=D_FF // 2, out_dtype=F32,
                     name="ffn_in_dx")
        stack2b(p + "_w_ffn_in", j, mm_tn(sv["hn2"], [dgg, dgu], t1=D // 2, tn=D_FF // 4, ts=ts, col_shards=True,
                                         name="ffn_in_dw"))
        dhn2 = after_hook(dhn2, "bwd_mid2", l, gb[p + "_w_ffn_in"][j])
        dh1, dg = norm_bwd(sv["h1"], small[p + "_pre_ffn_g"][j], dhn2, dh, F32, "pre_norm_bwd")
        stack2(p + "_pre_ffn_g", j, dg)
        dmix, dg = norm_bwd(sv["mix"], small[p + "_post_mix_g"][j], dh1, None, MXU, "post_norm_bwd")
        stack2(p + "_post_mix_g", j, dg)
        dym = mm_nt([dmix], big[p + "_w_out"][j], tm=tm, tn=1024, tk=D, out_dtype=F32, name="mix_out_dx")
        stack2b(p + "_w_out", j, mm_tn(sv["ym"], [dmix], t1=D, tn=1024, ts=ts, col_shards=False,
                                      name="mix_out_dw"))
        dym = after_hook(dym, "bwd_m1", l, gb[p + "_w_out"][j])
        if rec:
            dqm, dkvm[l] = mem_attn_bwd(sv["proj"], 2 * MIX_W // MEM_W, sv["kvm"], dym, "rec_mem_attn_bwd")
            dproj, dcw, dcb, dwa, dba, dwx, dbx, dlam = lru_bwd(
                dym, sv["proj"], sv["xc"], sv["hl"], dqm, small["a_conv_w"][j], small["a_gate_a_w"][j],
                small["a_gate_a_b"][j], small["a_gate_x_w"][j], small["a_gate_x_b"][j], small["a_lambda"][j],
                "lru_bwd")
            for nm, val in (("a_conv_w", dcw), ("a_conv_b", dcb), ("a_gate_a_w", dwa), ("a_gate_a_b", dba),
                            ("a_gate_x_w", dwx), ("a_gate_x_b", dbx), ("a_lambda", dlam)):
                stack2(nm, j, val)
            dhn = mm_nt([dproj], big["a_w_in"][j], tm=tm, tn=D, tk=896, out_dtype=F32, name="rec_in_dx")
            stack2b("a_w_in", j, mm_tn(sv["hn"], [dproj], t1=D, tn=896, ts=ts, col_shards=True, name="rec_in_dw"))
        else:
            dqm, dkvm[l] = mem_attn_bwd(sv["proj"], MIX_W // MEM_W, sv["kvm"], dym, "dil_mem_attn_bwd")
            do_list, dd_list = dil_mix_bwd(dym, sv["o"], sv["lse"], "dil_mix_bwd")
            dq_list, dk_list, dv_list = [], [], []
            for gi in range(3):
                dil = DIL_GROUPS[gi][1]
                view = (s // dil, dil * MEM_W)
                dq, dk, dv = dil_attn_bwd(sv["proj"], kv, sv["lse"][gi].reshape(view), do_list[gi].reshape(view),
                                          dd_list[gi].reshape(view), gi, "dil_attn_bwd%d" % gi)
                dq_list.append(dq.reshape(s, MEM_W))
                dk_list.append(dk.reshape(s, MEM_W))
                dv_list.append(dv.reshape(s, MEM_W))
            dkv_parts.append(dk_list + dv_list)
            dproj = sum_cast([(a,) for a in dq_list + [dqm]], "dil_dproj")
            dhn = mm_nt([dproj], big["b_w_in"][j], tm=tm, tn=1024, tk=D, out_dtype=F32, name="dil_in_dx")
            stack2b("b_w_in", j, mm_tn(sv["hn"], [dproj], t1=D, tn=1024, ts=ts, col_shards=False, name="dil_in_dw"))
        dk_m = dkvm[l].astype(MXU)
        dmem_parts.append(mm_nt([dk_m], big[p + "_w_mem_kv"][j], tm=N_MEM, tn=D, tk=2 * MEM_W, out_dtype=F32,
                                name="mem_kv_dx"))
        stack2b(p + "_w_mem_kv", j, mm_tn(mem_n, [dk_m], t1=D, tn=2 * MEM_W, ts=N_MEM, col_shards=False,
                                         name="mem_kv_dw"))
        dh, dg = norm_bwd(sv["h"], small[p + "_pre_mix_g"][j], dhn, dh1, F32, "pre_norm_bwd")
        stack2(p + "_pre_mix_g", j, dg)
        dh = after_hook(dh, "bwd_end", l, dh)

    _, gs["mem_norm_g"] = norm_bwd(mem, small["mem_norm_g"], add_n(dmem_parts, "dmem_sum"), None, F32,
                                   "mem_norm_bwd")
    return loss_parts, dh


ANY = pl.BlockSpec(memory_space=pl.ANY)
CHIP_FLIPS = (1, 2, 3)


def _coords():
    return lax.axis_index("x"), lax.axis_index("y"), lax.axis_index("c")


def _flip(x, y, m):
    return x ^ (m >> 1), y ^ (m & 1)


def _remote(src, dst, send_sems, recv_sems, k, device):
    return pltpu.make_async_remote_copy(src_ref=src, dst_ref=dst, send_sem=send_sems.at[k], recv_sem=recv_sems.at[k],
                                        device_id=device, device_id_type=MESH)


def gather_shards(shards, name):
    n = len(shards)

    def body(*refs):
        ins, outs = refs[:n], refs[n:2 * n]
        send_sems, recv_sems = refs[2 * n:]
        x, y, c = _coords()
        me = 2 * x + y
        sib = (x, y, 1 - c)
        halves, sends = [], []
        for i in range(n):
            hr = shards[i].shape[0] // 2
            mine = pl.ds(pl.multiple_of(c * hr, 8), hr)
            other = pl.ds(pl.multiple_of((1 - c) * hr, 8), hr)
            halves.append((mine, other))
            own = _remote(ins[i], outs[i].at[me], send_sems, recv_sems, 7 * i + 6, sib)
            own.start()
            sends.append(own)
            for j, m in enumerate(CHIP_FLIPS):
                cp = _remote(ins[i].at[mine], outs[i].at[me, mine], send_sems, recv_sems, 7 * i + j,
                             (*_flip(x, y, m), c))
                cp.start()
                sends.append(cp)
        for i in range(n):
            mine, _ = halves[i]
            for j, m in enumerate(CHIP_FLIPS):
                slot = outs[i].at[me ^ m, mine]
                _remote(slot, slot, send_sems, recv_sems, 7 * i + j, sib).wait_recv()
                fwd = _remote(slot, slot, send_sems, recv_sems, 7 * i + 3 + j, sib)
                fwd.start()
                sends.append(fwd)
        for i in range(n):
            _, other = halves[i]
            for j, m in enumerate(CHIP_FLIPS):
                slot = outs[i].at[me ^ m, other]
                _remote(slot, slot, send_sems, recv_sems, 7 * i + 3 + j, sib).wait_recv()
            _remote(ins[i], outs[i].at[me], send_sems, recv_sems, 7 * i + 6, sib).wait_recv()
        for cp in sends:
            cp.wait_send()

    return pl.pallas_call(
        body, in_specs=[ANY] * n, out_specs=[ANY] * n,
        out_shape=[SDS((N_CHIPS,) + sh.shape, sh.dtype) for sh in shards],
        scratch_shapes=[pltpu.SemaphoreType.DMA((7 * n,)), pltpu.SemaphoreType.DMA((7 * n,))],
        name=name,
    )(*shards)


def swap_halves(grads, name):
    n = len(grads)

    def body(*refs):
        ins, outs = refs[:n], refs[n:2 * n]
        send_sems, recv_sems = refs[2 * n:]
        x, y, c = _coords()
        cps = []
        for i in range(n):
            hr = grads[i].shape[1] // 2
            other = pl.ds(pl.multiple_of((1 - c) * hr, 8), hr)
            cp = _remote(ins[i].at[pl.ds(0, N_CHIPS), other], outs[i], send_sems, recv_sems, i, (x, y, 1 - c))
            cp.start()
            cps.append(cp)
        for cp in cps:
            cp.wait()

    return pl.pallas_call(
        body, in_specs=[ANY] * n, out_specs=[ANY] * n,
        out_shape=[SDS((N_CHIPS, g.shape[1] // 2, g.shape[2]), g.dtype) for g in grads],
        scratch_shapes=[pltpu.SemaphoreType.DMA((n,)), pltpu.SemaphoreType.DMA((n,))],
        name=name,
    )(*grads)


def _sum_rows_tile(rows, cols):
    for tr in (512, 256, 128, 64, 32, 16):
        if rows % tr == 0 and tr * cols * 4 <= 2 * 1024 * 1024:
            return tr
    raise ValueError((rows, cols))


def half_sum(g, got, c_arr, name):
    _, r, cols = g.shape
    hr = r // 2
    tr = _sum_rows_tile(hr, cols)

    def my_chip():
        return 2 * lax.axis_index("x") + lax.axis_index("y")

    def body(g_ref, got_ref, o_ref, own_ref):
        p = (g_ref[...].astype(F32) + got_ref[...].astype(F32)).astype(o_ref.dtype)
        o_ref[...] = p

        @pl.when(pl.program_id(1) == my_chip())
        def _():
            own_ref[...] = p

    out = SDS((N_CHIPS, hr, cols), jnp.bfloat16)
    return pl.pallas_call(
        body, grid=(hr // tr, N_CHIPS),
        in_specs=[BS((None, None, tr, cols), lambda i, s: (s, lax.axis_index("c"), i, 0)),
                  BS((None, tr, cols), lambda i, s: (s, i, 0))],
        out_specs=[BS((None, tr, cols), lambda i, s: (s, i, 0)),
                   BS((None, tr, cols), lambda i, s: (my_chip(), i, 0))],
        out_shape=[out, out], compiler_params=_cp("parallel", "arbitrary"), name=name,
    )(g.reshape(N_CHIPS, 2, hr, cols), got)


def exchange_parts(parts, name):
    n = len(parts)

    def body(*refs):
        ins, outs = refs[:n], refs[n:2 * n]
        send_sems, recv_sems, loc_sems = refs[2 * n:]
        x, y, c = _coords()
        me = 2 * x + y
        cps, locs = [], []
        for i in range(n):
            loc = pltpu.make_async_copy(ins[i].at[me], outs[i].at[me], loc_sems.at[i])
            loc.start()
            locs.append(loc)
            for j, m in enumerate(CHIP_FLIPS):
                cp = _remote(ins[i].at[me ^ m], outs[i].at[me], send_sems, recv_sems, 3 * i + j, (*_flip(x, y, m), c))
                cp.start()
                cps.append(cp)
        for cp in cps:
            cp.wait()
        for loc in locs:
            loc.wait()

    return pl.pallas_call(
        body, in_specs=[ANY] * n, out_specs=[ANY] * n,
        out_shape=[SDS(p.shape, p.dtype) for p in parts],
        scratch_shapes=[pltpu.SemaphoreType.DMA((3 * n,)), pltpu.SemaphoreType.DMA((3 * n,)),
                        pltpu.SemaphoreType.DMA((n,))],
        name=name,
    )(*parts)


def slot_sum(slots, c_arr, name):
    _, hr, cols = slots.shape
    tr = _sum_rows_tile(hr, cols)
    nblk = hr // tr

    def body(s_ref, o_ref):
        acc = s_ref[0].astype(F32)
        for p in range(1, N_CHIPS):
            acc = acc + s_ref[p].astype(F32)
        o_ref[...] = acc

    return pl.pallas_call(
        body, grid=(nblk,), in_specs=[BS((N_CHIPS, tr, cols), lambda i: (0, i, 0))],
        out_specs=BS((tr, cols), lambda i: (lax.axis_index("c") * nblk + i, 0)),
        out_shape=SDS((2 * hr, cols), F32), compiler_params=_cp("parallel"), name=name,
    )(slots)


def share_halves(bufs, name):
    n = len(bufs)

    def body(*refs):
        outs = refs[n:2 * n]
        send_sems, recv_sems = refs[2 * n:]
        x, y, c = _coords()
        cps = []
        for i in range(n):
            hr = bufs[i].shape[0] // 2
            mine = outs[i].at[pl.ds(pl.multiple_of(c * hr, 8), hr)]
            cp = _remote(mine, mine, send_sems, recv_sems, i, (x, y, 1 - c))
            cp.start()
            cps.append(cp)
        for cp in cps:
            cp.wait()

    return pl.pallas_call(
        body, in_specs=[ANY] * n, out_specs=[ANY] * n,
        out_shape=[SDS(b.shape, b.dtype) for b in bufs],
        input_output_aliases={i: i for i in range(n)},
        scratch_shapes=[pltpu.SemaphoreType.DMA((n,)), pltpu.SemaphoreType.DMA((n,))],
        name=name,
    )(*bufs)


HBM_SPEC = pl.BlockSpec(memory_space=pltpu.HBM)
SEM_SPEC = pl.BlockSpec(memory_space=pltpu.SEMAPHORE)
EFFECT = pltpu.SideEffectType.DATAFLOW_SIDE_EFFECTING


def split_start(name, bufs, plan, n_copies):
    nb = len(bufs)

    def body(*refs):
        send_sems, recv_sems = refs[nb], refs[nb + 1]
        for k, (src, dst, dev) in enumerate(plan(refs[:nb])):
            _remote(src, dst, send_sems, recv_sems, k, dev).start()
        refs[-1][...] = jnp.zeros_like(refs[-1])

    outs = pl.pallas_call(
        body, name=name,
        out_shape=(pltpu.SemaphoreType.DMA((n_copies,)), pltpu.SemaphoreType.DMA((n_copies,)),
                   *[pltpu.HBM(b.shape, b.dtype) for b in bufs], SDS((8, LANES), F32)),
        in_specs=[HBM_SPEC] * nb, out_specs=(SEM_SPEC, SEM_SPEC, *[HBM_SPEC] * nb, VM),
        input_output_aliases={i: 2 + i for i in range(nb)},
        compiler_params=pltpu.CompilerParams(has_side_effects=EFFECT),
    )(*[pltpu.with_memory_space_constraint(b, pltpu.HBM) for b in bufs])
    return outs[0], outs[1], list(outs[2:2 + nb]), outs[-1]


def split_wait(name, send_sems, recv_sems, bufs, after, plan):
    nb = len(bufs)

    def body(*refs):
        send_ref, recv_ref = refs[nb], refs[nb + 1]
        for k, (src, dst, dev) in enumerate(plan(refs[:nb])):
            cp = _remote(src, dst, send_ref, recv_ref, k, dev)
            cp.wait_send()
            cp.wait_recv()

    outs = pl.pallas_call(
        body, name=name, out_shape=[pltpu.HBM(b.shape, b.dtype) for b in bufs],
        in_specs=[HBM_SPEC] * nb + [SEM_SPEC, SEM_SPEC, ANY], out_specs=[HBM_SPEC] * nb,
        input_output_aliases={i: i for i in range(nb)},
        compiler_params=pltpu.CompilerParams(has_side_effects=EFFECT),
    )(*bufs, send_sems, recv_sems, after)
    return list(outs)


def tie(x, tokens, name):
    def body(*refs):
        pass

    return pl.pallas_call(
        body, name=name, out_shape=SDS(x.shape, x.dtype), in_specs=[ANY] * (1 + len(tokens)), out_specs=ANY,
        input_output_aliases={0: 0},
    )(x, *tokens)


def plan_gather_ici(n, rows):
    def plan(refs):
        x, y, c = _coords()
        me = 2 * x + y
        out = []
        for i in range(n):
            hr = rows[i] // 2
            mine = pl.ds(pl.multiple_of(c * hr, 8), hr)
            out.append((refs[i], refs[n + i].at[me], (x, y, 1 - c)))
            for m in CHIP_FLIPS:
                out.append((refs[i].at[mine], refs[n + i].at[me, mine], (*_flip(x, y, m), c)))
        return out
    return plan


def plan_gather_d2d(n, rows):
    def plan(refs):
        x, y, c = _coords()
        me = 2 * x + y
        out = []
        for i in range(n):
            hr = rows[i] // 2
            mine = pl.ds(pl.multiple_of(c * hr, 8), hr)
            for m in CHIP_FLIPS:
                slot = refs[i].at[me ^ m, mine]
                out.append((slot, slot, (x, y, 1 - c)))
        return out
    return plan


def plan_swap(n, rows):
    def plan(refs):
        x, y, c = _coords()
        out = []
        for i in range(n):
            hr = rows[i] // 2
            other = pl.ds(pl.multiple_of((1 - c) * hr, 8), hr)
            out.append((refs[i].at[pl.ds(0, N_CHIPS), other], refs[n + i], (x, y, 1 - c)))
        return out
    return plan


def plan_exchange(n):
    def plan(refs):
        x, y, c = _coords()
        me = 2 * x + y
        out = []
        for i in range(n):
            for m in CHIP_FLIPS:
                out.append((refs[i].at[me ^ m], refs[n + i].at[me], (*_flip(x, y, m), c)))
        return out
    return plan


def plan_share(n, rows):
    def plan(refs):
        x, y, c = _coords()
        out = []
        for i in range(n):
            hr = rows[i] // 2
            mine = refs[i].at[pl.ds(pl.multiple_of(c * hr, 8), hr)]
            out.append((mine, mine, (x, y, 1 - c)))
        return out
    return plan


def reduce_scatter(grads, c_arr, tag):
    got = swap_halves(grads, "rs_swap_" + tag)
    parts = [half_sum(g, r, c_arr, "rs_half_sum") for g, r in zip(grads, got)]
    slots = exchange_parts(parts, "rs_exchange_" + tag)
    return share_halves([slot_sum(s, c_arr, "rs_slot_sum") for s in slots], "rs_share_" + tag)


VM = pl.BlockSpec(memory_space=pltpu.VMEM)


def small_gather(v, name):
    def body(v_ref, out_ref, send_sems, recv_sems):
        x, y, c = _coords()
        me = 2 * x + y
        out_ref[me] = v_ref[...]
        cps = []
        for j, m in enumerate(CHIP_FLIPS):
            cp = _remote(v_ref, out_ref.at[me], send_sems, recv_sems, j, (*_flip(x, y, m), c))
            cp.start()
            cps.append(cp)
        for cp in cps:
            cp.wait()

    return pl.pallas_call(
        body, in_specs=[VM], out_specs=VM, out_shape=SDS((N_CHIPS,) + v.shape, v.dtype),
        scratch_shapes=[pltpu.SemaphoreType.DMA((3,)), pltpu.SemaphoreType.DMA((3,))],
        compiler_params=pltpu.CompilerParams(vmem_limit_bytes=VMEM_LIMIT_BYTES), name=name,
    )(v)


def small_allreduce(v, name):
    def body(v_ref, out_ref, sib_buf, slots, send_sems, recv_sems):
        x, y, c = _coords()
        me = 2 * x + y
        swap = _remote(v_ref, sib_buf, send_sems, recv_sems, 0, (x, y, 1 - c))
        swap.start()
        swap.wait()
        slots[me] = v_ref[...] + sib_buf[...]
        cps = []
        for j, m in enumerate(CHIP_FLIPS):
            cp = _remote(slots.at[me], slots.at[me], send_sems, recv_sems, 1 + j, (*_flip(x, y, m), c))
            cp.start()
            cps.append(cp)
        for cp in cps:
            cp.wait()
        out_ref[...] = (slots[0] + slots[1]) + (slots[2] + slots[3])

    return pl.pallas_call(
        body, in_specs=[VM], out_specs=VM, out_shape=SDS(v.shape, v.dtype),
        scratch_shapes=[pltpu.VMEM(v.shape, v.dtype), pltpu.VMEM((N_CHIPS,) + v.shape, v.dtype),
                        pltpu.SemaphoreType.DMA((4,)), pltpu.SemaphoreType.DMA((4,))],
        compiler_params=pltpu.CompilerParams(vmem_limit_bytes=VMEM_LIMIT_BYTES), name=name,
    )(v)


def adamw(w, g_list, m, v, name):
    nl, rows, cols = w.shape
    tr = _sum_rows_tile(rows, cols) if rows % 16 == 0 else rows
    bc1 = 1.0 - ADAM_B1 ** ADAM_STEP
    bc2 = 1.0 - ADAM_B2 ** ADAM_STEP

    def body(*refs):
        w_ref, m_ref, v_ref = refs[:3]
        g_refs = refs[3:3 + nl]
        go_ref, d_ref, mo_ref, vo_ref = refs[3 + nl:]
        layer = pl.program_id(0)
        for l in range(nl):
            @pl.when(layer == l)
            def _(l=l):
                g = g_refs[l][...]
                m_new = ADAM_B1 * m_ref[...] + (1.0 - ADAM_B1) * g
                v_new = ADAM_B2 * v_ref[...] + (1.0 - ADAM_B2) * (g * g)
                m_hat = m_new / bc1
                v_hat = v_new / bc2
                go_ref[...] = g
                d_ref[...] = -ADAM_LR * (m_hat / (jnp.sqrt(v_hat) + ADAM_EPS) + ADAM_WD * w_ref[...])
                mo_ref[...] = m_new
                vo_ref[...] = v_new

    stk = BS((None, tr, cols), lambda l, i: (l, i, 0))
    flat = BS((tr, cols), lambda l, i: (i, 0))
    out = SDS((nl, rows, cols), F32)
    return pl.pallas_call(
        body, grid=(nl, rows // tr), in_specs=[stk] * 3 + [flat] * nl, out_specs=[stk] * 4,
        out_shape=[out] * 4, compiler_params=_cp("parallel", "parallel"), name=name,
    )(w, m, v, *g_list)


WEIGHTS = ["mem_norm_g", "a_pre_mix_g", "a_post_mix_g", "a_pre_ffn_g", "a_post_ffn_g", "a_w_in", "a_conv_w",
           "a_conv_b", "a_gate_a_w", "a_gate_a_b", "a_gate_x_w", "a_gate_x_b", "a_lambda", "a_w_mem_kv", "a_w_out",
           "a_w_ffn_in", "a_w_ffn_out", "kv_norm_g", "w_kv_shared", "b_pre_mix_g", "b_post_mix_g", "b_pre_ffn_g",
           "b_post_ffn_g", "b_w_in", "b_w_mem_kv", "b_w_out", "b_w_ffn_in", "b_w_ffn_out"]
BIG = {"a_w_in": True, "a_w_mem_kv": False, "a_w_out": False, "a_w_ffn_in": True, "a_w_ffn_out": False,
       "w_kv_shared": True, "b_w_in": False, "b_w_mem_kv": False, "b_w_out": False, "b_w_ffn_in": True,
       "b_w_ffn_out": False}
SHARDED_SMALL = ["a_pre_mix_g", "a_post_mix_g", "a_pre_ffn_g", "a_post_ffn_g", "a_conv_w", "a_conv_b", "a_gate_a_b",
                 "a_gate_x_b", "a_lambda"]
REPL_SMALL = ["mem_norm_g", "kv_norm_g", "b_pre_mix_g", "b_post_mix_g", "b_pre_ffn_g", "b_post_ffn_g", "a_gate_a_w",
              "a_gate_x_w"]
LANES = 128


def _pack(arrs, row_multiple=8):
    flat = jnp.concatenate([a.reshape(-1) for a in arrs])
    pad = -flat.shape[0] % (LANES * row_multiple)
    if pad:
        flat = jnp.concatenate([flat, jnp.zeros((pad,), flat.dtype)])
    return flat.reshape(-1, LANES)


def _unpack(packed, shapes):
    flat = packed.reshape(-1)
    out, pos = [], 0
    for sh in shapes:
        size = math.prod(sh)
        out.append(flat[pos:pos + size].reshape(sh))
        pos += size
    return out


def kernel(x, mem, mem_norm_g, a_pre_mix_g, a_post_mix_g, a_pre_ffn_g, a_post_ffn_g, a_w_in, a_conv_w, a_conv_b,
           a_gate_a_w, a_gate_a_b, a_gate_x_w, a_gate_x_b, a_lambda, a_w_mem_kv, a_w_out, a_w_ffn_in, a_w_ffn_out,
           kv_norm_g, w_kv_shared, b_pre_mix_g, b_post_mix_g, b_pre_ffn_g, b_post_ffn_g, b_w_in, b_w_mem_kv, b_w_out,
           b_w_ffn_in, b_w_ffn_out, loss_target, m_mem_norm_g, m_a_pre_mix_g, m_a_post_mix_g, m_a_pre_ffn_g,
           m_a_post_ffn_g, m_a_w_in, m_a_conv_w, m_a_conv_b, m_a_gate_a_w, m_a_gate_a_b, m_a_gate_x_w, m_a_gate_x_b,
           m_a_lambda, m_a_w_mem_kv, m_a_w_out, m_a_w_ffn_in, m_a_w_ffn_out, m_kv_norm_g, m_w_kv_shared, m_b_pre_mix_g,
           m_b_post_mix_g, m_b_pre_ffn_g, m_b_post_ffn_g, m_b_w_in, m_b_w_mem_kv, m_b_w_out, m_b_w_ffn_in, m_b_w_ffn_out,
           v_mem_norm_g, v_a_pre_mix_g, v_a_post_mix_g, v_a_pre_ffn_g, v_a_post_ffn_g, v_a_w_in, v_a_conv_w, v_a_conv_b,
           v_a_gate_a_w, v_a_gate_a_b, v_a_gate_x_w, v_a_gate_x_b, v_a_lambda, v_a_w_mem_kv, v_a_w_out, v_a_w_ffn_in,
           v_a_w_ffn_out, v_kv_norm_g, v_w_kv_shared, v_b_pre_mix_g, v_b_post_mix_g, v_b_pre_ffn_g, v_b_post_ffn_g,
           v_b_w_in, v_b_w_mem_kv, v_b_w_out, v_b_w_ffn_in, v_b_w_ffn_out):
    a = dict(locals())
    xi, yi, ci = _coords()
    chip = 2 * xi + yi
    c_arr = jnp.stack([ci, chip]).astype(jnp.int32)

    got = small_gather(_pack([a[n] for n in SHARDED_SMALL]), "small_gather")
    per_chip = [_unpack(got[s], [a[n].shape for n in SHARDED_SMALL]) for s in range(N_CHIPS)]
    small = {n: jnp.concatenate([per_chip[s][k] for s in range(N_CHIPS)], axis=-1)
             for k, n in enumerate(SHARDED_SMALL)}
    small.update({n: a[n] for n in REPL_SMALL})

    groups = []
    for l in range(4):
        p, j = ("a", l) if l < 2 else ("b", l - 2)
        groups.append([(p + "_" + n, j) for n in ("w_in", "w_mem_kv", "w_out")])
        groups.append([(p + "_" + n, j) for n in ("w_ffn_in", "w_ffn_out")])
    groups[3].append(("w_kv_shared", None))
    big = {n: [None, None] for n in BIG if n != "w_kv_shared"}
    gs, gb = {}, {}
    reduced = {n: [None, None] for n in BIG if n != "w_kv_shared"}

    def put(store, n, j, val):
        if j is None:
            store[n] = val
        else:
            store[n][j] = val

    class Exchange:
        def __init__(self):
            self.state = {}

        def gather_ici(self, g):
            shards = [(a[n] if j is None else a[n][j]).astype(MXU) for n, j in groups[g]]
            rows = [sh.shape[0] for sh in shards]
            lands = [lax.empty((N_CHIPS,) + sh.shape, sh.dtype) for sh in shards]
            plan = plan_gather_ici(len(shards), rows)
            ss, rs, bufs, tok = split_start("gather_ici_%d" % g, shards + lands, plan, 4 * len(shards))
            self.state["g", g] = (ss, rs, bufs, plan, rows)
            return tok

        def gather_d2d(self, g, after):
            ss, rs, bufs, plan, rows = self.state.pop(("g", g))
            n = len(rows)
            outs = split_wait("gather_ici_wait_%d" % g, ss, rs, bufs, after, plan)[n:]
            plan = plan_gather_d2d(n, rows)
            ss, rs, bufs, tok = split_start("gather_d2d_%d" % g, outs, plan, 3 * n)
            self.state["g", g] = (ss, rs, bufs, plan)
            return tok

        def gather_done(self, g, after):
            ss, rs, bufs, plan = self.state.pop(("g", g))
            outs = split_wait("gather_d2d_wait_%d" % g, ss, rs, bufs, after, plan)
            for (n, j), w in zip(groups[g], outs):
                put(big, n, j, w if BIG[n] else w.reshape(-1, w.shape[-1]))

        def rs_swap(self, g):
            grads = []
            for n, j in groups[g]:
                gr = gb[n] if j is None else gb[n][j]
                grads.append(gr if BIG[n] else gr.reshape(N_CHIPS, gr.shape[0] // N_CHIPS, gr.shape[1]))
            rows = [gr.shape[1] for gr in grads]
            lands = [lax.empty((N_CHIPS, gr.shape[1] // 2, gr.shape[2]), gr.dtype) for gr in grads]
            plan = plan_swap(len(grads), rows)
            ss, rs, bufs, tok = split_start("rs_swap_%d" % g, grads + lands, plan, len(grads))
            self.state["r", g] = (ss, rs, bufs, plan, rows)
            return tok

        def rs_exchange(self, g, after):
            ss, rs, bufs, plan, rows = self.state.pop(("r", g))
            n = len(rows)
            bufs = split_wait("rs_swap_wait_%d" % g, ss, rs, bufs, after, plan)
            sums = [half_sum(gr, got, c_arr, "rs_half_sum") for gr, got in zip(bufs[:n], bufs[n:])]
            plan = plan_exchange(n)
            ss, rs, bufs, tok = split_start("rs_exchange_%d" % g, [p for p, _ in sums] + [s for _, s in sums], plan,
                                            3 * n)
            self.state["r", g] = (ss, rs, bufs, plan, rows)
            return tok

        def rs_share(self, g, after):
            ss, rs, bufs, plan, rows = self.state.pop(("r", g))
            n = len(rows)
            slots = split_wait("rs_exchange_wait_%d" % g, ss, rs, bufs, after, plan)[n:]
            fulls = [slot_sum(s, c_arr, "rs_slot_sum") for s in slots]
            plan = plan_share(n, rows)
            ss, rs, bufs, tok = split_start("rs_share_%d" % g, fulls, plan, n)
            self.state["r", g] = (ss, rs, bufs, plan)
            return tok

        def rs_done(self, g, after):
            ss, rs, bufs, plan = self.state.pop(("r", g))
            outs = split_wait("rs_share_wait_%d" % g, ss, rs, bufs, after, plan)
            for (n, j), r in zip(groups[g], outs):
                put(reduced, n, j, r)

        def hook(self, where, l, after):
            mix, ffn = 2 * l, 2 * l + 1
            toks = []
            if where == "fwd_begin":
                if l == 0:
                    tok = self.gather_ici(mix)
                    tok = self.gather_d2d(mix, tok)
                    self.gather_done(mix, tok)
                toks.append(self.gather_ici(ffn))
            elif where == "fwd_q1":
                toks.append(self.gather_d2d(ffn, after))
            elif where == "fwd_mid":
                self.gather_done(ffn, after)
                if l < 3:
                    toks.append(self.gather_ici(mix + 2))
            elif where == "fwd_q3":
                if l < 3:
                    toks.append(self.gather_d2d(mix + 2, after))
            elif where == "fwd_end":
                if l < 3:
                    self.gather_done(mix + 2, after)
            elif where == "bwd_begin":
                if l < 3:
                    self.rs_done(ffn + 2, after)
                    toks.append(self.rs_exchange(mix + 2, after))
            elif where == "bwd_mid1":
                if l < 3:
                    toks.append(self.rs_share(mix + 2, after))
            elif where == "bwd_mid2":
                if l < 3:
                    self.rs_done(mix + 2, after)
                toks.append(self.rs_swap(ffn))
            elif where == "bwd_m1":
                toks.append(self.rs_exchange(ffn, after))
            elif where == "bwd_end":
                toks.append(self.rs_share(ffn, after))
                toks.append(self.rs_swap(mix))
                if l == 0:
                    self.rs_done(ffn, toks[0])
                    tok = self.rs_exchange(mix, toks[1])
                    tok = self.rs_share(mix, adamw_big([n for n in BIG if n.startswith("b_")], tok))
                    self.rs_done(mix, tok)
                    toks = []
            else:
                raise ValueError(where)
            return toks

    res = {}

    def adamw_big(names, token=None):
        last = None
        for n in names:
            shape = a[n].shape
            rows, cols = shape[-2], shape[-1]
            stk = (-1, rows, cols)
            grads = reduced[n] if isinstance(reduced[n], list) else [reduced[n]]
            if token is not None:
                grads = [tie(grads[0], [token], "tie_adamw_" + n)] + grads[1:]
            outs = adamw(a[n].reshape(stk), grads, a["m_" + n].reshape(stk), a["v_" + n].reshape(stk), "adamw")
            res[n] = [o.reshape(shape) for o in outs]
            last = outs[1]
            token = last if token is not None else None
        return last

    loss_parts, dx = _fwd_bwd(x[0], mem[0], loss_target[0], small, big, gs, gb, Exchange())
    loss = lax.psum(jnp.sum(loss_parts) * (0.5 / D), ("x", "y", "c"))
    adamw_big([n for n in BIG if n not in res])

    def full(n):
        g = gs[n]
        return jnp.stack(g) if isinstance(g, list) else g

    order = SHARDED_SMALL + REPL_SMALL
    full_shapes = [full(n).shape for n in order]
    summed = _unpack(small_allreduce(_pack([full(n) for n in order]), "small_allreduce"), full_shapes)
    mine = []
    for n, g in zip(order, summed):
        if n in SHARDED_SMALL:
            width = a[n].shape[-1]
            g = lax.dynamic_slice_in_dim(g, chip * width, width, axis=g.ndim - 1)
        mine.append(g.reshape(a[n].shape))
    shapes = [a[n].shape for n in order]
    rm = 512
    outs = adamw(_pack([a[n] for n in order], rm)[None], [_pack(mine, rm)],
                 _pack([a["m_" + n] for n in order], rm)[None], _pack([a["v_" + n] for n in order], rm)[None],
                 "adamw_small")
    unpacked = [_unpack(o[0], shapes) for o in outs]
    for k, n in enumerate(order):
        res[n] = [u[k] for u in unpacked]

    return (loss, dx[None], *[res[n][0] for n in WEIGHTS], *[res[n][1] for n in WEIGHTS],
            *[res[n][2] for n in WEIGHTS], *[res[n][3] for n in WEIGHTS])
```

```python
import functools
import math

import jax
import jax.numpy as jnp
from jax import lax
from jax.experimental import pallas as pl
from jax.experimental.pallas import tpu as pltpu

D = 2048
HD = 128
MEM_W = 512
MEM_HEADS = 4
MIX_W = D - MEM_W
N_BLK = MIX_W // HD
D_FF = 5632
N_MEM = 256
RMS_EPS = 1e-6
NEG_INF = -1e30
LRU_C = 8.0
DIL_GROUPS = ((128, 1), (512, 4), (2048, 16))
Q_BLOCK = 128
SCALE = HD ** -0.5
N_CHIPS = 4
MXU_COLS = 256
ACC_CHUNK = 2 * MXU_COLS

ADAM_LR = 0.001
ADAM_B1 = 0.9
ADAM_B2 = 0.999
ADAM_EPS = 1e-08
ADAM_WD = 0.01
ADAM_STEP = 10

MXU = jnp.bfloat16
F32 = jnp.float32
VMEM_LIMIT_BYTES = 56 * 1024 * 1024

BS = pl.BlockSpec
SDS = jax.ShapeDtypeStruct
MESH = pl.DeviceIdType.MESH


def _cp(*sem):
    return pltpu.CompilerParams(dimension_semantics=sem or None, vmem_limit_bytes=VMEM_LIMIT_BYTES)


def _dot(a, b, dn=((1,), (0,))):
    return lax.dot_general(a, b, (dn, ((), ())), preferred_element_type=F32)


def _div(i, n):
    return lax.div(i, jnp.int32(n))


def _rem(i, n):
    return lax.rem(i, jnp.int32(n))


NN = ((1,), (0,))
NT = ((1,), (1,))
TN = ((0,), (0,))


def _sigmoid(z):
    return 0.5 * jnp.tanh(0.5 * z) + 0.5


def _log1p_pos(u):
    return jnp.where(u < 1e-2, u * (1.0 - u * (0.5 - u * (1.0 / 3.0))), jnp.log(1.0 + u))


def _neg_expm1(z):
    return jnp.where(z > -1e-2, -z * (1.0 + z * (0.5 + z * (1.0 / 6.0))), 1.0 - jnp.exp(z))


def _softplus(z):
    return jnp.maximum(z, 0.0) + _log1p_pos(jnp.exp(-jnp.abs(z)))


_GELU_C = math.sqrt(2.0 / math.pi)


def _gelu_and_grad(x):
    x2 = x * x
    t = jnp.tanh(_GELU_C * (x + 0.044715 * x * x2))
    g = 0.5 * x * (1.0 + t)
    dg = 0.5 * (1.0 + t) + 0.5 * x * (1.0 - t * t) * _GELU_C * (1.0 + 3.0 * 0.044715 * x2)
    return g, dg


def _row_tile(rows):
    return min(256, rows)


def norm_cast(x, g, name):
    rows = x.shape[0]
    tr = _row_tile(rows)

    def body(x_ref, g_ref, o_ref):
        xv = x_ref[...]
        r = lax.rsqrt(jnp.mean(xv * xv, axis=-1, keepdims=True) + RMS_EPS)
        o_ref[...] = (xv * r * g_ref[...]).astype(o_ref.dtype)

    return pl.pallas_call(
        body, grid=(rows // tr,),
        in_specs=[BS((tr, D), lambda i: (i, 0)), BS((1, D), lambda i: (0, 0))],
        out_specs=BS((tr, D), lambda i: (i, 0)),
        out_shape=SDS((rows, D), MXU), compiler_params=_cp("parallel"), name=name,
    )(x, g.reshape(1, D))


def resid_norm(h, y, g, name):
    rows = h.shape[0]
    tr = _row_tile(rows)

    def body(h_ref, y_ref, g_ref, o_ref):
        yv = y_ref[...]
        r = lax.rsqrt(jnp.mean(yv * yv, axis=-1, keepdims=True) + RMS_EPS)
        o_ref[...] = h_ref[...] + yv * r * g_ref[...]

    return pl.pallas_call(
        body, grid=(rows // tr,),
        in_specs=[BS((tr, D), lambda i: (i, 0)), BS((tr, D), lambda i: (i, 0)), BS((1, D), lambda i: (0, 0))],
        out_specs=BS((tr, D), lambda i: (i, 0)),
        out_shape=SDS((rows, D), F32), compiler_params=_cp("parallel"), name=name,
    )(h, y, g.reshape(1, D))


def resid_norm_next(h, y, g, g_next, name):
    rows = h.shape[0]
    tr = _row_tile(rows)

    def body(h_ref, y_ref, g_ref, gn_ref, o_ref, n_ref):
        yv = y_ref[...]
        r = lax.rsqrt(jnp.mean(yv * yv, axis=-1, keepdims=True) + RMS_EPS)
        hv = h_ref[...] + yv * r * g_ref[...]
        o_ref[...] = hv
        r2 = lax.rsqrt(jnp.mean(hv * hv, axis=-1, keepdims=True) + RMS_EPS)
        n_ref[...] = (hv * r2 * gn_ref[...]).astype(n_ref.dtype)

    row = BS((tr, D), lambda i: (i, 0))
    vec = BS((1, D), lambda i: (0, 0))
    return pl.pallas_call(
        body, grid=(rows // tr,), in_specs=[row, row, vec, vec], out_specs=[row, row],
        out_shape=[SDS((rows, D), F32), SDS((rows, D), MXU)], compiler_params=_cp("parallel"), name=name,
    )(h, y, g.reshape(1, D), g_next.reshape(1, D))


def _norm_bwd_rows(xv, gv, dyv):
    r = lax.rsqrt(jnp.mean(xv * xv, axis=-1, keepdims=True) + RMS_EPS)
    xhat = xv * r
    dxhat = dyv * gv
    dx = r * (dxhat - xhat * jnp.mean(dxhat * xhat, axis=-1, keepdims=True))
    return dx, jnp.sum(dyv * xhat, axis=0, keepdims=True)


def norm_bwd(x, g, dy, res, out_dtype, name, then=None):
    rows = x.shape[0]
    tr = _row_tile(rows)
    has_res = res is not None
    n_in = 3 + has_res + (2 if then else 0)

    def body(*refs):
        x_ref, g_ref, dy_ref = refs[:3]
        dx_ref, dg_ref = refs[n_in], refs[n_in + 1]
        dx, dg = _norm_bwd_rows(x_ref[...], g_ref[...], dy_ref[...].astype(F32))
        if has_res:
            dx = dx + refs[3][...]
        dx_ref[...] = dx.astype(dx_ref.dtype)
        first = pl.program_id(0) == 0

        @pl.when(first)
        def _():
            dg_ref[...] = jnp.zeros_like(dg_ref)

        dg_ref[...] += dg
        if then:
            x2_ref, g2_ref = refs[n_in - 2], refs[n_in - 1]
            dx2_ref, dg2_ref = refs[n_in + 2], refs[n_in + 3]
            dx2, dg2 = _norm_bwd_rows(x2_ref[...], g2_ref[...], dx)
            dx2_ref[...] = dx2.astype(dx2_ref.dtype)

            @pl.when(first)
            def _():
                dg2_ref[...] = jnp.zeros_like(dg2_ref)

            dg2_ref[...] += dg2

    row = BS((tr, D), lambda i: (i, 0))
    vec = BS((1, D), lambda i: (0, 0))
    ins = [x, g.reshape(1, D), dy] + ([res] if has_res else []) + ([then[0], then[1].reshape(1, D)] if then else [])
    outs = pl.pallas_call(
        body, grid=(rows // tr,),
        in_specs=[row, vec, row] + ([row] if has_res else []) + ([row, vec] if then else []),
        out_specs=[row, vec] + ([row, vec] if then else []),
        out_shape=[SDS((rows, D), out_dtype), SDS((1, D), F32)] + ([SDS((rows, D), MXU), SDS((1, D), F32)] if then else []),
        compiler_params=_cp("arbitrary"), name=name,
    )(*ins)
    if then:
        return outs[0], outs[1].reshape(D), outs[2], outs[3].reshape(D)
    return outs[0], outs[1].reshape(D)


def loss_head(y, target, name):
    rows = y.shape[0]
    tr = _row_tile(rows)

    def body(y_ref, t_ref, dy_ref, acc_ref):
        err = y_ref[...] - t_ref[...]
        dy_ref[...] = err * (1.0 / D)

        @pl.when(pl.program_id(0) == 0)
        def _():
            acc_ref[...] = jnp.zeros_like(acc_ref)

        acc_ref[...] += jnp.sum(err * err, axis=0, keepdims=True)

    row = BS((tr, D), lambda i: (i, 0))
    dy, acc = pl.pallas_call(
        body, grid=(rows // tr,), in_specs=[row, row],
        out_specs=[row, BS((1, D), lambda i: (0, 0))],
        out_shape=[SDS((rows, D), F32), SDS((1, D), F32)],
        compiler_params=_cp("arbitrary"), name=name,
    )(y, target)
    return acc, dy


def _mm_call(ins, in_specs, pick, dn, grid, o_spec, out_sds, name):
    gk = grid[2]
    n_in = len(ins)

    def body(*refs):
        o_ref = refs[n_in]
        k = pl.program_id(2)

        def step(a_ref, b_ref):
            acc = o_ref if (out_sds.dtype == F32 or gk == 1) else refs[n_in + 1]
            width = acc.shape[-1]
            if dn == TN or width <= ACC_CHUNK:
                chunks = [(0, width)]
            else:
                chunks = [(c0, min(c0 + ACC_CHUNK, width)) for c0 in range(0, width, ACC_CHUNK)]

            def sweep(first):
                a = a_ref[...]
                pending = None
                for c0, c1 in chunks:
                    p = _dot(a, b_ref[c0:c1, :] if dn == NT else b_ref[:, c0:c1], dn)
                    if pending is not None:
                        put(first, *pending)
                    pending = (c0, c1, p)
                put(first, *pending)

            def put(first, c0, c1, p):
                if first:
                    acc[:, c0:c1] = p.astype(acc.dtype)
                else:
                    acc[:, c0:c1] += p

            if gk == 1:
                sweep(True)
                return

            @pl.when(k == 0)
            def _():
                sweep(True)

            @pl.when(k > 0)
            def _():
                sweep(False)

            if acc is not o_ref:
                @pl.when(k == gk - 1)
                def _():
                    o_ref[...] = acc[...].astype(o_ref.dtype)

        pick(refs[:n_in], k, step)

    scratch = []
    if gk > 1 and out_sds.dtype != F32:
        scratch = [pltpu.VMEM(o_spec.block_shape[-2:], F32)]
    return pl.pallas_call(
        body, grid=grid, in_specs=in_specs, out_specs=o_spec, out_shape=out_sds,
        scratch_shapes=scratch, compiler_params=_cp("parallel", "parallel", "arbitrary"), name=name,
    )(*ins)


def _pick2(refs, k, step):
    step(refs[0], refs[1])


def mm_nn(a, w, *, tm, tn, tk, out_dtype, name):
    m, kdim = a.shape
    if w.ndim == 3:
        c = w.shape[2]
        n = N_CHIPS * c
        per = c // tn
        b_spec = BS((None, tk, tn), lambda i, j, k: (_div(j, per), k, _rem(j, per)))
    else:
        n = w.shape[1]
        b_spec = BS((tk, tn), lambda i, j, k: (k, j))
    grid = (m // tm, n // tn, kdim // tk)
    return _mm_call([a, w], [BS((tm, tk), lambda i, j, k: (i, k)), b_spec], _pick2, NN, grid,
                    BS((tm, tn), lambda i, j, k: (i, j)), SDS((m, n), out_dtype), name)


def mm_nt(a_list, w, *, tm, tn, tk, out_dtype, name):
    m = a_list[0].shape[0]
    ka = a_list[0].shape[1]
    n_a = len(a_list)
    kdim = ka * n_a
    if w.ndim == 3:
        c = w.shape[2]
        n = w.shape[1]
        per = c // tk
        b_spec = BS((None, tn, tk), lambda i, j, k: (_div(k, per), j, _rem(k, per)))
    else:
        n = w.shape[0]
        b_spec = BS((tn, tk), lambda i, j, k: (j, k))
    gk = kdim // tk
    half = gk // n_a
    grid = (m // tm, n // tn, gk)
    if n_a == 1:
        a_specs = [BS((tm, tk), lambda i, j, k: (i, k))]
        pick = lambda refs, k, step: step(refs[0], refs[1])
    else:
        a_specs = [BS((tm, tk), lambda i, j, k: (i, jnp.minimum(k, half - 1))),
                   BS((tm, tk), lambda i, j, k: (i, jnp.maximum(k - half, 0)))]

        def pick(refs, k, step):
            @pl.when(k < half)
            def _():
                step(refs[0], refs[2])

            @pl.when(k >= half)
            def _():
                step(refs[1], refs[2])

    return _mm_call(list(a_list) + [w], a_specs + [b_spec], pick, NT, grid,
                    BS((tm, tn), lambda i, j, k: (i, j)), SDS((m, n), out_dtype), name)


def mm_tn(a, b_list, *, t1, tn, ts, col_shards, name):
    s, k1 = a.shape
    nb = b_list[0].shape[1]
    n_b = len(b_list)
    n = nb * n_b
    gn = n // tn
    half = gn // n_b
    grid = (k1 // t1, gn, s // ts)
    if col_shards:
        c = n // N_CHIPS
        per = c // tn
        o_spec = BS((None, t1, tn), lambda i, j, k: (_div(j, per), i, _rem(j, per)))
        out_sds = SDS((N_CHIPS, k1, c), MXU)
    else:
        o_spec = BS((t1, tn), lambda i, j, k: (i, j))
        out_sds = SDS((k1, n), MXU)
    a_spec = BS((ts, t1), lambda i, j, k: (k, i))
    if n_b == 1:
        b_specs = [BS((ts, tn), lambda i, j, k: (k, j))]
        pick = lambda refs, k, step: step(refs[0], refs[1])
    else:
        b_specs = [BS((ts, tn), lambda i, j, k: (k, jnp.minimum(j, half - 1))),
                   BS((ts, tn), lambda i, j, k: (k, jnp.maximum(j - half, 0)))]

        def pick(refs, k, step):
            j = pl.program_id(1)

            @pl.when(j < half)
            def _():
                step(refs[0], refs[1])

            @pl.when(j >= half)
            def _():
                step(refs[0], refs[2])

    return _mm_call([a] + list(b_list), [a_spec] + b_specs, pick, TN, grid, o_spec, out_sds, name)


def ffn_in_fwd(hn, w, name):
    s = hn.shape[0]
    tm = min(512, s)
    tn = D_FF // 4

    def tail(dag_ref, dau_ref, act_ref, c0, c1, g, u):
        sg = _sigmoid(g)
        silu = g * sg
        dag_ref[:, c0:c1] = u * sg * (1.0 + g * (1.0 - sg))
        dau_ref[:, c0:c1] = silu
        act_ref[:, c0:c1] = (silu * u).astype(act_ref.dtype)

    def body(a_ref, wg_ref, wu_ref, dag_ref, dau_ref, act_ref):
        a = a_ref[...]
        pending = None
        for c0 in range(0, tn, ACC_CHUNK):
            c1 = min(c0 + ACC_CHUNK, tn)
            g = _dot(a, wg_ref[:, c0:c1])
            u = _dot(a, wu_ref[:, c0:c1])
            if pending is not None:
                tail(dag_ref, dau_ref, act_ref, *pending)
            pending = (c0, c1, g, u)
        tail(dag_ref, dau_ref, act_ref, *pending)

    tile = BS((tm, tn), lambda j, i: (i, j))
    return pl.pallas_call(
        body, grid=(4, s // tm),
        in_specs=[BS((tm, D), lambda j, i: (i, 0)),
                  BS((None, D, tn), lambda j, i: (_div(j, 2), 0, _rem(j, 2))),
                  BS((None, D, tn), lambda j, i: (2 + _div(j, 2), 0, _rem(j, 2)))],
        out_specs=[tile, tile, tile],
        out_shape=[SDS((s, D_FF), F32), SDS((s, D_FF), F32), SDS((s, D_FF), MXU)],
        compiler_params=_cp("parallel", "parallel"), name=name,
    )(hn, w, w)


def ffn_act_bwd(dy, w_out, dag, dau, name):
    s = dy.shape[0]
    tm = min(512, s)
    tn = D_FF // 4

    def body(dy_ref, w_ref, dag_ref, dau_ref, dg_ref, du_ref):
        def tail(c0, c1, dact):
            dg_ref[:, c0:c1] = (dact * dag_ref[:, c0:c1]).astype(dg_ref.dtype)
            du_ref[:, c0:c1] = (dact * dau_ref[:, c0:c1]).astype(du_ref.dtype)

        dy = dy_ref[...]
        pending = None
        for c0 in range(0, tn, ACC_CHUNK):
            c1 = min(c0 + ACC_CHUNK, tn)
            dact = _dot(dy, w_ref[c0:c1, :], NT)
            if pending is not None:
                tail(*pending)
            pending = (c0, c1, dact)
        tail(*pending)

    tile = BS((tm, tn), lambda j, i: (i, j))
    return pl.pallas_call(
        body, grid=(4, s // tm),
        in_specs=[BS((tm, D), lambda j, i: (i, 0)), BS((tn, D), lambda j, i: (j, 0)), tile, tile],
        out_specs=[tile, tile],
        out_shape=[SDS((s, D_FF), MXU), SDS((s, D_FF), MXU)],
        compiler_params=_cp("parallel", "parallel"), name=name,
    )(dy, w_out, dag, dau)


LRU_T = 256
HALO = 8


def _shift_down(x, k, fill):
    rows = x.shape[0]
    idx = lax.broadcasted_iota(jnp.int32, x.shape, 0)
    return jnp.where(idx < k, fill, pltpu.roll(x, k, 0))


def _shift_up(x, k, fill):
    rows = x.shape[0]
    idx = lax.broadcasted_iota(jnp.int32, x.shape, 0)
    return jnp.where(idx >= rows - k, fill, pltpu.roll(x, rows - k, 0))


def _scan_block(a, b, carry, reverse):
    rows, cols = a.shape
    sub = 8
    in_group = lax.broadcasted_iota(jnp.int32, a.shape, 0) % sub
    for sh in (1, 2, 4):
        if reverse:
            a_s, b_s, ok = pltpu.roll(a, rows - sh, 0), pltpu.roll(b, rows - sh, 0), in_group < sub - sh
        else:
            a_s, b_s, ok = pltpu.roll(a, sh, 0), pltpu.roll(b, sh, 0), in_group >= sh
        b = jnp.where(ok, a * b_s + b, b)
        a = jnp.where(ok, a * a_s, a)
    groups = list(range(rows // sub))
    edge = 0 if reverse else sub - 1
    carry_in = {}
    for v in (reversed(groups) if reverse else groups):
        carry_in[v] = carry
        row = sub * v + edge
        carry = b[row:row + 1, :] + a[row:row + 1, :] * carry
    cin = jnp.concatenate([jnp.broadcast_to(carry_in[v], (sub, cols)) for v in groups], axis=0)
    return b + a * cin


def _conv_taps(xcat):
    rows = xcat.shape[0]
    taps = []
    for k in range(4):
        off = HALO - 3 + k
        taps.append(xcat[off:off + LRU_T] if off == HALO else pltpu.roll(xcat, rows - off, 0)[:LRU_T])
    return taps


def _gates(xc, wa_ref, ba, wx_ref, bx, lam, za_ref, zx_ref):
    xm = xc.astype(MXU)
    for n in range(N_BLK):
        sl = slice(n * HD, (n + 1) * HD)
        za_ref[:, sl] = _dot(xm[:, sl], wa_ref[n])
        zx_ref[:, sl] = _dot(xm[:, sl], wx_ref[n])
    ra = _sigmoid(za_ref[...] + ba)
    ii = _sigmoid(zx_ref[...] + bx)
    sp = _softplus(-lam)
    log_a = -LRU_C * ra * sp
    a = jnp.exp(log_a)
    mult = jnp.sqrt(_neg_expm1(2.0 * log_a))
    return ra, ii, sp, a, mult


def lru_fwd(proj, conv_w, conv_b, wa, ba, wx, bx, lam, name):
    s = proj.shape[0]
    c = MIX_W
    nblk = s // LRU_T
    hpb = LRU_T // HALO

    def body(x_ref, halo_ref, cw_ref, cb_ref, wa_ref, ba_ref, wx_ref, bx_ref, lam_ref,
             xc_ref, h_ref, carry, za_ref, zx_ref):
        i = pl.program_id(0)

        @pl.when(i == 0)
        def _():
            carry[...] = jnp.zeros_like(carry)

        halo = jnp.where(i == 0, 0.0, halo_ref[...])
        xcat = jnp.concatenate([halo, x_ref[...]], axis=0)
        taps = _conv_taps(xcat)
        xc = cb_ref[...] + sum(cw_ref[k:k + 1, :] * taps[k] for k in range(4))
        xc_ref[...] = xc
        _, ii, _, a, mult = _gates(xc, wa_ref, ba_ref[...], wx_ref, bx_ref[...], lam_ref[...], za_ref, zx_ref)
        h = _scan_block(a, mult * (ii * xc), carry[HALO - 1:HALO, :], False)
        h_ref[...] = h
        carry[...] = h[LRU_T - HALO:, :]

    def full(shape):
        return BS(shape, lambda i: (0,) * len(shape))

    blk = BS((LRU_T, c), lambda i: (i, 0))
    return pl.pallas_call(
        body, grid=(nblk,),
        in_specs=[blk, BS((HALO, c), lambda i: (jnp.maximum(i * hpb - 1, 0), 0)),
                  full((4, c)), full((1, c)), full((N_BLK, HD, HD)), full((1, c)),
                  full((N_BLK, HD, HD)), full((1, c)), full((1, c))],
        out_specs=[blk, blk],
        out_shape=[SDS((s, c), F32), SDS((s, c), F32)],
        scratch_shapes=[pltpu.VMEM((HALO, c), F32), pltpu.VMEM((LRU_T, c), F32), pltpu.VMEM((LRU_T, c), F32)],
        compiler_params=_cp("arbitrary"), name=name,
    )(proj, proj, conv_w, conv_b.reshape(1, c), wa.astype(MXU), ba.reshape(1, c), wx.astype(MXU),
      bx.reshape(1, c), lam.reshape(1, c))


def lru_mix_prep(h, proj, m, name):
    s = h.shape[0]
    tr = _row_tile(s)

    def body(h_ref, gb_ref, m_ref, o_ref):
        ge, _ = _gelu_and_grad(gb_ref[...])
        o_ref[:, :MIX_W] = (h_ref[...] * ge).astype(o_ref.dtype)
        o_ref[:, MIX_W:] = m_ref[...]

    return pl.pallas_call(
        body, grid=(s // tr,),
        in_specs=[BS((tr, MIX_W), lambda i: (i, 0)), BS((tr, MIX_W), lambda i: (i, 1)),
                  BS((tr, MEM_W), lambda i: (i, 0))],
        out_specs=BS((tr, D), lambda i: (i, 0)), out_shape=SDS((s, D), MXU),
        compiler_params=_cp("parallel"), name=name,
    )(h, proj, m)


def lru_bwd(dym, proj, xc, hl, dqm, conv_w, wa, ba, wx, bx, lam, name):
    s = proj.shape[0]
    c = MIX_W
    nblk = s // LRU_T
    hpb = LRU_T // HALO
    wa_m = wa.astype(MXU)
    wx_m = wx.astype(MXU)

    def body(dy_ref, x_ref, xhalo_ref, gb_ref, xc_ref, h_ref, hhalo_ref, dqm_ref,
             cw_ref, wa_ref, ba_ref, wx_ref, bx_ref, lam_ref,
             dproj_ref, dcw_ref, dcb_ref, dwa_ref, dba_ref, dwx_ref, dbx_ref, dlam_ref,
             g_next, a_next, dxc_next, za_ref, zx_ref, dxc_ref):
        i = pl.program_id(0)

        @pl.when(i == 0)
        def _():
            g_next[...] = jnp.zeros_like(g_next)
            a_next[...] = jnp.zeros_like(a_next)
            dxc_next[...] = jnp.zeros_like(dxc_next)
            for r in (dcw_ref, dcb_ref, dwa_ref, dba_ref, dwx_ref, dbx_ref, dlam_ref):
                r[...] = jnp.zeros_like(r)

        first = i == nblk - 1
        xc = xc_ref[...]
        lam = lam_ref[...]
        ra, ii, sp, a, mult = _gates(xc, wa_ref, ba_ref[...], wx_ref, bx_ref[...], lam, za_ref, zx_ref)
        hl_v = h_ref[...]
        ge, dge = _gelu_and_grad(gb_ref[...])
        dyl = dy_ref[...]
        dhl = dyl * ge
        dproj_ref[:, c:2 * c] = (dyl * hl_v * dge).astype(dproj_ref.dtype)
        dproj_ref[:, 2 * c:] = dqm_ref[...]

        an = _shift_up(a, 1, 0.0)
        last_row = lax.broadcasted_iota(jnp.int32, a.shape, 0) == LRU_T - 1
        an = jnp.where(last_row, a_next[0:1, :], an)
        g = _scan_block(an, dhl, g_next[0:1, :], True)
        g_next[...] = g[:HALO, :]
        a_next[...] = a[:HALO, :]

        hhalo = jnp.where(first, 0.0, hhalo_ref[...])
        h_prev = _shift_down(hl_v, 1, 0.0)
        first_row = lax.broadcasted_iota(jnp.int32, a.shape, 0) == 0
        h_prev = jnp.where(first_row, hhalo[HALO - 1:HALO, :], h_prev)
        da = g * h_prev
        ixc = ii * xc
        dmult = g * ixc
        dii = g * mult * xc
        dxc = g * mult * ii
        dlog_a = (da - dmult * a / mult) * a
        dra = dlog_a * (-LRU_C) * sp
        dlam_ref[...] += jnp.sum(dlog_a * ra, axis=0, keepdims=True) * (LRU_C * _sigmoid(-lam))
        dza = dra * ra * (1.0 - ra)
        dzx = dii * ii * (1.0 - ii)
        dba_ref[...] += jnp.sum(dza, axis=0, keepdims=True)
        dbx_ref[...] += jnp.sum(dzx, axis=0, keepdims=True)
        xm = xc.astype(MXU)
        dza_m = dza.astype(MXU)
        dzx_m = dzx.astype(MXU)
        for n in range(N_BLK):
            sl = slice(n * HD, (n + 1) * HD)
            dwa_ref[n] += _dot(xm[:, sl], dza_m[:, sl], TN)
            dwx_ref[n] += _dot(xm[:, sl], dzx_m[:, sl], TN)
            dxc_ref[:, sl] = _dot(dza_m[:, sl], wa_ref[n], NT) + _dot(dzx_m[:, sl], wx_ref[n], NT)
        dxc = dxc + dxc_ref[...]

        dcat = jnp.concatenate([dxc, dxc_next[...]], axis=0)
        rows = dcat.shape[0]
        dxb = cw_ref[3:4, :] * dxc
        for k in range(3):
            dxb = dxb + cw_ref[k:k + 1, :] * pltpu.roll(dcat, rows - (3 - k), 0)[:LRU_T]
        dproj_ref[:, :c] = dxb.astype(dproj_ref.dtype)
        dxc_next[...] = dxc[:HALO, :]

        xhalo = jnp.where(first, 0.0, xhalo_ref[...])
        taps = _conv_taps(jnp.concatenate([xhalo, x_ref[...]], axis=0))
        for k in range(4):
            dcw_ref[k:k + 1, :] += jnp.sum(dxc * taps[k], axis=0, keepdims=True)
        dcb_ref[...] += jnp.sum(dxc, axis=0, keepdims=True)

    def full(shape):
        return BS(shape, lambda i: (0,) * len(shape))

    def rev(i):
        return nblk - 1 - i

    blk0 = BS((LRU_T, c), lambda i: (rev(i), 0))
    blk1 = BS((LRU_T, c), lambda i: (rev(i), 1))
    halo = BS((HALO, c), lambda i: (jnp.maximum(rev(i) * hpb - 1, 0), 0))
    outs = pl.pallas_call(
        body, grid=(nblk,),
        in_specs=[blk0, blk0, halo, blk1, blk0, blk0, halo, BS((LRU_T, MEM_W), lambda i: (rev(i), 0)),
                  full((4, c)), full((N_BLK, HD, HD)), full((1, c)), full((N_BLK, HD, HD)), full((1, c)),
                  full((1, c))],
        out_specs=[BS((LRU_T, 2 * c + MEM_W), lambda i: (rev(i), 0)), full((4, c)), full((1, c)),
                   full((N_BLK, HD, HD)), full((1, c)), full((N_BLK, HD, HD)), full((1, c)), full((1, c))],
        out_shape=[SDS((s, 2 * c + MEM_W), MXU), SDS((4, c), F32), SDS((1, c), F32),
                   SDS((N_BLK, HD, HD), F32), SDS((1, c), F32), SDS((N_BLK, HD, HD), F32), SDS((1, c), F32),
                   SDS((1, c), F32)],
        scratch_shapes=[pltpu.VMEM((HALO, c), F32), pltpu.VMEM((HALO, c), F32), pltpu.VMEM((HALO, c), F32),
                        pltpu.VMEM((LRU_T, c), F32), pltpu.VMEM((LRU_T, c), F32), pltpu.VMEM((LRU_T, c), F32)],
        compiler_params=_cp("arbitrary"), name=name,
    )(dym, proj, proj, proj, xc, hl, hl, dqm, conv_w, wa_m, ba.reshape(1, c), wx_m, bx.reshape(1, c),
      lam.reshape(1, c))
    dproj, dcw, dcb, dwa, dba, dwx, dbx, dlam = outs
    return dproj, dcw, dcb.reshape(c), dwa, dba.reshape(c), dwx, dbx.reshape(c), dlam.reshape(c)


def _mem_probs(q, kv):
    heads = [slice(hh * HD, (hh + 1) * HD) for hh in range(MEM_HEADS)]
    sc = [_dot(q[:, sl], kv[:, sl], NT) * SCALE for sl in heads]
    e = [jnp.exp(s - jnp.max(s, axis=-1, keepdims=True)) for s in sc]
    return [x / jnp.sum(x, axis=-1, keepdims=True) for x in e]


def mem_attn_fwd(proj, q_col, kvm, name):
    s = proj.shape[0]
    tq = min(512, s)

    def body(q_ref, kv_ref, o_ref):
        q = q_ref[...].astype(MXU)
        kv = kv_ref[...]
        p = _mem_probs(q, kv)
        outs = [_dot(p[hh].astype(MXU), kv[:, MEM_W + hh * HD:MEM_W + (hh + 1) * HD]) for hh in range(MEM_HEADS)]
        o_ref[...] = jnp.concatenate(outs, axis=1).astype(o_ref.dtype)

    return pl.pallas_call(
        body, grid=(s // tq,),
        in_specs=[BS((tq, MEM_W), lambda i: (i, q_col)), BS((N_MEM, 2 * MEM_W), lambda i: (0, 0))],
        out_specs=BS((tq, MEM_W), lambda i: (i, 0)), out_shape=SDS((s, MEM_W), MXU),
        compiler_params=_cp("parallel"), name=name,
    )(proj, kvm)


def mem_attn_bwd(proj, q_col, kvm, dym, name):
    s = proj.shape[0]
    tq = min(512, s)

    def body(q_ref, kv_ref, do_ref, dq_ref, dkv_ref):
        @pl.when(pl.program_id(0) == 0)
        def _():
            dkv_ref[...] = jnp.zeros_like(dkv_ref)

        q = q_ref[...].astype(MXU)
        do = do_ref[...].astype(MXU)
        kv = kv_ref[...]
        heads = [slice(hh * HD, (hh + 1) * HD) for hh in range(MEM_HEADS)]
        p = _mem_probs(q, kv)
        dp = [_dot(do[:, sl], kv[:, MEM_W + hh * HD:MEM_W + (hh + 1) * HD], NT) for hh, sl in enumerate(heads)]
        ds = [(pp * (d - jnp.sum(pp * d, axis=-1, keepdims=True)) * SCALE).astype(MXU) for pp, d in zip(p, dp)]
        dq = [_dot(x, kv[:, sl]) for x, sl in zip(ds, heads)]
        dk = [_dot(x, q[:, sl], TN) for x, sl in zip(ds, heads)]
        dv = [_dot(pp.astype(MXU), do[:, sl], TN) for pp, sl in zip(p, heads)]
        dq_ref[...] = jnp.concatenate(dq, axis=1).astype(dq_ref.dtype)
        dkv_ref[...] += jnp.concatenate(dk + dv, axis=1)

    return pl.pallas_call(
        body, grid=(s // tq,),
        in_specs=[BS((tq, MEM_W), lambda i: (i, q_col)), BS((N_MEM, 2 * MEM_W), lambda i: (0, 0)),
                  BS((tq, MEM_W), lambda i: (i, MIX_W // MEM_W))],
        out_specs=[BS((tq, MEM_W), lambda i: (i, 0)), BS((N_MEM, 2 * MEM_W), lambda i: (0, 0))],
        out_shape=[SDS((s, MEM_W), MXU), SDS((N_MEM, 2 * MEM_W), F32)],
        compiler_params=_cp("arbitrary"), name=name,
    )(proj, kvm, dym)


def _dil_scores(q, kp, kc, n, slope_dil):
    qi = lax.broadcasted_iota(jnp.int32, (Q_BLOCK, Q_BLOCK), 0)
    ki = lax.broadcasted_iota(jnp.int32, (Q_BLOCK, Q_BLOCK), 1)
    rel_p = qi + Q_BLOCK - ki
    rel_c = qi - ki
    s_p = _dot(q, kp, NT) * SCALE - slope_dil * rel_p.astype(F32)
    s_c = _dot(q, kc, NT) * SCALE - slope_dil * rel_c.astype(F32)
    s_p = jnp.where((rel_p <= Q_BLOCK) & (n > 0), s_p, NEG_INF)
    s_c = jnp.where(rel_c >= 0, s_c, NEG_INF)
    return s_p, s_c


def _slope_dil(gi, hh):
    head = 4 * gi + hh
    return DIL_GROUPS[gi][1] * 2.0 ** (-8.0 * (head + 1.0) / N_BLK)


def _dil_operands(proj, kv, gi):
    dil = DIL_GROUPS[gi][1]
    if dil == 1:
        return proj, kv, kv, (lambda r: gi), (lambda r: gi), (lambda r: MIX_W // MEM_W + gi)
    sub = proj.shape[0] // dil

    def view(a, col):
        return a[:, col:col + MEM_W].reshape(sub, dil * MEM_W)

    same = lambda r: r
    return view(proj, gi * MEM_W), view(kv, gi * MEM_W), view(kv, MIX_W + gi * MEM_W), same, same, same


def dil_attn_fwd(proj, kv, gi, name):
    dil = DIL_GROUPS[gi][1]
    s, pw = proj.shape
    sub = s // dil
    nb = sub // Q_BLOCK
    qc, kc_ = pw // MEM_W, kv.shape[1] // MEM_W

    def body(q_ref, kp_ref, kc_ref, vp_ref, vc_ref, o_ref, lse_ref):
        n = pl.program_id(1)
        q = q_ref[...].astype(MXU)
        kp, kc, vp, vc = kp_ref[...], kc_ref[...], vp_ref[...], vc_ref[...]
        heads = [slice(hh * HD, (hh + 1) * HD) for hh in range(4)]
        sc = [_dil_scores(q[:, sl], kp[:, sl], kc[:, sl], n, _slope_dil(gi, hh)) for hh, sl in enumerate(heads)]
        mx = [jnp.maximum(jnp.max(s_p, axis=-1, keepdims=True), jnp.max(s_c, axis=-1, keepdims=True))
              for s_p, s_c in sc]
        den = [jnp.sum(jnp.exp(s_p - m), axis=-1, keepdims=True) + jnp.sum(jnp.exp(s_c - m), axis=-1, keepdims=True)
               for (s_p, s_c), m in zip(sc, mx)]
        lse = [m + jnp.log(d) for m, d in zip(mx, den)]
        pr = [(jnp.exp(s_p - l).astype(MXU), jnp.exp(s_c - l).astype(MXU)) for (s_p, s_c), l in zip(sc, lse)]
        outs = [_dot(p_p, vp[:, sl]) + _dot(p_c, vc[:, sl]) for (p_p, p_c), sl in zip(pr, heads)]
        o_ref[...] = jnp.concatenate(outs, axis=1)
        lse_ref[...] = jnp.concatenate([jnp.broadcast_to(l, (Q_BLOCK, HD)) for l in lse], axis=1)

    blk = (Q_BLOCK, MEM_W)
    prev = lambda n: jnp.maximum(n - 1, 0)
    out = BS(blk, lambda r, n: (n, r))
    qv, kview, vview, qcol, kcol, vcol = _dil_operands(proj, kv, gi)
    return pl.pallas_call(
        body, grid=(dil, nb),
        in_specs=[BS(blk, lambda r, n: (n, qcol(r))),
                  BS(blk, lambda r, n: (prev(n), kcol(r))), BS(blk, lambda r, n: (n, kcol(r))),
                  BS(blk, lambda r, n: (prev(n), vcol(r))), BS(blk, lambda r, n: (n, vcol(r)))],
        out_specs=[out, out],
        out_shape=[SDS((sub, dil * MEM_W), F32), SDS((sub, dil * MEM_W), F32)],
        compiler_params=_cp("parallel", "parallel"), name=name,
    )(qv, kview, kview, vview, vview)


def dil_attn_bwd(proj, kv, lse, do, dd, gi, name):
    dil = DIL_GROUPS[gi][1]
    s, pw = proj.shape
    sub = s // dil
    nb = sub // Q_BLOCK
    qc, kc_ = pw // MEM_W, kv.shape[1] // MEM_W

    def body(q_ref, kp_ref, kc_ref, vp_ref, vc_ref, lse_ref, do_ref, dd_ref, dq_ref, dk_ref, dv_ref, ck, cv):
        n = pl.program_id(1)

        @pl.when(n == 0)
        def _():
            ck[...] = jnp.zeros_like(ck)
            cv[...] = jnp.zeros_like(cv)

        @pl.when(n < nb)
        def _():
            q = q_ref[...].astype(MXU)
            do_m = do_ref[...].astype(MXU)
            kp, kc, vp, vc = kp_ref[...], kc_ref[...], vp_ref[...], vc_ref[...]
            lse_v, dd_v, ck_v, cv_v = lse_ref[...], dd_ref[...], ck[...], cv[...]
            heads = [slice(hh * HD, (hh + 1) * HD) for hh in range(4)]
            sc = [_dil_scores(q[:, sl], kp[:, sl], kc[:, sl], n, _slope_dil(gi, hh)) for hh, sl in enumerate(heads)]
            dp = [(_dot(do_m[:, sl], vp[:, sl], NT), _dot(do_m[:, sl], vc[:, sl], NT)) for sl in heads]
            pr = [(jnp.exp(s_p - lse_v[:, sl]), jnp.exp(s_c - lse_v[:, sl])) for (s_p, s_c), sl in zip(sc, heads)]
            ds = [((p_p * (dp_p + dd_v[:, sl]) * SCALE).astype(MXU), (p_c * (dp_c + dd_v[:, sl]) * SCALE).astype(MXU))
                  for (p_p, p_c), (dp_p, dp_c), sl in zip(pr, dp, heads)]
            pm = [(p_p.astype(MXU), p_c.astype(MXU)) for p_p, p_c in pr]
            dq = [_dot(ds_p, kp[:, sl]) + _dot(ds_c, kc[:, sl]) for (ds_p, ds_c), sl in zip(ds, heads)]
            dk = [ck_v[:, sl] + _dot(ds_p, q[:, sl], TN) for (ds_p, _), sl in zip(ds, heads)]
            dv = [cv_v[:, sl] + _dot(p_p, do_m[:, sl], TN) for (p_p, _), sl in zip(pm, heads)]
            ck_new = [_dot(ds_c, q[:, sl], TN) for (_, ds_c), sl in zip(ds, heads)]
            cv_new = [_dot(p_c, do_m[:, sl], TN) for (_, p_c), sl in zip(pm, heads)]
            dq_ref[...] = jnp.concatenate(dq, axis=1).astype(dq_ref.dtype)
            dk_ref[...] = jnp.concatenate(dk, axis=1)
            dv_ref[...] = jnp.concatenate(dv, axis=1)
            ck[...] = jnp.concatenate(ck_new, axis=1)
            cv[...] = jnp.concatenate(cv_new, axis=1)

        @pl.when(n == nb)
        def _():
            dk_ref[...] = ck[...]
            dv_ref[...] = cv[...]

    blk = (Q_BLOCK, MEM_W)
    cur = lambda n: jnp.minimum(n, nb - 1)
    prev = lambda n: jnp.maximum(jnp.minimum(n, nb - 1) - 1, 0)
    done = lambda n: jnp.maximum(n - 1, 0)
    own = BS(blk, lambda r, n: (cur(n), r))
    qv, kview, vview, qcol, kcol, vcol = _dil_operands(proj, kv, gi)
    return pl.pallas_call(
        body, grid=(dil, nb + 1),
        in_specs=[BS(blk, lambda r, n: (cur(n), qcol(r))),
                  BS(blk, lambda r, n: (prev(n), kcol(r))), BS(blk, lambda r, n: (cur(n), kcol(r))),
                  BS(blk, lambda r, n: (prev(n), vcol(r))), BS(blk, lambda r, n: (cur(n), vcol(r))),
                  own, own, own],
        out_specs=[own, BS(blk, lambda r, n: (done(n), r)), BS(blk, lambda r, n: (done(n), r))],
        out_shape=[SDS((sub, dil * MEM_W), MXU), SDS((sub, dil * MEM_W), F32), SDS((sub, dil * MEM_W), F32)],
        scratch_shapes=[pltpu.VMEM(blk, F32), pltpu.VMEM(blk, F32)],
        compiler_params=_cp("parallel", "arbitrary"), name=name,
    )(qv, kview, kview, vview, vview, lse, do, dd)


def _group_weights(lse_refs):
    l0, l1, l2 = (r[...] for r in lse_refs)
    mx = jnp.maximum(jnp.maximum(l0, l1), l2)
    e = [jnp.exp(l - mx) for l in (l0, l1, l2)]
    den = e[0] + e[1] + e[2]
    return [x / den for x in e]


def dil_mix_prep(o_list, lse_list, m, name):
    s = m.shape[0]
    tr = _row_tile(s)

    def body(o0, o1, o2, l0, l1, l2, m_ref, out_ref):
        w = _group_weights((l0, l1, l2))
        for g, o_ref in enumerate((o0, o1, o2)):
            out_ref[:, g * MEM_W:(g + 1) * MEM_W] = (o_ref[...] * w[g]).astype(out_ref.dtype)
        out_ref[:, MIX_W:] = m_ref[...]

    blk = BS((tr, MEM_W), lambda i: (i, 0))
    return pl.pallas_call(
        body, grid=(s // tr,), in_specs=[blk] * 7,
        out_specs=BS((tr, D), lambda i: (i, 0)), out_shape=SDS((s, D), MXU),
        compiler_params=_cp("parallel"), name=name,
    )(*o_list, *lse_list, m)


def dil_mix_bwd(dym, o_list, lse_list, name):
    s = dym.shape[0]
    tr = _row_tile(s)

    def body(da_ref, o0, o1, o2, l0, l1, l2, do0, do1, do2, dd0, dd1, dd2):
        w = _group_weights((l0, l1, l2))
        tot = None
        for g, (o_ref, do_ref) in enumerate(zip((o0, o1, o2), (do0, do1, do2))):
            da = da_ref[:, g * MEM_W:(g + 1) * MEM_W]
            do_ref[...] = da * w[g]
            x = da * o_ref[...]
            dw = jnp.concatenate(
                [jnp.broadcast_to(jnp.sum(x[:, hh * HD:(hh + 1) * HD], axis=-1, keepdims=True), (tr, HD))
                 for hh in range(4)], axis=1)
            tot = w[g] * dw if tot is None else tot + w[g] * dw
        for g, dd_ref in enumerate((dd0, dd1, dd2)):
            dd_ref[...] = -w[g] * tot

    blk = BS((tr, MEM_W), lambda i: (i, 0))
    outs = pl.pallas_call(
        body, grid=(s // tr,), in_specs=[BS((tr, MIX_W), lambda i: (i, 0))] + [blk] * 6,
        out_specs=[blk] * 6, out_shape=[SDS((s, MEM_W), F32)] * 6,
        compiler_params=_cp("parallel"), name=name,
    )(dym, *o_list, *lse_list)
    return outs[:3], outs[3:]


def sum_cast(parts, name):
    s = parts[0][0].shape[0]
    tr = _row_tile(s)
    flat = [a for p in parts for a in p]
    sizes = [len(p) for p in parts]

    def body(*refs):
        out_ref = refs[-1]
        pos = 0
        for j, n in enumerate(sizes):
            acc = refs[pos][...].astype(F32)
            for t in range(1, n):
                acc = acc + refs[pos + t][...].astype(F32)
            out_ref[:, j * MEM_W:(j + 1) * MEM_W] = acc.astype(out_ref.dtype)
            pos += n

    blk = BS((tr, MEM_W), lambda i: (i, 0))
    width = MEM_W * len(parts)
    return pl.pallas_call(
        body, grid=(s // tr,), in_specs=[blk] * len(flat),
        out_specs=BS((tr, width), lambda i: (i, 0)), out_shape=SDS((s, width), MXU),
        compiler_params=_cp("parallel"), name=name,
    )(*flat)


def add_n(arrs, name):
    rows, cols = arrs[0].shape
    tr = _row_tile(rows)

    def body(*refs):
        acc = refs[0][...]
        for r in refs[1:-1]:
            acc = acc + r[...]
        refs[-1][...] = acc

    blk = BS((tr, cols), lambda i: (i, 0))
    return pl.pallas_call(
        body, grid=(rows // tr,), in_specs=[blk] * len(arrs), out_specs=blk,
        out_shape=SDS((rows, cols), F32), compiler_params=_cp("parallel"), name=name,
    )(*arrs)


class _NoExchange:
    def hook(self, where, l, after):
        return []


def _fwd_bwd(x, mem, target, small, big, gs, gb, sched):
    s = x.shape[0]
    tm = min(1024, s)
    ts = min(2048, s)

    def after_hook(arr, where, l, after):
        toks = sched.hook(where, l, after)
        return tie(arr, toks, "tie_%s_%d" % (where, l)) if toks else arr

    h = x
    saved = []
    kv = None
    mem_n = None
    hn = norm_cast(h, small["a_pre_mix_g"][0], "pre_norm")
    for l in range(4):
        rec = l < 2
        p, j = ("a", l) if rec else ("b", l - 2)
        sv = {"h": h}
        hn = after_hook(hn, "fwd_begin", l, h)
        if mem_n is None:
            mem_n = norm_cast(mem, small["mem_norm_g"], "mem_norm")
        kvm = mm_nn(mem_n, big[p + "_w_mem_kv"][j], tm=N_MEM, tn=2 * MEM_W, tk=D, out_dtype=MXU, name="mem_kv")
        if rec:
            proj = mm_nn(hn, big["a_w_in"][j], tm=tm, tn=896, tk=D, out_dtype=F32, name="rec_in")
            xc, hl = lru_fwd(proj, small["a_conv_w"][j], small["a_conv_b"][j], small["a_gate_a_w"][j],
                             small["a_gate_a_b"][j], small["a_gate_x_w"][j], small["a_gate_x_b"][j],
                             small["a_lambda"][j], "lru_fwd")
            m = mem_attn_fwd(proj, 2 * MIX_W // MEM_W, kvm, "rec_mem_attn")
            ym = lru_mix_prep(hl, proj, m, "lru_mix_prep")
            sv.update(xc=xc, hl=hl)
        else:
            proj = mm_nn(hn, big["b_w_in"][j], tm=tm, tn=1024, tk=D, out_dtype=F32, name="dil_in")
            o_list, lse_list = [], []
            for gi in range(3):
                o, lse = dil_attn_fwd(proj, kv, gi, "dil_attn_fwd%d" % gi)
                o_list.append(o.reshape(s, MEM_W))
                lse_list.append(lse.reshape(s, MEM_W))
            m = mem_attn_fwd(proj, MIX_W // MEM_W, kvm, "dil_mem_attn")
            ym = dil_mix_prep(o_list, lse_list, m, "dil_mix_prep")
            sv.update(o=o_list, lse=lse_list)
        ym = after_hook(ym, "fwd_q1", l, ym)
        mix = mm_nn(ym, big[p + "_w_out"][j], tm=tm, tn=1024, tk=D, out_dtype=F32, name="mix_out")
        h1, hn2 = resid_norm_next(h, mix, small[p + "_post_mix_g"][j], small[p + "_pre_ffn_g"][j], "post_pre_norm")
        hn2 = after_hook(hn2, "fwd_mid", l, mix)
        g, u, act = ffn_in_fwd(hn2, big[p + "_w_ffn_in"][j], "ffn_in")
        act = after_hook(act, "fwd_q3", l, u)
        y2 = mm_nn(act, big[p + "_w_ffn_out"][j], tm=tm // 2, tn=D, tk=D_FF // 2, out_dtype=F32, name="ffn_out")
        sv.update(kvm=kvm, hn=hn, proj=proj, ym=ym, mix=mix, h1=h1, hn2=hn2, g=g, u=u, act=act, y2=y2)
        saved.append(sv)
        if l < 3:
            pn, jn = ("a", l + 1) if l + 1 < 2 else ("b", l - 1)
            h, hn = resid_norm_next(h1, y2, small[p + "_post_ffn_g"][j], small[pn + "_pre_mix_g"][jn],
                                    "post_pre_norm")
        else:
            h = resid_norm(h1, y2, small[p + "_post_ffn_g"][j], "post_norm")
        sched.hook("fwd_end", l, h)
        if l == 1:
            h_kv = h
            kvn = norm_cast(h, small["kv_norm_g"], "pre_norm")
            kv = mm_nn(kvn, big["w_kv_shared"], tm=tm, tn=768, tk=D, out_dtype=MXU, name="kv_proj")

    loss_parts, dh = loss_head(h, target, "loss_head")

    def stack2(name, j, val):
        gs.setdefault(name, [None, None])[j] = val

    def stack2b(name, j, val):
        gb.setdefault(name, [None, None])[j] = val

    dkv_parts = []
    ahead = []
    dmem_parts = []
    dkvm = [None] * 4
    for l in (3, 2, 1, 0):
        rec = l < 2
        p, j = ("a", l) if rec else ("b", l - 2)
        sv = saved[l]
        if l == 1:
            dkv = sum_cast([(dkv_parts[0][c], dkv_parts[1][c]) for c in range(6)], "dkv_sum")
            dkvn = mm_nt([dkv], big["w_kv_shared"], tm=tm, tn=D, tk=768, out_dtype=F32, name="kv_proj_dx")
            gb["w_kv_shared"] = mm_tn(kvn, [dkv], t1=D, tn=768, ts=ts, col_shards=True, name="kv_proj_dw")
            dh, gs["kv_norm_g"], *ahead = norm_bwd(h_kv, small["kv_norm_g"], dkvn, dh, F32, "pre_post_norm_bwd",
                                                   then=(sv["y2"], small["a_post_ffn_g"][1]))
        if ahead:
            dy2, dg = ahead
            ahead = []
        else:
            dy2, dg = norm_bwd(sv["y2"], small[p + "_post_ffn_g"][j], dh, None, MXU, "post_norm_bwd")
        dy2 = after_hook(dy2, "bwd_begin", l, dh)
        stack2(p + "_post_ffn_g", j, dg)
        dgg, dgu = ffn_act_bwd(dy2, big[p + "_w_ffn_out"][j], sv["g"], sv["u"], "ffn_act_bwd")
        dgg = after_hook(dgg, "bwd_mid1", l, dgu)
        stack2b(p + "_w_ffn_out", j, mm_tn(sv["act"], [dy2], t1=D_FF // 4, tn=D, ts=ts // 2, col_shards=False,
                                          name="ffn_out_dw"))
        dhn2 = mm_nt([dgg, dgu], big[p + "_w_ffn_in"][j], tm=tm // 2, tn=D, tk=D_FF // 2, out_dtype=F32,
                     name="ffn_in_dx")
        stack2b(p + "_w_ffn_in", j, mm_tn(sv["hn2"], [dgg, dgu], t1=D // 2, tn=D_FF // 4, ts=ts, col_shards=True,
                                         name="ffn_in_dw"))
        dhn2 = after_hook(dhn2, "bwd_mid2", l, gb[p + "_w_ffn_in"][j])
        dh1, dg, dmix, dg_mix = norm_bwd(sv["h1"], small[p + "_pre_ffn_g"][j], dhn2, dh, F32, "pre_post_norm_bwd",
                                         then=(sv["mix"], small[p + "_post_mix_g"][j]))
        stack2(p + "_pre_ffn_g", j, dg)
        stack2(p + "_post_mix_g", j, dg_mix)
        dym = mm_nt([dmix], big[p + "_w_out"][j], tm=tm, tn=1024, tk=D, out_dtype=F32, name="mix_out_dx")
        stack2b(p + "_w_out", j, mm_tn(sv["ym"], [dmix], t1=D, tn=1024, ts=ts, col_shards=False,
                                      name="mix_out_dw"))
        dym = after_hook(dym, "bwd_m1", l, gb[p + "_w_out"][j])
        if rec:
            dqm, dkvm[l] = mem_attn_bwd(sv["proj"], 2 * MIX_W // MEM_W, sv["kvm"], dym, "rec_mem_attn_bwd")
            dproj, dcw, dcb, dwa, dba, dwx, dbx, dlam = lru_bwd(
                dym, sv["proj"], sv["xc"], sv["hl"], dqm, small["a_conv_w"][j], small["a_gate_a_w"][j],
                small["a_gate_a_b"][j], small["a_gate_x_w"][j], small["a_gate_x_b"][j], small["a_lambda"][j],
                "lru_bwd")
            for nm, val in (("a_conv_w", dcw), ("a_conv_b", dcb), ("a_gate_a_w", dwa), ("a_gate_a_b", dba),
                            ("a_gate_x_w", dwx), ("a_gate_x_b", dbx), ("a_lambda", dlam)):
                stack2(nm, j, val)
            dhn = mm_nt([dproj], big["a_w_in"][j], tm=tm, tn=D, tk=896, out_dtype=F32, name="rec_in_dx")
            stack2b("a_w_in", j, mm_tn(sv["hn"], [dproj], t1=D, tn=896, ts=ts, col_shards=True, name="rec_in_dw"))
        else:
            dqm, dkvm[l] = mem_attn_bwd(sv["proj"], MIX_W // MEM_W, sv["kvm"], dym, "dil_mem_attn_bwd")
            do_list, dd_list = dil_mix_bwd(dym, sv["o"], sv["lse"], "dil_mix_bwd")
            dq_list, dk_list, dv_list = [], [], []
            for gi in range(3):
                dil = DIL_GROUPS[gi][1]
                view = (s // dil, dil * MEM_W)
                dq, dk, dv = dil_attn_bwd(sv["proj"], kv, sv["lse"][gi].reshape(view), do_list[gi].reshape(view),
                                          dd_list[gi].reshape(view), gi, "dil_attn_bwd%d" % gi)
                dq_list.append(dq.reshape(s, MEM_W))
                dk_list.append(dk.reshape(s, MEM_W))
                dv_list.append(dv.reshape(s, MEM_W))
            dkv_parts.append(dk_list + dv_list)
            dproj = sum_cast([(a,) for a in dq_list + [dqm]], "dil_dproj")
            dhn = mm_nt([dproj], big["b_w_in"][j], tm=tm, tn=1024, tk=D, out_dtype=F32, name="dil_in_dx")
            stack2b("b_w_in", j, mm_tn(sv["hn"], [dproj], t1=D, tn=1024, ts=ts, col_shards=False, name="dil_in_dw"))
        dk_m = dkvm[l].astype(MXU)
        dmem_parts.append(mm_nt([dk_m], big[p + "_w_mem_kv"][j], tm=N_MEM, tn=D, tk=2 * MEM_W, out_dtype=F32,
                                name="mem_kv_dx"))
        stack2b(p + "_w_mem_kv", j, mm_tn(mem_n, [dk_m], t1=D, tn=2 * MEM_W, ts=N_MEM, col_shards=False,
                                         name="mem_kv_dw"))
        if l in (3, 1):
            pn, jn = ("b", 0) if l == 3 else ("a", 0)
            dh, dg, *ahead = norm_bwd(sv["h"], small[p + "_pre_mix_g"][j], dhn, dh1, F32, "pre_post_norm_bwd",
                                      then=(saved[l - 1]["y2"], small[pn + "_post_ffn_g"][jn]))
        else:
            dh, dg = norm_bwd(sv["h"], small[p + "_pre_mix_g"][j], dhn, dh1, F32, "pre_norm_bwd")
        stack2(p + "_pre_mix_g", j, dg)
        dh = after_hook(dh, "bwd_end", l, dh)

    _, gs["mem_norm_g"] = norm_bwd(mem, small["mem_norm_g"], add_n(dmem_parts, "dmem_sum"), None, F32,
                                   "mem_norm_bwd")
    return loss_parts, dh


ANY = pl.BlockSpec(memory_space=pl.ANY)
CHIP_FLIPS = (1, 2, 3)


def _coords():
    return lax.axis_index("x"), lax.axis_index("y"), lax.axis_index("c")


def _flip(x, y, m):
    return x ^ (m >> 1), y ^ (m & 1)


def _remote(src, dst, send_sems, recv_sems, k, device):
    return pltpu.make_async_remote_copy(src_ref=src, dst_ref=dst, send_sem=send_sems.at[k], recv_sem=recv_sems.at[k],
                                        device_id=device, device_id_type=MESH)


def gather_shards(shards, name):
    n = len(shards)

    def body(*refs):
        ins, outs = refs[:n], refs[n:2 * n]
        send_sems, recv_sems = refs[2 * n:]
        x, y, c = _coords()
        me = 2 * x + y
        sib = (x, y, 1 - c)
        halves, sends = [], []
        for i in range(n):
            hr = shards[i].shape[0] // 2
            mine = pl.ds(pl.multiple_of(c * hr, 8), hr)
            other = pl.ds(pl.multiple_of((1 - c) * hr, 8), hr)
            halves.append((mine, other))
            own = _remote(ins[i], outs[i].at[me], send_sems, recv_sems, 7 * i + 6, sib)
            own.start()
            sends.append(own)
            for j, m in enumerate(CHIP_FLIPS):
                cp = _remote(ins[i].at[mine], outs[i].at[me, mine], send_sems, recv_sems, 7 * i + j,
                             (*_flip(x, y, m), c))
                cp.start()
                sends.append(cp)
        for i in range(n):
            mine, _ = halves[i]
            for j, m in enumerate(CHIP_FLIPS):
                slot = outs[i].at[me ^ m, mine]
                _remote(slot, slot, send_sems, recv_sems, 7 * i + j, sib).wait_recv()
                fwd = _remote(slot, slot, send_sems, recv_sems, 7 * i + 3 + j, sib)
                fwd.start()
                sends.append(fwd)
        for i in range(n):
            _, other = halves[i]
            for j, m in enumerate(CHIP_FLIPS):
                slot = outs[i].at[me ^ m, other]
                _remote(slot, slot, send_sems, recv_sems, 7 * i + 3 + j, sib).wait_recv()
            _remote(ins[i], outs[i].at[me], send_sems, recv_sems, 7 * i + 6, sib).wait_recv()
        for cp in sends:
            cp.wait_send()

    return pl.pallas_call(
        body, in_specs=[ANY] * n, out_specs=[ANY] * n,
        out_shape=[SDS((N_CHIPS,) + sh.shape, sh.dtype) for sh in shards],
        scratch_shapes=[pltpu.SemaphoreType.DMA((7 * n,)), pltpu.SemaphoreType.DMA((7 * n,))],
        name=name,
    )(*shards)


def swap_halves(grads, name):
    n = len(grads)

    def body(*refs):
        ins, outs = refs[:n], refs[n:2 * n]
        send_sems, recv_sems = refs[2 * n:]
        x, y, c = _coords()
        cps = []
        for i in range(n):
            hr = grads[i].shape[1] // 2
            other = pl.ds(pl.multiple_of((1 - c) * hr, 8), hr)
            cp = _remote(ins[i].at[pl.ds(0, N_CHIPS), other], outs[i], send_sems, recv_sems, i, (x, y, 1 - c))
            cp.start()
            cps.append(cp)
        for cp in cps:
            cp.wait()

    return pl.pallas_call(
        body, in_specs=[ANY] * n, out_specs=[ANY] * n,
        out_shape=[SDS((N_CHIPS, g.shape[1] // 2, g.shape[2]), g.dtype) for g in grads],
        scratch_shapes=[pltpu.SemaphoreType.DMA((n,)), pltpu.SemaphoreType.DMA((n,))],
        name=name,
    )(*grads)


def _sum_rows_tile(rows, cols, itemsize=4):
    for tr in (512, 256, 128, 64, 32, 16):
        if rows % tr == 0 and tr * cols * itemsize <= 2 * 1024 * 1024:
            return tr
    raise ValueError((rows, cols))


def half_sum(g, got, c_arr, name):
    _, r, cols = g.shape
    hr = r // 2
    tr = _sum_rows_tile(hr, cols, g.dtype.itemsize)

    def my_chip():
        return 2 * lax.axis_index("x") + lax.axis_index("y")

    def body(g_ref, got_ref, o_ref, own_ref):
        p = (g_ref[...].astype(F32) + got_ref[...].astype(F32)).astype(o_ref.dtype)
        o_ref[...] = p

        @pl.when(pl.program_id(1) == my_chip())
        def _():
            own_ref[...] = p

    out = SDS((N_CHIPS, hr, cols), jnp.bfloat16)
    return pl.pallas_call(
        body, grid=(hr // tr, N_CHIPS),
        in_specs=[BS((None, None, tr, cols), lambda i, s: (s, lax.axis_index("c"), i, 0)),
                  BS((None, tr, cols), lambda i, s: (s, i, 0))],
        out_specs=[BS((None, tr, cols), lambda i, s: (s, i, 0)),
                   BS((None, tr, cols), lambda i, s: (my_chip(), i, 0))],
        out_shape=[out, out], compiler_params=_cp("parallel", "arbitrary"), name=name,
    )(g.reshape(N_CHIPS, 2, hr, cols), got)


def exchange_parts(parts, name):
    n = len(parts)

    def body(*refs):
        ins, outs = refs[:n], refs[n:2 * n]
        send_sems, recv_sems, loc_sems = refs[2 * n:]
        x, y, c = _coords()
        me = 2 * x + y
        cps, locs = [], []
        for i in range(n):
            loc = pltpu.make_async_copy(ins[i].at[me], outs[i].at[me], loc_sems.at[i])
            loc.start()
            locs.append(loc)
            for j, m in enumerate(CHIP_FLIPS):
                cp = _remote(ins[i].at[me ^ m], outs[i].at[me], send_sems, recv_sems, 3 * i + j, (*_flip(x, y, m), c))
                cp.start()
                cps.append(cp)
        for cp in cps:
            cp.wait()
        for loc in locs:
            loc.wait()

    return pl.pallas_call(
        body, in_specs=[ANY] * n, out_specs=[ANY] * n,
        out_shape=[SDS(p.shape, p.dtype) for p in parts],
        scratch_shapes=[pltpu.SemaphoreType.DMA((3 * n,)), pltpu.SemaphoreType.DMA((3 * n,)),
                        pltpu.SemaphoreType.DMA((n,))],
        name=name,
    )(*parts)


def slot_sum(slots, c_arr, name):
    _, hr, cols = slots.shape
    tr = _sum_rows_tile(hr, cols)
    nblk = hr // tr

    def body(s_ref, o_ref):
        acc = s_ref[0].astype(F32)
        for p in range(1, N_CHIPS):
            acc = acc + s_ref[p].astype(F32)
        o_ref[...] = acc

    return pl.pallas_call(
        body, grid=(nblk,), in_specs=[BS((N_CHIPS, tr, cols), lambda i: (0, i, 0))],
        out_specs=BS((tr, cols), lambda i: (lax.axis_index("c") * nblk + i, 0)),
        out_shape=SDS((2 * hr, cols), F32), compiler_params=_cp("parallel"), name=name,
    )(slots)


def share_halves(bufs, name):
    n = len(bufs)

    def body(*refs):
        outs = refs[n:2 * n]
        send_sems, recv_sems = refs[2 * n:]
        x, y, c = _coords()
        cps = []
        for i in range(n):
            hr = bufs[i].shape[0] // 2
            mine = outs[i].at[pl.ds(pl.multiple_of(c * hr, 8), hr)]
            cp = _remote(mine, mine, send_sems, recv_sems, i, (x, y, 1 - c))
            cp.start()
            cps.append(cp)
        for cp in cps:
            cp.wait()

    return pl.pallas_call(
        body, in_specs=[ANY] * n, out_specs=[ANY] * n,
        out_shape=[SDS(b.shape, b.dtype) for b in bufs],
        input_output_aliases={i: i for i in range(n)},
        scratch_shapes=[pltpu.SemaphoreType.DMA((n,)), pltpu.SemaphoreType.DMA((n,))],
        name=name,
    )(*bufs)


HBM_SPEC = pl.BlockSpec(memory_space=pltpu.HBM)
SEM_SPEC = pl.BlockSpec(memory_space=pltpu.SEMAPHORE)
EFFECT = pltpu.SideEffectType.DATAFLOW_SIDE_EFFECTING


def split_start(name, bufs, plan, n_copies):
    nb = len(bufs)

    def body(*refs):
        send_sems, recv_sems = refs[nb], refs[nb + 1]
        for k, (src, dst, dev) in enumerate(plan(refs[:nb])):
            _remote(src, dst, send_sems, recv_sems, k, dev).start()
        refs[-1][...] = jnp.zeros_like(refs[-1])

    outs = pl.pallas_call(
        body, name=name,
        out_shape=(pltpu.SemaphoreType.DMA((n_copies,)), pltpu.SemaphoreType.DMA((n_copies,)),
                   *[pltpu.HBM(b.shape, b.dtype) for b in bufs], SDS((8, LANES), F32)),
        in_specs=[HBM_SPEC] * nb, out_specs=(SEM_SPEC, SEM_SPEC, *[HBM_SPEC] * nb, VM),
        input_output_aliases={i: 2 + i for i in range(nb)},
        compiler_params=pltpu.CompilerParams(has_side_effects=EFFECT),
    )(*[pltpu.with_memory_space_constraint(b, pltpu.HBM) for b in bufs])
    return outs[0], outs[1], list(outs[2:2 + nb]), outs[-1]


def split_wait(name, send_sems, recv_sems, bufs, after, plan):
    nb = len(bufs)

    def body(*refs):
        send_ref, recv_ref = refs[nb], refs[nb + 1]
        for k, (src, dst, dev) in enumerate(plan(refs[:nb])):
            cp = _remote(src, dst, send_ref, recv_ref, k, dev)
            cp.wait_send()
            cp.wait_recv()

    outs = pl.pallas_call(
        body, name=name, out_shape=[pltpu.HBM(b.shape, b.dtype) for b in bufs],
        in_specs=[HBM_SPEC] * nb + [SEM_SPEC, SEM_SPEC, ANY], out_specs=[HBM_SPEC] * nb,
        input_output_aliases={i: i for i in range(nb)},
        compiler_params=pltpu.CompilerParams(has_side_effects=EFFECT),
    )(*bufs, send_sems, recv_sems, after)
    return list(outs)


def tie(x, tokens, name):
    def body(*refs):
        pass

    return pl.pallas_call(
        body, name=name, out_shape=SDS(x.shape, x.dtype), in_specs=[ANY] * (1 + len(tokens)), out_specs=ANY,
        input_output_aliases={0: 0},
    )(x, *tokens)


def plan_gather_ici(n, rows):
    def plan(refs):
        x, y, c = _coords()
        me = 2 * x + y
        out = []
        for i in range(n):
            hr = rows[i] // 2
            mine = pl.ds(pl.multiple_of(c * hr, 8), hr)
            out.append((refs[i], refs[n + i].at[me], (x, y, 1 - c)))
            for m in CHIP_FLIPS:
                out.append((refs[i].at[mine], refs[n + i].at[me, mine], (*_flip(x, y, m), c)))
        return out
    return plan


def plan_gather_d2d(n, rows):
    def plan(refs):
        x, y, c = _coords()
        me = 2 * x + y
        out = []
        for i in range(n):
            hr = rows[i] // 2
            mine = pl.ds(pl.multiple_of(c * hr, 8), hr)
            for m in CHIP_FLIPS:
                slot = refs[i].at[me ^ m, mine]
                out.append((slot, slot, (x, y, 1 - c)))
        return out
    return plan


def plan_swap(n, rows):
    def plan(refs):
        x, y, c = _coords()
        out = []
        for i in range(n):
            hr = rows[i] // 2
            other = pl.ds(pl.multiple_of((1 - c) * hr, 8), hr)
            out.append((refs[i].at[pl.ds(0, N_CHIPS), other], refs[n + i], (x, y, 1 - c)))
        return out
    return plan


def plan_exchange(n):
    def plan(refs):
        x, y, c = _coords()
        me = 2 * x + y
        out = []
        for i in range(n):
            for m in CHIP_FLIPS:
                out.append((refs[i].at[me ^ m], refs[n + i].at[me], (*_flip(x, y, m), c)))
        return out
    return plan


def plan_share(n, rows):
    def plan(refs):
        x, y, c = _coords()
        out = []
        for i in range(n):
            hr = rows[i] // 2
            mine = refs[i].at[pl.ds(pl.multiple_of(c * hr, 8), hr)]
            out.append((mine, mine, (x, y, 1 - c)))
        return out
    return plan


def reduce_scatter(grads, c_arr, tag):
    got = swap_halves(grads, "rs_swap_" + tag)
    parts = [half_sum(g, r, c_arr, "rs_half_sum") for g, r in zip(grads, got)]
    slots = exchange_parts(parts, "rs_exchange_" + tag)
    return share_halves([slot_sum(s, c_arr, "rs_slot_sum") for s in slots], "rs_share_" + tag)


VM = pl.BlockSpec(memory_space=pltpu.VMEM)


def small_gather(v, name):
    def body(v_ref, out_ref, send_sems, recv_sems):
        x, y, c = _coords()
        me = 2 * x + y
        out_ref[me] = v_ref[...]
        cps = []
        for j, m in enumerate(CHIP_FLIPS):
            cp = _remote(v_ref, out_ref.at[me], send_sems, recv_sems, j, (*_flip(x, y, m), c))
            cp.start()
            cps.append(cp)
        for cp in cps:
            cp.wait()

    return pl.pallas_call(
        body, in_specs=[VM], out_specs=VM, out_shape=SDS((N_CHIPS,) + v.shape, v.dtype),
        scratch_shapes=[pltpu.SemaphoreType.DMA((3,)), pltpu.SemaphoreType.DMA((3,))],
        compiler_params=pltpu.CompilerParams(vmem_limit_bytes=VMEM_LIMIT_BYTES), name=name,
    )(v)


def small_allreduce(v, name):
    def body(v_ref, out_ref, sib_buf, slots, send_sems, recv_sems):
        x, y, c = _coords()
        me = 2 * x + y
        swap = _remote(v_ref, sib_buf, send_sems, recv_sems, 0, (x, y, 1 - c))
        swap.start()
        swap.wait()
        slots[me] = v_ref[...] + sib_buf[...]
        cps = []
        for j, m in enumerate(CHIP_FLIPS):
            cp = _remote(slots.at[me], slots.at[me], send_sems, recv_sems, 1 + j, (*_flip(x, y, m), c))
            cp.start()
            cps.append(cp)
        for cp in cps:
            cp.wait()
        out_ref[...] = (slots[0] + slots[1]) + (slots[2] + slots[3])

    return pl.pallas_call(
        body, in_specs=[VM], out_specs=VM, out_shape=SDS(v.shape, v.dtype),
        scratch_shapes=[pltpu.VMEM(v.shape, v.dtype), pltpu.VMEM((N_CHIPS,) + v.shape, v.dtype),
                        pltpu.SemaphoreType.DMA((4,)), pltpu.SemaphoreType.DMA((4,))],
        compiler_params=pltpu.CompilerParams(vmem_limit_bytes=VMEM_LIMIT_BYTES), name=name,
    )(v)


def adamw(w, g_list, m, v, name):
    nl, rows, cols = w.shape
    tr = _sum_rows_tile(rows, cols) if rows % 16 == 0 else rows
    bc1 = 1.0 - ADAM_B1 ** ADAM_STEP
    bc2 = 1.0 - ADAM_B2 ** ADAM_STEP

    def body(*refs):
        w_ref, m_ref, v_ref = refs[:3]
        g_refs = refs[3:3 + nl]
        go_ref, d_ref, mo_ref, vo_ref = refs[3 + nl:]
        layer = pl.program_id(0)
        for l in range(nl):
            @pl.when(layer == l)
            def _(l=l):
                g = g_refs[l][...]
                m_new = ADAM_B1 * m_ref[...] + (1.0 - ADAM_B1) * g
                v_new = ADAM_B2 * v_ref[...] + (1.0 - ADAM_B2) * (g * g)
                m_hat = m_new / bc1
                v_hat = v_new / bc2
                go_ref[...] = g
                d_ref[...] = -ADAM_LR * (m_hat / (jnp.sqrt(v_hat) + ADAM_EPS) + ADAM_WD * w_ref[...])
                mo_ref[...] = m_new
                vo_ref[...] = v_new

    stk = BS((None, tr, cols), lambda l, i: (l, i, 0))
    flat = BS((tr, cols), lambda l, i: (i, 0))
    out = SDS((nl, rows, cols), F32)
    return pl.pallas_call(
        body, grid=(nl, rows // tr), in_specs=[stk] * 3 + [flat] * nl, out_specs=[stk] * 4,
        out_shape=[out] * 4, compiler_params=_cp("parallel", "parallel"), name=name,
    )(w, m, v, *g_list)


WEIGHTS = ["mem_norm_g", "a_pre_mix_g", "a_post_mix_g", "a_pre_ffn_g", "a_post_ffn_g", "a_w_in", "a_conv_w",
           "a_conv_b", "a_gate_a_w", "a_gate_a_b", "a_gate_x_w", "a_gate_x_b", "a_lambda", "a_w_mem_kv", "a_w_out",
           "a_w_ffn_in", "a_w_ffn_out", "kv_norm_g", "w_kv_shared", "b_pre_mix_g", "b_post_mix_g", "b_pre_ffn_g",
           "b_post_ffn_g", "b_w_in", "b_w_mem_kv", "b_w_out", "b_w_ffn_in", "b_w_ffn_out"]
BIG = {"a_w_in": True, "a_w_mem_kv": False, "a_w_out": False, "a_w_ffn_in": True, "a_w_ffn_out": False,
       "w_kv_shared": True, "b_w_in": False, "b_w_mem_kv": False, "b_w_out": False, "b_w_ffn_in": True,
       "b_w_ffn_out": False}
SHARDED_SMALL = ["a_pre_mix_g", "a_post_mix_g", "a_pre_ffn_g", "a_post_ffn_g", "a_conv_w", "a_conv_b", "a_gate_a_b",
                 "a_gate_x_b", "a_lambda"]
REPL_SMALL = ["mem_norm_g", "kv_norm_g", "b_pre_mix_g", "b_post_mix_g", "b_pre_ffn_g", "b_post_ffn_g", "a_gate_a_w",
              "a_gate_x_w"]
LANES = 128


def _pack(arrs, row_multiple=8):
    flat = jnp.concatenate([a.reshape(-1) for a in arrs])
    pad = -flat.shape[0] % (LANES * row_multiple)
    if pad:
        flat = jnp.concatenate([flat, jnp.zeros((pad,), flat.dtype)])
    return flat.reshape(-1, LANES)


def _unpack(packed, shapes):
    flat = packed.reshape(-1)
    out, pos = [], 0
    for sh in shapes:
        size = math.prod(sh)
        out.append(flat[pos:pos + size].reshape(sh))
        pos += size
    return out


def kernel(x, mem, mem_norm_g, a_pre_mix_g, a_post_mix_g, a_pre_ffn_g, a_post_ffn_g, a_w_in, a_conv_w, a_conv_b,
           a_gate_a_w, a_gate_a_b, a_gate_x_w, a_gate_x_b, a_lambda, a_w_mem_kv, a_w_out, a_w_ffn_in, a_w_ffn_out,
           kv_norm_g, w_kv_shared, b_pre_mix_g, b_post_mix_g, b_pre_ffn_g, b_post_ffn_g, b_w_in, b_w_mem_kv, b_w_out,
           b_w_ffn_in, b_w_ffn_out, loss_target, m_mem_norm_g, m_a_pre_mix_g, m_a_post_mix_g, m_a_pre_ffn_g,
           m_a_post_ffn_g, m_a_w_in, m_a_conv_w, m_a_conv_b, m_a_gate_a_w, m_a_gate_a_b, m_a_gate_x_w, m_a_gate_x_b,
           m_a_lambda, m_a_w_mem_kv, m_a_w_out, m_a_w_ffn_in, m_a_w_ffn_out, m_kv_norm_g, m_w_kv_shared, m_b_pre_mix_g,
           m_b_post_mix_g, m_b_pre_ffn_g, m_b_post_ffn_g, m_b_w_in, m_b_w_mem_kv, m_b_w_out, m_b_w_ffn_in, m_b_w_ffn_out,
           v_mem_norm_g, v_a_pre_mix_g, v_a_post_mix_g, v_a_pre_ffn_g, v_a_post_ffn_g, v_a_w_in, v_a_conv_w, v_a_conv_b,
           v_a_gate_a_w, v_a_gate_a_b, v_a_gate_x_w, v_a_gate_x_b, v_a_lambda, v_a_w_mem_kv, v_a_w_out, v_a_w_ffn_in,
           v_a_w_ffn_out, v_kv_norm_g, v_w_kv_shared, v_b_pre_mix_g, v_b_post_mix_g, v_b_pre_ffn_g, v_b_post_ffn_g,
           v_b_w_in, v_b_w_mem_kv, v_b_w_out, v_b_w_ffn_in, v_b_w_ffn_out):
    a = dict(locals())
    xi, yi, ci = _coords()
    chip = 2 * xi + yi
    c_arr = jnp.stack([ci, chip]).astype(jnp.int32)

    got = small_gather(_pack([a[n] for n in SHARDED_SMALL]), "small_gather")
    per_chip = [_unpack(got[s], [a[n].shape for n in SHARDED_SMALL]) for s in range(N_CHIPS)]
    small = {n: jnp.concatenate([per_chip[s][k] for s in range(N_CHIPS)], axis=-1)
             for k, n in enumerate(SHARDED_SMALL)}
    small.update({n: a[n] for n in REPL_SMALL})

    groups = []
    for l in range(4):
        p, j = ("a", l) if l < 2 else ("b", l - 2)
        groups.append([(p + "_" + n, j) for n in ("w_in", "w_mem_kv", "w_out")])
        groups.append([(p + "_" + n, j) for n in ("w_ffn_in", "w_ffn_out")])
    groups[3].append(("w_kv_shared", None))
    big = {n: [None, None] for n in BIG if n != "w_kv_shared"}
    gs, gb = {}, {}
    reduced = {n: [None, None] for n in BIG if n != "w_kv_shared"}

    def put(store, n, j, val):
        if j is None:
            store[n] = val
        else:
            store[n][j] = val

    class Exchange:
        def __init__(self):
            self.state = {}

        def gather_ici(self, g):
            shards = [(a[n] if j is None else a[n][j]).astype(MXU) for n, j in groups[g]]
            rows = [sh.shape[0] for sh in shards]
            lands = [lax.empty((N_CHIPS,) + sh.shape, sh.dtype) for sh in shards]
            plan = plan_gather_ici(len(shards), rows)
            ss, rs, bufs, tok = split_start("gather_ici_%d" % g, shards + lands, plan, 4 * len(shards))
            self.state["g", g] = (ss, rs, bufs, plan, rows)
            return tok

        def gather_d2d(self, g, after):
            ss, rs, bufs, plan, rows = self.state.pop(("g", g))
            n = len(rows)
            outs = split_wait("gather_ici_wait_%d" % g, ss, rs, bufs, after, plan)[n:]
            plan = plan_gather_d2d(n, rows)
            ss, rs, bufs, tok = split_start("gather_d2d_%d" % g, outs, plan, 3 * n)
            self.state["g", g] = (ss, rs, bufs, plan)
            return tok

        def gather_done(self, g, after):
            ss, rs, bufs, plan = self.state.pop(("g", g))
            outs = split_wait("gather_d2d_wait_%d" % g, ss, rs, bufs, after, plan)
            for (n, j), w in zip(groups[g], outs):
                put(big, n, j, w if BIG[n] else w.reshape(-1, w.shape[-1]))

        def rs_swap(self, g):
            grads = []
            for n, j in groups[g]:
                gr = gb[n] if j is None else gb[n][j]
                grads.append(gr if BIG[n] else gr.reshape(N_CHIPS, gr.shape[0] // N_CHIPS, gr.shape[1]))
            rows = [gr.shape[1] for gr in grads]
            lands = [lax.empty((N_CHIPS, gr.shape[1] // 2, gr.shape[2]), gr.dtype) for gr in grads]
            plan = plan_swap(len(grads), rows)
            ss, rs, bufs, tok = split_start("rs_swap_%d" % g, grads + lands, plan, len(grads))
            self.state["r", g] = (ss, rs, bufs, plan, rows)
            return tok

        def rs_exchange(self, g, after):
            ss, rs, bufs, plan, rows = self.state.pop(("r", g))
            n = len(rows)
            bufs = split_wait("rs_swap_wait_%d" % g, ss, rs, bufs, after, plan)
            sums = [half_sum(gr, got, c_arr, "rs_half_sum") for gr, got in zip(bufs[:n], bufs[n:])]
            plan = plan_exchange(n)
            ss, rs, bufs, tok = split_start("rs_exchange_%d" % g, [p for p, _ in sums] + [s for _, s in sums], plan,
                                            3 * n)
            self.state["r", g] = (ss, rs, bufs, plan, rows)
            return tok

        def rs_share(self, g, after):
            ss, rs, bufs, plan, rows = self.state.pop(("r", g))
            n = len(rows)
            slots = split_wait("rs_exchange_wait_%d" % g, ss, rs, bufs, after, plan)[n:]
            fulls = [slot_sum(s, c_arr, "rs_slot_sum") for s in slots]
            plan = plan_share(n, rows)
            ss, rs, bufs, tok = split_start("rs_share_%d" % g, fulls, plan, n)
            self.state["r", g] = (ss, rs, bufs, plan)
            return tok

        def rs_done(self, g, after):
            ss, rs, bufs, plan = self.state.pop(("r", g))
            outs = split_wait("rs_share_wait_%d" % g, ss, rs, bufs, after, plan)
            for (n, j), r in zip(groups[g], outs):
                put(reduced, n, j, r)

        def hook(self, where, l, after):
            mix, ffn = 2 * l, 2 * l + 1
            toks = []
            if where == "fwd_begin":
                if l == 0:
                    tok = self.gather_ici(mix)
                    tok = self.gather_d2d(mix, tok)
                    self.gather_done(mix, tok)
                toks.append(self.gather_ici(ffn))
            elif where == "fwd_q1":
                toks.append(self.gather_d2d(ffn, after))
            elif where == "fwd_mid":
                self.gather_done(ffn, after)
                if l < 3:
                    toks.append(self.gather_ici(mix + 2))
            elif where == "fwd_q3":
                if l < 3:
                    toks.append(self.gather_d2d(mix + 2, after))
            elif where == "fwd_end":
                if l < 3:
                    self.gather_done(mix + 2, after)
            elif where == "bwd_begin":
                if l < 3:
                    self.rs_done(ffn + 2, after)
                    toks.append(self.rs_exchange(mix + 2, after))
            elif where == "bwd_mid1":
                if l < 3:
                    toks.append(self.rs_share(mix + 2, after))
            elif where == "bwd_mid2":
                if l < 3:
                    self.rs_done(mix + 2, after)
                toks.append(self.rs_swap(ffn))
            elif where == "bwd_m1":
                toks.append(self.rs_exchange(ffn, after))
            elif where == "bwd_end":
                toks.append(self.rs_share(ffn, after))
                toks.append(self.rs_swap(mix))
                if l == 0:
                    self.rs_done(ffn, toks[0])
                    tok = self.rs_exchange(mix, toks[1])
                    tok = self.rs_share(mix, adamw_big([n for n in BIG if n.startswith("b_")], tok))
                    self.rs_done(mix, tok)
                    toks = []
            else:
                raise ValueError(where)
            return toks

    res = {}

    def adamw_big(names, token=None):
        last = None
        for n in names:
            shape = a[n].shape
            rows, cols = shape[-2], shape[-1]
            stk = (-1, rows, cols)
            grads = reduced[n] if isinstance(reduced[n], list) else [reduced[n]]
            if token is not None:
                grads = [tie(grads[0], [token], "tie_adamw_" + n)] + grads[1:]
            outs = adamw(a[n].reshape(stk), grads, a["m_" + n].reshape(stk), a["v_" + n].reshape(stk), "adamw")
            res[n] = [o.reshape(shape) for o in outs]
            last = outs[1]
            token = last if token is not None else None
        return last

    loss_parts, dx = _fwd_bwd(x[0], mem[0], loss_target[0], small, big, gs, gb, Exchange())
    loss = lax.psum(jnp.sum(loss_parts) * (0.5 / D), ("x", "y", "c"))
    adamw_big([n for n in BIG if n not in res])

    def full(n):
        g = gs[n]
        return jnp.stack(g) if isinstance(g, list) else g

    order = SHARDED_SMALL + REPL_SMALL
    full_shapes = [full(n).shape for n in order]
    summed = _unpack(small_allreduce(_pack([full(n) for n in order]), "small_allreduce"), full_shapes)
    mine = []
    for n, g in zip(order, summed):
        if n in SHARDED_SMALL:
            width = a[n].shape[-1]
            g = lax.dynamic_slice_in_dim(g, chip * width, width, axis=g.ndim - 1)
        mine.append(g.reshape(a[n].shape))
    shapes = [a[n].shape for n in order]
    rm = 512
    outs = adamw(_pack([a[n] for n in order], rm)[None], [_pack(mine, rm)],
                 _pack([a["m_" + n] for n in order], rm)[None], _pack([a["v_" + n] for n in order], rm)[None],
                 "adamw_small")
    unpacked = [_unpack(o[0], shapes) for o in outs]
    for k, n in enumerate(order):
        res[n] = [u[k] for u in unpacked]

    return (loss, dx[None], *[res[n][0] for n in WEIGHTS], *[res[n][1] for n in WEIGHTS],
            *[res[n][2] for n in WEIGHTS], *[res[n][3] for n in WEIGHTS])
```

```python
import functools
import math

import jax
import jax.numpy as jnp
from jax import lax
from jax.experimental import pallas as pl
from jax.experimental.pallas import tpu as pltpu

D = 2048
HD = 128
MEM_W = 512
MEM_HEADS = 4
MIX_W = D - MEM_W
N_BLK = MIX_W // HD
D_FF = 5632
N_MEM = 256
RMS_EPS = 1e-6
NEG_INF = -1e30
LRU_C = 8.0
DIL_GROUPS = ((128, 1), (512, 4), (2048, 16))
Q_BLOCK = 128
SCALE = HD ** -0.5
N_CHIPS = 4
MXU_COLS = 256
ACC_CHUNK = 2 * MXU_COLS

ADAM_LR = 0.001
ADAM_B1 = 0.9
ADAM_B2 = 0.999
ADAM_EPS = 1e-08
ADAM_WD = 0.01
ADAM_STEP = 10

MXU = jnp.bfloat16
F32 = jnp.float32
VMEM_LIMIT_BYTES = 56 * 1024 * 1024

BS = pl.BlockSpec
SDS = jax.ShapeDtypeStruct
MESH = pl.DeviceIdType.MESH


def _cp(*sem):
    return pltpu.CompilerParams(dimension_semantics=sem or None, vmem_limit_bytes=VMEM_LIMIT_BYTES)


def _dot(a, b, dn=((1,), (0,))):
    return lax.dot_general(a, b, (dn, ((), ())), preferred_element_type=F32)


def _div(i, n):
    return lax.div(i, jnp.int32(n))


def _rem(i, n):
    return lax.rem(i, jnp.int32(n))


NN = ((1,), (0,))
NT = ((1,), (1,))
TN = ((0,), (0,))


def _sigmoid(z):
    return 0.5 * jnp.tanh(0.5 * z) + 0.5


def _log1p_pos(u):
    return jnp.where(u < 1e-2, u * (1.0 - u * (0.5 - u * (1.0 / 3.0))), jnp.log(1.0 + u))


def _neg_expm1(z):
    return jnp.where(z > -1e-2, -z * (1.0 + z * (0.5 + z * (1.0 / 6.0))), 1.0 - jnp.exp(z))


def _softplus(z):
    return jnp.maximum(z, 0.0) + _log1p_pos(jnp.exp(-jnp.abs(z)))


_GELU_C = math.sqrt(2.0 / math.pi)


def _gelu_and_grad(x):
    x2 = x * x
    t = jnp.tanh(_GELU_C * (x + 0.044715 * x * x2))
    g = 0.5 * x * (1.0 + t)
    dg = 0.5 * (1.0 + t) + 0.5 * x * (1.0 - t * t) * _GELU_C * (1.0 + 3.0 * 0.044715 * x2)
    return g, dg


def _row_tile(rows):
    return min(256, rows)


def norm_cast(x, g, name):
    rows = x.shape[0]
    tr = _row_tile(rows)

    def body(x_ref, g_ref, o_ref):
        xv = x_ref[...]
        r = lax.rsqrt(jnp.mean(xv * xv, axis=-1, keepdims=True) + RMS_EPS)
        o_ref[...] = (xv * r * g_ref[...]).astype(o_ref.dtype)

    return pl.pallas_call(
        body, grid=(rows // tr,),
        in_specs=[BS((tr, D), lambda i: (i, 0)), BS((1, D), lambda i: (0, 0))],
        out_specs=BS((tr, D), lambda i: (i, 0)),
        out_shape=SDS((rows, D), MXU), compiler_params=_cp("parallel"), name=name,
    )(x, g.reshape(1, D))


def resid_norm(h, y, g, name):
    rows = h.shape[0]
    tr = _row_tile(rows)

    def body(h_ref, y_ref, g_ref, o_ref):
        yv = y_ref[...]
        r = lax.rsqrt(jnp.mean(yv * yv, axis=-1, keepdims=True) + RMS_EPS)
        o_ref[...] = h_ref[...] + yv * r * g_ref[...]

    return pl.pallas_call(
        body, grid=(rows // tr,),
        in_specs=[BS((tr, D), lambda i: (i, 0)), BS((tr, D), lambda i: (i, 0)), BS((1, D), lambda i: (0, 0))],
        out_specs=BS((tr, D), lambda i: (i, 0)),
        out_shape=SDS((rows, D), F32), compiler_params=_cp("parallel"), name=name,
    )(h, y, g.reshape(1, D))


def resid_norm_next(h, y, g, g_next, name):
    rows = h.shape[0]
    tr = _row_tile(rows)

    def body(h_ref, y_ref, g_ref, gn_ref, o_ref, n_ref):
        yv = y_ref[...]
        r = lax.rsqrt(jnp.mean(yv * yv, axis=-1, keepdims=True) + RMS_EPS)
        hv = h_ref[...] + yv * r * g_ref[...]
        o_ref[...] = hv
        r2 = lax.rsqrt(jnp.mean(hv * hv, axis=-1, keepdims=True) + RMS_EPS)
        n_ref[...] = (hv * r2 * gn_ref[...]).astype(n_ref.dtype)

    row = BS((tr, D), lambda i: (i, 0))
    vec = BS((1, D), lambda i: (0, 0))
    return pl.pallas_call(
        body, grid=(rows // tr,), in_specs=[row, row, vec, vec], out_specs=[row, row],
        out_shape=[SDS((rows, D), F32), SDS((rows, D), MXU)], compiler_params=_cp("parallel"), name=name,
    )(h, y, g.reshape(1, D), g_next.reshape(1, D))


def _norm_bwd_rows(xv, gv, dyv):
    r = lax.rsqrt(jnp.mean(xv * xv, axis=-1, keepdims=True) + RMS_EPS)
    xhat = xv * r
    dxhat = dyv * gv
    dx = r * (dxhat - xhat * jnp.mean(dxhat * xhat, axis=-1, keepdims=True))
    return dx, jnp.sum(dyv * xhat, axis=0, keepdims=True)


def norm_bwd(x, g, dy, res, out_dtype, name, then=None):
    rows = x.shape[0]
    tr = _row_tile(rows)
    has_res = res is not None
    n_in = 3 + has_res + (2 if then else 0)

    def body(*refs):
        x_ref, g_ref, dy_ref = refs[:3]
        dx_ref, dg_ref = refs[n_in], refs[n_in + 1]
        dx, dg = _norm_bwd_rows(x_ref[...], g_ref[...], dy_ref[...].astype(F32))
        if has_res:
            dx = dx + refs[3][...]
        dx_ref[...] = dx.astype(dx_ref.dtype)
        first = pl.program_id(0) == 0

        @pl.when(first)
        def _():
            dg_ref[...] = jnp.zeros_like(dg_ref)

        dg_ref[...] += dg
        if then:
            x2_ref, g2_ref = refs[n_in - 2], refs[n_in - 1]
            dx2_ref, dg2_ref = refs[n_in + 2], refs[n_in + 3]
            dx2, dg2 = _norm_bwd_rows(x2_ref[...], g2_ref[...], dx)
            dx2_ref[...] = dx2.astype(dx2_ref.dtype)

            @pl.when(first)
            def _():
                dg2_ref[...] = jnp.zeros_like(dg2_ref)

            dg2_ref[...] += dg2

    row = BS((tr, D), lambda i: (i, 0))
    vec = BS((1, D), lambda i: (0, 0))
    ins = [x, g.reshape(1, D), dy] + ([res] if has_res else []) + ([then[0], then[1].reshape(1, D)] if then else [])
    outs = pl.pallas_call(
        body, grid=(rows // tr,),
        in_specs=[row, vec, row] + ([row] if has_res else []) + ([row, vec] if then else []),
        out_specs=[row, vec] + ([row, vec] if then else []),
        out_shape=[SDS((rows, D), out_dtype), SDS((1, D), F32)] + ([SDS((rows, D), MXU), SDS((1, D), F32)] if then else []),
        compiler_params=_cp("arbitrary"), name=name,
    )(*ins)
    if then:
        return outs[0], outs[1].reshape(D), outs[2], outs[3].reshape(D)
    return outs[0], outs[1].reshape(D)


def loss_head(y, target, name):
    rows = y.shape[0]
    tr = _row_tile(rows)

    def body(y_ref, t_ref, dy_ref, acc_ref):
        err = y_ref[...] - t_ref[...]
        dy_ref[...] = err * (1.0 / D)

        @pl.when(pl.program_id(0) == 0)
        def _():
            acc_ref[...] = jnp.zeros_like(acc_ref)

        acc_ref[...] += jnp.sum(err * err, axis=0, keepdims=True)

    row = BS((tr, D), lambda i: (i, 0))
    dy, acc = pl.pallas_call(
        body, grid=(rows // tr,), in_specs=[row, row],
        out_specs=[row, BS((1, D), lambda i: (0, 0))],
        out_shape=[SDS((rows, D), F32), SDS((1, D), F32)],
        compiler_params=_cp("arbitrary"), name=name,
    )(y, target)
    return acc, dy


def _mm_call(ins, in_specs, pick, dn, grid, o_spec, out_sds, name):
    gk = grid[2]
    n_in = len(ins)

    def body(*refs):
        o_ref = refs[n_in]
        k = pl.program_id(2)

        def step(a_ref, b_ref):
            acc = o_ref if (out_sds.dtype == F32 or gk == 1) else refs[n_in + 1]
            width = acc.shape[-1]
            if dn == TN or width <= ACC_CHUNK:
                chunks = [(0, width)]
            else:
                chunks = [(c0, min(c0 + ACC_CHUNK, width)) for c0 in range(0, width, ACC_CHUNK)]

            def sweep(first):
                a = a_ref[...]
                pending = None
                for c0, c1 in chunks:
                    p = _dot(a, b_ref[c0:c1, :] if dn == NT else b_ref[:, c0:c1], dn)
                    if pending is not None:
                        put(first, *pending)
                    pending = (c0, c1, p)
                put(first, *pending)

            def put(first, c0, c1, p):
                if first:
                    acc[:, c0:c1] = p.astype(acc.dtype)
                else:
                    acc[:, c0:c1] += p

            if gk == 1:
                sweep(True)
                return

            @pl.when(k == 0)
            def _():
                sweep(True)

            @pl.when(k > 0)
            def _():
                sweep(False)

            if acc is not o_ref:
                @pl.when(k == gk - 1)
                def _():
                    o_ref[...] = acc[...].astype(o_ref.dtype)

        pick(refs[:n_in], k, step)

    scratch = []
    if gk > 1 and out_sds.dtype != F32:
        scratch = [pltpu.VMEM(o_spec.block_shape[-2:], F32)]
    return pl.pallas_call(
        body, grid=grid, in_specs=in_specs, out_specs=o_spec, out_shape=out_sds,
        scratch_shapes=scratch, compiler_params=_cp("parallel", "parallel", "arbitrary"), name=name,
    )(*ins)


def _pick2(refs, k, step):
    step(refs[0], refs[1])


def mm_nn(a, w, *, tm, tn, tk, out_dtype, name):
    m, kdim = a.shape
    if w.ndim == 3:
        c = w.shape[2]
        n = N_CHIPS * c
        per = c // tn
        b_spec = BS((None, tk, tn), lambda i, j, k: (_div(j, per), k, _rem(j, per)))
    else:
        n = w.shape[1]
        b_spec = BS((tk, tn), lambda i, j, k: (k, j))
    grid = (m // tm, n // tn, kdim // tk)
    return _mm_call([a, w], [BS((tm, tk), lambda i, j, k: (i, k)), b_spec], _pick2, NN, grid,
                    BS((tm, tn), lambda i, j, k: (i, j)), SDS((m, n), out_dtype), name)


def mm_nt(a_list, w, *, tm, tn, tk, out_dtype, name):
    m = a_list[0].shape[0]
    ka = a_list[0].shape[1]
    n_a = len(a_list)
    kdim = ka * n_a
    if w.ndim == 3:
        c = w.shape[2]
        n = w.shape[1]
        per = c // tk
        b_spec = BS((None, tn, tk), lambda i, j, k: (_div(k, per), j, _rem(k, per)))
    else:
        n = w.shape[0]
        b_spec = BS((tn, tk), lambda i, j, k: (j, k))
    gk = kdim // tk
    half = gk // n_a
    grid = (m // tm, n // tn, gk)
    if n_a == 1:
        a_specs = [BS((tm, tk), lambda i, j, k: (i, k))]
        pick = lambda refs, k, step: step(refs[0], refs[1])
    else:
        a_specs = [BS((tm, tk), lambda i, j, k: (i, jnp.minimum(k, half - 1))),
                   BS((tm, tk), lambda i, j, k: (i, jnp.maximum(k - half, 0)))]

        def pick(refs, k, step):
            @pl.when(k < half)
            def _():
                step(refs[0], refs[2])

            @pl.when(k >= half)
            def _():
                step(refs[1], refs[2])

    return _mm_call(list(a_list) + [w], a_specs + [b_spec], pick, NT, grid,
                    BS((tm, tn), lambda i, j, k: (i, j)), SDS((m, n), out_dtype), name)


def mm_tn(a, b_list, *, t1, tn, ts, col_shards, name):
    s, k1 = a.shape
    nb = b_list[0].shape[1]
    n_b = len(b_list)
    n = nb * n_b
    gn = n // tn
    half = gn // n_b
    grid = (k1 // t1, gn, s // ts)
    if col_shards:
        c = n // N_CHIPS
        per = c // tn
        o_spec = BS((None, t1, tn), lambda i, j, k: (_div(j, per), i, _rem(j, per)))
        out_sds = SDS((N_CHIPS, k1, c), MXU)
    else:
        o_spec = BS((t1, tn), lambda i, j, k: (i, j))
        out_sds = SDS((k1, n), MXU)
    a_spec = BS((ts, t1), lambda i, j, k: (k, i))
    if n_b == 1:
        b_specs = [BS((ts, tn), lambda i, j, k: (k, j))]
        pick = lambda refs, k, step: step(refs[0], refs[1])
    else:
        b_specs = [BS((ts, tn), lambda i, j, k: (k, jnp.minimum(j, half - 1))),
                   BS((ts, tn), lambda i, j, k: (k, jnp.maximum(j - half, 0)))]

        def pick(refs, k, step):
            j = pl.program_id(1)

            @pl.when(j < half)
            def _():
                step(refs[0], refs[1])

            @pl.when(j >= half)
            def _():
                step(refs[0], refs[2])

    return _mm_call([a] + list(b_list), [a_spec] + b_specs, pick, TN, grid, o_spec, out_sds, name)


def ffn_in_fwd(hn, w, name):
    s = hn.shape[0]
    tm = min(512, s)
    tn = D_FF // 4

    def tail(dag_ref, dau_ref, act_ref, c0, c1, g, u):
        sg = _sigmoid(g)
        silu = g * sg
        dag_ref[:, c0:c1] = u * sg * (1.0 + g * (1.0 - sg))
        dau_ref[:, c0:c1] = silu
        act_ref[:, c0:c1] = (silu * u).astype(act_ref.dtype)

    def body(a_ref, wg_ref, wu_ref, dag_ref, dau_ref, act_ref):
        a = a_ref[...]
        pending = None
        for c0 in range(0, tn, ACC_CHUNK):
            c1 = min(c0 + ACC_CHUNK, tn)
            g = _dot(a, wg_ref[:, c0:c1])
            u = _dot(a, wu_ref[:, c0:c1])
            if pending is not None:
                tail(dag_ref, dau_ref, act_ref, *pending)
            pending = (c0, c1, g, u)
        tail(dag_ref, dau_ref, act_ref, *pending)

    tile = BS((tm, tn), lambda j, i: (i, j))
    return pl.pallas_call(
        body, grid=(4, s // tm),
        in_specs=[BS((tm, D), lambda j, i: (i, 0)),
                  BS((None, D, tn), lambda j, i: (_div(j, 2), 0, _rem(j, 2))),
                  BS((None, D, tn), lambda j, i: (2 + _div(j, 2), 0, _rem(j, 2)))],
        out_specs=[tile, tile, tile],
        out_shape=[SDS((s, D_FF), F32), SDS((s, D_FF), F32), SDS((s, D_FF), MXU)],
        compiler_params=_cp("parallel", "parallel"), name=name,
    )(hn, w, w)


def ffn_act_bwd(dy, w_out, dag, dau, name):
    s = dy.shape[0]
    tm = min(512, s)
    tn = D_FF // 4

    def body(dy_ref, w_ref, dag_ref, dau_ref, dg_ref, du_ref):
        def tail(c0, c1, dact):
            dg_ref[:, c0:c1] = (dact * dag_ref[:, c0:c1]).astype(dg_ref.dtype)
            du_ref[:, c0:c1] = (dact * dau_ref[:, c0:c1]).astype(du_ref.dtype)

        dy = dy_ref[...]
        pending = None
        for c0 in range(0, tn, ACC_CHUNK):
            c1 = min(c0 + ACC_CHUNK, tn)
            dact = _dot(dy, w_ref[c0:c1, :], NT)
            if pending is not None:
                tail(*pending)
            pending = (c0, c1, dact)
        tail(*pending)

    tile = BS((tm, tn), lambda j, i: (i, j))
    return pl.pallas_call(
        body, grid=(4, s // tm),
        in_specs=[BS((tm, D), lambda j, i: (i, 0)), BS((tn, D), lambda j, i: (j, 0)), tile, tile],
        out_specs=[tile, tile],
        out_shape=[SDS((s, D_FF), MXU), SDS((s, D_FF), MXU)],
        compiler_params=_cp("parallel", "parallel"), name=name,
    )(dy, w_out, dag, dau)


LRU_T = 256
HALO = 8


def _shift_down(x, k, fill):
    rows = x.shape[0]
    idx = lax.broadcasted_iota(jnp.int32, x.shape, 0)
    return jnp.where(idx < k, fill, pltpu.roll(x, k, 0))


def _shift_up(x, k, fill):
    rows = x.shape[0]
    idx = lax.broadcasted_iota(jnp.int32, x.shape, 0)
    return jnp.where(idx >= rows - k, fill, pltpu.roll(x, rows - k, 0))


def _scan_block(a, b, carry, reverse):
    rows, cols = a.shape
    sub = 8
    in_group = lax.broadcasted_iota(jnp.int32, a.shape, 0) % sub
    for sh in (1, 2, 4):
        if reverse:
            a_s, b_s, ok = pltpu.roll(a, rows - sh, 0), pltpu.roll(b, rows - sh, 0), in_group < sub - sh
        else:
            a_s, b_s, ok = pltpu.roll(a, sh, 0), pltpu.roll(b, sh, 0), in_group >= sh
        b = jnp.where(ok, a * b_s + b, b)
        a = jnp.where(ok, a * a_s, a)
    groups = list(range(rows // sub))
    edge = 0 if reverse else sub - 1
    carry_in = {}
    for v in (reversed(groups) if reverse else groups):
        carry_in[v] = carry
        row = sub * v + edge
        carry = b[row:row + 1, :] + a[row:row + 1, :] * carry
    cin = jnp.concatenate([jnp.broadcast_to(carry_in[v], (sub, cols)) for v in groups], axis=0)
    return b + a * cin


def _conv_taps(xcat):
    rows = xcat.shape[0]
    taps = []
    for k in range(4):
        off = HALO - 3 + k
        taps.append(xcat[off:off + LRU_T] if off == HALO else pltpu.roll(xcat, rows - off, 0)[:LRU_T])
    return taps


def _gates(xc, wa_ref, ba, wx_ref, bx, lam, za_ref, zx_ref):
    xm = xc.astype(MXU)
    for n in range(N_BLK):
        sl = slice(n * HD, (n + 1) * HD)
        za_ref[:, sl] = _dot(xm[:, sl], wa_ref[n])
        zx_ref[:, sl] = _dot(xm[:, sl], wx_ref[n])
    ra = _sigmoid(za_ref[...] + ba)
    ii = _sigmoid(zx_ref[...] + bx)
    sp = _softplus(-lam)
    log_a = -LRU_C * ra * sp
    a = jnp.exp(log_a)
    mult = jnp.sqrt(_neg_expm1(2.0 * log_a))
    return ra, ii, sp, a, mult


def lru_fwd(proj, conv_w, conv_b, wa, ba, wx, bx, lam, name):
    s = proj.shape[0]
    c = MIX_W
    nblk = s // LRU_T
    hpb = LRU_T // HALO

    def body(x_ref, halo_ref, cw_ref, cb_ref, wa_ref, ba_ref, wx_ref, bx_ref, lam_ref,
             xc_ref, h_ref, carry, za_ref, zx_ref):
        i = pl.program_id(0)

        @pl.when(i == 0)
        def _():
            carry[...] = jnp.zeros_like(carry)

        halo = jnp.where(i == 0, 0.0, halo_ref[...])
        xcat = jnp.concatenate([halo, x_ref[...]], axis=0)
        taps = _conv_taps(xcat)
        xc = cb_ref[...] + sum(cw_ref[k:k + 1, :] * taps[k] for k in range(4))
        xc_ref[...] = xc
        _, ii, _, a, mult = _gates(xc, wa_ref, ba_ref[...], wx_ref, bx_ref[...], lam_ref[...], za_ref, zx_ref)
        h = _scan_block(a, mult * (ii * xc), carry[HALO - 1:HALO, :], False)
        h_ref[...] = h
        carry[...] = h[LRU_T - HALO:, :]

    def full(shape):
        return BS(shape, lambda i: (0,) * len(shape))

    blk = BS((LRU_T, c), lambda i: (i, 0))
    return pl.pallas_call(
        body, grid=(nblk,),
        in_specs=[blk, BS((HALO, c), lambda i: (jnp.maximum(i * hpb - 1, 0), 0)),
                  full((4, c)), full((1, c)), full((N_BLK, HD, HD)), full((1, c)),
                  full((N_BLK, HD, HD)), full((1, c)), full((1, c))],
        out_specs=[blk, blk],
        out_shape=[SDS((s, c), F32), SDS((s, c), F32)],
        scratch_shapes=[pltpu.VMEM((HALO, c), F32), pltpu.VMEM((LRU_T, c), F32), pltpu.VMEM((LRU_T, c), F32)],
        compiler_params=_cp("arbitrary"), name=name,
    )(proj, proj, conv_w, conv_b.reshape(1, c), wa.astype(MXU), ba.reshape(1, c), wx.astype(MXU),
      bx.reshape(1, c), lam.reshape(1, c))


def lru_mix_prep(h, proj, m, name):
    s = h.shape[0]
    tr = _row_tile(s)

    def body(h_ref, gb_ref, m_ref, o_ref):
        ge, _ = _gelu_and_grad(gb_ref[...])
        o_ref[:, :MIX_W] = (h_ref[...] * ge).astype(o_ref.dtype)
        o_ref[:, MIX_W:] = m_ref[...]

    return pl.pallas_call(
        body, grid=(s // tr,),
        in_specs=[BS((tr, MIX_W), lambda i: (i, 0)), BS((tr, MIX_W), lambda i: (i, 1)),
                  BS((tr, MEM_W), lambda i: (i, 0))],
        out_specs=BS((tr, D), lambda i: (i, 0)), out_shape=SDS((s, D), MXU),
        compiler_params=_cp("parallel"), name=name,
    )(h, proj, m)


def lru_bwd(dym, proj, xc, hl, dqm, conv_w, wa, ba, wx, bx, lam, name):
    s = proj.shape[0]
    c = MIX_W
    nblk = s // LRU_T
    hpb = LRU_T // HALO
    wa_m = wa.astype(MXU)
    wx_m = wx.astype(MXU)

    def body(dy_ref, x_ref, xhalo_ref, gb_ref, xc_ref, h_ref, hhalo_ref, dqm_ref,
             cw_ref, wa_ref, ba_ref, wx_ref, bx_ref, lam_ref,
             dproj_ref, dcw_ref, dcb_ref, dwa_ref, dba_ref, dwx_ref, dbx_ref, dlam_ref,
             g_next, a_next, dxc_next, za_ref, zx_ref, dxc_ref):
        i = pl.program_id(0)

        @pl.when(i == 0)
        def _():
            g_next[...] = jnp.zeros_like(g_next)
            a_next[...] = jnp.zeros_like(a_next)
            dxc_next[...] = jnp.zeros_like(dxc_next)
            for r in (dcw_ref, dcb_ref, dwa_ref, dba_ref, dwx_ref, dbx_ref, dlam_ref):
                r[...] = jnp.zeros_like(r)

        first = i == nblk - 1
        xc = xc_ref[...]
        lam = lam_ref[...]
        ra, ii, sp, a, mult = _gates(xc, wa_ref, ba_ref[...], wx_ref, bx_ref[...], lam, za_ref, zx_ref)
        hl_v = h_ref[...]
        ge, dge = _gelu_and_grad(gb_ref[...])
        dyl = dy_ref[...]
        dhl = dyl * ge
        dproj_ref[:, c:2 * c] = (dyl * hl_v * dge).astype(dproj_ref.dtype)
        dproj_ref[:, 2 * c:] = dqm_ref[...]

        an = _shift_up(a, 1, 0.0)
        last_row = lax.broadcasted_iota(jnp.int32, a.shape, 0) == LRU_T - 1
        an = jnp.where(last_row, a_next[0:1, :], an)
        g = _scan_block(an, dhl, g_next[0:1, :], True)
        g_next[...] = g[:HALO, :]
        a_next[...] = a[:HALO, :]

        hhalo = jnp.where(first, 0.0, hhalo_ref[...])
        h_prev = _shift_down(hl_v, 1, 0.0)
        first_row = lax.broadcasted_iota(jnp.int32, a.shape, 0) == 0
        h_prev = jnp.where(first_row, hhalo[HALO - 1:HALO, :], h_prev)
        da = g * h_prev
        ixc = ii * xc
        dmult = g * ixc
        dii = g * mult * xc
        dxc = g * mult * ii
        dlog_a = (da - dmult * a / mult) * a
        dra = dlog_a * (-LRU_C) * sp
        dlam_ref[...] += jnp.sum(dlog_a * ra, axis=0, keepdims=True) * (LRU_C * _sigmoid(-lam))
        dza = dra * ra * (1.0 - ra)
        dzx = dii * ii * (1.0 - ii)
        dba_ref[...] += jnp.sum(dza, axis=0, keepdims=True)
        dbx_ref[...] += jnp.sum(dzx, axis=0, keepdims=True)
        xm = xc.astype(MXU)
        dza_m = dza.astype(MXU)
        dzx_m = dzx.astype(MXU)
        for n in range(N_BLK):
            sl = slice(n * HD, (n + 1) * HD)
            dwa_ref[n] += _dot(xm[:, sl], dza_m[:, sl], TN)
            dwx_ref[n] += _dot(xm[:, sl], dzx_m[:, sl], TN)
            dxc_ref[:, sl] = _dot(dza_m[:, sl], wa_ref[n], NT) + _dot(dzx_m[:, sl], wx_ref[n], NT)
        dxc = dxc + dxc_ref[...]

        dcat = jnp.concatenate([dxc, dxc_next[...]], axis=0)
        rows = dcat.shape[0]
        dxb = cw_ref[3:4, :] * dxc
        for k in range(3):
            dxb = dxb + cw_ref[k:k + 1, :] * pltpu.roll(dcat, rows - (3 - k), 0)[:LRU_T]
        dproj_ref[:, :c] = dxb.astype(dproj_ref.dtype)
        dxc_next[...] = dxc[:HALO, :]

        xhalo = jnp.where(first, 0.0, xhalo_ref[...])
        taps = _conv_taps(jnp.concatenate([xhalo, x_ref[...]], axis=0))
        for k in range(4):
            dcw_ref[k:k + 1, :] += jnp.sum(dxc * taps[k], axis=0, keepdims=True)
        dcb_ref[...] += jnp.sum(dxc, axis=0, keepdims=True)

    def full(shape):
        return BS(shape, lambda i: (0,) * len(shape))

    def rev(i):
        return nblk - 1 - i

    blk0 = BS((LRU_T, c), lambda i: (rev(i), 0))
    blk1 = BS((LRU_T, c), lambda i: (rev(i), 1))
    halo = BS((HALO, c), lambda i: (jnp.maximum(rev(i) * hpb - 1, 0), 0))
    outs = pl.pallas_call(
        body, grid=(nblk,),
        in_specs=[blk0, blk0, halo, blk1, blk0, blk0, halo, BS((LRU_T, MEM_W), lambda i: (rev(i), 0)),
                  full((4, c)), full((N_BLK, HD, HD)), full((1, c)), full((N_BLK, HD, HD)), full((1, c)),
                  full((1, c))],
        out_specs=[BS((LRU_T, 2 * c + MEM_W), lambda i: (rev(i), 0)), full((4, c)), full((1, c)),
                   full((N_BLK, HD, HD)), full((1, c)), full((N_BLK, HD, HD)), full((1, c)), full((1, c))],
        out_shape=[SDS((s, 2 * c + MEM_W), MXU), SDS((4, c), F32), SDS((1, c), F32),
                   SDS((N_BLK, HD, HD), F32), SDS((1, c), F32), SDS((N_BLK, HD, HD), F32), SDS((1, c), F32),
                   SDS((1, c), F32)],
        scratch_shapes=[pltpu.VMEM((HALO, c), F32), pltpu.VMEM((HALO, c), F32), pltpu.VMEM((HALO, c), F32),
                        pltpu.VMEM((LRU_T, c), F32), pltpu.VMEM((LRU_T, c), F32), pltpu.VMEM((LRU_T, c), F32)],
        compiler_params=_cp("arbitrary"), name=name,
    )(dym, proj, proj, proj, xc, hl, hl, dqm, conv_w, wa_m, ba.reshape(1, c), wx_m, bx.reshape(1, c),
      lam.reshape(1, c))
    dproj, dcw, dcb, dwa, dba, dwx, dbx, dlam = outs
    return dproj, dcw, dcb.reshape(c), dwa, dba.reshape(c), dwx, dbx.reshape(c), dlam.reshape(c)


def _mem_probs(q, kv):
    heads = [slice(hh * HD, (hh + 1) * HD) for hh in range(MEM_HEADS)]
    sc = [_dot(q[:, sl], kv[:, sl], NT) * SCALE for sl in heads]
    e = [jnp.exp(s - jnp.max(s, axis=-1, keepdims=True)) for s in sc]
    return [x / jnp.sum(x, axis=-1, keepdims=True) for x in e]


def mem_attn_fwd(proj, q_col, kvm, name):
    s = proj.shape[0]
    tq = min(512, s)

    def body(q_ref, kv_ref, o_ref):
        q = q_ref[...].astype(MXU)
        kv = kv_ref[...]
        p = _mem_probs(q, kv)
        outs = [_dot(p[hh].astype(MXU), kv[:, MEM_W + hh * HD:MEM_W + (hh + 1) * HD]) for hh in range(MEM_HEADS)]
        o_ref[...] = jnp.concatenate(outs, axis=1).astype(o_ref.dtype)

    return pl.pallas_call(
        body, grid=(s // tq,),
        in_specs=[BS((tq, MEM_W), lambda i: (i, q_col)), BS((N_MEM, 2 * MEM_W), lambda i: (0, 0))],
        out_specs=BS((tq, MEM_W), lambda i: (i, 0)), out_shape=SDS((s, MEM_W), MXU),
        compiler_params=_cp("parallel"), name=name,
    )(proj, kvm)


def mem_attn_bwd(proj, q_col, kvm, dym, name):
    s = proj.shape[0]
    tq = min(512, s)

    def body(q_ref, kv_ref, do_ref, dq_ref, dkv_ref):
        @pl.when(pl.program_id(0) == 0)
        def _():
            dkv_ref[...] = jnp.zeros_like(dkv_ref)

        q = q_ref[...].astype(MXU)
        do = do_ref[...].astype(MXU)
        kv = kv_ref[...]
        heads = [slice(hh * HD, (hh + 1) * HD) for hh in range(MEM_HEADS)]
        p = _mem_probs(q, kv)
        dp = [_dot(do[:, sl], kv[:, MEM_W + hh * HD:MEM_W + (hh + 1) * HD], NT) for hh, sl in enumerate(heads)]
        ds = [(pp * (d - jnp.sum(pp * d, axis=-1, keepdims=True)) * SCALE).astype(MXU) for pp, d in zip(p, dp)]
        dq = [_dot(x, kv[:, sl]) for x, sl in zip(ds, heads)]
        dk = [_dot(x, q[:, sl], TN) for x, sl in zip(ds, heads)]
        dv = [_dot(pp.astype(MXU), do[:, sl], TN) for pp, sl in zip(p, heads)]
        dq_ref[...] = jnp.concatenate(dq, axis=1).astype(dq_ref.dtype)
        dkv_ref[...] += jnp.concatenate(dk + dv, axis=1)

    return pl.pallas_call(
        body, grid=(s // tq,),
        in_specs=[BS((tq, MEM_W), lambda i: (i, q_col)), BS((N_MEM, 2 * MEM_W), lambda i: (0, 0)),
                  BS((tq, MEM_W), lambda i: (i, MIX_W // MEM_W))],
        out_specs=[BS((tq, MEM_W), lambda i: (i, 0)), BS((N_MEM, 2 * MEM_W), lambda i: (0, 0))],
        out_shape=[SDS((s, MEM_W), MXU), SDS((N_MEM, 2 * MEM_W), F32)],
        compiler_params=_cp("arbitrary"), name=name,
    )(proj, kvm, dym)


def _dil_scores(q, kp, kc, n, slope_dil):
    qi = lax.broadcasted_iota(jnp.int32, (Q_BLOCK, Q_BLOCK), 0)
    ki = lax.broadcasted_iota(jnp.int32, (Q_BLOCK, Q_BLOCK), 1)
    rel_p = qi + Q_BLOCK - ki
    rel_c = qi - ki
    s_p = _dot(q, kp, NT) * SCALE - slope_dil * rel_p.astype(F32)
    s_c = _dot(q, kc, NT) * SCALE - slope_dil * rel_c.astype(F32)
    s_p = jnp.where((rel_p <= Q_BLOCK) & (n > 0), s_p, NEG_INF)
    s_c = jnp.where(rel_c >= 0, s_c, NEG_INF)
    return s_p, s_c


def _slope_dil(gi, hh):
    head = 4 * gi + hh
    return DIL_GROUPS[gi][1] * 2.0 ** (-8.0 * (head + 1.0) / N_BLK)


def _dil_operands(proj, kv, gi):
    dil = DIL_GROUPS[gi][1]
    if dil == 1:
        return proj, kv, kv, (lambda r: gi), (lambda r: gi), (lambda r: MIX_W // MEM_W + gi)
    sub = proj.shape[0] // dil

    def view(a, col):
        return a[:, col:col + MEM_W].reshape(sub, dil * MEM_W)

    same = lambda r: r
    return view(proj, gi * MEM_W), view(kv, gi * MEM_W), view(kv, MIX_W + gi * MEM_W), same, same, same


def dil_attn_fwd(proj, kv, gi, name):
    dil = DIL_GROUPS[gi][1]
    s, pw = proj.shape
    sub = s // dil
    nb = sub // Q_BLOCK
    qc, kc_ = pw // MEM_W, kv.shape[1] // MEM_W

    def body(q_ref, kp_ref, kc_ref, vp_ref, vc_ref, o_ref, lse_ref):
        n = pl.program_id(1)
        q = q_ref[...].astype(MXU)
        kp, kc, vp, vc = kp_ref[...], kc_ref[...], vp_ref[...], vc_ref[...]
        heads = [slice(hh * HD, (hh + 1) * HD) for hh in range(4)]
        sc = [_dil_scores(q[:, sl], kp[:, sl], kc[:, sl], n, _slope_dil(gi, hh)) for hh, sl in enumerate(heads)]
        mx = [jnp.maximum(jnp.max(s_p, axis=-1, keepdims=True), jnp.max(s_c, axis=-1, keepdims=True))
              for s_p, s_c in sc]
        den = [jnp.sum(jnp.exp(s_p - m), axis=-1, keepdims=True) + jnp.sum(jnp.exp(s_c - m), axis=-1, keepdims=True)
               for (s_p, s_c), m in zip(sc, mx)]
        lse = [m + jnp.log(d) for m, d in zip(mx, den)]
        pr = [(jnp.exp(s_p - l).astype(MXU), jnp.exp(s_c - l).astype(MXU)) for (s_p, s_c), l in zip(sc, lse)]
        outs = [_dot(p_p, vp[:, sl]) + _dot(p_c, vc[:, sl]) for (p_p, p_c), sl in zip(pr, heads)]
        o_ref[...] = jnp.concatenate(outs, axis=1)
        lse_ref[...] = jnp.concatenate([jnp.broadcast_to(l, (Q_BLOCK, HD)) for l in lse], axis=1)

    blk = (Q_BLOCK, MEM_W)
    prev = lambda n: jnp.maximum(n - 1, 0)
    out = BS(blk, lambda r, n: (n, r))
    qv, kview, vview, qcol, kcol, vcol = _dil_operands(proj, kv, gi)
    return pl.pallas_call(
        body, grid=(dil, nb),
        in_specs=[BS(blk, lambda r, n: (n, qcol(r))),
                  BS(blk, lambda r, n: (prev(n), kcol(r))), BS(blk, lambda r, n: (n, kcol(r))),
                  BS(blk, lambda r, n: (prev(n), vcol(r))), BS(blk, lambda r, n: (n, vcol(r)))],
        out_specs=[out, out],
        out_shape=[SDS((sub, dil * MEM_W), F32), SDS((sub, dil * MEM_W), F32)],
        compiler_params=_cp("parallel", "parallel"), name=name,
    )(qv, kview, kview, vview, vview)


def dil_attn_bwd(proj, kv, lse, do, dd, gi, name):
    dil = DIL_GROUPS[gi][1]
    s, pw = proj.shape
    sub = s // dil
    nb = sub // Q_BLOCK
    qc, kc_ = pw // MEM_W, kv.shape[1] // MEM_W

    def body(q_ref, kp_ref, kc_ref, vp_ref, vc_ref, lse_ref, do_ref, dd_ref, dq_ref, dk_ref, dv_ref, ck, cv):
        n = pl.program_id(1)

        @pl.when(n == 0)
        def _():
            ck[...] = jnp.zeros_like(ck)
            cv[...] = jnp.zeros_like(cv)

        @pl.when(n < nb)
        def _():
            q = q_ref[...].astype(MXU)
            do_m = do_ref[...].astype(MXU)
            kp, kc, vp, vc = kp_ref[...], kc_ref[...], vp_ref[...], vc_ref[...]
            lse_v, dd_v, ck_v, cv_v = lse_ref[...], dd_ref[...], ck[...], cv[...]
            heads = [slice(hh * HD, (hh + 1) * HD) for hh in range(4)]
            sc = [_dil_scores(q[:, sl], kp[:, sl], kc[:, sl], n, _slope_dil(gi, hh)) for hh, sl in enumerate(heads)]
            dp = [(_dot(do_m[:, sl], vp[:, sl], NT), _dot(do_m[:, sl], vc[:, sl], NT)) for sl in heads]
            pr = [(jnp.exp(s_p - lse_v[:, sl]), jnp.exp(s_c - lse_v[:, sl])) for (s_p, s_c), sl in zip(sc, heads)]
            ds = [((p_p * (dp_p + dd_v[:, sl]) * SCALE).astype(MXU), (p_c * (dp_c + dd_v[:, sl]) * SCALE).astype(MXU))
                  for (p_p, p_c), (dp_p, dp_c), sl in zip(pr, dp, heads)]
            pm = [(p_p.astype(MXU), p_c.astype(MXU)) for p_p, p_c in pr]
            dq = [_dot(ds_p, kp[:, sl]) + _dot(ds_c, kc[:, sl]) for (ds_p, ds_c), sl in zip(ds, heads)]
            dk = [ck_v[:, sl] + _dot(ds_p, q[:, sl], TN) for (ds_p, _), sl in zip(ds, heads)]
            dv = [cv_v[:, sl] + _dot(p_p, do_m[:, sl], TN) for (p_p, _), sl in zip(pm, heads)]
            ck_new = [_dot(ds_c, q[:, sl], TN) for (_, ds_c), sl in zip(ds, heads)]
            cv_new = [_dot(p_c, do_m[:, sl], TN) for (_, p_c), sl in zip(pm, heads)]
            dq_ref[...] = jnp.concatenate(dq, axis=1).astype(dq_ref.dtype)
            dk_ref[...] = jnp.concatenate(dk, axis=1)
            dv_ref[...] = jnp.concatenate(dv, axis=1)
            ck[...] = jnp.concatenate(ck_new, axis=1)
            cv[...] = jnp.concatenate(cv_new, axis=1)

        @pl.when(n == nb)
        def _():
            dk_ref[...] = ck[...]
            dv_ref[...] = cv[...]

    blk = (Q_BLOCK, MEM_W)
    cur = lambda n: jnp.minimum(n, nb - 1)
    prev = lambda n: jnp.maximum(jnp.minimum(n, nb - 1) - 1, 0)
    done = lambda n: jnp.maximum(n - 1, 0)
    own = BS(blk, lambda r, n: (cur(n), r))
    qv, kview, vview, qcol, kcol, vcol = _dil_operands(proj, kv, gi)
    return pl.pallas_call(
        body, grid=(dil, nb + 1),
        in_specs=[BS(blk, lambda r, n: (cur(n), qcol(r))),
                  BS(blk, lambda r, n: (prev(n), kcol(r))), BS(blk, lambda r, n: (cur(n), kcol(r))),
                  BS(blk, lambda r, n: (prev(n), vcol(r))), BS(blk, lambda r, n: (cur(n), vcol(r))),
                  own, own, own],
        out_specs=[own, BS(blk, lambda r, n: (done(n), r)), BS(blk, lambda r, n: (done(n), r))],
        out_shape=[SDS((sub, dil * MEM_W), MXU), SDS((sub, dil * MEM_W), F32), SDS((sub, dil * MEM_W), F32)],
        scratch_shapes=[pltpu.VMEM(blk, F32), pltpu.VMEM(blk, F32)],
        compiler_params=_cp("parallel", "arbitrary"), name=name,
    )(qv, kview, kview, vview, vview, lse, do, dd)


def _group_weights(lse_refs):
    l0, l1, l2 = (r[...] for r in lse_refs)
    mx = jnp.maximum(jnp.maximum(l0, l1), l2)
    e = [jnp.exp(l - mx) for l in (l0, l1, l2)]
    den = e[0] + e[1] + e[2]
    return [x / den for x in e]


def dil_mix_prep(o_list, lse_list, m, name):
    s = m.shape[0]
    tr = _row_tile(s)

    def body(o0, o1, o2, l0, l1, l2, m_ref, out_ref):
        w = _group_weights((l0, l1, l2))
        for g, o_ref in enumerate((o0, o1, o2)):
            out_ref[:, g * MEM_W:(g + 1) * MEM_W] = (o_ref[...] * w[g]).astype(out_ref.dtype)
        out_ref[:, MIX_W:] = m_ref[...]

    blk = BS((tr, MEM_W), lambda i: (i, 0))
    return pl.pallas_call(
        body, grid=(s // tr,), in_specs=[blk] * 7,
        out_specs=BS((tr, D), lambda i: (i, 0)), out_shape=SDS((s, D), MXU),
        compiler_params=_cp("parallel"), name=name,
    )(*o_list, *lse_list, m)


def dil_mix_bwd(dym, o_list, lse_list, name):
    s = dym.shape[0]
    tr = _row_tile(s)

    def body(da_ref, o0, o1, o2, l0, l1, l2, do0, do1, do2, dd0, dd1, dd2):
        w = _group_weights((l0, l1, l2))
        tot = None
        for g, (o_ref, do_ref) in enumerate(zip((o0, o1, o2), (do0, do1, do2))):
            da = da_ref[:, g * MEM_W:(g + 1) * MEM_W]
            do_ref[...] = da * w[g]
            x = da * o_ref[...]
            dw = jnp.concatenate(
                [jnp.broadcast_to(jnp.sum(x[:, hh * HD:(hh + 1) * HD], axis=-1, keepdims=True), (tr, HD))
                 for hh in range(4)], axis=1)
            tot = w[g] * dw if tot is None else tot + w[g] * dw
        for g, dd_ref in enumerate((dd0, dd1, dd2)):
            dd_ref[...] = -w[g] * tot

    blk = BS((tr, MEM_W), lambda i: (i, 0))
    outs = pl.pallas_call(
        body, grid=(s // tr,), in_specs=[BS((tr, MIX_W), lambda i: (i, 0))] + [blk] * 6,
        out_specs=[blk] * 6, out_shape=[SDS((s, MEM_W), F32)] * 6,
        compiler_params=_cp("parallel"), name=name,
    )(dym, *o_list, *lse_list)
    return outs[:3], outs[3:]


def sum_cast(parts, name):
    s = parts[0][0].shape[0]
    tr = _row_tile(s)
    flat = [a for p in parts for a in p]
    sizes = [len(p) for p in parts]

    def body(*refs):
        out_ref = refs[-1]
        pos = 0
        for j, n in enumerate(sizes):
            acc = refs[pos][...].astype(F32)
            for t in range(1, n):
                acc = acc + refs[pos + t][...].astype(F32)
            out_ref[:, j * MEM_W:(j + 1) * MEM_W] = acc.astype(out_ref.dtype)
            pos += n

    blk = BS((tr, MEM_W), lambda i: (i, 0))
    width = MEM_W * len(parts)
    return pl.pallas_call(
        body, grid=(s // tr,), in_specs=[blk] * len(flat),
        out_specs=BS((tr, width), lambda i: (i, 0)), out_shape=SDS((s, width), MXU),
        compiler_params=_cp("parallel"), name=name,
    )(*flat)


def add_n(arrs, name):
    rows, cols = arrs[0].shape
    tr = _row_tile(rows)

    def body(*refs):
        acc = refs[0][...]
        for r in refs[1:-1]:
            acc = acc + r[...]
        refs[-1][...] = acc

    blk = BS((tr, cols), lambda i: (i, 0))
    return pl.pallas_call(
        body, grid=(rows // tr,), in_specs=[blk] * len(arrs), out_specs=blk,
        out_shape=SDS((rows, cols), F32), compiler_params=_cp("parallel"), name=name,
    )(*arrs)


class _NoExchange:
    def hook(self, where, l, after):
        return []


def _fwd_bwd(x, mem, target, small, big, gs, gb, sched):
    s = x.shape[0]
    tm = min(1024, s)
    ts = min(2048, s)

    def after_hook(arr, where, l, after):
        toks = sched.hook(where, l, after)
        return tie(arr, toks, "tie_%s_%d" % (where, l)) if toks else arr

    h = x
    saved = []
    kv = None
    mem_n = None
    hn = norm_cast(h, small["a_pre_mix_g"][0], "pre_norm")
    for l in range(4):
        rec = l < 2
        p, j = ("a", l) if rec else ("b", l - 2)
        sv = {"h": h}
        hn = after_hook(hn, "fwd_begin", l, h)
        if mem_n is None:
            mem_n = norm_cast(mem, small["mem_norm_g"], "mem_norm")
        kvm = mm_nn(mem_n, big[p + "_w_mem_kv"][j], tm=N_MEM, tn=2 * MEM_W, tk=D, out_dtype=MXU, name="mem_kv")
        if rec:
            proj = mm_nn(hn, big["a_w_in"][j], tm=min(2 * tm, s), tn=896, tk=D, out_dtype=F32, name="rec_in")
            xc, hl = lru_fwd(proj, small["a_conv_w"][j], small["a_conv_b"][j], small["a_gate_a_w"][j],
                             small["a_gate_a_b"][j], small["a_gate_x_w"][j], small["a_gate_x_b"][j],
                             small["a_lambda"][j], "lru_fwd")
            m = mem_attn_fwd(proj, 2 * MIX_W // MEM_W, kvm, "rec_mem_attn")
            ym = lru_mix_prep(hl, proj, m, "lru_mix_prep")
            sv.update(xc=xc, hl=hl)
        else:
            proj = mm_nn(hn, big["b_w_in"][j], tm=tm, tn=D, tk=D, out_dtype=F32, name="dil_in")
            o_list, lse_list = [], []
            for gi in range(3):
                o, lse = dil_attn_fwd(proj, kv, gi, "dil_attn_fwd%d" % gi)
                o_list.append(o.reshape(s, MEM_W))
                lse_list.append(lse.reshape(s, MEM_W))
            m = mem_attn_fwd(proj, MIX_W // MEM_W, kvm, "dil_mem_attn")
            ym = dil_mix_prep(o_list, lse_list, m, "dil_mix_prep")
            sv.update(o=o_list, lse=lse_list)
        ym = after_hook(ym, "fwd_q1", l, ym)
        mix = mm_nn(ym, big[p + "_w_out"][j], tm=tm, tn=D, tk=D, out_dtype=F32, name="mix_out")
        h1, hn2 = resid_norm_next(h, mix, small[p + "_post_mix_g"][j], small[p + "_pre_ffn_g"][j], "post_pre_norm")
        hn2 = after_hook(hn2, "fwd_mid", l, mix)
        g, u, act = ffn_in_fwd(hn2, big[p + "_w_ffn_in"][j], "ffn_in")
        act = after_hook(act, "fwd_q3", l, u)
        y2 = mm_nn(act, big[p + "_w_ffn_out"][j], tm=tm // 2, tn=D, tk=D_FF // 2, out_dtype=F32, name="ffn_out")
        sv.update(kvm=kvm, hn=hn, proj=proj, ym=ym, mix=mix, h1=h1, hn2=hn2, g=g, u=u, act=act, y2=y2)
        saved.append(sv)
        if l < 3:
            pn, jn = ("a", l + 1) if l + 1 < 2 else ("b", l - 1)
            h, hn = resid_norm_next(h1, y2, small[p + "_post_ffn_g"][j], small[pn + "_pre_mix_g"][jn],
                                    "post_pre_norm")
        else:
            h = resid_norm(h1, y2, small[p + "_post_ffn_g"][j], "post_norm")
        sched.hook("fwd_end", l, h)
        if l == 1:
            h_kv = h
            kvn = norm_cast(h, small["kv_norm_g"], "pre_norm")
            kv = mm_nn(kvn, big["w_kv_shared"], tm=tm, tn=768, tk=D, out_dtype=MXU, name="kv_proj")

    loss_parts, dh = loss_head(h, target, "loss_head")

    def stack2(name, j, val):
        gs.setdefault(name, [None, None])[j] = val

    def stack2b(name, j, val):
        gb.setdefault(name, [None, None])[j] = val

    dkv_parts = []
    ahead = []
    dmem_parts = []
    dkvm = [None] * 4
    for l in (3, 2, 1, 0):
        rec = l < 2
        p, j = ("a", l) if rec else ("b", l - 2)
        sv = saved[l]
        if l == 1:
            dkv = sum_cast([(dkv_parts[0][c], dkv_parts[1][c]) for c in range(6)], "dkv_sum")
            dkvn = mm_nt([dkv], big["w_kv_shared"], tm=tm, tn=D, tk=768, out_dtype=F32, name="kv_proj_dx")
            gb["w_kv_shared"] = mm_tn(kvn, [dkv], t1=D, tn=768, ts=ts, col_shards=True, name="kv_proj_dw")
            dh, gs["kv_norm_g"], *ahead = norm_bwd(h_kv, small["kv_norm_g"], dkvn, dh, F32, "pre_post_norm_bwd",
                                                   then=(sv["y2"], small["a_post_ffn_g"][1]))
        if ahead:
            dy2, dg = ahead
            ahead = []
        else:
            dy2, dg = norm_bwd(sv["y2"], small[p + "_post_ffn_g"][j], dh, None, MXU, "post_norm_bwd")
        dy2 = after_hook(dy2, "bwd_begin", l, dh)
        stack2(p + "_post_ffn_g", j, dg)
        dgg, dgu = ffn_act_bwd(dy2, big[p + "_w_ffn_out"][j], sv["g"], sv["u"], "ffn_act_bwd")
        dgg = after_hook(dgg, "bwd_mid1", l, dgu)
        stack2b(p + "_w_ffn_out", j, mm_tn(sv["act"], [dy2], t1=D_FF // 4, tn=D, ts=ts // 2, col_shards=False,
                                          name="ffn_out_dw"))
        dhn2 = mm_nt([dgg, dgu], big[p + "_w_ffn_in"][j], tm=tm // 2, tn=D, tk=D_FF // 2, out_dtype=F32,
                     name="ffn_in_dx")
        stack2b(p + "_w_ffn_in", j, mm_tn(sv["hn2"], [dgg, dgu], t1=D // 2, tn=D_FF // 4, ts=ts, col_shards=True,
                                         name="ffn_in_dw"))
        dhn2 = after_hook(dhn2, "bwd_mid2", l, gb[p + "_w_ffn_in"][j])
        dh1, dg, dmix, dg_mix = norm_bwd(sv["h1"], small[p + "_pre_ffn_g"][j], dhn2, dh, F32, "pre_post_norm_bwd",
                                         then=(sv["mix"], small[p + "_post_mix_g"][j]))
        stack2(p + "_pre_ffn_g", j, dg)
        stack2(p + "_post_mix_g", j, dg_mix)
        dym = mm_nt([dmix], big[p + "_w_out"][j], tm=tm, tn=D, tk=D, out_dtype=F32, name="mix_out_dx")
        stack2b(p + "_w_out", j, mm_tn(sv["ym"], [dmix], t1=D, tn=1024, ts=ts, col_shards=False,
                                      name="mix_out_dw"))
        dym = after_hook(dym, "bwd_m1", l, gb[p + "_w_out"][j])
        if rec:
            dqm, dkvm[l] = mem_attn_bwd(sv["proj"], 2 * MIX_W // MEM_W, sv["kvm"], dym, "rec_mem_attn_bwd")
            dproj, dcw, dcb, dwa, dba, dwx, dbx, dlam = lru_bwd(
                dym, sv["proj"], sv["xc"], sv["hl"], dqm, small["a_conv_w"][j], small["a_gate_a_w"][j],
                small["a_gate_a_b"][j], small["a_gate_x_w"][j], small["a_gate_x_b"][j], small["a_lambda"][j],
                "lru_bwd")
            for nm, val in (("a_conv_w", dcw), ("a_conv_b", dcb), ("a_gate_a_w", dwa), ("a_gate_a_b", dba),
                            ("a_gate_x_w", dwx), ("a_gate_x_b", dbx), ("a_lambda", dlam)):
                stack2(nm, j, val)
            dhn = mm_nt([dproj], big["a_w_in"][j], tm=tm, tn=D, tk=896, out_dtype=F32, name="rec_in_dx")
            stack2b("a_w_in", j, mm_tn(sv["hn"], [dproj], t1=D, tn=896, ts=ts, col_shards=True, name="rec_in_dw"))
        else:
            dqm, dkvm[l] = mem_attn_bwd(sv["proj"], MIX_W // MEM_W, sv["kvm"], dym, "dil_mem_attn_bwd")
            do_list, dd_list = dil_mix_bwd(dym, sv["o"], sv["lse"], "dil_mix_bwd")
            dq_list, dk_list, dv_list = [], [], []
            for gi in range(3):
                dil = DIL_GROUPS[gi][1]
                view = (s // dil, dil * MEM_W)
                dq, dk, dv = dil_attn_bwd(sv["proj"], kv, sv["lse"][gi].reshape(view), do_list[gi].reshape(view),
                                          dd_list[gi].reshape(view), gi, "dil_attn_bwd%d" % gi)
                dq_list.append(dq.reshape(s, MEM_W))
                dk_list.append(dk.reshape(s, MEM_W))
                dv_list.append(dv.reshape(s, MEM_W))
            dkv_parts.append(dk_list + dv_list)
            dproj = sum_cast([(a,) for a in dq_list + [dqm]], "dil_dproj")
            dhn = mm_nt([dproj], big["b_w_in"][j], tm=tm, tn=D, tk=D, out_dtype=F32, name="dil_in_dx")
            stack2b("b_w_in", j, mm_tn(sv["hn"], [dproj], t1=D, tn=1024, ts=ts, col_shards=False, name="dil_in_dw"))
        dk_m = dkvm[l].astype(MXU)
        dmem_parts.append(mm_nt([dk_m], big[p + "_w_mem_kv"][j], tm=N_MEM, tn=D, tk=2 * MEM_W, out_dtype=F32,
                                name="mem_kv_dx"))
        stack2b(p + "_w_mem_kv", j, mm_tn(mem_n, [dk_m], t1=D, tn=2 * MEM_W, ts=N_MEM, col_shards=False,
                                         name="mem_kv_dw"))
        if l in (3, 1):
            pn, jn = ("b", 0) if l == 3 else ("a", 0)
            dh, dg, *ahead = norm_bwd(sv["h"], small[p + "_pre_mix_g"][j], dhn, dh1, F32, "pre_post_norm_bwd",
                                      then=(saved[l - 1]["y2"], small[pn + "_post_ffn_g"][jn]))
        else:
            dh, dg = norm_bwd(sv["h"], small[p + "_pre_mix_g"][j], dhn, dh1, F32, "pre_norm_bwd")
        stack2(p + "_pre_mix_g", j, dg)
        dh = after_hook(dh, "bwd_end", l, dh)

    _, gs["mem_norm_g"] = norm_bwd(mem, small["mem_norm_g"], add_n(dmem_parts, "dmem_sum"), None, F32,
                                   "mem_norm_bwd")
    return loss_parts, dh


ANY = pl.BlockSpec(memory_space=pl.ANY)
CHIP_FLIPS = (1, 2, 3)


def _coords():
    return lax.axis_index("x"), lax.axis_index("y"), lax.axis_index("c")


def _flip(x, y, m):
    return x ^ (m >> 1), y ^ (m & 1)


def _remote(src, dst, send_sems, recv_sems, k, device):
    return pltpu.make_async_remote_copy(src_ref=src, dst_ref=dst, send_sem=send_sems.at[k], recv_sem=recv_sems.at[k],
                                        device_id=device, device_id_type=MESH)


def gather_shards(shards, name):
    n = len(shards)

    def body(*refs):
        ins, outs = refs[:n], refs[n:2 * n]
        send_sems, recv_sems = refs[2 * n:]
        x, y, c = _coords()
        me = 2 * x + y
        sib = (x, y, 1 - c)
        halves, sends = [], []
        for i in range(n):
            hr = shards[i].shape[0] // 2
            mine = pl.ds(pl.multiple_of(c * hr, 8), hr)
            other = pl.ds(pl.multiple_of((1 - c) * hr, 8), hr)
            halves.append((mine, other))
            own = _remote(ins[i], outs[i].at[me], send_sems, recv_sems, 7 * i + 6, sib)
            own.start()
            sends.append(own)
            for j, m in enumerate(CHIP_FLIPS):
                cp = _remote(ins[i].at[mine], outs[i].at[me, mine], send_sems, recv_sems, 7 * i + j,
                             (*_flip(x, y, m), c))
                cp.start()
                sends.append(cp)
        for i in range(n):
            mine, _ = halves[i]
            for j, m in enumerate(CHIP_FLIPS):
                slot = outs[i].at[me ^ m, mine]
                _remote(slot, slot, send_sems, recv_sems, 7 * i + j, sib).wait_recv()
                fwd = _remote(slot, slot, send_sems, recv_sems, 7 * i + 3 + j, sib)
                fwd.start()
                sends.append(fwd)
        for i in range(n):
            _, other = halves[i]
            for j, m in enumerate(CHIP_FLIPS):
                slot = outs[i].at[me ^ m, other]
                _remote(slot, slot, send_sems, recv_sems, 7 * i + 3 + j, sib).wait_recv()
            _remote(ins[i], outs[i].at[me], send_sems, recv_sems, 7 * i + 6, sib).wait_recv()
        for cp in sends:
            cp.wait_send()

    return pl.pallas_call(
        body, in_specs=[ANY] * n, out_specs=[ANY] * n,
        out_shape=[SDS((N_CHIPS,) + sh.shape, sh.dtype) for sh in shards],
        scratch_shapes=[pltpu.SemaphoreType.DMA((7 * n,)), pltpu.SemaphoreType.DMA((7 * n,))],
        name=name,
    )(*shards)


def swap_halves(grads, name):
    n = len(grads)

    def body(*refs):
        ins, outs = refs[:n], refs[n:2 * n]
        send_sems, recv_sems = refs[2 * n:]
        x, y, c = _coords()
        cps = []
        for i in range(n):
            hr = grads[i].shape[1] // 2
            other = pl.ds(pl.multiple_of((1 - c) * hr, 8), hr)
            cp = _remote(ins[i].at[pl.ds(0, N_CHIPS), other], outs[i], send_sems, recv_sems, i, (x, y, 1 - c))
            cp.start()
            cps.append(cp)
        for cp in cps:
            cp.wait()

    return pl.pallas_call(
        body, in_specs=[ANY] * n, out_specs=[ANY] * n,
        out_shape=[SDS((N_CHIPS, g.shape[1] // 2, g.shape[2]), g.dtype) for g in grads],
        scratch_shapes=[pltpu.SemaphoreType.DMA((n,)), pltpu.SemaphoreType.DMA((n,))],
        name=name,
    )(*grads)


def _sum_rows_tile(rows, cols, itemsize=4):
    for tr in (512, 256, 128, 64, 32, 16):
        if rows % tr == 0 and tr * cols * itemsize <= 2 * 1024 * 1024:
            return tr
    raise ValueError((rows, cols))


def half_sum(g, got, c_arr, name):
    _, r, cols = g.shape
    hr = r // 2
    tr = _sum_rows_tile(hr, cols, g.dtype.itemsize)

    def my_chip():
        return 2 * lax.axis_index("x") + lax.axis_index("y")

    def body(g_ref, got_ref, o_ref, own_ref):
        p = (g_ref[...].astype(F32) + got_ref[...].astype(F32)).astype(o_ref.dtype)
        o_ref[...] = p

        @pl.when(pl.program_id(1) == my_chip())
        def _():
            own_ref[...] = p

    out = SDS((N_CHIPS, hr, cols), jnp.bfloat16)
    return pl.pallas_call(
        body, grid=(hr // tr, N_CHIPS),
        in_specs=[BS((None, None, tr, cols), lambda i, s: (s, lax.axis_index("c"), i, 0)),
                  BS((None, tr, cols), lambda i, s: (s, i, 0))],
        out_specs=[BS((None, tr, cols), lambda i, s: (s, i, 0)),
                   BS((None, tr, cols), lambda i, s: (my_chip(), i, 0))],
        out_shape=[out, out], compiler_params=_cp("parallel", "arbitrary"), name=name,
    )(g.reshape(N_CHIPS, 2, hr, cols), got)


def exchange_parts(parts, name):
    n = len(parts)

    def body(*refs):
        ins, outs = refs[:n], refs[n:2 * n]
        send_sems, recv_sems, loc_sems = refs[2 * n:]
        x, y, c = _coords()
        me = 2 * x + y
        cps, locs = [], []
        for i in range(n):
            loc = pltpu.make_async_copy(ins[i].at[me], outs[i].at[me], loc_sems.at[i])
            loc.start()
            locs.append(loc)
            for j, m in enumerate(CHIP_FLIPS):
                cp = _remote(ins[i].at[me ^ m], outs[i].at[me], send_sems, recv_sems, 3 * i + j, (*_flip(x, y, m), c))
                cp.start()
                cps.append(cp)
        for cp in cps:
            cp.wait()
        for loc in locs:
            loc.wait()

    return pl.pallas_call(
        body, in_specs=[ANY] * n, out_specs=[ANY] * n,
        out_shape=[SDS(p.shape, p.dtype) for p in parts],
        scratch_shapes=[pltpu.SemaphoreType.DMA((3 * n,)), pltpu.SemaphoreType.DMA((3 * n,)),
                        pltpu.SemaphoreType.DMA((n,))],
        name=name,
    )(*parts)


def slot_sum(slots, c_arr, name):
    _, hr, cols = slots.shape
    tr = _sum_rows_tile(hr, cols)
    nblk = hr // tr

    def body(s_ref, o_ref):
        acc = s_ref[0].astype(F32)
        for p in range(1, N_CHIPS):
            acc = acc + s_ref[p].astype(F32)
        o_ref[...] = acc

    return pl.pallas_call(
        body, grid=(nblk,), in_specs=[BS((N_CHIPS, tr, cols), lambda i: (0, i, 0))],
        out_specs=BS((tr, cols), lambda i: (lax.axis_index("c") * nblk + i, 0)),
        out_shape=SDS((2 * hr, cols), F32), compiler_params=_cp("parallel"), name=name,
    )(slots)


def share_halves(bufs, name):
    n = len(bufs)

    def body(*refs):
        outs = refs[n:2 * n]
        send_sems, recv_sems = refs[2 * n:]
        x, y, c = _coords()
        cps = []
        for i in range(n):
            hr = bufs[i].shape[0] // 2
            mine = outs[i].at[pl.ds(pl.multiple_of(c * hr, 8), hr)]
            cp = _remote(mine, mine, send_sems, recv_sems, i, (x, y, 1 - c))
            cp.start()
            cps.append(cp)
        for cp in cps:
            cp.wait()

    return pl.pallas_call(
        body, in_specs=[ANY] * n, out_specs=[ANY] * n,
        out_shape=[SDS(b.shape, b.dtype) for b in bufs],
        input_output_aliases={i: i for i in range(n)},
        scratch_shapes=[pltpu.SemaphoreType.DMA((n,)), pltpu.SemaphoreType.DMA((n,))],
        name=name,
    )(*bufs)


HBM_SPEC = pl.BlockSpec(memory_space=pltpu.HBM)
SEM_SPEC = pl.BlockSpec(memory_space=pltpu.SEMAPHORE)
EFFECT = pltpu.SideEffectType.DATAFLOW_SIDE_EFFECTING


def split_start(name, bufs, plan, n_copies):
    nb = len(bufs)

    def body(*refs):
        send_sems, recv_sems = refs[nb], refs[nb + 1]
        for k, (src, dst, dev) in enumerate(plan(refs[:nb])):
            _remote(src, dst, send_sems, recv_sems, k, dev).start()
        refs[-1][...] = jnp.zeros_like(refs[-1])

    outs = pl.pallas_call(
        body, name=name,
        out_shape=(pltpu.SemaphoreType.DMA((n_copies,)), pltpu.SemaphoreType.DMA((n_copies,)),
                   *[pltpu.HBM(b.shape, b.dtype) for b in bufs], SDS((8, LANES), F32)),
        in_specs=[HBM_SPEC] * nb, out_specs=(SEM_SPEC, SEM_SPEC, *[HBM_SPEC] * nb, VM),
        input_output_aliases={i: 2 + i for i in range(nb)},
        compiler_params=pltpu.CompilerParams(has_side_effects=EFFECT),
    )(*[pltpu.with_memory_space_constraint(b, pltpu.HBM) for b in bufs])
    return outs[0], outs[1], list(outs[2:2 + nb]), outs[-1]


def split_wait(name, send_sems, recv_sems, bufs, after, plan):
    nb = len(bufs)

    def body(*refs):
        send_ref, recv_ref = refs[nb], refs[nb + 1]
        for k, (src, dst, dev) in enumerate(plan(refs[:nb])):
            cp = _remote(src, dst, send_ref, recv_ref, k, dev)
            cp.wait_send()
            cp.wait_recv()

    outs = pl.pallas_call(
        body, name=name, out_shape=[pltpu.HBM(b.shape, b.dtype) for b in bufs],
        in_specs=[HBM_SPEC] * nb + [SEM_SPEC, SEM_SPEC, ANY], out_specs=[HBM_SPEC] * nb,
        input_output_aliases={i: i for i in range(nb)},
        compiler_params=pltpu.CompilerParams(has_side_effects=EFFECT),
    )(*bufs, send_sems, recv_sems, after)
    return list(outs)


def tie(x, tokens, name):
    def body(*refs):
        pass

    return pl.pallas_call(
        body, name=name, out_shape=SDS(x.shape, x.dtype), in_specs=[ANY] * (1 + len(tokens)), out_specs=ANY,
        input_output_aliases={0: 0},
    )(x, *tokens)


def plan_gather_ici(n, rows):
    def plan(refs):
        x, y, c = _coords()
        me = 2 * x + y
        out = []
        for i in range(n):
            hr = rows[i] // 2
            mine = pl.ds(pl.multiple_of(c * hr, 8), hr)
            out.append((refs[i], refs[n + i].at[me], (x, y, 1 - c)))
            for m in CHIP_FLIPS:
                out.append((refs[i].at[mine], refs[n + i].at[me, mine], (*_flip(x, y, m), c)))
        return out
    return plan


def plan_gather_d2d(n, rows):
    def plan(refs):
        x, y, c = _coords()
        me = 2 * x + y
        out = []
        for i in range(n):
            hr = rows[i] // 2
            mine = pl.ds(pl.multiple_of(c * hr, 8), hr)
            for m in CHIP_FLIPS:
                slot = refs[i].at[me ^ m, mine]
                out.append((slot, slot, (x, y, 1 - c)))
        return out
    return plan


def plan_swap(n, rows):
    def plan(refs):
        x, y, c = _coords()
        out = []
        for i in range(n):
            hr = rows[i] // 2
            other = pl.ds(pl.multiple_of((1 - c) * hr, 8), hr)
            out.append((refs[i].at[pl.ds(0, N_CHIPS), other], refs[n + i], (x, y, 1 - c)))
        return out
    return plan


def plan_exchange(n):
    def plan(refs):
        x, y, c = _coords()
        me = 2 * x + y
        out = []
        for i in range(n):
            for m in CHIP_FLIPS:
                out.append((refs[i].at[me ^ m], refs[n + i].at[me], (*_flip(x, y, m), c)))
        return out
    return plan


def plan_share(n, rows):
    def plan(refs):
        x, y, c = _coords()
        out = []
        for i in range(n):
            hr = rows[i] // 2
            mine = refs[i].at[pl.ds(pl.multiple_of(c * hr, 8), hr)]
            out.append((mine, mine, (x, y, 1 - c)))
        return out
    return plan


def reduce_scatter(grads, c_arr, tag):
    got = swap_halves(grads, "rs_swap_" + tag)
    parts = [half_sum(g, r, c_arr, "rs_half_sum") for g, r in zip(grads, got)]
    slots = exchange_parts(parts, "rs_exchange_" + tag)
    return share_halves([slot_sum(s, c_arr, "rs_slot_sum") for s in slots], "rs_share_" + tag)


VM = pl.BlockSpec(memory_space=pltpu.VMEM)


def small_gather(v, name):
    def body(v_ref, out_ref, send_sems, recv_sems):
        x, y, c = _coords()
        me = 2 * x + y
        out_ref[me] = v_ref[...]
        cps = []
        for j, m in enumerate(CHIP_FLIPS):
            cp = _remote(v_ref, out_ref.at[me], send_sems, recv_sems, j, (*_flip(x, y, m), c))
            cp.start()
            cps.append(cp)
        for cp in cps:
            cp.wait()

    return pl.pallas_call(
        body, in_specs=[VM], out_specs=VM, out_shape=SDS((N_CHIPS,) + v.shape, v.dtype),
        scratch_shapes=[pltpu.SemaphoreType.DMA((3,)), pltpu.SemaphoreType.DMA((3,))],
        compiler_params=pltpu.CompilerParams(vmem_limit_bytes=VMEM_LIMIT_BYTES), name=name,
    )(v)


def plan_small_swap(refs):
    x, y, c = _coords()
    return [(refs[0], refs[1], (x, y, 1 - c))]


def plan_small_exchange(refs):
    x, y, c = _coords()
    me = 2 * x + y
    return [(refs[0].at[me], refs[0].at[me], (*_flip(x, y, m), c)) for m in CHIP_FLIPS]


def small_pair(v, sib, name):
    rows, cols = v.shape
    tr = _sum_rows_tile(rows, cols)

    def body(v_ref, s_ref, o_ref):
        o_ref[...] = v_ref[...] + s_ref[...]

    blk = BS((tr, cols), lambda i: (i, 0))
    return pl.pallas_call(
        body, grid=(rows // tr,), in_specs=[blk, blk],
        out_specs=BS((None, tr, cols), lambda i: (2 * lax.axis_index("x") + lax.axis_index("y"), i, 0)),
        out_shape=SDS((N_CHIPS, rows, cols), F32), compiler_params=_cp("parallel"), name=name,
    )(v, sib)


def small_total(slots, name):
    _, rows, cols = slots.shape
    tr = _sum_rows_tile(rows, cols)

    def body(s_ref, o_ref):
        o_ref[...] = (s_ref[0] + s_ref[1]) + (s_ref[2] + s_ref[3])

    return pl.pallas_call(
        body, grid=(rows // tr,), in_specs=[BS((N_CHIPS, tr, cols), lambda i: (0, i, 0))],
        out_specs=BS((tr, cols), lambda i: (i, 0)), out_shape=SDS((rows, cols), F32),
        compiler_params=_cp("parallel"), name=name,
    )(slots)


def small_allreduce(v, name):
    def body(v_ref, out_ref, sib_buf, slots, send_sems, recv_sems):
        x, y, c = _coords()
        me = 2 * x + y
        swap = _remote(v_ref, sib_buf, send_sems, recv_sems, 0, (x, y, 1 - c))
        swap.start()
        swap.wait()
        slots[me] = v_ref[...] + sib_buf[...]
        cps = []
        for j, m in enumerate(CHIP_FLIPS):
            cp = _remote(slots.at[me], slots.at[me], send_sems, recv_sems, 1 + j, (*_flip(x, y, m), c))
            cp.start()
            cps.append(cp)
        for cp in cps:
            cp.wait()
        out_ref[...] = (slots[0] + slots[1]) + (slots[2] + slots[3])

    return pl.pallas_call(
        body, in_specs=[VM], out_specs=VM, out_shape=SDS(v.shape, v.dtype),
        scratch_shapes=[pltpu.VMEM(v.shape, v.dtype), pltpu.VMEM((N_CHIPS,) + v.shape, v.dtype),
                        pltpu.SemaphoreType.DMA((4,)), pltpu.SemaphoreType.DMA((4,))],
        compiler_params=pltpu.CompilerParams(vmem_limit_bytes=VMEM_LIMIT_BYTES), name=name,
    )(v)


def adamw(w, g_list, m, v, name):
    nl, rows, cols = w.shape
    tr = _sum_rows_tile(rows, cols) if rows % 16 == 0 else rows
    bc1 = 1.0 - ADAM_B1 ** ADAM_STEP
    bc2 = 1.0 - ADAM_B2 ** ADAM_STEP

    def body(*refs):
        w_ref, m_ref, v_ref = refs[:3]
        g_refs = refs[3:3 + nl]
        go_ref, d_ref, mo_ref, vo_ref = refs[3 + nl:]
        layer = pl.program_id(0)
        for l in range(nl):
            @pl.when(layer == l)
            def _(l=l):
                g = g_refs[l][...]
                m_new = ADAM_B1 * m_ref[...] + (1.0 - ADAM_B1) * g
                v_new = ADAM_B2 * v_ref[...] + (1.0 - ADAM_B2) * (g * g)
                m_hat = m_new / bc1
                v_hat = v_new / bc2
                go_ref[...] = g
                d_ref[...] = -ADAM_LR * (m_hat / (jnp.sqrt(v_hat) + ADAM_EPS) + ADAM_WD * w_ref[...])
                mo_ref[...] = m_new
                vo_ref[...] = v_new

    stk = BS((None, tr, cols), lambda l, i: (l, i, 0))
    flat = BS((tr, cols), lambda l, i: (i, 0))
    out = SDS((nl, rows, cols), F32)
    return pl.pallas_call(
        body, grid=(nl, rows // tr), in_specs=[stk] * 3 + [flat] * nl, out_specs=[stk] * 4,
        out_shape=[out] * 4, compiler_params=_cp("parallel", "parallel"), name=name,
    )(w, m, v, *g_list)


WEIGHTS = ["mem_norm_g", "a_pre_mix_g", "a_post_mix_g", "a_pre_ffn_g", "a_post_ffn_g", "a_w_in", "a_conv_w",
           "a_conv_b", "a_gate_a_w", "a_gate_a_b", "a_gate_x_w", "a_gate_x_b", "a_lambda", "a_w_mem_kv", "a_w_out",
           "a_w_ffn_in", "a_w_ffn_out", "kv_norm_g", "w_kv_shared", "b_pre_mix_g", "b_post_mix_g", "b_pre_ffn_g",
           "b_post_ffn_g", "b_w_in", "b_w_mem_kv", "b_w_out", "b_w_ffn_in", "b_w_ffn_out"]
BIG = {"a_w_in": True, "a_w_mem_kv": False, "a_w_out": False, "a_w_ffn_in": True, "a_w_ffn_out": False,
       "w_kv_shared": True, "b_w_in": False, "b_w_mem_kv": False, "b_w_out": False, "b_w_ffn_in": True,
       "b_w_ffn_out": False}
SHARDED_SMALL = ["a_pre_mix_g", "a_post_mix_g", "a_pre_ffn_g", "a_post_ffn_g", "a_conv_w", "a_conv_b", "a_gate_a_b",
                 "a_gate_x_b", "a_lambda"]
REPL_SMALL = ["mem_norm_g", "kv_norm_g", "b_pre_mix_g", "b_post_mix_g", "b_pre_ffn_g", "b_post_ffn_g", "a_gate_a_w",
              "a_gate_x_w"]
LANES = 128


def _pack(arrs, row_multiple=8):
    flat = jnp.concatenate([a.reshape(-1) for a in arrs])
    pad = -flat.shape[0] % (LANES * row_multiple)
    if pad:
        flat = jnp.concatenate([flat, jnp.zeros((pad,), flat.dtype)])
    return flat.reshape(-1, LANES)


def _unpack(packed, shapes):
    flat = packed.reshape(-1)
    out, pos = [], 0
    for sh in shapes:
        size = math.prod(sh)
        out.append(flat[pos:pos + size].reshape(sh))
        pos += size
    return out


def kernel(x, mem, mem_norm_g, a_pre_mix_g, a_post_mix_g, a_pre_ffn_g, a_post_ffn_g, a_w_in, a_conv_w, a_conv_b,
           a_gate_a_w, a_gate_a_b, a_gate_x_w, a_gate_x_b, a_lambda, a_w_mem_kv, a_w_out, a_w_ffn_in, a_w_ffn_out,
           kv_norm_g, w_kv_shared, b_pre_mix_g, b_post_mix_g, b_pre_ffn_g, b_post_ffn_g, b_w_in, b_w_mem_kv, b_w_out,
           b_w_ffn_in, b_w_ffn_out, loss_target, m_mem_norm_g, m_a_pre_mix_g, m_a_post_mix_g, m_a_pre_ffn_g,
           m_a_post_ffn_g, m_a_w_in, m_a_conv_w, m_a_conv_b, m_a_gate_a_w, m_a_gate_a_b, m_a_gate_x_w, m_a_gate_x_b,
           m_a_lambda, m_a_w_mem_kv, m_a_w_out, m_a_w_ffn_in, m_a_w_ffn_out, m_kv_norm_g, m_w_kv_shared, m_b_pre_mix_g,
           m_b_post_mix_g, m_b_pre_ffn_g, m_b_post_ffn_g, m_b_w_in, m_b_w_mem_kv, m_b_w_out, m_b_w_ffn_in, m_b_w_ffn_out,
           v_mem_norm_g, v_a_pre_mix_g, v_a_post_mix_g, v_a_pre_ffn_g, v_a_post_ffn_g, v_a_w_in, v_a_conv_w, v_a_conv_b,
           v_a_gate_a_w, v_a_gate_a_b, v_a_gate_x_w, v_a_gate_x_b, v_a_lambda, v_a_w_mem_kv, v_a_w_out, v_a_w_ffn_in,
           v_a_w_ffn_out, v_kv_norm_g, v_w_kv_shared, v_b_pre_mix_g, v_b_post_mix_g, v_b_pre_ffn_g, v_b_post_ffn_g,
           v_b_w_in, v_b_w_mem_kv, v_b_w_out, v_b_w_ffn_in, v_b_w_ffn_out):
    a = dict(locals())
    xi, yi, ci = _coords()
    chip = 2 * xi + yi
    c_arr = jnp.stack([ci, chip]).astype(jnp.int32)

    got = small_gather(_pack([a[n] for n in SHARDED_SMALL]), "small_gather")
    per_chip = [_unpack(got[s], [a[n].shape for n in SHARDED_SMALL]) for s in range(N_CHIPS)]
    small = {n: jnp.concatenate([per_chip[s][k] for s in range(N_CHIPS)], axis=-1)
             for k, n in enumerate(SHARDED_SMALL)}
    small.update({n: a[n] for n in REPL_SMALL})

    groups = []
    for l in range(4):
        p, j = ("a", l) if l < 2 else ("b", l - 2)
        groups.append([(p + "_" + n, j) for n in ("w_in", "w_mem_kv", "w_out")])
        groups.append([(p + "_" + n, j) for n in ("w_ffn_in", "w_ffn_out")])
    groups[3].append(("w_kv_shared", None))
    big = {n: [None, None] for n in BIG if n != "w_kv_shared"}
    gs, gb = {}, {}
    reduced = {n: [None, None] for n in BIG if n != "w_kv_shared"}

    def put(store, n, j, val):
        if j is None:
            store[n] = val
        else:
            store[n][j] = val

    class Exchange:
        def __init__(self):
            self.state = {}

        def gather_ici(self, g):
            shards = [(a[n] if j is None else a[n][j]).astype(MXU) for n, j in groups[g]]
            rows = [sh.shape[0] for sh in shards]
            lands = [lax.empty((N_CHIPS,) + sh.shape, sh.dtype) for sh in shards]
            plan = plan_gather_ici(len(shards), rows)
            ss, rs, bufs, tok = split_start("gather_ici_%d" % g, shards + lands, plan, 4 * len(shards))
            self.state["g", g] = (ss, rs, bufs, plan, rows)
            return tok

        def gather_d2d(self, g, after):
            ss, rs, bufs, plan, rows = self.state.pop(("g", g))
            n = len(rows)
            outs = split_wait("gather_ici_wait_%d" % g, ss, rs, bufs, after, plan)[n:]
            plan = plan_gather_d2d(n, rows)
            ss, rs, bufs, tok = split_start("gather_d2d_%d" % g, outs, plan, 3 * n)
            self.state["g", g] = (ss, rs, bufs, plan)
            return tok

        def gather_done(self, g, after):
            ss, rs, bufs, plan = self.state.pop(("g", g))
            outs = split_wait("gather_d2d_wait_%d" % g, ss, rs, bufs, after, plan)
            for (n, j), w in zip(groups[g], outs):
                put(big, n, j, w if BIG[n] else w.reshape(-1, w.shape[-1]))

        def rs_swap(self, g):
            grads = []
            for n, j in groups[g]:
                gr = gb[n] if j is None else gb[n][j]
                grads.append(gr if BIG[n] else gr.reshape(N_CHIPS, gr.shape[0] // N_CHIPS, gr.shape[1]))
            rows = [gr.shape[1] for gr in grads]
            lands = [lax.empty((N_CHIPS, gr.shape[1] // 2, gr.shape[2]), gr.dtype) for gr in grads]
            plan = plan_swap(len(grads), rows)
            ss, rs, bufs, tok = split_start("rs_swap_%d" % g, grads + lands, plan, len(grads))
            self.state["r", g] = (ss, rs, bufs, plan, rows)
            return tok

        def rs_exchange(self, g, after):
            ss, rs, bufs, plan, rows = self.state.pop(("r", g))
            n = len(rows)
            bufs = split_wait("rs_swap_wait_%d" % g, ss, rs, bufs, after, plan)
            sums = [half_sum(gr, got, c_arr, "rs_half_sum") for gr, got in zip(bufs[:n], bufs[n:])]
            plan = plan_exchange(n)
            ss, rs, bufs, tok = split_start("rs_exchange_%d" % g, [p for p, _ in sums] + [s for _, s in sums], plan,
                                            3 * n)
            self.state["r", g] = (ss, rs, bufs, plan, rows)
            return tok

        def rs_share(self, g, after):
            ss, rs, bufs, plan, rows = self.state.pop(("r", g))
            n = len(rows)
            slots = split_wait("rs_exchange_wait_%d" % g, ss, rs, bufs, after, plan)[n:]
            fulls = [slot_sum(s, c_arr, "rs_slot_sum") for s in slots]
            plan = plan_share(n, rows)
            ss, rs, bufs, tok = split_start("rs_share_%d" % g, fulls, plan, n)
            self.state["r", g] = (ss, rs, bufs, plan)
            return tok

        def rs_done(self, g, after):
            ss, rs, bufs, plan = self.state.pop(("r", g))
            outs = split_wait("rs_share_wait_%d" % g, ss, rs, bufs, after, plan)
            for (n, j), r in zip(groups[g], outs):
                put(reduced, n, j, r)

        def hook(self, where, l, after):
            mix, ffn = 2 * l, 2 * l + 1
            toks = []
            if where == "fwd_begin":
                if l == 0:
                    tok = self.gather_ici(mix)
                    tok = self.gather_d2d(mix, tok)
                    self.gather_done(mix, tok)
                toks.append(self.gather_ici(ffn))
            elif where == "fwd_q1":
                toks.append(self.gather_d2d(ffn, after))
            elif where == "fwd_mid":
                self.gather_done(ffn, after)
                if l < 3:
                    toks.append(self.gather_ici(mix + 2))
            elif where == "fwd_q3":
                if l < 3:
                    toks.append(self.gather_d2d(mix + 2, after))
            elif where == "fwd_end":
                if l < 3:
                    self.gather_done(mix + 2, after)
            elif where == "bwd_begin":
                if l < 3:
                    self.rs_done(ffn + 2, after)
                    toks.append(self.rs_exchange(mix + 2, after))
            elif where == "bwd_mid1":
                if l < 3:
                    toks.append(self.rs_share(mix + 2, after))
            elif where == "bwd_mid2":
                if l < 3:
                    self.rs_done(mix + 2, after)
                toks.append(self.rs_swap(ffn))
            elif where == "bwd_m1":
                toks.append(self.rs_exchange(ffn, after))
            elif where == "bwd_end":
                toks.append(self.rs_share(ffn, after))
                toks.append(self.rs_swap(mix))
                if l == 0:
                    self.rs_done(ffn, toks[0])
                    tok = self.rs_exchange(mix, toks[1])
                    tok = self.rs_share(mix, adamw_big([n for n in BIG if n.startswith("b_")], tok))
                    self.rs_done(mix, tok)
                    toks = []
            else:
                raise ValueError(where)
            return toks

    res = {}

    def adamw_big(names, token=None):
        last = None
        for n in names:
            shape = a[n].shape
            rows, cols = shape[-2], shape[-1]
            stk = (-1, rows, cols)
            grads = reduced[n] if isinstance(reduced[n], list) else [reduced[n]]
            if token is not None:
                grads = [tie(grads[0], [token], "tie_adamw_" + n)] + grads[1:]
            outs = adamw(a[n].reshape(stk), grads, a["m_" + n].reshape(stk), a["v_" + n].reshape(stk), "adamw")
            res[n] = [o.reshape(shape) for o in outs]
            last = outs[1]
            token = last if token is not None else None
        return last

    loss_parts, dx = _fwd_bwd(x[0], mem[0], loss_target[0], small, big, gs, gb, Exchange())
    loss = lax.psum(jnp.sum(loss_parts) * (0.5 / D), ("x", "y", "c"))

    def full(n):
        g = gs[n]
        return jnp.stack(g) if isinstance(g, list) else g

    order = SHARDED_SMALL + REPL_SMALL
    full_shapes = [full(n).shape for n in order]
    pack = _pack([full(n) for n in order], 512)
    ss, rs, bufs, tok = split_start("small_swap", [pack, lax.empty(pack.shape, F32)], plan_small_swap, 1)
    mine_v, sib_v = split_wait("small_swap_wait", ss, rs, bufs, tok, plan_small_swap)
    ss, rs, bufs, tok = split_start("small_exchange", [small_pair(mine_v, sib_v, "small_pair")],
                                    plan_small_exchange, 3)
    last = adamw_big([n for n in BIG if n not in res], tok)
    slots = split_wait("small_exchange_wait", ss, rs, bufs, last, plan_small_exchange)[0]
    summed = _unpack(small_total(slots, "small_total"), full_shapes)
    mine = []
    for n, g in zip(order, summed):
        if n in SHARDED_SMALL:
            width = a[n].shape[-1]
            g = lax.dynamic_slice_in_dim(g, chip * width, width, axis=g.ndim - 1)
        mine.append(g.reshape(a[n].shape))
    shapes = [a[n].shape for n in order]
    rm = 512
    outs = adamw(_pack([a[n] for n in order], rm)[None], [_pack(mine, rm)],
                 _pack([a["m_" + n] for n in order], rm)[None], _pack([a["v_" + n] for n in order], rm)[None],
                 "adamw_small")
    unpacked = [_unpack(o[0], shapes) for o in outs]
    for k, n in enumerate(order):
        res[n] = [u[k] for u in unpacked]

    return (loss, dx[None], *[res[n][0] for n in WEIGHTS], *[res[n][1] for n in WEIGHTS],
            *[res[n][2] for n in WEIGHTS], *[res[n][3] for n in WEIGHTS])
```

```python
import math

import jax
import jax.numpy as jnp
from jax import lax
from jax.experimental import pallas as pl
from jax.experimental.pallas import tpu as pltpu

D = 2048
HD = 128
MEM_W = 512
MEM_HEADS = 4
MIX_W = D - MEM_W
N_BLK = MIX_W // HD
D_FF = 5632
N_MEM = 256
RMS_EPS = 1e-6
NEG_INF = -1e30
LRU_C = 8.0
DIL_GROUPS = ((128, 1), (512, 4), (2048, 16))
Q_BLOCK = 128
SCALE = HD ** -0.5
N_CHIPS = 4
MXU_COLS = 256
ACC_CHUNK = 2 * MXU_COLS

ADAM_LR = 0.001
ADAM_B1 = 0.9
ADAM_B2 = 0.999
ADAM_EPS = 1e-08
ADAM_WD = 0.01
ADAM_STEP = 10

MXU = jnp.bfloat16
F32 = jnp.float32
VMEM_LIMIT_BYTES = 56 * 1024 * 1024

BS = pl.BlockSpec
SDS = jax.ShapeDtypeStruct
MESH = pl.DeviceIdType.MESH


def _cp(*sem):
    return pltpu.CompilerParams(dimension_semantics=sem or None, vmem_limit_bytes=VMEM_LIMIT_BYTES)


def _dot(a, b, dn=((1,), (0,))):
    return lax.dot_general(a, b, (dn, ((), ())), preferred_element_type=F32)


def _div(i, n):
    return lax.div(i, jnp.int32(n))


def _rem(i, n):
    return lax.rem(i, jnp.int32(n))


NN = ((1,), (0,))
NT = ((1,), (1,))
TN = ((0,), (0,))


def _sigmoid(z):
    return 0.5 * jnp.tanh(0.5 * z) + 0.5


def _log1p_pos(u):
    return jnp.where(u < 1e-2, u * (1.0 - u * (0.5 - u * (1.0 / 3.0))), jnp.log(1.0 + u))


def _neg_expm1(z):
    return jnp.where(z > -1e-2, -z * (1.0 + z * (0.5 + z * (1.0 / 6.0))), 1.0 - jnp.exp(z))


def _softplus(z):
    return jnp.maximum(z, 0.0) + _log1p_pos(jnp.exp(-jnp.abs(z)))


_GELU_C = math.sqrt(2.0 / math.pi)


def _gelu_and_grad(x):
    x2 = x * x
    t = jnp.tanh(_GELU_C * (x + 0.044715 * x * x2))
    g = 0.5 * x * (1.0 + t)
    dg = 0.5 * (1.0 + t) + 0.5 * x * (1.0 - t * t) * _GELU_C * (1.0 + 3.0 * 0.044715 * x2)
    return g, dg


def _row_tile(rows):
    return min(256, rows)


def norm_cast(x, g, name):
    rows = x.shape[0]
    tr = _row_tile(rows)

    def body(x_ref, g_ref, o_ref):
        xv = x_ref[...]
        r = lax.rsqrt(jnp.mean(xv * xv, axis=-1, keepdims=True) + RMS_EPS)
        o_ref[...] = (xv * r * g_ref[...]).astype(o_ref.dtype)

    return pl.pallas_call(
        body, grid=(rows // tr,),
        in_specs=[BS((tr, D), lambda i: (i, 0)), BS((1, D), lambda i: (0, 0))],
        out_specs=BS((tr, D), lambda i: (i, 0)),
        out_shape=SDS((rows, D), MXU), compiler_params=_cp("parallel"), name=name,
    )(x, g.reshape(1, D))


def resid_norm(h, y, g, name):
    rows = h.shape[0]
    tr = _row_tile(rows)

    def body(h_ref, y_ref, g_ref, o_ref):
        yv = y_ref[...]
        r = lax.rsqrt(jnp.mean(yv * yv, axis=-1, keepdims=True) + RMS_EPS)
        o_ref[...] = h_ref[...] + yv * r * g_ref[...]

    return pl.pallas_call(
        body, grid=(rows // tr,),
        in_specs=[BS((tr, D), lambda i: (i, 0)), BS((tr, D), lambda i: (i, 0)), BS((1, D), lambda i: (0, 0))],
        out_specs=BS((tr, D), lambda i: (i, 0)),
        out_shape=SDS((rows, D), F32), compiler_params=_cp("parallel"), name=name,
    )(h, y, g.reshape(1, D))


def resid_norm_next(h, y, g, g_next, name):
    rows = h.shape[0]
    tr = _row_tile(rows)

    def body(h_ref, y_ref, g_ref, gn_ref, o_ref, n_ref):
        yv = y_ref[...]
        r = lax.rsqrt(jnp.mean(yv * yv, axis=-1, keepdims=True) + RMS_EPS)
        hv = h_ref[...] + yv * r * g_ref[...]
        o_ref[...] = hv
        r2 = lax.rsqrt(jnp.mean(hv * hv, axis=-1, keepdims=True) + RMS_EPS)
        n_ref[...] = (hv * r2 * gn_ref[...]).astype(n_ref.dtype)

    row = BS((tr, D), lambda i: (i, 0))
    vec = BS((1, D), lambda i: (0, 0))
    return pl.pallas_call(
        body, grid=(rows // tr,), in_specs=[row, row, vec, vec], out_specs=[row, row],
        out_shape=[SDS((rows, D), F32), SDS((rows, D), MXU)], compiler_params=_cp("parallel"), name=name,
    )(h, y, g.reshape(1, D), g_next.reshape(1, D))


def _norm_bwd_rows(xv, gv, dyv):
    r = lax.rsqrt(jnp.mean(xv * xv, axis=-1, keepdims=True) + RMS_EPS)
    xhat = xv * r
    dxhat = dyv * gv
    dx = r * (dxhat - xhat * jnp.mean(dxhat * xhat, axis=-1, keepdims=True))
    return dx, jnp.sum(dyv * xhat, axis=0, keepdims=True)


def norm_bwd(x, g, dy, res, out_dtype, name, then=None):
    rows = x.shape[0]
    tr = _row_tile(rows)
    has_res = res is not None
    n_in = 3 + has_res + (2 if then else 0)

    def body(*refs):
        x_ref, g_ref, dy_ref = refs[:3]
        dx_ref, dg_ref = refs[n_in], refs[n_in + 1]
        dx, dg = _norm_bwd_rows(x_ref[...], g_ref[...], dy_ref[...].astype(F32))
        if has_res:
            dx = dx + refs[3][...]
        dx_ref[...] = dx.astype(dx_ref.dtype)
        first = pl.program_id(0) == 0

        @pl.when(first)
        def _():
            dg_ref[...] = jnp.zeros_like(dg_ref)

        dg_ref[...] += dg
        if then:
            x2_ref, g2_ref = refs[n_in - 2], refs[n_in - 1]
            dx2_ref, dg2_ref = refs[n_in + 2], refs[n_in + 3]
            dx2, dg2 = _norm_bwd_rows(x2_ref[...], g2_ref[...], dx)
            dx2_ref[...] = dx2.astype(dx2_ref.dtype)

            @pl.when(first)
            def _():
                dg2_ref[...] = jnp.zeros_like(dg2_ref)

            dg2_ref[...] += dg2

    row = BS((tr, D), lambda i: (i, 0))
    vec = BS((1, D), lambda i: (0, 0))
    ins = [x, g.reshape(1, D), dy] + ([res] if has_res else []) + ([then[0], then[1].reshape(1, D)] if then else [])
    outs = pl.pallas_call(
        body, grid=(rows // tr,),
        in_specs=[row, vec, row] + ([row] if has_res else []) + ([row, vec] if then else []),
        out_specs=[row, vec] + ([row, vec] if then else []),
        out_shape=[SDS((rows, D), out_dtype), SDS((1, D), F32)] + ([SDS((rows, D), MXU), SDS((1, D), F32)] if then else []),
        compiler_params=_cp("arbitrary"), name=name,
    )(*ins)
    if then:
        return outs[0], outs[1].reshape(D), outs[2], outs[3].reshape(D)
    return outs[0], outs[1].reshape(D)


def loss_head(y, target, name):
    rows = y.shape[0]
    tr = _row_tile(rows)

    def body(y_ref, t_ref, dy_ref, acc_ref):
        err = y_ref[...] - t_ref[...]
        dy_ref[...] = err * (1.0 / D)

        @pl.when(pl.program_id(0) == 0)
        def _():
            acc_ref[...] = jnp.zeros_like(acc_ref)

        acc_ref[...] += jnp.sum(err * err, axis=0, keepdims=True)

    row = BS((tr, D), lambda i: (i, 0))
    dy, acc = pl.pallas_call(
        body, grid=(rows // tr,), in_specs=[row, row],
        out_specs=[row, BS((1, D), lambda i: (0, 0))],
        out_shape=[SDS((rows, D), F32), SDS((1, D), F32)],
        compiler_params=_cp("arbitrary"), name=name,
    )(y, target)
    return acc, dy


def _mm_call(ins, in_specs, pick, dn, grid, o_spec, out_sds, name):
    gk = grid[2]
    n_in = len(ins)

    def body(*refs):
        o_ref = refs[n_in]
        k = pl.program_id(2)

        def step(a_ref, b_ref):
            acc = o_ref if (out_sds.dtype == F32 or gk == 1) else refs[n_in + 1]
            width = acc.shape[-1]
            if dn == TN or width <= ACC_CHUNK:
                chunks = [(0, width)]
            else:
                chunks = [(c0, min(c0 + ACC_CHUNK, width)) for c0 in range(0, width, ACC_CHUNK)]

            def sweep(first):
                a = a_ref[...]
                pending = None
                for c0, c1 in chunks:
                    p = _dot(a, b_ref[c0:c1, :] if dn == NT else b_ref[:, c0:c1], dn)
                    if pending is not None:
                        put(first, *pending)
                    pending = (c0, c1, p)
                put(first, *pending)

            def put(first, c0, c1, p):
                if first:
                    acc[:, c0:c1] = p.astype(acc.dtype)
                else:
                    acc[:, c0:c1] += p

            if gk == 1:
                sweep(True)
                return

            @pl.when(k == 0)
            def _():
                sweep(True)

            @pl.when(k > 0)
            def _():
                sweep(False)

            if acc is not o_ref:
                @pl.when(k == gk - 1)
                def _():
                    o_ref[...] = acc[...].astype(o_ref.dtype)

        pick(refs[:n_in], k, step)

    scratch = []
    if gk > 1 and out_sds.dtype != F32:
        scratch = [pltpu.VMEM(o_spec.block_shape[-2:], F32)]
    return pl.pallas_call(
        body, grid=grid, in_specs=in_specs, out_specs=o_spec, out_shape=out_sds,
        scratch_shapes=scratch, compiler_params=_cp("parallel", "parallel", "arbitrary"), name=name,
    )(*ins)


def _pick2(refs, k, step):
    step(refs[0], refs[1])


def mm_nn(a, w, *, tm, tn, tk, out_dtype, name):
    m, kdim = a.shape
    if w.ndim == 3:
        c = w.shape[2]
        n = N_CHIPS * c
        per = c // tn
        b_spec = BS((None, tk, tn), lambda i, j, k: (_div(j, per), k, _rem(j, per)))
    else:
        n = w.shape[1]
        b_spec = BS((tk, tn), lambda i, j, k: (k, j))
    grid = (m // tm, n // tn, kdim // tk)
    return _mm_call([a, w], [BS((tm, tk), lambda i, j, k: (i, k)), b_spec], _pick2, NN, grid,
                    BS((tm, tn), lambda i, j, k: (i, j)), SDS((m, n), out_dtype), name)


def mm_nt(a_list, w, *, tm, tn, tk, out_dtype, name):
    m = a_list[0].shape[0]
    ka = a_list[0].shape[1]
    n_a = len(a_list)
    kdim = ka * n_a
    if w.ndim == 3:
        c = w.shape[2]
        n = w.shape[1]
        per = c // tk
        b_spec = BS((None, tn, tk), lambda i, j, k: (_div(k, per), j, _rem(k, per)))
    else:
        n = w.shape[0]
        b_spec = BS((tn, tk), lambda i, j, k: (j, k))
    gk = kdim // tk
    half = gk // n_a
    grid = (m // tm, n // tn, gk)
    if n_a == 1:
        a_specs = [BS((tm, tk), lambda i, j, k: (i, k))]
        pick = lambda refs, k, step: step(refs[0], refs[1])
    else:
        a_specs = [BS((tm, tk), lambda i, j, k: (i, jnp.minimum(k, half - 1))),
                   BS((tm, tk), lambda i, j, k: (i, jnp.maximum(k - half, 0)))]

        def pick(refs, k, step):
            @pl.when(k < half)
            def _():
                step(refs[0], refs[2])

            @pl.when(k >= half)
            def _():
                step(refs[1], refs[2])

    return _mm_call(list(a_list) + [w], a_specs + [b_spec], pick, NT, grid,
                    BS((tm, tn), lambda i, j, k: (i, j)), SDS((m, n), out_dtype), name)


def mm_tn(a, b_list, *, t1, tn, ts, col_shards, name):
    s, k1 = a.shape
    nb = b_list[0].shape[1]
    n_b = len(b_list)
    n = nb * n_b
    gn = n // tn
    half = gn // n_b
    grid = (k1 // t1, gn, s // ts)
    if col_shards:
        c = n // N_CHIPS
        per = c // tn
        o_spec = BS((None, t1, tn), lambda i, j, k: (_div(j, per), i, _rem(j, per)))
        out_sds = SDS((N_CHIPS, k1, c), MXU)
    else:
        o_spec = BS((t1, tn), lambda i, j, k: (i, j))
        out_sds = SDS((k1, n), MXU)
    a_spec = BS((ts, t1), lambda i, j, k: (k, i))
    if n_b == 1:
        b_specs = [BS((ts, tn), lambda i, j, k: (k, j))]
        pick = lambda refs, k, step: step(refs[0], refs[1])
    else:
        b_specs = [BS((ts, tn), lambda i, j, k: (k, jnp.minimum(j, half - 1))),
                   BS((ts, tn), lambda i, j, k: (k, jnp.maximum(j - half, 0)))]

        def pick(refs, k, step):
            j = pl.program_id(1)

            @pl.when(j < half)
            def _():
                step(refs[0], refs[1])

            @pl.when(j >= half)
            def _():
                step(refs[0], refs[2])

    return _mm_call([a] + list(b_list), [a_spec] + b_specs, pick, TN, grid, o_spec, out_sds, name)


def ffn_in_fwd(hn, w, name):
    s = hn.shape[0]
    tm = min(512, s)
    tn = D_FF // 4

    def tail(dag_ref, dau_ref, act_ref, c0, c1, g, u):
        sg = _sigmoid(g)
        silu = g * sg
        dag_ref[:, c0:c1] = (u * sg * (1.0 + g * (1.0 - sg))).astype(dag_ref.dtype)
        dau_ref[:, c0:c1] = silu.astype(dau_ref.dtype)
        act_ref[:, c0:c1] = (silu * u).astype(act_ref.dtype)

    def body(a_ref, wg_ref, wu_ref, dag_ref, dau_ref, act_ref):
        a = a_ref[...]
        pending = None
        for c0 in range(0, tn, ACC_CHUNK):
            c1 = min(c0 + ACC_CHUNK, tn)
            g = _dot(a, wg_ref[:, c0:c1])
            u = _dot(a, wu_ref[:, c0:c1])
            if pending is not None:
                tail(dag_ref, dau_ref, act_ref, *pending)
            pending = (c0, c1, g, u)
        tail(dag_ref, dau_ref, act_ref, *pending)

    tile = BS((tm, tn), lambda j, i: (i, j))
    return pl.pallas_call(
        body, grid=(4, s // tm),
        in_specs=[BS((tm, D), lambda j, i: (i, 0)),
                  BS((None, D, tn), lambda j, i: (_div(j, 2), 0, _rem(j, 2))),
                  BS((None, D, tn), lambda j, i: (2 + _div(j, 2), 0, _rem(j, 2)))],
        out_specs=[tile, tile, tile],
        out_shape=[SDS((s, D_FF), MXU), SDS((s, D_FF), MXU), SDS((s, D_FF), MXU)],
        compiler_params=_cp("parallel", "parallel"), name=name,
    )(hn, w, w)


def ffn_act_bwd(dy, w_out, dag, dau, name):
    s = dy.shape[0]
    tm = min(512, s)
    tn = D_FF // 4

    def body(dy_ref, w_ref, dag_ref, dau_ref, dg_ref, du_ref):
        def tail(c0, c1, dact):
            dg_ref[:, c0:c1] = (dact * dag_ref[:, c0:c1].astype(F32)).astype(dg_ref.dtype)
            du_ref[:, c0:c1] = (dact * dau_ref[:, c0:c1].astype(F32)).astype(du_ref.dtype)

        dy = dy_ref[...]
        pending = None
        for c0 in range(0, tn, ACC_CHUNK):
            c1 = min(c0 + ACC_CHUNK, tn)
            dact = _dot(dy, w_ref[c0:c1, :], NT)
            if pending is not None:
                tail(*pending)
            pending = (c0, c1, dact)
        tail(*pending)

    tile = BS((tm, tn), lambda j, i: (i, j))
    return pl.pallas_call(
        body, grid=(4, s // tm),
        in_specs=[BS((tm, D), lambda j, i: (i, 0)), BS((tn, D), lambda j, i: (j, 0)), tile, tile],
        out_specs=[tile, tile],
        out_shape=[SDS((s, D_FF), MXU), SDS((s, D_FF), MXU)],
        compiler_params=_cp("parallel", "parallel"), name=name,
    )(dy, w_out, dag, dau)


LRU_T = 256
HALO = 8


def _shift_down(x, k, fill):
    rows = x.shape[0]
    idx = lax.broadcasted_iota(jnp.int32, x.shape, 0)
    return jnp.where(idx < k, fill, pltpu.roll(x, k, 0))


def _shift_up(x, k, fill):
    rows = x.shape[0]
    idx = lax.broadcasted_iota(jnp.int32, x.shape, 0)
    return jnp.where(idx >= rows - k, fill, pltpu.roll(x, rows - k, 0))


def _scan_block(a, b, carry, reverse):
    rows, cols = a.shape
    sub = 8
    in_group = lax.broadcasted_iota(jnp.int32, a.shape, 0) % sub
    for sh in (1, 2, 4):
        if reverse:
            a_s, b_s, ok = pltpu.roll(a, rows - sh, 0), pltpu.roll(b, rows - sh, 0), in_group < sub - sh
        else:
            a_s, b_s, ok = pltpu.roll(a, sh, 0), pltpu.roll(b, sh, 0), in_group >= sh
        b = jnp.where(ok, a * b_s + b, b)
        a = jnp.where(ok, a * a_s, a)
    groups = list(range(rows // sub))
    edge = 0 if reverse else sub - 1
    carry_in = {}
    for v in (reversed(groups) if reverse else groups):
        carry_in[v] = carry
        row = sub * v + edge
        carry = b[row:row + 1, :] + a[row:row + 1, :] * carry
    cin = jnp.concatenate([jnp.broadcast_to(carry_in[v], (sub, cols)) for v in groups], axis=0)
    return b + a * cin


def _conv_taps(xcat):
    rows = xcat.shape[0]
    taps = []
    for k in range(4):
        off = HALO - 3 + k
        taps.append(xcat[off:off + LRU_T] if off == HALO else pltpu.roll(xcat, rows - off, 0)[:LRU_T])
    return taps


def _gates(xc, wa_ref, ba, wx_ref, bx, lam, za_ref, zx_ref):
    xm = xc.astype(MXU)
    for n in range(N_BLK):
        sl = slice(n * HD, (n + 1) * HD)
        za_ref[:, sl] = _dot(xm[:, sl], wa_ref[n])
        zx_ref[:, sl] = _dot(xm[:, sl], wx_ref[n])
    ra = _sigmoid(za_ref[...] + ba)
    ii = _sigmoid(zx_ref[...] + bx)
    sp = _softplus(-lam)
    log_a = -LRU_C * ra * sp
    a = jnp.exp(log_a)
    mult = jnp.sqrt(_neg_expm1(2.0 * log_a))
    return ra, ii, sp, a, mult


def lru_fwd(proj, conv_w, conv_b, wa, ba, wx, bx, lam, name):
    s = proj.shape[0]
    c = MIX_W
    nblk = s // LRU_T
    hpb = LRU_T // HALO

    def body(x_ref, halo_ref, cw_ref, cb_ref, wa_ref, ba_ref, wx_ref, bx_ref, lam_ref,
             xc_ref, h_ref, carry, za_ref, zx_ref):
        i = pl.program_id(0)

        @pl.when(i == 0)
        def _():
            carry[...] = jnp.zeros_like(carry)

        halo = jnp.where(i == 0, 0.0, halo_ref[...])
        xcat = jnp.concatenate([halo, x_ref[...]], axis=0)
        taps = _conv_taps(xcat)
        xc = cb_ref[...] + sum(cw_ref[k:k + 1, :] * taps[k] for k in range(4))
        xc_ref[...] = xc
        _, ii, _, a, mult = _gates(xc, wa_ref, ba_ref[...], wx_ref, bx_ref[...], lam_ref[...], za_ref, zx_ref)
        h = _scan_block(a, mult * (ii * xc), carry[HALO - 1:HALO, :], False)
        h_ref[...] = h
        carry[...] = h[LRU_T - HALO:, :]

    def full(shape):
        return BS(shape, lambda i: (0,) * len(shape))

    blk = BS((LRU_T, c), lambda i: (i, 0))
    return pl.pallas_call(
        body, grid=(nblk,),
        in_specs=[blk, BS((HALO, c), lambda i: (jnp.maximum(i * hpb - 1, 0), 0)),
                  full((4, c)), full((1, c)), full((N_BLK, HD, HD)), full((1, c)),
                  full((N_BLK, HD, HD)), full((1, c)), full((1, c))],
        out_specs=[blk, blk],
        out_shape=[SDS((s, c), F32), SDS((s, c), F32)],
        scratch_shapes=[pltpu.VMEM((HALO, c), F32), pltpu.VMEM((LRU_T, c), F32), pltpu.VMEM((LRU_T, c), F32)],
        compiler_params=_cp("arbitrary"), name=name,
    )(proj, proj, conv_w, conv_b.reshape(1, c), wa.astype(MXU), ba.reshape(1, c), wx.astype(MXU),
      bx.reshape(1, c), lam.reshape(1, c))


def lru_mix_prep(h, proj, m, name):
    s = h.shape[0]
    tr = _row_tile(s)

    def body(h_ref, gb_ref, m_ref, o_ref):
        ge, _ = _gelu_and_grad(gb_ref[...])
        o_ref[:, :MIX_W] = (h_ref[...] * ge).astype(o_ref.dtype)
        o_ref[:, MIX_W:] = m_ref[...]

    return pl.pallas_call(
        body, grid=(s // tr,),
        in_specs=[BS((tr, MIX_W), lambda i: (i, 0)), BS((tr, MIX_W), lambda i: (i, 1)),
                  BS((tr, MEM_W), lambda i: (i, 0))],
        out_specs=BS((tr, D), lambda i: (i, 0)), out_shape=SDS((s, D), MXU),
        compiler_params=_cp("parallel"), name=name,
    )(h, proj, m)


def lru_bwd(dym, proj, xc, hl, dqm, conv_w, wa, ba, wx, bx, lam, name):
    s = proj.shape[0]
    c = MIX_W
    nblk = s // LRU_T
    hpb = LRU_T // HALO
    wa_m = wa.astype(MXU)
    wx_m = wx.astype(MXU)

    def body(dy_ref, x_ref, xhalo_ref, gb_ref, xc_ref, h_ref, hhalo_ref, dqm_ref,
             cw_ref, wa_ref, ba_ref, wx_ref, bx_ref, lam_ref,
             dproj_ref, dcw_ref, dcb_ref, dwa_ref, dba_ref, dwx_ref, dbx_ref, dlam_ref,
             g_next, a_next, dxc_next, za_ref, zx_ref, dxc_ref):
        i = pl.program_id(0)

        @pl.when(i == 0)
        def _():
            g_next[...] = jnp.zeros_like(g_next)
            a_next[...] = jnp.zeros_like(a_next)
            dxc_next[...] = jnp.zeros_like(dxc_next)
            for r in (dcw_ref, dcb_ref, dwa_ref, dba_ref, dwx_ref, dbx_ref, dlam_ref):
                r[...] = jnp.zeros_like(r)

        first = i == nblk - 1
        xc = xc_ref[...]
        lam = lam_ref[...]
        ra, ii, sp, a, mult = _gates(xc, wa_ref, ba_ref[...], wx_ref, bx_ref[...], lam, za_ref, zx_ref)
        hl_v = h_ref[...]
        ge, dge = _gelu_and_grad(gb_ref[...])
        dyl = dy_ref[...]
        dhl = dyl * ge
        dproj_ref[:, c:2 * c] = (dyl * hl_v * dge).astype(dproj_ref.dtype)
        dproj_ref[:, 2 * c:] = dqm_ref[...]

        an = _shift_up(a, 1, 0.0)
        last_row = lax.broadcasted_iota(jnp.int32, a.shape, 0) == LRU_T - 1
        an = jnp.where(last_row, a_next[0:1, :], an)
        g = _scan_block(an, dhl, g_next[0:1, :], True)
        g_next[...] = g[:HALO, :]
        a_next[...] = a[:HALO, :]

        hhalo = jnp.where(first, 0.0, hhalo_ref[...])
        h_prev = _shift_down(hl_v, 1, 0.0)
        first_row = lax.broadcasted_iota(jnp.int32, a.shape, 0) == 0
        h_prev = jnp.where(first_row, hhalo[HALO - 1:HALO, :], h_prev)
        da = g * h_prev
        ixc = ii * xc
        dmult = g * ixc
        dii = g * mult * xc
        dxc = g * mult * ii
        dlog_a = (da - dmult * a / mult) * a
        dra = dlog_a * (-LRU_C) * sp
        dlam_ref[...] += jnp.sum(dlog_a * ra, axis=0, keepdims=True) * (LRU_C * _sigmoid(-lam))
        dza = dra * ra * (1.0 - ra)
        dzx = dii * ii * (1.0 - ii)
        dba_ref[...] += jnp.sum(dza, axis=0, keepdims=True)
        dbx_ref[...] += jnp.sum(dzx, axis=0, keepdims=True)
        xm = xc.astype(MXU)
        dza_m = dza.astype(MXU)
        dzx_m = dzx.astype(MXU)
        for n in range(N_BLK):
            sl = slice(n * HD, (n + 1) * HD)
            dwa_ref[n] += _dot(xm[:, sl], dza_m[:, sl], TN)
            dwx_ref[n] += _dot(xm[:, sl], dzx_m[:, sl], TN)
            dxc_ref[:, sl] = _dot(dza_m[:, sl], wa_ref[n], NT) + _dot(dzx_m[:, sl], wx_ref[n], NT)
        dxc = dxc + dxc_ref[...]

        dcat = jnp.concatenate([dxc, dxc_next[...]], axis=0)
        rows = dcat.shape[0]
        dxb = cw_ref[3:4, :] * dxc
        for k in range(3):
            dxb = dxb + cw_ref[k:k + 1, :] * pltpu.roll(dcat, rows - (3 - k), 0)[:LRU_T]
        dproj_ref[:, :c] = dxb.astype(dproj_ref.dtype)
        dxc_next[...] = dxc[:HALO, :]

        xhalo = jnp.where(first, 0.0, xhalo_ref[...])
        taps = _conv_taps(jnp.concatenate([xhalo, x_ref[...]], axis=0))
        for k in range(4):
            dcw_ref[k:k + 1, :] += jnp.sum(dxc * taps[k], axis=0, keepdims=True)
        dcb_ref[...] += jnp.sum(dxc, axis=0, keepdims=True)

    def full(shape):
        return BS(shape, lambda i: (0,) * len(shape))

    def rev(i):
        return nblk - 1 - i

    blk0 = BS((LRU_T, c), lambda i: (rev(i), 0))
    blk1 = BS((LRU_T, c), lambda i: (rev(i), 1))
    halo = BS((HALO, c), lambda i: (jnp.maximum(rev(i) * hpb - 1, 0), 0))
    outs = pl.pallas_call(
        body, grid=(nblk,),
        in_specs=[blk0, blk0, halo, blk1, blk0, blk0, halo, BS((LRU_T, MEM_W), lambda i: (rev(i), 0)),
                  full((4, c)), full((N_BLK, HD, HD)), full((1, c)), full((N_BLK, HD, HD)), full((1, c)),
                  full((1, c))],
        out_specs=[BS((LRU_T, 2 * c + MEM_W), lambda i: (rev(i), 0)), full((4, c)), full((1, c)),
                   full((N_BLK, HD, HD)), full((1, c)), full((N_BLK, HD, HD)), full((1, c)), full((1, c))],
        out_shape=[SDS((s, 2 * c + MEM_W), MXU), SDS((4, c), F32), SDS((1, c), F32),
                   SDS((N_BLK, HD, HD), F32), SDS((1, c), F32), SDS((N_BLK, HD, HD), F32), SDS((1, c), F32),
                   SDS((1, c), F32)],
        scratch_shapes=[pltpu.VMEM((HALO, c), F32), pltpu.VMEM((HALO, c), F32), pltpu.VMEM((HALO, c), F32),
                        pltpu.VMEM((LRU_T, c), F32), pltpu.VMEM((LRU_T, c), F32), pltpu.VMEM((LRU_T, c), F32)],
        compiler_params=_cp("arbitrary"), name=name,
    )(dym, proj, proj, proj, xc, hl, hl, dqm, conv_w, wa_m, ba.reshape(1, c), wx_m, bx.reshape(1, c),
      lam.reshape(1, c))
    dproj, dcw, dcb, dwa, dba, dwx, dbx, dlam = outs
    return dproj, dcw, dcb.reshape(c), dwa, dba.reshape(c), dwx, dbx.reshape(c), dlam.reshape(c)


def _mem_probs(q, kv):
    heads = [slice(hh * HD, (hh + 1) * HD) for hh in range(MEM_HEADS)]
    sc = [_dot(q[:, sl], kv[:, sl], NT) * SCALE for sl in heads]
    e = [jnp.exp(s - jnp.max(s, axis=-1, keepdims=True)) for s in sc]
    return [x / jnp.sum(x, axis=-1, keepdims=True) for x in e]


def mem_attn_fwd(proj, q_col, kvm, name):
    s = proj.shape[0]
    tq = min(512, s)

    def body(q_ref, kv_ref, o_ref):
        q = q_ref[...].astype(MXU)
        kv = kv_ref[...]
        p = _mem_probs(q, kv)
        outs = [_dot(p[hh].astype(MXU), kv[:, MEM_W + hh * HD:MEM_W + (hh + 1) * HD]) for hh in range(MEM_HEADS)]
        o_ref[...] = jnp.concatenate(outs, axis=1).astype(o_ref.dtype)

    return pl.pallas_call(
        body, grid=(s // tq,),
        in_specs=[BS((tq, MEM_W), lambda i: (i, q_col)), BS((N_MEM, 2 * MEM_W), lambda i: (0, 0))],
        out_specs=BS((tq, MEM_W), lambda i: (i, 0)), out_shape=SDS((s, MEM_W), MXU),
        compiler_params=_cp("parallel"), name=name,
    )(proj, kvm)


def mem_attn_bwd(proj, q_col, kvm, dym, name):
    s = proj.shape[0]
    tq = min(512, s)

    def body(q_ref, kv_ref, do_ref, dq_ref, dkv_ref):
        @pl.when(pl.program_id(0) == 0)
        def _():
            dkv_ref[...] = jnp.zeros_like(dkv_ref)

        q = q_ref[...].astype(MXU)
        do = do_ref[...].astype(MXU)
        kv = kv_ref[...]
        heads = [slice(hh * HD, (hh + 1) * HD) for hh in range(MEM_HEADS)]
        p = _mem_probs(q, kv)
        dp = [_dot(do[:, sl], kv[:, MEM_W + hh * HD:MEM_W + (hh + 1) * HD], NT) for hh, sl in enumerate(heads)]
        ds = [(pp * (d - jnp.sum(pp * d, axis=-1, keepdims=True)) * SCALE).astype(MXU) for pp, d in zip(p, dp)]
        dq = [_dot(x, kv[:, sl]) for x, sl in zip(ds, heads)]
        dk = [_dot(x, q[:, sl], TN) for x, sl in zip(ds, heads)]
        dv = [_dot(pp.astype(MXU), do[:, sl], TN) for pp, sl in zip(p, heads)]
        dq_ref[...] = jnp.concatenate(dq, axis=1).astype(dq_ref.dtype)
        dkv_ref[...] += jnp.concatenate(dk + dv, axis=1)

    return pl.pallas_call(
        body, grid=(s // tq,),
        in_specs=[BS((tq, MEM_W), lambda i: (i, q_col)), BS((N_MEM, 2 * MEM_W), lambda i: (0, 0)),
                  BS((tq, MEM_W), lambda i: (i, MIX_W // MEM_W))],
        out_specs=[BS((tq, MEM_W), lambda i: (i, 0)), BS((N_MEM, 2 * MEM_W), lambda i: (0, 0))],
        out_shape=[SDS((s, MEM_W), MXU), SDS((N_MEM, 2 * MEM_W), F32)],
        compiler_params=_cp("arbitrary"), name=name,
    )(proj, kvm, dym)


def _dil_scores(q, kp, kc, n, slope_dil):
    qi = lax.broadcasted_iota(jnp.int32, (Q_BLOCK, Q_BLOCK), 0)
    ki = lax.broadcasted_iota(jnp.int32, (Q_BLOCK, Q_BLOCK), 1)
    rel_p = qi + Q_BLOCK - ki
    rel_c = qi - ki
    s_p = _dot(q, kp, NT) * SCALE - slope_dil * rel_p.astype(F32)
    s_c = _dot(q, kc, NT) * SCALE - slope_dil * rel_c.astype(F32)
    s_p = jnp.where((rel_p <= Q_BLOCK) & (n > 0), s_p, NEG_INF)
    s_c = jnp.where(rel_c >= 0, s_c, NEG_INF)
    return s_p, s_c


def _slope_dil(gi, hh):
    head = 4 * gi + hh
    return DIL_GROUPS[gi][1] * 2.0 ** (-8.0 * (head + 1.0) / N_BLK)


def _dil_operands(proj, kv, gi):
    dil = DIL_GROUPS[gi][1]
    if dil == 1:
        return proj, kv, kv, (lambda r: gi), (lambda r: gi), (lambda r: MIX_W // MEM_W + gi)
    sub = proj.shape[0] // dil

    def view(a, col):
        return a[:, col:col + MEM_W].reshape(sub, dil * MEM_W)

    same = lambda r: r
    return view(proj, gi * MEM_W), view(kv, gi * MEM_W), view(kv, MIX_W + gi * MEM_W), same, same, same


def dil_attn_fwd(proj, kv, gi, name):
    dil = DIL_GROUPS[gi][1]
    s, pw = proj.shape
    sub = s // dil
    nb = sub // Q_BLOCK
    qc, kc_ = pw // MEM_W, kv.shape[1] // MEM_W

    def body(q_ref, kp_ref, kc_ref, vp_ref, vc_ref, o_ref, lse_ref):
        n = pl.program_id(1)
        q = q_ref[...].astype(MXU)
        kp, kc, vp, vc = kp_ref[...], kc_ref[...], vp_ref[...], vc_ref[...]
        heads = [slice(hh * HD, (hh + 1) * HD) for hh in range(4)]
        sc = [_dil_scores(q[:, sl], kp[:, sl], kc[:, sl], n, _slope_dil(gi, hh)) for hh, sl in enumerate(heads)]
        mx = [jnp.maximum(jnp.max(s_p, axis=-1, keepdims=True), jnp.max(s_c, axis=-1, keepdims=True))
              for s_p, s_c in sc]
        den = [jnp.sum(jnp.exp(s_p - m), axis=-1, keepdims=True) + jnp.sum(jnp.exp(s_c - m), axis=-1, keepdims=True)
               for (s_p, s_c), m in zip(sc, mx)]
        lse = [m + jnp.log(d) for m, d in zip(mx, den)]
        pr = [(jnp.exp(s_p - l).astype(MXU), jnp.exp(s_c - l).astype(MXU)) for (s_p, s_c), l in zip(sc, lse)]
        outs = [_dot(p_p, vp[:, sl]) + _dot(p_c, vc[:, sl]) for (p_p, p_c), sl in zip(pr, heads)]
        o_ref[...] = jnp.concatenate(outs, axis=1)
        lse_ref[...] = jnp.concatenate([jnp.broadcast_to(l, (Q_BLOCK, HD)) for l in lse], axis=1)

    blk = (Q_BLOCK, MEM_W)
    prev = lambda n: jnp.maximum(n - 1, 0)
    out = BS(blk, lambda r, n: (n, r))
    qv, kview, vview, qcol, kcol, vcol = _dil_operands(proj, kv, gi)
    return pl.pallas_call(
        body, grid=(dil, nb),
        in_specs=[BS(blk, lambda r, n: (n, qcol(r))),
                  BS(blk, lambda r, n: (prev(n), kcol(r))), BS(blk, lambda r, n: (n, kcol(r))),
                  BS(blk, lambda r, n: (prev(n), vcol(r))), BS(blk, lambda r, n: (n, vcol(r)))],
        out_specs=[out, out],
        out_shape=[SDS((sub, dil * MEM_W), F32), SDS((sub, dil * MEM_W), F32)],
        compiler_params=_cp("parallel", "parallel"), name=name,
    )(qv, kview, kview, vview, vview)


def dil_attn_bwd(proj, kv, lse, do, dd, gi, name):
    dil = DIL_GROUPS[gi][1]
    s, pw = proj.shape
    sub = s // dil
    nb = sub // Q_BLOCK
    qc, kc_ = pw // MEM_W, kv.shape[1] // MEM_W

    def body(q_ref, kp_ref, kc_ref, vp_ref, vc_ref, lse_ref, do_ref, dd_ref, dq_ref, dk_ref, dv_ref, ck, cv):
        n = pl.program_id(1)

        @pl.when(n == 0)
        def _():
            ck[...] = jnp.zeros_like(ck)
            cv[...] = jnp.zeros_like(cv)

        @pl.when(n < nb)
        def _():
            q = q_ref[...].astype(MXU)
            do_m = do_ref[...].astype(MXU)
            kp, kc, vp, vc = kp_ref[...], kc_ref[...], vp_ref[...], vc_ref[...]
            lse_v, dd_v, ck_v, cv_v = lse_ref[...], dd_ref[...], ck[...], cv[...]
            heads = [slice(hh * HD, (hh + 1) * HD) for hh in range(4)]
            sc = [_dil_scores(q[:, sl], kp[:, sl], kc[:, sl], n, _slope_dil(gi, hh)) for hh, sl in enumerate(heads)]
            dp = [(_dot(do_m[:, sl], vp[:, sl], NT), _dot(do_m[:, sl], vc[:, sl], NT)) for sl in heads]
            pr = [(jnp.exp(s_p - lse_v[:, sl]), jnp.exp(s_c - lse_v[:, sl])) for (s_p, s_c), sl in zip(sc, heads)]
            ds = [((p_p * (dp_p + dd_v[:, sl]) * SCALE).astype(MXU), (p_c * (dp_c + dd_v[:, sl]) * SCALE).astype(MXU))
                  for (p_p, p_c), (dp_p, dp_c), sl in zip(pr, dp, heads)]
            pm = [(p_p.astype(MXU), p_c.astype(MXU)) for p_p, p_c in pr]
            dq = [_dot(ds_p, kp[:, sl]) + _dot(ds_c, kc[:, sl]) for (ds_p, ds_c), sl in zip(ds, heads)]
            dk = [ck_v[:, sl] + _dot(ds_p, q[:, sl], TN) for (ds_p, _), sl in zip(ds, heads)]
            dv = [cv_v[:, sl] + _dot(p_p, do_m[:, sl], TN) for (p_p, _), sl in zip(pm, heads)]
            ck_new = [_dot(ds_c, q[:, sl], TN) for (_, ds_c), sl in zip(ds, heads)]
            cv_new = [_dot(p_c, do_m[:, sl], TN) for (_, p_c), sl in zip(pm, heads)]
            dq_ref[...] = jnp.concatenate(dq, axis=1).astype(dq_ref.dtype)
            dk_ref[...] = jnp.concatenate(dk, axis=1)
            dv_ref[...] = jnp.concatenate(dv, axis=1)
            ck[...] = jnp.concatenate(ck_new, axis=1)
            cv[...] = jnp.concatenate(cv_new, axis=1)

        @pl.when(n == nb)
        def _():
            dk_ref[...] = ck[...]
            dv_ref[...] = cv[...]

    blk = (Q_BLOCK, MEM_W)
    cur = lambda n: jnp.minimum(n, nb - 1)
    prev = lambda n: jnp.maximum(jnp.minimum(n, nb - 1) - 1, 0)
    done = lambda n: jnp.maximum(n - 1, 0)
    own = BS(blk, lambda r, n: (cur(n), r))
    qv, kview, vview, qcol, kcol, vcol = _dil_operands(proj, kv, gi)
    return pl.pallas_call(
        body, grid=(dil, nb + 1),
        in_specs=[BS(blk, lambda r, n: (cur(n), qcol(r))),
                  BS(blk, lambda r, n: (prev(n), kcol(r))), BS(blk, lambda r, n: (cur(n), kcol(r))),
                  BS(blk, lambda r, n: (prev(n), vcol(r))), BS(blk, lambda r, n: (cur(n), vcol(r))),
                  own, own, own],
        out_specs=[own, BS(blk, lambda r, n: (done(n), r)), BS(blk, lambda r, n: (done(n), r))],
        out_shape=[SDS((sub, dil * MEM_W), MXU), SDS((sub, dil * MEM_W), F32), SDS((sub, dil * MEM_W), F32)],
        scratch_shapes=[pltpu.VMEM(blk, F32), pltpu.VMEM(blk, F32)],
        compiler_params=_cp("parallel", "arbitrary"), name=name,
    )(qv, kview, kview, vview, vview, lse, do, dd)


def _group_weights(lse_refs):
    l0, l1, l2 = (r[...] for r in lse_refs)
    mx = jnp.maximum(jnp.maximum(l0, l1), l2)
    e = [jnp.exp(l - mx) for l in (l0, l1, l2)]
    den = e[0] + e[1] + e[2]
    return [x / den for x in e]


def dil_mix_prep(o_list, lse_list, m, name):
    s = m.shape[0]
    tr = _row_tile(s)

    def body(o0, o1, o2, l0, l1, l2, m_ref, out_ref):
        w = _group_weights((l0, l1, l2))
        for g, o_ref in enumerate((o0, o1, o2)):
            out_ref[:, g * MEM_W:(g + 1) * MEM_W] = (o_ref[...] * w[g]).astype(out_ref.dtype)
        out_ref[:, MIX_W:] = m_ref[...]

    blk = BS((tr, MEM_W), lambda i: (i, 0))
    return pl.pallas_call(
        body, grid=(s // tr,), in_specs=[blk] * 7,
        out_specs=BS((tr, D), lambda i: (i, 0)), out_shape=SDS((s, D), MXU),
        compiler_params=_cp("parallel"), name=name,
    )(*o_list, *lse_list, m)


def dil_mix_bwd(dym, o_list, lse_list, name):
    s = dym.shape[0]
    tr = _row_tile(s)

    def body(da_ref, o0, o1, o2, l0, l1, l2, do0, do1, do2, dd0, dd1, dd2):
        w = _group_weights((l0, l1, l2))
        tot = None
        for g, (o_ref, do_ref) in enumerate(zip((o0, o1, o2), (do0, do1, do2))):
            da = da_ref[:, g * MEM_W:(g + 1) * MEM_W]
            do_ref[...] = da * w[g]
            x = da * o_ref[...]
            dw = jnp.concatenate(
                [jnp.broadcast_to(jnp.sum(x[:, hh * HD:(hh + 1) * HD], axis=-1, keepdims=True), (tr, HD))
                 for hh in range(4)], axis=1)
            tot = w[g] * dw if tot is None else tot + w[g] * dw
        for g, dd_ref in enumerate((dd0, dd1, dd2)):
            dd_ref[...] = -w[g] * tot

    blk = BS((tr, MEM_W), lambda i: (i, 0))
    outs = pl.pallas_call(
        body, grid=(s // tr,), in_specs=[BS((tr, MIX_W), lambda i: (i, 0))] + [blk] * 6,
        out_specs=[blk] * 6, out_shape=[SDS((s, MEM_W), F32)] * 6,
        compiler_params=_cp("parallel"), name=name,
    )(dym, *o_list, *lse_list)
    return outs[:3], outs[3:]


def sum_cast(parts, name):
    s = parts[0][0].shape[0]
    tr = _row_tile(s)
    flat = [a for p in parts for a in p]
    sizes = [len(p) for p in parts]

    def body(*refs):
        out_ref = refs[-1]
        pos = 0
        for j, n in enumerate(sizes):
            acc = refs[pos][...].astype(F32)
            for t in range(1, n):
                acc = acc + refs[pos + t][...].astype(F32)
            out_ref[:, j * MEM_W:(j + 1) * MEM_W] = acc.astype(out_ref.dtype)
            pos += n

    blk = BS((tr, MEM_W), lambda i: (i, 0))
    width = MEM_W * len(parts)
    return pl.pallas_call(
        body, grid=(s // tr,), in_specs=[blk] * len(flat),
        out_specs=BS((tr, width), lambda i: (i, 0)), out_shape=SDS((s, width), MXU),
        compiler_params=_cp("parallel"), name=name,
    )(*flat)


def add_n(arrs, name):
    rows, cols = arrs[0].shape
    tr = _row_tile(rows)

    def body(*refs):
        acc = refs[0][...]
        for r in refs[1:-1]:
            acc = acc + r[...]
        refs[-1][...] = acc

    blk = BS((tr, cols), lambda i: (i, 0))
    return pl.pallas_call(
        body, grid=(rows // tr,), in_specs=[blk] * len(arrs), out_specs=blk,
        out_shape=SDS((rows, cols), F32), compiler_params=_cp("parallel"), name=name,
    )(*arrs)


class _NoExchange:
    def hook(self, where, l, after):
        return []


def _fwd_bwd(x, mem, target, small, big, gs, gb, sched):
    s = x.shape[0]
    tm = min(1024, s)
    ts = min(2048, s)

    def after_hook(arr, where, l, after):
        toks = sched.hook(where, l, after)
        return tie(arr, toks, "tie_%s_%d" % (where, l)) if toks else arr

    h = x
    saved = []
    kv = None
    mem_n = None
    hn = norm_cast(h, small["a_pre_mix_g"][0], "pre_norm")
    for l in range(4):
        rec = l < 2
        p, j = ("a", l) if rec else ("b", l - 2)
        sv = {"h": h}
        hn = after_hook(hn, "fwd_begin", l, h)
        if mem_n is None:
            mem_n = norm_cast(mem, small["mem_norm_g"], "mem_norm")
        kvm = mm_nn(mem_n, big[p + "_w_mem_kv"][j], tm=N_MEM, tn=2 * MEM_W, tk=D, out_dtype=MXU, name="mem_kv")
        if rec:
            proj = mm_nn(hn, big["a_w_in"][j], tm=min(2 * tm, s), tn=896, tk=D, out_dtype=F32, name="rec_in")
            xc, hl = lru_fwd(proj, small["a_conv_w"][j], small["a_conv_b"][j], small["a_gate_a_w"][j],
                             small["a_gate_a_b"][j], small["a_gate_x_w"][j], small["a_gate_x_b"][j],
                             small["a_lambda"][j], "lru_fwd")
            m = mem_attn_fwd(proj, 2 * MIX_W // MEM_W, kvm, "rec_mem_attn")
            ym = lru_mix_prep(hl, proj, m, "lru_mix_prep")
            sv.update(xc=xc, hl=hl)
        else:
            proj = mm_nn(hn, big["b_w_in"][j], tm=tm, tn=D, tk=D, out_dtype=F32, name="dil_in")
            o_list, lse_list = [], []
            for gi in range(3):
                o, lse = dil_attn_fwd(proj, kv, gi, "dil_attn_fwd%d" % gi)
                o_list.append(o.reshape(s, MEM_W))
                lse_list.append(lse.reshape(s, MEM_W))
            m = mem_attn_fwd(proj, MIX_W // MEM_W, kvm, "dil_mem_attn")
            ym = dil_mix_prep(o_list, lse_list, m, "dil_mix_prep")
            sv.update(o=o_list, lse=lse_list)
        ym = after_hook(ym, "fwd_q1", l, ym)
        mix = mm_nn(ym, big[p + "_w_out"][j], tm=tm, tn=D, tk=D, out_dtype=F32, name="mix_out")
        h1, hn2 = resid_norm_next(h, mix, small[p + "_post_mix_g"][j], small[p + "_pre_ffn_g"][j], "post_pre_norm")
        hn2 = after_hook(hn2, "fwd_mid", l, mix)
        g, u, act = ffn_in_fwd(hn2, big[p + "_w_ffn_in"][j], "ffn_in")
        act = after_hook(act, "fwd_q3", l, u)
        y2 = mm_nn(act, big[p + "_w_ffn_out"][j], tm=tm // 2, tn=D, tk=D_FF // 2, out_dtype=F32, name="ffn_out")
        sv.update(kvm=kvm, hn=hn, proj=proj, ym=ym, mix=mix, h1=h1, hn2=hn2, g=g, u=u, act=act, y2=y2)
        saved.append(sv)
        if l < 3:
            pn, jn = ("a", l + 1) if l + 1 < 2 else ("b", l - 1)
            h, hn = resid_norm_next(h1, y2, small[p + "_post_ffn_g"][j], small[pn + "_pre_mix_g"][jn],
                                    "post_pre_norm")
        else:
            h = resid_norm(h1, y2, small[p + "_post_ffn_g"][j], "post_norm")
        sched.hook("fwd_end", l, h)
        if l == 1:
            h_kv = h
            kvn = norm_cast(h, small["kv_norm_g"], "pre_norm")
            kv = mm_nn(kvn, big["w_kv_shared"], tm=tm, tn=768, tk=D, out_dtype=MXU, name="kv_proj")

    loss_parts, dh = loss_head(h, target, "loss_head")

    def stack2(name, j, val):
        gs.setdefault(name, [None, None])[j] = val

    def stack2b(name, j, val):
        gb.setdefault(name, [None, None])[j] = val

    dkv_parts = []
    ahead = []
    dmem_parts = []
    dkvm = [None] * 4
    for l in (3, 2, 1, 0):
        rec = l < 2
        p, j = ("a", l) if rec else ("b", l - 2)
        sv = saved[l]
        if l == 1:
            dkv = sum_cast([(dkv_parts[0][c], dkv_parts[1][c]) for c in range(6)], "dkv_sum")
            dkvn = mm_nt([dkv], big["w_kv_shared"], tm=tm, tn=D, tk=768, out_dtype=F32, name="kv_proj_dx")
            gb["w_kv_shared"] = mm_tn(kvn, [dkv], t1=D, tn=768, ts=ts, col_shards=True, name="kv_proj_dw")
            dh, gs["kv_norm_g"], *ahead = norm_bwd(h_kv, small["kv_norm_g"], dkvn, dh, F32, "pre_post_norm_bwd",
                                                   then=(sv["y2"], small["a_post_ffn_g"][1]))
        if ahead:
            dy2, dg = ahead
            ahead = []
        else:
            dy2, dg = norm_bwd(sv["y2"], small[p + "_post_ffn_g"][j], dh, None, MXU, "post_norm_bwd")
        dy2 = after_hook(dy2, "bwd_begin", l, dh)
        stack2(p + "_post_ffn_g", j, dg)
        dgg, dgu = ffn_act_bwd(dy2, big[p + "_w_ffn_out"][j], sv["g"], sv["u"], "ffn_act_bwd")
        dgg = after_hook(dgg, "bwd_mid1", l, dgu)
        stack2b(p + "_w_ffn_out", j, mm_tn(sv["act"], [dy2], t1=D_FF // 4, tn=D, ts=ts // 2, col_shards=False,
                                          name="ffn_out_dw"))
        dhn2 = mm_nt([dgg, dgu], big[p + "_w_ffn_in"][j], tm=tm // 2, tn=D, tk=D_FF // 2, out_dtype=F32,
                     name="ffn_in_dx")
        stack2b(p + "_w_ffn_in", j, mm_tn(sv["hn2"], [dgg, dgu], t1=D // 2, tn=D_FF // 4, ts=ts, col_shards=True,
                                         name="ffn_in_dw"))
        dhn2 = after_hook(dhn2, "bwd_mid2", l, gb[p + "_w_ffn_in"][j])
        dh1, dg, dmix, dg_mix = norm_bwd(sv["h1"], small[p + "_pre_ffn_g"][j], dhn2, dh, F32, "pre_post_norm_bwd",
                                         then=(sv["mix"], small[p + "_post_mix_g"][j]))
        stack2(p + "_pre_ffn_g", j, dg)
        stack2(p + "_post_mix_g", j, dg_mix)
        dym = mm_nt([dmix], big[p + "_w_out"][j], tm=tm, tn=D, tk=D, out_dtype=F32, name="mix_out_dx")
        stack2b(p + "_w_out", j, mm_tn(sv["ym"], [dmix], t1=D, tn=1024, ts=ts, col_shards=False,
                                      name="mix_out_dw"))
        dym = after_hook(dym, "bwd_m1", l, gb[p + "_w_out"][j])
        if rec:
            dqm, dkvm[l] = mem_attn_bwd(sv["proj"], 2 * MIX_W // MEM_W, sv["kvm"], dym, "rec_mem_attn_bwd")
            dproj, dcw, dcb, dwa, dba, dwx, dbx, dlam = lru_bwd(
                dym, sv["proj"], sv["xc"], sv["hl"], dqm, small["a_conv_w"][j], small["a_gate_a_w"][j],
                small["a_gate_a_b"][j], small["a_gate_x_w"][j], small["a_gate_x_b"][j], small["a_lambda"][j],
                "lru_bwd")
            for nm, val in (("a_conv_w", dcw), ("a_conv_b", dcb), ("a_gate_a_w", dwa), ("a_gate_a_b", dba),
                            ("a_gate_x_w", dwx), ("a_gate_x_b", dbx), ("a_lambda", dlam)):
                stack2(nm, j, val)
            dhn = mm_nt([dproj], big["a_w_in"][j], tm=tm, tn=D, tk=896, out_dtype=F32, name="rec_in_dx")
            stack2b("a_w_in", j, mm_tn(sv["hn"], [dproj], t1=D, tn=896, ts=ts, col_shards=True, name="rec_in_dw"))
        else:
            dqm, dkvm[l] = mem_attn_bwd(sv["proj"], MIX_W // MEM_W, sv["kvm"], dym, "dil_mem_attn_bwd")
            do_list, dd_list = dil_mix_bwd(dym, sv["o"], sv["lse"], "dil_mix_bwd")
            dq_list, dk_list, dv_list = [], [], []
            for gi in range(3):
                dil = DIL_GROUPS[gi][1]
                view = (s // dil, dil * MEM_W)
                dq, dk, dv = dil_attn_bwd(sv["proj"], kv, sv["lse"][gi].reshape(view), do_list[gi].reshape(view),
                                          dd_list[gi].reshape(view), gi, "dil_attn_bwd%d" % gi)
                dq_list.append(dq.reshape(s, MEM_W))
                dk_list.append(dk.reshape(s, MEM_W))
                dv_list.append(dv.reshape(s, MEM_W))
            dkv_parts.append(dk_list + dv_list)
            dproj = sum_cast([(a,) for a in dq_list + [dqm]], "dil_dproj")
            dhn = mm_nt([dproj], big["b_w_in"][j], tm=tm, tn=D, tk=D, out_dtype=F32, name="dil_in_dx")
            stack2b("b_w_in", j, mm_tn(sv["hn"], [dproj], t1=D, tn=1024, ts=ts, col_shards=False, name="dil_in_dw"))
        dk_m = dkvm[l].astype(MXU)
        dmem_parts.append(mm_nt([dk_m], big[p + "_w_mem_kv"][j], tm=N_MEM, tn=D, tk=2 * MEM_W, out_dtype=F32,
                                name="mem_kv_dx"))
        stack2b(p + "_w_mem_kv", j, mm_tn(mem_n, [dk_m], t1=D, tn=2 * MEM_W, ts=N_MEM, col_shards=False,
                                         name="mem_kv_dw"))
        if l in (3, 1):
            pn, jn = ("b", 0) if l == 3 else ("a", 0)
            dh, dg, *ahead = norm_bwd(sv["h"], small[p + "_pre_mix_g"][j], dhn, dh1, F32, "pre_post_norm_bwd",
                                      then=(saved[l - 1]["y2"], small[pn + "_post_ffn_g"][jn]))
        else:
            dh, dg = norm_bwd(sv["h"], small[p + "_pre_mix_g"][j], dhn, dh1, F32, "pre_norm_bwd")
        stack2(p + "_pre_mix_g", j, dg)
        dh = after_hook(dh, "bwd_end", l, dh)

    _, gs["mem_norm_g"] = norm_bwd(mem, small["mem_norm_g"], add_n(dmem_parts, "dmem_sum"), None, F32,
                                   "mem_norm_bwd")
    return loss_parts, dh


ANY = pl.BlockSpec(memory_space=pl.ANY)
CHIP_FLIPS = (1, 2, 3)


def _coords():
    return lax.axis_index("x"), lax.axis_index("y"), lax.axis_index("c")


def _flip(x, y, m):
    return x ^ (m >> 1), y ^ (m & 1)


def _remote(src, dst, send_sems, recv_sems, k, device):
    return pltpu.make_async_remote_copy(src_ref=src, dst_ref=dst, send_sem=send_sems.at[k], recv_sem=recv_sems.at[k],
                                        device_id=device, device_id_type=MESH)


def _sum_rows_tile(rows, cols, itemsize=4):
    for tr in (512, 256, 128, 64, 32, 16):
        if rows % tr == 0 and tr * cols * itemsize <= 2 * 1024 * 1024:
            return tr
    raise ValueError((rows, cols))


def half_sum(g, got, name):
    _, r, cols = g.shape
    hr = r // 2
    tr = _sum_rows_tile(hr, cols, g.dtype.itemsize)

    def my_chip():
        return 2 * lax.axis_index("x") + lax.axis_index("y")

    def body(g_ref, got_ref, o_ref, own_ref):
        p = (g_ref[...].astype(F32) + got_ref[...].astype(F32)).astype(o_ref.dtype)
        o_ref[...] = p

        @pl.when(pl.program_id(1) == my_chip())
        def _():
            own_ref[...] = p

    out = SDS((N_CHIPS, hr, cols), jnp.bfloat16)
    return pl.pallas_call(
        body, grid=(hr // tr, N_CHIPS),
        in_specs=[BS((None, None, tr, cols), lambda i, s: (s, lax.axis_index("c"), i, 0)),
                  BS((None, tr, cols), lambda i, s: (s, i, 0))],
        out_specs=[BS((None, tr, cols), lambda i, s: (s, i, 0)),
                   BS((None, tr, cols), lambda i, s: (my_chip(), i, 0))],
        out_shape=[out, out], compiler_params=_cp("parallel", "arbitrary"), name=name,
    )(g.reshape(N_CHIPS, 2, hr, cols), got)


def slot_sum(slots, name):
    _, hr, cols = slots.shape
    tr = _sum_rows_tile(hr, cols)
    nblk = hr // tr

    def body(s_ref, o_ref):
        acc = s_ref[0].astype(F32)
        for p in range(1, N_CHIPS):
            acc = acc + s_ref[p].astype(F32)
        o_ref[...] = acc

    return pl.pallas_call(
        body, grid=(nblk,), in_specs=[BS((N_CHIPS, tr, cols), lambda i: (0, i, 0))],
        out_specs=BS((tr, cols), lambda i: (lax.axis_index("c") * nblk + i, 0)),
        out_shape=SDS((2 * hr, cols), F32), compiler_params=_cp("parallel"), name=name,
    )(slots)


HBM_SPEC = pl.BlockSpec(memory_space=pltpu.HBM)
SEM_SPEC = pl.BlockSpec(memory_space=pltpu.SEMAPHORE)
EFFECT = pltpu.SideEffectType.DATAFLOW_SIDE_EFFECTING


def split_start(name, bufs, plan, n_copies):
    nb = len(bufs)

    def body(*refs):
        send_sems, recv_sems = refs[nb], refs[nb + 1]
        for k, (src, dst, dev) in enumerate(plan(refs[:nb])):
            _remote(src, dst, send_sems, recv_sems, k, dev).start()
        refs[-1][...] = jnp.zeros_like(refs[-1])

    outs = pl.pallas_call(
        body, name=name,
        out_shape=(pltpu.SemaphoreType.DMA((n_copies,)), pltpu.SemaphoreType.DMA((n_copies,)),
                   *[pltpu.HBM(b.shape, b.dtype) for b in bufs], SDS((8, LANES), F32)),
        in_specs=[HBM_SPEC] * nb, out_specs=(SEM_SPEC, SEM_SPEC, *[HBM_SPEC] * nb, VM),
        input_output_aliases={i: 2 + i for i in range(nb)},
        compiler_params=pltpu.CompilerParams(has_side_effects=EFFECT),
    )(*[pltpu.with_memory_space_constraint(b, pltpu.HBM) for b in bufs])
    return outs[0], outs[1], list(outs[2:2 + nb]), outs[-1]


def split_wait(name, send_sems, recv_sems, bufs, after, plan):
    nb = len(bufs)

    def body(*refs):
        send_ref, recv_ref = refs[nb], refs[nb + 1]
        for k, (src, dst, dev) in enumerate(plan(refs[:nb])):
            cp = _remote(src, dst, send_ref, recv_ref, k, dev)
            cp.wait_send()
            cp.wait_recv()

    outs = pl.pallas_call(
        body, name=name, out_shape=[pltpu.HBM(b.shape, b.dtype) for b in bufs],
        in_specs=[HBM_SPEC] * nb + [SEM_SPEC, SEM_SPEC, ANY], out_specs=[HBM_SPEC] * nb,
        input_output_aliases={i: i for i in range(nb)},
        compiler_params=pltpu.CompilerParams(has_side_effects=EFFECT),
    )(*bufs, send_sems, recv_sems, after)
    return list(outs)


def tie(x, tokens, name):
    def body(*refs):
        pass

    return pl.pallas_call(
        body, name=name, out_shape=SDS(x.shape, x.dtype), in_specs=[ANY] * (1 + len(tokens)), out_specs=ANY,
        input_output_aliases={0: 0},
    )(x, *tokens)


def plan_gather_ici(n, rows):
    def plan(refs):
        x, y, c = _coords()
        me = 2 * x + y
        out = []
        for i in range(n):
            hr = rows[i] // 2
            mine = pl.ds(pl.multiple_of(c * hr, 8), hr)
            out.append((refs[i], refs[n + i].at[me], (x, y, 1 - c)))
            for m in CHIP_FLIPS:
                out.append((refs[i].at[mine], refs[n + i].at[me, mine], (*_flip(x, y, m), c)))
        return out
    return plan


def plan_gather_d2d(n, rows):
    def plan(refs):
        x, y, c = _coords()
        me = 2 * x + y
        out = []
        for i in range(n):
            hr = rows[i] // 2
            mine = pl.ds(pl.multiple_of(c * hr, 8), hr)
            for m in CHIP_FLIPS:
                slot = refs[i].at[me ^ m, mine]
                out.append((slot, slot, (x, y, 1 - c)))
        return out
    return plan


def plan_swap(n, rows):
    def plan(refs):
        x, y, c = _coords()
        out = []
        for i in range(n):
            hr = rows[i] // 2
            other = pl.ds(pl.multiple_of((1 - c) * hr, 8), hr)
            out.append((refs[i].at[pl.ds(0, N_CHIPS), other], refs[n + i], (x, y, 1 - c)))
        return out
    return plan


def plan_exchange(n):
    def plan(refs):
        x, y, c = _coords()
        me = 2 * x + y
        out = []
        for i in range(n):
            for m in CHIP_FLIPS:
                out.append((refs[i].at[me ^ m], refs[n + i].at[me], (*_flip(x, y, m), c)))
        return out
    return plan


def plan_share(n, rows):
    def plan(refs):
        x, y, c = _coords()
        out = []
        for i in range(n):
            hr = rows[i] // 2
            mine = refs[i].at[pl.ds(pl.multiple_of(c * hr, 8), hr)]
            out.append((mine, mine, (x, y, 1 - c)))
        return out
    return plan


VM = pl.BlockSpec(memory_space=pltpu.VMEM)


def small_gather(v, name):
    def body(v_ref, out_ref, send_sems, recv_sems):
        x, y, c = _coords()
        me = 2 * x + y
        out_ref[me] = v_ref[...]
        cps = []
        for j, m in enumerate(CHIP_FLIPS):
            cp = _remote(v_ref, out_ref.at[me], send_sems, recv_sems, j, (*_flip(x, y, m), c))
            cp.start()
            cps.append(cp)
        for cp in cps:
            cp.wait()

    return pl.pallas_call(
        body, in_specs=[VM], out_specs=VM, out_shape=SDS((N_CHIPS,) + v.shape, v.dtype),
        scratch_shapes=[pltpu.SemaphoreType.DMA((3,)), pltpu.SemaphoreType.DMA((3,))],
        compiler_params=pltpu.CompilerParams(vmem_limit_bytes=VMEM_LIMIT_BYTES), name=name,
    )(v)


def plan_small_swap(refs):
    x, y, c = _coords()
    return [(refs[0], refs[1], (x, y, 1 - c))]


def plan_small_exchange(refs):
    x, y, c = _coords()
    me = 2 * x + y
    return [(refs[0].at[me], refs[0].at[me], (*_flip(x, y, m), c)) for m in CHIP_FLIPS]


def small_pair(v, sib, name):
    rows, cols = v.shape
    tr = _sum_rows_tile(rows, cols)

    def body(v_ref, s_ref, o_ref):
        o_ref[...] = v_ref[...] + s_ref[...]

    blk = BS((tr, cols), lambda i: (i, 0))
    return pl.pallas_call(
        body, grid=(rows // tr,), in_specs=[blk, blk],
        out_specs=BS((None, tr, cols), lambda i: (2 * lax.axis_index("x") + lax.axis_index("y"), i, 0)),
        out_shape=SDS((N_CHIPS, rows, cols), F32), compiler_params=_cp("parallel"), name=name,
    )(v, sib)


def small_total(slots, name):
    _, rows, cols = slots.shape
    tr = _sum_rows_tile(rows, cols)

    def body(s_ref, o_ref):
        o_ref[...] = (s_ref[0] + s_ref[1]) + (s_ref[2] + s_ref[3])

    return pl.pallas_call(
        body, grid=(rows // tr,), in_specs=[BS((N_CHIPS, tr, cols), lambda i: (0, i, 0))],
        out_specs=BS((tr, cols), lambda i: (i, 0)), out_shape=SDS((rows, cols), F32),
        compiler_params=_cp("parallel"), name=name,
    )(slots)


def adamw(w, g_list, m, v, name):
    nl, rows, cols = w.shape
    tr = _sum_rows_tile(rows, cols) if rows % 16 == 0 else rows
    bc1 = 1.0 - ADAM_B1 ** ADAM_STEP
    bc2 = 1.0 - ADAM_B2 ** ADAM_STEP

    def body(*refs):
        w_ref, m_ref, v_ref = refs[:3]
        g_refs = refs[3:3 + nl]
        go_ref, d_ref, mo_ref, vo_ref = refs[3 + nl:]
        layer = pl.program_id(0)
        for l in range(nl):
            @pl.when(layer == l)
            def _(l=l):
                g = g_refs[l][...]
                m_new = ADAM_B1 * m_ref[...] + (1.0 - ADAM_B1) * g
                v_new = ADAM_B2 * v_ref[...] + (1.0 - ADAM_B2) * (g * g)
                m_hat = m_new / bc1
                v_hat = v_new / bc2
                go_ref[...] = g
                d_ref[...] = -ADAM_LR * (m_hat / (jnp.sqrt(v_hat) + ADAM_EPS) + ADAM_WD * w_ref[...])
                mo_ref[...] = m_new
                vo_ref[...] = v_new

    stk = BS((None, tr, cols), lambda l, i: (l, i, 0))
    flat = BS((tr, cols), lambda l, i: (i, 0))
    out = SDS((nl, rows, cols), F32)
    return pl.pallas_call(
        body, grid=(nl, rows // tr), in_specs=[stk] * 3 + [flat] * nl, out_specs=[stk] * 4,
        out_shape=[out] * 4, compiler_params=_cp("parallel", "parallel"), name=name,
    )(w, m, v, *g_list)


WEIGHTS = ["mem_norm_g", "a_pre_mix_g", "a_post_mix_g", "a_pre_ffn_g", "a_post_ffn_g", "a_w_in", "a_conv_w",
           "a_conv_b", "a_gate_a_w", "a_gate_a_b", "a_gate_x_w", "a_gate_x_b", "a_lambda", "a_w_mem_kv", "a_w_out",
           "a_w_ffn_in", "a_w_ffn_out", "kv_norm_g", "w_kv_shared", "b_pre_mix_g", "b_post_mix_g", "b_pre_ffn_g",
           "b_post_ffn_g", "b_w_in", "b_w_mem_kv", "b_w_out", "b_w_ffn_in", "b_w_ffn_out"]
BIG = {"a_w_in": True, "a_w_mem_kv": False, "a_w_out": False, "a_w_ffn_in": True, "a_w_ffn_out": False,
       "w_kv_shared": True, "b_w_in": False, "b_w_mem_kv": False, "b_w_out": False, "b_w_ffn_in": True,
       "b_w_ffn_out": False}
SHARDED_SMALL = ["a_pre_mix_g", "a_post_mix_g", "a_pre_ffn_g", "a_post_ffn_g", "a_conv_w", "a_conv_b", "a_gate_a_b",
                 "a_gate_x_b", "a_lambda"]
REPL_SMALL = ["mem_norm_g", "kv_norm_g", "b_pre_mix_g", "b_post_mix_g", "b_pre_ffn_g", "b_post_ffn_g", "a_gate_a_w",
              "a_gate_x_w"]
LANES = 128


def _pack(arrs, row_multiple=8):
    flat = jnp.concatenate([a.reshape(-1) for a in arrs])
    pad = -flat.shape[0] % (LANES * row_multiple)
    if pad:
        flat = jnp.concatenate([flat, jnp.zeros((pad,), flat.dtype)])
    return flat.reshape(-1, LANES)


def _unpack(packed, shapes):
    flat = packed.reshape(-1)
    out, pos = [], 0
    for sh in shapes:
        size = math.prod(sh)
        out.append(flat[pos:pos + size].reshape(sh))
        pos += size
    return out


def kernel(x, mem, mem_norm_g, a_pre_mix_g, a_post_mix_g, a_pre_ffn_g, a_post_ffn_g, a_w_in, a_conv_w, a_conv_b,
           a_gate_a_w, a_gate_a_b, a_gate_x_w, a_gate_x_b, a_lambda, a_w_mem_kv, a_w_out, a_w_ffn_in, a_w_ffn_out,
           kv_norm_g, w_kv_shared, b_pre_mix_g, b_post_mix_g, b_pre_ffn_g, b_post_ffn_g, b_w_in, b_w_mem_kv, b_w_out,
           b_w_ffn_in, b_w_ffn_out, loss_target, m_mem_norm_g, m_a_pre_mix_g, m_a_post_mix_g, m_a_pre_ffn_g,
           m_a_post_ffn_g, m_a_w_in, m_a_conv_w, m_a_conv_b, m_a_gate_a_w, m_a_gate_a_b, m_a_gate_x_w, m_a_gate_x_b,
           m_a_lambda, m_a_w_mem_kv, m_a_w_out, m_a_w_ffn_in, m_a_w_ffn_out, m_kv_norm_g, m_w_kv_shared, m_b_pre_mix_g,
           m_b_post_mix_g, m_b_pre_ffn_g, m_b_post_ffn_g, m_b_w_in, m_b_w_mem_kv, m_b_w_out, m_b_w_ffn_in, m_b_w_ffn_out,
           v_mem_norm_g, v_a_pre_mix_g, v_a_post_mix_g, v_a_pre_ffn_g, v_a_post_ffn_g, v_a_w_in, v_a_conv_w, v_a_conv_b,
           v_a_gate_a_w, v_a_gate_a_b, v_a_gate_x_w, v_a_gate_x_b, v_a_lambda, v_a_w_mem_kv, v_a_w_out, v_a_w_ffn_in,
           v_a_w_ffn_out, v_kv_norm_g, v_w_kv_shared, v_b_pre_mix_g, v_b_post_mix_g, v_b_pre_ffn_g, v_b_post_ffn_g,
           v_b_w_in, v_b_w_mem_kv, v_b_w_out, v_b_w_ffn_in, v_b_w_ffn_out):
    a = dict(locals())
    xi, yi, ci = _coords()
    chip = 2 * xi + yi

    got = small_gather(_pack([a[n] for n in SHARDED_SMALL]), "small_gather")
    per_chip = [_unpack(got[s], [a[n].shape for n in SHARDED_SMALL]) for s in range(N_CHIPS)]
    small = {n: jnp.concatenate([per_chip[s][k] for s in range(N_CHIPS)], axis=-1)
             for k, n in enumerate(SHARDED_SMALL)}
    small.update({n: a[n] for n in REPL_SMALL})

    groups = []
    for l in range(4):
        p, j = ("a", l) if l < 2 else ("b", l - 2)
        groups.append([(p + "_" + n, j) for n in ("w_in", "w_mem_kv", "w_out")])
        groups.append([(p + "_" + n, j) for n in ("w_ffn_in", "w_ffn_out")])
    groups[3].append(("w_kv_shared", None))
    big = {n: [None, None] for n in BIG if n != "w_kv_shared"}
    gs, gb = {}, {}
    reduced = {n: [None, None] for n in BIG if n != "w_kv_shared"}

    def put(store, n, j, val):
        if j is None:
            store[n] = val
        else:
            store[n][j] = val

    class Exchange:
        def __init__(self):
            self.state = {}

        def gather_ici(self, g):
            shards = [(a[n] if j is None else a[n][j]).astype(MXU) for n, j in groups[g]]
            rows = [sh.shape[0] for sh in shards]
            lands = [lax.empty((N_CHIPS,) + sh.shape, sh.dtype) for sh in shards]
            plan = plan_gather_ici(len(shards), rows)
            ss, rs, bufs, tok = split_start("gather_ici_%d" % g, shards + lands, plan, 4 * len(shards))
            self.state["g", g] = (ss, rs, bufs, plan, rows)
            return tok

        def gather_d2d(self, g, after):
            ss, rs, bufs, plan, rows = self.state.pop(("g", g))
            n = len(rows)
            outs = split_wait("gather_ici_wait_%d" % g, ss, rs, bufs, after, plan)[n:]
            plan = plan_gather_d2d(n, rows)
            ss, rs, bufs, tok = split_start("gather_d2d_%d" % g, outs, plan, 3 * n)
            self.state["g", g] = (ss, rs, bufs, plan)
            return tok

        def gather_done(self, g, after):
            ss, rs, bufs, plan = self.state.pop(("g", g))
            outs = split_wait("gather_d2d_wait_%d" % g, ss, rs, bufs, after, plan)
            for (n, j), w in zip(groups[g], outs):
                put(big, n, j, w if BIG[n] else w.reshape(-1, w.shape[-1]))

        def rs_swap(self, g):
            grads = []
            for n, j in groups[g]:
                gr = gb[n] if j is None else gb[n][j]
                grads.append(gr if BIG[n] else gr.reshape(N_CHIPS, gr.shape[0] // N_CHIPS, gr.shape[1]))
            rows = [gr.shape[1] for gr in grads]
            lands = [lax.empty((N_CHIPS, gr.shape[1] // 2, gr.shape[2]), gr.dtype) for gr in grads]
            plan = plan_swap(len(grads), rows)
            ss, rs, bufs, tok = split_start("rs_swap_%d" % g, grads + lands, plan, len(grads))
            self.state["r", g] = (ss, rs, bufs, plan, rows)
            return tok

        def rs_exchange(self, g, after):
            ss, rs, bufs, plan, rows = self.state.pop(("r", g))
            n = len(rows)
            bufs = split_wait("rs_swap_wait_%d" % g, ss, rs, bufs, after, plan)
            sums = [half_sum(gr, got, "rs_half_sum") for gr, got in zip(bufs[:n], bufs[n:])]
            plan = plan_exchange(n)
            ss, rs, bufs, tok = split_start("rs_exchange_%d" % g, [p for p, _ in sums] + [s for _, s in sums], plan,
                                            3 * n)
            self.state["r", g] = (ss, rs, bufs, plan, rows)
            return tok

        def rs_share(self, g, after):
            ss, rs, bufs, plan, rows = self.state.pop(("r", g))
            n = len(rows)
            slots = split_wait("rs_exchange_wait_%d" % g, ss, rs, bufs, after, plan)[n:]
            fulls = [slot_sum(s, "rs_slot_sum") for s in slots]
            plan = plan_share(n, rows)
            ss, rs, bufs, tok = split_start("rs_share_%d" % g, fulls, plan, n)
            self.state["r", g] = (ss, rs, bufs, plan)
            return tok

        def rs_done(self, g, after):
            ss, rs, bufs, plan = self.state.pop(("r", g))
            outs = split_wait("rs_share_wait_%d" % g, ss, rs, bufs, after, plan)
            for (n, j), r in zip(groups[g], outs):
                put(reduced, n, j, r)

        def hook(self, where, l, after):
            mix, ffn = 2 * l, 2 * l + 1
            toks = []
            if where == "fwd_begin":
                if l == 0:
                    tok = self.gather_ici(mix)
                    tok = self.gather_d2d(mix, tok)
                    self.gather_done(mix, tok)
                toks.append(self.gather_ici(ffn))
            elif where == "fwd_q1":
                toks.append(self.gather_d2d(ffn, after))
            elif where == "fwd_mid":
                self.gather_done(ffn, after)
                if l < 3:
                    toks.append(self.gather_ici(mix + 2))
            elif where == "fwd_q3":
                if l < 3:
                    toks.append(self.gather_d2d(mix + 2, after))
            elif where == "fwd_end":
                if l < 3:
                    self.gather_done(mix + 2, after)
            elif where == "bwd_begin":
                if l < 3:
                    self.rs_done(ffn + 2, after)
                    toks.append(self.rs_exchange(mix + 2, after))
            elif where == "bwd_mid1":
                if l < 3:
                    toks.append(self.rs_share(mix + 2, after))
            elif where == "bwd_mid2":
                if l < 3:
                    self.rs_done(mix + 2, after)
                toks.append(self.rs_swap(ffn))
            elif where == "bwd_m1":
                toks.append(self.rs_exchange(ffn, after))
            elif where == "bwd_end":
                toks.append(self.rs_share(ffn, after))
                toks.append(self.rs_swap(mix))
                if l == 0:
                    self.rs_done(ffn, toks[0])
                    tok = self.rs_exchange(mix, toks[1])
                    tok = self.rs_share(mix, adamw_big([n for n in BIG if n.startswith("b_")], tok))
                    self.rs_done(mix, tok)
                    toks = []
            else:
                raise ValueError(where)
            return toks

    res = {}

    def adamw_big(names, token=None):
        last = None
        for n in names:
            shape = a[n].shape
            rows, cols = shape[-2], shape[-1]
            stk = (-1, rows, cols)
            grads = reduced[n] if isinstance(reduced[n], list) else [reduced[n]]
            if token is not None:
                grads = [tie(grads[0], [token], "tie_adamw_" + n)] + grads[1:]
            outs = adamw(a[n].reshape(stk), grads, a["m_" + n].reshape(stk), a["v_" + n].reshape(stk), "adamw")
            res[n] = [o.reshape(shape) for o in outs]
            last = outs[1]
            token = last if token is not None else None
        return last

    loss_parts, dx = _fwd_bwd(x[0], mem[0], loss_target[0], small, big, gs, gb, Exchange())
    loss = lax.psum(jnp.sum(loss_parts) * (0.5 / D), ("x", "y", "c"))

    def full(n):
        g = gs[n]
        return jnp.stack(g) if isinstance(g, list) else g

    order = SHARDED_SMALL + REPL_SMALL
    full_shapes = [full(n).shape for n in order]
    pack = _pack([full(n) for n in order], 512)
    ss, rs, bufs, tok = split_start("small_swap", [pack, lax.empty(pack.shape, F32)], plan_small_swap, 1)
    mine_v, sib_v = split_wait("small_swap_wait", ss, rs, bufs, tok, plan_small_swap)
    ss, rs, bufs, tok = split_start("small_exchange", [small_pair(mine_v, sib_v, "small_pair")],
                                    plan_small_exchange, 3)
    last = adamw_big([n for n in BIG if n not in res], tok)
    slots = split_wait("small_exchange_wait", ss, rs, bufs, last, plan_small_exchange)[0]
    summed = _unpack(small_total(slots, "small_total"), full_shapes)
    mine = []
    for n, g in zip(order, summed):
        if n in SHARDED_SMALL:
            width = a[n].shape[-1]
            g = lax.dynamic_slice_in_dim(g, chip * width, width, axis=g.ndim - 1)
        mine.append(g.reshape(a[n].shape))
    shapes = [a[n].shape for n in order]
    rm = 512
    outs = adamw(_pack([a[n] for n in order], rm)[None], [_pack(mine, rm)],
                 _pack([a["m_" + n] for n in order], rm)[None], _pack([a["v_" + n] for n in order], rm)[None],
                 "adamw_small")
    unpacked = [_unpack(o[0], shapes) for o in outs]
    for k, n in enumerate(order):
        res[n] = [u[k] for u in unpacked]

    return (loss, dx[None], *[res[n][0] for n in WEIGHTS], *[res[n][1] for n in WEIGHTS],
            *[res[n][2] for n in WEIGHTS], *[res[n][3] for n in WEIGHTS])
```

```python
import math

import jax
import jax.numpy as jnp
from jax import lax
from jax.experimental import pallas as pl
from jax.experimental.pallas import tpu as pltpu

D = 2048
HD = 128
MEM_W = 512
MEM_HEADS = 4
MIX_W = D - MEM_W
N_BLK = MIX_W // HD
D_FF = 5632
N_MEM = 256
RMS_EPS = 1e-6
NEG_INF = -1e30
LRU_C = 8.0
DIL_GROUPS = ((128, 1), (512, 4), (2048, 16))
Q_BLOCK = 128
SCALE = HD ** -0.5
N_CHIPS = 4
MXU_COLS = 256
ACC_CHUNK = 2 * MXU_COLS

ADAM_LR = 0.001
ADAM_B1 = 0.9
ADAM_B2 = 0.999
ADAM_EPS = 1e-08
ADAM_WD = 0.01
ADAM_STEP = 10

MXU = jnp.bfloat16
F32 = jnp.float32
VMEM_LIMIT_BYTES = 56 * 1024 * 1024

BS = pl.BlockSpec
SDS = jax.ShapeDtypeStruct
MESH = pl.DeviceIdType.MESH


def _cp(*sem):
    return pltpu.CompilerParams(dimension_semantics=sem or None, vmem_limit_bytes=VMEM_LIMIT_BYTES)


def _dot(a, b, dn=((1,), (0,))):
    return lax.dot_general(a, b, (dn, ((), ())), preferred_element_type=F32)


def _div(i, n):
    return lax.div(i, jnp.int32(n))


def _rem(i, n):
    return lax.rem(i, jnp.int32(n))


NN = ((1,), (0,))
NT = ((1,), (1,))
TN = ((0,), (0,))


def _sigmoid(z):
    return 0.5 * jnp.tanh(0.5 * z) + 0.5


def _log1p_pos(u):
    return jnp.where(u < 1e-2, u * (1.0 - u * (0.5 - u * (1.0 / 3.0))), jnp.log(1.0 + u))


def _neg_expm1(z):
    return jnp.where(z > -1e-2, -z * (1.0 + z * (0.5 + z * (1.0 / 6.0))), 1.0 - jnp.exp(z))


def _softplus(z):
    return jnp.maximum(z, 0.0) + _log1p_pos(jnp.exp(-jnp.abs(z)))


_GELU_C = math.sqrt(2.0 / math.pi)


def _gelu_and_grad(x):
    x2 = x * x
    t = jnp.tanh(_GELU_C * (x + 0.044715 * x * x2))
    g = 0.5 * x * (1.0 + t)
    dg = 0.5 * (1.0 + t) + 0.5 * x * (1.0 - t * t) * _GELU_C * (1.0 + 3.0 * 0.044715 * x2)
    return g, dg


def _row_tile(rows):
    return min(256, rows)


def norm_cast(x, g, name):
    rows = x.shape[0]
    tr = _row_tile(rows)

    def body(x_ref, g_ref, o_ref):
        xv = x_ref[...]
        r = lax.rsqrt(jnp.mean(xv * xv, axis=-1, keepdims=True) + RMS_EPS)
        o_ref[...] = (xv * r * g_ref[...]).astype(o_ref.dtype)

    return pl.pallas_call(
        body, grid=(rows // tr,),
        in_specs=[BS((tr, D), lambda i: (i, 0)), BS((1, D), lambda i: (0, 0))],
        out_specs=BS((tr, D), lambda i: (i, 0)),
        out_shape=SDS((rows, D), MXU), compiler_params=_cp("parallel"), name=name,
    )(x, g.reshape(1, D))


def resid_norm(h, y, g, name):
    rows = h.shape[0]
    tr = _row_tile(rows)

    def body(h_ref, y_ref, g_ref, o_ref):
        yv = y_ref[...]
        r = lax.rsqrt(jnp.mean(yv * yv, axis=-1, keepdims=True) + RMS_EPS)
        o_ref[...] = h_ref[...] + yv * r * g_ref[...]

    return pl.pallas_call(
        body, grid=(rows // tr,),
        in_specs=[BS((tr, D), lambda i: (i, 0)), BS((tr, D), lambda i: (i, 0)), BS((1, D), lambda i: (0, 0))],
        out_specs=BS((tr, D), lambda i: (i, 0)),
        out_shape=SDS((rows, D), F32), compiler_params=_cp("parallel"), name=name,
    )(h, y, g.reshape(1, D))


def resid_norm_next(h, y, g, g_next, name):
    rows = h.shape[0]
    tr = min(2 * _row_tile(rows), rows)

    def body(h_ref, y_ref, g_ref, gn_ref, o_ref, n_ref):
        yv = y_ref[...]
        r = lax.rsqrt(jnp.mean(yv * yv, axis=-1, keepdims=True) + RMS_EPS)
        hv = h_ref[...] + yv * r * g_ref[...]
        o_ref[...] = hv
        r2 = lax.rsqrt(jnp.mean(hv * hv, axis=-1, keepdims=True) + RMS_EPS)
        n_ref[...] = (hv * r2 * gn_ref[...]).astype(n_ref.dtype)

    row = BS((tr, D), lambda i: (i, 0))
    vec = BS((1, D), lambda i: (0, 0))
    return pl.pallas_call(
        body, grid=(rows // tr,), in_specs=[row, row, vec, vec], out_specs=[row, row],
        out_shape=[SDS((rows, D), F32), SDS((rows, D), MXU)], compiler_params=_cp("parallel"), name=name,
    )(h, y, g.reshape(1, D), g_next.reshape(1, D))


def _norm_bwd_rows(xv, gv, dyv):
    r = lax.rsqrt(jnp.mean(xv * xv, axis=-1, keepdims=True) + RMS_EPS)
    xhat = xv * r
    dxhat = dyv * gv
    dx = r * (dxhat - xhat * jnp.mean(dxhat * xhat, axis=-1, keepdims=True))
    return dx, jnp.sum(dyv * xhat, axis=0, keepdims=True)


def norm_bwd(x, g, dy, res, out_dtype, name, then=None):
    rows = x.shape[0]
    tr = min(2 * _row_tile(rows), rows)
    has_res = res is not None
    n_in = 3 + has_res + (2 if then else 0)

    def body(*refs):
        x_ref, g_ref, dy_ref = refs[:3]
        dx_ref, dg_ref = refs[n_in], refs[n_in + 1]
        dx, dg = _norm_bwd_rows(x_ref[...], g_ref[...], dy_ref[...].astype(F32))
        if has_res:
            dx = dx + refs[3][...]
        dx_ref[...] = dx.astype(dx_ref.dtype)
        first = pl.program_id(0) == 0

        @pl.when(first)
        def _():
            dg_ref[...] = jnp.zeros_like(dg_ref)

        dg_ref[...] += dg
        if then:
            x2_ref, g2_ref = refs[n_in - 2], refs[n_in - 1]
            dx2_ref, dg2_ref = refs[n_in + 2], refs[n_in + 3]
            dx2, dg2 = _norm_bwd_rows(x2_ref[...], g2_ref[...], dx)
            dx2_ref[...] = dx2.astype(dx2_ref.dtype)

            @pl.when(first)
            def _():
                dg2_ref[...] = jnp.zeros_like(dg2_ref)

            dg2_ref[...] += dg2

    row = BS((tr, D), lambda i: (i, 0))
    vec = BS((1, D), lambda i: (0, 0))
    ins = [x, g.reshape(1, D), dy] + ([res] if has_res else []) + ([then[0], then[1].reshape(1, D)] if then else [])
    outs = pl.pallas_call(
        body, grid=(rows // tr,),
        in_specs=[row, vec, row] + ([row] if has_res else []) + ([row, vec] if then else []),
        out_specs=[row, vec] + ([row, vec] if then else []),
        out_shape=[SDS((rows, D), out_dtype), SDS((1, D), F32)] + ([SDS((rows, D), MXU), SDS((1, D), F32)] if then else []),
        compiler_params=_cp("arbitrary"), name=name,
    )(*ins)
    if then:
        return outs[0], outs[1].reshape(D), outs[2], outs[3].reshape(D)
    return outs[0], outs[1].reshape(D)


def loss_head(y, target, name):
    rows = y.shape[0]
    tr = _row_tile(rows)

    def body(y_ref, t_ref, dy_ref, acc_ref):
        err = y_ref[...] - t_ref[...]
        dy_ref[...] = err * (1.0 / D)

        @pl.when(pl.program_id(0) == 0)
        def _():
            acc_ref[...] = jnp.zeros_like(acc_ref)

        acc_ref[...] += jnp.sum(err * err, axis=0, keepdims=True)

    row = BS((tr, D), lambda i: (i, 0))
    dy, acc = pl.pallas_call(
        body, grid=(rows // tr,), in_specs=[row, row],
        out_specs=[row, BS((1, D), lambda i: (0, 0))],
        out_shape=[SDS((rows, D), F32), SDS((1, D), F32)],
        compiler_params=_cp("arbitrary"), name=name,
    )(y, target)
    return acc, dy


def _mm_call(ins, in_specs, pick, dn, grid, o_spec, out_sds, name):
    gk = grid[2]
    n_in = len(ins)

    def body(*refs):
        o_ref = refs[n_in]
        k = pl.program_id(2)

        def step(a_ref, b_ref):
            acc = o_ref if (out_sds.dtype == F32 or gk == 1) else refs[n_in + 1]
            width = acc.shape[-1]
            if dn == TN or width <= ACC_CHUNK:
                chunks = [(0, width)]
            else:
                chunks = [(c0, min(c0 + ACC_CHUNK, width)) for c0 in range(0, width, ACC_CHUNK)]

            def sweep(first):
                a = a_ref[...]
                pending = None
                for c0, c1 in chunks:
                    p = _dot(a, b_ref[c0:c1, :] if dn == NT else b_ref[:, c0:c1], dn)
                    if pending is not None:
                        put(first, *pending)
                    pending = (c0, c1, p)
                put(first, *pending)

            def put(first, c0, c1, p):
                if first:
                    acc[:, c0:c1] = p.astype(acc.dtype)
                else:
                    acc[:, c0:c1] += p

            if gk == 1:
                sweep(True)
                return

            @pl.when(k == 0)
            def _():
                sweep(True)

            @pl.when(k > 0)
            def _():
                sweep(False)

            if acc is not o_ref:
                @pl.when(k == gk - 1)
                def _():
                    o_ref[...] = acc[...].astype(o_ref.dtype)

        pick(refs[:n_in], k, step)

    scratch = []
    if gk > 1 and out_sds.dtype != F32:
        scratch = [pltpu.VMEM(o_spec.block_shape[-2:], F32)]
    return pl.pallas_call(
        body, grid=grid, in_specs=in_specs, out_specs=o_spec, out_shape=out_sds,
        scratch_shapes=scratch, compiler_params=_cp("parallel", "parallel", "arbitrary"), name=name,
    )(*ins)


def _pick2(refs, k, step):
    step(refs[0], refs[1])


def mm_nn(a, w, *, tm, tn, tk, out_dtype, name):
    m, kdim = a.shape
    if w.ndim == 3:
        c = w.shape[2]
        n = N_CHIPS * c
        per = c // tn
        b_spec = BS((None, tk, tn), lambda i, j, k: (_div(j, per), k, _rem(j, per)))
    else:
        n = w.shape[1]
        b_spec = BS((tk, tn), lambda i, j, k: (k, j))
    grid = (m // tm, n // tn, kdim // tk)
    return _mm_call([a, w], [BS((tm, tk), lambda i, j, k: (i, k)), b_spec], _pick2, NN, grid,
                    BS((tm, tn), lambda i, j, k: (i, j)), SDS((m, n), out_dtype), name)


def mm_nt(a_list, w, *, tm, tn, tk, out_dtype, name):
    m = a_list[0].shape[0]
    ka = a_list[0].shape[1]
    n_a = len(a_list)
    kdim = ka * n_a
    if w.ndim == 3:
        c = w.shape[2]
        n = w.shape[1]
        per = c // tk
        b_spec = BS((None, tn, tk), lambda i, j, k: (_div(k, per), j, _rem(k, per)))
    else:
        n = w.shape[0]
        b_spec = BS((tn, tk), lambda i, j, k: (j, k))
    gk = kdim // tk
    half = gk // n_a
    grid = (m // tm, n // tn, gk)
    if n_a == 1:
        a_specs = [BS((tm, tk), lambda i, j, k: (i, k))]
        pick = lambda refs, k, step: step(refs[0], refs[1])
    else:
        a_specs = [BS((tm, tk), lambda i, j, k: (i, jnp.minimum(k, half - 1))),
                   BS((tm, tk), lambda i, j, k: (i, jnp.maximum(k - half, 0)))]

        def pick(refs, k, step):
            @pl.when(k < half)
            def _():
                step(refs[0], refs[2])

            @pl.when(k >= half)
            def _():
                step(refs[1], refs[2])

    return _mm_call(list(a_list) + [w], a_specs + [b_spec], pick, NT, grid,
                    BS((tm, tn), lambda i, j, k: (i, j)), SDS((m, n), out_dtype), name)


def mm_tn(a, b_list, *, t1, tn, ts, col_shards, name):
    s, k1 = a.shape
    nb = b_list[0].shape[1]
    n_b = len(b_list)
    n = nb * n_b
    gn = n // tn
    half = gn // n_b
    grid = (k1 // t1, gn, s // ts)
    if col_shards:
        c = n // N_CHIPS
        per = c // tn
        o_spec = BS((None, t1, tn), lambda i, j, k: (_div(j, per), i, _rem(j, per)))
        out_sds = SDS((N_CHIPS, k1, c), MXU)
    else:
        o_spec = BS((t1, tn), lambda i, j, k: (i, j))
        out_sds = SDS((k1, n), MXU)
    a_spec = BS((ts, t1), lambda i, j, k: (k, i))
    if n_b == 1:
        b_specs = [BS((ts, tn), lambda i, j, k: (k, j))]
        pick = lambda refs, k, step: step(refs[0], refs[1])
    else:
        b_specs = [BS((ts, tn), lambda i, j, k: (k, jnp.minimum(j, half - 1))),
                   BS((ts, tn), lambda i, j, k: (k, jnp.maximum(j - half, 0)))]

        def pick(refs, k, step):
            j = pl.program_id(1)

            @pl.when(j < half)
            def _():
                step(refs[0], refs[1])

            @pl.when(j >= half)
            def _():
                step(refs[0], refs[2])

    return _mm_call([a] + list(b_list), [a_spec] + b_specs, pick, TN, grid, o_spec, out_sds, name)


def ffn_in_fwd(hn, w, name):
    s = hn.shape[0]
    tm = min(512, s)
    tn = D_FF // 4

    def tail(dag_ref, dau_ref, act_ref, c0, c1, g, u):
        sg = _sigmoid(g)
        silu = g * sg
        dag_ref[:, c0:c1] = (u * sg * (1.0 + g * (1.0 - sg))).astype(dag_ref.dtype)
        dau_ref[:, c0:c1] = silu.astype(dau_ref.dtype)
        act_ref[:, c0:c1] = (silu * u).astype(act_ref.dtype)

    def body(a_ref, wg_ref, wu_ref, dag_ref, dau_ref, act_ref):
        a = a_ref[...]
        pending = None
        for c0 in range(0, tn, ACC_CHUNK):
            c1 = min(c0 + ACC_CHUNK, tn)
            g = _dot(a, wg_ref[:, c0:c1])
            u = _dot(a, wu_ref[:, c0:c1])
            if pending is not None:
                tail(dag_ref, dau_ref, act_ref, *pending)
            pending = (c0, c1, g, u)
        tail(dag_ref, dau_ref, act_ref, *pending)

    tile = BS((tm, tn), lambda j, i: (i, j))
    return pl.pallas_call(
        body, grid=(4, s // tm),
        in_specs=[BS((tm, D), lambda j, i: (i, 0)),
                  BS((None, D, tn), lambda j, i: (_div(j, 2), 0, _rem(j, 2))),
                  BS((None, D, tn), lambda j, i: (2 + _div(j, 2), 0, _rem(j, 2)))],
        out_specs=[tile, tile, tile],
        out_shape=[SDS((s, D_FF), MXU), SDS((s, D_FF), MXU), SDS((s, D_FF), MXU)],
        compiler_params=_cp("parallel", "parallel"), name=name,
    )(hn, w, w)


def ffn_act_bwd(dy, w_out, dag, dau, name):
    s = dy.shape[0]
    tm = min(1024, s)
    tn = D_FF // 4

    def body(dy_ref, w_ref, dag_ref, dau_ref, dg_ref, du_ref):
        def tail(c0, c1, dact):
            dg_ref[:, c0:c1] = (dact * dag_ref[:, c0:c1].astype(F32)).astype(dg_ref.dtype)
            du_ref[:, c0:c1] = (dact * dau_ref[:, c0:c1].astype(F32)).astype(du_ref.dtype)

        dy = dy_ref[...]
        pending = None
        for c0 in range(0, tn, ACC_CHUNK):
            c1 = min(c0 + ACC_CHUNK, tn)
            dact = _dot(dy, w_ref[c0:c1, :], NT)
            if pending is not None:
                tail(*pending)
            pending = (c0, c1, dact)
        tail(*pending)

    tile = BS((tm, tn), lambda j, i: (i, j))
    return pl.pallas_call(
        body, grid=(4, s // tm),
        in_specs=[BS((tm, D), lambda j, i: (i, 0)), BS((tn, D), lambda j, i: (j, 0)), tile, tile],
        out_specs=[tile, tile],
        out_shape=[SDS((s, D_FF), MXU), SDS((s, D_FF), MXU)],
        compiler_params=_cp("parallel", "parallel"), name=name,
    )(dy, w_out, dag, dau)


LRU_T = 256
HALO = 8


def _shift_down(x, k, fill):
    rows = x.shape[0]
    idx = lax.broadcasted_iota(jnp.int32, x.shape, 0)
    return jnp.where(idx < k, fill, pltpu.roll(x, k, 0))


def _shift_up(x, k, fill):
    rows = x.shape[0]
    idx = lax.broadcasted_iota(jnp.int32, x.shape, 0)
    return jnp.where(idx >= rows - k, fill, pltpu.roll(x, rows - k, 0))


def _scan_block(a, b, carry, reverse):
    rows, cols = a.shape
    sub = 8
    in_group = lax.broadcasted_iota(jnp.int32, a.shape, 0) % sub
    for sh in (1, 2, 4):
        if reverse:
            a_s, b_s, ok = pltpu.roll(a, rows - sh, 0), pltpu.roll(b, rows - sh, 0), in_group < sub - sh
        else:
            a_s, b_s, ok = pltpu.roll(a, sh, 0), pltpu.roll(b, sh, 0), in_group >= sh
        b = jnp.where(ok, a * b_s + b, b)
        a = jnp.where(ok, a * a_s, a)
    groups = list(range(rows // sub))
    edge = 0 if reverse else sub - 1
    carry_in = {}
    for v in (reversed(groups) if reverse else groups):
        carry_in[v] = carry
        row = sub * v + edge
        carry = b[row:row + 1, :] + a[row:row + 1, :] * carry
    cin = jnp.concatenate([jnp.broadcast_to(carry_in[v], (sub, cols)) for v in groups], axis=0)
    return b + a * cin


def _conv_taps(xcat):
    rows = xcat.shape[0]
    taps = []
    for k in range(4):
        off = HALO - 3 + k
        taps.append(xcat[off:off + LRU_T] if off == HALO else pltpu.roll(xcat, rows - off, 0)[:LRU_T])
    return taps


def _gates(xc, wa_ref, ba, wx_ref, bx, lam, za_ref, zx_ref):
    xm = xc.astype(MXU)
    for n in range(N_BLK):
        sl = slice(n * HD, (n + 1) * HD)
        za_ref[:, sl] = _dot(xm[:, sl], wa_ref[n])
        zx_ref[:, sl] = _dot(xm[:, sl], wx_ref[n])
    ra = _sigmoid(za_ref[...] + ba)
    ii = _sigmoid(zx_ref[...] + bx)
    sp = _softplus(-lam)
    log_a = -LRU_C * ra * sp
    a = jnp.exp(log_a)
    mult = jnp.sqrt(_neg_expm1(2.0 * log_a))
    return ra, ii, sp, a, mult


def lru_fwd(proj, conv_w, conv_b, wa, ba, wx, bx, lam, name):
    s = proj.shape[0]
    c = MIX_W
    nblk = s // LRU_T
    hpb = LRU_T // HALO

    def body(x_ref, halo_ref, cw_ref, cb_ref, wa_ref, ba_ref, wx_ref, bx_ref, lam_ref,
             xc_ref, h_ref, carry, za_ref, zx_ref):
        i = pl.program_id(0)

        @pl.when(i == 0)
        def _():
            carry[...] = jnp.zeros_like(carry)

        halo = jnp.where(i == 0, 0.0, halo_ref[...])
        xcat = jnp.concatenate([halo, x_ref[...]], axis=0)
        taps = _conv_taps(xcat)
        xc = cb_ref[...] + sum(cw_ref[k:k + 1, :] * taps[k] for k in range(4))
        xc_ref[...] = xc
        _, ii, _, a, mult = _gates(xc, wa_ref, ba_ref[...], wx_ref, bx_ref[...], lam_ref[...], za_ref, zx_ref)
        h = _scan_block(a, mult * (ii * xc), carry[HALO - 1:HALO, :], False)
        h_ref[...] = h
        carry[...] = h[LRU_T - HALO:, :]

    def full(shape):
        return BS(shape, lambda i: (0,) * len(shape))

    blk = BS((LRU_T, c), lambda i: (i, 0))
    return pl.pallas_call(
        body, grid=(nblk,),
        in_specs=[blk, BS((HALO, c), lambda i: (jnp.maximum(i * hpb - 1, 0), 0)),
                  full((4, c)), full((1, c)), full((N_BLK, HD, HD)), full((1, c)),
                  full((N_BLK, HD, HD)), full((1, c)), full((1, c))],
        out_specs=[blk, blk],
        out_shape=[SDS((s, c), F32), SDS((s, c), F32)],
        scratch_shapes=[pltpu.VMEM((HALO, c), F32), pltpu.VMEM((LRU_T, c), F32), pltpu.VMEM((LRU_T, c), F32)],
        compiler_params=_cp("arbitrary"), name=name,
    )(proj, proj, conv_w, conv_b.reshape(1, c), wa.astype(MXU), ba.reshape(1, c), wx.astype(MXU),
      bx.reshape(1, c), lam.reshape(1, c))


def lru_mix_prep(h, proj, m, name):
    s = h.shape[0]
    tr = _row_tile(s)

    def body(h_ref, gb_ref, m_ref, o_ref):
        ge, _ = _gelu_and_grad(gb_ref[...])
        o_ref[:, :MIX_W] = (h_ref[...] * ge).astype(o_ref.dtype)
        o_ref[:, MIX_W:] = m_ref[...]

    return pl.pallas_call(
        body, grid=(s // tr,),
        in_specs=[BS((tr, MIX_W), lambda i: (i, 0)), BS((tr, MIX_W), lambda i: (i, 1)),
                  BS((tr, MEM_W), lambda i: (i, 0))],
        out_specs=BS((tr, D), lambda i: (i, 0)), out_shape=SDS((s, D), MXU),
        compiler_params=_cp("parallel"), name=name,
    )(h, proj, m)


def lru_bwd(dym, proj, xc, hl, dqm, conv_w, wa, ba, wx, bx, lam, name):
    s = proj.shape[0]
    c = MIX_W
    nblk = s // LRU_T
    hpb = LRU_T // HALO
    wa_m = wa.astype(MXU)
    wx_m = wx.astype(MXU)

    def body(dy_ref, x_ref, xhalo_ref, gb_ref, xc_ref, h_ref, hhalo_ref, dqm_ref,
             cw_ref, wa_ref, ba_ref, wx_ref, bx_ref, lam_ref,
             dproj_ref, dcw_ref, dcb_ref, dwa_ref, dba_ref, dwx_ref, dbx_ref, dlam_ref,
             g_next, a_next, dxc_next, za_ref, zx_ref, dxc_ref):
        i = pl.program_id(0)

        @pl.when(i == 0)
        def _():
            g_next[...] = jnp.zeros_like(g_next)
            a_next[...] = jnp.zeros_like(a_next)
            dxc_next[...] = jnp.zeros_like(dxc_next)
            for r in (dcw_ref, dcb_ref, dwa_ref, dba_ref, dwx_ref, dbx_ref, dlam_ref):
                r[...] = jnp.zeros_like(r)

        first = i == nblk - 1
        xc = xc_ref[...]
        lam = lam_ref[...]
        ra, ii, sp, a, mult = _gates(xc, wa_ref, ba_ref[...], wx_ref, bx_ref[...], lam, za_ref, zx_ref)
        hl_v = h_ref[...]
        ge, dge = _gelu_and_grad(gb_ref[...])
        dyl = dy_ref[...]
        dhl = dyl * ge
        dproj_ref[:, c:2 * c] = (dyl * hl_v * dge).astype(dproj_ref.dtype)
        dproj_ref[:, 2 * c:] = dqm_ref[...]

        an = _shift_up(a, 1, 0.0)
        last_row = lax.broadcasted_iota(jnp.int32, a.shape, 0) == LRU_T - 1
        an = jnp.where(last_row, a_next[0:1, :], an)
        g = _scan_block(an, dhl, g_next[0:1, :], True)
        g_next[...] = g[:HALO, :]
        a_next[...] = a[:HALO, :]

        hhalo = jnp.where(first, 0.0, hhalo_ref[...])
        h_prev = _shift_down(hl_v, 1, 0.0)
        first_row = lax.broadcasted_iota(jnp.int32, a.shape, 0) == 0
        h_prev = jnp.where(first_row, hhalo[HALO - 1:HALO, :], h_prev)
        da = g * h_prev
        ixc = ii * xc
        dmult = g * ixc
        dii = g * mult * xc
        dxc = g * mult * ii
        dlog_a = (da - dmult * a / mult) * a
        dra = dlog_a * (-LRU_C) * sp
        dlam_ref[...] += jnp.sum(dlog_a * ra, axis=0, keepdims=True) * (LRU_C * _sigmoid(-lam))
        dza = dra * ra * (1.0 - ra)
        dzx = dii * ii * (1.0 - ii)
        dba_ref[...] += jnp.sum(dza, axis=0, keepdims=True)
        dbx_ref[...] += jnp.sum(dzx, axis=0, keepdims=True)
        xm = xc.astype(MXU)
        dza_m = dza.astype(MXU)
        dzx_m = dzx.astype(MXU)
        for n in range(N_BLK):
            sl = slice(n * HD, (n + 1) * HD)
            dwa_ref[n] += _dot(xm[:, sl], dza_m[:, sl], TN)
            dwx_ref[n] += _dot(xm[:, sl], dzx_m[:, sl], TN)
            dxc_ref[:, sl] = _dot(dza_m[:, sl], wa_ref[n], NT) + _dot(dzx_m[:, sl], wx_ref[n], NT)
        dxc = dxc + dxc_ref[...]

        dcat = jnp.concatenate([dxc, dxc_next[...]], axis=0)
        rows = dcat.shape[0]
        dxb = cw_ref[3:4, :] * dxc
        for k in range(3):
            dxb = dxb + cw_ref[k:k + 1, :] * pltpu.roll(dcat, rows - (3 - k), 0)[:LRU_T]
        dproj_ref[:, :c] = dxb.astype(dproj_ref.dtype)
        dxc_next[...] = dxc[:HALO, :]

        xhalo = jnp.where(first, 0.0, xhalo_ref[...])
        taps = _conv_taps(jnp.concatenate([xhalo, x_ref[...]], axis=0))
        for k in range(4):
            dcw_ref[k:k + 1, :] += jnp.sum(dxc * taps[k], axis=0, keepdims=True)
        dcb_ref[...] += jnp.sum(dxc, axis=0, keepdims=True)

    def full(shape):
        return BS(shape, lambda i: (0,) * len(shape))

    def rev(i):
        return nblk - 1 - i

    blk0 = BS((LRU_T, c), lambda i: (rev(i), 0))
    blk1 = BS((LRU_T, c), lambda i: (rev(i), 1))
    halo = BS((HALO, c), lambda i: (jnp.maximum(rev(i) * hpb - 1, 0), 0))
    outs = pl.pallas_call(
        body, grid=(nblk,),
        in_specs=[blk0, blk0, halo, blk1, blk0, blk0, halo, BS((LRU_T, MEM_W), lambda i: (rev(i), 0)),
                  full((4, c)), full((N_BLK, HD, HD)), full((1, c)), full((N_BLK, HD, HD)), full((1, c)),
                  full((1, c))],
        out_specs=[BS((LRU_T, 2 * c + MEM_W), lambda i: (rev(i), 0)), full((4, c)), full((1, c)),
                   full((N_BLK, HD, HD)), full((1, c)), full((N_BLK, HD, HD)), full((1, c)), full((1, c))],
        out_shape=[SDS((s, 2 * c + MEM_W), MXU), SDS((4, c), F32), SDS((1, c), F32),
                   SDS((N_BLK, HD, HD), F32), SDS((1, c), F32), SDS((N_BLK, HD, HD), F32), SDS((1, c), F32),
                   SDS((1, c), F32)],
        scratch_shapes=[pltpu.VMEM((HALO, c), F32), pltpu.VMEM((HALO, c), F32), pltpu.VMEM((HALO, c), F32),
                        pltpu.VMEM((LRU_T, c), F32), pltpu.VMEM((LRU_T, c), F32), pltpu.VMEM((LRU_T, c), F32)],
        compiler_params=_cp("arbitrary"), name=name,
    )(dym, proj, proj, proj, xc, hl, hl, dqm, conv_w, wa_m, ba.reshape(1, c), wx_m, bx.reshape(1, c),
      lam.reshape(1, c))
    dproj, dcw, dcb, dwa, dba, dwx, dbx, dlam = outs
    return dproj, dcw, dcb.reshape(c), dwa, dba.reshape(c), dwx, dbx.reshape(c), dlam.reshape(c)


def _mem_probs(q, kv):
    heads = [slice(hh * HD, (hh + 1) * HD) for hh in range(MEM_HEADS)]
    sc = [_dot(q[:, sl], kv[:, sl], NT) * SCALE for sl in heads]
    e = [jnp.exp(s - jnp.max(s, axis=-1, keepdims=True)) for s in sc]
    return [x / jnp.sum(x, axis=-1, keepdims=True) for x in e]


def mem_attn_fwd(proj, q_col, kvm, name):
    s = proj.shape[0]
    tq = min(512, s)

    def body(q_ref, kv_ref, o_ref):
        q = q_ref[...].astype(MXU)
        kv = kv_ref[...]
        p = _mem_probs(q, kv)
        outs = [_dot(p[hh].astype(MXU), kv[:, MEM_W + hh * HD:MEM_W + (hh + 1) * HD]) for hh in range(MEM_HEADS)]
        o_ref[...] = jnp.concatenate(outs, axis=1).astype(o_ref.dtype)

    return pl.pallas_call(
        body, grid=(s // tq,),
        in_specs=[BS((tq, MEM_W), lambda i: (i, q_col)), BS((N_MEM, 2 * MEM_W), lambda i: (0, 0))],
        out_specs=BS((tq, MEM_W), lambda i: (i, 0)), out_shape=SDS((s, MEM_W), MXU),
        compiler_params=_cp("parallel"), name=name,
    )(proj, kvm)


def mem_attn_bwd(proj, q_col, kvm, dym, name):
    s = proj.shape[0]
    tq = min(512, s)

    def body(q_ref, kv_ref, do_ref, dq_ref, dkv_ref):
        @pl.when(pl.program_id(0) == 0)
        def _():
            dkv_ref[...] = jnp.zeros_like(dkv_ref)

        q = q_ref[...].astype(MXU)
        do = do_ref[...].astype(MXU)
        kv = kv_ref[...]
        heads = [slice(hh * HD, (hh + 1) * HD) for hh in range(MEM_HEADS)]
        p = _mem_probs(q, kv)
        dp = [_dot(do[:, sl], kv[:, MEM_W + hh * HD:MEM_W + (hh + 1) * HD], NT) for hh, sl in enumerate(heads)]
        ds = [(pp * (d - jnp.sum(pp * d, axis=-1, keepdims=True)) * SCALE).astype(MXU) for pp, d in zip(p, dp)]
        dq = [_dot(x, kv[:, sl]) for x, sl in zip(ds, heads)]
        dk = [_dot(x, q[:, sl], TN) for x, sl in zip(ds, heads)]
        dv = [_dot(pp.astype(MXU), do[:, sl], TN) for pp, sl in zip(p, heads)]
        dq_ref[...] = jnp.concatenate(dq, axis=1).astype(dq_ref.dtype)
        dkv_ref[...] += jnp.concatenate(dk + dv, axis=1)

    return pl.pallas_call(
        body, grid=(s // tq,),
        in_specs=[BS((tq, MEM_W), lambda i: (i, q_col)), BS((N_MEM, 2 * MEM_W), lambda i: (0, 0)),
                  BS((tq, MEM_W), lambda i: (i, MIX_W // MEM_W))],
        out_specs=[BS((tq, MEM_W), lambda i: (i, 0)), BS((N_MEM, 2 * MEM_W), lambda i: (0, 0))],
        out_shape=[SDS((s, MEM_W), MXU), SDS((N_MEM, 2 * MEM_W), F32)],
        compiler_params=_cp("arbitrary"), name=name,
    )(proj, kvm, dym)


def _dil_scores(q, kp, kc, n, slope_dil):
    qi = lax.broadcasted_iota(jnp.int32, (Q_BLOCK, Q_BLOCK), 0)
    ki = lax.broadcasted_iota(jnp.int32, (Q_BLOCK, Q_BLOCK), 1)
    rel_p = qi + Q_BLOCK - ki
    rel_c = qi - ki
    s_p = _dot(q, kp, NT) * SCALE - slope_dil * rel_p.astype(F32)
    s_c = _dot(q, kc, NT) * SCALE - slope_dil * rel_c.astype(F32)
    s_p = jnp.where((rel_p <= Q_BLOCK) & (n > 0), s_p, NEG_INF)
    s_c = jnp.where(rel_c >= 0, s_c, NEG_INF)
    return s_p, s_c


def _slope_dil(gi, hh):
    head = 4 * gi + hh
    return DIL_GROUPS[gi][1] * 2.0 ** (-8.0 * (head + 1.0) / N_BLK)


def _dil_operands(proj, kv, gi):
    dil = DIL_GROUPS[gi][1]
    if dil == 1:
        return proj, kv, kv, (lambda r: gi), (lambda r: gi), (lambda r: MIX_W // MEM_W + gi)
    sub = proj.shape[0] // dil

    def view(a, col):
        return a[:, col:col + MEM_W].reshape(sub, dil * MEM_W)

    same = lambda r: r
    return view(proj, gi * MEM_W), view(kv, gi * MEM_W), view(kv, MIX_W + gi * MEM_W), same, same, same


def dil_attn_fwd(proj, kv, gi, name):
    dil = DIL_GROUPS[gi][1]
    s, pw = proj.shape
    sub = s // dil
    nb = sub // Q_BLOCK
    qc, kc_ = pw // MEM_W, kv.shape[1] // MEM_W

    def body(q_ref, kp_ref, kc_ref, vp_ref, vc_ref, o_ref, lse_ref):
        n = pl.program_id(1)
        q = q_ref[...].astype(MXU)
        kp, kc, vp, vc = kp_ref[...], kc_ref[...], vp_ref[...], vc_ref[...]
        heads = [slice(hh * HD, (hh + 1) * HD) for hh in range(4)]
        sc = [_dil_scores(q[:, sl], kp[:, sl], kc[:, sl], n, _slope_dil(gi, hh)) for hh, sl in enumerate(heads)]
        mx = [jnp.maximum(jnp.max(s_p, axis=-1, keepdims=True), jnp.max(s_c, axis=-1, keepdims=True))
              for s_p, s_c in sc]
        den = [jnp.sum(jnp.exp(s_p - m), axis=-1, keepdims=True) + jnp.sum(jnp.exp(s_c - m), axis=-1, keepdims=True)
               for (s_p, s_c), m in zip(sc, mx)]
        lse = [m + jnp.log(d) for m, d in zip(mx, den)]
        pr = [(jnp.exp(s_p - l).astype(MXU), jnp.exp(s_c - l).astype(MXU)) for (s_p, s_c), l in zip(sc, lse)]
        outs = [_dot(p_p, vp[:, sl]) + _dot(p_c, vc[:, sl]) for (p_p, p_c), sl in zip(pr, heads)]
        o_ref[...] = jnp.concatenate(outs, axis=1)
        lse_ref[...] = jnp.concatenate([jnp.broadcast_to(l, (Q_BLOCK, HD)) for l in lse], axis=1)

    blk = (Q_BLOCK, MEM_W)
    prev = lambda n: jnp.maximum(n - 1, 0)
    out = BS(blk, lambda r, n: (n, r))
    qv, kview, vview, qcol, kcol, vcol = _dil_operands(proj, kv, gi)
    return pl.pallas_call(
        body, grid=(dil, nb),
        in_specs=[BS(blk, lambda r, n: (n, qcol(r))),
                  BS(blk, lambda r, n: (prev(n), kcol(r))), BS(blk, lambda r, n: (n, kcol(r))),
                  BS(blk, lambda r, n: (prev(n), vcol(r))), BS(blk, lambda r, n: (n, vcol(r)))],
        out_specs=[out, out],
        out_shape=[SDS((sub, dil * MEM_W), F32), SDS((sub, dil * MEM_W), F32)],
        compiler_params=_cp("parallel", "parallel"), name=name,
    )(qv, kview, kview, vview, vview)


def dil_attn_bwd(proj, kv, lse, do, dd, gi, name):
    dil = DIL_GROUPS[gi][1]
    s, pw = proj.shape
    sub = s // dil
    nb = sub // Q_BLOCK
    qc, kc_ = pw // MEM_W, kv.shape[1] // MEM_W

    def body(q_ref, kp_ref, kc_ref, vp_ref, vc_ref, lse_ref, do_ref, dd_ref, dq_ref, dk_ref, dv_ref, ck, cv):
        n = pl.program_id(1)

        @pl.when(n == 0)
        def _():
            ck[...] = jnp.zeros_like(ck)
            cv[...] = jnp.zeros_like(cv)

        @pl.when(n < nb)
        def _():
            q = q_ref[...].astype(MXU)
            do_m = do_ref[...].astype(MXU)
            kp, kc, vp, vc = kp_ref[...], kc_ref[...], vp_ref[...], vc_ref[...]
            lse_v, dd_v, ck_v, cv_v = lse_ref[...], dd_ref[...], ck[...], cv[...]
            heads = [slice(hh * HD, (hh + 1) * HD) for hh in range(4)]
            sc = [_dil_scores(q[:, sl], kp[:, sl], kc[:, sl], n, _slope_dil(gi, hh)) for hh, sl in enumerate(heads)]
            dp = [(_dot(do_m[:, sl], vp[:, sl], NT), _dot(do_m[:, sl], vc[:, sl], NT)) for sl in heads]
            pr = [(jnp.exp(s_p - lse_v[:, sl]), jnp.exp(s_c - lse_v[:, sl])) for (s_p, s_c), sl in zip(sc, heads)]
            ds = [((p_p * (dp_p + dd_v[:, sl]) * SCALE).astype(MXU), (p_c * (dp_c + dd_v[:, sl]) * SCALE).astype(MXU))
                  for (p_p, p_c), (dp_p, dp_c), sl in zip(pr, dp, heads)]
            pm = [(p_p.astype(MXU), p_c.astype(MXU)) for p_p, p_c in pr]
            dq = [_dot(ds_p, kp[:, sl]) + _dot(ds_c, kc[:, sl]) for (ds_p, ds_c), sl in zip(ds, heads)]
            dk = [ck_v[:, sl] + _dot(ds_p, q[:, sl], TN) for (ds_p, _), sl in zip(ds, heads)]
            dv = [cv_v[:, sl] + _dot(p_p, do_m[:, sl], TN) for (p_p, _), sl in zip(pm, heads)]
            ck_new = [_dot(ds_c, q[:, sl], TN) for (_, ds_c), sl in zip(ds, heads)]
            cv_new = [_dot(p_c, do_m[:, sl], TN) for (_, p_c), sl in zip(pm, heads)]
            dq_ref[...] = jnp.concatenate(dq, axis=1).astype(dq_ref.dtype)
            dk_ref[...] = jnp.concatenate(dk, axis=1)
            dv_ref[...] = jnp.concatenate(dv, axis=1)
            ck[...] = jnp.concatenate(ck_new, axis=1)
            cv[...] = jnp.concatenate(cv_new, axis=1)

        @pl.when(n == nb)
        def _():
            dk_ref[...] = ck[...]
            dv_ref[...] = cv[...]

    blk = (Q_BLOCK, MEM_W)
    cur = lambda n: jnp.minimum(n, nb - 1)
    prev = lambda n: jnp.maximum(jnp.minimum(n, nb - 1) - 1, 0)
    done = lambda n: jnp.maximum(n - 1, 0)
    own = BS(blk, lambda r, n: (cur(n), r))
    qv, kview, vview, qcol, kcol, vcol = _dil_operands(proj, kv, gi)
    return pl.pallas_call(
        body, grid=(dil, nb + 1),
        in_specs=[BS(blk, lambda r, n: (cur(n), qcol(r))),
                  BS(blk, lambda r, n: (prev(n), kcol(r))), BS(blk, lambda r, n: (cur(n), kcol(r))),
                  BS(blk, lambda r, n: (prev(n), vcol(r))), BS(blk, lambda r, n: (cur(n), vcol(r))),
                  own, own, own],
        out_specs=[own, BS(blk, lambda r, n: (done(n), r)), BS(blk, lambda r, n: (done(n), r))],
        out_shape=[SDS((sub, dil * MEM_W), MXU), SDS((sub, dil * MEM_W), F32), SDS((sub, dil * MEM_W), F32)],
        scratch_shapes=[pltpu.VMEM(blk, F32), pltpu.VMEM(blk, F32)],
        compiler_params=_cp("parallel", "arbitrary"), name=name,
    )(qv, kview, kview, vview, vview, lse, do, dd)


def _group_weights(lse_refs):
    l0, l1, l2 = (r[...] for r in lse_refs)
    mx = jnp.maximum(jnp.maximum(l0, l1), l2)
    e = [jnp.exp(l - mx) for l in (l0, l1, l2)]
    den = e[0] + e[1] + e[2]
    return [x / den for x in e]


def dil_mix_prep(o_list, lse_list, m, name):
    s = m.shape[0]
    tr = _row_tile(s)

    def body(o0, o1, o2, l0, l1, l2, m_ref, out_ref):
        w = _group_weights((l0, l1, l2))
        for g, o_ref in enumerate((o0, o1, o2)):
            out_ref[:, g * MEM_W:(g + 1) * MEM_W] = (o_ref[...] * w[g]).astype(out_ref.dtype)
        out_ref[:, MIX_W:] = m_ref[...]

    blk = BS((tr, MEM_W), lambda i: (i, 0))
    return pl.pallas_call(
        body, grid=(s // tr,), in_specs=[blk] * 7,
        out_specs=BS((tr, D), lambda i: (i, 0)), out_shape=SDS((s, D), MXU),
        compiler_params=_cp("parallel"), name=name,
    )(*o_list, *lse_list, m)


def dil_mix_bwd(dym, o_list, lse_list, name):
    s = dym.shape[0]
    tr = _row_tile(s)

    def body(da_ref, o0, o1, o2, l0, l1, l2, do0, do1, do2, dd0, dd1, dd2):
        w = _group_weights((l0, l1, l2))
        tot = None
        for g, (o_ref, do_ref) in enumerate(zip((o0, o1, o2), (do0, do1, do2))):
            da = da_ref[:, g * MEM_W:(g + 1) * MEM_W]
            do_ref[...] = da * w[g]
            x = da * o_ref[...]
            dw = jnp.concatenate(
                [jnp.broadcast_to(jnp.sum(x[:, hh * HD:(hh + 1) * HD], axis=-1, keepdims=True), (tr, HD))
                 for hh in range(4)], axis=1)
            tot = w[g] * dw if tot is None else tot + w[g] * dw
        for g, dd_ref in enumerate((dd0, dd1, dd2)):
            dd_ref[...] = -w[g] * tot

    blk = BS((tr, MEM_W), lambda i: (i, 0))
    outs = pl.pallas_call(
        body, grid=(s // tr,), in_specs=[BS((tr, MIX_W), lambda i: (i, 0))] + [blk] * 6,
        out_specs=[blk] * 6, out_shape=[SDS((s, MEM_W), F32)] * 6,
        compiler_params=_cp("parallel"), name=name,
    )(dym, *o_list, *lse_list)
    return outs[:3], outs[3:]


def sum_cast(parts, name):
    s = parts[0][0].shape[0]
    tr = _row_tile(s)
    flat = [a for p in parts for a in p]
    sizes = [len(p) for p in parts]

    def body(*refs):
        out_ref = refs[-1]
        pos = 0
        for j, n in enumerate(sizes):
            acc = refs[pos][...].astype(F32)
            for t in range(1, n):
                acc = acc + refs[pos + t][...].astype(F32)
            out_ref[:, j * MEM_W:(j + 1) * MEM_W] = acc.astype(out_ref.dtype)
            pos += n

    blk = BS((tr, MEM_W), lambda i: (i, 0))
    width = MEM_W * len(parts)
    return pl.pallas_call(
        body, grid=(s // tr,), in_specs=[blk] * len(flat),
        out_specs=BS((tr, width), lambda i: (i, 0)), out_shape=SDS((s, width), MXU),
        compiler_params=_cp("parallel"), name=name,
    )(*flat)


def add_n(arrs, name):
    rows, cols = arrs[0].shape
    tr = _row_tile(rows)

    def body(*refs):
        acc = refs[0][...]
        for r in refs[1:-1]:
            acc = acc + r[...]
        refs[-1][...] = acc

    blk = BS((tr, cols), lambda i: (i, 0))
    return pl.pallas_call(
        body, grid=(rows // tr,), in_specs=[blk] * len(arrs), out_specs=blk,
        out_shape=SDS((rows, cols), F32), compiler_params=_cp("parallel"), name=name,
    )(*arrs)


class _NoExchange:
    def hook(self, where, l, after):
        return []


def _fwd_bwd(x, mem, target, small, big, gs, gb, sched):
    s = x.shape[0]
    tm = min(1024, s)
    ts = min(2048, s)

    def after_hook(arr, where, l, after):
        toks = sched.hook(where, l, after)
        return tie(arr, toks, "tie_%s_%d" % (where, l)) if toks else arr

    h = x
    saved = []
    kv = None
    mem_n = None
    hn = norm_cast(h, small["a_pre_mix_g"][0], "pre_norm")
    for l in range(4):
        rec = l < 2
        p, j = ("a", l) if rec else ("b", l - 2)
        sv = {"h": h}
        hn = after_hook(hn, "fwd_begin", l, h)
        if mem_n is None:
            mem_n = norm_cast(mem, small["mem_norm_g"], "mem_norm")
        kvm = mm_nn(mem_n, big[p + "_w_mem_kv"][j], tm=N_MEM, tn=2 * MEM_W, tk=D, out_dtype=MXU, name="mem_kv")
        if rec:
            proj = mm_nn(hn, big["a_w_in"][j], tm=min(2 * tm, s), tn=896, tk=D, out_dtype=F32, name="rec_in")
            xc, hl = lru_fwd(proj, small["a_conv_w"][j], small["a_conv_b"][j], small["a_gate_a_w"][j],
                             small["a_gate_a_b"][j], small["a_gate_x_w"][j], small["a_gate_x_b"][j],
                             small["a_lambda"][j], "lru_fwd")
            m = mem_attn_fwd(proj, 2 * MIX_W // MEM_W, kvm, "rec_mem_attn")
            ym = lru_mix_prep(hl, proj, m, "lru_mix_prep")
            sv.update(xc=xc, hl=hl)
        else:
            proj = mm_nn(hn, big["b_w_in"][j], tm=tm, tn=D, tk=D, out_dtype=F32, name="dil_in")
            o_list, lse_list = [], []
            for gi in range(3):
                o, lse = dil_attn_fwd(proj, kv, gi, "dil_attn_fwd%d" % gi)
                o_list.append(o.reshape(s, MEM_W))
                lse_list.append(lse.reshape(s, MEM_W))
            m = mem_attn_fwd(proj, MIX_W // MEM_W, kvm, "dil_mem_attn")
            ym = dil_mix_prep(o_list, lse_list, m, "dil_mix_prep")
            sv.update(o=o_list, lse=lse_list)
        ym = after_hook(ym, "fwd_q1", l, ym)
        mix = mm_nn(ym, big[p + "_w_out"][j], tm=tm, tn=D, tk=D, out_dtype=F32, name="mix_out")
        h1, hn2 = resid_norm_next(h, mix, small[p + "_post_mix_g"][j], small[p + "_pre_ffn_g"][j], "post_pre_norm")
        hn2 = after_hook(hn2, "fwd_mid", l, mix)
        g, u, act = ffn_in_fwd(hn2, big[p + "_w_ffn_in"][j], "ffn_in")
        act = after_hook(act, "fwd_q3", l, u)
        y2 = mm_nn(act, big[p + "_w_ffn_out"][j], tm=tm // 2, tn=D, tk=D_FF // 2, out_dtype=F32, name="ffn_out")
        sv.update(kvm=kvm, hn=hn, proj=proj, ym=ym, mix=mix, h1=h1, hn2=hn2, g=g, u=u, act=act, y2=y2)
        saved.append(sv)
        if l < 3:
            pn, jn = ("a", l + 1) if l + 1 < 2 else ("b", l - 1)
            h, hn = resid_norm_next(h1, y2, small[p + "_post_ffn_g"][j], small[pn + "_pre_mix_g"][jn],
                                    "post_pre_norm")
        else:
            h = resid_norm(h1, y2, small[p + "_post_ffn_g"][j], "post_norm")
        sched.hook("fwd_end", l, h)
        if l == 1:
            h_kv = h
            kvn = norm_cast(h, small["kv_norm_g"], "pre_norm")
            kv = mm_nn(kvn, big["w_kv_shared"], tm=tm, tn=768, tk=D, out_dtype=MXU, name="kv_proj")

    loss_parts, dh = loss_head(h, target, "loss_head")

    def stack2(name, j, val):
        gs.setdefault(name, [None, None])[j] = val

    def stack2b(name, j, val):
        gb.setdefault(name, [None, None])[j] = val

    dkv_parts = []
    ahead = []
    dmem_parts = []
    dkvm = [None] * 4
    for l in (3, 2, 1, 0):
        rec = l < 2
        p, j = ("a", l) if rec else ("b", l - 2)
        sv = saved[l]
        if l == 1:
            dkv = sum_cast([(dkv_parts[0][c], dkv_parts[1][c]) for c in range(6)], "dkv_sum")
            dkvn = mm_nt([dkv], big["w_kv_shared"], tm=tm, tn=D, tk=768, out_dtype=MXU, name="kv_proj_dx")
            gb["w_kv_shared"] = mm_tn(kvn, [dkv], t1=D, tn=768, ts=ts, col_shards=True, name="kv_proj_dw")
            dh, gs["kv_norm_g"], *ahead = norm_bwd(h_kv, small["kv_norm_g"], dkvn, dh, F32, "pre_post_norm_bwd",
                                                   then=(sv["y2"], small["a_post_ffn_g"][1]))
        if ahead:
            dy2, dg = ahead
            ahead = []
        else:
            dy2, dg = norm_bwd(sv["y2"], small[p + "_post_ffn_g"][j], dh, None, MXU, "post_norm_bwd")
        dy2 = after_hook(dy2, "bwd_begin", l, dh)
        stack2(p + "_post_ffn_g", j, dg)
        dgg, dgu = ffn_act_bwd(dy2, big[p + "_w_ffn_out"][j], sv["g"], sv["u"], "ffn_act_bwd")
        dgg = after_hook(dgg, "bwd_mid1", l, dgu)
        stack2b(p + "_w_ffn_out", j, mm_tn(sv["act"], [dy2], t1=D_FF // 4, tn=D, ts=ts // 2, col_shards=False,
                                          name="ffn_out_dw"))
        dhn2 = mm_nt([dgg, dgu], big[p + "_w_ffn_in"][j], tm=tm // 2, tn=D, tk=D_FF // 2, out_dtype=MXU,
                     name="ffn_in_dx")
        stack2b(p + "_w_ffn_in", j, mm_tn(sv["hn2"], [dgg, dgu], t1=D // 2, tn=D_FF // 4, ts=ts, col_shards=True,
                                         name="ffn_in_dw"))
        dhn2 = after_hook(dhn2, "bwd_mid2", l, gb[p + "_w_ffn_in"][j])
        dh1, dg, dmix, dg_mix = norm_bwd(sv["h1"], small[p + "_pre_ffn_g"][j], dhn2, dh, F32, "pre_post_norm_bwd",
                                         then=(sv["mix"], small[p + "_post_mix_g"][j]))
        stack2(p + "_pre_ffn_g", j, dg)
        stack2(p + "_post_mix_g", j, dg_mix)
        dym = mm_nt([dmix], big[p + "_w_out"][j], tm=tm, tn=D, tk=D, out_dtype=F32, name="mix_out_dx")
        stack2b(p + "_w_out", j, mm_tn(sv["ym"], [dmix], t1=D, tn=1024, ts=ts, col_shards=False,
                                      name="mix_out_dw"))
        dym = after_hook(dym, "bwd_m1", l, gb[p + "_w_out"][j])
        if rec:
            dqm, dkvm[l] = mem_attn_bwd(sv["proj"], 2 * MIX_W // MEM_W, sv["kvm"], dym, "rec_mem_attn_bwd")
            dproj, dcw, dcb, dwa, dba, dwx, dbx, dlam = lru_bwd(
                dym, sv["proj"], sv["xc"], sv["hl"], dqm, small["a_conv_w"][j], small["a_gate_a_w"][j],
                small["a_gate_a_b"][j], small["a_gate_x_w"][j], small["a_gate_x_b"][j], small["a_lambda"][j],
                "lru_bwd")
            for nm, val in (("a_conv_w", dcw), ("a_conv_b", dcb), ("a_gate_a_w", dwa), ("a_gate_a_b", dba),
                            ("a_gate_x_w", dwx), ("a_gate_x_b", dbx), ("a_lambda", dlam)):
                stack2(nm, j, val)
            dhn = mm_nt([dproj], big["a_w_in"][j], tm=tm, tn=D, tk=896, out_dtype=MXU, name="rec_in_dx")
            stack2b("a_w_in", j, mm_tn(sv["hn"], [dproj], t1=D, tn=896, ts=ts, col_shards=True, name="rec_in_dw"))
        else:
            dqm, dkvm[l] = mem_attn_bwd(sv["proj"], MIX_W // MEM_W, sv["kvm"], dym, "dil_mem_attn_bwd")
            do_list, dd_list = dil_mix_bwd(dym, sv["o"], sv["lse"], "dil_mix_bwd")
            dq_list, dk_list, dv_list = [], [], []
            for gi in range(3):
                dil = DIL_GROUPS[gi][1]
                view = (s // dil, dil * MEM_W)
                dq, dk, dv = dil_attn_bwd(sv["proj"], kv, sv["lse"][gi].reshape(view), do_list[gi].reshape(view),
                                          dd_list[gi].reshape(view), gi, "dil_attn_bwd%d" % gi)
                dq_list.append(dq.reshape(s, MEM_W))
                dk_list.append(dk.reshape(s, MEM_W))
                dv_list.append(dv.reshape(s, MEM_W))
            dkv_parts.append(dk_list + dv_list)
            dproj = sum_cast([(a,) for a in dq_list + [dqm]], "dil_dproj")
            dhn = mm_nt([dproj], big["b_w_in"][j], tm=tm, tn=D, tk=D, out_dtype=MXU, name="dil_in_dx")
            stack2b("b_w_in", j, mm_tn(sv["hn"], [dproj], t1=D, tn=1024, ts=ts, col_shards=False, name="dil_in_dw"))
        dk_m = dkvm[l].astype(MXU)
        dmem_parts.append(mm_nt([dk_m], big[p + "_w_mem_kv"][j], tm=N_MEM, tn=D, tk=2 * MEM_W, out_dtype=F32,
                                name="mem_kv_dx"))
        stack2b(p + "_w_mem_kv", j, mm_tn(mem_n, [dk_m], t1=D, tn=2 * MEM_W, ts=N_MEM, col_shards=False,
                                         name="mem_kv_dw"))
        if l in (3, 1):
            pn, jn = ("b", 0) if l == 3 else ("a", 0)
            dh, dg, *ahead = norm_bwd(sv["h"], small[p + "_pre_mix_g"][j], dhn, dh1, F32, "pre_post_norm_bwd",
                                      then=(saved[l - 1]["y2"], small[pn + "_post_ffn_g"][jn]))
        else:
            dh, dg = norm_bwd(sv["h"], small[p + "_pre_mix_g"][j], dhn, dh1, F32, "pre_norm_bwd")
        stack2(p + "_pre_mix_g", j, dg)
        dh = after_hook(dh, "bwd_end", l, dh)

    _, gs["mem_norm_g"] = norm_bwd(mem, small["mem_norm_g"], add_n(dmem_parts, "dmem_sum"), None, F32,
                                   "mem_norm_bwd")
    return loss_parts, dh


ANY = pl.BlockSpec(memory_space=pl.ANY)
CHIP_FLIPS = (1, 2, 3)


def _coords():
    return lax.axis_index("x"), lax.axis_index("y"), lax.axis_index("c")


def _flip(x, y, m):
    return x ^ (m >> 1), y ^ (m & 1)


def _remote(src, dst, send_sems, recv_sems, k, device):
    return pltpu.make_async_remote_copy(src_ref=src, dst_ref=dst, send_sem=send_sems.at[k], recv_sem=recv_sems.at[k],
                                        device_id=device, device_id_type=MESH)


def _sum_rows_tile(rows, cols, itemsize=4):
    for tr in (512, 256, 128, 64, 32, 16):
        if rows % tr == 0 and tr * cols * itemsize <= 2 * 1024 * 1024:
            return tr
    raise ValueError((rows, cols))


def half_sum(g, got, name):
    _, r, cols = g.shape
    hr = r // 2
    tr = _sum_rows_tile(hr, cols, g.dtype.itemsize)

    def my_chip():
        return 2 * lax.axis_index("x") + lax.axis_index("y")

    def body(g_ref, got_ref, o_ref, own_ref):
        p = (g_ref[...].astype(F32) + got_ref[...].astype(F32)).astype(o_ref.dtype)
        o_ref[...] = p

        @pl.when(pl.program_id(1) == my_chip())
        def _():
            own_ref[...] = p

    out = SDS((N_CHIPS, hr, cols), jnp.bfloat16)
    return pl.pallas_call(
        body, grid=(hr // tr, N_CHIPS),
        in_specs=[BS((None, None, tr, cols), lambda i, s: (s, lax.axis_index("c"), i, 0)),
                  BS((None, tr, cols), lambda i, s: (s, i, 0))],
        out_specs=[BS((None, tr, cols), lambda i, s: (s, i, 0)),
                   BS((None, tr, cols), lambda i, s: (my_chip(), i, 0))],
        out_shape=[out, out], compiler_params=_cp("parallel", "arbitrary"), name=name,
    )(g.reshape(N_CHIPS, 2, hr, cols), got)


def slot_sum(slots, name):
    _, hr, cols = slots.shape
    tr = _sum_rows_tile(hr, cols)
    nblk = hr // tr

    def body(s_ref, o_ref):
        acc = s_ref[0].astype(F32)
        for p in range(1, N_CHIPS):
            acc = acc + s_ref[p].astype(F32)
        o_ref[...] = acc

    return pl.pallas_call(
        body, grid=(nblk,), in_specs=[BS((N_CHIPS, tr, cols), lambda i: (0, i, 0))],
        out_specs=BS((tr, cols), lambda i: (lax.axis_index("c") * nblk + i, 0)),
        out_shape=SDS((2 * hr, cols), F32), compiler_params=_cp("parallel"), name=name,
    )(slots)


HBM_SPEC = pl.BlockSpec(memory_space=pltpu.HBM)
SEM_SPEC = pl.BlockSpec(memory_space=pltpu.SEMAPHORE)
EFFECT = pltpu.SideEffectType.DATAFLOW_SIDE_EFFECTING


def split_start(name, bufs, plan, n_copies):
    nb = len(bufs)

    def body(*refs):
        send_sems, recv_sems = refs[nb], refs[nb + 1]
        for k, (src, dst, dev) in enumerate(plan(refs[:nb])):
            _remote(src, dst, send_sems, recv_sems, k, dev).start()
        refs[-1][...] = jnp.zeros_like(refs[-1])

    outs = pl.pallas_call(
        body, name=name,
        out_shape=(pltpu.SemaphoreType.DMA((n_copies,)), pltpu.SemaphoreType.DMA((n_copies,)),
                   *[pltpu.HBM(b.shape, b.dtype) for b in bufs], SDS((8, LANES), F32)),
        in_specs=[HBM_SPEC] * nb, out_specs=(SEM_SPEC, SEM_SPEC, *[HBM_SPEC] * nb, VM),
        input_output_aliases={i: 2 + i for i in range(nb)},
        compiler_params=pltpu.CompilerParams(has_side_effects=EFFECT),
    )(*[pltpu.with_memory_space_constraint(b, pltpu.HBM) for b in bufs])
    return outs[0], outs[1], list(outs[2:2 + nb]), outs[-1]


def split_wait(name, send_sems, recv_sems, bufs, after, plan):
    nb = len(bufs)

    def body(*refs):
        send_ref, recv_ref = refs[nb], refs[nb + 1]
        for k, (src, dst, dev) in enumerate(plan(refs[:nb])):
            cp = _remote(src, dst, send_ref, recv_ref, k, dev)
            cp.wait_send()
            cp.wait_recv()

    outs = pl.pallas_call(
        body, name=name, out_shape=[pltpu.HBM(b.shape, b.dtype) for b in bufs],
        in_specs=[HBM_SPEC] * nb + [SEM_SPEC, SEM_SPEC, ANY], out_specs=[HBM_SPEC] * nb,
        input_output_aliases={i: i for i in range(nb)},
        compiler_params=pltpu.CompilerParams(has_side_effects=EFFECT),
    )(*bufs, send_sems, recv_sems, after)
    return list(outs)


def tie(x, tokens, name):
    def body(*refs):
        pass

    return pl.pallas_call(
        body, name=name, out_shape=SDS(x.shape, x.dtype), in_specs=[ANY] * (1 + len(tokens)), out_specs=ANY,
        input_output_aliases={0: 0},
    )(x, *tokens)


def plan_gather_ici(n, rows):
    def plan(refs):
        x, y, c = _coords()
        me = 2 * x + y
        out = []
        for i in range(n):
            hr = rows[i] // 2
            mine = pl.ds(pl.multiple_of(c * hr, 8), hr)
            out.append((refs[i], refs[n + i].at[me], (x, y, 1 - c)))
            for m in CHIP_FLIPS:
                out.append((refs[i].at[mine], refs[n + i].at[me, mine], (*_flip(x, y, m), c)))
        return out
    return plan


def plan_gather_d2d(n, rows):
    def plan(refs):
        x, y, c = _coords()
        me = 2 * x + y
        out = []
        for i in range(n):
            hr = rows[i] // 2
            mine = pl.ds(pl.multiple_of(c * hr, 8), hr)
            for m in CHIP_FLIPS:
                slot = refs[i].at[me ^ m, mine]
                out.append((slot, slot, (x, y, 1 - c)))
        return out
    return plan


def plan_swap(n, rows):
    def plan(refs):
        x, y, c = _coords()
        out = []
        for i in range(n):
            hr = rows[i] // 2
            other = pl.ds(pl.multiple_of((1 - c) * hr, 8), hr)
            out.append((refs[i].at[pl.ds(0, N_CHIPS), other], refs[n + i], (x, y, 1 - c)))
        return out
    return plan


def plan_exchange(n):
    def plan(refs):
        x, y, c = _coords()
        me = 2 * x + y
        out = []
        for i in range(n):
            for m in CHIP_FLIPS:
                out.append((refs[i].at[me ^ m], refs[n + i].at[me], (*_flip(x, y, m), c)))
        return out
    return plan


def plan_share(n, rows):
    def plan(refs):
        x, y, c = _coords()
        out = []
        for i in range(n):
            hr = rows[i] // 2
            mine = refs[i].at[pl.ds(pl.multiple_of(c * hr, 8), hr)]
            out.append((mine, mine, (x, y, 1 - c)))
        return out
    return plan


VM = pl.BlockSpec(memory_space=pltpu.VMEM)


def small_gather(v, name):
    def body(v_ref, out_ref, send_sems, recv_sems):
        x, y, c = _coords()
        me = 2 * x + y
        out_ref[me] = v_ref[...]
        cps = []
        for j, m in enumerate(CHIP_FLIPS):
            cp = _remote(v_ref, out_ref.at[me], send_sems, recv_sems, j, (*_flip(x, y, m), c))
            cp.start()
            cps.append(cp)
        for cp in cps:
            cp.wait()

    return pl.pallas_call(
        body, in_specs=[VM], out_specs=VM, out_shape=SDS((N_CHIPS,) + v.shape, v.dtype),
        scratch_shapes=[pltpu.SemaphoreType.DMA((3,)), pltpu.SemaphoreType.DMA((3,))],
        compiler_params=pltpu.CompilerParams(vmem_limit_bytes=VMEM_LIMIT_BYTES), name=name,
    )(v)


def plan_small_swap(refs):
    x, y, c = _coords()
    return [(refs[0], refs[1], (x, y, 1 - c))]


def plan_small_exchange(refs):
    x, y, c = _coords()
    me = 2 * x + y
    return [(refs[0].at[me], refs[0].at[me], (*_flip(x, y, m), c)) for m in CHIP_FLIPS]


def small_pair(v, sib, name):
    rows, cols = v.shape
    tr = _sum_rows_tile(rows, cols)

    def body(v_ref, s_ref, o_ref):
        o_ref[...] = v_ref[...] + s_ref[...]

    blk = BS((tr, cols), lambda i: (i, 0))
    return pl.pallas_call(
        body, grid=(rows // tr,), in_specs=[blk, blk],
        out_specs=BS((None, tr, cols), lambda i: (2 * lax.axis_index("x") + lax.axis_index("y"), i, 0)),
        out_shape=SDS((N_CHIPS, rows, cols), F32), compiler_params=_cp("parallel"), name=name,
    )(v, sib)


def small_total(slots, name):
    _, rows, cols = slots.shape
    tr = _sum_rows_tile(rows, cols)

    def body(s_ref, o_ref):
        o_ref[...] = (s_ref[0] + s_ref[1]) + (s_ref[2] + s_ref[3])

    return pl.pallas_call(
        body, grid=(rows // tr,), in_specs=[BS((N_CHIPS, tr, cols), lambda i: (0, i, 0))],
        out_specs=BS((tr, cols), lambda i: (i, 0)), out_shape=SDS((rows, cols), F32),
        compiler_params=_cp("parallel"), name=name,
    )(slots)


def adamw(w, g_list, m, v, name):
    nl, rows, cols = w.shape
    tr = _sum_rows_tile(rows, cols) if rows % 16 == 0 else rows
    bc1 = 1.0 - ADAM_B1 ** ADAM_STEP
    bc2 = 1.0 - ADAM_B2 ** ADAM_STEP

    def body(*refs):
        w_ref, m_ref, v_ref = refs[:3]
        g_refs = refs[3:3 + nl]
        go_ref, d_ref, mo_ref, vo_ref = refs[3 + nl:]
        layer = pl.program_id(0)
        for l in range(nl):
            @pl.when(layer == l)
            def _(l=l):
                g = g_refs[l][...]
                m_new = ADAM_B1 * m_ref[...] + (1.0 - ADAM_B1) * g
                v_new = ADAM_B2 * v_ref[...] + (1.0 - ADAM_B2) * (g * g)
                m_hat = m_new / bc1
                v_hat = v_new / bc2
                go_ref[...] = g
                d_ref[...] = -ADAM_LR * (m_hat / (jnp.sqrt(v_hat) + ADAM_EPS) + ADAM_WD * w_ref[...])
                mo_ref[...] = m_new
                vo_ref[...] = v_new

    stk = BS((None, tr, cols), lambda l, i: (l, i, 0))
    flat = BS((tr, cols), lambda l, i: (i, 0))
    out = SDS((nl, rows, cols), F32)
    return pl.pallas_call(
        body, grid=(nl, rows // tr), in_specs=[stk] * 3 + [flat] * nl, out_specs=[stk] * 4,
        out_shape=[out] * 4, compiler_params=_cp("parallel", "parallel"), name=name,
    )(w, m, v, *g_list)


WEIGHTS = ["mem_norm_g", "a_pre_mix_g", "a_post_mix_g", "a_pre_ffn_g", "a_post_ffn_g", "a_w_in", "a_conv_w",
           "a_conv_b", "a_gate_a_w", "a_gate_a_b", "a_gate_x_w", "a_gate_x_b", "a_lambda", "a_w_mem_kv", "a_w_out",
           "a_w_ffn_in", "a_w_ffn_out", "kv_norm_g", "w_kv_shared", "b_pre_mix_g", "b_post_mix_g", "b_pre_ffn_g",
           "b_post_ffn_g", "b_w_in", "b_w_mem_kv", "b_w_out", "b_w_ffn_in", "b_w_ffn_out"]
BIG = {"a_w_in": True, "a_w_mem_kv": False, "a_w_out": False, "a_w_ffn_in": True, "a_w_ffn_out": False,
       "w_kv_shared": True, "b_w_in": False, "b_w_mem_kv": False, "b_w_out": False, "b_w_ffn_in": True,
       "b_w_ffn_out": False}
SHARDED_SMALL = ["a_pre_mix_g", "a_post_mix_g", "a_pre_ffn_g", "a_post_ffn_g", "a_conv_w", "a_conv_b", "a_gate_a_b",
                 "a_gate_x_b", "a_lambda"]
REPL_SMALL = ["mem_norm_g", "kv_norm_g", "b_pre_mix_g", "b_post_mix_g", "b_pre_ffn_g", "b_post_ffn_g", "a_gate_a_w",
              "a_gate_x_w"]
LANES = 128


def _pack(arrs, row_multiple=8):
    flat = jnp.concatenate([a.reshape(-1) for a in arrs])
    pad = -flat.shape[0] % (LANES * row_multiple)
    if pad:
        flat = jnp.concatenate([flat, jnp.zeros((pad,), flat.dtype)])
    return flat.reshape(-1, LANES)


def _unpack(packed, shapes):
    flat = packed.reshape(-1)
    out, pos = [], 0
    for sh in shapes:
        size = math.prod(sh)
        out.append(flat[pos:pos + size].reshape(sh))
        pos += size
    return out


def kernel(x, mem, mem_norm_g, a_pre_mix_g, a_post_mix_g, a_pre_ffn_g, a_post_ffn_g, a_w_in, a_conv_w, a_conv_b,
           a_gate_a_w, a_gate_a_b, a_gate_x_w, a_gate_x_b, a_lambda, a_w_mem_kv, a_w_out, a_w_ffn_in, a_w_ffn_out,
           kv_norm_g, w_kv_shared, b_pre_mix_g, b_post_mix_g, b_pre_ffn_g, b_post_ffn_g, b_w_in, b_w_mem_kv, b_w_out,
           b_w_ffn_in, b_w_ffn_out, loss_target, m_mem_norm_g, m_a_pre_mix_g, m_a_post_mix_g, m_a_pre_ffn_g,
           m_a_post_ffn_g, m_a_w_in, m_a_conv_w, m_a_conv_b, m_a_gate_a_w, m_a_gate_a_b, m_a_gate_x_w, m_a_gate_x_b,
           m_a_lambda, m_a_w_mem_kv, m_a_w_out, m_a_w_ffn_in, m_a_w_ffn_out, m_kv_norm_g, m_w_kv_shared, m_b_pre_mix_g,
           m_b_post_mix_g, m_b_pre_ffn_g, m_b_post_ffn_g, m_b_w_in, m_b_w_mem_kv, m_b_w_out, m_b_w_ffn_in, m_b_w_ffn_out,
           v_mem_norm_g, v_a_pre_mix_g, v_a_post_mix_g, v_a_pre_ffn_g, v_a_post_ffn_g, v_a_w_in, v_a_conv_w, v_a_conv_b,
           v_a_gate_a_w, v_a_gate_a_b, v_a_gate_x_w, v_a_gate_x_b, v_a_lambda, v_a_w_mem_kv, v_a_w_out, v_a_w_ffn_in,
           v_a_w_ffn_out, v_kv_norm_g, v_w_kv_shared, v_b_pre_mix_g, v_b_post_mix_g, v_b_pre_ffn_g, v_b_post_ffn_g,
           v_b_w_in, v_b_w_mem_kv, v_b_w_out, v_b_w_ffn_in, v_b_w_ffn_out):
    a = dict(locals())
    xi, yi, ci = _coords()
    chip = 2 * xi + yi

    got = small_gather(_pack([a[n] for n in SHARDED_SMALL]), "small_gather")
    per_chip = [_unpack(got[s], [a[n].shape for n in SHARDED_SMALL]) for s in range(N_CHIPS)]
    small = {n: jnp.concatenate([per_chip[s][k] for s in range(N_CHIPS)], axis=-1)
             for k, n in enumerate(SHARDED_SMALL)}
    small.update({n: a[n] for n in REPL_SMALL})

    groups = []
    for l in range(4):
        p, j = ("a", l) if l < 2 else ("b", l - 2)
        groups.append([(p + "_" + n, j) for n in ("w_in", "w_mem_kv", "w_out")])
        groups.append([(p + "_" + n, j) for n in ("w_ffn_in", "w_ffn_out")])
    groups[3].append(("w_kv_shared", None))
    big = {n: [None, None] for n in BIG if n != "w_kv_shared"}
    gs, gb = {}, {}
    reduced = {n: [None, None] for n in BIG if n != "w_kv_shared"}

    def put(store, n, j, val):
        if j is None:
            store[n] = val
        else:
            store[n][j] = val

    class Exchange:
        def __init__(self):
            self.state = {}

        def gather_ici(self, g):
            shards = [(a[n] if j is None else a[n][j]).astype(MXU) for n, j in groups[g]]
            rows = [sh.shape[0] for sh in shards]
            lands = [lax.empty((N_CHIPS,) + sh.shape, sh.dtype) for sh in shards]
            plan = plan_gather_ici(len(shards), rows)
            ss, rs, bufs, tok = split_start("gather_ici_%d" % g, shards + lands, plan, 4 * len(shards))
            self.state["g", g] = (ss, rs, bufs, plan, rows)
            return tok

        def gather_d2d(self, g, after):
            ss, rs, bufs, plan, rows = self.state.pop(("g", g))
            n = len(rows)
            outs = split_wait("gather_ici_wait_%d" % g, ss, rs, bufs, after, plan)[n:]
            plan = plan_gather_d2d(n, rows)
            ss, rs, bufs, tok = split_start("gather_d2d_%d" % g, outs, plan, 3 * n)
            self.state["g", g] = (ss, rs, bufs, plan)
            return tok

        def gather_done(self, g, after):
            ss, rs, bufs, plan = self.state.pop(("g", g))
            outs = split_wait("gather_d2d_wait_%d" % g, ss, rs, bufs, after, plan)
            for (n, j), w in zip(groups[g], outs):
                put(big, n, j, w if BIG[n] else w.reshape(-1, w.shape[-1]))

        def rs_swap(self, g):
            grads = []
            for n, j in groups[g]:
                gr = gb[n] if j is None else gb[n][j]
                grads.append(gr if BIG[n] else gr.reshape(N_CHIPS, gr.shape[0] // N_CHIPS, gr.shape[1]))
            rows = [gr.shape[1] for gr in grads]
            lands = [lax.empty((N_CHIPS, gr.shape[1] // 2, gr.shape[2]), gr.dtype) for gr in grads]
            plan = plan_swap(len(grads), rows)
            ss, rs, bufs, tok = split_start("rs_swap_%d" % g, grads + lands, plan, len(grads))
            self.state["r", g] = (ss, rs, bufs, plan, rows)
            return tok

        def rs_exchange(self, g, after):
            ss, rs, bufs, plan, rows = self.state.pop(("r", g))
            n = len(rows)
            bufs = split_wait("rs_swap_wait_%d" % g, ss, rs, bufs, after, plan)
            sums = [half_sum(gr, got, "rs_half_sum") for gr, got in zip(bufs[:n], bufs[n:])]
            plan = plan_exchange(n)
            ss, rs, bufs, tok = split_start("rs_exchange_%d" % g, [p for p, _ in sums] + [s for _, s in sums], plan,
                                            3 * n)
            self.state["r", g] = (ss, rs, bufs, plan, rows)
            return tok

        def rs_share(self, g, after):
            ss, rs, bufs, plan, rows = self.state.pop(("r", g))
            n = len(rows)
            slots = split_wait("rs_exchange_wait_%d" % g, ss, rs, bufs, after, plan)[n:]
            fulls = [slot_sum(s, "rs_slot_sum") for s in slots]
            plan = plan_share(n, rows)
            ss, rs, bufs, tok = split_start("rs_share_%d" % g, fulls, plan, n)
            self.state["r", g] = (ss, rs, bufs, plan)
            return tok

        def rs_done(self, g, after):
            ss, rs, bufs, plan = self.state.pop(("r", g))
            outs = split_wait("rs_share_wait_%d" % g, ss, rs, bufs, after, plan)
            for (n, j), r in zip(groups[g], outs):
                put(reduced, n, j, r)

        def hook(self, where, l, after):
            mix, ffn = 2 * l, 2 * l + 1
            toks = []
            if where == "fwd_begin":
                if l == 0:
                    tok = self.gather_ici(mix)
                    tok = self.gather_d2d(mix, tok)
                    self.gather_done(mix, tok)
                toks.append(self.gather_ici(ffn))
            elif where == "fwd_q1":
                toks.append(self.gather_d2d(ffn, after))
            elif where == "fwd_mid":
                self.gather_done(ffn, after)
                if l < 3:
                    toks.append(self.gather_ici(mix + 2))
            elif where == "fwd_q3":
                if l < 3:
                    toks.append(self.gather_d2d(mix + 2, after))
            elif where == "fwd_end":
                if l < 3:
                    self.gather_done(mix + 2, after)
            elif where == "bwd_begin":
                if l < 3:
                    self.rs_done(ffn + 2, after)
                    toks.append(self.rs_exchange(mix + 2, after))
            elif where == "bwd_mid1":
                if l < 3:
                    toks.append(self.rs_share(mix + 2, after))
            elif where == "bwd_mid2":
                if l < 3:
                    self.rs_done(mix + 2, after)
                toks.append(self.rs_swap(ffn))
            elif where == "bwd_m1":
                toks.append(self.rs_exchange(ffn, after))
            elif where == "bwd_end":
                toks.append(self.rs_share(ffn, after))
                toks.append(self.rs_swap(mix))
                if l == 0:
                    self.rs_done(ffn, toks[0])
                    tok = self.rs_exchange(mix, toks[1])
                    tok = self.rs_share(mix, adamw_big([n for n in BIG if n.startswith("b_")], tok))
                    self.rs_done(mix, tok)
                    toks = []
            else:
                raise ValueError(where)
            return toks

    res = {}

    def adamw_big(names, token=None):
        last = None
        for n in names:
            shape = a[n].shape
            rows, cols = shape[-2], shape[-1]
            stk = (-1, rows, cols)
            grads = reduced[n] if isinstance(reduced[n], list) else [reduced[n]]
            if token is not None:
                grads = [tie(grads[0], [token], "tie_adamw_" + n)] + grads[1:]
            outs = adamw(a[n].reshape(stk), grads, a["m_" + n].reshape(stk), a["v_" + n].reshape(stk), "adamw")
            res[n] = [o.reshape(shape) for o in outs]
            last = outs[1]
            token = last if token is not None else None
        return last

    loss_parts, dx = _fwd_bwd(x[0], mem[0], loss_target[0], small, big, gs, gb, Exchange())
    loss = lax.psum(jnp.sum(loss_parts) * (0.5 / D), ("x", "y", "c"))

    def full(n):
        g = gs[n]
        return jnp.stack(g) if isinstance(g, list) else g

    order = SHARDED_SMALL + REPL_SMALL
    full_shapes = [full(n).shape for n in order]
    pack = _pack([full(n) for n in order], 512)
    ss, rs, bufs, tok = split_start("small_swap", [pack, lax.empty(pack.shape, F32)], plan_small_swap, 1)
    mine_v, sib_v = split_wait("small_swap_wait", ss, rs, bufs, tok, plan_small_swap)
    ss, rs, bufs, tok = split_start("small_exchange", [small_pair(mine_v, sib_v, "small_pair")],
                                    plan_small_exchange, 3)
    last = adamw_big([n for n in BIG if n not in res], tok)
    slots = split_wait("small_exchange_wait", ss, rs, bufs, last, plan_small_exchange)[0]
    summed = _unpack(small_total(slots, "small_total"), full_shapes)
    mine = []
    for n, g in zip(order, summed):
        if n in SHARDED_SMALL:
            width = a[n].shape[-1]
            g = lax.dynamic_slice_in_dim(g, chip * width, width, axis=g.ndim - 1)
        mine.append(g.reshape(a[n].shape))
    shapes = [a[n].shape for n in order]
    rm = 512
    outs = adamw(_pack([a[n] for n in order], rm)[None], [_pack(mine, rm)],
                 _pack([a["m_" + n] for n in order], rm)[None], _pack([a["v_" + n] for n in order], rm)[None],
                 "adamw_small")
    unpacked = [_unpack(o[0], shapes) for o in outs]
    for k, n in enumerate(order):
        res[n] = [u[k] for u in unpacked]

    return (loss, dx[None], *[res[n][0] for n in WEIGHTS], *[res[n][1] for n in WEIGHTS],
            *[res[n][2] for n in WEIGHTS], *[res[n][3] for n in WEIGHTS])
```

```python
import math

import jax
import jax.numpy as jnp
from jax import lax
from jax.experimental import pallas as pl
from jax.experimental.pallas import tpu as pltpu

D = 2048
HD = 128
MEM_W = 512
MEM_HEADS = 4
MIX_W = D - MEM_W
N_BLK = MIX_W // HD
D_FF = 5632
N_MEM = 256
RMS_EPS = 1e-6
NEG_INF = -1e30
LRU_C = 8.0
DIL_GROUPS = ((128, 1), (512, 4), (2048, 16))
Q_BLOCK = 128
SCALE = HD ** -0.5
N_CHIPS = 4
MXU_COLS = 256
ACC_CHUNK = 2 * MXU_COLS

ADAM_LR = 0.001
ADAM_B1 = 0.9
ADAM_B2 = 0.999
ADAM_EPS = 1e-08
ADAM_WD = 0.01
ADAM_STEP = 10

MXU = jnp.bfloat16
F32 = jnp.float32
VMEM_LIMIT_BYTES = 56 * 1024 * 1024

BS = pl.BlockSpec
SDS = jax.ShapeDtypeStruct
MESH = pl.DeviceIdType.MESH


def _cp(*sem):
    return pltpu.CompilerParams(dimension_semantics=sem or None, vmem_limit_bytes=VMEM_LIMIT_BYTES)


def _dot(a, b, dn=((1,), (0,))):
    return lax.dot_general(a, b, (dn, ((), ())), preferred_element_type=F32)


def _div(i, n):
    return lax.div(i, jnp.int32(n))


def _rem(i, n):
    return lax.rem(i, jnp.int32(n))


NN = ((1,), (0,))
NT = ((1,), (1,))
TN = ((0,), (0,))


def _sigmoid(z):
    return 0.5 * jnp.tanh(0.5 * z) + 0.5


def _log1p_pos(u):
    return jnp.where(u < 1e-2, u * (1.0 - u * (0.5 - u * (1.0 / 3.0))), jnp.log(1.0 + u))


def _neg_expm1(z):
    return jnp.where(z > -1e-2, -z * (1.0 + z * (0.5 + z * (1.0 / 6.0))), 1.0 - jnp.exp(z))


def _softplus(z):
    return jnp.maximum(z, 0.0) + _log1p_pos(jnp.exp(-jnp.abs(z)))


_GELU_C = math.sqrt(2.0 / math.pi)


def _gelu_and_grad(x):
    x2 = x * x
    t = jnp.tanh(_GELU_C * (x + 0.044715 * x * x2))
    g = 0.5 * x * (1.0 + t)
    dg = 0.5 * (1.0 + t) + 0.5 * x * (1.0 - t * t) * _GELU_C * (1.0 + 3.0 * 0.044715 * x2)
    return g, dg


def _row_tile(rows):
    return min(256, rows)


def norm_cast(x, g, name):
    rows = x.shape[0]
    tr = _row_tile(rows)

    def body(x_ref, g_ref, o_ref):
        xv = x_ref[...]
        r = lax.rsqrt(jnp.mean(xv * xv, axis=-1, keepdims=True) + RMS_EPS)
        o_ref[...] = (xv * r * g_ref[...]).astype(o_ref.dtype)

    return pl.pallas_call(
        body, grid=(rows // tr,),
        in_specs=[BS((tr, D), lambda i: (i, 0)), BS((1, D), lambda i: (0, 0))],
        out_specs=BS((tr, D), lambda i: (i, 0)),
        out_shape=SDS((rows, D), MXU), compiler_params=_cp("parallel"), name=name,
    )(x, g.reshape(1, D))


def resid_norm(h, y, g, name):
    rows = h.shape[0]
    tr = _row_tile(rows)

    def body(h_ref, y_ref, g_ref, o_ref):
        yv = y_ref[...]
        r = lax.rsqrt(jnp.mean(yv * yv, axis=-1, keepdims=True) + RMS_EPS)
        o_ref[...] = h_ref[...] + yv * r * g_ref[...]

    return pl.pallas_call(
        body, grid=(rows // tr,),
        in_specs=[BS((tr, D), lambda i: (i, 0)), BS((tr, D), lambda i: (i, 0)), BS((1, D), lambda i: (0, 0))],
        out_specs=BS((tr, D), lambda i: (i, 0)),
        out_shape=SDS((rows, D), F32), compiler_params=_cp("parallel"), name=name,
    )(h, y, g.reshape(1, D))


def resid_norm_next(h, y, g, g_next, name):
    rows = h.shape[0]
    tr = min(2 * _row_tile(rows), rows)

    def body(h_ref, y_ref, g_ref, gn_ref, o_ref, n_ref):
        yv = y_ref[...]
        r = lax.rsqrt(jnp.mean(yv * yv, axis=-1, keepdims=True) + RMS_EPS)
        hv = h_ref[...] + yv * r * g_ref[...]
        o_ref[...] = hv
        r2 = lax.rsqrt(jnp.mean(hv * hv, axis=-1, keepdims=True) + RMS_EPS)
        n_ref[...] = (hv * r2 * gn_ref[...]).astype(n_ref.dtype)

    row = BS((tr, D), lambda i: (i, 0))
    vec = BS((1, D), lambda i: (0, 0))
    return pl.pallas_call(
        body, grid=(rows // tr,), in_specs=[row, row, vec, vec], out_specs=[row, row],
        out_shape=[SDS((rows, D), F32), SDS((rows, D), MXU)], compiler_params=_cp("parallel"), name=name,
    )(h, y, g.reshape(1, D), g_next.reshape(1, D))


def _norm_bwd_rows(xv, gv, dyv):
    r = lax.rsqrt(jnp.mean(xv * xv, axis=-1, keepdims=True) + RMS_EPS)
    xhat = xv * r
    dxhat = dyv * gv
    dx = r * (dxhat - xhat * jnp.mean(dxhat * xhat, axis=-1, keepdims=True))
    return dx, jnp.sum(dyv * xhat, axis=0, keepdims=True)


def norm_bwd(x, g, dy, res, out_dtype, name, then=None):
    rows = x.shape[0]
    tr = min(2 * _row_tile(rows), rows)
    has_res = res is not None
    n_in = 3 + has_res + (2 if then else 0)

    def body(*refs):
        x_ref, g_ref, dy_ref = refs[:3]
        dx_ref, dg_ref = refs[n_in], refs[n_in + 1]
        dx, dg = _norm_bwd_rows(x_ref[...], g_ref[...], dy_ref[...].astype(F32))
        if has_res:
            dx = dx + refs[3][...]
        dx_ref[...] = dx.astype(dx_ref.dtype)
        first = pl.program_id(0) == 0

        @pl.when(first)
        def _():
            dg_ref[...] = jnp.zeros_like(dg_ref)

        dg_ref[...] += dg
        if then:
            x2_ref, g2_ref = refs[n_in - 2], refs[n_in - 1]
            dx2_ref, dg2_ref = refs[n_in + 2], refs[n_in + 3]
            dx2, dg2 = _norm_bwd_rows(x2_ref[...], g2_ref[...], dx)
            dx2_ref[...] = dx2.astype(dx2_ref.dtype)

            @pl.when(first)
            def _():
                dg2_ref[...] = jnp.zeros_like(dg2_ref)

            dg2_ref[...] += dg2

    row = BS((tr, D), lambda i: (i, 0))
    vec = BS((1, D), lambda i: (0, 0))
    ins = [x, g.reshape(1, D), dy] + ([res] if has_res else []) + ([then[0], then[1].reshape(1, D)] if then else [])
    outs = pl.pallas_call(
        body, grid=(rows // tr,),
        in_specs=[row, vec, row] + ([row] if has_res else []) + ([row, vec] if then else []),
        out_specs=[row, vec] + ([row, vec] if then else []),
        out_shape=[SDS((rows, D), out_dtype), SDS((1, D), F32)] + ([SDS((rows, D), MXU), SDS((1, D), F32)] if then else []),
        compiler_params=_cp("arbitrary"), name=name,
    )(*ins)
    if then:
        return outs[0], outs[1].reshape(D), outs[2], outs[3].reshape(D)
    return outs[0], outs[1].reshape(D)


def loss_head(y, target, name):
    rows = y.shape[0]
    tr = _row_tile(rows)

    def body(y_ref, t_ref, dy_ref, acc_ref):
        err = y_ref[...] - t_ref[...]
        dy_ref[...] = err * (1.0 / D)

        @pl.when(pl.program_id(0) == 0)
        def _():
            acc_ref[...] = jnp.zeros_like(acc_ref)

        acc_ref[...] += jnp.sum(err * err, axis=0, keepdims=True)

    row = BS((tr, D), lambda i: (i, 0))
    dy, acc = pl.pallas_call(
        body, grid=(rows // tr,), in_specs=[row, row],
        out_specs=[row, BS((1, D), lambda i: (0, 0))],
        out_shape=[SDS((rows, D), F32), SDS((1, D), F32)],
        compiler_params=_cp("arbitrary"), name=name,
    )(y, target)
    return acc, dy


def _mm_call(ins, in_specs, pick, dn, grid, o_spec, out_sds, name):
    gk = grid[2]
    n_in = len(ins)

    def body(*refs):
        o_ref = refs[n_in]
        k = pl.program_id(2)

        def step(a_ref, b_ref):
            acc = o_ref if (out_sds.dtype == F32 or gk == 1) else refs[n_in + 1]
            width = acc.shape[-1]
            if dn == TN or width <= ACC_CHUNK:
                chunks = [(0, width)]
            else:
                chunks = [(c0, min(c0 + ACC_CHUNK, width)) for c0 in range(0, width, ACC_CHUNK)]

            def sweep(first):
                a = a_ref[...]
                pending = None
                for c0, c1 in chunks:
                    p = _dot(a, b_ref[c0:c1, :] if dn == NT else b_ref[:, c0:c1], dn)
                    if pending is not None:
                        put(first, *pending)
                    pending = (c0, c1, p)
                put(first, *pending)

            def put(first, c0, c1, p):
                if first:
                    acc[:, c0:c1] = p.astype(acc.dtype)
                else:
                    acc[:, c0:c1] += p

            if gk == 1:
                sweep(True)
                return

            @pl.when(k == 0)
            def _():
                sweep(True)

            @pl.when(k > 0)
            def _():
                sweep(False)

            if acc is not o_ref:
                @pl.when(k == gk - 1)
                def _():
                    o_ref[...] = acc[...].astype(o_ref.dtype)

        pick(refs[:n_in], k, step)

    scratch = []
    if gk > 1 and out_sds.dtype != F32:
        scratch = [pltpu.VMEM(o_spec.block_shape[-2:], F32)]
    return pl.pallas_call(
        body, grid=grid, in_specs=in_specs, out_specs=o_spec, out_shape=out_sds,
        scratch_shapes=scratch, compiler_params=_cp("parallel", "parallel", "arbitrary"), name=name,
    )(*ins)


def _pick2(refs, k, step):
    step(refs[0], refs[1])


def mm_nn(a, w, *, tm, tn, tk, out_dtype, name):
    m, kdim = a.shape
    if w.ndim == 3:
        c = w.shape[2]
        n = N_CHIPS * c
        per = c // tn
        b_spec = BS((None, tk, tn), lambda i, j, k: (_div(j, per), k, _rem(j, per)))
    else:
        n = w.shape[1]
        b_spec = BS((tk, tn), lambda i, j, k: (k, j))
    grid = (m // tm, n // tn, kdim // tk)
    return _mm_call([a, w], [BS((tm, tk), lambda i, j, k: (i, k)), b_spec], _pick2, NN, grid,
                    BS((tm, tn), lambda i, j, k: (i, j)), SDS((m, n), out_dtype), name)


def mm_nt(a_list, w, *, tm, tn, tk, out_dtype, name):
    m = a_list[0].shape[0]
    ka = a_list[0].shape[1]
    n_a = len(a_list)
    kdim = ka * n_a
    if w.ndim == 3:
        c = w.shape[2]
        n = w.shape[1]
        per = c // tk
        b_spec = BS((None, tn, tk), lambda i, j, k: (_div(k, per), j, _rem(k, per)))
    else:
        n = w.shape[0]
        b_spec = BS((tn, tk), lambda i, j, k: (j, k))
    gk = kdim // tk
    half = gk // n_a
    grid = (m // tm, n // tn, gk)
    if n_a == 1:
        a_specs = [BS((tm, tk), lambda i, j, k: (i, k))]
        pick = lambda refs, k, step: step(refs[0], refs[1])
    else:
        a_specs = [BS((tm, tk), lambda i, j, k: (i, jnp.minimum(k, half - 1))),
                   BS((tm, tk), lambda i, j, k: (i, jnp.maximum(k - half, 0)))]

        def pick(refs, k, step):
            @pl.when(k < half)
            def _():
                step(refs[0], refs[2])

            @pl.when(k >= half)
            def _():
                step(refs[1], refs[2])

    return _mm_call(list(a_list) + [w], a_specs + [b_spec], pick, NT, grid,
                    BS((tm, tn), lambda i, j, k: (i, j)), SDS((m, n), out_dtype), name)


def mm_tn(a, b_list, *, t1, tn, ts, col_shards, name):
    s, k1 = a.shape
    nb = b_list[0].shape[1]
    n_b = len(b_list)
    n = nb * n_b
    gn = n // tn
    half = gn // n_b
    grid = (k1 // t1, gn, s // ts)
    if col_shards:
        c = n // N_CHIPS
        per = c // tn
        o_spec = BS((None, t1, tn), lambda i, j, k: (_div(j, per), i, _rem(j, per)))
        out_sds = SDS((N_CHIPS, k1, c), MXU)
    else:
        o_spec = BS((t1, tn), lambda i, j, k: (i, j))
        out_sds = SDS((k1, n), MXU)
    a_spec = BS((ts, t1), lambda i, j, k: (k, i))
    if n_b == 1:
        b_specs = [BS((ts, tn), lambda i, j, k: (k, j))]
        pick = lambda refs, k, step: step(refs[0], refs[1])
    else:
        b_specs = [BS((ts, tn), lambda i, j, k: (k, jnp.minimum(j, half - 1))),
                   BS((ts, tn), lambda i, j, k: (k, jnp.maximum(j - half, 0)))]

        def pick(refs, k, step):
            j = pl.program_id(1)

            @pl.when(j < half)
            def _():
                step(refs[0], refs[1])

            @pl.when(j >= half)
            def _():
                step(refs[0], refs[2])

    return _mm_call([a] + list(b_list), [a_spec] + b_specs, pick, TN, grid, o_spec, out_sds, name)


def ffn_in_fwd(hn, w, name):
    s = hn.shape[0]
    tm = min(512, s)
    tn = D_FF // 4

    def tail(dag_ref, dau_ref, act_ref, c0, c1, g, u):
        sg = _sigmoid(g)
        silu = g * sg
        dag_ref[:, c0:c1] = (u * sg * (1.0 + g * (1.0 - sg))).astype(dag_ref.dtype)
        dau_ref[:, c0:c1] = silu.astype(dau_ref.dtype)
        act_ref[:, c0:c1] = (silu * u).astype(act_ref.dtype)

    def body(a_ref, wg_ref, wu_ref, dag_ref, dau_ref, act_ref):
        a = a_ref[...]
        pending = None
        for c0 in range(0, tn, ACC_CHUNK):
            c1 = min(c0 + ACC_CHUNK, tn)
            g = _dot(a, wg_ref[:, c0:c1])
            u = _dot(a, wu_ref[:, c0:c1])
            if pending is not None:
                tail(dag_ref, dau_ref, act_ref, *pending)
            pending = (c0, c1, g, u)
        tail(dag_ref, dau_ref, act_ref, *pending)

    tile = BS((tm, tn), lambda j, i: (i, j))
    return pl.pallas_call(
        body, grid=(4, s // tm),
        in_specs=[BS((tm, D), lambda j, i: (i, 0)),
                  BS((None, D, tn), lambda j, i: (_div(j, 2), 0, _rem(j, 2))),
                  BS((None, D, tn), lambda j, i: (2 + _div(j, 2), 0, _rem(j, 2)))],
        out_specs=[tile, tile, tile],
        out_shape=[SDS((s, D_FF), MXU), SDS((s, D_FF), MXU), SDS((s, D_FF), MXU)],
        compiler_params=_cp("parallel", "parallel"), name=name,
    )(hn, w, w)


def ffn_act_bwd(dy, w_out, dag, dau, name):
    s = dy.shape[0]
    tm = min(1024, s)
    tn = D_FF // 4

    def body(dy_ref, w_ref, dag_ref, dau_ref, dg_ref, du_ref):
        def tail(c0, c1, dact):
            dg_ref[:, c0:c1] = (dact * dag_ref[:, c0:c1].astype(F32)).astype(dg_ref.dtype)
            du_ref[:, c0:c1] = (dact * dau_ref[:, c0:c1].astype(F32)).astype(du_ref.dtype)

        dy = dy_ref[...]
        pending = None
        for c0 in range(0, tn, ACC_CHUNK):
            c1 = min(c0 + ACC_CHUNK, tn)
            dact = _dot(dy, w_ref[c0:c1, :], NT)
            if pending is not None:
                tail(*pending)
            pending = (c0, c1, dact)
        tail(*pending)

    tile = BS((tm, tn), lambda j, i: (i, j))
    return pl.pallas_call(
        body, grid=(4, s // tm),
        in_specs=[BS((tm, D), lambda j, i: (i, 0)), BS((tn, D), lambda j, i: (j, 0)), tile, tile],
        out_specs=[tile, tile],
        out_shape=[SDS((s, D_FF), MXU), SDS((s, D_FF), MXU)],
        compiler_params=_cp("parallel", "parallel"), name=name,
    )(dy, w_out, dag, dau)


LRU_T = 256
HALO = 8


def _shift_down(x, k, fill):
    rows = x.shape[0]
    idx = lax.broadcasted_iota(jnp.int32, x.shape, 0)
    return jnp.where(idx < k, fill, pltpu.roll(x, k, 0))


def _shift_up(x, k, fill):
    rows = x.shape[0]
    idx = lax.broadcasted_iota(jnp.int32, x.shape, 0)
    return jnp.where(idx >= rows - k, fill, pltpu.roll(x, rows - k, 0))


def _scan_block(a, b, carry, reverse):
    rows, cols = a.shape
    sub = 8
    in_group = lax.broadcasted_iota(jnp.int32, a.shape, 0) % sub
    for sh in (1, 2, 4):
        if reverse:
            a_s, b_s, ok = pltpu.roll(a, rows - sh, 0), pltpu.roll(b, rows - sh, 0), in_group < sub - sh
        else:
            a_s, b_s, ok = pltpu.roll(a, sh, 0), pltpu.roll(b, sh, 0), in_group >= sh
        b = jnp.where(ok, a * b_s + b, b)
        a = jnp.where(ok, a * a_s, a)
    groups = list(range(rows // sub))
    edge = 0 if reverse else sub - 1
    carry_in = {}
    for v in (reversed(groups) if reverse else groups):
        carry_in[v] = carry
        row = sub * v + edge
        carry = b[row:row + 1, :] + a[row:row + 1, :] * carry
    cin = jnp.concatenate([jnp.broadcast_to(carry_in[v], (sub, cols)) for v in groups], axis=0)
    return b + a * cin


def _conv_taps(xcat):
    rows = xcat.shape[0]
    taps = []
    for k in range(4):
        off = HALO - 3 + k
        taps.append(xcat[off:off + LRU_T] if off == HALO else pltpu.roll(xcat, rows - off, 0)[:LRU_T])
    return taps


def _gates(xc, wa_ref, ba, wx_ref, bx, lam, za_ref, zx_ref):
    xm = xc.astype(MXU)
    for n in range(N_BLK):
        sl = slice(n * HD, (n + 1) * HD)
        za_ref[:, sl] = _dot(xm[:, sl], wa_ref[n])
        zx_ref[:, sl] = _dot(xm[:, sl], wx_ref[n])
    ra = _sigmoid(za_ref[...] + ba)
    ii = _sigmoid(zx_ref[...] + bx)
    sp = _softplus(-lam)
    log_a = -LRU_C * ra * sp
    a = jnp.exp(log_a)
    mult = jnp.sqrt(_neg_expm1(2.0 * log_a))
    return ra, ii, sp, a, mult


def lru_fwd(proj, conv_w, conv_b, wa, ba, wx, bx, lam, name):
    s = proj.shape[0]
    c = MIX_W
    nblk = s // LRU_T
    hpb = LRU_T // HALO

    def body(x_ref, halo_ref, cw_ref, cb_ref, wa_ref, ba_ref, wx_ref, bx_ref, lam_ref,
             xc_ref, h_ref, carry, za_ref, zx_ref):
        i = pl.program_id(0)

        @pl.when(i == 0)
        def _():
            carry[...] = jnp.zeros_like(carry)

        halo = jnp.where(i == 0, 0.0, halo_ref[...])
        xcat = jnp.concatenate([halo, x_ref[...]], axis=0)
        taps = _conv_taps(xcat)
        xc = cb_ref[...] + sum(cw_ref[k:k + 1, :] * taps[k] for k in range(4))
        xc_ref[...] = xc
        _, ii, _, a, mult = _gates(xc, wa_ref, ba_ref[...], wx_ref, bx_ref[...], lam_ref[...], za_ref, zx_ref)
        h = _scan_block(a, mult * (ii * xc), carry[HALO - 1:HALO, :], False)
        h_ref[...] = h
        carry[...] = h[LRU_T - HALO:, :]

    def full(shape):
        return BS(shape, lambda i: (0,) * len(shape))

    blk = BS((LRU_T, c), lambda i: (i, 0))
    return pl.pallas_call(
        body, grid=(nblk,),
        in_specs=[blk, BS((HALO, c), lambda i: (jnp.maximum(i * hpb - 1, 0), 0)),
                  full((4, c)), full((1, c)), full((N_BLK, HD, HD)), full((1, c)),
                  full((N_BLK, HD, HD)), full((1, c)), full((1, c))],
        out_specs=[blk, blk],
        out_shape=[SDS((s, c), F32), SDS((s, c), F32)],
        scratch_shapes=[pltpu.VMEM((HALO, c), F32), pltpu.VMEM((LRU_T, c), F32), pltpu.VMEM((LRU_T, c), F32)],
        compiler_params=_cp("arbitrary"), name=name,
    )(proj, proj, conv_w, conv_b.reshape(1, c), wa.astype(MXU), ba.reshape(1, c), wx.astype(MXU),
      bx.reshape(1, c), lam.reshape(1, c))


def lru_mix_prep(h, proj, m, name):
    s = h.shape[0]
    tr = _row_tile(s)

    def body(h_ref, gb_ref, m_ref, o_ref):
        ge, _ = _gelu_and_grad(gb_ref[...])
        o_ref[:, :MIX_W] = (h_ref[...] * ge).astype(o_ref.dtype)
        o_ref[:, MIX_W:] = m_ref[...]

    return pl.pallas_call(
        body, grid=(s // tr,),
        in_specs=[BS((tr, MIX_W), lambda i: (i, 0)), BS((tr, MIX_W), lambda i: (i, 1)),
                  BS((tr, MEM_W), lambda i: (i, 0))],
        out_specs=BS((tr, D), lambda i: (i, 0)), out_shape=SDS((s, D), MXU),
        compiler_params=_cp("parallel"), name=name,
    )(h, proj, m)


def lru_bwd(dym, proj, xc, hl, dqm, conv_w, wa, ba, wx, bx, lam, name):
    s = proj.shape[0]
    c = MIX_W
    nblk = s // LRU_T
    hpb = LRU_T // HALO
    wa_m = wa.astype(MXU)
    wx_m = wx.astype(MXU)

    def body(dy_ref, x_ref, xhalo_ref, gb_ref, xc_ref, h_ref, hhalo_ref, dqm_ref,
             cw_ref, wa_ref, ba_ref, wx_ref, bx_ref, lam_ref,
             dproj_ref, dcw_ref, dcb_ref, dwa_ref, dba_ref, dwx_ref, dbx_ref, dlam_ref,
             g_next, a_next, dxc_next, za_ref, zx_ref, dxc_ref):
        i = pl.program_id(0)

        @pl.when(i == 0)
        def _():
            g_next[...] = jnp.zeros_like(g_next)
            a_next[...] = jnp.zeros_like(a_next)
            dxc_next[...] = jnp.zeros_like(dxc_next)
            for r in (dcw_ref, dcb_ref, dwa_ref, dba_ref, dwx_ref, dbx_ref, dlam_ref):
                r[...] = jnp.zeros_like(r)

        first = i == nblk - 1
        xc = xc_ref[...]
        lam = lam_ref[...]
        ra, ii, sp, a, mult = _gates(xc, wa_ref, ba_ref[...], wx_ref, bx_ref[...], lam, za_ref, zx_ref)
        hl_v = h_ref[...]
        ge, dge = _gelu_and_grad(gb_ref[...])
        dyl = dy_ref[...]
        dhl = dyl * ge
        dproj_ref[:, c:2 * c] = (dyl * hl_v * dge).astype(dproj_ref.dtype)
        dproj_ref[:, 2 * c:] = dqm_ref[...]

        an = _shift_up(a, 1, 0.0)
        last_row = lax.broadcasted_iota(jnp.int32, a.shape, 0) == LRU_T - 1
        an = jnp.where(last_row, a_next[0:1, :], an)
        g = _scan_block(an, dhl, g_next[0:1, :], True)
        g_next[...] = g[:HALO, :]
        a_next[...] = a[:HALO, :]

        hhalo = jnp.where(first, 0.0, hhalo_ref[...])
        h_prev = _shift_down(hl_v, 1, 0.0)
        first_row = lax.broadcasted_iota(jnp.int32, a.shape, 0) == 0
        h_prev = jnp.where(first_row, hhalo[HALO - 1:HALO, :], h_prev)
        da = g * h_prev
        ixc = ii * xc
        dmult = g * ixc
        dii = g * mult * xc
        dxc = g * mult * ii
        dlog_a = (da - dmult * a / mult) * a
        dra = dlog_a * (-LRU_C) * sp
        dlam_ref[...] += jnp.sum(dlog_a * ra, axis=0, keepdims=True) * (LRU_C * _sigmoid(-lam))
        dza = dra * ra * (1.0 - ra)
        dzx = dii * ii * (1.0 - ii)
        dba_ref[...] += jnp.sum(dza, axis=0, keepdims=True)
        dbx_ref[...] += jnp.sum(dzx, axis=0, keepdims=True)
        xm = xc.astype(MXU)
        dza_m = dza.astype(MXU)
        dzx_m = dzx.astype(MXU)
        for n in range(N_BLK):
            sl = slice(n * HD, (n + 1) * HD)
            dwa_ref[n] += _dot(xm[:, sl], dza_m[:, sl], TN)
            dwx_ref[n] += _dot(xm[:, sl], dzx_m[:, sl], TN)
            dxc_ref[:, sl] = _dot(dza_m[:, sl], wa_ref[n], NT) + _dot(dzx_m[:, sl], wx_ref[n], NT)
        dxc = dxc + dxc_ref[...]

        dcat = jnp.concatenate([dxc, dxc_next[...]], axis=0)
        rows = dcat.shape[0]
        dxb = cw_ref[3:4, :] * dxc
        for k in range(3):
            dxb = dxb + cw_ref[k:k + 1, :] * pltpu.roll(dcat, rows - (3 - k), 0)[:LRU_T]
        dproj_ref[:, :c] = dxb.astype(dproj_ref.dtype)
        dxc_next[...] = dxc[:HALO, :]

        xhalo = jnp.where(first, 0.0, xhalo_ref[...])
        taps = _conv_taps(jnp.concatenate([xhalo, x_ref[...]], axis=0))
        for k in range(4):
            dcw_ref[k:k + 1, :] += jnp.sum(dxc * taps[k], axis=0, keepdims=True)
        dcb_ref[...] += jnp.sum(dxc, axis=0, keepdims=True)

    def full(shape):
        return BS(shape, lambda i: (0,) * len(shape))

    def rev(i):
        return nblk - 1 - i

    blk0 = BS((LRU_T, c), lambda i: (rev(i), 0))
    blk1 = BS((LRU_T, c), lambda i: (rev(i), 1))
    halo = BS((HALO, c), lambda i: (jnp.maximum(rev(i) * hpb - 1, 0), 0))
    outs = pl.pallas_call(
        body, grid=(nblk,),
        in_specs=[blk0, blk0, halo, blk1, blk0, blk0, halo, BS((LRU_T, MEM_W), lambda i: (rev(i), 0)),
                  full((4, c)), full((N_BLK, HD, HD)), full((1, c)), full((N_BLK, HD, HD)), full((1, c)),
                  full((1, c))],
        out_specs=[BS((LRU_T, 2 * c + MEM_W), lambda i: (rev(i), 0)), full((4, c)), full((1, c)),
                   full((N_BLK, HD, HD)), full((1, c)), full((N_BLK, HD, HD)), full((1, c)), full((1, c))],
        out_shape=[SDS((s, 2 * c + MEM_W), MXU), SDS((4, c), F32), SDS((1, c), F32),
                   SDS((N_BLK, HD, HD), F32), SDS((1, c), F32), SDS((N_BLK, HD, HD), F32), SDS((1, c), F32),
                   SDS((1, c), F32)],
        scratch_shapes=[pltpu.VMEM((HALO, c), F32), pltpu.VMEM((HALO, c), F32), pltpu.VMEM((HALO, c), F32),
                        pltpu.VMEM((LRU_T, c), F32), pltpu.VMEM((LRU_T, c), F32), pltpu.VMEM((LRU_T, c), F32)],
        compiler_params=_cp("arbitrary"), name=name,
    )(dym, proj, proj, proj, xc, hl, hl, dqm, conv_w, wa_m, ba.reshape(1, c), wx_m, bx.reshape(1, c),
      lam.reshape(1, c))
    dproj, dcw, dcb, dwa, dba, dwx, dbx, dlam = outs
    return dproj, dcw, dcb.reshape(c), dwa, dba.reshape(c), dwx, dbx.reshape(c), dlam.reshape(c)


def _mem_probs(q, kv):
    heads = [slice(hh * HD, (hh + 1) * HD) for hh in range(MEM_HEADS)]
    sc = [_dot(q[:, sl], kv[:, sl], NT) * SCALE for sl in heads]
    e = [jnp.exp(s - jnp.max(s, axis=-1, keepdims=True)) for s in sc]
    return [x / jnp.sum(x, axis=-1, keepdims=True) for x in e]


def mem_attn_fwd(proj, q_col, kvm, name):
    s = proj.shape[0]
    tq = min(512, s)

    def body(q_ref, kv_ref, o_ref):
        q = q_ref[...].astype(MXU)
        kv = kv_ref[...]
        p = _mem_probs(q, kv)
        outs = [_dot(p[hh].astype(MXU), kv[:, MEM_W + hh * HD:MEM_W + (hh + 1) * HD]) for hh in range(MEM_HEADS)]
        o_ref[...] = jnp.concatenate(outs, axis=1).astype(o_ref.dtype)

    return pl.pallas_call(
        body, grid=(s // tq,),
        in_specs=[BS((tq, MEM_W), lambda i: (i, q_col)), BS((N_MEM, 2 * MEM_W), lambda i: (0, 0))],
        out_specs=BS((tq, MEM_W), lambda i: (i, 0)), out_shape=SDS((s, MEM_W), MXU),
        compiler_params=_cp("parallel"), name=name,
    )(proj, kvm)


def mem_attn_bwd(proj, q_col, kvm, dym, name):
    s = proj.shape[0]
    tq = min(512, s)

    def body(q_ref, kv_ref, do_ref, dq_ref, dkv_ref):
        @pl.when(pl.program_id(0) == 0)
        def _():
            dkv_ref[...] = jnp.zeros_like(dkv_ref)

        q = q_ref[...].astype(MXU)
        do = do_ref[...].astype(MXU)
        kv = kv_ref[...]
        heads = [slice(hh * HD, (hh + 1) * HD) for hh in range(MEM_HEADS)]
        p = _mem_probs(q, kv)
        dp = [_dot(do[:, sl], kv[:, MEM_W + hh * HD:MEM_W + (hh + 1) * HD], NT) for hh, sl in enumerate(heads)]
        ds = [(pp * (d - jnp.sum(pp * d, axis=-1, keepdims=True)) * SCALE).astype(MXU) for pp, d in zip(p, dp)]
        dq = [_dot(x, kv[:, sl]) for x, sl in zip(ds, heads)]
        dk = [_dot(x, q[:, sl], TN) for x, sl in zip(ds, heads)]
        dv = [_dot(pp.astype(MXU), do[:, sl], TN) for pp, sl in zip(p, heads)]
        dq_ref[...] = jnp.concatenate(dq, axis=1).astype(dq_ref.dtype)
        dkv_ref[...] += jnp.concatenate(dk + dv, axis=1)

    return pl.pallas_call(
        body, grid=(s // tq,),
        in_specs=[BS((tq, MEM_W), lambda i: (i, q_col)), BS((N_MEM, 2 * MEM_W), lambda i: (0, 0)),
                  BS((tq, MEM_W), lambda i: (i, MIX_W // MEM_W))],
        out_specs=[BS((tq, MEM_W), lambda i: (i, 0)), BS((N_MEM, 2 * MEM_W), lambda i: (0, 0))],
        out_shape=[SDS((s, MEM_W), MXU), SDS((N_MEM, 2 * MEM_W), F32)],
        compiler_params=_cp("arbitrary"), name=name,
    )(proj, kvm, dym)


def _dil_scores(q, kp, kc, n, slope_dil):
    qi = lax.broadcasted_iota(jnp.int32, (Q_BLOCK, Q_BLOCK), 0)
    ki = lax.broadcasted_iota(jnp.int32, (Q_BLOCK, Q_BLOCK), 1)
    rel_p = qi + Q_BLOCK - ki
    rel_c = qi - ki
    s_p = _dot(q, kp, NT) * SCALE - slope_dil * rel_p.astype(F32)
    s_c = _dot(q, kc, NT) * SCALE - slope_dil * rel_c.astype(F32)
    s_p = jnp.where((rel_p <= Q_BLOCK) & (n > 0), s_p, NEG_INF)
    s_c = jnp.where(rel_c >= 0, s_c, NEG_INF)
    return s_p, s_c


def _slope_dil(gi, hh):
    head = 4 * gi + hh
    return DIL_GROUPS[gi][1] * 2.0 ** (-8.0 * (head + 1.0) / N_BLK)


def _dil_operands(proj, kv, gi):
    dil = DIL_GROUPS[gi][1]
    if dil == 1:
        return proj, kv, kv, (lambda r: gi), (lambda r: gi), (lambda r: MIX_W // MEM_W + gi)
    sub = proj.shape[0] // dil

    def view(a, col):
        return a[:, col:col + MEM_W].reshape(sub, dil * MEM_W)

    same = lambda r: r
    return view(proj, gi * MEM_W), view(kv, gi * MEM_W), view(kv, MIX_W + gi * MEM_W), same, same, same


def dil_attn_fwd(proj, kv, gi, name):
    dil = DIL_GROUPS[gi][1]
    s, pw = proj.shape
    sub = s // dil
    nb = sub // Q_BLOCK
    pair = 2 * Q_BLOCK

    def body(q_ref, kp_ref, kc_ref, vp_ref, vc_ref, o_ref, lse_ref):
        t = pl.program_id(1)
        q = q_ref[...].astype(MXU)
        k_prev, k_cur, v_prev, v_cur = kp_ref[...], kc_ref[...], vp_ref[...], vc_ref[...]
        units = []
        for b in range(2):
            rows = slice(b * Q_BLOCK, (b + 1) * Q_BLOCK)
            kp, vp = (k_prev, v_prev) if b == 0 else (k_cur[:Q_BLOCK], v_cur[:Q_BLOCK])
            for hh in range(4):
                sl = slice(hh * HD, (hh + 1) * HD)
                units.append((q[rows, sl], kp[:, sl], k_cur[rows, sl], vp[:, sl], v_cur[rows, sl], 2 * t + b, hh))
        sc = [_dil_scores(qh, kp, kc, n, _slope_dil(gi, hh)) for qh, kp, kc, _, _, n, hh in units]
        mx = [jnp.maximum(jnp.max(s_p, axis=-1, keepdims=True), jnp.max(s_c, axis=-1, keepdims=True))
              for s_p, s_c in sc]
        den = [jnp.sum(jnp.exp(s_p - m), axis=-1, keepdims=True) + jnp.sum(jnp.exp(s_c - m), axis=-1, keepdims=True)
               for (s_p, s_c), m in zip(sc, mx)]
        lse = [m + jnp.log(d) for m, d in zip(mx, den)]
        pr = [(jnp.exp(s_p - l).astype(MXU), jnp.exp(s_c - l).astype(MXU)) for (s_p, s_c), l in zip(sc, lse)]
        outs = [_dot(p_p, u[3]) + _dot(p_c, u[4]) for (p_p, p_c), u in zip(pr, units)]
        wide = [jnp.broadcast_to(l, (Q_BLOCK, HD)) for l in lse]
        o_ref[...] = jnp.concatenate([jnp.concatenate(outs[4 * b:4 * b + 4], axis=1) for b in range(2)], axis=0)
        lse_ref[...] = jnp.concatenate([jnp.concatenate(wide[4 * b:4 * b + 4], axis=1) for b in range(2)], axis=0)

    one, two = (Q_BLOCK, MEM_W), (pair, MEM_W)
    before = lambda t: jnp.maximum(2 * t - 1, 0)
    out = BS(two, lambda r, t: (t, r))
    qv, kview, vview, qcol, kcol, vcol = _dil_operands(proj, kv, gi)
    return pl.pallas_call(
        body, grid=(dil, nb // 2),
        in_specs=[BS(two, lambda r, t: (t, qcol(r))),
                  BS(one, lambda r, t: (before(t), kcol(r))), BS(two, lambda r, t: (t, kcol(r))),
                  BS(one, lambda r, t: (before(t), vcol(r))), BS(two, lambda r, t: (t, vcol(r)))],
        out_specs=[out, out],
        out_shape=[SDS((sub, dil * MEM_W), F32), SDS((sub, dil * MEM_W), F32)],
        compiler_params=_cp("parallel", "parallel"), name=name,
    )(qv, kview, kview, vview, vview)


def dil_attn_bwd(proj, kv, lse, do, dd, gi, name):
    dil = DIL_GROUPS[gi][1]
    s, pw = proj.shape
    sub = s // dil
    nb = sub // Q_BLOCK
    qc, kc_ = pw // MEM_W, kv.shape[1] // MEM_W

    def body(q_ref, kp_ref, kc_ref, vp_ref, vc_ref, lse_ref, do_ref, dd_ref, dq_ref, dk_ref, dv_ref, ck, cv):
        n = pl.program_id(1)

        @pl.when(n == 0)
        def _():
            ck[...] = jnp.zeros_like(ck)
            cv[...] = jnp.zeros_like(cv)

        @pl.when(n < nb)
        def _():
            q = q_ref[...].astype(MXU)
            do_m = do_ref[...].astype(MXU)
            kp, kc, vp, vc = kp_ref[...], kc_ref[...], vp_ref[...], vc_ref[...]
            lse_v, dd_v, ck_v, cv_v = lse_ref[...], dd_ref[...], ck[...], cv[...]
            heads = [slice(hh * HD, (hh + 1) * HD) for hh in range(4)]
            sc = [_dil_scores(q[:, sl], kp[:, sl], kc[:, sl], n, _slope_dil(gi, hh)) for hh, sl in enumerate(heads)]
            dp = [(_dot(do_m[:, sl], vp[:, sl], NT), _dot(do_m[:, sl], vc[:, sl], NT)) for sl in heads]
            pr = [(jnp.exp(s_p - lse_v[:, sl]), jnp.exp(s_c - lse_v[:, sl])) for (s_p, s_c), sl in zip(sc, heads)]
            ds = [((p_p * (dp_p + dd_v[:, sl]) * SCALE).astype(MXU), (p_c * (dp_c + dd_v[:, sl]) * SCALE).astype(MXU))
                  for (p_p, p_c), (dp_p, dp_c), sl in zip(pr, dp, heads)]
            pm = [(p_p.astype(MXU), p_c.astype(MXU)) for p_p, p_c in pr]
            dq = [_dot(ds_p, kp[:, sl]) + _dot(ds_c, kc[:, sl]) for (ds_p, ds_c), sl in zip(ds, heads)]
            dk = [ck_v[:, sl] + _dot(ds_p, q[:, sl], TN) for (ds_p, _), sl in zip(ds, heads)]
            dv = [cv_v[:, sl] + _dot(p_p, do_m[:, sl], TN) for (p_p, _), sl in zip(pm, heads)]
            ck_new = [_dot(ds_c, q[:, sl], TN) for (_, ds_c), sl in zip(ds, heads)]
            cv_new = [_dot(p_c, do_m[:, sl], TN) for (_, p_c), sl in zip(pm, heads)]
            dq_ref[...] = jnp.concatenate(dq, axis=1).astype(dq_ref.dtype)
            dk_ref[...] = jnp.concatenate(dk, axis=1)
            dv_ref[...] = jnp.concatenate(dv, axis=1)
            ck[...] = jnp.concatenate(ck_new, axis=1)
            cv[...] = jnp.concatenate(cv_new, axis=1)

        @pl.when(n == nb)
        def _():
            dk_ref[...] = ck[...]
            dv_ref[...] = cv[...]

    blk = (Q_BLOCK, MEM_W)
    cur = lambda n: jnp.minimum(n, nb - 1)
    prev = lambda n: jnp.maximum(jnp.minimum(n, nb - 1) - 1, 0)
    done = lambda n: jnp.maximum(n - 1, 0)
    own = BS(blk, lambda r, n: (cur(n), r))
    qv, kview, vview, qcol, kcol, vcol = _dil_operands(proj, kv, gi)
    return pl.pallas_call(
        body, grid=(dil, nb + 1),
        in_specs=[BS(blk, lambda r, n: (cur(n), qcol(r))),
                  BS(blk, lambda r, n: (prev(n), kcol(r))), BS(blk, lambda r, n: (cur(n), kcol(r))),
                  BS(blk, lambda r, n: (prev(n), vcol(r))), BS(blk, lambda r, n: (cur(n), vcol(r))),
                  own, own, own],
        out_specs=[own, BS(blk, lambda r, n: (done(n), r)), BS(blk, lambda r, n: (done(n), r))],
        out_shape=[SDS((sub, dil * MEM_W), MXU), SDS((sub, dil * MEM_W), F32), SDS((sub, dil * MEM_W), F32)],
        scratch_shapes=[pltpu.VMEM(blk, F32), pltpu.VMEM(blk, F32)],
        compiler_params=_cp("parallel", "arbitrary"), name=name,
    )(qv, kview, kview, vview, vview, lse, do, dd)


def _group_weights(lse_refs):
    l0, l1, l2 = (r[...] for r in lse_refs)
    mx = jnp.maximum(jnp.maximum(l0, l1), l2)
    e = [jnp.exp(l - mx) for l in (l0, l1, l2)]
    den = e[0] + e[1] + e[2]
    return [x / den for x in e]


def dil_mix_prep(o_list, lse_list, m, name):
    s = m.shape[0]
    tr = _row_tile(s)

    def body(o0, o1, o2, l0, l1, l2, m_ref, out_ref):
        w = _group_weights((l0, l1, l2))
        for g, o_ref in enumerate((o0, o1, o2)):
            out_ref[:, g * MEM_W:(g + 1) * MEM_W] = (o_ref[...] * w[g]).astype(out_ref.dtype)
        out_ref[:, MIX_W:] = m_ref[...]

    blk = BS((tr, MEM_W), lambda i: (i, 0))
    return pl.pallas_call(
        body, grid=(s // tr,), in_specs=[blk] * 7,
        out_specs=BS((tr, D), lambda i: (i, 0)), out_shape=SDS((s, D), MXU),
        compiler_params=_cp("parallel"), name=name,
    )(*o_list, *lse_list, m)


def dil_mix_bwd(dym, o_list, lse_list, name):
    s = dym.shape[0]
    tr = _row_tile(s)

    def body(da_ref, o0, o1, o2, l0, l1, l2, do0, do1, do2, dd0, dd1, dd2):
        w = _group_weights((l0, l1, l2))
        tot = None
        for g, (o_ref, do_ref) in enumerate(zip((o0, o1, o2), (do0, do1, do2))):
            da = da_ref[:, g * MEM_W:(g + 1) * MEM_W]
            do_ref[...] = da * w[g]
            x = da * o_ref[...]
            dw = jnp.concatenate(
                [jnp.broadcast_to(jnp.sum(x[:, hh * HD:(hh + 1) * HD], axis=-1, keepdims=True), (tr, HD))
                 for hh in range(4)], axis=1)
            tot = w[g] * dw if tot is None else tot + w[g] * dw
        for g, dd_ref in enumerate((dd0, dd1, dd2)):
            dd_ref[...] = -w[g] * tot

    blk = BS((tr, MEM_W), lambda i: (i, 0))
    outs = pl.pallas_call(
        body, grid=(s // tr,), in_specs=[BS((tr, MIX_W), lambda i: (i, 0))] + [blk] * 6,
        out_specs=[blk] * 6, out_shape=[SDS((s, MEM_W), F32)] * 6,
        compiler_params=_cp("parallel"), name=name,
    )(dym, *o_list, *lse_list)
    return outs[:3], outs[3:]


def sum_cast(parts, name):
    s = parts[0][0].shape[0]
    tr = _row_tile(s)
    flat = [a for p in parts for a in p]
    sizes = [len(p) for p in parts]

    def body(*refs):
        out_ref = refs[-1]
        pos = 0
        for j, n in enumerate(sizes):
            acc = refs[pos][...].astype(F32)
            for t in range(1, n):
                acc = acc + refs[pos + t][...].astype(F32)
            out_ref[:, j * MEM_W:(j + 1) * MEM_W] = acc.astype(out_ref.dtype)
            pos += n

    blk = BS((tr, MEM_W), lambda i: (i, 0))
    width = MEM_W * len(parts)
    return pl.pallas_call(
        body, grid=(s // tr,), in_specs=[blk] * len(flat),
        out_specs=BS((tr, width), lambda i: (i, 0)), out_shape=SDS((s, width), MXU),
        compiler_params=_cp("parallel"), name=name,
    )(*flat)


def add_n(arrs, name):
    rows, cols = arrs[0].shape
    tr = _row_tile(rows)

    def body(*refs):
        acc = refs[0][...]
        for r in refs[1:-1]:
            acc = acc + r[...]
        refs[-1][...] = acc

    blk = BS((tr, cols), lambda i: (i, 0))
    return pl.pallas_call(
        body, grid=(rows // tr,), in_specs=[blk] * len(arrs), out_specs=blk,
        out_shape=SDS((rows, cols), F32), compiler_params=_cp("parallel"), name=name,
    )(*arrs)


class _NoExchange:
    def hook(self, where, l, after):
        return []


def _fwd_bwd(x, mem, target, small, big, gs, gb, sched):
    s = x.shape[0]
    tm = min(1024, s)
    ts = min(2048, s)

    def after_hook(arr, where, l, after):
        toks = sched.hook(where, l, after)
        return tie(arr, toks, "tie_%s_%d" % (where, l)) if toks else arr

    h = x
    saved = []
    kv = None
    mem_n = None
    hn = norm_cast(h, small["a_pre_mix_g"][0], "pre_norm")
    for l in range(4):
        rec = l < 2
        p, j = ("a", l) if rec else ("b", l - 2)
        sv = {"h": h}
        hn = after_hook(hn, "fwd_begin", l, h)
        if mem_n is None:
            mem_n = norm_cast(mem, small["mem_norm_g"], "mem_norm")
        kvm = mm_nn(mem_n, big[p + "_w_mem_kv"][j], tm=N_MEM, tn=2 * MEM_W, tk=D, out_dtype=MXU, name="mem_kv")
        if rec:
            proj = mm_nn(hn, big["a_w_in"][j], tm=min(2 * tm, s), tn=896, tk=D, out_dtype=F32, name="rec_in")
            xc, hl = lru_fwd(proj, small["a_conv_w"][j], small["a_conv_b"][j], small["a_gate_a_w"][j],
                             small["a_gate_a_b"][j], small["a_gate_x_w"][j], small["a_gate_x_b"][j],
                             small["a_lambda"][j], "lru_fwd")
            m = mem_attn_fwd(proj, 2 * MIX_W // MEM_W, kvm, "rec_mem_attn")
            ym = lru_mix_prep(hl, proj, m, "lru_mix_prep")
            sv.update(xc=xc, hl=hl)
        else:
            proj = mm_nn(hn, big["b_w_in"][j], tm=tm, tn=D, tk=D, out_dtype=F32, name="dil_in")
            o_list, lse_list = [], []
            for gi in range(3):
                o, lse = dil_attn_fwd(proj, kv, gi, "dil_attn_fwd%d" % gi)
                o_list.append(o.reshape(s, MEM_W))
                lse_list.append(lse.reshape(s, MEM_W))
            m = mem_attn_fwd(proj, MIX_W // MEM_W, kvm, "dil_mem_attn")
            ym = dil_mix_prep(o_list, lse_list, m, "dil_mix_prep")
            sv.update(o=o_list, lse=lse_list)
        ym = after_hook(ym, "fwd_q1", l, ym)
        mix = mm_nn(ym, big[p + "_w_out"][j], tm=tm, tn=D, tk=D, out_dtype=F32, name="mix_out")
        h1, hn2 = resid_norm_next(h, mix, small[p + "_post_mix_g"][j], small[p + "_pre_ffn_g"][j], "post_pre_norm")
        hn2 = after_hook(hn2, "fwd_mid", l, mix)
        g, u, act = ffn_in_fwd(hn2, big[p + "_w_ffn_in"][j], "ffn_in")
        act = after_hook(act, "fwd_q3", l, u)
        y2 = mm_nn(act, big[p + "_w_ffn_out"][j], tm=tm // 2, tn=D, tk=D_FF // 2, out_dtype=F32, name="ffn_out")
        sv.update(kvm=kvm, hn=hn, proj=proj, ym=ym, mix=mix, h1=h1, hn2=hn2, g=g, u=u, act=act, y2=y2)
        saved.append(sv)
        if l < 3:
            pn, jn = ("a", l + 1) if l + 1 < 2 else ("b", l - 1)
            h, hn = resid_norm_next(h1, y2, small[p + "_post_ffn_g"][j], small[pn + "_pre_mix_g"][jn],
                                    "post_pre_norm")
        else:
            h = resid_norm(h1, y2, small[p + "_post_ffn_g"][j], "post_norm")
        sched.hook("fwd_end", l, h)
        if l == 1:
            h_kv = h
            kvn = norm_cast(h, small["kv_norm_g"], "pre_norm")
            kv = mm_nn(kvn, big["w_kv_shared"], tm=tm, tn=768, tk=D, out_dtype=MXU, name="kv_proj")

    loss_parts, dh = loss_head(h, target, "loss_head")

    def stack2(name, j, val):
        gs.setdefault(name, [None, None])[j] = val

    def stack2b(name, j, val):
        gb.setdefault(name, [None, None])[j] = val

    dkv_parts = []
    ahead = []
    dmem_parts = []
    dkvm = [None] * 4
    for l in (3, 2, 1, 0):
        rec = l < 2
        p, j = ("a", l) if rec else ("b", l - 2)
        sv = saved[l]
        if l == 1:
            dkv = sum_cast([(dkv_parts[0][c], dkv_parts[1][c]) for c in range(6)], "dkv_sum")
            dkvn = mm_nt([dkv], big["w_kv_shared"], tm=tm, tn=D, tk=768, out_dtype=MXU, name="kv_proj_dx")
            gb["w_kv_shared"] = mm_tn(kvn, [dkv], t1=D, tn=768, ts=ts, col_shards=True, name="kv_proj_dw")
            dh, gs["kv_norm_g"], *ahead = norm_bwd(h_kv, small["kv_norm_g"], dkvn, dh, F32, "pre_post_norm_bwd",
                                                   then=(sv["y2"], small["a_post_ffn_g"][1]))
        if ahead:
            dy2, dg = ahead
            ahead = []
        else:
            dy2, dg = norm_bwd(sv["y2"], small[p + "_post_ffn_g"][j], dh, None, MXU, "post_norm_bwd")
        dy2 = after_hook(dy2, "bwd_begin", l, dh)
        stack2(p + "_post_ffn_g", j, dg)
        dgg, dgu = ffn_act_bwd(dy2, big[p + "_w_ffn_out"][j], sv["g"], sv["u"], "ffn_act_bwd")
        dgg = after_hook(dgg, "bwd_mid1", l, dgu)
        stack2b(p + "_w_ffn_out", j, mm_tn(sv["act"], [dy2], t1=D_FF // 4, tn=D, ts=ts // 2, col_shards=False,
                                          name="ffn_out_dw"))
        dhn2 = mm_nt([dgg, dgu], big[p + "_w_ffn_in"][j], tm=tm // 2, tn=D, tk=D_FF // 2, out_dtype=MXU,
                     name="ffn_in_dx")
        stack2b(p + "_w_ffn_in", j, mm_tn(sv["hn2"], [dgg, dgu], t1=D // 2, tn=D_FF // 4, ts=ts, col_shards=True,
                                         name="ffn_in_dw"))
        dhn2 = after_hook(dhn2, "bwd_mid2", l, gb[p + "_w_ffn_in"][j])
        dh1, dg, dmix, dg_mix = norm_bwd(sv["h1"], small[p + "_pre_ffn_g"][j], dhn2, dh, F32, "pre_post_norm_bwd",
                                         then=(sv["mix"], small[p + "_post_mix_g"][j]))
        stack2(p + "_pre_ffn_g", j, dg)
        stack2(p + "_post_mix_g", j, dg_mix)
        dym = mm_nt([dmix], big[p + "_w_out"][j], tm=tm, tn=D, tk=D, out_dtype=F32, name="mix_out_dx")
        stack2b(p + "_w_out", j, mm_tn(sv["ym"], [dmix], t1=D, tn=1024, ts=ts, col_shards=False,
                                      name="mix_out_dw"))
        dym = after_hook(dym, "bwd_m1", l, gb[p + "_w_out"][j])
        if rec:
            dqm, dkvm[l] = mem_attn_bwd(sv["proj"], 2 * MIX_W // MEM_W, sv["kvm"], dym, "rec_mem_attn_bwd")
            dproj, dcw, dcb, dwa, dba, dwx, dbx, dlam = lru_bwd(
                dym, sv["proj"], sv["xc"], sv["hl"], dqm, small["a_conv_w"][j], small["a_gate_a_w"][j],
                small["a_gate_a_b"][j], small["a_gate_x_w"][j], small["a_gate_x_b"][j], small["a_lambda"][j],
                "lru_bwd")
            for nm, val in (("a_conv_w", dcw), ("a_conv_b", dcb), ("a_gate_a_w", dwa), ("a_gate_a_b", dba),
                            ("a_gate_x_w", dwx), ("a_gate_x_b", dbx), ("a_lambda", dlam)):
                stack2(nm, j, val)
            dhn = mm_nt([dproj], big["a_w_in"][j], tm=tm, tn=D, tk=896, out_dtype=MXU, name="rec_in_dx")
            stack2b("a_w_in", j, mm_tn(sv["hn"], [dproj], t1=D, tn=896, ts=ts, col_shards=True, name="rec_in_dw"))
        else:
            dqm, dkvm[l] = mem_attn_bwd(sv["proj"], MIX_W // MEM_W, sv["kvm"], dym, "dil_mem_attn_bwd")
            do_list, dd_list = dil_mix_bwd(dym, sv["o"], sv["lse"], "dil_mix_bwd")
            dq_list, dk_list, dv_list = [], [], []
            for gi in range(3):
                dil = DIL_GROUPS[gi][1]
                view = (s // dil, dil * MEM_W)
                dq, dk, dv = dil_attn_bwd(sv["proj"], kv, sv["lse"][gi].reshape(view), do_list[gi].reshape(view),
                                          dd_list[gi].reshape(view), gi, "dil_attn_bwd%d" % gi)
                dq_list.append(dq.reshape(s, MEM_W))
                dk_list.append(dk.reshape(s, MEM_W))
                dv_list.append(dv.reshape(s, MEM_W))
            dkv_parts.append(dk_list + dv_list)
            dproj = sum_cast([(a,) for a in dq_list + [dqm]], "dil_dproj")
            dhn = mm_nt([dproj], big["b_w_in"][j], tm=tm, tn=D, tk=D, out_dtype=MXU, name="dil_in_dx")
            stack2b("b_w_in", j, mm_tn(sv["hn"], [dproj], t1=D, tn=1024, ts=ts, col_shards=False, name="dil_in_dw"))
        dk_m = dkvm[l].astype(MXU)
        dmem_parts.append(mm_nt([dk_m], big[p + "_w_mem_kv"][j], tm=N_MEM, tn=D, tk=2 * MEM_W, out_dtype=F32,
                                name="mem_kv_dx"))
        stack2b(p + "_w_mem_kv", j, mm_tn(mem_n, [dk_m], t1=D, tn=2 * MEM_W, ts=N_MEM, col_shards=False,
                                         name="mem_kv_dw"))
        if l in (3, 1):
            pn, jn = ("b", 0) if l == 3 else ("a", 0)
            dh, dg, *ahead = norm_bwd(sv["h"], small[p + "_pre_mix_g"][j], dhn, dh1, F32, "pre_post_norm_bwd",
                                      then=(saved[l - 1]["y2"], small[pn + "_post_ffn_g"][jn]))
        else:
            dh, dg = norm_bwd(sv["h"], small[p + "_pre_mix_g"][j], dhn, dh1, F32, "pre_norm_bwd")
        stack2(p + "_pre_mix_g", j, dg)
        dh = after_hook(dh, "bwd_end", l, dh)

    _, gs["mem_norm_g"] = norm_bwd(mem, small["mem_norm_g"], add_n(dmem_parts, "dmem_sum"), None, F32,
                                   "mem_norm_bwd")
    return loss_parts, dh


ANY = pl.BlockSpec(memory_space=pl.ANY)
CHIP_FLIPS = (1, 2, 3)


def _coords():
    return lax.axis_index("x"), lax.axis_index("y"), lax.axis_index("c")


def _flip(x, y, m):
    return x ^ (m >> 1), y ^ (m & 1)


def _remote(src, dst, send_sems, recv_sems, k, device):
    return pltpu.make_async_remote_copy(src_ref=src, dst_ref=dst, send_sem=send_sems.at[k], recv_sem=recv_sems.at[k],
                                        device_id=device, device_id_type=MESH)


def _sum_rows_tile(rows, cols, itemsize=4):
    for tr in (512, 256, 128, 64, 32, 16):
        if rows % tr == 0 and tr * cols * itemsize <= 2 * 1024 * 1024:
            return tr
    raise ValueError((rows, cols))


def half_sum(g, got, name):
    _, r, cols = g.shape
    hr = r // 2
    tr = _sum_rows_tile(hr, cols, g.dtype.itemsize)

    def my_chip():
        return 2 * lax.axis_index("x") + lax.axis_index("y")

    def body(g_ref, got_ref, o_ref, own_ref):
        p = (g_ref[...].astype(F32) + got_ref[...].astype(F32)).astype(o_ref.dtype)
        o_ref[...] = p

        @pl.when(pl.program_id(1) == my_chip())
        def _():
            own_ref[...] = p

    out = SDS((N_CHIPS, hr, cols), jnp.bfloat16)
    return pl.pallas_call(
        body, grid=(hr // tr, N_CHIPS),
        in_specs=[BS((None, None, tr, cols), lambda i, s: (s, lax.axis_index("c"), i, 0)),
                  BS((None, tr, cols), lambda i, s: (s, i, 0))],
        out_specs=[BS((None, tr, cols), lambda i, s: (s, i, 0)),
                   BS((None, tr, cols), lambda i, s: (my_chip(), i, 0))],
        out_shape=[out, out], compiler_params=_cp("parallel", "arbitrary"), name=name,
    )(g.reshape(N_CHIPS, 2, hr, cols), got)


def slot_sum(slots, name):
    _, hr, cols = slots.shape
    tr = _sum_rows_tile(hr, cols)
    nblk = hr // tr

    def body(s_ref, o_ref):
        acc = s_ref[0].astype(F32)
        for p in range(1, N_CHIPS):
            acc = acc + s_ref[p].astype(F32)
        o_ref[...] = acc

    return pl.pallas_call(
        body, grid=(nblk,), in_specs=[BS((N_CHIPS, tr, cols), lambda i: (0, i, 0))],
        out_specs=BS((tr, cols), lambda i: (lax.axis_index("c") * nblk + i, 0)),
        out_shape=SDS((2 * hr, cols), F32), compiler_params=_cp("parallel"), name=name,
    )(slots)


HBM_SPEC = pl.BlockSpec(memory_space=pltpu.HBM)
SEM_SPEC = pl.BlockSpec(memory_space=pltpu.SEMAPHORE)
EFFECT = pltpu.SideEffectType.DATAFLOW_SIDE_EFFECTING


def split_start(name, bufs, plan, n_copies):
    nb = len(bufs)

    def body(*refs):
        send_sems, recv_sems = refs[nb], refs[nb + 1]
        for k, (src, dst, dev) in enumerate(plan(refs[:nb])):
            _remote(src, dst, send_sems, recv_sems, k, dev).start()
        refs[-1][...] = jnp.zeros_like(refs[-1])

    outs = pl.pallas_call(
        body, name=name,
        out_shape=(pltpu.SemaphoreType.DMA((n_copies,)), pltpu.SemaphoreType.DMA((n_copies,)),
                   *[pltpu.HBM(b.shape, b.dtype) for b in bufs], SDS((8, LANES), F32)),
        in_specs=[HBM_SPEC] * nb, out_specs=(SEM_SPEC, SEM_SPEC, *[HBM_SPEC] * nb, VM),
        input_output_aliases={i: 2 + i for i in range(nb)},
        compiler_params=pltpu.CompilerParams(has_side_effects=EFFECT),
    )(*[pltpu.with_memory_space_constraint(b, pltpu.HBM) for b in bufs])
    return outs[0], outs[1], list(outs[2:2 + nb]), outs[-1]


def split_wait(name, send_sems, recv_sems, bufs, after, plan):
    nb = len(bufs)

    def body(*refs):
        send_ref, recv_ref = refs[nb], refs[nb + 1]
        for k, (src, dst, dev) in enumerate(plan(refs[:nb])):
            cp = _remote(src, dst, send_ref, recv_ref, k, dev)
            cp.wait_send()
            cp.wait_recv()

    outs = pl.pallas_call(
        body, name=name, out_shape=[pltpu.HBM(b.shape, b.dtype) for b in bufs],
        in_specs=[HBM_SPEC] * nb + [SEM_SPEC, SEM_SPEC, ANY], out_specs=[HBM_SPEC] * nb,
        input_output_aliases={i: i for i in range(nb)},
        compiler_params=pltpu.CompilerParams(has_side_effects=EFFECT),
    )(*bufs, send_sems, recv_sems, after)
    return list(outs)


def tie(x, tokens, name):
    def body(*refs):
        pass

    return pl.pallas_call(
        body, name=name, out_shape=SDS(x.shape, x.dtype), in_specs=[ANY] * (1 + len(tokens)), out_specs=ANY,
        input_output_aliases={0: 0},
    )(x, *tokens)


def plan_gather_ici(n, rows):
    def plan(refs):
        x, y, c = _coords()
        me = 2 * x + y
        out = []
        for i in range(n):
            hr = rows[i] // 2
            mine = pl.ds(pl.multiple_of(c * hr, 8), hr)
            out.append((refs[i], refs[n + i].at[me], (x, y, 1 - c)))
            for m in CHIP_FLIPS:
                out.append((refs[i].at[mine], refs[n + i].at[me, mine], (*_flip(x, y, m), c)))
        return out
    return plan


def plan_gather_d2d(n, rows):
    def plan(refs):
        x, y, c = _coords()
        me = 2 * x + y
        out = []
        for i in range(n):
            hr = rows[i] // 2
            mine = pl.ds(pl.multiple_of(c * hr, 8), hr)
            for m in CHIP_FLIPS:
                slot = refs[i].at[me ^ m, mine]
                out.append((slot, slot, (x, y, 1 - c)))
        return out
    return plan


def plan_swap(n, rows):
    def plan(refs):
        x, y, c = _coords()
        out = []
        for i in range(n):
            hr = rows[i] // 2
            other = pl.ds(pl.multiple_of((1 - c) * hr, 8), hr)
            out.append((refs[i].at[pl.ds(0, N_CHIPS), other], refs[n + i], (x, y, 1 - c)))
        return out
    return plan


def plan_exchange(n):
    def plan(refs):
        x, y, c = _coords()
        me = 2 * x + y
        out = []
        for i in range(n):
            for m in CHIP_FLIPS:
                out.append((refs[i].at[me ^ m], refs[n + i].at[me], (*_flip(x, y, m), c)))
        return out
    return plan


def plan_share(n, rows):
    def plan(refs):
        x, y, c = _coords()
        out = []
        for i in range(n):
            hr = rows[i] // 2
            mine = refs[i].at[pl.ds(pl.multiple_of(c * hr, 8), hr)]
            out.append((mine, mine, (x, y, 1 - c)))
        return out
    return plan


VM = pl.BlockSpec(memory_space=pltpu.VMEM)


def small_gather(v, name):
    def body(v_ref, out_ref, send_sems, recv_sems):
        x, y, c = _coords()
        me = 2 * x + y
        out_ref[me] = v_ref[...]
        cps = []
        for j, m in enumerate(CHIP_FLIPS):
            cp = _remote(v_ref, out_ref.at[me], send_sems, recv_sems, j, (*_flip(x, y, m), c))
            cp.start()
            cps.append(cp)
        for cp in cps:
            cp.wait()

    return pl.pallas_call(
        body, in_specs=[VM], out_specs=VM, out_shape=SDS((N_CHIPS,) + v.shape, v.dtype),
        scratch_shapes=[pltpu.SemaphoreType.DMA((3,)), pltpu.SemaphoreType.DMA((3,))],
        compiler_params=pltpu.CompilerParams(vmem_limit_bytes=VMEM_LIMIT_BYTES), name=name,
    )(v)


def plan_small_swap(refs):
    x, y, c = _coords()
    return [(refs[0], refs[1], (x, y, 1 - c))]


def plan_small_exchange(refs):
    x, y, c = _coords()
    me = 2 * x + y
    return [(refs[0].at[me], refs[0].at[me], (*_flip(x, y, m), c)) for m in CHIP_FLIPS]


def small_pair(v, sib, name):
    rows, cols = v.shape
    tr = _sum_rows_tile(rows, cols)

    def body(v_ref, s_ref, o_ref):
        o_ref[...] = v_ref[...] + s_ref[...]

    blk = BS((tr, cols), lambda i: (i, 0))
    return pl.pallas_call(
        body, grid=(rows // tr,), in_specs=[blk, blk],
        out_specs=BS((None, tr, cols), lambda i: (2 * lax.axis_index("x") + lax.axis_index("y"), i, 0)),
        out_shape=SDS((N_CHIPS, rows, cols), F32), compiler_params=_cp("parallel"), name=name,
    )(v, sib)


def small_total(slots, name):
    _, rows, cols = slots.shape
    tr = _sum_rows_tile(rows, cols)

    def body(s_ref, o_ref):
        o_ref[...] = (s_ref[0] + s_ref[1]) + (s_ref[2] + s_ref[3])

    return pl.pallas_call(
        body, grid=(rows // tr,), in_specs=[BS((N_CHIPS, tr, cols), lambda i: (0, i, 0))],
        out_specs=BS((tr, cols), lambda i: (i, 0)), out_shape=SDS((rows, cols), F32),
        compiler_params=_cp("parallel"), name=name,
    )(slots)


def adamw(w, g_list, m, v, name):
    nl, rows, cols = w.shape
    tr = _sum_rows_tile(rows, cols) if rows % 16 == 0 else rows
    bc1 = 1.0 - ADAM_B1 ** ADAM_STEP
    bc2 = 1.0 - ADAM_B2 ** ADAM_STEP

    def body(*refs):
        w_ref, m_ref, v_ref = refs[:3]
        g_refs = refs[3:3 + nl]
        go_ref, d_ref, mo_ref, vo_ref = refs[3 + nl:]
        layer = pl.program_id(0)
        for l in range(nl):
            @pl.when(layer == l)
            def _(l=l):
                g = g_refs[l][...]
                m_new = ADAM_B1 * m_ref[...] + (1.0 - ADAM_B1) * g
                v_new = ADAM_B2 * v_ref[...] + (1.0 - ADAM_B2) * (g * g)
                m_hat = m_new / bc1
                v_hat = v_new / bc2
                go_ref[...] = g
                d_ref[...] = -ADAM_LR * (m_hat / (jnp.sqrt(v_hat) + ADAM_EPS) + ADAM_WD * w_ref[...])
                mo_ref[...] = m_new
                vo_ref[...] = v_new

    stk = BS((None, tr, cols), lambda l, i: (l, i, 0))
    flat = BS((tr, cols), lambda l, i: (i, 0))
    out = SDS((nl, rows, cols), F32)
    return pl.pallas_call(
        body, grid=(nl, rows // tr), in_specs=[stk] * 3 + [flat] * nl, out_specs=[stk] * 4,
        out_shape=[out] * 4, compiler_params=_cp("parallel", "parallel"), name=name,
    )(w, m, v, *g_list)


WEIGHTS = ["mem_norm_g", "a_pre_mix_g", "a_post_mix_g", "a_pre_ffn_g", "a_post_ffn_g", "a_w_in", "a_conv_w",
           "a_conv_b", "a_gate_a_w", "a_gate_a_b", "a_gate_x_w", "a_gate_x_b", "a_lambda", "a_w_mem_kv", "a_w_out",
           "a_w_ffn_in", "a_w_ffn_out", "kv_norm_g", "w_kv_shared", "b_pre_mix_g", "b_post_mix_g", "b_pre_ffn_g",
           "b_post_ffn_g", "b_w_in", "b_w_mem_kv", "b_w_out", "b_w_ffn_in", "b_w_ffn_out"]
BIG = {"a_w_in": True, "a_w_mem_kv": False, "a_w_out": False, "a_w_ffn_in": True, "a_w_ffn_out": False,
       "w_kv_shared": True, "b_w_in": False, "b_w_mem_kv": False, "b_w_out": False, "b_w_ffn_in": True,
       "b_w_ffn_out": False}
SHARDED_SMALL = ["a_pre_mix_g", "a_post_mix_g", "a_pre_ffn_g", "a_post_ffn_g", "a_conv_w", "a_conv_b", "a_gate_a_b",
                 "a_gate_x_b", "a_lambda"]
REPL_SMALL = ["mem_norm_g", "kv_norm_g", "b_pre_mix_g", "b_post_mix_g", "b_pre_ffn_g", "b_post_ffn_g", "a_gate_a_w",
              "a_gate_x_w"]
LANES = 128


def _pack(arrs, row_multiple=8):
    flat = jnp.concatenate([a.reshape(-1) for a in arrs])
    pad = -flat.shape[0] % (LANES * row_multiple)
    if pad:
        flat = jnp.concatenate([flat, jnp.zeros((pad,), flat.dtype)])
    return flat.reshape(-1, LANES)


def _unpack(packed, shapes):
    flat = packed.reshape(-1)
    out, pos = [], 0
    for sh in shapes:
        size = math.prod(sh)
        out.append(flat[pos:pos + size].reshape(sh))
        pos += size
    return out


def kernel(x, mem, mem_norm_g, a_pre_mix_g, a_post_mix_g, a_pre_ffn_g, a_post_ffn_g, a_w_in, a_conv_w, a_conv_b,
           a_gate_a_w, a_gate_a_b, a_gate_x_w, a_gate_x_b, a_lambda, a_w_mem_kv, a_w_out, a_w_ffn_in, a_w_ffn_out,
           kv_norm_g, w_kv_shared, b_pre_mix_g, b_post_mix_g, b_pre_ffn_g, b_post_ffn_g, b_w_in, b_w_mem_kv, b_w_out,
           b_w_ffn_in, b_w_ffn_out, loss_target, m_mem_norm_g, m_a_pre_mix_g, m_a_post_mix_g, m_a_pre_ffn_g,
           m_a_post_ffn_g, m_a_w_in, m_a_conv_w, m_a_conv_b, m_a_gate_a_w, m_a_gate_a_b, m_a_gate_x_w, m_a_gate_x_b,
           m_a_lambda, m_a_w_mem_kv, m_a_w_out, m_a_w_ffn_in, m_a_w_ffn_out, m_kv_norm_g, m_w_kv_shared, m_b_pre_mix_g,
           m_b_post_mix_g, m_b_pre_ffn_g, m_b_post_ffn_g, m_b_w_in, m_b_w_mem_kv, m_b_w_out, m_b_w_ffn_in, m_b_w_ffn_out,
           v_mem_norm_g, v_a_pre_mix_g, v_a_post_mix_g, v_a_pre_ffn_g, v_a_post_ffn_g, v_a_w_in, v_a_conv_w, v_a_conv_b,
           v_a_gate_a_w, v_a_gate_a_b, v_a_gate_x_w, v_a_gate_x_b, v_a_lambda, v_a_w_mem_kv, v_a_w_out, v_a_w_ffn_in,
           v_a_w_ffn_out, v_kv_norm_g, v_w_kv_shared, v_b_pre_mix_g, v_b_post_mix_g, v_b_pre_ffn_g, v_b_post_ffn_g,
           v_b_w_in, v_b_w_mem_kv, v_b_w_out, v_b_w_ffn_in, v_b_w_ffn_out):
    a = dict(locals())
    xi, yi, ci = _coords()
    chip = 2 * xi + yi

    got = small_gather(_pack([a[n] for n in SHARDED_SMALL]), "small_gather")
    per_chip = [_unpack(got[s], [a[n].shape for n in SHARDED_SMALL]) for s in range(N_CHIPS)]
    small = {n: jnp.concatenate([per_chip[s][k] for s in range(N_CHIPS)], axis=-1)
             for k, n in enumerate(SHARDED_SMALL)}
    small.update({n: a[n] for n in REPL_SMALL})

    groups = []
    for l in range(4):
        p, j = ("a", l) if l < 2 else ("b", l - 2)
        groups.append([(p + "_" + n, j) for n in ("w_in", "w_mem_kv", "w_out")])
        groups.append([(p + "_" + n, j) for n in ("w_ffn_in", "w_ffn_out")])
    groups[3].append(("w_kv_shared", None))
    big = {n: [None, None] for n in BIG if n != "w_kv_shared"}
    gs, gb = {}, {}
    reduced = {n: [None, None] for n in BIG if n != "w_kv_shared"}

    def put(store, n, j, val):
        if j is None:
            store[n] = val
        else:
            store[n][j] = val

    class Exchange:
        def __init__(self):
            self.state = {}

        def gather_ici(self, g):
            shards = [(a[n] if j is None else a[n][j]).astype(MXU) for n, j in groups[g]]
            rows = [sh.shape[0] for sh in shards]
            lands = [lax.empty((N_CHIPS,) + sh.shape, sh.dtype) for sh in shards]
            plan = plan_gather_ici(len(shards), rows)
            ss, rs, bufs, tok = split_start("gather_ici_%d" % g, shards + lands, plan, 4 * len(shards))
            self.state["g", g] = (ss, rs, bufs, plan, rows)
            return tok

        def gather_d2d(self, g, after):
            ss, rs, bufs, plan, rows = self.state.pop(("g", g))
            n = len(rows)
            outs = split_wait("gather_ici_wait_%d" % g, ss, rs, bufs, after, plan)[n:]
            plan = plan_gather_d2d(n, rows)
            ss, rs, bufs, tok = split_start("gather_d2d_%d" % g, outs, plan, 3 * n)
            self.state["g", g] = (ss, rs, bufs, plan)
            return tok

        def gather_done(self, g, after):
            ss, rs, bufs, plan = self.state.pop(("g", g))
            outs = split_wait("gather_d2d_wait_%d" % g, ss, rs, bufs, after, plan)
            for (n, j), w in zip(groups[g], outs):
                put(big, n, j, w if BIG[n] else w.reshape(-1, w.shape[-1]))

        def rs_swap(self, g):
            grads = []
            for n, j in groups[g]:
                gr = gb[n] if j is None else gb[n][j]
                grads.append(gr if BIG[n] else gr.reshape(N_CHIPS, gr.shape[0] // N_CHIPS, gr.shape[1]))
            rows = [gr.shape[1] for gr in grads]
            lands = [lax.empty((N_CHIPS, gr.shape[1] // 2, gr.shape[2]), gr.dtype) for gr in grads]
            plan = plan_swap(len(grads), rows)
            ss, rs, bufs, tok = split_start("rs_swap_%d" % g, grads + lands, plan, len(grads))
            self.state["r", g] = (ss, rs, bufs, plan, rows)
            return tok

        def rs_exchange(self, g, after):
            ss, rs, bufs, plan, rows = self.state.pop(("r", g))
            n = len(rows)
            bufs = split_wait("rs_swap_wait_%d" % g, ss, rs, bufs, after, plan)
            sums = [half_sum(gr, got, "rs_half_sum") for gr, got in zip(bufs[:n], bufs[n:])]
            plan = plan_exchange(n)
            ss, rs, bufs, tok = split_start("rs_exchange_%d" % g, [p for p, _ in sums] + [s for _, s in sums], plan,
                                            3 * n)
            self.state["r", g] = (ss, rs, bufs, plan, rows)
            return tok

        def rs_share(self, g, after):
            ss, rs, bufs, plan, rows = self.state.pop(("r", g))
            n = len(rows)
            slots = split_wait("rs_exchange_wait_%d" % g, ss, rs, bufs, after, plan)[n:]
            fulls = [slot_sum(s, "rs_slot_sum") for s in slots]
            plan = plan_share(n, rows)
            ss, rs, bufs, tok = split_start("rs_share_%d" % g, fulls, plan, n)
            self.state["r", g] = (ss, rs, bufs, plan)
            return tok

        def rs_done(self, g, after):
            ss, rs, bufs, plan = self.state.pop(("r", g))
            outs = split_wait("rs_share_wait_%d" % g, ss, rs, bufs, after, plan)
            for (n, j), r in zip(groups[g], outs):
                put(reduced, n, j, r)

        def hook(self, where, l, after):
            mix, ffn = 2 * l, 2 * l + 1
            toks = []
            if where == "fwd_begin":
                if l == 0:
                    tok = self.gather_ici(mix)
                    tok = self.gather_d2d(mix, tok)
                    self.gather_done(mix, tok)
                toks.append(self.gather_ici(ffn))
            elif where == "fwd_q1":
                toks.append(self.gather_d2d(ffn, after))
            elif where == "fwd_mid":
                self.gather_done(ffn, after)
                if l < 3:
                    toks.append(self.gather_ici(mix + 2))
            elif where == "fwd_q3":
                if l < 3:
                    toks.append(self.gather_d2d(mix + 2, after))
            elif where == "fwd_end":
                if l < 3:
                    self.gather_done(mix + 2, after)
            elif where == "bwd_begin":
                if l < 3:
                    self.rs_done(ffn + 2, after)
                    toks.append(self.rs_exchange(mix + 2, after))
            elif where == "bwd_mid1":
                if l < 3:
                    toks.append(self.rs_share(mix + 2, after))
            elif where == "bwd_mid2":
                if l < 3:
                    self.rs_done(mix + 2, after)
                toks.append(self.rs_swap(ffn))
            elif where == "bwd_m1":
                toks.append(self.rs_exchange(ffn, after))
            elif where == "bwd_end":
                toks.append(self.rs_share(ffn, after))
                toks.append(self.rs_swap(mix))
                if l == 0:
                    self.rs_done(ffn, toks[0])
                    tok = self.rs_exchange(mix, toks[1])
                    tok = self.rs_share(mix, adamw_big([n for n in BIG if n.startswith("b_")], tok))
                    self.rs_done(mix, tok)
                    toks = []
            else:
                raise ValueError(where)
            return toks

    res = {}

    def adamw_big(names, token=None):
        last = None
        for n in names:
            shape = a[n].shape
            rows, cols = shape[-2], shape[-1]
            stk = (-1, rows, cols)
            grads = reduced[n] if isinstance(reduced[n], list) else [reduced[n]]
            if token is not None:
                grads = [tie(grads[0], [token], "tie_adamw_" + n)] + grads[1:]
            outs = adamw(a[n].reshape(stk), grads, a["m_" + n].reshape(stk), a["v_" + n].reshape(stk), "adamw")
            res[n] = [o.reshape(shape) for o in outs]
            last = outs[1]
            token = last if token is not None else None
        return last

    loss_parts, dx = _fwd_bwd(x[0], mem[0], loss_target[0], small, big, gs, gb, Exchange())
    loss = lax.psum(jnp.sum(loss_parts) * (0.5 / D), ("x", "y", "c"))

    def full(n):
        g = gs[n]
        return jnp.stack(g) if isinstance(g, list) else g

    order = SHARDED_SMALL + REPL_SMALL
    full_shapes = [full(n).shape for n in order]
    pack = _pack([full(n) for n in order], 512)
    ss, rs, bufs, tok = split_start("small_swap", [pack, lax.empty(pack.shape, F32)], plan_small_swap, 1)
    mine_v, sib_v = split_wait("small_swap_wait", ss, rs, bufs, tok, plan_small_swap)
    ss, rs, bufs, tok = split_start("small_exchange", [small_pair(mine_v, sib_v, "small_pair")],
                                    plan_small_exchange, 3)
    last = adamw_big([n for n in BIG if n not in res], tok)
    slots = split_wait("small_exchange_wait", ss, rs, bufs, last, plan_small_exchange)[0]
    summed = _unpack(small_total(slots, "small_total"), full_shapes)
    mine = []
    for n, g in zip(order, summed):
        if n in SHARDED_SMALL:
            width = a[n].shape[-1]
            g = lax.dynamic_slice_in_dim(g, chip * width, width, axis=g.ndim - 1)
        mine.append(g.reshape(a[n].shape))
    shapes = [a[n].shape for n in order]
    rm = 512
    outs = adamw(_pack([a[n] for n in order], rm)[None], [_pack(mine, rm)],
                 _pack([a["m_" + n] for n in order], rm)[None], _pack([a["v_" + n] for n in order], rm)[None],
                 "adamw_small")
    unpacked = [_unpack(o[0], shapes) for o in outs]
    for k, n in enumerate(order):
        res[n] = [u[k] for u in unpacked]

    return (loss, dx[None], *[res[n][0] for n in WEIGHTS], *[res[n][1] for n in WEIGHTS],
            *[res[n][2] for n in WEIGHTS], *[res[n][3] for n in WEIGHTS])
```

```python
import math

import jax
import jax.numpy as jnp
from jax import lax
from jax.experimental import pallas as pl
from jax.experimental.pallas import tpu as pltpu

D = 2048
HD = 128
MEM_W = 512
MEM_HEADS = 4
MIX_W = D - MEM_W
N_BLK = MIX_W // HD
D_FF = 5632
N_MEM = 256
RMS_EPS = 1e-6
NEG_INF = -1e30
LRU_C = 8.0
DIL_GROUPS = ((128, 1), (512, 4), (2048, 16))
Q_BLOCK = 128
SCALE = HD ** -0.5
N_CHIPS = 4
MXU_COLS = 256
ACC_CHUNK = 2 * MXU_COLS

ADAM_LR = 0.001
ADAM_B1 = 0.9
ADAM_B2 = 0.999
ADAM_EPS = 1e-08
ADAM_WD = 0.01
ADAM_STEP = 10

MXU = jnp.bfloat16
F32 = jnp.float32
VMEM_LIMIT_BYTES = 56 * 1024 * 1024

BS = pl.BlockSpec
SDS = jax.ShapeDtypeStruct
MESH = pl.DeviceIdType.MESH


def _cp(*sem):
    return pltpu.CompilerParams(dimension_semantics=sem or None, vmem_limit_bytes=VMEM_LIMIT_BYTES)


def _dot(a, b, dn=((1,), (0,))):
    return lax.dot_general(a, b, (dn, ((), ())), preferred_element_type=F32)


def _div(i, n):
    return lax.div(i, jnp.int32(n))


def _rem(i, n):
    return lax.rem(i, jnp.int32(n))


NN = ((1,), (0,))
NT = ((1,), (1,))
TN = ((0,), (0,))


def _sigmoid(z):
    return 0.5 * jnp.tanh(0.5 * z) + 0.5


def _log1p_pos(u):
    return jnp.where(u < 1e-2, u * (1.0 - u * (0.5 - u * (1.0 / 3.0))), jnp.log(1.0 + u))


def _neg_expm1(z):
    return jnp.where(z > -1e-2, -z * (1.0 + z * (0.5 + z * (1.0 / 6.0))), 1.0 - jnp.exp(z))


def _softplus(z):
    return jnp.maximum(z, 0.0) + _log1p_pos(jnp.exp(-jnp.abs(z)))


_GELU_C = math.sqrt(2.0 / math.pi)


def _gelu_and_grad(x):
    x2 = x * x
    t = jnp.tanh(_GELU_C * (x + 0.044715 * x * x2))
    g = 0.5 * x * (1.0 + t)
    dg = 0.5 * (1.0 + t) + 0.5 * x * (1.0 - t * t) * _GELU_C * (1.0 + 3.0 * 0.044715 * x2)
    return g, dg


def _row_tile(rows):
    return min(512, rows)


def norm_cast(x, g, name):
    rows = x.shape[0]
    tr = _row_tile(rows)

    def body(x_ref, g_ref, o_ref):
        xv = x_ref[...]
        r = lax.rsqrt(jnp.mean(xv * xv, axis=-1, keepdims=True) + RMS_EPS)
        o_ref[...] = (xv * r * g_ref[...]).astype(o_ref.dtype)

    return pl.pallas_call(
        body, grid=(rows // tr,),
        in_specs=[BS((tr, D), lambda i: (i, 0)), BS((1, D), lambda i: (0, 0))],
        out_specs=BS((tr, D), lambda i: (i, 0)),
        out_shape=SDS((rows, D), MXU), compiler_params=_cp("parallel"), name=name,
    )(x, g.reshape(1, D))


def resid_norm(h, y, g, name):
    rows = h.shape[0]
    tr = _row_tile(rows)

    def body(h_ref, y_ref, g_ref, o_ref):
        yv = y_ref[...]
        r = lax.rsqrt(jnp.mean(yv * yv, axis=-1, keepdims=True) + RMS_EPS)
        o_ref[...] = h_ref[...] + yv * r * g_ref[...]

    return pl.pallas_call(
        body, grid=(rows // tr,),
        in_specs=[BS((tr, D), lambda i: (i, 0)), BS((tr, D), lambda i: (i, 0)), BS((1, D), lambda i: (0, 0))],
        out_specs=BS((tr, D), lambda i: (i, 0)),
        out_shape=SDS((rows, D), F32), compiler_params=_cp("parallel"), name=name,
    )(h, y, g.reshape(1, D))


def resid_norm_next(h, y, g, g_next, name):
    rows = h.shape[0]
    tr = _row_tile(rows)

    def body(h_ref, y_ref, g_ref, gn_ref, o_ref, n_ref):
        yv = y_ref[...]
        r = lax.rsqrt(jnp.mean(yv * yv, axis=-1, keepdims=True) + RMS_EPS)
        hv = h_ref[...] + yv * r * g_ref[...]
        o_ref[...] = hv
        r2 = lax.rsqrt(jnp.mean(hv * hv, axis=-1, keepdims=True) + RMS_EPS)
        n_ref[...] = (hv * r2 * gn_ref[...]).astype(n_ref.dtype)

    row = BS((tr, D), lambda i: (i, 0))
    vec = BS((1, D), lambda i: (0, 0))
    return pl.pallas_call(
        body, grid=(rows // tr,), in_specs=[row, row, vec, vec], out_specs=[row, row],
        out_shape=[SDS((rows, D), F32), SDS((rows, D), MXU)], compiler_params=_cp("parallel"), name=name,
    )(h, y, g.reshape(1, D), g_next.reshape(1, D))


def _norm_bwd_rows(xv, gv, dyv):
    r = lax.rsqrt(jnp.mean(xv * xv, axis=-1, keepdims=True) + RMS_EPS)
    xhat = xv * r
    dxhat = dyv * gv
    dx = r * (dxhat - xhat * jnp.mean(dxhat * xhat, axis=-1, keepdims=True))
    return dx, jnp.sum(dyv * xhat, axis=0, keepdims=True)


def norm_bwd(x, g, dy, res, out_dtype, name, then=None):
    rows = x.shape[0]
    tr = _row_tile(rows)
    has_res = res is not None
    n_in = 3 + has_res + (2 if then else 0)

    def body(*refs):
        x_ref, g_ref, dy_ref = refs[:3]
        dx_ref, dg_ref = refs[n_in], refs[n_in + 1]
        dx, dg = _norm_bwd_rows(x_ref[...], g_ref[...], dy_ref[...].astype(F32))
        if has_res:
            dx = dx + refs[3][...]
        dx_ref[...] = dx.astype(dx_ref.dtype)
        first = pl.program_id(0) == 0

        @pl.when(first)
        def _():
            dg_ref[...] = jnp.zeros_like(dg_ref)

        dg_ref[...] += dg
        if then:
            x2_ref, g2_ref = refs[n_in - 2], refs[n_in - 1]
            dx2_ref, dg2_ref = refs[n_in + 2], refs[n_in + 3]
            dx2, dg2 = _norm_bwd_rows(x2_ref[...], g2_ref[...], dx)
            dx2_ref[...] = dx2.astype(dx2_ref.dtype)

            @pl.when(first)
            def _():
                dg2_ref[...] = jnp.zeros_like(dg2_ref)

            dg2_ref[...] += dg2

    row = BS((tr, D), lambda i: (i, 0))
    vec = BS((1, D), lambda i: (0, 0))
    ins = [x, g.reshape(1, D), dy] + ([res] if has_res else []) + ([then[0], then[1].reshape(1, D)] if then else [])
    outs = pl.pallas_call(
        body, grid=(rows // tr,),
        in_specs=[row, vec, row] + ([row] if has_res else []) + ([row, vec] if then else []),
        out_specs=[row, vec] + ([row, vec] if then else []),
        out_shape=[SDS((rows, D), out_dtype), SDS((1, D), F32)] + ([SDS((rows, D), MXU), SDS((1, D), F32)] if then else []),
        compiler_params=_cp("arbitrary"), name=name,
    )(*ins)
    if then:
        return outs[0], outs[1].reshape(D), outs[2], outs[3].reshape(D)
    return outs[0], outs[1].reshape(D)


def loss_head(y, target, name):
    rows = y.shape[0]
    tr = _row_tile(rows)

    def body(y_ref, t_ref, dy_ref, acc_ref):
        err = y_ref[...] - t_ref[...]
        dy_ref[...] = err * (1.0 / D)

        @pl.when(pl.program_id(0) == 0)
        def _():
            acc_ref[...] = jnp.zeros_like(acc_ref)

        acc_ref[...] += jnp.sum(err * err, axis=0, keepdims=True)

    row = BS((tr, D), lambda i: (i, 0))
    dy, acc = pl.pallas_call(
        body, grid=(rows // tr,), in_specs=[row, row],
        out_specs=[row, BS((1, D), lambda i: (0, 0))],
        out_shape=[SDS((rows, D), F32), SDS((1, D), F32)],
        compiler_params=_cp("arbitrary"), name=name,
    )(y, target)
    return acc, dy


def _mm_call(ins, in_specs, pick, dn, grid, o_spec, out_sds, name):
    gk = grid[2]
    n_in = len(ins)

    def body(*refs):
        o_ref = refs[n_in]
        k = pl.program_id(2)

        def step(a_ref, b_ref):
            acc = o_ref if (out_sds.dtype == F32 or gk == 1) else refs[n_in + 1]
            width = acc.shape[-1]
            if dn == TN or width <= ACC_CHUNK:
                chunks = [(0, width)]
            else:
                chunks = [(c0, min(c0 + ACC_CHUNK, width)) for c0 in range(0, width, ACC_CHUNK)]

            def sweep(first):
                a = a_ref[...]
                pending = None
                for c0, c1 in chunks:
                    p = _dot(a, b_ref[c0:c1, :] if dn == NT else b_ref[:, c0:c1], dn)
                    if pending is not None:
                        put(first, *pending)
                    pending = (c0, c1, p)
                put(first, *pending)

            def put(first, c0, c1, p):
                if first:
                    acc[:, c0:c1] = p.astype(acc.dtype)
                else:
                    acc[:, c0:c1] += p

            if gk == 1:
                sweep(True)
                return

            @pl.when(k == 0)
            def _():
                sweep(True)

            @pl.when(k > 0)
            def _():
                sweep(False)

            if acc is not o_ref:
                @pl.when(k == gk - 1)
                def _():
                    o_ref[...] = acc[...].astype(o_ref.dtype)

        pick(refs[:n_in], k, step)

    scratch = []
    if gk > 1 and out_sds.dtype != F32:
        scratch = [pltpu.VMEM(o_spec.block_shape[-2:], F32)]
    return pl.pallas_call(
        body, grid=grid, in_specs=in_specs, out_specs=o_spec, out_shape=out_sds,
        scratch_shapes=scratch, compiler_params=_cp("parallel", "parallel", "arbitrary"), name=name,
    )(*ins)


def _pick2(refs, k, step):
    step(refs[0], refs[1])


def mm_nn(a, w, *, tm, tn, tk, out_dtype, name):
    m, kdim = a.shape
    if w.ndim == 3:
        c = w.shape[2]
        n = N_CHIPS * c
        per = c // tn
        b_spec = BS((None, tk, tn), lambda i, j, k: (_div(j, per), k, _rem(j, per)))
    else:
        n = w.shape[1]
        b_spec = BS((tk, tn), lambda i, j, k: (k, j))
    grid = (m // tm, n // tn, kdim // tk)
    return _mm_call([a, w], [BS((tm, tk), lambda i, j, k: (i, k)), b_spec], _pick2, NN, grid,
                    BS((tm, tn), lambda i, j, k: (i, j)), SDS((m, n), out_dtype), name)


def mm_nt(a_list, w, *, tm, tn, tk, out_dtype, name):
    m = a_list[0].shape[0]
    ka = a_list[0].shape[1]
    n_a = len(a_list)
    kdim = ka * n_a
    if w.ndim == 3:
        c = w.shape[2]
        n = w.shape[1]
        per = c // tk
        b_spec = BS((None, tn, tk), lambda i, j, k: (_div(k, per), j, _rem(k, per)))
    else:
        n = w.shape[0]
        b_spec = BS((tn, tk), lambda i, j, k: (j, k))
    gk = kdim // tk
    half = gk // n_a
    grid = (m // tm, n // tn, gk)
    if n_a == 1:
        a_specs = [BS((tm, tk), lambda i, j, k: (i, k))]
        pick = lambda refs, k, step: step(refs[0], refs[1])
    else:
        a_specs = [BS((tm, tk), lambda i, j, k: (i, jnp.minimum(k, half - 1))),
                   BS((tm, tk), lambda i, j, k: (i, jnp.maximum(k - half, 0)))]

        def pick(refs, k, step):
            @pl.when(k < half)
            def _():
                step(refs[0], refs[2])

            @pl.when(k >= half)
            def _():
                step(refs[1], refs[2])

    return _mm_call(list(a_list) + [w], a_specs + [b_spec], pick, NT, grid,
                    BS((tm, tn), lambda i, j, k: (i, j)), SDS((m, n), out_dtype), name)


def mm_tn(a, b_list, *, t1, tn, ts, col_shards, name):
    s, k1 = a.shape
    nb = b_list[0].shape[1]
    n_b = len(b_list)
    n = nb * n_b
    gn = n // tn
    half = gn // n_b
    grid = (k1 // t1, gn, s // ts)
    if col_shards:
        c = n // N_CHIPS
        per = c // tn
        o_spec = BS((None, t1, tn), lambda i, j, k: (_div(j, per), i, _rem(j, per)))
        out_sds = SDS((N_CHIPS, k1, c), MXU)
    else:
        o_spec = BS((t1, tn), lambda i, j, k: (i, j))
        out_sds = SDS((k1, n), MXU)
    a_spec = BS((ts, t1), lambda i, j, k: (k, i))
    if n_b == 1:
        b_specs = [BS((ts, tn), lambda i, j, k: (k, j))]
        pick = lambda refs, k, step: step(refs[0], refs[1])
    else:
        b_specs = [BS((ts, tn), lambda i, j, k: (jnp.where(j < half, k, 0), jnp.minimum(j, half - 1))),
                   BS((ts, tn), lambda i, j, k: (jnp.where(j >= half, k, 0), jnp.maximum(j - half, 0)))]

        def pick(refs, k, step):
            j = pl.program_id(1)

            @pl.when(j < half)
            def _():
                step(refs[0], refs[1])

            @pl.when(j >= half)
            def _():
                step(refs[0], refs[2])

    return _mm_call([a] + list(b_list), [a_spec] + b_specs, pick, TN, grid, o_spec, out_sds, name)


def ffn_in_fwd(hn, w, name):
    s = hn.shape[0]
    tm = min(512, s)
    tn = D_FF // 4

    def tail(dag_ref, dau_ref, act_ref, c0, c1, g, u):
        sg = _sigmoid(g)
        silu = g * sg
        dag_ref[:, c0:c1] = (u * sg * (1.0 + g * (1.0 - sg))).astype(dag_ref.dtype)
        dau_ref[:, c0:c1] = silu.astype(dau_ref.dtype)
        act_ref[:, c0:c1] = (silu * u).astype(act_ref.dtype)

    def body(a_ref, wg_ref, wu_ref, dag_ref, dau_ref, act_ref):
        a = a_ref[...]
        pending = None
        for c0 in range(0, tn, ACC_CHUNK):
            c1 = min(c0 + ACC_CHUNK, tn)
            g = _dot(a, wg_ref[:, c0:c1])
            u = _dot(a, wu_ref[:, c0:c1])
            if pending is not None:
                tail(dag_ref, dau_ref, act_ref, *pending)
            pending = (c0, c1, g, u)
        tail(dag_ref, dau_ref, act_ref, *pending)

    tile = BS((tm, tn), lambda j, i: (i, j))
    return pl.pallas_call(
        body, grid=(4, s // tm),
        in_specs=[BS((tm, D), lambda j, i: (i, 0)),
                  BS((None, D, tn), lambda j, i: (_div(j, 2), 0, _rem(j, 2))),
                  BS((None, D, tn), lambda j, i: (2 + _div(j, 2), 0, _rem(j, 2)))],
        out_specs=[tile, tile, tile],
        out_shape=[SDS((s, D_FF), MXU), SDS((s, D_FF), MXU), SDS((s, D_FF), MXU)],
        compiler_params=_cp("parallel", "parallel"), name=name,
    )(hn, w, w)


def ffn_act_bwd(dy, w_out, dag, dau, name):
    s = dy.shape[0]
    tm = min(1024, s)
    tn = D_FF // 4

    def body(dy_ref, w_ref, dag_ref, dau_ref, dg_ref, du_ref):
        def tail(c0, c1, dact):
            dg_ref[:, c0:c1] = (dact * dag_ref[:, c0:c1].astype(F32)).astype(dg_ref.dtype)
            du_ref[:, c0:c1] = (dact * dau_ref[:, c0:c1].astype(F32)).astype(du_ref.dtype)

        dy = dy_ref[...]
        pending = None
        for c0 in range(0, tn, ACC_CHUNK):
            c1 = min(c0 + ACC_CHUNK, tn)
            dact = _dot(dy, w_ref[c0:c1, :], NT)
            if pending is not None:
                tail(*pending)
            pending = (c0, c1, dact)
        tail(*pending)

    tile = BS((tm, tn), lambda j, i: (i, j))
    return pl.pallas_call(
        body, grid=(4, s // tm),
        in_specs=[BS((tm, D), lambda j, i: (i, 0)), BS((tn, D), lambda j, i: (j, 0)), tile, tile],
        out_specs=[tile, tile],
        out_shape=[SDS((s, D_FF), MXU), SDS((s, D_FF), MXU)],
        compiler_params=_cp("parallel", "parallel"), name=name,
    )(dy, w_out, dag, dau)


LRU_T = 256
HALO = 8


def _shift_down(x, k, fill):
    rows = x.shape[0]
    idx = lax.broadcasted_iota(jnp.int32, x.shape, 0)
    return jnp.where(idx < k, fill, pltpu.roll(x, k, 0))


def _shift_up(x, k, fill):
    rows = x.shape[0]
    idx = lax.broadcasted_iota(jnp.int32, x.shape, 0)
    return jnp.where(idx >= rows - k, fill, pltpu.roll(x, rows - k, 0))


def _scan_block(a, b, carry, reverse):
    rows, cols = a.shape
    sub = 8
    in_group = lax.broadcasted_iota(jnp.int32, a.shape, 0) % sub
    for sh in (1, 2, 4):
        if reverse:
            a_s, b_s, ok = pltpu.roll(a, rows - sh, 0), pltpu.roll(b, rows - sh, 0), in_group < sub - sh
        else:
            a_s, b_s, ok = pltpu.roll(a, sh, 0), pltpu.roll(b, sh, 0), in_group >= sh
        b = jnp.where(ok, a * b_s + b, b)
        a = jnp.where(ok, a * a_s, a)
    groups = list(range(rows // sub))
    edge = 0 if reverse else sub - 1
    carry_in = {}
    for v in (reversed(groups) if reverse else groups):
        carry_in[v] = carry
        row = sub * v + edge
        carry = b[row:row + 1, :] + a[row:row + 1, :] * carry
    cin = jnp.concatenate([jnp.broadcast_to(carry_in[v], (sub, cols)) for v in groups], axis=0)
    return b + a * cin


def _conv_taps(xcat):
    rows = xcat.shape[0]
    taps = []
    for k in range(4):
        off = HALO - 3 + k
        taps.append(xcat[off:off + LRU_T] if off == HALO else pltpu.roll(xcat, rows - off, 0)[:LRU_T])
    return taps


def _gates(xc, wa_ref, ba, wx_ref, bx, lam, za_ref, zx_ref):
    xm = xc.astype(MXU)
    for n in range(N_BLK):
        sl = slice(n * HD, (n + 1) * HD)
        za_ref[:, sl] = _dot(xm[:, sl], wa_ref[n])
        zx_ref[:, sl] = _dot(xm[:, sl], wx_ref[n])
    ra = _sigmoid(za_ref[...] + ba)
    ii = _sigmoid(zx_ref[...] + bx)
    sp = _softplus(-lam)
    log_a = -LRU_C * ra * sp
    a = jnp.exp(log_a)
    mult = jnp.sqrt(_neg_expm1(2.0 * log_a))
    return ra, ii, sp, a, mult


def lru_fwd(proj, conv_w, conv_b, wa, ba, wx, bx, lam, name):
    s = proj.shape[0]
    c = MIX_W
    nblk = s // LRU_T
    hpb = LRU_T // HALO

    def body(x_ref, halo_ref, cw_ref, cb_ref, wa_ref, ba_ref, wx_ref, bx_ref, lam_ref,
             xc_ref, h_ref, carry, za_ref, zx_ref):
        i = pl.program_id(0)

        @pl.when(i == 0)
        def _():
            carry[...] = jnp.zeros_like(carry)

        halo = jnp.where(i == 0, 0.0, halo_ref[...])
        xcat = jnp.concatenate([halo, x_ref[...]], axis=0)
        taps = _conv_taps(xcat)
        xc = cb_ref[...] + sum(cw_ref[k:k + 1, :] * taps[k] for k in range(4))
        xc_ref[...] = xc
        _, ii, _, a, mult = _gates(xc, wa_ref, ba_ref[...], wx_ref, bx_ref[...], lam_ref[...], za_ref, zx_ref)
        h = _scan_block(a, mult * (ii * xc), carry[HALO - 1:HALO, :], False)
        h_ref[...] = h
        carry[...] = h[LRU_T - HALO:, :]

    def full(shape):
        return BS(shape, lambda i: (0,) * len(shape))

    blk = BS((LRU_T, c), lambda i: (i, 0))
    return pl.pallas_call(
        body, grid=(nblk,),
        in_specs=[blk, BS((HALO, c), lambda i: (jnp.maximum(i * hpb - 1, 0), 0)),
                  full((4, c)), full((1, c)), full((N_BLK, HD, HD)), full((1, c)),
                  full((N_BLK, HD, HD)), full((1, c)), full((1, c))],
        out_specs=[blk, blk],
        out_shape=[SDS((s, c), F32), SDS((s, c), F32)],
        scratch_shapes=[pltpu.VMEM((HALO, c), F32), pltpu.VMEM((LRU_T, c), F32), pltpu.VMEM((LRU_T, c), F32)],
        compiler_params=_cp("arbitrary"), name=name,
    )(proj, proj, conv_w, conv_b.reshape(1, c), wa.astype(MXU), ba.reshape(1, c), wx.astype(MXU),
      bx.reshape(1, c), lam.reshape(1, c))


def lru_mix_prep(h, proj, m, name):
    s = h.shape[0]
    tr = _row_tile(s)

    def body(h_ref, gb_ref, m_ref, o_ref):
        ge, _ = _gelu_and_grad(gb_ref[...])
        o_ref[:, :MIX_W] = (h_ref[...] * ge).astype(o_ref.dtype)
        o_ref[:, MIX_W:] = m_ref[...]

    return pl.pallas_call(
        body, grid=(s // tr,),
        in_specs=[BS((tr, MIX_W), lambda i: (i, 0)), BS((tr, MIX_W), lambda i: (i, 1)),
                  BS((tr, MEM_W), lambda i: (i, 0))],
        out_specs=BS((tr, D), lambda i: (i, 0)), out_shape=SDS((s, D), MXU),
        compiler_params=_cp("parallel"), name=name,
    )(h, proj, m)


def lru_bwd(dym, proj, xc, hl, dqm, conv_w, wa, ba, wx, bx, lam, name):
    s = proj.shape[0]
    c = MIX_W
    nblk = s // LRU_T
    hpb = LRU_T // HALO
    wa_m = wa.astype(MXU)
    wx_m = wx.astype(MXU)

    def body(dy_ref, x_ref, xhalo_ref, gb_ref, xc_ref, h_ref, hhalo_ref, dqm_ref,
             cw_ref, wa_ref, ba_ref, wx_ref, bx_ref, lam_ref,
             dproj_ref, dcw_ref, dcb_ref, dwa_ref, dba_ref, dwx_ref, dbx_ref, dlam_ref,
             g_next, a_next, dxc_next, za_ref, zx_ref, dxc_ref):
        i = pl.program_id(0)

        @pl.when(i == 0)
        def _():
            g_next[...] = jnp.zeros_like(g_next)
            a_next[...] = jnp.zeros_like(a_next)
            dxc_next[...] = jnp.zeros_like(dxc_next)
            for r in (dcw_ref, dcb_ref, dwa_ref, dba_ref, dwx_ref, dbx_ref, dlam_ref):
                r[...] = jnp.zeros_like(r)

        first = i == nblk - 1
        xc = xc_ref[...]
        lam = lam_ref[...]
        ra, ii, sp, a, mult = _gates(xc, wa_ref, ba_ref[...], wx_ref, bx_ref[...], lam, za_ref, zx_ref)
        hl_v = h_ref[...]
        ge, dge = _gelu_and_grad(gb_ref[...])
        dyl = dy_ref[...]
        dhl = dyl * ge
        dproj_ref[:, c:2 * c] = (dyl * hl_v * dge).astype(dproj_ref.dtype)
        dproj_ref[:, 2 * c:] = dqm_ref[...]

        an = _shift_up(a, 1, 0.0)
        last_row = lax.broadcasted_iota(jnp.int32, a.shape, 0) == LRU_T - 1
        an = jnp.where(last_row, a_next[0:1, :], an)
        g = _scan_block(an, dhl, g_next[0:1, :], True)
        g_next[...] = g[:HALO, :]
        a_next[...] = a[:HALO, :]

        hhalo = jnp.where(first, 0.0, hhalo_ref[...])
        h_prev = _shift_down(hl_v, 1, 0.0)
        first_row = lax.broadcasted_iota(jnp.int32, a.shape, 0) == 0
        h_prev = jnp.where(first_row, hhalo[HALO - 1:HALO, :], h_prev)
        da = g * h_prev
        ixc = ii * xc
        dmult = g * ixc
        dii = g * mult * xc
        dxc = g * mult * ii
        dlog_a = (da - dmult * a / mult) * a
        dra = dlog_a * (-LRU_C) * sp
        dlam_ref[...] += jnp.sum(dlog_a * ra, axis=0, keepdims=True) * (LRU_C * _sigmoid(-lam))
        dza = dra * ra * (1.0 - ra)
        dzx = dii * ii * (1.0 - ii)
        dba_ref[...] += jnp.sum(dza, axis=0, keepdims=True)
        dbx_ref[...] += jnp.sum(dzx, axis=0, keepdims=True)
        xm = xc.astype(MXU)
        dza_m = dza.astype(MXU)
        dzx_m = dzx.astype(MXU)
        for n in range(N_BLK):
            sl = slice(n * HD, (n + 1) * HD)
            dwa_ref[n] += _dot(xm[:, sl], dza_m[:, sl], TN)
            dwx_ref[n] += _dot(xm[:, sl], dzx_m[:, sl], TN)
            dxc_ref[:, sl] = _dot(dza_m[:, sl], wa_ref[n], NT) + _dot(dzx_m[:, sl], wx_ref[n], NT)
        dxc = dxc + dxc_ref[...]

        dcat = jnp.concatenate([dxc, dxc_next[...]], axis=0)
        rows = dcat.shape[0]
        dxb = cw_ref[3:4, :] * dxc
        for k in range(3):
            dxb = dxb + cw_ref[k:k + 1, :] * pltpu.roll(dcat, rows - (3 - k), 0)[:LRU_T]
        dproj_ref[:, :c] = dxb.astype(dproj_ref.dtype)
        dxc_next[...] = dxc[:HALO, :]

        xhalo = jnp.where(first, 0.0, xhalo_ref[...])
        taps = _conv_taps(jnp.concatenate([xhalo, x_ref[...]], axis=0))
        for k in range(4):
            dcw_ref[k:k + 1, :] += jnp.sum(dxc * taps[k], axis=0, keepdims=True)
        dcb_ref[...] += jnp.sum(dxc, axis=0, keepdims=True)

    def full(shape):
        return BS(shape, lambda i: (0,) * len(shape))

    def rev(i):
        return nblk - 1 - i

    blk0 = BS((LRU_T, c), lambda i: (rev(i), 0))
    blk1 = BS((LRU_T, c), lambda i: (rev(i), 1))
    halo = BS((HALO, c), lambda i: (jnp.maximum(rev(i) * hpb - 1, 0), 0))
    outs = pl.pallas_call(
        body, grid=(nblk,),
        in_specs=[blk0, blk0, halo, blk1, blk0, blk0, halo, BS((LRU_T, MEM_W), lambda i: (rev(i), 0)),
                  full((4, c)), full((N_BLK, HD, HD)), full((1, c)), full((N_BLK, HD, HD)), full((1, c)),
                  full((1, c))],
        out_specs=[BS((LRU_T, 2 * c + MEM_W), lambda i: (rev(i), 0)), full((4, c)), full((1, c)),
                   full((N_BLK, HD, HD)), full((1, c)), full((N_BLK, HD, HD)), full((1, c)), full((1, c))],
        out_shape=[SDS((s, 2 * c + MEM_W), MXU), SDS((4, c), F32), SDS((1, c), F32),
                   SDS((N_BLK, HD, HD), F32), SDS((1, c), F32), SDS((N_BLK, HD, HD), F32), SDS((1, c), F32),
                   SDS((1, c), F32)],
        scratch_shapes=[pltpu.VMEM((HALO, c), F32), pltpu.VMEM((HALO, c), F32), pltpu.VMEM((HALO, c), F32),
                        pltpu.VMEM((LRU_T, c), F32), pltpu.VMEM((LRU_T, c), F32), pltpu.VMEM((LRU_T, c), F32)],
        compiler_params=_cp("arbitrary"), name=name,
    )(dym, proj, proj, proj, xc, hl, hl, dqm, conv_w, wa_m, ba.reshape(1, c), wx_m, bx.reshape(1, c),
      lam.reshape(1, c))
    dproj, dcw, dcb, dwa, dba, dwx, dbx, dlam = outs
    return dproj, dcw, dcb.reshape(c), dwa, dba.reshape(c), dwx, dbx.reshape(c), dlam.reshape(c)


def _mem_probs(q, kv):
    heads = [slice(hh * HD, (hh + 1) * HD) for hh in range(MEM_HEADS)]
    sc = [_dot(q[:, sl], kv[:, sl], NT) * SCALE for sl in heads]
    e = [jnp.exp(s - jnp.max(s, axis=-1, keepdims=True)) for s in sc]
    return [x / jnp.sum(x, axis=-1, keepdims=True) for x in e]


def mem_attn_fwd(proj, q_col, kvm, name):
    s = proj.shape[0]
    tq = min(512, s)

    def body(q_ref, kv_ref, o_ref):
        q = q_ref[...].astype(MXU)
        kv = kv_ref[...]
        p = _mem_probs(q, kv)
        outs = [_dot(p[hh].astype(MXU), kv[:, MEM_W + hh * HD:MEM_W + (hh + 1) * HD]) for hh in range(MEM_HEADS)]
        o_ref[...] = jnp.concatenate(outs, axis=1).astype(o_ref.dtype)

    return pl.pallas_call(
        body, grid=(s // tq,),
        in_specs=[BS((tq, MEM_W), lambda i: (i, q_col)), BS((N_MEM, 2 * MEM_W), lambda i: (0, 0))],
        out_specs=BS((tq, MEM_W), lambda i: (i, 0)), out_shape=SDS((s, MEM_W), MXU),
        compiler_params=_cp("parallel"), name=name,
    )(proj, kvm)


def mem_attn_bwd(proj, q_col, kvm, dym, name):
    s = proj.shape[0]
    tq = min(512, s)

    def body(q_ref, kv_ref, do_ref, dq_ref, dkv_ref):
        @pl.when(pl.program_id(0) == 0)
        def _():
            dkv_ref[...] = jnp.zeros_like(dkv_ref)

        q = q_ref[...].astype(MXU)
        do = do_ref[...].astype(MXU)
        kv = kv_ref[...]
        heads = [slice(hh * HD, (hh + 1) * HD) for hh in range(MEM_HEADS)]
        p = _mem_probs(q, kv)
        dp = [_dot(do[:, sl], kv[:, MEM_W + hh * HD:MEM_W + (hh + 1) * HD], NT) for hh, sl in enumerate(heads)]
        ds = [(pp * (d - jnp.sum(pp * d, axis=-1, keepdims=True)) * SCALE).astype(MXU) for pp, d in zip(p, dp)]
        dq = [_dot(x, kv[:, sl]) for x, sl in zip(ds, heads)]
        dk = [_dot(x, q[:, sl], TN) for x, sl in zip(ds, heads)]
        dv = [_dot(pp.astype(MXU), do[:, sl], TN) for pp, sl in zip(p, heads)]
        dq_ref[...] = jnp.concatenate(dq, axis=1).astype(dq_ref.dtype)
        dkv_ref[...] += jnp.concatenate(dk + dv, axis=1)

    return pl.pallas_call(
        body, grid=(s // tq,),
        in_specs=[BS((tq, MEM_W), lambda i: (i, q_col)), BS((N_MEM, 2 * MEM_W), lambda i: (0, 0)),
                  BS((tq, MEM_W), lambda i: (i, MIX_W // MEM_W))],
        out_specs=[BS((tq, MEM_W), lambda i: (i, 0)), BS((N_MEM, 2 * MEM_W), lambda i: (0, 0))],
        out_shape=[SDS((s, MEM_W), MXU), SDS((N_MEM, 2 * MEM_W), F32)],
        compiler_params=_cp("arbitrary"), name=name,
    )(proj, kvm, dym)


def _dil_scores(q, kp, kc, n, slope_dil):
    qi = lax.broadcasted_iota(jnp.int32, (Q_BLOCK, Q_BLOCK), 0)
    ki = lax.broadcasted_iota(jnp.int32, (Q_BLOCK, Q_BLOCK), 1)
    rel_p = qi + Q_BLOCK - ki
    rel_c = qi - ki
    s_p = _dot(q, kp, NT) * SCALE - slope_dil * rel_p.astype(F32)
    s_c = _dot(q, kc, NT) * SCALE - slope_dil * rel_c.astype(F32)
    s_p = jnp.where((rel_p <= Q_BLOCK) & (n > 0), s_p, NEG_INF)
    s_c = jnp.where(rel_c >= 0, s_c, NEG_INF)
    return s_p, s_c


def _slope_dil(gi, hh):
    head = 4 * gi + hh
    return DIL_GROUPS[gi][1] * 2.0 ** (-8.0 * (head + 1.0) / N_BLK)


def _dil_operands(proj, kv, gi):
    dil = DIL_GROUPS[gi][1]
    if dil == 1:
        return proj, kv, kv, (lambda r: gi), (lambda r: gi), (lambda r: MIX_W // MEM_W + gi)
    sub = proj.shape[0] // dil

    def view(a, col):
        return a[:, col:col + MEM_W].reshape(sub, dil * MEM_W)

    same = lambda r: r
    return view(proj, gi * MEM_W), view(kv, gi * MEM_W), view(kv, MIX_W + gi * MEM_W), same, same, same


def dil_attn_fwd(proj, kv, gi, name):
    dil = DIL_GROUPS[gi][1]
    s, pw = proj.shape
    sub = s // dil
    nb = sub // Q_BLOCK
    pair = 2 * Q_BLOCK

    def body(q_ref, kp_ref, kc_ref, vp_ref, vc_ref, o_ref, lse_ref):
        t = pl.program_id(1)
        q = q_ref[...].astype(MXU)
        k_prev, k_cur, v_prev, v_cur = kp_ref[...], kc_ref[...], vp_ref[...], vc_ref[...]
        units = []
        for b in range(2):
            rows = slice(b * Q_BLOCK, (b + 1) * Q_BLOCK)
            kp, vp = (k_prev, v_prev) if b == 0 else (k_cur[:Q_BLOCK], v_cur[:Q_BLOCK])
            for hh in range(4):
                sl = slice(hh * HD, (hh + 1) * HD)
                units.append((q[rows, sl], kp[:, sl], k_cur[rows, sl], vp[:, sl], v_cur[rows, sl], 2 * t + b, hh))
        sc = [_dil_scores(qh, kp, kc, n, _slope_dil(gi, hh)) for qh, kp, kc, _, _, n, hh in units]
        mx = [jnp.maximum(jnp.max(s_p, axis=-1, keepdims=True), jnp.max(s_c, axis=-1, keepdims=True))
              for s_p, s_c in sc]
        den = [jnp.sum(jnp.exp(s_p - m), axis=-1, keepdims=True) + jnp.sum(jnp.exp(s_c - m), axis=-1, keepdims=True)
               for (s_p, s_c), m in zip(sc, mx)]
        lse = [m + jnp.log(d) for m, d in zip(mx, den)]
        pr = [(jnp.exp(s_p - l).astype(MXU), jnp.exp(s_c - l).astype(MXU)) for (s_p, s_c), l in zip(sc, lse)]
        outs = [_dot(p_p, u[3]) + _dot(p_c, u[4]) for (p_p, p_c), u in zip(pr, units)]
        wide = [jnp.broadcast_to(l, (Q_BLOCK, HD)) for l in lse]
        o_ref[...] = jnp.concatenate([jnp.concatenate(outs[4 * b:4 * b + 4], axis=1) for b in range(2)], axis=0)
        lse_ref[...] = jnp.concatenate([jnp.concatenate(wide[4 * b:4 * b + 4], axis=1) for b in range(2)], axis=0)

    one, two = (Q_BLOCK, MEM_W), (pair, MEM_W)
    before = lambda t: jnp.maximum(2 * t - 1, 0)
    out = BS(two, lambda r, t: (t, r))
    qv, kview, vview, qcol, kcol, vcol = _dil_operands(proj, kv, gi)
    return pl.pallas_call(
        body, grid=(dil, nb // 2),
        in_specs=[BS(two, lambda r, t: (t, qcol(r))),
                  BS(one, lambda r, t: (before(t), kcol(r))), BS(two, lambda r, t: (t, kcol(r))),
                  BS(one, lambda r, t: (before(t), vcol(r))), BS(two, lambda r, t: (t, vcol(r)))],
        out_specs=[out, out],
        out_shape=[SDS((sub, dil * MEM_W), F32), SDS((sub, dil * MEM_W), F32)],
        compiler_params=_cp("parallel", "parallel"), name=name,
    )(qv, kview, kview, vview, vview)


def dil_attn_bwd(proj, kv, lse, do, dd, gi, name):
    dil = DIL_GROUPS[gi][1]
    s, pw = proj.shape
    sub = s // dil
    nb = sub // Q_BLOCK
    qc, kc_ = pw // MEM_W, kv.shape[1] // MEM_W

    def body(q_ref, kp_ref, kc_ref, vp_ref, vc_ref, lse_ref, do_ref, dd_ref, dq_ref, dk_ref, dv_ref, ck, cv):
        n = pl.program_id(1)

        @pl.when(n == 0)
        def _():
            ck[...] = jnp.zeros_like(ck)
            cv[...] = jnp.zeros_like(cv)

        @pl.when(n < nb)
        def _():
            q = q_ref[...].astype(MXU)
            do_m = do_ref[...].astype(MXU)
            kp, kc, vp, vc = kp_ref[...], kc_ref[...], vp_ref[...], vc_ref[...]
            lse_v, dd_v, ck_v, cv_v = lse_ref[...], dd_ref[...], ck[...], cv[...]
            heads = [slice(hh * HD, (hh + 1) * HD) for hh in range(4)]
            sc = [_dil_scores(q[:, sl], kp[:, sl], kc[:, sl], n, _slope_dil(gi, hh)) for hh, sl in enumerate(heads)]
            dp = [(_dot(do_m[:, sl], vp[:, sl], NT), _dot(do_m[:, sl], vc[:, sl], NT)) for sl in heads]
            pr = [(jnp.exp(s_p - lse_v[:, sl]), jnp.exp(s_c - lse_v[:, sl])) for (s_p, s_c), sl in zip(sc, heads)]
            ds = [((p_p * (dp_p + dd_v[:, sl]) * SCALE).astype(MXU), (p_c * (dp_c + dd_v[:, sl]) * SCALE).astype(MXU))
                  for (p_p, p_c), (dp_p, dp_c), sl in zip(pr, dp, heads)]
            pm = [(p_p.astype(MXU), p_c.astype(MXU)) for p_p, p_c in pr]
            dq = [_dot(ds_p, kp[:, sl]) + _dot(ds_c, kc[:, sl]) for (ds_p, ds_c), sl in zip(ds, heads)]
            dk = [ck_v[:, sl] + _dot(ds_p, q[:, sl], TN) for (ds_p, _), sl in zip(ds, heads)]
            dv = [cv_v[:, sl] + _dot(p_p, do_m[:, sl], TN) for (p_p, _), sl in zip(pm, heads)]
            ck_new = [_dot(ds_c, q[:, sl], TN) for (_, ds_c), sl in zip(ds, heads)]
            cv_new = [_dot(p_c, do_m[:, sl], TN) for (_, p_c), sl in zip(pm, heads)]
            dq_ref[...] = jnp.concatenate(dq, axis=1).astype(dq_ref.dtype)
            dk_ref[...] = jnp.concatenate(dk, axis=1)
            dv_ref[...] = jnp.concatenate(dv, axis=1)
            ck[...] = jnp.concatenate(ck_new, axis=1)
            cv[...] = jnp.concatenate(cv_new, axis=1)

        @pl.when(n == nb)
        def _():
            dk_ref[...] = ck[...]
            dv_ref[...] = cv[...]

    blk = (Q_BLOCK, MEM_W)
    cur = lambda n: jnp.minimum(n, nb - 1)
    prev = lambda n: jnp.maximum(jnp.minimum(n, nb - 1) - 1, 0)
    done = lambda n: jnp.maximum(n - 1, 0)
    own = BS(blk, lambda r, n: (cur(n), r))
    qv, kview, vview, qcol, kcol, vcol = _dil_operands(proj, kv, gi)
    return pl.pallas_call(
        body, grid=(dil, nb + 1),
        in_specs=[BS(blk, lambda r, n: (cur(n), qcol(r))),
                  BS(blk, lambda r, n: (prev(n), kcol(r))), BS(blk, lambda r, n: (cur(n), kcol(r))),
                  BS(blk, lambda r, n: (prev(n), vcol(r))), BS(blk, lambda r, n: (cur(n), vcol(r))),
                  own, own, own],
        out_specs=[own, BS(blk, lambda r, n: (done(n), r)), BS(blk, lambda r, n: (done(n), r))],
        out_shape=[SDS((sub, dil * MEM_W), MXU), SDS((sub, dil * MEM_W), F32), SDS((sub, dil * MEM_W), F32)],
        scratch_shapes=[pltpu.VMEM(blk, F32), pltpu.VMEM(blk, F32)],
        compiler_params=_cp("parallel", "arbitrary"), name=name,
    )(qv, kview, kview, vview, vview, lse, do, dd)


def _group_weights(lse_refs):
    l0, l1, l2 = (r[...] for r in lse_refs)
    mx = jnp.maximum(jnp.maximum(l0, l1), l2)
    e = [jnp.exp(l - mx) for l in (l0, l1, l2)]
    den = e[0] + e[1] + e[2]
    return [x / den for x in e]


def dil_mix_prep(o_list, lse_list, m, name):
    s = m.shape[0]
    tr = _row_tile(s)

    def body(o0, o1, o2, l0, l1, l2, m_ref, out_ref):
        w = _group_weights((l0, l1, l2))
        for g, o_ref in enumerate((o0, o1, o2)):
            out_ref[:, g * MEM_W:(g + 1) * MEM_W] = (o_ref[...] * w[g]).astype(out_ref.dtype)
        out_ref[:, MIX_W:] = m_ref[...]

    blk = BS((tr, MEM_W), lambda i: (i, 0))
    return pl.pallas_call(
        body, grid=(s // tr,), in_specs=[blk] * 7,
        out_specs=BS((tr, D), lambda i: (i, 0)), out_shape=SDS((s, D), MXU),
        compiler_params=_cp("parallel"), name=name,
    )(*o_list, *lse_list, m)


def dil_mix_bwd(dym, o_list, lse_list, name):
    s = dym.shape[0]
    tr = _row_tile(s)

    def body(da_ref, o0, o1, o2, l0, l1, l2, do0, do1, do2, dd0, dd1, dd2):
        w = _group_weights((l0, l1, l2))
        tot = None
        for g, (o_ref, do_ref) in enumerate(zip((o0, o1, o2), (do0, do1, do2))):
            da = da_ref[:, g * MEM_W:(g + 1) * MEM_W]
            do_ref[...] = da * w[g]
            x = da * o_ref[...]
            dw = jnp.concatenate(
                [jnp.broadcast_to(jnp.sum(x[:, hh * HD:(hh + 1) * HD], axis=-1, keepdims=True), (tr, HD))
                 for hh in range(4)], axis=1)
            tot = w[g] * dw if tot is None else tot + w[g] * dw
        for g, dd_ref in enumerate((dd0, dd1, dd2)):
            dd_ref[...] = -w[g] * tot

    blk = BS((tr, MEM_W), lambda i: (i, 0))
    outs = pl.pallas_call(
        body, grid=(s // tr,), in_specs=[BS((tr, MIX_W), lambda i: (i, 0))] + [blk] * 6,
        out_specs=[blk] * 6, out_shape=[SDS((s, MEM_W), F32)] * 6,
        compiler_params=_cp("parallel"), name=name,
    )(dym, *o_list, *lse_list)
    return outs[:3], outs[3:]


def sum_cast(parts, name):
    s = parts[0][0].shape[0]
    tr = _row_tile(s)
    flat = [a for p in parts for a in p]
    sizes = [len(p) for p in parts]

    def body(*refs):
        out_ref = refs[-1]
        pos = 0
        for j, n in enumerate(sizes):
            acc = refs[pos][...].astype(F32)
            for t in range(1, n):
                acc = acc + refs[pos + t][...].astype(F32)
            out_ref[:, j * MEM_W:(j + 1) * MEM_W] = acc.astype(out_ref.dtype)
            pos += n

    blk = BS((tr, MEM_W), lambda i: (i, 0))
    width = MEM_W * len(parts)
    return pl.pallas_call(
        body, grid=(s // tr,), in_specs=[blk] * len(flat),
        out_specs=BS((tr, width), lambda i: (i, 0)), out_shape=SDS((s, width), MXU),
        compiler_params=_cp("parallel"), name=name,
    )(*flat)


def add_n(arrs, name):
    rows, cols = arrs[0].shape
    tr = _row_tile(rows)

    def body(*refs):
        acc = refs[0][...]
        for r in refs[1:-1]:
            acc = acc + r[...]
        refs[-1][...] = acc

    blk = BS((tr, cols), lambda i: (i, 0))
    return pl.pallas_call(
        body, grid=(rows // tr,), in_specs=[blk] * len(arrs), out_specs=blk,
        out_shape=SDS((rows, cols), F32), compiler_params=_cp("parallel"), name=name,
    )(*arrs)


class _NoExchange:
    def hook(self, where, l, after):
        return []


def _fwd_bwd(x, mem, target, small, big, gs, gb, sched):
    s = x.shape[0]
    tm = min(1024, s)
    ts = min(2048, s)

    def after_hook(arr, where, l, after):
        toks = sched.hook(where, l, after)
        return tie(arr, toks, "tie_%s_%d" % (where, l)) if toks else arr

    h = x
    saved = []
    kv = None
    mem_n = None
    hn = norm_cast(h, small["a_pre_mix_g"][0], "pre_norm")
    for l in range(4):
        rec = l < 2
        p, j = ("a", l) if rec else ("b", l - 2)
        sv = {"h": h}
        hn = after_hook(hn, "fwd_begin", l, h)
        if mem_n is None:
            mem_n = norm_cast(mem, small["mem_norm_g"], "mem_norm")
        kvm = mm_nn(mem_n, big[p + "_w_mem_kv"][j], tm=N_MEM, tn=2 * MEM_W, tk=D, out_dtype=MXU, name="mem_kv")
        if rec:
            proj = mm_nn(hn, big["a_w_in"][j], tm=min(2 * tm, s), tn=896, tk=D, out_dtype=F32, name="rec_in")
            xc, hl = lru_fwd(proj, small["a_conv_w"][j], small["a_conv_b"][j], small["a_gate_a_w"][j],
                             small["a_gate_a_b"][j], small["a_gate_x_w"][j], small["a_gate_x_b"][j],
                             small["a_lambda"][j], "lru_fwd")
            m = mem_attn_fwd(proj, 2 * MIX_W // MEM_W, kvm, "rec_mem_attn")
            ym = lru_mix_prep(hl, proj, m, "lru_mix_prep")
            sv.update(xc=xc, hl=hl)
        else:
            proj = mm_nn(hn, big["b_w_in"][j], tm=tm, tn=D, tk=D, out_dtype=F32, name="dil_in")
            o_list, lse_list = [], []
            for gi in range(3):
                o, lse = dil_attn_fwd(proj, kv, gi, "dil_attn_fwd%d" % gi)
                o_list.append(o.reshape(s, MEM_W))
                lse_list.append(lse.reshape(s, MEM_W))
            m = mem_attn_fwd(proj, MIX_W // MEM_W, kvm, "dil_mem_attn")
            ym = dil_mix_prep(o_list, lse_list, m, "dil_mix_prep")
            sv.update(o=o_list, lse=lse_list)
        ym = after_hook(ym, "fwd_q1", l, ym)
        mix = mm_nn(ym, big[p + "_w_out"][j], tm=tm, tn=D, tk=D, out_dtype=F32, name="mix_out")
        h1, hn2 = resid_norm_next(h, mix, small[p + "_post_mix_g"][j], small[p + "_pre_ffn_g"][j], "post_pre_norm")
        hn2 = after_hook(hn2, "fwd_mid", l, mix)
        g, u, act = ffn_in_fwd(hn2, big[p + "_w_ffn_in"][j], "ffn_in")
        act = after_hook(act, "fwd_q3", l, u)
        y2 = mm_nn(act, big[p + "_w_ffn_out"][j], tm=tm // 2, tn=D, tk=D_FF // 2, out_dtype=F32, name="ffn_out")
        sv.update(kvm=kvm, hn=hn, proj=proj, ym=ym, mix=mix, h1=h1, hn2=hn2, g=g, u=u, act=act, y2=y2)
        saved.append(sv)
        if l < 3:
            pn, jn = ("a", l + 1) if l + 1 < 2 else ("b", l - 1)
            h, hn = resid_norm_next(h1, y2, small[p + "_post_ffn_g"][j], small[pn + "_pre_mix_g"][jn],
                                    "post_pre_norm")
        else:
            h = resid_norm(h1, y2, small[p + "_post_ffn_g"][j], "post_norm")
        sched.hook("fwd_end", l, h)
        if l == 1:
            h_kv = h
            kvn = norm_cast(h, small["kv_norm_g"], "pre_norm")
            kv = mm_nn(kvn, big["w_kv_shared"], tm=tm, tn=768, tk=D, out_dtype=MXU, name="kv_proj")

    loss_parts, dh = loss_head(h, target, "loss_head")

    def stack2(name, j, val):
        gs.setdefault(name, [None, None])[j] = val

    def stack2b(name, j, val):
        gb.setdefault(name, [None, None])[j] = val

    dkv_parts = []
    ahead = []
    dmem_parts = []
    dkvm = [None] * 4
    for l in (3, 2, 1, 0):
        rec = l < 2
        p, j = ("a", l) if rec else ("b", l - 2)
        sv = saved[l]
        if l == 1:
            dkv = sum_cast([(dkv_parts[0][c], dkv_parts[1][c]) for c in range(6)], "dkv_sum")
            dkvn = mm_nt([dkv], big["w_kv_shared"], tm=tm, tn=D, tk=768, out_dtype=MXU, name="kv_proj_dx")
            gb["w_kv_shared"] = mm_tn(kvn, [dkv], t1=D, tn=768, ts=ts, col_shards=True, name="kv_proj_dw")
            dh, gs["kv_norm_g"], *ahead = norm_bwd(h_kv, small["kv_norm_g"], dkvn, dh, F32, "pre_post_norm_bwd",
                                                   then=(sv["y2"], small["a_post_ffn_g"][1]))
        if ahead:
            dy2, dg = ahead
            ahead = []
        else:
            dy2, dg = norm_bwd(sv["y2"], small[p + "_post_ffn_g"][j], dh, None, MXU, "post_norm_bwd")
        dy2 = after_hook(dy2, "bwd_begin", l, dh)
        stack2(p + "_post_ffn_g", j, dg)
        dgg, dgu = ffn_act_bwd(dy2, big[p + "_w_ffn_out"][j], sv["g"], sv["u"], "ffn_act_bwd")
        dgg = after_hook(dgg, "bwd_mid1", l, dgu)
        stack2b(p + "_w_ffn_out", j, mm_tn(sv["act"], [dy2], t1=D_FF // 4, tn=D, ts=ts // 2, col_shards=False,
                                          name="ffn_out_dw"))
        dhn2 = mm_nt([dgg, dgu], big[p + "_w_ffn_in"][j], tm=tm // 2, tn=D, tk=D_FF // 2, out_dtype=MXU,
                     name="ffn_in_dx")
        stack2b(p + "_w_ffn_in", j, mm_tn(sv["hn2"], [dgg, dgu], t1=D // 2, tn=D_FF // 4, ts=ts, col_shards=True,
                                         name="ffn_in_dw"))
        dhn2 = after_hook(dhn2, "bwd_mid2", l, gb[p + "_w_ffn_in"][j])
        dh1, dg, dmix, dg_mix = norm_bwd(sv["h1"], small[p + "_pre_ffn_g"][j], dhn2, dh, F32, "pre_post_norm_bwd",
                                         then=(sv["mix"], small[p + "_post_mix_g"][j]))
        stack2(p + "_pre_ffn_g", j, dg)
        stack2(p + "_post_mix_g", j, dg_mix)
        dym = mm_nt([dmix], big[p + "_w_out"][j], tm=tm, tn=D, tk=D, out_dtype=F32, name="mix_out_dx")
        stack2b(p + "_w_out", j, mm_tn(sv["ym"], [dmix], t1=D, tn=1024, ts=ts, col_shards=False,
                                      name="mix_out_dw"))
        dym = after_hook(dym, "bwd_m1", l, gb[p + "_w_out"][j])
        if rec:
            dqm, dkvm[l] = mem_attn_bwd(sv["proj"], 2 * MIX_W // MEM_W, sv["kvm"], dym, "rec_mem_attn_bwd")
            dproj, dcw, dcb, dwa, dba, dwx, dbx, dlam = lru_bwd(
                dym, sv["proj"], sv["xc"], sv["hl"], dqm, small["a_conv_w"][j], small["a_gate_a_w"][j],
                small["a_gate_a_b"][j], small["a_gate_x_w"][j], small["a_gate_x_b"][j], small["a_lambda"][j],
                "lru_bwd")
            for nm, val in (("a_conv_w", dcw), ("a_conv_b", dcb), ("a_gate_a_w", dwa), ("a_gate_a_b", dba),
                            ("a_gate_x_w", dwx), ("a_gate_x_b", dbx), ("a_lambda", dlam)):
                stack2(nm, j, val)
            dhn = mm_nt([dproj], big["a_w_in"][j], tm=tm, tn=D, tk=896, out_dtype=MXU, name="rec_in_dx")
            stack2b("a_w_in", j, mm_tn(sv["hn"], [dproj], t1=D, tn=896, ts=ts, col_shards=True, name="rec_in_dw"))
        else:
            dqm, dkvm[l] = mem_attn_bwd(sv["proj"], MIX_W // MEM_W, sv["kvm"], dym, "dil_mem_attn_bwd")
            do_list, dd_list = dil_mix_bwd(dym, sv["o"], sv["lse"], "dil_mix_bwd")
            dq_list, dk_list, dv_list = [], [], []
            for gi in range(3):
                dil = DIL_GROUPS[gi][1]
                view = (s // dil, dil * MEM_W)
                dq, dk, dv = dil_attn_bwd(sv["proj"], kv, sv["lse"][gi].reshape(view), do_list[gi].reshape(view),
                                          dd_list[gi].reshape(view), gi, "dil_attn_bwd%d" % gi)
                dq_list.append(dq.reshape(s, MEM_W))
                dk_list.append(dk.reshape(s, MEM_W))
                dv_list.append(dv.reshape(s, MEM_W))
            dkv_parts.append(dk_list + dv_list)
            dproj = sum_cast([(a,) for a in dq_list + [dqm]], "dil_dproj")
            dhn = mm_nt([dproj], big["b_w_in"][j], tm=tm, tn=D, tk=D, out_dtype=MXU, name="dil_in_dx")
            stack2b("b_w_in", j, mm_tn(sv["hn"], [dproj], t1=D, tn=1024, ts=ts, col_shards=False, name="dil_in_dw"))
        dk_m = dkvm[l].astype(MXU)
        dmem_parts.append(mm_nt([dk_m], big[p + "_w_mem_kv"][j], tm=N_MEM, tn=D, tk=2 * MEM_W, out_dtype=F32,
                                name="mem_kv_dx"))
        stack2b(p + "_w_mem_kv", j, mm_tn(mem_n, [dk_m], t1=D, tn=2 * MEM_W, ts=N_MEM, col_shards=False,
                                         name="mem_kv_dw"))
        if l in (3, 1):
            pn, jn = ("b", 0) if l == 3 else ("a", 0)
            dh, dg, *ahead = norm_bwd(sv["h"], small[p + "_pre_mix_g"][j], dhn, dh1, F32, "pre_post_norm_bwd",
                                      then=(saved[l - 1]["y2"], small[pn + "_post_ffn_g"][jn]))
        else:
            dh, dg = norm_bwd(sv["h"], small[p + "_pre_mix_g"][j], dhn, dh1, F32, "pre_norm_bwd")
        stack2(p + "_pre_mix_g", j, dg)
        dh = after_hook(dh, "bwd_end", l, dh)

    _, gs["mem_norm_g"] = norm_bwd(mem, small["mem_norm_g"], add_n(dmem_parts, "dmem_sum"), None, F32,
                                   "mem_norm_bwd")
    return loss_parts, dh


ANY = pl.BlockSpec(memory_space=pl.ANY)
CHIP_FLIPS = (1, 2, 3)


def _coords():
    return lax.axis_index("x"), lax.axis_index("y"), lax.axis_index("c")


def _flip(x, y, m):
    return x ^ (m >> 1), y ^ (m & 1)


def _remote(src, dst, send_sems, recv_sems, k, device):
    return pltpu.make_async_remote_copy(src_ref=src, dst_ref=dst, send_sem=send_sems.at[k], recv_sem=recv_sems.at[k],
                                        device_id=device, device_id_type=MESH)


def _sum_rows_tile(rows, cols, itemsize=4):
    for tr in (512, 256, 128, 64, 32, 16):
        if rows % tr == 0 and tr * cols * itemsize <= 2 * 1024 * 1024:
            return tr
    raise ValueError((rows, cols))


def half_sum(g, got, name):
    _, r, cols = g.shape
    hr = r // 2
    tr = _sum_rows_tile(hr, cols, g.dtype.itemsize)

    def my_chip():
        return 2 * lax.axis_index("x") + lax.axis_index("y")

    def body(g_ref, got_ref, o_ref, own_ref):
        p = (g_ref[...].astype(F32) + got_ref[...].astype(F32)).astype(o_ref.dtype)
        o_ref[...] = p

        @pl.when(pl.program_id(1) == my_chip())
        def _():
            own_ref[...] = p

    out = SDS((N_CHIPS, hr, cols), jnp.bfloat16)
    return pl.pallas_call(
        body, grid=(hr // tr, N_CHIPS),
        in_specs=[BS((None, None, tr, cols), lambda i, s: (s, lax.axis_index("c"), i, 0)),
                  BS((None, tr, cols), lambda i, s: (s, i, 0))],
        out_specs=[BS((None, tr, cols), lambda i, s: (s, i, 0)),
                   BS((None, tr, cols), lambda i, s: (my_chip(), i, 0))],
        out_shape=[out, out], compiler_params=_cp("parallel", "arbitrary"), name=name,
    )(g.reshape(N_CHIPS, 2, hr, cols), got)


def slot_sum(slots, name):
    _, hr, cols = slots.shape
    tr = _sum_rows_tile(hr, cols)
    nblk = hr // tr

    def body(s_ref, o_ref):
        acc = s_ref[0].astype(F32)
        for p in range(1, N_CHIPS):
            acc = acc + s_ref[p].astype(F32)
        o_ref[...] = acc

    return pl.pallas_call(
        body, grid=(nblk,), in_specs=[BS((N_CHIPS, tr, cols), lambda i: (0, i, 0))],
        out_specs=BS((tr, cols), lambda i: (lax.axis_index("c") * nblk + i, 0)),
        out_shape=SDS((2 * hr, cols), F32), compiler_params=_cp("parallel"), name=name,
    )(slots)


HBM_SPEC = pl.BlockSpec(memory_space=pltpu.HBM)
SEM_SPEC = pl.BlockSpec(memory_space=pltpu.SEMAPHORE)
EFFECT = pltpu.SideEffectType.DATAFLOW_SIDE_EFFECTING


def split_start(name, bufs, plan, n_copies):
    nb = len(bufs)

    def body(*refs):
        send_sems, recv_sems = refs[nb], refs[nb + 1]
        for k, (src, dst, dev) in enumerate(plan(refs[:nb])):
            _remote(src, dst, send_sems, recv_sems, k, dev).start()
        refs[-1][...] = jnp.zeros_like(refs[-1])

    outs = pl.pallas_call(
        body, name=name,
        out_shape=(pltpu.SemaphoreType.DMA((n_copies,)), pltpu.SemaphoreType.DMA((n_copies,)),
                   *[pltpu.HBM(b.shape, b.dtype) for b in bufs], SDS((8, LANES), F32)),
        in_specs=[HBM_SPEC] * nb, out_specs=(SEM_SPEC, SEM_SPEC, *[HBM_SPEC] * nb, VM),
        input_output_aliases={i: 2 + i for i in range(nb)},
        compiler_params=pltpu.CompilerParams(has_side_effects=EFFECT),
    )(*[pltpu.with_memory_space_constraint(b, pltpu.HBM) for b in bufs])
    return outs[0], outs[1], list(outs[2:2 + nb]), outs[-1]


def split_wait(name, send_sems, recv_sems, bufs, after, plan):
    nb = len(bufs)

    def body(*refs):
        send_ref, recv_ref = refs[nb], refs[nb + 1]
        for k, (src, dst, dev) in enumerate(plan(refs[:nb])):
            cp = _remote(src, dst, send_ref, recv_ref, k, dev)
            cp.wait_send()
            cp.wait_recv()

    outs = pl.pallas_call(
        body, name=name, out_shape=[pltpu.HBM(b.shape, b.dtype) for b in bufs],
        in_specs=[HBM_SPEC] * nb + [SEM_SPEC, SEM_SPEC, ANY], out_specs=[HBM_SPEC] * nb,
        input_output_aliases={i: i for i in range(nb)},
        compiler_params=pltpu.CompilerParams(has_side_effects=EFFECT),
    )(*bufs, send_sems, recv_sems, after)
    return list(outs)


def tie(x, tokens, name):
    def body(*refs):
        pass

    return pl.pallas_call(
        body, name=name, out_shape=SDS(x.shape, x.dtype), in_specs=[ANY] * (1 + len(tokens)), out_specs=ANY,
        input_output_aliases={0: 0},
    )(x, *tokens)


def plan_gather_ici(n, rows):
    def plan(refs):
        x, y, c = _coords()
        me = 2 * x + y
        out = []
        for i in range(n):
            hr = rows[i] // 2
            mine = pl.ds(pl.multiple_of(c * hr, 8), hr)
            out.append((refs[i], refs[n + i].at[me], (x, y, 1 - c)))
            for m in CHIP_FLIPS:
                out.append((refs[i].at[mine], refs[n + i].at[me, mine], (*_flip(x, y, m), c)))
        return out
    return plan


def plan_gather_d2d(n, rows):
    def plan(refs):
        x, y, c = _coords()
        me = 2 * x + y
        out = []
        for i in range(n):
            hr = rows[i] // 2
            mine = pl.ds(pl.multiple_of(c * hr, 8), hr)
            for m in CHIP_FLIPS:
                slot = refs[i].at[me ^ m, mine]
                out.append((slot, slot, (x, y, 1 - c)))
        return out
    return plan


def plan_swap(n, rows):
    def plan(refs):
        x, y, c = _coords()
        out = []
        for i in range(n):
            hr = rows[i] // 2
            other = pl.ds(pl.multiple_of((1 - c) * hr, 8), hr)
            out.append((refs[i].at[pl.ds(0, N_CHIPS), other], refs[n + i], (x, y, 1 - c)))
        return out
    return plan


def plan_exchange(n):
    def plan(refs):
        x, y, c = _coords()
        me = 2 * x + y
        out = []
        for i in range(n):
            for m in CHIP_FLIPS:
                out.append((refs[i].at[me ^ m], refs[n + i].at[me], (*_flip(x, y, m), c)))
        return out
    return plan


def plan_share(n, rows):
    def plan(refs):
        x, y, c = _coords()
        out = []
        for i in range(n):
            hr = rows[i] // 2
            mine = refs[i].at[pl.ds(pl.multiple_of(c * hr, 8), hr)]
            out.append((mine, mine, (x, y, 1 - c)))
        return out
    return plan


VM = pl.BlockSpec(memory_space=pltpu.VMEM)


def small_gather(v, name):
    def body(v_ref, out_ref, send_sems, recv_sems):
        x, y, c = _coords()
        me = 2 * x + y
        out_ref[me] = v_ref[...]
        cps = []
        for j, m in enumerate(CHIP_FLIPS):
            cp = _remote(v_ref, out_ref.at[me], send_sems, recv_sems, j, (*_flip(x, y, m), c))
            cp.start()
            cps.append(cp)
        for cp in cps:
            cp.wait()

    return pl.pallas_call(
        body, in_specs=[VM], out_specs=VM, out_shape=SDS((N_CHIPS,) + v.shape, v.dtype),
        scratch_shapes=[pltpu.SemaphoreType.DMA((3,)), pltpu.SemaphoreType.DMA((3,))],
        compiler_params=pltpu.CompilerParams(vmem_limit_bytes=VMEM_LIMIT_BYTES), name=name,
    )(v)


def plan_small_swap(refs):
    x, y, c = _coords()
    return [(refs[0], refs[1], (x, y, 1 - c))]


def plan_small_exchange(refs):
    x, y, c = _coords()
    me = 2 * x + y
    return [(refs[0].at[me], refs[0].at[me], (*_flip(x, y, m), c)) for m in CHIP_FLIPS]


def small_pair(v, sib, name):
    rows, cols = v.shape
    tr = _sum_rows_tile(rows, cols)

    def body(v_ref, s_ref, o_ref):
        o_ref[...] = v_ref[...] + s_ref[...]

    blk = BS((tr, cols), lambda i: (i, 0))
    return pl.pallas_call(
        body, grid=(rows // tr,), in_specs=[blk, blk],
        out_specs=BS((None, tr, cols), lambda i: (2 * lax.axis_index("x") + lax.axis_index("y"), i, 0)),
        out_shape=SDS((N_CHIPS, rows, cols), F32), compiler_params=_cp("parallel"), name=name,
    )(v, sib)


def small_total(slots, name):
    _, rows, cols = slots.shape
    tr = _sum_rows_tile(rows, cols)

    def body(s_ref, o_ref):
        o_ref[...] = (s_ref[0] + s_ref[1]) + (s_ref[2] + s_ref[3])

    return pl.pallas_call(
        body, grid=(rows // tr,), in_specs=[BS((N_CHIPS, tr, cols), lambda i: (0, i, 0))],
        out_specs=BS((tr, cols), lambda i: (i, 0)), out_shape=SDS((rows, cols), F32),
        compiler_params=_cp("parallel"), name=name,
    )(slots)


def adamw(w, g_list, m, v, name):
    nl, rows, cols = w.shape
    tr = _sum_rows_tile(rows, cols) if rows % 16 == 0 else rows
    bc1 = 1.0 - ADAM_B1 ** ADAM_STEP
    bc2 = 1.0 - ADAM_B2 ** ADAM_STEP

    def body(*refs):
        w_ref, m_ref, v_ref = refs[:3]
        g_refs = refs[3:3 + nl]
        go_ref, d_ref, mo_ref, vo_ref = refs[3 + nl:]
        layer = pl.program_id(0)
        for l in range(nl):
            @pl.when(layer == l)
            def _(l=l):
                g = g_refs[l][...]
                m_new = ADAM_B1 * m_ref[...] + (1.0 - ADAM_B1) * g
                v_new = ADAM_B2 * v_ref[...] + (1.0 - ADAM_B2) * (g * g)
                m_hat = m_new / bc1
                v_hat = v_new / bc2
                go_ref[...] = g
                d_ref[...] = -ADAM_LR * (m_hat / (jnp.sqrt(v_hat) + ADAM_EPS) + ADAM_WD * w_ref[...])
                mo_ref[...] = m_new
                vo_ref[...] = v_new

    stk = BS((None, tr, cols), lambda l, i: (l, i, 0))
    flat = BS((tr, cols), lambda l, i: (i, 0))
    out = SDS((nl, rows, cols), F32)
    return pl.pallas_call(
        body, grid=(nl, rows // tr), in_specs=[stk] * 3 + [flat] * nl, out_specs=[stk] * 4,
        out_shape=[out] * 4, compiler_params=_cp("parallel", "parallel"), name=name,
    )(w, m, v, *g_list)


WEIGHTS = ["mem_norm_g", "a_pre_mix_g", "a_post_mix_g", "a_pre_ffn_g", "a_post_ffn_g", "a_w_in", "a_conv_w",
           "a_conv_b", "a_gate_a_w", "a_gate_a_b", "a_gate_x_w", "a_gate_x_b", "a_lambda", "a_w_mem_kv", "a_w_out",
           "a_w_ffn_in", "a_w_ffn_out", "kv_norm_g", "w_kv_shared", "b_pre_mix_g", "b_post_mix_g", "b_pre_ffn_g",
           "b_post_ffn_g", "b_w_in", "b_w_mem_kv", "b_w_out", "b_w_ffn_in", "b_w_ffn_out"]
BIG = {"a_w_in": True, "a_w_mem_kv": False, "a_w_out": False, "a_w_ffn_in": True, "a_w_ffn_out": False,
       "w_kv_shared": True, "b_w_in": False, "b_w_mem_kv": False, "b_w_out": False, "b_w_ffn_in": True,
       "b_w_ffn_out": False}
SHARDED_SMALL = ["a_pre_mix_g", "a_post_mix_g", "a_pre_ffn_g", "a_post_ffn_g", "a_conv_w", "a_conv_b", "a_gate_a_b",
                 "a_gate_x_b", "a_lambda"]
REPL_SMALL = ["mem_norm_g", "kv_norm_g", "b_pre_mix_g", "b_post_mix_g", "b_pre_ffn_g", "b_post_ffn_g", "a_gate_a_w",
              "a_gate_x_w"]
LANES = 128


def _pack(arrs, row_multiple=8):
    flat = jnp.concatenate([a.reshape(-1) for a in arrs])
    pad = -flat.shape[0] % (LANES * row_multiple)
    if pad:
        flat = jnp.concatenate([flat, jnp.zeros((pad,), flat.dtype)])
    return flat.reshape(-1, LANES)


def _unpack(packed, shapes):
    flat = packed.reshape(-1)
    out, pos = [], 0
    for sh in shapes:
        size = math.prod(sh)
        out.append(flat[pos:pos + size].reshape(sh))
        pos += size
    return out


def kernel(x, mem, mem_norm_g, a_pre_mix_g, a_post_mix_g, a_pre_ffn_g, a_post_ffn_g, a_w_in, a_conv_w, a_conv_b,
           a_gate_a_w, a_gate_a_b, a_gate_x_w, a_gate_x_b, a_lambda, a_w_mem_kv, a_w_out, a_w_ffn_in, a_w_ffn_out,
           kv_norm_g, w_kv_shared, b_pre_mix_g, b_post_mix_g, b_pre_ffn_g, b_post_ffn_g, b_w_in, b_w_mem_kv, b_w_out,
           b_w_ffn_in, b_w_ffn_out, loss_target, m_mem_norm_g, m_a_pre_mix_g, m_a_post_mix_g, m_a_pre_ffn_g,
           m_a_post_ffn_g, m_a_w_in, m_a_conv_w, m_a_conv_b, m_a_gate_a_w, m_a_gate_a_b, m_a_gate_x_w, m_a_gate_x_b,
           m_a_lambda, m_a_w_mem_kv, m_a_w_out, m_a_w_ffn_in, m_a_w_ffn_out, m_kv_norm_g, m_w_kv_shared, m_b_pre_mix_g,
           m_b_post_mix_g, m_b_pre_ffn_g, m_b_post_ffn_g, m_b_w_in, m_b_w_mem_kv, m_b_w_out, m_b_w_ffn_in, m_b_w_ffn_out,
           v_mem_norm_g, v_a_pre_mix_g, v_a_post_mix_g, v_a_pre_ffn_g, v_a_post_ffn_g, v_a_w_in, v_a_conv_w, v_a_conv_b,
           v_a_gate_a_w, v_a_gate_a_b, v_a_gate_x_w, v_a_gate_x_b, v_a_lambda, v_a_w_mem_kv, v_a_w_out, v_a_w_ffn_in,
           v_a_w_ffn_out, v_kv_norm_g, v_w_kv_shared, v_b_pre_mix_g, v_b_post_mix_g, v_b_pre_ffn_g, v_b_post_ffn_g,
           v_b_w_in, v_b_w_mem_kv, v_b_w_out, v_b_w_ffn_in, v_b_w_ffn_out):
    a = dict(locals())
    xi, yi, ci = _coords()
    chip = 2 * xi + yi

    got = small_gather(_pack([a[n] for n in SHARDED_SMALL]), "small_gather")
    per_chip = [_unpack(got[s], [a[n].shape for n in SHARDED_SMALL]) for s in range(N_CHIPS)]
    small = {n: jnp.concatenate([per_chip[s][k] for s in range(N_CHIPS)], axis=-1)
             for k, n in enumerate(SHARDED_SMALL)}
    small.update({n: a[n] for n in REPL_SMALL})

    groups = []
    for l in range(4):
        p, j = ("a", l) if l < 2 else ("b", l - 2)
        groups.append([(p + "_" + n, j) for n in ("w_in", "w_mem_kv", "w_out")])
        groups.append([(p + "_" + n, j) for n in ("w_ffn_in", "w_ffn_out")])
    groups[3].append(("w_kv_shared", None))
    big = {n: [None, None] for n in BIG if n != "w_kv_shared"}
    gs, gb = {}, {}
    reduced = {n: [None, None] for n in BIG if n != "w_kv_shared"}

    def put(store, n, j, val):
        if j is None:
            store[n] = val
        else:
            store[n][j] = val

    class Exchange:
        def __init__(self):
            self.state = {}

        def gather_ici(self, g):
            shards = [(a[n] if j is None else a[n][j]).astype(MXU) for n, j in groups[g]]
            rows = [sh.shape[0] for sh in shards]
            lands = [lax.empty((N_CHIPS,) + sh.shape, sh.dtype) for sh in shards]
            plan = plan_gather_ici(len(shards), rows)
            ss, rs, bufs, tok = split_start("gather_ici_%d" % g, shards + lands, plan, 4 * len(shards))
            self.state["g", g] = (ss, rs, bufs, plan, rows)
            return tok

        def gather_d2d(self, g, after):
            ss, rs, bufs, plan, rows = self.state.pop(("g", g))
            n = len(rows)
            outs = split_wait("gather_ici_wait_%d" % g, ss, rs, bufs, after, plan)[n:]
            plan = plan_gather_d2d(n, rows)
            ss, rs, bufs, tok = split_start("gather_d2d_%d" % g, outs, plan, 3 * n)
            self.state["g", g] = (ss, rs, bufs, plan)
            return tok

        def gather_done(self, g, after):
            ss, rs, bufs, plan = self.state.pop(("g", g))
            outs = split_wait("gather_d2d_wait_%d" % g, ss, rs, bufs, after, plan)
            for (n, j), w in zip(groups[g], outs):
                put(big, n, j, w if BIG[n] else w.reshape(-1, w.shape[-1]))

        def rs_swap(self, g):
            grads = []
            for n, j in groups[g]:
                gr = gb[n] if j is None else gb[n][j]
                grads.append(gr if BIG[n] else gr.reshape(N_CHIPS, gr.shape[0] // N_CHIPS, gr.shape[1]))
            rows = [gr.shape[1] for gr in grads]
            lands = [lax.empty((N_CHIPS, gr.shape[1] // 2, gr.shape[2]), gr.dtype) for gr in grads]
            plan = plan_swap(len(grads), rows)
            ss, rs, bufs, tok = split_start("rs_swap_%d" % g, grads + lands, plan, len(grads))
            self.state["r", g] = (ss, rs, bufs, plan, rows)
            return tok

        def rs_exchange(self, g, after):
            ss, rs, bufs, plan, rows = self.state.pop(("r", g))
            n = len(rows)
            bufs = split_wait("rs_swap_wait_%d" % g, ss, rs, bufs, after, plan)
            sums = [half_sum(gr, got, "rs_half_sum") for gr, got in zip(bufs[:n], bufs[n:])]
            plan = plan_exchange(n)
            ss, rs, bufs, tok = split_start("rs_exchange_%d" % g, [p for p, _ in sums] + [s for _, s in sums], plan,
                                            3 * n)
            self.state["r", g] = (ss, rs, bufs, plan, rows)
            return tok

        def rs_share(self, g, after):
            ss, rs, bufs, plan, rows = self.state.pop(("r", g))
            n = len(rows)
            slots = split_wait("rs_exchange_wait_%d" % g, ss, rs, bufs, after, plan)[n:]
            fulls = [slot_sum(s, "rs_slot_sum") for s in slots]
            plan = plan_share(n, rows)
            ss, rs, bufs, tok = split_start("rs_share_%d" % g, fulls, plan, n)
            self.state["r", g] = (ss, rs, bufs, plan)
            return tok

        def rs_done(self, g, after):
            ss, rs, bufs, plan = self.state.pop(("r", g))
            outs = split_wait("rs_share_wait_%d" % g, ss, rs, bufs, after, plan)
            for (n, j), r in zip(groups[g], outs):
                put(reduced, n, j, r)

        def hook(self, where, l, after):
            mix, ffn = 2 * l, 2 * l + 1
            toks = []
            if where == "fwd_begin":
                if l == 0:
                    tok = self.gather_ici(mix)
                    tok = self.gather_d2d(mix, tok)
                    self.gather_done(mix, tok)
                toks.append(self.gather_ici(ffn))
            elif where == "fwd_q1":
                toks.append(self.gather_d2d(ffn, after))
            elif where == "fwd_mid":
                self.gather_done(ffn, after)
                if l < 3:
                    toks.append(self.gather_ici(mix + 2))
            elif where == "fwd_q3":
                if l < 3:
                    toks.append(self.gather_d2d(mix + 2, after))
            elif where == "fwd_end":
                if l < 3:
                    self.gather_done(mix + 2, after)
            elif where == "bwd_begin":
                if l < 3:
                    self.rs_done(ffn + 2, after)
                    toks.append(self.rs_exchange(mix + 2, after))
            elif where == "bwd_mid1":
                if l < 3:
                    toks.append(self.rs_share(mix + 2, after))
            elif where == "bwd_mid2":
                if l < 3:
                    self.rs_done(mix + 2, after)
                toks.append(self.rs_swap(ffn))
            elif where == "bwd_m1":
                toks.append(self.rs_exchange(ffn, after))
            elif where == "bwd_end":
                toks.append(self.rs_share(ffn, after))
                toks.append(self.rs_swap(mix))
                if l == 0:
                    self.rs_done(ffn, toks[0])
                    tok = self.rs_exchange(mix, toks[1])
                    tok = self.rs_share(mix, adamw_big([n for n in BIG if n.startswith("b_")], tok))
                    self.rs_done(mix, tok)
                    toks = []
            else:
                raise ValueError(where)
            return toks

    res = {}

    def adamw_big(names, token=None):
        last = None
        for n in names:
            shape = a[n].shape
            rows, cols = shape[-2], shape[-1]
            stk = (-1, rows, cols)
            grads = reduced[n] if isinstance(reduced[n], list) else [reduced[n]]
            if token is not None:
                grads = [tie(grads[0], [token], "tie_adamw_" + n)] + grads[1:]
            outs = adamw(a[n].reshape(stk), grads, a["m_" + n].reshape(stk), a["v_" + n].reshape(stk), "adamw")
            res[n] = [o.reshape(shape) for o in outs]
            last = outs[1]
            token = last if token is not None else None
        return last

    loss_parts, dx = _fwd_bwd(x[0], mem[0], loss_target[0], small, big, gs, gb, Exchange())
    loss = lax.psum(jnp.sum(loss_parts) * (0.5 / D), ("x", "y", "c"))

    def full(n):
        g = gs[n]
        return jnp.stack(g) if isinstance(g, list) else g

    order = SHARDED_SMALL + REPL_SMALL
    full_shapes = [full(n).shape for n in order]
    pack = _pack([full(n) for n in order], 512)
    ss, rs, bufs, tok = split_start("small_swap", [pack, lax.empty(pack.shape, F32)], plan_small_swap, 1)
    mine_v, sib_v = split_wait("small_swap_wait", ss, rs, bufs, tok, plan_small_swap)
    ss, rs, bufs, tok = split_start("small_exchange", [small_pair(mine_v, sib_v, "small_pair")],
                                    plan_small_exchange, 3)
    last = adamw_big([n for n in BIG if n not in res], tok)
    slots = split_wait("small_exchange_wait", ss, rs, bufs, last, plan_small_exchange)[0]
    summed = _unpack(small_total(slots, "small_total"), full_shapes)
    mine = []
    for n, g in zip(order, summed):
        if n in SHARDED_SMALL:
            width = a[n].shape[-1]
            g = lax.dynamic_slice_in_dim(g, chip * width, width, axis=g.ndim - 1)
        mine.append(g.reshape(a[n].shape))
    shapes = [a[n].shape for n in order]
    rm = 512
    outs = adamw(_pack([a[n] for n in order], rm)[None], [_pack(mine, rm)],
                 _pack([a["m_" + n] for n in order], rm)[None], _pack([a["v_" + n] for n in order], rm)[None],
                 "adamw_small")
    unpacked = [_unpack(o[0], shapes) for o in outs]
    for k, n in enumerate(order):
        res[n] = [u[k] for u in unpacked]

    return (loss, dx[None], *[res[n][0] for n in WEIGHTS], *[res[n][1] for n in WEIGHTS],
            *[res[n][2] for n in WEIGHTS], *[res[n][3] for n in WEIGHTS])
```

```python
import math

import jax
import jax.numpy as jnp
from jax import lax
from jax.experimental import pallas as pl
from jax.experimental.pallas import tpu as pltpu

D = 2048
HD = 128
MEM_W = 512
MEM_HEADS = 4
MIX_W = D - MEM_W
N_BLK = MIX_W // HD
D_FF = 5632
N_MEM = 256
RMS_EPS = 1e-6
NEG_INF = -1e30
LRU_C = 8.0
DIL_GROUPS = ((128, 1), (512, 4), (2048, 16))
Q_BLOCK = 128
SCALE = HD ** -0.5
N_CHIPS = 4
MXU_COLS = 256
ACC_CHUNK = 2 * MXU_COLS

ADAM_LR = 0.001
ADAM_B1 = 0.9
ADAM_B2 = 0.999
ADAM_EPS = 1e-08
ADAM_WD = 0.01
ADAM_STEP = 10

MXU = jnp.bfloat16
F32 = jnp.float32
VMEM_LIMIT_BYTES = 56 * 1024 * 1024

BS = pl.BlockSpec
SDS = jax.ShapeDtypeStruct
MESH = pl.DeviceIdType.MESH


def _cp(*sem):
    return pltpu.CompilerParams(dimension_semantics=sem or None, vmem_limit_bytes=VMEM_LIMIT_BYTES)


def _dot(a, b, dn=((1,), (0,))):
    return lax.dot_general(a, b, (dn, ((), ())), preferred_element_type=F32)


def _div(i, n):
    return lax.div(i, jnp.int32(n))


def _rem(i, n):
    return lax.rem(i, jnp.int32(n))


NN = ((1,), (0,))
NT = ((1,), (1,))
TN = ((0,), (0,))


def _sigmoid(z):
    return 0.5 * jnp.tanh(0.5 * z) + 0.5


def _log1p_pos(u):
    return jnp.where(u < 1e-2, u * (1.0 - u * (0.5 - u * (1.0 / 3.0))), jnp.log(1.0 + u))


def _neg_expm1(z):
    return jnp.where(z > -1e-2, -z * (1.0 + z * (0.5 + z * (1.0 / 6.0))), 1.0 - jnp.exp(z))


def _softplus(z):
    return jnp.maximum(z, 0.0) + _log1p_pos(jnp.exp(-jnp.abs(z)))


_GELU_C = math.sqrt(2.0 / math.pi)


def _gelu_and_grad(x):
    x2 = x * x
    t = jnp.tanh(_GELU_C * (x + 0.044715 * x * x2))
    g = 0.5 * x * (1.0 + t)
    dg = 0.5 * (1.0 + t) + 0.5 * x * (1.0 - t * t) * _GELU_C * (1.0 + 3.0 * 0.044715 * x2)
    return g, dg


def _row_tile(rows):
    return min(512, rows)


def norm_cast(x, g, name):
    rows = x.shape[0]
    tr = _row_tile(rows)

    def body(x_ref, g_ref, o_ref):
        xv = x_ref[...]
        r = lax.rsqrt(jnp.mean(xv * xv, axis=-1, keepdims=True) + RMS_EPS)
        o_ref[...] = (xv * r * g_ref[...]).astype(o_ref.dtype)

    return pl.pallas_call(
        body, grid=(rows // tr,),
        in_specs=[BS((tr, D), lambda i: (i, 0)), BS((1, D), lambda i: (0, 0))],
        out_specs=BS((tr, D), lambda i: (i, 0)),
        out_shape=SDS((rows, D), MXU), compiler_params=_cp("parallel"), name=name,
    )(x, g.reshape(1, D))


def resid_norm(h, y, g, name):
    rows = h.shape[0]
    tr = _row_tile(rows)

    def body(h_ref, y_ref, g_ref, o_ref):
        yv = y_ref[...]
        r = lax.rsqrt(jnp.mean(yv * yv, axis=-1, keepdims=True) + RMS_EPS)
        o_ref[...] = h_ref[...] + yv * r * g_ref[...]

    return pl.pallas_call(
        body, grid=(rows // tr,),
        in_specs=[BS((tr, D), lambda i: (i, 0)), BS((tr, D), lambda i: (i, 0)), BS((1, D), lambda i: (0, 0))],
        out_specs=BS((tr, D), lambda i: (i, 0)),
        out_shape=SDS((rows, D), F32), compiler_params=_cp("parallel"), name=name,
    )(h, y, g.reshape(1, D))


def resid_norm_next(h, y, g, g_next, name):
    rows = h.shape[0]
    tr = _row_tile(rows)

    def body(h_ref, y_ref, g_ref, gn_ref, o_ref, n_ref):
        yv = y_ref[...]
        r = lax.rsqrt(jnp.mean(yv * yv, axis=-1, keepdims=True) + RMS_EPS)
        hv = h_ref[...] + yv * r * g_ref[...]
        o_ref[...] = hv
        r2 = lax.rsqrt(jnp.mean(hv * hv, axis=-1, keepdims=True) + RMS_EPS)
        n_ref[...] = (hv * r2 * gn_ref[...]).astype(n_ref.dtype)

    row = BS((tr, D), lambda i: (i, 0))
    vec = BS((1, D), lambda i: (0, 0))
    return pl.pallas_call(
        body, grid=(rows // tr,), in_specs=[row, row, vec, vec], out_specs=[row, row],
        out_shape=[SDS((rows, D), F32), SDS((rows, D), MXU)], compiler_params=_cp("parallel"), name=name,
    )(h, y, g.reshape(1, D), g_next.reshape(1, D))


def _norm_bwd_rows(xv, gv, dyv):
    r = lax.rsqrt(jnp.mean(xv * xv, axis=-1, keepdims=True) + RMS_EPS)
    xhat = xv * r
    dxhat = dyv * gv
    dx = r * (dxhat - xhat * jnp.mean(dxhat * xhat, axis=-1, keepdims=True))
    return dx, jnp.sum(dyv * xhat, axis=0, keepdims=True)


def norm_bwd(x, g, dy, res, out_dtype, name, then=None):
    rows = x.shape[0]
    tr = _row_tile(rows)
    has_res = res is not None
    n_in = 3 + has_res + (2 if then else 0)

    def body(*refs):
        x_ref, g_ref, dy_ref = refs[:3]
        dx_ref, dg_ref = refs[n_in], refs[n_in + 1]
        dx, dg = _norm_bwd_rows(x_ref[...], g_ref[...], dy_ref[...].astype(F32))
        if has_res:
            dx = dx + refs[3][...]
        dx_ref[...] = dx.astype(dx_ref.dtype)
        first = pl.program_id(0) == 0

        @pl.when(first)
        def _():
            dg_ref[...] = jnp.zeros_like(dg_ref)

        dg_ref[...] += dg
        if then:
            x2_ref, g2_ref = refs[n_in - 2], refs[n_in - 1]
            dx2_ref, dg2_ref = refs[n_in + 2], refs[n_in + 3]
            dx2, dg2 = _norm_bwd_rows(x2_ref[...], g2_ref[...], dx)
            dx2_ref[...] = dx2.astype(dx2_ref.dtype)

            @pl.when(first)
            def _():
                dg2_ref[...] = jnp.zeros_like(dg2_ref)

            dg2_ref[...] += dg2

    row = BS((tr, D), lambda i: (i, 0))
    vec = BS((1, D), lambda i: (0, 0))
    ins = [x, g.reshape(1, D), dy] + ([res] if has_res else []) + ([then[0], then[1].reshape(1, D)] if then else [])
    outs = pl.pallas_call(
        body, grid=(rows // tr,),
        in_specs=[row, vec, row] + ([row] if has_res else []) + ([row, vec] if then else []),
        out_specs=[row, vec] + ([row, vec] if then else []),
        out_shape=[SDS((rows, D), out_dtype), SDS((1, D), F32)] + ([SDS((rows, D), MXU), SDS((1, D), F32)] if then else []),
        compiler_params=_cp("arbitrary"), name=name,
    )(*ins)
    if then:
        return outs[0], outs[1].reshape(D), outs[2], outs[3].reshape(D)
    return outs[0], outs[1].reshape(D)


def loss_head(y, target, name):
    rows = y.shape[0]
    tr = _row_tile(rows)

    def body(y_ref, t_ref, dy_ref, acc_ref):
        err = y_ref[...] - t_ref[...]
        dy_ref[...] = err * (1.0 / D)

        @pl.when(pl.program_id(0) == 0)
        def _():
            acc_ref[...] = jnp.zeros_like(acc_ref)

        acc_ref[...] += jnp.sum(err * err, axis=0, keepdims=True)

    row = BS((tr, D), lambda i: (i, 0))
    dy, acc = pl.pallas_call(
        body, grid=(rows // tr,), in_specs=[row, row],
        out_specs=[row, BS((1, D), lambda i: (0, 0))],
        out_shape=[SDS((rows, D), F32), SDS((1, D), F32)],
        compiler_params=_cp("arbitrary"), name=name,
    )(y, target)
    return acc, dy


def _mm_call(ins, in_specs, pick, dn, grid, o_spec, out_sds, name):
    gk = grid[2]
    n_in = len(ins)

    def body(*refs):
        o_ref = refs[n_in]
        k = pl.program_id(2)

        def step(a_ref, b_ref):
            acc = o_ref if (out_sds.dtype == F32 or gk == 1) else refs[n_in + 1]
            width = acc.shape[-1]
            if dn == TN or width <= ACC_CHUNK:
                chunks = [(0, width)]
            else:
                chunks = [(c0, min(c0 + ACC_CHUNK, width)) for c0 in range(0, width, ACC_CHUNK)]

            def sweep(first):
                a = a_ref[...]
                pending = None
                for c0, c1 in chunks:
                    p = _dot(a, b_ref[c0:c1, :] if dn == NT else b_ref[:, c0:c1], dn)
                    if pending is not None:
                        put(first, *pending)
                    pending = (c0, c1, p)
                put(first, *pending)

            def put(first, c0, c1, p):
                if first:
                    acc[:, c0:c1] = p.astype(acc.dtype)
                else:
                    acc[:, c0:c1] += p

            if gk == 1:
                sweep(True)
                return

            @pl.when(k == 0)
            def _():
                sweep(True)

            @pl.when(k > 0)
            def _():
                sweep(False)

            if acc is not o_ref:
                @pl.when(k == gk - 1)
                def _():
                    o_ref[...] = acc[...].astype(o_ref.dtype)

        pick(refs[:n_in], k, step)

    scratch = []
    if gk > 1 and out_sds.dtype != F32:
        scratch = [pltpu.VMEM(o_spec.block_shape[-2:], F32)]
    return pl.pallas_call(
        body, grid=grid, in_specs=in_specs, out_specs=o_spec, out_shape=out_sds,
        scratch_shapes=scratch, compiler_params=_cp("parallel", "parallel", "arbitrary"), name=name,
    )(*ins)


def _pick2(refs, k, step):
    step(refs[0], refs[1])


def mm_nn(a, w, *, tm, tn, tk, out_dtype, name):
    m, kdim = a.shape
    if w.ndim == 3:
        c = w.shape[2]
        n = N_CHIPS * c
        per = c // tn
        b_spec = BS((None, tk, tn), lambda i, j, k: (_div(j, per), k, _rem(j, per)))
    else:
        n = w.shape[1]
        b_spec = BS((tk, tn), lambda i, j, k: (k, j))
    grid = (m // tm, n // tn, kdim // tk)
    return _mm_call([a, w], [BS((tm, tk), lambda i, j, k: (i, k)), b_spec], _pick2, NN, grid,
                    BS((tm, tn), lambda i, j, k: (i, j)), SDS((m, n), out_dtype), name)


def mm_nt(a_list, w, *, tm, tn, tk, out_dtype, name):
    m = a_list[0].shape[0]
    ka = a_list[0].shape[1]
    n_a = len(a_list)
    kdim = ka * n_a
    if w.ndim == 3:
        c = w.shape[2]
        n = w.shape[1]
        per = c // tk
        b_spec = BS((None, tn, tk), lambda i, j, k: (_div(k, per), j, _rem(k, per)))
    else:
        n = w.shape[0]
        b_spec = BS((tn, tk), lambda i, j, k: (j, k))
    gk = kdim // tk
    half = gk // n_a
    grid = (m // tm, n // tn, gk)
    if n_a == 1:
        a_specs = [BS((tm, tk), lambda i, j, k: (i, k))]
        pick = lambda refs, k, step: step(refs[0], refs[1])
    else:
        a_specs = [BS((tm, tk), lambda i, j, k: (i, jnp.minimum(k, half - 1))),
                   BS((tm, tk), lambda i, j, k: (i, jnp.maximum(k - half, 0)))]

        def pick(refs, k, step):
            @pl.when(k < half)
            def _():
                step(refs[0], refs[2])

            @pl.when(k >= half)
            def _():
                step(refs[1], refs[2])

    return _mm_call(list(a_list) + [w], a_specs + [b_spec], pick, NT, grid,
                    BS((tm, tn), lambda i, j, k: (i, j)), SDS((m, n), out_dtype), name)


def mm_tn(a, b_list, *, t1, tn, ts, col_shards, name):
    s, k1 = a.shape
    nb = b_list[0].shape[1]
    n_b = len(b_list)
    n = nb * n_b
    gn = n // tn
    half = gn // n_b
    grid = (k1 // t1, gn, s // ts)
    if col_shards:
        c = n // N_CHIPS
        per = c // tn
        o_spec = BS((None, t1, tn), lambda i, j, k: (_div(j, per), i, _rem(j, per)))
        out_sds = SDS((N_CHIPS, k1, c), MXU)
    else:
        o_spec = BS((t1, tn), lambda i, j, k: (i, j))
        out_sds = SDS((k1, n), MXU)
    a_spec = BS((ts, t1), lambda i, j, k: (k, i))
    if n_b == 1:
        b_specs = [BS((ts, tn), lambda i, j, k: (k, j))]
        pick = lambda refs, k, step: step(refs[0], refs[1])
    else:
        b_specs = [BS((ts, tn), lambda i, j, k: (jnp.where(j < half, k, 0), jnp.minimum(j, half - 1))),
                   BS((ts, tn), lambda i, j, k: (jnp.where(j >= half, k, 0), jnp.maximum(j - half, 0)))]

        def pick(refs, k, step):
            j = pl.program_id(1)

            @pl.when(j < half)
            def _():
                step(refs[0], refs[1])

            @pl.when(j >= half)
            def _():
                step(refs[0], refs[2])

    return _mm_call([a] + list(b_list), [a_spec] + b_specs, pick, TN, grid, o_spec, out_sds, name)


def ffn_in_fwd(hn, w, name):
    s = hn.shape[0]
    tm = min(512, s)
    tn = D_FF // 4

    def tail(dag_ref, dau_ref, act_ref, c0, c1, g, u):
        sg = _sigmoid(g)
        silu = g * sg
        dag_ref[:, c0:c1] = (u * sg * (1.0 + g * (1.0 - sg))).astype(dag_ref.dtype)
        dau_ref[:, c0:c1] = silu.astype(dau_ref.dtype)
        act_ref[:, c0:c1] = (silu * u).astype(act_ref.dtype)

    def body(a_ref, wg_ref, wu_ref, dag_ref, dau_ref, act_ref):
        a = a_ref[...]
        pending = None
        for c0 in range(0, tn, ACC_CHUNK):
            c1 = min(c0 + ACC_CHUNK, tn)
            g = _dot(a, wg_ref[:, c0:c1])
            u = _dot(a, wu_ref[:, c0:c1])
            if pending is not None:
                tail(dag_ref, dau_ref, act_ref, *pending)
            pending = (c0, c1, g, u)
        tail(dag_ref, dau_ref, act_ref, *pending)

    tile = BS((tm, tn), lambda j, i: (i, j))
    return pl.pallas_call(
        body, grid=(4, s // tm),
        in_specs=[BS((tm, D), lambda j, i: (i, 0)),
                  BS((None, D, tn), lambda j, i: (_div(j, 2), 0, _rem(j, 2))),
                  BS((None, D, tn), lambda j, i: (2 + _div(j, 2), 0, _rem(j, 2)))],
        out_specs=[tile, tile, tile],
        out_shape=[SDS((s, D_FF), MXU), SDS((s, D_FF), MXU), SDS((s, D_FF), MXU)],
        compiler_params=_cp("parallel", "parallel"), name=name,
    )(hn, w, w)


def ffn_act_bwd(dy, w_out, dag, dau, name):
    s = dy.shape[0]
    tm = min(1024, s)
    tn = D_FF // 4

    def body(dy_ref, w_ref, dag_ref, dau_ref, dg_ref, du_ref):
        def tail(c0, c1, dact):
            dg_ref[:, c0:c1] = (dact * dag_ref[:, c0:c1].astype(F32)).astype(dg_ref.dtype)
            du_ref[:, c0:c1] = (dact * dau_ref[:, c0:c1].astype(F32)).astype(du_ref.dtype)

        dy = dy_ref[...]
        pending = None
        for c0 in range(0, tn, ACC_CHUNK):
            c1 = min(c0 + ACC_CHUNK, tn)
            dact = _dot(dy, w_ref[c0:c1, :], NT)
            if pending is not None:
                tail(*pending)
            pending = (c0, c1, dact)
        tail(*pending)

    tile = BS((tm, tn), lambda j, i: (i, j))
    return pl.pallas_call(
        body, grid=(4, s // tm),
        in_specs=[BS((tm, D), lambda j, i: (i, 0)), BS((tn, D), lambda j, i: (j, 0)), tile, tile],
        out_specs=[tile, tile],
        out_shape=[SDS((s, D_FF), MXU), SDS((s, D_FF), MXU)],
        compiler_params=_cp("parallel", "parallel"), name=name,
    )(dy, w_out, dag, dau)


LRU_T = 256
HALO = 8


def _shift_down(x, k, fill):
    rows = x.shape[0]
    idx = lax.broadcasted_iota(jnp.int32, x.shape, 0)
    return jnp.where(idx < k, fill, pltpu.roll(x, k, 0))


def _shift_up(x, k, fill):
    rows = x.shape[0]
    idx = lax.broadcasted_iota(jnp.int32, x.shape, 0)
    return jnp.where(idx >= rows - k, fill, pltpu.roll(x, rows - k, 0))


def _scan_block(a, b, carry, reverse):
    rows, cols = a.shape
    sub = 8
    in_group = lax.broadcasted_iota(jnp.int32, a.shape, 0) % sub
    for sh in (1, 2, 4):
        if reverse:
            a_s, b_s, ok = pltpu.roll(a, rows - sh, 0), pltpu.roll(b, rows - sh, 0), in_group < sub - sh
        else:
            a_s, b_s, ok = pltpu.roll(a, sh, 0), pltpu.roll(b, sh, 0), in_group >= sh
        b = jnp.where(ok, a * b_s + b, b)
        a = jnp.where(ok, a * a_s, a)
    groups = list(range(rows // sub))
    edge = 0 if reverse else sub - 1
    carry_in = {}
    for v in (reversed(groups) if reverse else groups):
        carry_in[v] = carry
        row = sub * v + edge
        carry = b[row:row + 1, :] + a[row:row + 1, :] * carry
    cin = jnp.concatenate([jnp.broadcast_to(carry_in[v], (sub, cols)) for v in groups], axis=0)
    return b + a * cin


def _conv_taps(xcat):
    rows = xcat.shape[0]
    taps = []
    for k in range(4):
        off = HALO - 3 + k
        taps.append(xcat[off:off + LRU_T] if off == HALO else pltpu.roll(xcat, rows - off, 0)[:LRU_T])
    return taps


def _gates(xc, wa_ref, ba, wx_ref, bx, lam, za_ref, zx_ref):
    xm = xc.astype(MXU)
    for n in range(N_BLK):
        sl = slice(n * HD, (n + 1) * HD)
        za_ref[:, sl] = _dot(xm[:, sl], wa_ref[n])
        zx_ref[:, sl] = _dot(xm[:, sl], wx_ref[n])
    ra = _sigmoid(za_ref[...] + ba)
    ii = _sigmoid(zx_ref[...] + bx)
    sp = _softplus(-lam)
    log_a = -LRU_C * ra * sp
    a = jnp.exp(log_a)
    mult = jnp.sqrt(_neg_expm1(2.0 * log_a))
    return ra, ii, sp, a, mult


def lru_fwd(proj, conv_w, conv_b, wa, ba, wx, bx, lam, name):
    s = proj.shape[0]
    c = MIX_W
    nblk = s // LRU_T
    hpb = LRU_T // HALO

    def body(x_ref, halo_ref, cw_ref, cb_ref, wa_ref, ba_ref, wx_ref, bx_ref, lam_ref,
             xc_ref, h_ref, carry, za_ref, zx_ref):
        i = pl.program_id(0)

        @pl.when(i == 0)
        def _():
            carry[...] = jnp.zeros_like(carry)

        halo = jnp.where(i == 0, 0.0, halo_ref[...])
        xcat = jnp.concatenate([halo, x_ref[...]], axis=0)
        taps = _conv_taps(xcat)
        xc = cb_ref[...] + sum(cw_ref[k:k + 1, :] * taps[k] for k in range(4))
        xc_ref[...] = xc
        _, ii, _, a, mult = _gates(xc, wa_ref, ba_ref[...], wx_ref, bx_ref[...], lam_ref[...], za_ref, zx_ref)
        h = _scan_block(a, mult * (ii * xc), carry[HALO - 1:HALO, :], False)
        h_ref[...] = h
        carry[...] = h[LRU_T - HALO:, :]

    def full(shape):
        return BS(shape, lambda i: (0,) * len(shape))

    blk = BS((LRU_T, c), lambda i: (i, 0))
    return pl.pallas_call(
        body, grid=(nblk,),
        in_specs=[blk, BS((HALO, c), lambda i: (jnp.maximum(i * hpb - 1, 0), 0)),
                  full((4, c)), full((1, c)), full((N_BLK, HD, HD)), full((1, c)),
                  full((N_BLK, HD, HD)), full((1, c)), full((1, c))],
        out_specs=[blk, blk],
        out_shape=[SDS((s, c), F32), SDS((s, c), F32)],
        scratch_shapes=[pltpu.VMEM((HALO, c), F32), pltpu.VMEM((LRU_T, c), F32), pltpu.VMEM((LRU_T, c), F32)],
        compiler_params=_cp("arbitrary"), name=name,
    )(proj, proj, conv_w, conv_b.reshape(1, c), wa.astype(MXU), ba.reshape(1, c), wx.astype(MXU),
      bx.reshape(1, c), lam.reshape(1, c))


def lru_mix_prep(h, proj, m, name):
    s = h.shape[0]
    tr = _row_tile(s)

    def body(h_ref, gb_ref, m_ref, o_ref):
        ge, _ = _gelu_and_grad(gb_ref[...])
        o_ref[:, :MIX_W] = (h_ref[...] * ge).astype(o_ref.dtype)
        o_ref[:, MIX_W:] = m_ref[...]

    return pl.pallas_call(
        body, grid=(s // tr,),
        in_specs=[BS((tr, MIX_W), lambda i: (i, 0)), BS((tr, MIX_W), lambda i: (i, 1)),
                  BS((tr, MEM_W), lambda i: (i, 0))],
        out_specs=BS((tr, D), lambda i: (i, 0)), out_shape=SDS((s, D), MXU),
        compiler_params=_cp("parallel"), name=name,
    )(h, proj, m)


def lru_bwd(dym, proj, xc, hl, dqm, conv_w, wa, ba, wx, bx, lam, name):
    s = proj.shape[0]
    c = MIX_W
    nblk = s // LRU_T
    hpb = LRU_T // HALO
    wa_m = wa.astype(MXU)
    wx_m = wx.astype(MXU)

    def body(dy_ref, x_ref, xhalo_ref, gb_ref, xc_ref, h_ref, hhalo_ref, dqm_ref,
             cw_ref, wa_ref, ba_ref, wx_ref, bx_ref, lam_ref,
             dproj_ref, dcw_ref, dcb_ref, dwa_ref, dba_ref, dwx_ref, dbx_ref, dlam_ref,
             g_next, a_next, dxc_next, za_ref, zx_ref, dxc_ref):
        i = pl.program_id(0)

        @pl.when(i == 0)
        def _():
            g_next[...] = jnp.zeros_like(g_next)
            a_next[...] = jnp.zeros_like(a_next)
            dxc_next[...] = jnp.zeros_like(dxc_next)
            for r in (dcw_ref, dcb_ref, dwa_ref, dba_ref, dwx_ref, dbx_ref, dlam_ref):
                r[...] = jnp.zeros_like(r)

        first = i == nblk - 1
        xc = xc_ref[...]
        lam = lam_ref[...]
        ra, ii, sp, a, mult = _gates(xc, wa_ref, ba_ref[...], wx_ref, bx_ref[...], lam, za_ref, zx_ref)
        hl_v = h_ref[...]
        ge, dge = _gelu_and_grad(gb_ref[...])
        dyl = dy_ref[...]
        dhl = dyl * ge
        dproj_ref[:, c:2 * c] = (dyl * hl_v * dge).astype(dproj_ref.dtype)
        dproj_ref[:, 2 * c:] = dqm_ref[...]

        an = _shift_up(a, 1, 0.0)
        last_row = lax.broadcasted_iota(jnp.int32, a.shape, 0) == LRU_T - 1
        an = jnp.where(last_row, a_next[0:1, :], an)
        g = _scan_block(an, dhl, g_next[0:1, :], True)
        g_next[...] = g[:HALO, :]
        a_next[...] = a[:HALO, :]

        hhalo = jnp.where(first, 0.0, hhalo_ref[...])
        h_prev = _shift_down(hl_v, 1, 0.0)
        first_row = lax.broadcasted_iota(jnp.int32, a.shape, 0) == 0
        h_prev = jnp.where(first_row, hhalo[HALO - 1:HALO, :], h_prev)
        da = g * h_prev
        ixc = ii * xc
        dmult = g * ixc
        dii = g * mult * xc
        dxc = g * mult * ii
        dlog_a = (da - dmult * a / mult) * a
        dra = dlog_a * (-LRU_C) * sp
        dlam_ref[...] += jnp.sum(dlog_a * ra, axis=0, keepdims=True) * (LRU_C * _sigmoid(-lam))
        dza = dra * ra * (1.0 - ra)
        dzx = dii * ii * (1.0 - ii)
        dba_ref[...] += jnp.sum(dza, axis=0, keepdims=True)
        dbx_ref[...] += jnp.sum(dzx, axis=0, keepdims=True)
        xm = xc.astype(MXU)
        dza_m = dza.astype(MXU)
        dzx_m = dzx.astype(MXU)
        for n in range(N_BLK):
            sl = slice(n * HD, (n + 1) * HD)
            dwa_ref[n] += _dot(xm[:, sl], dza_m[:, sl], TN)
            dwx_ref[n] += _dot(xm[:, sl], dzx_m[:, sl], TN)
            dxc_ref[:, sl] = _dot(dza_m[:, sl], wa_ref[n], NT) + _dot(dzx_m[:, sl], wx_ref[n], NT)
        dxc = dxc + dxc_ref[...]

        dcat = jnp.concatenate([dxc, dxc_next[...]], axis=0)
        rows = dcat.shape[0]
        dxb = cw_ref[3:4, :] * dxc
        for k in range(3):
            dxb = dxb + cw_ref[k:k + 1, :] * pltpu.roll(dcat, rows - (3 - k), 0)[:LRU_T]
        dproj_ref[:, :c] = dxb.astype(dproj_ref.dtype)
        dxc_next[...] = dxc[:HALO, :]

        xhalo = jnp.where(first, 0.0, xhalo_ref[...])
        taps = _conv_taps(jnp.concatenate([xhalo, x_ref[...]], axis=0))
        for k in range(4):
            dcw_ref[k:k + 1, :] += jnp.sum(dxc * taps[k], axis=0, keepdims=True)
        dcb_ref[...] += jnp.sum(dxc, axis=0, keepdims=True)

    def full(shape):
        return BS(shape, lambda i: (0,) * len(shape))

    def rev(i):
        return nblk - 1 - i

    blk0 = BS((LRU_T, c), lambda i: (rev(i), 0))
    blk1 = BS((LRU_T, c), lambda i: (rev(i), 1))
    halo = BS((HALO, c), lambda i: (jnp.maximum(rev(i) * hpb - 1, 0), 0))
    outs = pl.pallas_call(
        body, grid=(nblk,),
        in_specs=[blk0, blk0, halo, blk1, blk0, blk0, halo, BS((LRU_T, MEM_W), lambda i: (rev(i), 0)),
                  full((4, c)), full((N_BLK, HD, HD)), full((1, c)), full((N_BLK, HD, HD)), full((1, c)),
                  full((1, c))],
        out_specs=[BS((LRU_T, 2 * c + MEM_W), lambda i: (rev(i), 0)), full((4, c)), full((1, c)),
                   full((N_BLK, HD, HD)), full((1, c)), full((N_BLK, HD, HD)), full((1, c)), full((1, c))],
        out_shape=[SDS((s, 2 * c + MEM_W), MXU), SDS((4, c), F32), SDS((1, c), F32),
                   SDS((N_BLK, HD, HD), F32), SDS((1, c), F32), SDS((N_BLK, HD, HD), F32), SDS((1, c), F32),
                   SDS((1, c), F32)],
        scratch_shapes=[pltpu.VMEM((HALO, c), F32), pltpu.VMEM((HALO, c), F32), pltpu.VMEM((HALO, c), F32),
                        pltpu.VMEM((LRU_T, c), F32), pltpu.VMEM((LRU_T, c), F32), pltpu.VMEM((LRU_T, c), F32)],
        compiler_params=_cp("arbitrary"), name=name,
    )(dym, proj, proj, proj, xc, hl, hl, dqm, conv_w, wa_m, ba.reshape(1, c), wx_m, bx.reshape(1, c),
      lam.reshape(1, c))
    dproj, dcw, dcb, dwa, dba, dwx, dbx, dlam = outs
    return dproj, dcw, dcb.reshape(c), dwa, dba.reshape(c), dwx, dbx.reshape(c), dlam.reshape(c)


def _mem_probs(q, kv):
    heads = [slice(hh * HD, (hh + 1) * HD) for hh in range(MEM_HEADS)]
    sc = [_dot(q[:, sl], kv[:, sl], NT) * SCALE for sl in heads]
    e = [jnp.exp(s - jnp.max(s, axis=-1, keepdims=True)) for s in sc]
    return [x / jnp.sum(x, axis=-1, keepdims=True) for x in e]


def mem_attn_fwd(proj, q_col, kvm, name):
    s = proj.shape[0]
    tq = min(512, s)

    def body(q_ref, kv_ref, o_ref):
        q = q_ref[...].astype(MXU)
        kv = kv_ref[...]
        p = _mem_probs(q, kv)
        outs = [_dot(p[hh].astype(MXU), kv[:, MEM_W + hh * HD:MEM_W + (hh + 1) * HD]) for hh in range(MEM_HEADS)]
        o_ref[...] = jnp.concatenate(outs, axis=1).astype(o_ref.dtype)

    return pl.pallas_call(
        body, grid=(s // tq,),
        in_specs=[BS((tq, MEM_W), lambda i: (i, q_col)), BS((N_MEM, 2 * MEM_W), lambda i: (0, 0))],
        out_specs=BS((tq, MEM_W), lambda i: (i, 0)), out_shape=SDS((s, MEM_W), MXU),
        compiler_params=_cp("parallel"), name=name,
    )(proj, kvm)


def mem_attn_bwd(proj, q_col, kvm, dym, name):
    s = proj.shape[0]
    tq = min(512, s)

    def body(q_ref, kv_ref, do_ref, dq_ref, dkv_ref):
        @pl.when(pl.program_id(0) == 0)
        def _():
            dkv_ref[...] = jnp.zeros_like(dkv_ref)

        q = q_ref[...].astype(MXU)
        do = do_ref[...].astype(MXU)
        kv = kv_ref[...]
        heads = [slice(hh * HD, (hh + 1) * HD) for hh in range(MEM_HEADS)]
        p = _mem_probs(q, kv)
        dp = [_dot(do[:, sl], kv[:, MEM_W + hh * HD:MEM_W + (hh + 1) * HD], NT) for hh, sl in enumerate(heads)]
        ds = [(pp * (d - jnp.sum(pp * d, axis=-1, keepdims=True)) * SCALE).astype(MXU) for pp, d in zip(p, dp)]
        dq = [_dot(x, kv[:, sl]) for x, sl in zip(ds, heads)]
        dk = [_dot(x, q[:, sl], TN) for x, sl in zip(ds, heads)]
        dv = [_dot(pp.astype(MXU), do[:, sl], TN) for pp, sl in zip(p, heads)]
        dq_ref[...] = jnp.concatenate(dq, axis=1).astype(dq_ref.dtype)
        dkv_ref[...] += jnp.concatenate(dk + dv, axis=1)

    return pl.pallas_call(
        body, grid=(s // tq,),
        in_specs=[BS((tq, MEM_W), lambda i: (i, q_col)), BS((N_MEM, 2 * MEM_W), lambda i: (0, 0)),
                  BS((tq, MEM_W), lambda i: (i, MIX_W // MEM_W))],
        out_specs=[BS((tq, MEM_W), lambda i: (i, 0)), BS((N_MEM, 2 * MEM_W), lambda i: (0, 0))],
        out_shape=[SDS((s, MEM_W), MXU), SDS((N_MEM, 2 * MEM_W), F32)],
        compiler_params=_cp("arbitrary"), name=name,
    )(proj, kvm, dym)


def _score_prev(q, kp, n, slope_dil):
    qi = lax.broadcasted_iota(jnp.int32, (Q_BLOCK, Q_BLOCK), 0)
    ki = lax.broadcasted_iota(jnp.int32, (Q_BLOCK, Q_BLOCK), 1)
    rel = qi + Q_BLOCK - ki
    s = _dot(q, kp, NT) * SCALE - slope_dil * rel.astype(F32)
    return jnp.where((rel <= Q_BLOCK) & (n > 0), s, NEG_INF)


def _score_cur(q, kc, slope_dil):
    qi = lax.broadcasted_iota(jnp.int32, (Q_BLOCK, Q_BLOCK), 0)
    ki = lax.broadcasted_iota(jnp.int32, (Q_BLOCK, Q_BLOCK), 1)
    rel = qi - ki
    s = _dot(q, kc, NT) * SCALE - slope_dil * rel.astype(F32)
    return jnp.where(rel >= 0, s, NEG_INF)


def _dil_scores(q, kp, kc, n, slope_dil):
    return _score_prev(q, kp, n, slope_dil), _score_cur(q, kc, slope_dil)


def _slope_dil(gi, hh):
    head = 4 * gi + hh
    return DIL_GROUPS[gi][1] * 2.0 ** (-8.0 * (head + 1.0) / N_BLK)


def _dil_operands(proj, kv, gi):
    dil = DIL_GROUPS[gi][1]
    if dil == 1:
        return proj, kv, kv, (lambda r: gi), (lambda r: gi), (lambda r: MIX_W // MEM_W + gi)
    sub = proj.shape[0] // dil

    def view(a, col):
        return a[:, col:col + MEM_W].reshape(sub, dil * MEM_W)

    same = lambda r: r
    return view(proj, gi * MEM_W), view(kv, gi * MEM_W), view(kv, MIX_W + gi * MEM_W), same, same, same


def dil_attn_fwd(proj, kv, gi, name):
    dil = DIL_GROUPS[gi][1]
    s, pw = proj.shape
    sub = s // dil
    nb = sub // Q_BLOCK
    pair = 2 * Q_BLOCK

    def body(q_ref, kp_ref, kc_ref, vp_ref, vc_ref, o_ref, lse_ref):
        t = pl.program_id(1)
        q = q_ref[...].astype(MXU)
        k_prev, k_cur, v_prev, v_cur = kp_ref[...], kc_ref[...], vp_ref[...], vc_ref[...]
        units = []
        for b in range(2):
            rows = slice(b * Q_BLOCK, (b + 1) * Q_BLOCK)
            kp, vp = (k_prev, v_prev) if b == 0 else (k_cur[:Q_BLOCK], v_cur[:Q_BLOCK])
            for hh in range(4):
                sl = slice(hh * HD, (hh + 1) * HD)
                units.append((q[rows, sl], kp[:, sl], k_cur[rows, sl], vp[:, sl], v_cur[rows, sl], 2 * t + b, hh))
        sc = [_dil_scores(qh, kp, kc, n, _slope_dil(gi, hh)) for qh, kp, kc, _, _, n, hh in units]
        mx = [jnp.maximum(jnp.max(s_p, axis=-1, keepdims=True), jnp.max(s_c, axis=-1, keepdims=True))
              for s_p, s_c in sc]
        den = [jnp.sum(jnp.exp(s_p - m), axis=-1, keepdims=True) + jnp.sum(jnp.exp(s_c - m), axis=-1, keepdims=True)
               for (s_p, s_c), m in zip(sc, mx)]
        lse = [m + jnp.log(d) for m, d in zip(mx, den)]
        pr = [(jnp.exp(s_p - l).astype(MXU), jnp.exp(s_c - l).astype(MXU)) for (s_p, s_c), l in zip(sc, lse)]
        outs = [_dot(p_p, u[3]) + _dot(p_c, u[4]) for (p_p, p_c), u in zip(pr, units)]
        wide = [jnp.broadcast_to(l, (Q_BLOCK, HD)) for l in lse]
        o_ref[...] = jnp.concatenate([jnp.concatenate(outs[4 * b:4 * b + 4], axis=1) for b in range(2)], axis=0)
        lse_ref[...] = jnp.concatenate([jnp.concatenate(wide[4 * b:4 * b + 4], axis=1) for b in range(2)], axis=0)

    one, two = (Q_BLOCK, MEM_W), (pair, MEM_W)
    before = lambda t: jnp.maximum(2 * t - 1, 0)
    out = BS(two, lambda r, t: (t, r))
    qv, kview, vview, qcol, kcol, vcol = _dil_operands(proj, kv, gi)
    return pl.pallas_call(
        body, grid=(dil, nb // 2),
        in_specs=[BS(two, lambda r, t: (t, qcol(r))),
                  BS(one, lambda r, t: (before(t), kcol(r))), BS(two, lambda r, t: (t, kcol(r))),
                  BS(one, lambda r, t: (before(t), vcol(r))), BS(two, lambda r, t: (t, vcol(r)))],
        out_specs=[out, out],
        out_shape=[SDS((sub, dil * MEM_W), F32), SDS((sub, dil * MEM_W), F32)],
        compiler_params=_cp("parallel", "parallel"), name=name,
    )(qv, kview, kview, vview, vview)


def dil_attn_bwd(proj, kv, lse, do, dd, gi, name):
    dil = DIL_GROUPS[gi][1]
    sub = proj.shape[0] // dil
    nb = sub // Q_BLOCK
    pair = 2 * Q_BLOCK
    lo, hi = slice(0, Q_BLOCK), slice(Q_BLOCK, pair)

    def body(q_ref, qn_ref, kp_ref, kc_ref, vp_ref, vc_ref, lse_ref, lsen_ref, do_ref, don_ref, dd_ref, ddn_ref,
             dq_ref, dk_ref, dv_ref):
        t = pl.program_id(1)
        has_next = 2 * t + 2 < nb
        q, qn = q_ref[...].astype(MXU), qn_ref[...].astype(MXU)
        do_m, don_m = do_ref[...].astype(MXU), don_ref[...].astype(MXU)
        k_prev, k_cur, v_prev, v_cur = kp_ref[...], kc_ref[...], vp_ref[...], vc_ref[...]
        lse_v, lsen_v, dd_v, ddn_v = lse_ref[...], lsen_ref[...], dd_ref[...], ddn_ref[...]
        units = []
        for hh in range(4):
            sl = slice(hh * HD, (hh + 1) * HD)
            q0, q1, q2 = q[lo, sl], q[hi, sl], qn[:, sl]
            g0, g1, g2 = do_m[lo, sl], do_m[hi, sl], don_m[:, sl]
            l0, l1, l2 = lse_v[lo, sl], lse_v[hi, sl], lsen_v[:, sl]
            e0, e1, e2 = dd_v[lo, sl], dd_v[hi, sl], ddn_v[:, sl]
            ka, kb, kc = k_prev[:, sl], k_cur[lo, sl], k_cur[hi, sl]
            va, vb, vc = v_prev[:, sl], v_cur[lo, sl], v_cur[hi, sl]
            units += [(q0, g0, l0, e0, ka, va, 2 * t, None), (q0, g0, l0, e0, kb, vb, None, None),
                      (q1, g1, l1, e1, kb, vb, 2 * t + 1, None), (q1, g1, l1, e1, kc, vc, None, None),
                      (q2, g2, l2, e2, kc, vc, 2 * t + 2, has_next)]
        slopes = [_slope_dil(gi, hh) for hh in range(4) for _ in range(5)]
        sc = [_score_cur(u[0], u[4], sp) if u[6] is None else _score_prev(u[0], u[4], u[6], sp)
              for u, sp in zip(units, slopes)]
        dp = [_dot(u[1], u[5], NT) for u in units]
        pr = [jnp.exp(s - u[2]) if u[7] is None else jnp.where(u[7], jnp.exp(s - u[2]), 0.0) for s, u in zip(sc, units)]
        ds = [(p * (d + u[3]) * SCALE).astype(MXU) for p, d, u in zip(pr, dp, units)]
        pm = [p.astype(MXU) for p in pr]
        dq, dk, dv = [[], []], [[], []], [[], []]
        for hh in range(4):
            a, b, c, d, e = range(5 * hh, 5 * hh + 5)
            dq[0].append(_dot(ds[a], units[a][4]) + _dot(ds[b], units[b][4]))
            dq[1].append(_dot(ds[c], units[c][4]) + _dot(ds[d], units[d][4]))
        for hh in range(4):
            a, b, c, d, e = range(5 * hh, 5 * hh + 5)
            dk[0].append(_dot(ds[b], units[b][0], TN) + _dot(ds[c], units[c][0], TN))
            dk[1].append(_dot(ds[d], units[d][0], TN) + _dot(ds[e], units[e][0], TN))
            dv[0].append(_dot(pm[b], units[b][1], TN) + _dot(pm[c], units[c][1], TN))
            dv[1].append(_dot(pm[d], units[d][1], TN) + _dot(pm[e], units[e][1], TN))

        def tile(parts):
            return jnp.concatenate([jnp.concatenate(parts[0], axis=1), jnp.concatenate(parts[1], axis=1)], axis=0)

        dq_ref[...] = tile(dq).astype(dq_ref.dtype)
        dk_ref[...] = tile(dk)
        dv_ref[...] = tile(dv)

    one, two = (Q_BLOCK, MEM_W), (pair, MEM_W)
    before = lambda t: jnp.maximum(2 * t - 1, 0)
    after = lambda t: jnp.minimum(2 * t + 2, nb - 1)
    own, nxt = BS(two, lambda r, t: (t, r)), BS(one, lambda r, t: (after(t), r))
    qv, kview, vview, qcol, kcol, vcol = _dil_operands(proj, kv, gi)
    return pl.pallas_call(
        body, grid=(dil, nb // 2),
        in_specs=[BS(two, lambda r, t: (t, qcol(r))), BS(one, lambda r, t: (after(t), qcol(r))),
                  BS(one, lambda r, t: (before(t), kcol(r))), BS(two, lambda r, t: (t, kcol(r))),
                  BS(one, lambda r, t: (before(t), vcol(r))), BS(two, lambda r, t: (t, vcol(r))),
                  own, nxt, own, nxt, own, nxt],
        out_specs=[own, own, own],
        out_shape=[SDS((sub, dil * MEM_W), MXU), SDS((sub, dil * MEM_W), F32), SDS((sub, dil * MEM_W), F32)],
        compiler_params=_cp("parallel", "parallel"), name=name,
    )(qv, qv, kview, kview, vview, vview, lse, lse, do, do, dd, dd)


def _group_weights(lse_refs):
    l0, l1, l2 = (r[...] for r in lse_refs)
    mx = jnp.maximum(jnp.maximum(l0, l1), l2)
    e = [jnp.exp(l - mx) for l in (l0, l1, l2)]
    den = e[0] + e[1] + e[2]
    return [x / den for x in e]


def dil_mix_prep(o_list, lse_list, m, name):
    s = m.shape[0]
    tr = _row_tile(s)

    def body(o0, o1, o2, l0, l1, l2, m_ref, out_ref):
        w = _group_weights((l0, l1, l2))
        for g, o_ref in enumerate((o0, o1, o2)):
            out_ref[:, g * MEM_W:(g + 1) * MEM_W] = (o_ref[...] * w[g]).astype(out_ref.dtype)
        out_ref[:, MIX_W:] = m_ref[...]

    blk = BS((tr, MEM_W), lambda i: (i, 0))
    return pl.pallas_call(
        body, grid=(s // tr,), in_specs=[blk] * 7,
        out_specs=BS((tr, D), lambda i: (i, 0)), out_shape=SDS((s, D), MXU),
        compiler_params=_cp("parallel"), name=name,
    )(*o_list, *lse_list, m)


def dil_mix_bwd(dym, o_list, lse_list, name):
    s = dym.shape[0]
    tr = _row_tile(s)

    def body(da_ref, o0, o1, o2, l0, l1, l2, do0, do1, do2, dd0, dd1, dd2):
        w = _group_weights((l0, l1, l2))
        tot = None
        for g, (o_ref, do_ref) in enumerate(zip((o0, o1, o2), (do0, do1, do2))):
            da = da_ref[:, g * MEM_W:(g + 1) * MEM_W]
            do_ref[...] = da * w[g]
            x = da * o_ref[...]
            dw = jnp.concatenate(
                [jnp.broadcast_to(jnp.sum(x[:, hh * HD:(hh + 1) * HD], axis=-1, keepdims=True), (tr, HD))
                 for hh in range(4)], axis=1)
            tot = w[g] * dw if tot is None else tot + w[g] * dw
        for g, dd_ref in enumerate((dd0, dd1, dd2)):
            dd_ref[...] = -w[g] * tot

    blk = BS((tr, MEM_W), lambda i: (i, 0))
    outs = pl.pallas_call(
        body, grid=(s // tr,), in_specs=[BS((tr, MIX_W), lambda i: (i, 0))] + [blk] * 6,
        out_specs=[blk] * 6, out_shape=[SDS((s, MEM_W), F32)] * 6,
        compiler_params=_cp("parallel"), name=name,
    )(dym, *o_list, *lse_list)
    return outs[:3], outs[3:]


def sum_cast(parts, name):
    s = parts[0][0].shape[0]
    tr = _row_tile(s)
    flat = [a for p in parts for a in p]
    sizes = [len(p) for p in parts]

    def body(*refs):
        out_ref = refs[-1]
        pos = 0
        for j, n in enumerate(sizes):
            acc = refs[pos][...].astype(F32)
            for t in range(1, n):
                acc = acc + refs[pos + t][...].astype(F32)
            out_ref[:, j * MEM_W:(j + 1) * MEM_W] = acc.astype(out_ref.dtype)
            pos += n

    blk = BS((tr, MEM_W), lambda i: (i, 0))
    width = MEM_W * len(parts)
    return pl.pallas_call(
        body, grid=(s // tr,), in_specs=[blk] * len(flat),
        out_specs=BS((tr, width), lambda i: (i, 0)), out_shape=SDS((s, width), MXU),
        compiler_params=_cp("parallel"), name=name,
    )(*flat)


def add_n(arrs, name):
    rows, cols = arrs[0].shape
    tr = _row_tile(rows)

    def body(*refs):
        acc = refs[0][...]
        for r in refs[1:-1]:
            acc = acc + r[...]
        refs[-1][...] = acc

    blk = BS((tr, cols), lambda i: (i, 0))
    return pl.pallas_call(
        body, grid=(rows // tr,), in_specs=[blk] * len(arrs), out_specs=blk,
        out_shape=SDS((rows, cols), F32), compiler_params=_cp("parallel"), name=name,
    )(*arrs)


class _NoExchange:
    def hook(self, where, l, after):
        return []


def _fwd_bwd(x, mem, target, small, big, gs, gb, sched):
    s = x.shape[0]
    tm = min(1024, s)
    ts = min(2048, s)

    def after_hook(arr, where, l, after):
        toks = sched.hook(where, l, after)
        return tie(arr, toks, "tie_%s_%d" % (where, l)) if toks else arr

    h = x
    saved = []
    kv = None
    mem_n = None
    hn = norm_cast(h, small["a_pre_mix_g"][0], "pre_norm")
    for l in range(4):
        rec = l < 2
        p, j = ("a", l) if rec else ("b", l - 2)
        sv = {"h": h}
        hn = after_hook(hn, "fwd_begin", l, h)
        if mem_n is None:
            mem_n = norm_cast(mem, small["mem_norm_g"], "mem_norm")
        kvm = mm_nn(mem_n, big[p + "_w_mem_kv"][j], tm=N_MEM, tn=2 * MEM_W, tk=D, out_dtype=MXU, name="mem_kv")
        if rec:
            proj = mm_nn(hn, big["a_w_in"][j], tm=min(2 * tm, s), tn=896, tk=D, out_dtype=F32, name="rec_in")
            xc, hl = lru_fwd(proj, small["a_conv_w"][j], small["a_conv_b"][j], small["a_gate_a_w"][j],
                             small["a_gate_a_b"][j], small["a_gate_x_w"][j], small["a_gate_x_b"][j],
                             small["a_lambda"][j], "lru_fwd")
            m = mem_attn_fwd(proj, 2 * MIX_W // MEM_W, kvm, "rec_mem_attn")
            ym = lru_mix_prep(hl, proj, m, "lru_mix_prep")
            sv.update(xc=xc, hl=hl)
        else:
            proj = mm_nn(hn, big["b_w_in"][j], tm=tm, tn=D, tk=D, out_dtype=F32, name="dil_in")
            o_list, lse_list = [], []
            for gi in range(3):
                o, lse = dil_attn_fwd(proj, kv, gi, "dil_attn_fwd%d" % gi)
                o_list.append(o.reshape(s, MEM_W))
                lse_list.append(lse.reshape(s, MEM_W))
            m = mem_attn_fwd(proj, MIX_W // MEM_W, kvm, "dil_mem_attn")
            ym = dil_mix_prep(o_list, lse_list, m, "dil_mix_prep")
            sv.update(o=o_list, lse=lse_list)
        ym = after_hook(ym, "fwd_q1", l, ym)
        mix = mm_nn(ym, big[p + "_w_out"][j], tm=tm, tn=D, tk=D, out_dtype=F32, name="mix_out")
        h1, hn2 = resid_norm_next(h, mix, small[p + "_post_mix_g"][j], small[p + "_pre_ffn_g"][j], "post_pre_norm")
        hn2 = after_hook(hn2, "fwd_mid", l, mix)
        g, u, act = ffn_in_fwd(hn2, big[p + "_w_ffn_in"][j], "ffn_in")
        act = after_hook(act, "fwd_q3", l, u)
        y2 = mm_nn(act, big[p + "_w_ffn_out"][j], tm=tm // 2, tn=D, tk=D_FF // 2, out_dtype=F32, name="ffn_out")
        sv.update(kvm=kvm, hn=hn, proj=proj, ym=ym, mix=mix, h1=h1, hn2=hn2, g=g, u=u, act=act, y2=y2)
        saved.append(sv)
        if l < 3:
            pn, jn = ("a", l + 1) if l + 1 < 2 else ("b", l - 1)
            h, hn = resid_norm_next(h1, y2, small[p + "_post_ffn_g"][j], small[pn + "_pre_mix_g"][jn],
                                    "post_pre_norm")
        else:
            h = resid_norm(h1, y2, small[p + "_post_ffn_g"][j], "post_norm")
        sched.hook("fwd_end", l, h)
        if l == 1:
            h_kv = h
            kvn = norm_cast(h, small["kv_norm_g"], "pre_norm")
            kv = mm_nn(kvn, big["w_kv_shared"], tm=tm, tn=768, tk=D, out_dtype=MXU, name="kv_proj")

    loss_parts, dh = loss_head(h, target, "loss_head")

    def stack2(name, j, val):
        gs.setdefault(name, [None, None])[j] = val

    def stack2b(name, j, val):
        gb.setdefault(name, [None, None])[j] = val

    dkv_parts = []
    ahead = []
    dmem_parts = []
    dkvm = [None] * 4
    for l in (3, 2, 1, 0):
        rec = l < 2
        p, j = ("a", l) if rec else ("b", l - 2)
        sv = saved[l]
        if l == 1:
            dkv = sum_cast([(dkv_parts[0][c], dkv_parts[1][c]) for c in range(6)], "dkv_sum")
            dkvn = mm_nt([dkv], big["w_kv_shared"], tm=tm, tn=D, tk=768, out_dtype=MXU, name="kv_proj_dx")
            gb["w_kv_shared"] = mm_tn(kvn, [dkv], t1=D, tn=768, ts=ts, col_shards=True, name="kv_proj_dw")
            dh, gs["kv_norm_g"], *ahead = norm_bwd(h_kv, small["kv_norm_g"], dkvn, dh, F32, "pre_post_norm_bwd",
                                                   then=(sv["y2"], small["a_post_ffn_g"][1]))
        if ahead:
            dy2, dg = ahead
            ahead = []
        else:
            dy2, dg = norm_bwd(sv["y2"], small[p + "_post_ffn_g"][j], dh, None, MXU, "post_norm_bwd")
        dy2 = after_hook(dy2, "bwd_begin", l, dh)
        stack2(p + "_post_ffn_g", j, dg)
        dgg, dgu = ffn_act_bwd(dy2, big[p + "_w_ffn_out"][j], sv["g"], sv["u"], "ffn_act_bwd")
        dgg = after_hook(dgg, "bwd_mid1", l, dgu)
        stack2b(p + "_w_ffn_out", j, mm_tn(sv["act"], [dy2], t1=D_FF // 4, tn=D, ts=ts // 2, col_shards=False,
                                          name="ffn_out_dw"))
        dhn2 = mm_nt([dgg, dgu], big[p + "_w_ffn_in"][j], tm=tm // 2, tn=D, tk=D_FF // 2, out_dtype=MXU,
                     name="ffn_in_dx")
        stack2b(p + "_w_ffn_in", j, mm_tn(sv["hn2"], [dgg, dgu], t1=D // 2, tn=D_FF // 4, ts=ts, col_shards=True,
                                         name="ffn_in_dw"))
        dhn2 = after_hook(dhn2, "bwd_mid2", l, gb[p + "_w_ffn_in"][j])
        dh1, dg, dmix, dg_mix = norm_bwd(sv["h1"], small[p + "_pre_ffn_g"][j], dhn2, dh, F32, "pre_post_norm_bwd",
                                         then=(sv["mix"], small[p + "_post_mix_g"][j]))
        stack2(p + "_pre_ffn_g", j, dg)
        stack2(p + "_post_mix_g", j, dg_mix)
        dym = mm_nt([dmix], big[p + "_w_out"][j], tm=tm, tn=D, tk=D, out_dtype=F32, name="mix_out_dx")
        stack2b(p + "_w_out", j, mm_tn(sv["ym"], [dmix], t1=D, tn=1024, ts=ts, col_shards=False,
                                      name="mix_out_dw"))
        dym = after_hook(dym, "bwd_m1", l, gb[p + "_w_out"][j])
        if rec:
            dqm, dkvm[l] = mem_attn_bwd(sv["proj"], 2 * MIX_W // MEM_W, sv["kvm"], dym, "rec_mem_attn_bwd")
            dproj, dcw, dcb, dwa, dba, dwx, dbx, dlam = lru_bwd(
                dym, sv["proj"], sv["xc"], sv["hl"], dqm, small["a_conv_w"][j], small["a_gate_a_w"][j],
                small["a_gate_a_b"][j], small["a_gate_x_w"][j], small["a_gate_x_b"][j], small["a_lambda"][j],
                "lru_bwd")
            for nm, val in (("a_conv_w", dcw), ("a_conv_b", dcb), ("a_gate_a_w", dwa), ("a_gate_a_b", dba),
                            ("a_gate_x_w", dwx), ("a_gate_x_b", dbx), ("a_lambda", dlam)):
                stack2(nm, j, val)
            dhn = mm_nt([dproj], big["a_w_in"][j], tm=tm, tn=D, tk=896, out_dtype=MXU, name="rec_in_dx")
            stack2b("a_w_in", j, mm_tn(sv["hn"], [dproj], t1=D, tn=896, ts=ts, col_shards=True, name="rec_in_dw"))
        else:
            dqm, dkvm[l] = mem_attn_bwd(sv["proj"], MIX_W // MEM_W, sv["kvm"], dym, "dil_mem_attn_bwd")
            do_list, dd_list = dil_mix_bwd(dym, sv["o"], sv["lse"], "dil_mix_bwd")
            dq_list, dk_list, dv_list = [], [], []
            for gi in range(3):
                dil = DIL_GROUPS[gi][1]
                view = (s // dil, dil * MEM_W)
                dq, dk, dv = dil_attn_bwd(sv["proj"], kv, sv["lse"][gi].reshape(view), do_list[gi].reshape(view),
                                          dd_list[gi].reshape(view), gi, "dil_attn_bwd%d" % gi)
                dq_list.append(dq.reshape(s, MEM_W))
                dk_list.append(dk.reshape(s, MEM_W))
                dv_list.append(dv.reshape(s, MEM_W))
            dkv_parts.append(dk_list + dv_list)
            dproj = sum_cast([(a,) for a in dq_list + [dqm]], "dil_dproj")
            dhn = mm_nt([dproj], big["b_w_in"][j], tm=tm, tn=D, tk=D, out_dtype=MXU, name="dil_in_dx")
            stack2b("b_w_in", j, mm_tn(sv["hn"], [dproj], t1=D, tn=1024, ts=ts, col_shards=False, name="dil_in_dw"))
        dk_m = dkvm[l].astype(MXU)
        dmem_parts.append(mm_nt([dk_m], big[p + "_w_mem_kv"][j], tm=N_MEM, tn=D, tk=2 * MEM_W, out_dtype=F32,
                                name="mem_kv_dx"))
        stack2b(p + "_w_mem_kv", j, mm_tn(mem_n, [dk_m], t1=D, tn=2 * MEM_W, ts=N_MEM, col_shards=False,
                                         name="mem_kv_dw"))
        if l in (3, 1):
            pn, jn = ("b", 0) if l == 3 else ("a", 0)
            dh, dg, *ahead = norm_bwd(sv["h"], small[p + "_pre_mix_g"][j], dhn, dh1, F32, "pre_post_norm_bwd",
                                      then=(saved[l - 1]["y2"], small[pn + "_post_ffn_g"][jn]))
        else:
            dh, dg = norm_bwd(sv["h"], small[p + "_pre_mix_g"][j], dhn, dh1, F32, "pre_norm_bwd")
        stack2(p + "_pre_mix_g", j, dg)
        dh = after_hook(dh, "bwd_end", l, dh)

    _, gs["mem_norm_g"] = norm_bwd(mem, small["mem_norm_g"], add_n(dmem_parts, "dmem_sum"), None, F32,
                                   "mem_norm_bwd")
    return loss_parts, dh


ANY = pl.BlockSpec(memory_space=pl.ANY)
CHIP_FLIPS = (1, 2, 3)


def _coords():
    return lax.axis_index("x"), lax.axis_index("y"), lax.axis_index("c")


def _flip(x, y, m):
    return x ^ (m >> 1), y ^ (m & 1)


def _remote(src, dst, send_sems, recv_sems, k, device):
    return pltpu.make_async_remote_copy(src_ref=src, dst_ref=dst, send_sem=send_sems.at[k], recv_sem=recv_sems.at[k],
                                        device_id=device, device_id_type=MESH)


def _sum_rows_tile(rows, cols, itemsize=4):
    for tr in (512, 256, 128, 64, 32, 16):
        if rows % tr == 0 and tr * cols * itemsize <= 2 * 1024 * 1024:
            return tr
    raise ValueError((rows, cols))


def half_sum(g, got, name):
    _, r, cols = g.shape
    hr = r // 2
    tr = _sum_rows_tile(hr, cols, g.dtype.itemsize)

    def my_chip():
        return 2 * lax.axis_index("x") + lax.axis_index("y")

    def body(g_ref, got_ref, o_ref, own_ref):
        p = (g_ref[...].astype(F32) + got_ref[...].astype(F32)).astype(o_ref.dtype)
        o_ref[...] = p

        @pl.when(pl.program_id(1) == my_chip())
        def _():
            own_ref[...] = p

    out = SDS((N_CHIPS, hr, cols), jnp.bfloat16)
    return pl.pallas_call(
        body, grid=(hr // tr, N_CHIPS),
        in_specs=[BS((None, None, tr, cols), lambda i, s: (s, lax.axis_index("c"), i, 0)),
                  BS((None, tr, cols), lambda i, s: (s, i, 0))],
        out_specs=[BS((None, tr, cols), lambda i, s: (s, i, 0)),
                   BS((None, tr, cols), lambda i, s: (my_chip(), i, 0))],
        out_shape=[out, out], compiler_params=_cp("parallel", "arbitrary"), name=name,
    )(g.reshape(N_CHIPS, 2, hr, cols), got)


def slot_sum(slots, name):
    _, hr, cols = slots.shape
    tr = _sum_rows_tile(hr, cols)
    nblk = hr // tr

    def body(s_ref, o_ref):
        acc = s_ref[0].astype(F32)
        for p in range(1, N_CHIPS):
            acc = acc + s_ref[p].astype(F32)
        o_ref[...] = acc

    return pl.pallas_call(
        body, grid=(nblk,), in_specs=[BS((N_CHIPS, tr, cols), lambda i: (0, i, 0))],
        out_specs=BS((tr, cols), lambda i: (lax.axis_index("c") * nblk + i, 0)),
        out_shape=SDS((2 * hr, cols), F32), compiler_params=_cp("parallel"), name=name,
    )(slots)


HBM_SPEC = pl.BlockSpec(memory_space=pltpu.HBM)
SEM_SPEC = pl.BlockSpec(memory_space=pltpu.SEMAPHORE)
EFFECT = pltpu.SideEffectType.DATAFLOW_SIDE_EFFECTING


def split_start(name, bufs, plan, n_copies):
    nb = len(bufs)

    def body(*refs):
        send_sems, recv_sems = refs[nb], refs[nb + 1]
        for k, (src, dst, dev) in enumerate(plan(refs[:nb])):
            _remote(src, dst, send_sems, recv_sems, k, dev).start()
        refs[-1][...] = jnp.zeros_like(refs[-1])

    outs = pl.pallas_call(
        body, name=name,
        out_shape=(pltpu.SemaphoreType.DMA((n_copies,)), pltpu.SemaphoreType.DMA((n_copies,)),
                   *[pltpu.HBM(b.shape, b.dtype) for b in bufs], SDS((8, LANES), F32)),
        in_specs=[HBM_SPEC] * nb, out_specs=(SEM_SPEC, SEM_SPEC, *[HBM_SPEC] * nb, VM),
        input_output_aliases={i: 2 + i for i in range(nb)},
        compiler_params=pltpu.CompilerParams(has_side_effects=EFFECT),
    )(*[pltpu.with_memory_space_constraint(b, pltpu.HBM) for b in bufs])
    return outs[0], outs[1], list(outs[2:2 + nb]), outs[-1]


def split_wait(name, send_sems, recv_sems, bufs, after, plan):
    nb = len(bufs)

    def body(*refs):
        send_ref, recv_ref = refs[nb], refs[nb + 1]
        for k, (src, dst, dev) in enumerate(plan(refs[:nb])):
            cp = _remote(src, dst, send_ref, recv_ref, k, dev)
            cp.wait_send()
            cp.wait_recv()

    outs = pl.pallas_call(
        body, name=name, out_shape=[pltpu.HBM(b.shape, b.dtype) for b in bufs],
        in_specs=[HBM_SPEC] * nb + [SEM_SPEC, SEM_SPEC, ANY], out_specs=[HBM_SPEC] * nb,
        input_output_aliases={i: i for i in range(nb)},
        compiler_params=pltpu.CompilerParams(has_side_effects=EFFECT),
    )(*bufs, send_sems, recv_sems, after)
    return list(outs)


def tie(x, tokens, name):
    def body(*refs):
        pass

    return pl.pallas_call(
        body, name=name, out_shape=SDS(x.shape, x.dtype), in_specs=[ANY] * (1 + len(tokens)), out_specs=ANY,
        input_output_aliases={0: 0},
    )(x, *tokens)


def plan_gather_ici(n, rows):
    def plan(refs):
        x, y, c = _coords()
        me = 2 * x + y
        out = []
        for i in range(n):
            hr = rows[i] // 2
            mine = pl.ds(pl.multiple_of(c * hr, 8), hr)
            out.append((refs[i], refs[n + i].at[me], (x, y, 1 - c)))
            for m in CHIP_FLIPS:
                out.append((refs[i].at[mine], refs[n + i].at[me, mine], (*_flip(x, y, m), c)))
        return out
    return plan


def plan_gather_d2d(n, rows):
    def plan(refs):
        x, y, c = _coords()
        me = 2 * x + y
        out = []
        for i in range(n):
            hr = rows[i] // 2
            mine = pl.ds(pl.multiple_of(c * hr, 8), hr)
            for m in CHIP_FLIPS:
                slot = refs[i].at[me ^ m, mine]
                out.append((slot, slot, (x, y, 1 - c)))
        return out
    return plan


def plan_swap(n, rows):
    def plan(refs):
        x, y, c = _coords()
        out = []
        for i in range(n):
            hr = rows[i] // 2
            other = pl.ds(pl.multiple_of((1 - c) * hr, 8), hr)
            out.append((refs[i].at[pl.ds(0, N_CHIPS), other], refs[n + i], (x, y, 1 - c)))
        return out
    return plan


def plan_exchange(n):
    def plan(refs):
        x, y, c = _coords()
        me = 2 * x + y
        out = []
        for i in range(n):
            for m in CHIP_FLIPS:
                out.append((refs[i].at[me ^ m], refs[n + i].at[me], (*_flip(x, y, m), c)))
        return out
    return plan


def plan_share(n, rows):
    def plan(refs):
        x, y, c = _coords()
        out = []
        for i in range(n):
            hr = rows[i] // 2
            mine = refs[i].at[pl.ds(pl.multiple_of(c * hr, 8), hr)]
            out.append((mine, mine, (x, y, 1 - c)))
        return out
    return plan


VM = pl.BlockSpec(memory_space=pltpu.VMEM)


def small_gather(v, name):
    def body(v_ref, out_ref, send_sems, recv_sems):
        x, y, c = _coords()
        me = 2 * x + y
        out_ref[me] = v_ref[...]
        cps = []
        for j, m in enumerate(CHIP_FLIPS):
            cp = _remote(v_ref, out_ref.at[me], send_sems, recv_sems, j, (*_flip(x, y, m), c))
            cp.start()
            cps.append(cp)
        for cp in cps:
            cp.wait()

    return pl.pallas_call(
        body, in_specs=[VM], out_specs=VM, out_shape=SDS((N_CHIPS,) + v.shape, v.dtype),
        scratch_shapes=[pltpu.SemaphoreType.DMA((3,)), pltpu.SemaphoreType.DMA((3,))],
        compiler_params=pltpu.CompilerParams(vmem_limit_bytes=VMEM_LIMIT_BYTES), name=name,
    )(v)


def plan_small_swap(refs):
    x, y, c = _coords()
    return [(refs[0], refs[1], (x, y, 1 - c))]


def plan_small_exchange(refs):
    x, y, c = _coords()
    me = 2 * x + y
    return [(refs[0].at[me], refs[0].at[me], (*_flip(x, y, m), c)) for m in CHIP_FLIPS]


def small_pair(v, sib, name):
    rows, cols = v.shape
    tr = _sum_rows_tile(rows, cols)

    def body(v_ref, s_ref, o_ref):
        o_ref[...] = v_ref[...] + s_ref[...]

    blk = BS((tr, cols), lambda i: (i, 0))
    return pl.pallas_call(
        body, grid=(rows // tr,), in_specs=[blk, blk],
        out_specs=BS((None, tr, cols), lambda i: (2 * lax.axis_index("x") + lax.axis_index("y"), i, 0)),
        out_shape=SDS((N_CHIPS, rows, cols), F32), compiler_params=_cp("parallel"), name=name,
    )(v, sib)


def small_total(slots, name):
    _, rows, cols = slots.shape
    tr = _sum_rows_tile(rows, cols)

    def body(s_ref, o_ref):
        o_ref[...] = (s_ref[0] + s_ref[1]) + (s_ref[2] + s_ref[3])

    return pl.pallas_call(
        body, grid=(rows // tr,), in_specs=[BS((N_CHIPS, tr, cols), lambda i: (0, i, 0))],
        out_specs=BS((tr, cols), lambda i: (i, 0)), out_shape=SDS((rows, cols), F32),
        compiler_params=_cp("parallel"), name=name,
    )(slots)


def adamw(w, g_list, m, v, name):
    nl, rows, cols = w.shape
    tr = _sum_rows_tile(rows, cols) if rows % 16 == 0 else rows
    bc1 = 1.0 - ADAM_B1 ** ADAM_STEP
    bc2 = 1.0 - ADAM_B2 ** ADAM_STEP

    def body(*refs):
        w_ref, m_ref, v_ref = refs[:3]
        g_refs = refs[3:3 + nl]
        go_ref, d_ref, mo_ref, vo_ref = refs[3 + nl:]
        layer = pl.program_id(0)
        for l in range(nl):
            @pl.when(layer == l)
            def _(l=l):
                g = g_refs[l][...]
                m_new = ADAM_B1 * m_ref[...] + (1.0 - ADAM_B1) * g
                v_new = ADAM_B2 * v_ref[...] + (1.0 - ADAM_B2) * (g * g)
                m_hat = m_new / bc1
                v_hat = v_new / bc2
                go_ref[...] = g
                d_ref[...] = -ADAM_LR * (m_hat / (jnp.sqrt(v_hat) + ADAM_EPS) + ADAM_WD * w_ref[...])
                mo_ref[...] = m_new
                vo_ref[...] = v_new

    stk = BS((None, tr, cols), lambda l, i: (l, i, 0))
    flat = BS((tr, cols), lambda l, i: (i, 0))
    out = SDS((nl, rows, cols), F32)
    return pl.pallas_call(
        body, grid=(nl, rows // tr), in_specs=[stk] * 3 + [flat] * nl, out_specs=[stk] * 4,
        out_shape=[out] * 4, compiler_params=_cp("parallel", "parallel"), name=name,
    )(w, m, v, *g_list)


WEIGHTS = ["mem_norm_g", "a_pre_mix_g", "a_post_mix_g", "a_pre_ffn_g", "a_post_ffn_g", "a_w_in", "a_conv_w",
           "a_conv_b", "a_gate_a_w", "a_gate_a_b", "a_gate_x_w", "a_gate_x_b", "a_lambda", "a_w_mem_kv", "a_w_out",
           "a_w_ffn_in", "a_w_ffn_out", "kv_norm_g", "w_kv_shared", "b_pre_mix_g", "b_post_mix_g", "b_pre_ffn_g",
           "b_post_ffn_g", "b_w_in", "b_w_mem_kv", "b_w_out", "b_w_ffn_in", "b_w_ffn_out"]
BIG = {"a_w_in": True, "a_w_mem_kv": False, "a_w_out": False, "a_w_ffn_in": True, "a_w_ffn_out": False,
       "w_kv_shared": True, "b_w_in": False, "b_w_mem_kv": False, "b_w_out": False, "b_w_ffn_in": True,
       "b_w_ffn_out": False}
SHARDED_SMALL = ["a_pre_mix_g", "a_post_mix_g", "a_pre_ffn_g", "a_post_ffn_g", "a_conv_w", "a_conv_b", "a_gate_a_b",
                 "a_gate_x_b", "a_lambda"]
REPL_SMALL = ["mem_norm_g", "kv_norm_g", "b_pre_mix_g", "b_post_mix_g", "b_pre_ffn_g", "b_post_ffn_g", "a_gate_a_w",
              "a_gate_x_w"]
LANES = 128


def _pack(arrs, row_multiple=8):
    flat = jnp.concatenate([a.reshape(-1) for a in arrs])
    pad = -flat.shape[0] % (LANES * row_multiple)
    if pad:
        flat = jnp.concatenate([flat, jnp.zeros((pad,), flat.dtype)])
    return flat.reshape(-1, LANES)


def _unpack(packed, shapes):
    flat = packed.reshape(-1)
    out, pos = [], 0
    for sh in shapes:
        size = math.prod(sh)
        out.append(flat[pos:pos + size].reshape(sh))
        pos += size
    return out


def kernel(x, mem, mem_norm_g, a_pre_mix_g, a_post_mix_g, a_pre_ffn_g, a_post_ffn_g, a_w_in, a_conv_w, a_conv_b,
           a_gate_a_w, a_gate_a_b, a_gate_x_w, a_gate_x_b, a_lambda, a_w_mem_kv, a_w_out, a_w_ffn_in, a_w_ffn_out,
           kv_norm_g, w_kv_shared, b_pre_mix_g, b_post_mix_g, b_pre_ffn_g, b_post_ffn_g, b_w_in, b_w_mem_kv, b_w_out,
           b_w_ffn_in, b_w_ffn_out, loss_target, m_mem_norm_g, m_a_pre_mix_g, m_a_post_mix_g, m_a_pre_ffn_g,
           m_a_post_ffn_g, m_a_w_in, m_a_conv_w, m_a_conv_b, m_a_gate_a_w, m_a_gate_a_b, m_a_gate_x_w, m_a_gate_x_b,
           m_a_lambda, m_a_w_mem_kv, m_a_w_out, m_a_w_ffn_in, m_a_w_ffn_out, m_kv_norm_g, m_w_kv_shared, m_b_pre_mix_g,
           m_b_post_mix_g, m_b_pre_ffn_g, m_b_post_ffn_g, m_b_w_in, m_b_w_mem_kv, m_b_w_out, m_b_w_ffn_in, m_b_w_ffn_out,
           v_mem_norm_g, v_a_pre_mix_g, v_a_post_mix_g, v_a_pre_ffn_g, v_a_post_ffn_g, v_a_w_in, v_a_conv_w, v_a_conv_b,
           v_a_gate_a_w, v_a_gate_a_b, v_a_gate_x_w, v_a_gate_x_b, v_a_lambda, v_a_w_mem_kv, v_a_w_out, v_a_w_ffn_in,
           v_a_w_ffn_out, v_kv_norm_g, v_w_kv_shared, v_b_pre_mix_g, v_b_post_mix_g, v_b_pre_ffn_g, v_b_post_ffn_g,
           v_b_w_in, v_b_w_mem_kv, v_b_w_out, v_b_w_ffn_in, v_b_w_ffn_out):
    a = dict(locals())
    xi, yi, ci = _coords()
    chip = 2 * xi + yi

    got = small_gather(_pack([a[n] for n in SHARDED_SMALL]), "small_gather")
    per_chip = [_unpack(got[s], [a[n].shape for n in SHARDED_SMALL]) for s in range(N_CHIPS)]
    small = {n: jnp.concatenate([per_chip[s][k] for s in range(N_CHIPS)], axis=-1)
             for k, n in enumerate(SHARDED_SMALL)}
    small.update({n: a[n] for n in REPL_SMALL})

    groups = []
    for l in range(4):
        p, j = ("a", l) if l < 2 else ("b", l - 2)
        groups.append([(p + "_" + n, j) for n in ("w_in", "w_mem_kv", "w_out")])
        groups.append([(p + "_" + n, j) for n in ("w_ffn_in", "w_ffn_out")])
    groups[3].append(("w_kv_shared", None))
    big = {n: [None, None] for n in BIG if n != "w_kv_shared"}
    gs, gb = {}, {}
    reduced = {n: [None, None] for n in BIG if n != "w_kv_shared"}

    def put(store, n, j, val):
        if j is None:
            store[n] = val
        else:
            store[n][j] = val

    class Exchange:
        def __init__(self):
            self.state = {}

        def gather_ici(self, g):
            shards = [(a[n] if j is None else a[n][j]).astype(MXU) for n, j in groups[g]]
            rows = [sh.shape[0] for sh in shards]
            lands = [lax.empty((N_CHIPS,) + sh.shape, sh.dtype) for sh in shards]
            plan = plan_gather_ici(len(shards), rows)
            ss, rs, bufs, tok = split_start("gather_ici_%d" % g, shards + lands, plan, 4 * len(shards))
            self.state["g", g] = (ss, rs, bufs, plan, rows)
            return tok

        def gather_d2d(self, g, after):
            ss, rs, bufs, plan, rows = self.state.pop(("g", g))
            n = len(rows)
            outs = split_wait("gather_ici_wait_%d" % g, ss, rs, bufs, after, plan)[n:]
            plan = plan_gather_d2d(n, rows)
            ss, rs, bufs, tok = split_start("gather_d2d_%d" % g, outs, plan, 3 * n)
            self.state["g", g] = (ss, rs, bufs, plan)
            return tok

        def gather_done(self, g, after):
            ss, rs, bufs, plan = self.state.pop(("g", g))
            outs = split_wait("gather_d2d_wait_%d" % g, ss, rs, bufs, after, plan)
            for (n, j), w in zip(groups[g], outs):
                put(big, n, j, w if BIG[n] else w.reshape(-1, w.shape[-1]))

        def rs_swap(self, g):
            grads = []
            for n, j in groups[g]:
                gr = gb[n] if j is None else gb[n][j]
                grads.append(gr if BIG[n] else gr.reshape(N_CHIPS, gr.shape[0] // N_CHIPS, gr.shape[1]))
            rows = [gr.shape[1] for gr in grads]
            lands = [lax.empty((N_CHIPS, gr.shape[1] // 2, gr.shape[2]), gr.dtype) for gr in grads]
            plan = plan_swap(len(grads), rows)
            ss, rs, bufs, tok = split_start("rs_swap_%d" % g, grads + lands, plan, len(grads))
            self.state["r", g] = (ss, rs, bufs, plan, rows)
            return tok

        def rs_exchange(self, g, after):
            ss, rs, bufs, plan, rows = self.state.pop(("r", g))
            n = len(rows)
            bufs = split_wait("rs_swap_wait_%d" % g, ss, rs, bufs, after, plan)
            sums = [half_sum(gr, got, "rs_half_sum") for gr, got in zip(bufs[:n], bufs[n:])]
            plan = plan_exchange(n)
            ss, rs, bufs, tok = split_start("rs_exchange_%d" % g, [p for p, _ in sums] + [s for _, s in sums], plan,
                                            3 * n)
            self.state["r", g] = (ss, rs, bufs, plan, rows)
            return tok

        def rs_share(self, g, after):
            ss, rs, bufs, plan, rows = self.state.pop(("r", g))
            n = len(rows)
            slots = split_wait("rs_exchange_wait_%d" % g, ss, rs, bufs, after, plan)[n:]
            fulls = [slot_sum(s, "rs_slot_sum") for s in slots]
            plan = plan_share(n, rows)
            ss, rs, bufs, tok = split_start("rs_share_%d" % g, fulls, plan, n)
            self.state["r", g] = (ss, rs, bufs, plan)
            return tok

        def rs_done(self, g, after):
            ss, rs, bufs, plan = self.state.pop(("r", g))
            outs = split_wait("rs_share_wait_%d" % g, ss, rs, bufs, after, plan)
            for (n, j), r in zip(groups[g], outs):
                put(reduced, n, j, r)

        def hook(self, where, l, after):
            mix, ffn = 2 * l, 2 * l + 1
            toks = []
            if where == "fwd_begin":
                if l == 0:
                    tok = self.gather_ici(mix)
                    tok = self.gather_d2d(mix, tok)
                    self.gather_done(mix, tok)
                toks.append(self.gather_ici(ffn))
            elif where == "fwd_q1":
                toks.append(self.gather_d2d(ffn, after))
            elif where == "fwd_mid":
                self.gather_done(ffn, after)
                if l < 3:
                    toks.append(self.gather_ici(mix + 2))
            elif where == "fwd_q3":
                if l < 3:
                    toks.append(self.gather_d2d(mix + 2, after))
            elif where == "fwd_end":
                if l < 3:
                    self.gather_done(mix + 2, after)
            elif where == "bwd_begin":
                if l < 3:
                    self.rs_done(ffn + 2, after)
                    toks.append(self.rs_exchange(mix + 2, after))
            elif where == "bwd_mid1":
                if l < 3:
                    toks.append(self.rs_share(mix + 2, after))
            elif where == "bwd_mid2":
                if l < 3:
                    self.rs_done(mix + 2, after)
                toks.append(self.rs_swap(ffn))
            elif where == "bwd_m1":
                toks.append(self.rs_exchange(ffn, after))
            elif where == "bwd_end":
                toks.append(self.rs_share(ffn, after))
                toks.append(self.rs_swap(mix))
                if l == 0:
                    self.rs_done(ffn, toks[0])
                    tok = self.rs_exchange(mix, toks[1])
                    tok = self.rs_share(mix, adamw_big([n for n in BIG if n.startswith("b_")], tok))
                    self.rs_done(mix, tok)
                    toks = []
            else:
                raise ValueError(where)
            return toks

    res = {}

    def adamw_big(names, token=None):
        last = None
        for n in names:
            shape = a[n].shape
            rows, cols = shape[-2], shape[-1]
            stk = (-1, rows, cols)
            grads = reduced[n] if isinstance(reduced[n], list) else [reduced[n]]
            if token is not None:
                grads = [tie(grads[0], [token], "tie_adamw_" + n)] + grads[1:]
            outs = adamw(a[n].reshape(stk), grads, a["m_" + n].reshape(stk), a["v_" + n].reshape(stk), "adamw")
            res[n] = [o.reshape(shape) for o in outs]
            last = outs[1]
            token = last if token is not None else None
        return last

    loss_parts, dx = _fwd_bwd(x[0], mem[0], loss_target[0], small, big, gs, gb, Exchange())
    loss = lax.psum(jnp.sum(loss_parts) * (0.5 / D), ("x", "y", "c"))

    def full(n):
        g = gs[n]
        return jnp.stack(g) if isinstance(g, list) else g

    order = SHARDED_SMALL + REPL_SMALL
    full_shapes = [full(n).shape for n in order]
    pack = _pack([full(n) for n in order], 512)
    ss, rs, bufs, tok = split_start("small_swap", [pack, lax.empty(pack.shape, F32)], plan_small_swap, 1)
    mine_v, sib_v = split_wait("small_swap_wait", ss, rs, bufs, tok, plan_small_swap)
    ss, rs, bufs, tok = split_start("small_exchange", [small_pair(mine_v, sib_v, "small_pair")],
                                    plan_small_exchange, 3)
    last = adamw_big([n for n in BIG if n not in res], tok)
    slots = split_wait("small_exchange_wait", ss, rs, bufs, last, plan_small_exchange)[0]
    summed = _unpack(small_total(slots, "small_total"), full_shapes)
    mine = []
    for n, g in zip(order, summed):
        if n in SHARDED_SMALL:
            width = a[n].shape[-1]
            g = lax.dynamic_slice_in_dim(g, chip * width, width, axis=g.ndim - 1)
        mine.append(g.reshape(a[n].shape))
    shapes = [a[n].shape for n in order]
    rm = 512
    outs = adamw(_pack([a[n] for n in order], rm)[None], [_pack(mine, rm)],
                 _pack([a["m_" + n] for n in order], rm)[None], _pack([a["v_" + n] for n in order], rm)[None],
                 "adamw_small")
    unpacked = [_unpack(o[0], shapes) for o in outs]
    for k, n in enumerate(order):
        res[n] = [u[k] for u in unpacked]

    return (loss, dx[None], *[res[n][0] for n in WEIGHTS], *[res[n][1] for n in WEIGHTS],
            *[res[n][2] for n in WEIGHTS], *[res[n][3] for n in WEIGHTS])
```

```python
import math

import jax
import jax.numpy as jnp
from jax import lax
from jax.experimental import pallas as pl
from jax.experimental.pallas import tpu as pltpu

D = 2048
HD = 128
MEM_W = 512
MEM_HEADS = 4
MIX_W = D - MEM_W
N_BLK = MIX_W // HD
D_FF = 5632
N_MEM = 256
RMS_EPS = 1e-6
NEG_INF = -1e30
LRU_C = 8.0
DIL_GROUPS = ((128, 1), (512, 4), (2048, 16))
Q_BLOCK = 128
SCALE = HD ** -0.5
N_CHIPS = 4
MXU_COLS = 256
ACC_CHUNK = 2 * MXU_COLS

ADAM_LR = 0.001
ADAM_B1 = 0.9
ADAM_B2 = 0.999
ADAM_EPS = 1e-08
ADAM_WD = 0.01
ADAM_STEP = 10

MXU = jnp.bfloat16
F32 = jnp.float32
VMEM_LIMIT_BYTES = 56 * 1024 * 1024

BS = pl.BlockSpec
SDS = jax.ShapeDtypeStruct
MESH = pl.DeviceIdType.MESH


def _cp(*sem):
    return pltpu.CompilerParams(dimension_semantics=sem or None, vmem_limit_bytes=VMEM_LIMIT_BYTES)


def _dot(a, b, dn=((1,), (0,))):
    return lax.dot_general(a, b, (dn, ((), ())), preferred_element_type=F32)


def _div(i, n):
    return lax.div(i, jnp.int32(n))


def _rem(i, n):
    return lax.rem(i, jnp.int32(n))


NN = ((1,), (0,))
NT = ((1,), (1,))
TN = ((0,), (0,))


def _sigmoid(z):
    return 0.5 * jnp.tanh(0.5 * z) + 0.5


def _logistic(z):
    return 1.0 / (1.0 + jnp.exp(-z))


def _log1p_pos(u):
    return jnp.where(u < 1e-2, u * (1.0 - u * (0.5 - u * (1.0 / 3.0))), jnp.log(1.0 + u))


def _neg_expm1(z):
    return jnp.where(z > -1e-2, -z * (1.0 + z * (0.5 + z * (1.0 / 6.0))), 1.0 - jnp.exp(z))


def _softplus(z):
    return jnp.maximum(z, 0.0) + _log1p_pos(jnp.exp(-jnp.abs(z)))


_GELU_C = math.sqrt(2.0 / math.pi)


def _gelu_and_grad(x):
    x2 = x * x
    t = jnp.tanh(_GELU_C * (x + 0.044715 * x * x2))
    g = 0.5 * x * (1.0 + t)
    dg = 0.5 * (1.0 + t) + 0.5 * x * (1.0 - t * t) * _GELU_C * (1.0 + 3.0 * 0.044715 * x2)
    return g, dg


def _row_tile(rows):
    return min(512, rows)


def norm_cast(x, g, name):
    rows = x.shape[0]
    tr = _row_tile(rows)

    def body(x_ref, g_ref, o_ref):
        xv = x_ref[...]
        r = lax.rsqrt(jnp.mean(xv * xv, axis=-1, keepdims=True) + RMS_EPS)
        o_ref[...] = (xv * r * g_ref[...]).astype(o_ref.dtype)

    return pl.pallas_call(
        body, grid=(rows // tr,),
        in_specs=[BS((tr, D), lambda i: (i, 0)), BS((1, D), lambda i: (0, 0))],
        out_specs=BS((tr, D), lambda i: (i, 0)),
        out_shape=SDS((rows, D), MXU), compiler_params=_cp("parallel"), name=name,
    )(x, g.reshape(1, D))


def resid_norm(h, y, g, name):
    rows = h.shape[0]
    tr = _row_tile(rows)

    def body(h_ref, y_ref, g_ref, o_ref):
        yv = y_ref[...]
        r = lax.rsqrt(jnp.mean(yv * yv, axis=-1, keepdims=True) + RMS_EPS)
        o_ref[...] = h_ref[...] + yv * r * g_ref[...]

    return pl.pallas_call(
        body, grid=(rows // tr,),
        in_specs=[BS((tr, D), lambda i: (i, 0)), BS((tr, D), lambda i: (i, 0)), BS((1, D), lambda i: (0, 0))],
        out_specs=BS((tr, D), lambda i: (i, 0)),
        out_shape=SDS((rows, D), F32), compiler_params=_cp("parallel"), name=name,
    )(h, y, g.reshape(1, D))


def resid_norm_next(h, y, g, g_next, name):
    rows = h.shape[0]
    tr = _row_tile(rows)

    def body(h_ref, y_ref, g_ref, gn_ref, o_ref, n_ref):
        yv = y_ref[...]
        r = lax.rsqrt(jnp.mean(yv * yv, axis=-1, keepdims=True) + RMS_EPS)
        hv = h_ref[...] + yv * r * g_ref[...]
        o_ref[...] = hv
        r2 = lax.rsqrt(jnp.mean(hv * hv, axis=-1, keepdims=True) + RMS_EPS)
        n_ref[...] = (hv * r2 * gn_ref[...]).astype(n_ref.dtype)

    row = BS((tr, D), lambda i: (i, 0))
    vec = BS((1, D), lambda i: (0, 0))
    return pl.pallas_call(
        body, grid=(rows // tr,), in_specs=[row, row, vec, vec], out_specs=[row, row],
        out_shape=[SDS((rows, D), F32), SDS((rows, D), MXU)], compiler_params=_cp("parallel"), name=name,
    )(h, y, g.reshape(1, D), g_next.reshape(1, D))


def _norm_bwd_rows(xv, gv, dyv):
    r = lax.rsqrt(jnp.mean(xv * xv, axis=-1, keepdims=True) + RMS_EPS)
    xhat = xv * r
    dxhat = dyv * gv
    dx = r * (dxhat - xhat * jnp.mean(dxhat * xhat, axis=-1, keepdims=True))
    return dx, jnp.sum(dyv * xhat, axis=0, keepdims=True)


def norm_bwd(x, g, dy, res, out_dtype, name, then=None):
    rows = x.shape[0]
    tr = _row_tile(rows)
    has_res = res is not None
    n_in = 3 + has_res + (2 if then else 0)

    def body(*refs):
        x_ref, g_ref, dy_ref = refs[:3]
        dx_ref, dg_ref = refs[n_in], refs[n_in + 1]
        dx, dg = _norm_bwd_rows(x_ref[...], g_ref[...], dy_ref[...].astype(F32))
        if has_res:
            dx = dx + refs[3][...]
        dx_ref[...] = dx.astype(dx_ref.dtype)
        first = pl.program_id(0) == 0

        @pl.when(first)
        def _():
            dg_ref[...] = jnp.zeros_like(dg_ref)

        dg_ref[...] += dg
        if then:
            x2_ref, g2_ref = refs[n_in - 2], refs[n_in - 1]
            dx2_ref, dg2_ref = refs[n_in + 2], refs[n_in + 3]
            dx2, dg2 = _norm_bwd_rows(x2_ref[...], g2_ref[...], dx)
            dx2_ref[...] = dx2.astype(dx2_ref.dtype)

            @pl.when(first)
            def _():
                dg2_ref[...] = jnp.zeros_like(dg2_ref)

            dg2_ref[...] += dg2

    row = BS((tr, D), lambda i: (i, 0))
    vec = BS((1, D), lambda i: (0, 0))
    ins = [x, g.reshape(1, D), dy] + ([res] if has_res else []) + ([then[0], then[1].reshape(1, D)] if then else [])
    outs = pl.pallas_call(
        body, grid=(rows // tr,),
        in_specs=[row, vec, row] + ([row] if has_res else []) + ([row, vec] if then else []),
        out_specs=[row, vec] + ([row, vec] if then else []),
        out_shape=[SDS((rows, D), out_dtype), SDS((1, D), F32)] + ([SDS((rows, D), MXU), SDS((1, D), F32)] if then else []),
        compiler_params=_cp("arbitrary"), name=name,
    )(*ins)
    if then:
        return outs[0], outs[1].reshape(D), outs[2], outs[3].reshape(D)
    return outs[0], outs[1].reshape(D)


def loss_head(y, target, name):
    rows = y.shape[0]
    tr = _row_tile(rows)

    def body(y_ref, t_ref, dy_ref, acc_ref):
        err = y_ref[...] - t_ref[...]
        dy_ref[...] = err * (1.0 / D)

        @pl.when(pl.program_id(0) == 0)
        def _():
            acc_ref[...] = jnp.zeros_like(acc_ref)

        acc_ref[...] += jnp.sum(err * err, axis=0, keepdims=True)

    row = BS((tr, D), lambda i: (i, 0))
    dy, acc = pl.pallas_call(
        body, grid=(rows // tr,), in_specs=[row, row],
        out_specs=[row, BS((1, D), lambda i: (0, 0))],
        out_shape=[SDS((rows, D), F32), SDS((1, D), F32)],
        compiler_params=_cp("arbitrary"), name=name,
    )(y, target)
    return acc, dy


def _mm_call(ins, in_specs, pick, dn, grid, o_spec, out_sds, name):
    gk = grid[2]
    n_in = len(ins)

    def body(*refs):
        o_ref = refs[n_in]
        k = pl.program_id(2)

        def step(a_ref, b_ref):
            acc = o_ref if (out_sds.dtype == F32 or gk == 1) else refs[n_in + 1]
            width = acc.shape[-1]
            if dn == TN or width <= ACC_CHUNK:
                chunks = [(0, width)]
            else:
                chunks = [(c0, min(c0 + ACC_CHUNK, width)) for c0 in range(0, width, ACC_CHUNK)]

            def sweep(first):
                a = a_ref[...]
                pending = None
                for c0, c1 in chunks:
                    p = _dot(a, b_ref[c0:c1, :] if dn == NT else b_ref[:, c0:c1], dn)
                    if pending is not None:
                        put(first, *pending)
                    pending = (c0, c1, p)
                put(first, *pending)

            def put(first, c0, c1, p):
                if first:
                    acc[:, c0:c1] = p.astype(acc.dtype)
                else:
                    acc[:, c0:c1] += p

            if gk == 1:
                sweep(True)
                return

            @pl.when(k == 0)
            def _():
                sweep(True)

            @pl.when(k > 0)
            def _():
                sweep(False)

            if acc is not o_ref:
                @pl.when(k == gk - 1)
                def _():
                    o_ref[...] = acc[...].astype(o_ref.dtype)

        pick(refs[:n_in], k, step)

    scratch = []
    if gk > 1 and out_sds.dtype != F32:
        scratch = [pltpu.VMEM(o_spec.block_shape[-2:], F32)]
    return pl.pallas_call(
        body, grid=grid, in_specs=in_specs, out_specs=o_spec, out_shape=out_sds,
        scratch_shapes=scratch, compiler_params=_cp("parallel", "parallel", "arbitrary"), name=name,
    )(*ins)


def _pick2(refs, k, step):
    step(refs[0], refs[1])


def mm_nn(a, w, *, tm, tn, tk, out_dtype, name):
    m, kdim = a.shape
    if w.ndim == 3:
        c = w.shape[2]
        n = N_CHIPS * c
        per = c // tn
        b_spec = BS((None, tk, tn), lambda i, j, k: (_div(j, per), k, _rem(j, per)))
    else:
        n = w.shape[1]
        b_spec = BS((tk, tn), lambda i, j, k: (k, j))
    grid = (m // tm, n // tn, kdim // tk)
    return _mm_call([a, w], [BS((tm, tk), lambda i, j, k: (i, k)), b_spec], _pick2, NN, grid,
                    BS((tm, tn), lambda i, j, k: (i, j)), SDS((m, n), out_dtype), name)


def mm_nt(a_list, w, *, tm, tn, tk, out_dtype, name):
    m = a_list[0].shape[0]
    ka = a_list[0].shape[1]
    n_a = len(a_list)
    kdim = ka * n_a
    if w.ndim == 3:
        c = w.shape[2]
        n = w.shape[1]
        per = c // tk
        b_spec = BS((None, tn, tk), lambda i, j, k: (_div(k, per), j, _rem(k, per)))
    else:
        n = w.shape[0]
        b_spec = BS((tn, tk), lambda i, j, k: (j, k))
    gk = kdim // tk
    half = gk // n_a
    grid = (m // tm, n // tn, gk)
    if n_a == 1:
        a_specs = [BS((tm, tk), lambda i, j, k: (i, k))]
        pick = lambda refs, k, step: step(refs[0], refs[1])
    else:
        a_specs = [BS((tm, tk), lambda i, j, k: (i, jnp.minimum(k, half - 1))),
                   BS((tm, tk), lambda i, j, k: (i, jnp.maximum(k - half, 0)))]

        def pick(refs, k, step):
            @pl.when(k < half)
            def _():
                step(refs[0], refs[2])

            @pl.when(k >= half)
            def _():
                step(refs[1], refs[2])

    return _mm_call(list(a_list) + [w], a_specs + [b_spec], pick, NT, grid,
                    BS((tm, tn), lambda i, j, k: (i, j)), SDS((m, n), out_dtype), name)


def mm_tn(a, b_list, *, t1, tn, ts, col_shards, name):
    s, k1 = a.shape
    nb = b_list[0].shape[1]
    n_b = len(b_list)
    n = nb * n_b
    gn = n // tn
    half = gn // n_b
    grid = (k1 // t1, gn, s // ts)
    if col_shards:
        c = n // N_CHIPS
        per = c // tn
        o_spec = BS((None, t1, tn), lambda i, j, k: (_div(j, per), i, _rem(j, per)))
        out_sds = SDS((N_CHIPS, k1, c), MXU)
    else:
        o_spec = BS((t1, tn), lambda i, j, k: (i, j))
        out_sds = SDS((k1, n), MXU)
    a_spec = BS((ts, t1), lambda i, j, k: (k, i))
    if n_b == 1:
        b_specs = [BS((ts, tn), lambda i, j, k: (k, j))]
        pick = lambda refs, k, step: step(refs[0], refs[1])
    else:
        b_specs = [BS((ts, tn), lambda i, j, k: (jnp.where(j < half, k, 0), jnp.minimum(j, half - 1))),
                   BS((ts, tn), lambda i, j, k: (jnp.where(j >= half, k, 0), jnp.maximum(j - half, 0)))]

        def pick(refs, k, step):
            j = pl.program_id(1)

            @pl.when(j < half)
            def _():
                step(refs[0], refs[1])

            @pl.when(j >= half)
            def _():
                step(refs[0], refs[2])

    return _mm_call([a] + list(b_list), [a_spec] + b_specs, pick, TN, grid, o_spec, out_sds, name)


def ffn_in_fwd(hn, w, name):
    s = hn.shape[0]
    tm = min(512, s)
    tn = D_FF // 4

    def tail(dag_ref, dau_ref, act_ref, c0, c1, g, u):
        sg = _sigmoid(g)
        silu = g * sg
        dag_ref[:, c0:c1] = (u * sg * (1.0 + g * (1.0 - sg))).astype(dag_ref.dtype)
        dau_ref[:, c0:c1] = silu.astype(dau_ref.dtype)
        act_ref[:, c0:c1] = (silu * u).astype(act_ref.dtype)

    def body(a_ref, wg_ref, wu_ref, dag_ref, dau_ref, act_ref):
        a = a_ref[...]
        pending = None
        for c0 in range(0, tn, ACC_CHUNK):
            c1 = min(c0 + ACC_CHUNK, tn)
            g = _dot(a, wg_ref[:, c0:c1])
            u = _dot(a, wu_ref[:, c0:c1])
            if pending is not None:
                tail(dag_ref, dau_ref, act_ref, *pending)
            pending = (c0, c1, g, u)
        tail(dag_ref, dau_ref, act_ref, *pending)

    tile = BS((tm, tn), lambda j, i: (i, j))
    return pl.pallas_call(
        body, grid=(4, s // tm),
        in_specs=[BS((tm, D), lambda j, i: (i, 0)),
                  BS((None, D, tn), lambda j, i: (_div(j, 2), 0, _rem(j, 2))),
                  BS((None, D, tn), lambda j, i: (2 + _div(j, 2), 0, _rem(j, 2)))],
        out_specs=[tile, tile, tile],
        out_shape=[SDS((s, D_FF), MXU), SDS((s, D_FF), MXU), SDS((s, D_FF), MXU)],
        compiler_params=_cp("parallel", "parallel"), name=name,
    )(hn, w, w)


def ffn_act_bwd(dy, w_out, dag, dau, name):
    s = dy.shape[0]
    tm = min(1024, s)
    tn = D_FF // 4

    def body(dy_ref, w_ref, dag_ref, dau_ref, dg_ref, du_ref):
        def tail(c0, c1, dact):
            dg_ref[:, c0:c1] = (dact * dag_ref[:, c0:c1].astype(F32)).astype(dg_ref.dtype)
            du_ref[:, c0:c1] = (dact * dau_ref[:, c0:c1].astype(F32)).astype(du_ref.dtype)

        dy = dy_ref[...]
        pending = None
        for c0 in range(0, tn, ACC_CHUNK):
            c1 = min(c0 + ACC_CHUNK, tn)
            dact = _dot(dy, w_ref[c0:c1, :], NT)
            if pending is not None:
                tail(*pending)
            pending = (c0, c1, dact)
        tail(*pending)

    tile = BS((tm, tn), lambda j, i: (i, j))
    return pl.pallas_call(
        body, grid=(4, s // tm),
        in_specs=[BS((tm, D), lambda j, i: (i, 0)), BS((tn, D), lambda j, i: (j, 0)), tile, tile],
        out_specs=[tile, tile],
        out_shape=[SDS((s, D_FF), MXU), SDS((s, D_FF), MXU)],
        compiler_params=_cp("parallel", "parallel"), name=name,
    )(dy, w_out, dag, dau)


LRU_T = 256
HALO = 8


def _shift_down(x, k, fill):
    rows = x.shape[0]
    idx = lax.broadcasted_iota(jnp.int32, x.shape, 0)
    return jnp.where(idx < k, fill, pltpu.roll(x, k, 0))


def _shift_up(x, k, fill):
    rows = x.shape[0]
    idx = lax.broadcasted_iota(jnp.int32, x.shape, 0)
    return jnp.where(idx >= rows - k, fill, pltpu.roll(x, rows - k, 0))


def _scan_block(a, b, carry, reverse):
    rows, cols = a.shape
    sub = 8
    in_group = lax.broadcasted_iota(jnp.int32, a.shape, 0) % sub
    for sh in (1, 2, 4):
        if reverse:
            a_s, b_s, ok = pltpu.roll(a, rows - sh, 0), pltpu.roll(b, rows - sh, 0), in_group < sub - sh
        else:
            a_s, b_s, ok = pltpu.roll(a, sh, 0), pltpu.roll(b, sh, 0), in_group >= sh
        b = jnp.where(ok, a * b_s + b, b)
        a = jnp.where(ok, a * a_s, a)
    groups = list(range(rows // sub))
    edge = 0 if reverse else sub - 1
    carry_in = {}
    for v in (reversed(groups) if reverse else groups):
        carry_in[v] = carry
        row = sub * v + edge
        carry = b[row:row + 1, :] + a[row:row + 1, :] * carry
    cin = jnp.concatenate([jnp.broadcast_to(carry_in[v], (sub, cols)) for v in groups], axis=0)
    return b + a * cin


def _conv_taps(xcat):
    rows = xcat.shape[0]
    taps = []
    for k in range(4):
        off = HALO - 3 + k
        taps.append(xcat[off:off + LRU_T] if off == HALO else pltpu.roll(xcat, rows - off, 0)[:LRU_T])
    return taps


def _gates(xc, wa_ref, ba, wx_ref, bx, lam, za_ref, zx_ref):
    xm = xc.astype(MXU)
    for n in range(N_BLK):
        sl = slice(n * HD, (n + 1) * HD)
        za_ref[:, sl] = _dot(xm[:, sl], wa_ref[n])
        zx_ref[:, sl] = _dot(xm[:, sl], wx_ref[n])
    ra = _logistic(za_ref[...] + ba)
    ii = _logistic(zx_ref[...] + bx)
    sp = _softplus(-lam)
    log_a = -LRU_C * ra * sp
    a = jnp.exp(log_a)
    mult = jnp.sqrt(_neg_expm1(2.0 * log_a))
    return ra, ii, sp, a, mult


def lru_fwd(proj, conv_w, conv_b, wa, ba, wx, bx, lam, name):
    s = proj.shape[0]
    c = MIX_W
    nblk = s // LRU_T
    hpb = LRU_T // HALO

    def body(x_ref, halo_ref, cw_ref, cb_ref, wa_ref, ba_ref, wx_ref, bx_ref, lam_ref,
             xc_ref, h_ref, carry, za_ref, zx_ref):
        i = pl.program_id(0)

        @pl.when(i == 0)
        def _():
            carry[...] = jnp.zeros_like(carry)

        halo = jnp.where(i == 0, 0.0, halo_ref[...])
        xcat = jnp.concatenate([halo, x_ref[...]], axis=0)
        taps = _conv_taps(xcat)
        xc = cb_ref[...] + sum(cw_ref[k:k + 1, :] * taps[k] for k in range(4))
        xc_ref[...] = xc
        _, ii, _, a, mult = _gates(xc, wa_ref, ba_ref[...], wx_ref, bx_ref[...], lam_ref[...], za_ref, zx_ref)
        h = _scan_block(a, mult * (ii * xc), carry[HALO - 1:HALO, :], False)
        h_ref[...] = h
        carry[...] = h[LRU_T - HALO:, :]

    def full(shape):
        return BS(shape, lambda i: (0,) * len(shape))

    blk = BS((LRU_T, c), lambda i: (i, 0))
    return pl.pallas_call(
        body, grid=(nblk,),
        in_specs=[blk, BS((HALO, c), lambda i: (jnp.maximum(i * hpb - 1, 0), 0)),
                  full((4, c)), full((1, c)), full((N_BLK, HD, HD)), full((1, c)),
                  full((N_BLK, HD, HD)), full((1, c)), full((1, c))],
        out_specs=[blk, blk],
        out_shape=[SDS((s, c), F32), SDS((s, c), F32)],
        scratch_shapes=[pltpu.VMEM((HALO, c), F32), pltpu.VMEM((LRU_T, c), F32), pltpu.VMEM((LRU_T, c), F32)],
        compiler_params=_cp("arbitrary"), name=name,
    )(proj, proj, conv_w, conv_b.reshape(1, c), wa.astype(MXU), ba.reshape(1, c), wx.astype(MXU),
      bx.reshape(1, c), lam.reshape(1, c))


def lru_mix_prep(h, proj, m, name):
    s = h.shape[0]
    tr = _row_tile(s)

    def body(h_ref, gb_ref, m_ref, o_ref):
        ge, _ = _gelu_and_grad(gb_ref[...])
        o_ref[:, :MIX_W] = (h_ref[...] * ge).astype(o_ref.dtype)
        o_ref[:, MIX_W:] = m_ref[...]

    return pl.pallas_call(
        body, grid=(s // tr,),
        in_specs=[BS((tr, MIX_W), lambda i: (i, 0)), BS((tr, MIX_W), lambda i: (i, 1)),
                  BS((tr, MEM_W), lambda i: (i, 0))],
        out_specs=BS((tr, D), lambda i: (i, 0)), out_shape=SDS((s, D), MXU),
        compiler_params=_cp("parallel"), name=name,
    )(h, proj, m)


def lru_bwd(dym, proj, xc, hl, dqm, conv_w, wa, ba, wx, bx, lam, name):
    s = proj.shape[0]
    c = MIX_W
    nblk = s // LRU_T
    hpb = LRU_T // HALO
    wa_m = wa.astype(MXU)
    wx_m = wx.astype(MXU)

    def body(dy_ref, x_ref, xhalo_ref, gb_ref, xc_ref, h_ref, hhalo_ref, dqm_ref,
             cw_ref, wa_ref, ba_ref, wx_ref, bx_ref, lam_ref,
             dproj_ref, dcw_ref, dcb_ref, dwa_ref, dba_ref, dwx_ref, dbx_ref, dlam_ref,
             g_next, a_next, dxc_next, za_ref, zx_ref, dxc_ref):
        i = pl.program_id(0)

        @pl.when(i == 0)
        def _():
            g_next[...] = jnp.zeros_like(g_next)
            a_next[...] = jnp.zeros_like(a_next)
            dxc_next[...] = jnp.zeros_like(dxc_next)
            for r in (dcw_ref, dcb_ref, dwa_ref, dba_ref, dwx_ref, dbx_ref, dlam_ref):
                r[...] = jnp.zeros_like(r)

        first = i == nblk - 1
        xc = xc_ref[...]
        lam = lam_ref[...]
        ra, ii, sp, a, mult = _gates(xc, wa_ref, ba_ref[...], wx_ref, bx_ref[...], lam, za_ref, zx_ref)
        hl_v = h_ref[...]
        ge, dge = _gelu_and_grad(gb_ref[...])
        dyl = dy_ref[...]
        dhl = dyl * ge
        dproj_ref[:, c:2 * c] = (dyl * hl_v * dge).astype(dproj_ref.dtype)
        dproj_ref[:, 2 * c:] = dqm_ref[...]

        an = _shift_up(a, 1, 0.0)
        last_row = lax.broadcasted_iota(jnp.int32, a.shape, 0) == LRU_T - 1
        an = jnp.where(last_row, a_next[0:1, :], an)
        g = _scan_block(an, dhl, g_next[0:1, :], True)
        g_next[...] = g[:HALO, :]
        a_next[...] = a[:HALO, :]

        hhalo = jnp.where(first, 0.0, hhalo_ref[...])
        h_prev = _shift_down(hl_v, 1, 0.0)
        first_row = lax.broadcasted_iota(jnp.int32, a.shape, 0) == 0
        h_prev = jnp.where(first_row, hhalo[HALO - 1:HALO, :], h_prev)
        da = g * h_prev
        ixc = ii * xc
        dmult = g * ixc
        dii = g * mult * xc
        dxc = g * mult * ii
        dlog_a = (da - dmult * a / mult) * a
        dra = dlog_a * (-LRU_C) * sp
        dlam_ref[...] += jnp.sum(dlog_a * ra, axis=0, keepdims=True) * (LRU_C * _logistic(-lam))
        dza = dra * ra * (1.0 - ra)
        dzx = dii * ii * (1.0 - ii)
        dba_ref[...] += jnp.sum(dza, axis=0, keepdims=True)
        dbx_ref[...] += jnp.sum(dzx, axis=0, keepdims=True)
        xm = xc.astype(MXU)
        dza_m = dza.astype(MXU)
        dzx_m = dzx.astype(MXU)
        for n in range(N_BLK):
            sl = slice(n * HD, (n + 1) * HD)
            dwa_ref[n] += _dot(xm[:, sl], dza_m[:, sl], TN)
            dwx_ref[n] += _dot(xm[:, sl], dzx_m[:, sl], TN)
            dxc_ref[:, sl] = _dot(dza_m[:, sl], wa_ref[n], NT) + _dot(dzx_m[:, sl], wx_ref[n], NT)
        dxc = dxc + dxc_ref[...]

        dcat = jnp.concatenate([dxc, dxc_next[...]], axis=0)
        rows = dcat.shape[0]
        dxb = cw_ref[3:4, :] * dxc
        for k in range(3):
            dxb = dxb + cw_ref[k:k + 1, :] * pltpu.roll(dcat, rows - (3 - k), 0)[:LRU_T]
        dproj_ref[:, :c] = dxb.astype(dproj_ref.dtype)
        dxc_next[...] = dxc[:HALO, :]

        xhalo = jnp.where(first, 0.0, xhalo_ref[...])
        taps = _conv_taps(jnp.concatenate([xhalo, x_ref[...]], axis=0))
        for k in range(4):
            dcw_ref[k:k + 1, :] += jnp.sum(dxc * taps[k], axis=0, keepdims=True)
        dcb_ref[...] += jnp.sum(dxc, axis=0, keepdims=True)

    def full(shape):
        return BS(shape, lambda i: (0,) * len(shape))

    def rev(i):
        return nblk - 1 - i

    blk0 = BS((LRU_T, c), lambda i: (rev(i), 0))
    blk1 = BS((LRU_T, c), lambda i: (rev(i), 1))
    halo = BS((HALO, c), lambda i: (jnp.maximum(rev(i) * hpb - 1, 0), 0))
    outs = pl.pallas_call(
        body, grid=(nblk,),
        in_specs=[blk0, blk0, halo, blk1, blk0, blk0, halo, BS((LRU_T, MEM_W), lambda i: (rev(i), 0)),
                  full((4, c)), full((N_BLK, HD, HD)), full((1, c)), full((N_BLK, HD, HD)), full((1, c)),
                  full((1, c))],
        out_specs=[BS((LRU_T, 2 * c + MEM_W), lambda i: (rev(i), 0)), full((4, c)), full((1, c)),
                   full((N_BLK, HD, HD)), full((1, c)), full((N_BLK, HD, HD)), full((1, c)), full((1, c))],
        out_shape=[SDS((s, 2 * c + MEM_W), MXU), SDS((4, c), F32), SDS((1, c), F32),
                   SDS((N_BLK, HD, HD), F32), SDS((1, c), F32), SDS((N_BLK, HD, HD), F32), SDS((1, c), F32),
                   SDS((1, c), F32)],
        scratch_shapes=[pltpu.VMEM((HALO, c), F32), pltpu.VMEM((HALO, c), F32), pltpu.VMEM((HALO, c), F32),
                        pltpu.VMEM((LRU_T, c), F32), pltpu.VMEM((LRU_T, c), F32), pltpu.VMEM((LRU_T, c), F32)],
        compiler_params=_cp("arbitrary"), name=name,
    )(dym, proj, proj, proj, xc, hl, hl, dqm, conv_w, wa_m, ba.reshape(1, c), wx_m, bx.reshape(1, c),
      lam.reshape(1, c))
    dproj, dcw, dcb, dwa, dba, dwx, dbx, dlam = outs
    return dproj, dcw, dcb.reshape(c), dwa, dba.reshape(c), dwx, dbx.reshape(c), dlam.reshape(c)


def _mem_probs(q, kv):
    heads = [slice(hh * HD, (hh + 1) * HD) for hh in range(MEM_HEADS)]
    sc = [_dot(q[:, sl], kv[:, sl], NT) * SCALE for sl in heads]
    e = [jnp.exp(s - jnp.max(s, axis=-1, keepdims=True)) for s in sc]
    return [x / jnp.sum(x, axis=-1, keepdims=True) for x in e]


def mem_attn_fwd(proj, q_col, kvm, name):
    s = proj.shape[0]
    tq = min(512, s)

    def body(q_ref, kv_ref, o_ref):
        q = q_ref[...].astype(MXU)
        kv = kv_ref[...]
        p = _mem_probs(q, kv)
        outs = [_dot(p[hh].astype(MXU), kv[:, MEM_W + hh * HD:MEM_W + (hh + 1) * HD]) for hh in range(MEM_HEADS)]
        o_ref[...] = jnp.concatenate(outs, axis=1).astype(o_ref.dtype)

    return pl.pallas_call(
        body, grid=(s // tq,),
        in_specs=[BS((tq, MEM_W), lambda i: (i, q_col)), BS((N_MEM, 2 * MEM_W), lambda i: (0, 0))],
        out_specs=BS((tq, MEM_W), lambda i: (i, 0)), out_shape=SDS((s, MEM_W), MXU),
        compiler_params=_cp("parallel"), name=name,
    )(proj, kvm)


def mem_attn_bwd(proj, q_col, kvm, dym, name):
    s = proj.shape[0]
    tq = min(512, s)

    def body(q_ref, kv_ref, do_ref, dq_ref, dkv_ref):
        @pl.when(pl.program_id(0) == 0)
        def _():
            dkv_ref[...] = jnp.zeros_like(dkv_ref)

        q = q_ref[...].astype(MXU)
        do = do_ref[...].astype(MXU)
        kv = kv_ref[...]
        heads = [slice(hh * HD, (hh + 1) * HD) for hh in range(MEM_HEADS)]
        p = _mem_probs(q, kv)
        dp = [_dot(do[:, sl], kv[:, MEM_W + hh * HD:MEM_W + (hh + 1) * HD], NT) for hh, sl in enumerate(heads)]
        ds = [(pp * (d - jnp.sum(pp * d, axis=-1, keepdims=True)) * SCALE).astype(MXU) for pp, d in zip(p, dp)]
        dq = [_dot(x, kv[:, sl]) for x, sl in zip(ds, heads)]
        dk = [_dot(x, q[:, sl], TN) for x, sl in zip(ds, heads)]
        dv = [_dot(pp.astype(MXU), do[:, sl], TN) for pp, sl in zip(p, heads)]
        dq_ref[...] = jnp.concatenate(dq, axis=1).astype(dq_ref.dtype)
        dkv_ref[...] += jnp.concatenate(dk + dv, axis=1)

    return pl.pallas_call(
        body, grid=(s // tq,),
        in_specs=[BS((tq, MEM_W), lambda i: (i, q_col)), BS((N_MEM, 2 * MEM_W), lambda i: (0, 0)),
                  BS((tq, MEM_W), lambda i: (i, MIX_W // MEM_W))],
        out_specs=[BS((tq, MEM_W), lambda i: (i, 0)), BS((N_MEM, 2 * MEM_W), lambda i: (0, 0))],
        out_shape=[SDS((s, MEM_W), MXU), SDS((N_MEM, 2 * MEM_W), F32)],
        compiler_params=_cp("arbitrary"), name=name,
    )(proj, kvm, dym)


def _score_prev(q, kp, n, slope_dil):
    qi = lax.broadcasted_iota(jnp.int32, (Q_BLOCK, Q_BLOCK), 0)
    ki = lax.broadcasted_iota(jnp.int32, (Q_BLOCK, Q_BLOCK), 1)
    rel = qi + Q_BLOCK - ki
    s = _dot(q, kp, NT) * SCALE - slope_dil * rel.astype(F32)
    return jnp.where((rel <= Q_BLOCK) & (n > 0), s, NEG_INF)


def _score_cur(q, kc, slope_dil):
    qi = lax.broadcasted_iota(jnp.int32, (Q_BLOCK, Q_BLOCK), 0)
    ki = lax.broadcasted_iota(jnp.int32, (Q_BLOCK, Q_BLOCK), 1)
    rel = qi - ki
    s = _dot(q, kc, NT) * SCALE - slope_dil * rel.astype(F32)
    return jnp.where(rel >= 0, s, NEG_INF)


def _dil_scores(q, kp, kc, n, slope_dil):
    return _score_prev(q, kp, n, slope_dil), _score_cur(q, kc, slope_dil)


def _slope_dil(gi, hh):
    head = 4 * gi + hh
    return DIL_GROUPS[gi][1] * 2.0 ** (-8.0 * (head + 1.0) / N_BLK)


def _dil_operands(proj, kv, gi):
    dil = DIL_GROUPS[gi][1]
    if dil == 1:
        return proj, kv, kv, (lambda r: gi), (lambda r: gi), (lambda r: MIX_W // MEM_W + gi)
    sub = proj.shape[0] // dil

    def view(a, col):
        return a[:, col:col + MEM_W].reshape(sub, dil * MEM_W)

    same = lambda r: r
    return view(proj, gi * MEM_W), view(kv, gi * MEM_W), view(kv, MIX_W + gi * MEM_W), same, same, same


def dil_attn_fwd(proj, kv, gi, name):
    dil = DIL_GROUPS[gi][1]
    s, pw = proj.shape
    sub = s // dil
    nb = sub // Q_BLOCK
    pair = 2 * Q_BLOCK

    def body(q_ref, kp_ref, kc_ref, vp_ref, vc_ref, o_ref, lse_ref):
        t = pl.program_id(1)
        q = q_ref[...].astype(MXU)
        k_prev, k_cur, v_prev, v_cur = kp_ref[...], kc_ref[...], vp_ref[...], vc_ref[...]
        units = []
        for b in range(2):
            rows = slice(b * Q_BLOCK, (b + 1) * Q_BLOCK)
            kp, vp = (k_prev, v_prev) if b == 0 else (k_cur[:Q_BLOCK], v_cur[:Q_BLOCK])
            for hh in range(4):
                sl = slice(hh * HD, (hh + 1) * HD)
                units.append((q[rows, sl], kp[:, sl], k_cur[rows, sl], vp[:, sl], v_cur[rows, sl], 2 * t + b, hh))
        sc = [_dil_scores(qh, kp, kc, n, _slope_dil(gi, hh)) for qh, kp, kc, _, _, n, hh in units]
        mx = [jnp.maximum(jnp.max(s_p, axis=-1, keepdims=True), jnp.max(s_c, axis=-1, keepdims=True))
              for s_p, s_c in sc]
        den = [jnp.sum(jnp.exp(s_p - m), axis=-1, keepdims=True) + jnp.sum(jnp.exp(s_c - m), axis=-1, keepdims=True)
               for (s_p, s_c), m in zip(sc, mx)]
        lse = [m + jnp.log(d) for m, d in zip(mx, den)]
        pr = [(jnp.exp(s_p - l).astype(MXU), jnp.exp(s_c - l).astype(MXU)) for (s_p, s_c), l in zip(sc, lse)]
        outs = [_dot(p_p, u[3]) + _dot(p_c, u[4]) for (p_p, p_c), u in zip(pr, units)]
        wide = [jnp.broadcast_to(l, (Q_BLOCK, HD)) for l in lse]
        o_ref[...] = jnp.concatenate([jnp.concatenate(outs[4 * b:4 * b + 4], axis=1) for b in range(2)], axis=0)
        lse_ref[...] = jnp.concatenate([jnp.concatenate(wide[4 * b:4 * b + 4], axis=1) for b in range(2)], axis=0)

    one, two = (Q_BLOCK, MEM_W), (pair, MEM_W)
    before = lambda t: jnp.maximum(2 * t - 1, 0)
    out = BS(two, lambda r, t: (t, r))
    qv, kview, vview, qcol, kcol, vcol = _dil_operands(proj, kv, gi)
    return pl.pallas_call(
        body, grid=(dil, nb // 2),
        in_specs=[BS(two, lambda r, t: (t, qcol(r))),
                  BS(one, lambda r, t: (before(t), kcol(r))), BS(two, lambda r, t: (t, kcol(r))),
                  BS(one, lambda r, t: (before(t), vcol(r))), BS(two, lambda r, t: (t, vcol(r)))],
        out_specs=[out, out],
        out_shape=[SDS((sub, dil * MEM_W), F32), SDS((sub, dil * MEM_W), F32)],
        compiler_params=_cp("parallel", "parallel"), name=name,
    )(qv, kview, kview, vview, vview)


def dil_attn_bwd(proj, kv, lse, do, dd, gi, name):
    dil = DIL_GROUPS[gi][1]
    sub = proj.shape[0] // dil
    nb = sub // Q_BLOCK
    pair = 2 * Q_BLOCK
    lo, hi = slice(0, Q_BLOCK), slice(Q_BLOCK, pair)

    def body(q_ref, qn_ref, kp_ref, kc_ref, vp_ref, vc_ref, lse_ref, lsen_ref, do_ref, don_ref, dd_ref, ddn_ref,
             dq_ref, dk_ref, dv_ref):
        t = pl.program_id(1)
        has_next = 2 * t + 2 < nb
        q, qn = q_ref[...].astype(MXU), qn_ref[...].astype(MXU)
        do_m, don_m = do_ref[...].astype(MXU), don_ref[...].astype(MXU)
        k_prev, k_cur, v_prev, v_cur = kp_ref[...], kc_ref[...], vp_ref[...], vc_ref[...]
        lse_v, lsen_v, dd_v, ddn_v = lse_ref[...], lsen_ref[...], dd_ref[...], ddn_ref[...]
        units = []
        for hh in range(4):
            sl = slice(hh * HD, (hh + 1) * HD)
            q0, q1, q2 = q[lo, sl], q[hi, sl], qn[:, sl]
            g0, g1, g2 = do_m[lo, sl], do_m[hi, sl], don_m[:, sl]
            l0, l1, l2 = lse_v[lo, sl], lse_v[hi, sl], lsen_v[:, sl]
            e0, e1, e2 = dd_v[lo, sl], dd_v[hi, sl], ddn_v[:, sl]
            ka, kb, kc = k_prev[:, sl], k_cur[lo, sl], k_cur[hi, sl]
            va, vb, vc = v_prev[:, sl], v_cur[lo, sl], v_cur[hi, sl]
            units += [(q0, g0, l0, e0, ka, va, 2 * t, None), (q0, g0, l0, e0, kb, vb, None, None),
                      (q1, g1, l1, e1, kb, vb, 2 * t + 1, None), (q1, g1, l1, e1, kc, vc, None, None),
                      (q2, g2, l2, e2, kc, vc, 2 * t + 2, has_next)]
        slopes = [_slope_dil(gi, hh) for hh in range(4) for _ in range(5)]
        sc = [_score_cur(u[0], u[4], sp) if u[6] is None else _score_prev(u[0], u[4], u[6], sp)
              for u, sp in zip(units, slopes)]
        dp = [_dot(u[1], u[5], NT) for u in units]
        pr = [jnp.exp(s - u[2]) if u[7] is None else jnp.where(u[7], jnp.exp(s - u[2]), 0.0) for s, u in zip(sc, units)]
        ds = [(p * (d + u[3]) * SCALE).astype(MXU) for p, d, u in zip(pr, dp, units)]
        pm = [p.astype(MXU) for p in pr]
        dq, dk, dv = [[], []], [[], []], [[], []]
        for hh in range(4):
            a, b, c, d, e = range(5 * hh, 5 * hh + 5)
            dq[0].append(_dot(ds[a], units[a][4]) + _dot(ds[b], units[b][4]))
            dq[1].append(_dot(ds[c], units[c][4]) + _dot(ds[d], units[d][4]))
        for hh in range(4):
            a, b, c, d, e = range(5 * hh, 5 * hh + 5)
            dk[0].append(_dot(ds[b], units[b][0], TN) + _dot(ds[c], units[c][0], TN))
            dk[1].append(_dot(ds[d], units[d][0], TN) + _dot(ds[e], units[e][0], TN))
            dv[0].append(_dot(pm[b], units[b][1], TN) + _dot(pm[c], units[c][1], TN))
            dv[1].append(_dot(pm[d], units[d][1], TN) + _dot(pm[e], units[e][1], TN))

        def tile(parts):
            return jnp.concatenate([jnp.concatenate(parts[0], axis=1), jnp.concatenate(parts[1], axis=1)], axis=0)

        dq_ref[...] = tile(dq).astype(dq_ref.dtype)
        dk_ref[...] = tile(dk)
        dv_ref[...] = tile(dv)

    one, two = (Q_BLOCK, MEM_W), (pair, MEM_W)
    before = lambda t: jnp.maximum(2 * t - 1, 0)
    after = lambda t: jnp.minimum(2 * t + 2, nb - 1)
    own, nxt = BS(two, lambda r, t: (t, r)), BS(one, lambda r, t: (after(t), r))
    qv, kview, vview, qcol, kcol, vcol = _dil_operands(proj, kv, gi)
    return pl.pallas_call(
        body, grid=(dil, nb // 2),
        in_specs=[BS(two, lambda r, t: (t, qcol(r))), BS(one, lambda r, t: (after(t), qcol(r))),
                  BS(one, lambda r, t: (before(t), kcol(r))), BS(two, lambda r, t: (t, kcol(r))),
                  BS(one, lambda r, t: (before(t), vcol(r))), BS(two, lambda r, t: (t, vcol(r))),
                  own, nxt, own, nxt, own, nxt],
        out_specs=[own, own, own],
        out_shape=[SDS((sub, dil * MEM_W), MXU), SDS((sub, dil * MEM_W), F32), SDS((sub, dil * MEM_W), F32)],
        compiler_params=_cp("parallel", "parallel"), name=name,
    )(qv, qv, kview, kview, vview, vview, lse, lse, do, do, dd, dd)


def _group_weights(lse_refs):
    l0, l1, l2 = (r[...] for r in lse_refs)
    mx = jnp.maximum(jnp.maximum(l0, l1), l2)
    e = [jnp.exp(l - mx) for l in (l0, l1, l2)]
    den = e[0] + e[1] + e[2]
    return [x / den for x in e]


def dil_mix_prep(o_list, lse_list, m, name):
    s = m.shape[0]
    tr = _row_tile(s)

    def body(o0, o1, o2, l0, l1, l2, m_ref, out_ref):
        w = _group_weights((l0, l1, l2))
        for g, o_ref in enumerate((o0, o1, o2)):
            out_ref[:, g * MEM_W:(g + 1) * MEM_W] = (o_ref[...] * w[g]).astype(out_ref.dtype)
        out_ref[:, MIX_W:] = m_ref[...]

    blk = BS((tr, MEM_W), lambda i: (i, 0))
    return pl.pallas_call(
        body, grid=(s // tr,), in_specs=[blk] * 7,
        out_specs=BS((tr, D), lambda i: (i, 0)), out_shape=SDS((s, D), MXU),
        compiler_params=_cp("parallel"), name=name,
    )(*o_list, *lse_list, m)


def dil_mix_bwd(dym, o_list, lse_list, name):
    s = dym.shape[0]
    tr = _row_tile(s)

    def body(da_ref, o0, o1, o2, l0, l1, l2, do0, do1, do2, dd0, dd1, dd2):
        w = _group_weights((l0, l1, l2))
        tot = None
        for g, (o_ref, do_ref) in enumerate(zip((o0, o1, o2), (do0, do1, do2))):
            da = da_ref[:, g * MEM_W:(g + 1) * MEM_W]
            do_ref[...] = da * w[g]
            x = da * o_ref[...]
            dw = jnp.concatenate(
                [jnp.broadcast_to(jnp.sum(x[:, hh * HD:(hh + 1) * HD], axis=-1, keepdims=True), (tr, HD))
                 for hh in range(4)], axis=1)
            tot = w[g] * dw if tot is None else tot + w[g] * dw
        for g, dd_ref in enumerate((dd0, dd1, dd2)):
            dd_ref[...] = -w[g] * tot

    blk = BS((tr, MEM_W), lambda i: (i, 0))
    outs = pl.pallas_call(
        body, grid=(s // tr,), in_specs=[BS((tr, MIX_W), lambda i: (i, 0))] + [blk] * 6,
        out_specs=[blk] * 6, out_shape=[SDS((s, MEM_W), F32)] * 6,
        compiler_params=_cp("parallel"), name=name,
    )(dym, *o_list, *lse_list)
    return outs[:3], outs[3:]


def sum_cast(parts, name):
    s = parts[0][0].shape[0]
    tr = _row_tile(s)
    flat = [a for p in parts for a in p]
    sizes = [len(p) for p in parts]

    def body(*refs):
        out_ref = refs[-1]
        pos = 0
        for j, n in enumerate(sizes):
            acc = refs[pos][...].astype(F32)
            for t in range(1, n):
                acc = acc + refs[pos + t][...].astype(F32)
            out_ref[:, j * MEM_W:(j + 1) * MEM_W] = acc.astype(out_ref.dtype)
            pos += n

    blk = BS((tr, MEM_W), lambda i: (i, 0))
    width = MEM_W * len(parts)
    return pl.pallas_call(
        body, grid=(s // tr,), in_specs=[blk] * len(flat),
        out_specs=BS((tr, width), lambda i: (i, 0)), out_shape=SDS((s, width), MXU),
        compiler_params=_cp("parallel"), name=name,
    )(*flat)


def add_n(arrs, name):
    rows, cols = arrs[0].shape
    tr = _row_tile(rows)

    def body(*refs):
        acc = refs[0][...]
        for r in refs[1:-1]:
            acc = acc + r[...]
        refs[-1][...] = acc

    blk = BS((tr, cols), lambda i: (i, 0))
    return pl.pallas_call(
        body, grid=(rows // tr,), in_specs=[blk] * len(arrs), out_specs=blk,
        out_shape=SDS((rows, cols), F32), compiler_params=_cp("parallel"), name=name,
    )(*arrs)


class _NoExchange:
    def hook(self, where, l, after):
        return []


def _fwd_bwd(x, mem, target, small, big, gs, gb, sched):
    s = x.shape[0]
    tm = min(1024, s)
    ts = min(2048, s)

    def after_hook(arr, where, l, after):
        toks = sched.hook(where, l, after)
        return tie(arr, toks, "tie_%s_%d" % (where, l)) if toks else arr

    h = x
    saved = []
    kv = None
    mem_n = None
    hn = norm_cast(h, small["a_pre_mix_g"][0], "pre_norm")
    for l in range(4):
        rec = l < 2
        p, j = ("a", l) if rec else ("b", l - 2)
        sv = {"h": h}
        hn = after_hook(hn, "fwd_begin", l, h)
        if mem_n is None:
            mem_n = norm_cast(mem, small["mem_norm_g"], "mem_norm")
        kvm = mm_nn(mem_n, big[p + "_w_mem_kv"][j], tm=N_MEM, tn=2 * MEM_W, tk=D, out_dtype=MXU, name="mem_kv")
        if rec:
            proj = mm_nn(hn, big["a_w_in"][j], tm=min(2 * tm, s), tn=896, tk=D, out_dtype=F32, name="rec_in")
            xc, hl = lru_fwd(proj, small["a_conv_w"][j], small["a_conv_b"][j], small["a_gate_a_w"][j],
                             small["a_gate_a_b"][j], small["a_gate_x_w"][j], small["a_gate_x_b"][j],
                             small["a_lambda"][j], "lru_fwd")
            m = mem_attn_fwd(proj, 2 * MIX_W // MEM_W, kvm, "rec_mem_attn")
            ym = lru_mix_prep(hl, proj, m, "lru_mix_prep")
            sv.update(xc=xc, hl=hl)
        else:
            proj = mm_nn(hn, big["b_w_in"][j], tm=tm, tn=D, tk=D, out_dtype=F32, name="dil_in")
            o_list, lse_list = [], []
            for gi in range(3):
                o, lse = dil_attn_fwd(proj, kv, gi, "dil_attn_fwd%d" % gi)
                o_list.append(o.reshape(s, MEM_W))
                lse_list.append(lse.reshape(s, MEM_W))
            m = mem_attn_fwd(proj, MIX_W // MEM_W, kvm, "dil_mem_attn")
            ym = dil_mix_prep(o_list, lse_list, m, "dil_mix_prep")
            sv.update(o=o_list, lse=lse_list)
        ym = after_hook(ym, "fwd_q1", l, ym)
        mix = mm_nn(ym, big[p + "_w_out"][j], tm=tm, tn=D, tk=D, out_dtype=F32, name="mix_out")
        h1, hn2 = resid_norm_next(h, mix, small[p + "_post_mix_g"][j], small[p + "_pre_ffn_g"][j], "post_pre_norm")
        hn2 = after_hook(hn2, "fwd_mid", l, mix)
        g, u, act = ffn_in_fwd(hn2, big[p + "_w_ffn_in"][j], "ffn_in")
        act = after_hook(act, "fwd_q3", l, u)
        y2 = mm_nn(act, big[p + "_w_ffn_out"][j], tm=tm // 2, tn=D, tk=D_FF // 2, out_dtype=F32, name="ffn_out")
        sv.update(kvm=kvm, hn=hn, proj=proj, ym=ym, mix=mix, h1=h1, hn2=hn2, g=g, u=u, act=act, y2=y2)
        saved.append(sv)
        if l < 3:
            pn, jn = ("a", l + 1) if l + 1 < 2 else ("b", l - 1)
            h, hn = resid_norm_next(h1, y2, small[p + "_post_ffn_g"][j], small[pn + "_pre_mix_g"][jn],
                                    "post_pre_norm")
        else:
            h = resid_norm(h1, y2, small[p + "_post_ffn_g"][j], "post_norm")
        sched.hook("fwd_end", l, h)
        if l == 1:
            h_kv = h
            kvn = norm_cast(h, small["kv_norm_g"], "pre_norm")
            kv = mm_nn(kvn, big["w_kv_shared"], tm=tm, tn=768, tk=D, out_dtype=MXU, name="kv_proj")

    loss_parts, dh = loss_head(h, target, "loss_head")

    def stack2(name, j, val):
        gs.setdefault(name, [None, None])[j] = val

    def stack2b(name, j, val):
        gb.setdefault(name, [None, None])[j] = val

    dkv_parts = []
    ahead = []
    dmem_parts = []
    dkvm = [None] * 4
    for l in (3, 2, 1, 0):
        rec = l < 2
        p, j = ("a", l) if rec else ("b", l - 2)
        sv = saved[l]
        if l == 1:
            dkv = sum_cast([(dkv_parts[0][c], dkv_parts[1][c]) for c in range(6)], "dkv_sum")
            dkvn = mm_nt([dkv], big["w_kv_shared"], tm=tm, tn=D, tk=768, out_dtype=MXU, name="kv_proj_dx")
            gb["w_kv_shared"] = mm_tn(kvn, [dkv], t1=D, tn=768, ts=ts, col_shards=True, name="kv_proj_dw")
            dh, gs["kv_norm_g"], *ahead = norm_bwd(h_kv, small["kv_norm_g"], dkvn, dh, F32, "pre_post_norm_bwd",
                                                   then=(sv["y2"], small["a_post_ffn_g"][1]))
        if ahead:
            dy2, dg = ahead
            ahead = []
        else:
            dy2, dg = norm_bwd(sv["y2"], small[p + "_post_ffn_g"][j], dh, None, MXU, "post_norm_bwd")
        dy2 = after_hook(dy2, "bwd_begin", l, dh)
        stack2(p + "_post_ffn_g", j, dg)
        dgg, dgu = ffn_act_bwd(dy2, big[p + "_w_ffn_out"][j], sv["g"], sv["u"], "ffn_act_bwd")
        dgg = after_hook(dgg, "bwd_mid1", l, dgu)
        stack2b(p + "_w_ffn_out", j, mm_tn(sv["act"], [dy2], t1=D_FF // 4, tn=D, ts=ts // 2, col_shards=False,
                                          name="ffn_out_dw"))
        dhn2 = mm_nt([dgg, dgu], big[p + "_w_ffn_in"][j], tm=tm // 2, tn=D, tk=D_FF // 2, out_dtype=MXU,
                     name="ffn_in_dx")
        stack2b(p + "_w_ffn_in", j, mm_tn(sv["hn2"], [dgg, dgu], t1=D // 2, tn=D_FF // 4, ts=ts, col_shards=True,
                                         name="ffn_in_dw"))
        dhn2 = after_hook(dhn2, "bwd_mid2", l, gb[p + "_w_ffn_in"][j])
        dh1, dg, dmix, dg_mix = norm_bwd(sv["h1"], small[p + "_pre_ffn_g"][j], dhn2, dh, F32, "pre_post_norm_bwd",
                                         then=(sv["mix"], small[p + "_post_mix_g"][j]))
        stack2(p + "_pre_ffn_g", j, dg)
        stack2(p + "_post_mix_g", j, dg_mix)
        dym = mm_nt([dmix], big[p + "_w_out"][j], tm=tm, tn=D, tk=D, out_dtype=F32, name="mix_out_dx")
        stack2b(p + "_w_out", j, mm_tn(sv["ym"], [dmix], t1=D, tn=1024, ts=ts, col_shards=False,
                                      name="mix_out_dw"))
        dym = after_hook(dym, "bwd_m1", l, gb[p + "_w_out"][j])
        if rec:
            dqm, dkvm[l] = mem_attn_bwd(sv["proj"], 2 * MIX_W // MEM_W, sv["kvm"], dym, "rec_mem_attn_bwd")
            dproj, dcw, dcb, dwa, dba, dwx, dbx, dlam = lru_bwd(
                dym, sv["proj"], sv["xc"], sv["hl"], dqm, small["a_conv_w"][j], small["a_gate_a_w"][j],
                small["a_gate_a_b"][j], small["a_gate_x_w"][j], small["a_gate_x_b"][j], small["a_lambda"][j],
                "lru_bwd")
            for nm, val in (("a_conv_w", dcw), ("a_conv_b", dcb), ("a_gate_a_w", dwa), ("a_gate_a_b", dba),
                            ("a_gate_x_w", dwx), ("a_gate_x_b", dbx), ("a_lambda", dlam)):
                stack2(nm, j, val)
            dhn = mm_nt([dproj], big["a_w_in"][j], tm=tm, tn=D, tk=896, out_dtype=MXU, name="rec_in_dx")
            stack2b("a_w_in", j, mm_tn(sv["hn"], [dproj], t1=D, tn=896, ts=ts, col_shards=True, name="rec_in_dw"))
        else:
            dqm, dkvm[l] = mem_attn_bwd(sv["proj"], MIX_W // MEM_W, sv["kvm"], dym, "dil_mem_attn_bwd")
            do_list, dd_list = dil_mix_bwd(dym, sv["o"], sv["lse"], "dil_mix_bwd")
            dq_list, dk_list, dv_list = [], [], []
            for gi in range(3):
                dil = DIL_GROUPS[gi][1]
                view = (s // dil, dil * MEM_W)
                dq, dk, dv = dil_attn_bwd(sv["proj"], kv, sv["lse"][gi].reshape(view), do_list[gi].reshape(view),
                                          dd_list[gi].reshape(view), gi, "dil_attn_bwd%d" % gi)
                dq_list.append(dq.reshape(s, MEM_W))
                dk_list.append(dk.reshape(s, MEM_W))
                dv_list.append(dv.reshape(s, MEM_W))
            dkv_parts.append(dk_list + dv_list)
            dproj = sum_cast([(a,) for a in dq_list + [dqm]], "dil_dproj")
            dhn = mm_nt([dproj], big["b_w_in"][j], tm=tm, tn=D, tk=D, out_dtype=MXU, name="dil_in_dx")
            stack2b("b_w_in", j, mm_tn(sv["hn"], [dproj], t1=D, tn=1024, ts=ts, col_shards=False, name="dil_in_dw"))
        dk_m = dkvm[l].astype(MXU)
        dmem_parts.append(mm_nt([dk_m], big[p + "_w_mem_kv"][j], tm=N_MEM, tn=D, tk=2 * MEM_W, out_dtype=F32,
                                name="mem_kv_dx"))
        stack2b(p + "_w_mem_kv", j, mm_tn(mem_n, [dk_m], t1=D, tn=2 * MEM_W, ts=N_MEM, col_shards=False,
                                         name="mem_kv_dw"))
        if l in (3, 1):
            pn, jn = ("b", 0) if l == 3 else ("a", 0)
            dh, dg, *ahead = norm_bwd(sv["h"], small[p + "_pre_mix_g"][j], dhn, dh1, F32, "pre_post_norm_bwd",
                                      then=(saved[l - 1]["y2"], small[pn + "_post_ffn_g"][jn]))
        else:
            dh, dg = norm_bwd(sv["h"], small[p + "_pre_mix_g"][j], dhn, dh1, F32, "pre_norm_bwd")
        stack2(p + "_pre_mix_g", j, dg)
        dh = after_hook(dh, "bwd_end", l, dh)

    _, gs["mem_norm_g"] = norm_bwd(mem, small["mem_norm_g"], add_n(dmem_parts, "dmem_sum"), None, F32,
                                   "mem_norm_bwd")
    return loss_parts, dh


ANY = pl.BlockSpec(memory_space=pl.ANY)
CHIP_FLIPS = (1, 2, 3)


def _coords():
    return lax.axis_index("x"), lax.axis_index("y"), lax.axis_index("c")


def _flip(x, y, m):
    return x ^ (m >> 1), y ^ (m & 1)


def _remote(src, dst, send_sems, recv_sems, k, device):
    return pltpu.make_async_remote_copy(src_ref=src, dst_ref=dst, send_sem=send_sems.at[k], recv_sem=recv_sems.at[k],
                                        device_id=device, device_id_type=MESH)


def _sum_rows_tile(rows, cols, itemsize=4):
    for tr in (512, 256, 128, 64, 32, 16):
        if rows % tr == 0 and tr * cols * itemsize <= 2 * 1024 * 1024:
            return tr
    raise ValueError((rows, cols))


def half_sum(g, got, name):
    _, r, cols = g.shape
    hr = r // 2
    tr = _sum_rows_tile(hr, cols, g.dtype.itemsize)

    def my_chip():
        return 2 * lax.axis_index("x") + lax.axis_index("y")

    def body(g_ref, got_ref, o_ref, own_ref):
        p = (g_ref[...].astype(F32) + got_ref[...].astype(F32)).astype(o_ref.dtype)
        o_ref[...] = p

        @pl.when(pl.program_id(1) == my_chip())
        def _():
            own_ref[...] = p

    out = SDS((N_CHIPS, hr, cols), jnp.bfloat16)
    return pl.pallas_call(
        body, grid=(hr // tr, N_CHIPS),
        in_specs=[BS((None, None, tr, cols), lambda i, s: (s, lax.axis_index("c"), i, 0)),
                  BS((None, tr, cols), lambda i, s: (s, i, 0))],
        out_specs=[BS((None, tr, cols), lambda i, s: (s, i, 0)),
                   BS((None, tr, cols), lambda i, s: (my_chip(), i, 0))],
        out_shape=[out, out], compiler_params=_cp("parallel", "arbitrary"), name=name,
    )(g.reshape(N_CHIPS, 2, hr, cols), got)


def slot_sum(slots, name):
    _, hr, cols = slots.shape
    tr = _sum_rows_tile(hr, cols)
    nblk = hr // tr

    def body(s_ref, o_ref):
        acc = s_ref[0].astype(F32)
        for p in range(1, N_CHIPS):
            acc = acc + s_ref[p].astype(F32)
        o_ref[...] = acc

    return pl.pallas_call(
        body, grid=(nblk,), in_specs=[BS((N_CHIPS, tr, cols), lambda i: (0, i, 0))],
        out_specs=BS((tr, cols), lambda i: (lax.axis_index("c") * nblk + i, 0)),
        out_shape=SDS((2 * hr, cols), F32), compiler_params=_cp("parallel"), name=name,
    )(slots)


HBM_SPEC = pl.BlockSpec(memory_space=pltpu.HBM)
SEM_SPEC = pl.BlockSpec(memory_space=pltpu.SEMAPHORE)
EFFECT = pltpu.SideEffectType.DATAFLOW_SIDE_EFFECTING


def split_start(name, bufs, plan, n_copies):
    nb = len(bufs)

    def body(*refs):
        send_sems, recv_sems = refs[nb], refs[nb + 1]
        for k, (src, dst, dev) in enumerate(plan(refs[:nb])):
            _remote(src, dst, send_sems, recv_sems, k, dev).start()
        refs[-1][...] = jnp.zeros_like(refs[-1])

    outs = pl.pallas_call(
        body, name=name,
        out_shape=(pltpu.SemaphoreType.DMA((n_copies,)), pltpu.SemaphoreType.DMA((n_copies,)),
                   *[pltpu.HBM(b.shape, b.dtype) for b in bufs], SDS((8, LANES), F32)),
        in_specs=[HBM_SPEC] * nb, out_specs=(SEM_SPEC, SEM_SPEC, *[HBM_SPEC] * nb, VM),
        input_output_aliases={i: 2 + i for i in range(nb)},
        compiler_params=pltpu.CompilerParams(has_side_effects=EFFECT),
    )(*[pltpu.with_memory_space_constraint(b, pltpu.HBM) for b in bufs])
    return outs[0], outs[1], list(outs[2:2 + nb]), outs[-1]


def split_wait(name, send_sems, recv_sems, bufs, after, plan):
    nb = len(bufs)

    def body(*refs):
        send_ref, recv_ref = refs[nb], refs[nb + 1]
        for k, (src, dst, dev) in enumerate(plan(refs[:nb])):
            cp = _remote(src, dst, send_ref, recv_ref, k, dev)
            cp.wait_send()
            cp.wait_recv()

    outs = pl.pallas_call(
        body, name=name, out_shape=[pltpu.HBM(b.shape, b.dtype) for b in bufs],
        in_specs=[HBM_SPEC] * nb + [SEM_SPEC, SEM_SPEC, ANY], out_specs=[HBM_SPEC] * nb,
        input_output_aliases={i: i for i in range(nb)},
        compiler_params=pltpu.CompilerParams(has_side_effects=EFFECT),
    )(*bufs, send_sems, recv_sems, after)
    return list(outs)


def tie(x, tokens, name):
    def body(*refs):
        pass

    return pl.pallas_call(
        body, name=name, out_shape=SDS(x.shape, x.dtype), in_specs=[ANY] * (1 + len(tokens)), out_specs=ANY,
        input_output_aliases={0: 0},
    )(x, *tokens)


def plan_gather_ici(n, rows):
    def plan(refs):
        x, y, c = _coords()
        me = 2 * x + y
        out = []
        for i in range(n):
            hr = rows[i] // 2
            mine = pl.ds(pl.multiple_of(c * hr, 8), hr)
            out.append((refs[i], refs[n + i].at[me], (x, y, 1 - c)))
            for m in CHIP_FLIPS:
                out.append((refs[i].at[mine], refs[n + i].at[me, mine], (*_flip(x, y, m), c)))
        return out
    return plan


def plan_gather_d2d(n, rows):
    def plan(refs):
        x, y, c = _coords()
        me = 2 * x + y
        out = []
        for i in range(n):
            hr = rows[i] // 2
            mine = pl.ds(pl.multiple_of(c * hr, 8), hr)
            for m in CHIP_FLIPS:
                slot = refs[i].at[me ^ m, mine]
                out.append((slot, slot, (x, y, 1 - c)))
        return out
    return plan


def plan_swap(n, rows):
    def plan(refs):
        x, y, c = _coords()
        out = []
        for i in range(n):
            hr = rows[i] // 2
            other = pl.ds(pl.multiple_of((1 - c) * hr, 8), hr)
            out.append((refs[i].at[pl.ds(0, N_CHIPS), other], refs[n + i], (x, y, 1 - c)))
        return out
    return plan


def plan_exchange(n):
    def plan(refs):
        x, y, c = _coords()
        me = 2 * x + y
        out = []
        for i in range(n):
            for m in CHIP_FLIPS:
                out.append((refs[i].at[me ^ m], refs[n + i].at[me], (*_flip(x, y, m), c)))
        return out
    return plan


def plan_share(n, rows):
    def plan(refs):
        x, y, c = _coords()
        out = []
        for i in range(n):
            hr = rows[i] // 2
            mine = refs[i].at[pl.ds(pl.multiple_of(c * hr, 8), hr)]
            out.append((mine, mine, (x, y, 1 - c)))
        return out
    return plan


VM = pl.BlockSpec(memory_space=pltpu.VMEM)


def small_gather(v, name):
    def body(v_ref, out_ref, send_sems, recv_sems):
        x, y, c = _coords()
        me = 2 * x + y
        out_ref[me] = v_ref[...]
        cps = []
        for j, m in enumerate(CHIP_FLIPS):
            cp = _remote(v_ref, out_ref.at[me], send_sems, recv_sems, j, (*_flip(x, y, m), c))
            cp.start()
            cps.append(cp)
        for cp in cps:
            cp.wait()

    return pl.pallas_call(
        body, in_specs=[VM], out_specs=VM, out_shape=SDS((N_CHIPS,) + v.shape, v.dtype),
        scratch_shapes=[pltpu.SemaphoreType.DMA((3,)), pltpu.SemaphoreType.DMA((3,))],
        compiler_params=pltpu.CompilerParams(vmem_limit_bytes=VMEM_LIMIT_BYTES), name=name,
    )(v)


def plan_small_swap(refs):
    x, y, c = _coords()
    return [(refs[0], refs[1], (x, y, 1 - c))]


def plan_small_exchange(refs):
    x, y, c = _coords()
    me = 2 * x + y
    return [(refs[0].at[me], refs[0].at[me], (*_flip(x, y, m), c)) for m in CHIP_FLIPS]


def small_pair(v, sib, name):
    rows, cols = v.shape
    tr = _sum_rows_tile(rows, cols)

    def body(v_ref, s_ref, o_ref):
        o_ref[...] = v_ref[...] + s_ref[...]

    blk = BS((tr, cols), lambda i: (i, 0))
    return pl.pallas_call(
        body, grid=(rows // tr,), in_specs=[blk, blk],
        out_specs=BS((None, tr, cols), lambda i: (2 * lax.axis_index("x") + lax.axis_index("y"), i, 0)),
        out_shape=SDS((N_CHIPS, rows, cols), F32), compiler_params=_cp("parallel"), name=name,
    )(v, sib)


def small_total(slots, name):
    _, rows, cols = slots.shape
    tr = _sum_rows_tile(rows, cols)

    def body(s_ref, o_ref):
        o_ref[...] = (s_ref[0] + s_ref[1]) + (s_ref[2] + s_ref[3])

    return pl.pallas_call(
        body, grid=(rows // tr,), in_specs=[BS((N_CHIPS, tr, cols), lambda i: (0, i, 0))],
        out_specs=BS((tr, cols), lambda i: (i, 0)), out_shape=SDS((rows, cols), F32),
        compiler_params=_cp("parallel"), name=name,
    )(slots)


def adamw(w, g_list, m, v, name):
    nl, rows, cols = w.shape
    tr = _sum_rows_tile(rows, cols) if rows % 16 == 0 else rows
    bc1 = 1.0 - ADAM_B1 ** ADAM_STEP
    bc2 = 1.0 - ADAM_B2 ** ADAM_STEP

    def body(*refs):
        w_ref, m_ref, v_ref = refs[:3]
        g_refs = refs[3:3 + nl]
        go_ref, d_ref, mo_ref, vo_ref = refs[3 + nl:]
        layer = pl.program_id(0)
        for l in range(nl):
            @pl.when(layer == l)
            def _(l=l):
                g = g_refs[l][...]
                m_new = ADAM_B1 * m_ref[...] + (1.0 - ADAM_B1) * g
                v_new = ADAM_B2 * v_ref[...] + (1.0 - ADAM_B2) * (g * g)
                m_hat = m_new / bc1
                v_hat = v_new / bc2
                go_ref[...] = g
                d_ref[...] = -ADAM_LR * (m_hat / (jnp.sqrt(v_hat) + ADAM_EPS) + ADAM_WD * w_ref[...])
                mo_ref[...] = m_new
                vo_ref[...] = v_new

    stk = BS((None, tr, cols), lambda l, i: (l, i, 0))
    flat = BS((tr, cols), lambda l, i: (i, 0))
    out = SDS((nl, rows, cols), F32)
    return pl.pallas_call(
        body, grid=(nl, rows // tr), in_specs=[stk] * 3 + [flat] * nl, out_specs=[stk] * 4,
        out_shape=[out] * 4, compiler_params=_cp("parallel", "parallel"), name=name,
    )(w, m, v, *g_list)


WEIGHTS = ["mem_norm_g", "a_pre_mix_g", "a_post_mix_g", "a_pre_ffn_g", "a_post_ffn_g", "a_w_in", "a_conv_w",
           "a_conv_b", "a_gate_a_w", "a_gate_a_b", "a_gate_x_w", "a_gate_x_b", "a_lambda", "a_w_mem_kv", "a_w_out",
           "a_w_ffn_in", "a_w_ffn_out", "kv_norm_g", "w_kv_shared", "b_pre_mix_g", "b_post_mix_g", "b_pre_ffn_g",
           "b_post_ffn_g", "b_w_in", "b_w_mem_kv", "b_w_out", "b_w_ffn_in", "b_w_ffn_out"]
BIG = {"a_w_in": True, "a_w_mem_kv": False, "a_w_out": False, "a_w_ffn_in": True, "a_w_ffn_out": False,
       "w_kv_shared": True, "b_w_in": False, "b_w_mem_kv": False, "b_w_out": False, "b_w_ffn_in": True,
       "b_w_ffn_out": False}
SHARDED_SMALL = ["a_pre_mix_g", "a_post_mix_g", "a_pre_ffn_g", "a_post_ffn_g", "a_conv_w", "a_conv_b", "a_gate_a_b",
                 "a_gate_x_b", "a_lambda"]
REPL_SMALL = ["mem_norm_g", "kv_norm_g", "b_pre_mix_g", "b_post_mix_g", "b_pre_ffn_g", "b_post_ffn_g", "a_gate_a_w",
              "a_gate_x_w"]
LANES = 128


def _pack(arrs, row_multiple=8):
    flat = jnp.concatenate([a.reshape(-1) for a in arrs])
    pad = -flat.shape[0] % (LANES * row_multiple)
    if pad:
        flat = jnp.concatenate([flat, jnp.zeros((pad,), flat.dtype)])
    return flat.reshape(-1, LANES)


def _unpack(packed, shapes):
    flat = packed.reshape(-1)
    out, pos = [], 0
    for sh in shapes:
        size = math.prod(sh)
        out.append(flat[pos:pos + size].reshape(sh))
        pos += size
    return out


def kernel(x, mem, mem_norm_g, a_pre_mix_g, a_post_mix_g, a_pre_ffn_g, a_post_ffn_g, a_w_in, a_conv_w, a_conv_b,
           a_gate_a_w, a_gate_a_b, a_gate_x_w, a_gate_x_b, a_lambda, a_w_mem_kv, a_w_out, a_w_ffn_in, a_w_ffn_out,
           kv_norm_g, w_kv_shared, b_pre_mix_g, b_post_mix_g, b_pre_ffn_g, b_post_ffn_g, b_w_in, b_w_mem_kv, b_w_out,
           b_w_ffn_in, b_w_ffn_out, loss_target, m_mem_norm_g, m_a_pre_mix_g, m_a_post_mix_g, m_a_pre_ffn_g,
           m_a_post_ffn_g, m_a_w_in, m_a_conv_w, m_a_conv_b, m_a_gate_a_w, m_a_gate_a_b, m_a_gate_x_w, m_a_gate_x_b,
           m_a_lambda, m_a_w_mem_kv, m_a_w_out, m_a_w_ffn_in, m_a_w_ffn_out, m_kv_norm_g, m_w_kv_shared, m_b_pre_mix_g,
           m_b_post_mix_g, m_b_pre_ffn_g, m_b_post_ffn_g, m_b_w_in, m_b_w_mem_kv, m_b_w_out, m_b_w_ffn_in, m_b_w_ffn_out,
           v_mem_norm_g, v_a_pre_mix_g, v_a_post_mix_g, v_a_pre_ffn_g, v_a_post_ffn_g, v_a_w_in, v_a_conv_w, v_a_conv_b,
           v_a_gate_a_w, v_a_gate_a_b, v_a_gate_x_w, v_a_gate_x_b, v_a_lambda, v_a_w_mem_kv, v_a_w_out, v_a_w_ffn_in,
           v_a_w_ffn_out, v_kv_norm_g, v_w_kv_shared, v_b_pre_mix_g, v_b_post_mix_g, v_b_pre_ffn_g, v_b_post_ffn_g,
           v_b_w_in, v_b_w_mem_kv, v_b_w_out, v_b_w_ffn_in, v_b_w_ffn_out):
    a = dict(locals())
    xi, yi, ci = _coords()
    chip = 2 * xi + yi

    got = small_gather(_pack([a[n] for n in SHARDED_SMALL]), "small_gather")
    per_chip = [_unpack(got[s], [a[n].shape for n in SHARDED_SMALL]) for s in range(N_CHIPS)]
    small = {n: jnp.concatenate([per_chip[s][k] for s in range(N_CHIPS)], axis=-1)
             for k, n in enumerate(SHARDED_SMALL)}
    small.update({n: a[n] for n in REPL_SMALL})

    groups = []
    for l in range(4):
        p, j = ("a", l) if l < 2 else ("b", l - 2)
        groups.append([(p + "_" + n, j) for n in ("w_in", "w_mem_kv", "w_out")])
        groups.append([(p + "_" + n, j) for n in ("w_ffn_in", "w_ffn_out")])
    groups[3].append(("w_kv_shared", None))
    big = {n: [None, None] for n in BIG if n != "w_kv_shared"}
    gs, gb = {}, {}
    reduced = {n: [None, None] for n in BIG if n != "w_kv_shared"}

    def put(store, n, j, val):
        if j is None:
            store[n] = val
        else:
            store[n][j] = val

    class Exchange:
        def __init__(self):
            self.state = {}

        def gather_ici(self, g):
            shards = [(a[n] if j is None else a[n][j]).astype(MXU) for n, j in groups[g]]
            rows = [sh.shape[0] for sh in shards]
            lands = [lax.empty((N_CHIPS,) + sh.shape, sh.dtype) for sh in shards]
            plan = plan_gather_ici(len(shards), rows)
            ss, rs, bufs, tok = split_start("gather_ici_%d" % g, shards + lands, plan, 4 * len(shards))
            self.state["g", g] = (ss, rs, bufs, plan, rows)
            return tok

        def gather_d2d(self, g, after):
            ss, rs, bufs, plan, rows = self.state.pop(("g", g))
            n = len(rows)
            outs = split_wait("gather_ici_wait_%d" % g, ss, rs, bufs, after, plan)[n:]
            plan = plan_gather_d2d(n, rows)
            ss, rs, bufs, tok = split_start("gather_d2d_%d" % g, outs, plan, 3 * n)
            self.state["g", g] = (ss, rs, bufs, plan)
            return tok

        def gather_done(self, g, after):
            ss, rs, bufs, plan = self.state.pop(("g", g))
            outs = split_wait("gather_d2d_wait_%d" % g, ss, rs, bufs, after, plan)
            for (n, j), w in zip(groups[g], outs):
                put(big, n, j, w if BIG[n] else w.reshape(-1, w.shape[-1]))

        def rs_swap(self, g):
            grads = []
            for n, j in groups[g]:
                gr = gb[n] if j is None else gb[n][j]
                grads.append(gr if BIG[n] else gr.reshape(N_CHIPS, gr.shape[0] // N_CHIPS, gr.shape[1]))
            rows = [gr.shape[1] for gr in grads]
            lands = [lax.empty((N_CHIPS, gr.shape[1] // 2, gr.shape[2]), gr.dtype) for gr in grads]
            plan = plan_swap(len(grads), rows)
            ss, rs, bufs, tok = split_start("rs_swap_%d" % g, grads + lands, plan, len(grads))
            self.state["r", g] = (ss, rs, bufs, plan, rows)
            return tok

        def rs_exchange(self, g, after):
            ss, rs, bufs, plan, rows = self.state.pop(("r", g))
            n = len(rows)
            bufs = split_wait("rs_swap_wait_%d" % g, ss, rs, bufs, after, plan)
            sums = [half_sum(gr, got, "rs_half_sum") for gr, got in zip(bufs[:n], bufs[n:])]
            plan = plan_exchange(n)
            ss, rs, bufs, tok = split_start("rs_exchange_%d" % g, [p for p, _ in sums] + [s for _, s in sums], plan,
                                            3 * n)
            self.state["r", g] = (ss, rs, bufs, plan, rows)
            return tok

        def rs_share(self, g, after):
            ss, rs, bufs, plan, rows = self.state.pop(("r", g))
            n = len(rows)
            slots = split_wait("rs_exchange_wait_%d" % g, ss, rs, bufs, after, plan)[n:]
            fulls = [slot_sum(s, "rs_slot_sum") for s in slots]
            plan = plan_share(n, rows)
            ss, rs, bufs, tok = split_start("rs_share_%d" % g, fulls, plan, n)
            self.state["r", g] = (ss, rs, bufs, plan)
            return tok

        def rs_done(self, g, after):
            ss, rs, bufs, plan = self.state.pop(("r", g))
            outs = split_wait("rs_share_wait_%d" % g, ss, rs, bufs, after, plan)
            for (n, j), r in zip(groups[g], outs):
                put(reduced, n, j, r)

        def hook(self, where, l, after):
            mix, ffn = 2 * l, 2 * l + 1
            toks = []
            if where == "fwd_begin":
                if l == 0:
                    tok = self.gather_ici(mix)
                    tok = self.gather_d2d(mix, tok)
                    self.gather_done(mix, tok)
                toks.append(self.gather_ici(ffn))
            elif where == "fwd_q1":
                toks.append(self.gather_d2d(ffn, after))
            elif where == "fwd_mid":
                self.gather_done(ffn, after)
                if l < 3:
                    toks.append(self.gather_ici(mix + 2))
            elif where == "fwd_q3":
                if l < 3:
                    toks.append(self.gather_d2d(mix + 2, after))
            elif where == "fwd_end":
                if l < 3:
                    self.gather_done(mix + 2, after)
            elif where == "bwd_begin":
                if l < 3:
                    self.rs_done(ffn + 2, after)
                    toks.append(self.rs_exchange(mix + 2, after))
            elif where == "bwd_mid1":
                if l < 3:
                    toks.append(self.rs_share(mix + 2, after))
            elif where == "bwd_mid2":
                if l < 3:
                    self.rs_done(mix + 2, after)
                toks.append(self.rs_swap(ffn))
            elif where == "bwd_m1":
                toks.append(self.rs_exchange(ffn, after))
            elif where == "bwd_end":
                toks.append(self.rs_share(ffn, after))
                toks.append(self.rs_swap(mix))
                if l == 0:
                    self.rs_done(ffn, toks[0])
                    tok = self.rs_exchange(mix, toks[1])
                    tok = self.rs_share(mix, adamw_big([n for n in BIG if n.startswith("b_")], tok))
                    self.rs_done(mix, tok)
                    toks = []
            else:
                raise ValueError(where)
            return toks

    res = {}

    def adamw_big(names, token=None):
        last = None
        for n in names:
            shape = a[n].shape
            rows, cols = shape[-2], shape[-1]
            stk = (-1, rows, cols)
            grads = reduced[n] if isinstance(reduced[n], list) else [reduced[n]]
            if token is not None:
                grads = [tie(grads[0], [token], "tie_adamw_" + n)] + grads[1:]
            outs = adamw(a[n].reshape(stk), grads, a["m_" + n].reshape(stk), a["v_" + n].reshape(stk), "adamw")
            res[n] = [o.reshape(shape) for o in outs]
            last = outs[1]
            token = last if token is not None else None
        return last

    loss_parts, dx = _fwd_bwd(x[0], mem[0], loss_target[0], small, big, gs, gb, Exchange())
    loss = lax.psum(jnp.sum(loss_parts) * (0.5 / D), ("x", "y", "c"))

    def full(n):
        g = gs[n]
        return jnp.stack(g) if isinstance(g, list) else g

    order = SHARDED_SMALL + REPL_SMALL
    full_shapes = [full(n).shape for n in order]
    pack = _pack([full(n) for n in order], 512)
    ss, rs, bufs, tok = split_start("small_swap", [pack, lax.empty(pack.shape, F32)], plan_small_swap, 1)
    mine_v, sib_v = split_wait("small_swap_wait", ss, rs, bufs, tok, plan_small_swap)
    ss, rs, bufs, tok = split_start("small_exchange", [small_pair(mine_v, sib_v, "small_pair")],
                                    plan_small_exchange, 3)
    last = adamw_big([n for n in BIG if n not in res], tok)
    slots = split_wait("small_exchange_wait", ss, rs, bufs, last, plan_small_exchange)[0]
    summed = _unpack(small_total(slots, "small_total"), full_shapes)
    mine = []
    for n, g in zip(order, summed):
        if n in SHARDED_SMALL:
            width = a[n].shape[-1]
            g = lax.dynamic_slice_in_dim(g, chip * width, width, axis=g.ndim - 1)
        mine.append(g.reshape(a[n].shape))
    shapes = [a[n].shape for n in order]
    rm = 512
    outs = adamw(_pack([a[n] for n in order], rm)[None], [_pack(mine, rm)],
                 _pack([a["m_" + n] for n in order], rm)[None], _pack([a["v_" + n] for n in order], rm)[None],
                 "adamw_small")
    unpacked = [_unpack(o[0], shapes) for o in outs]
    for k, n in enumerate(order):
        res[n] = [u[k] for u in unpacked]

    return (loss, dx[None], *[res[n][0] for n in WEIGHTS], *[res[n][1] for n in WEIGHTS],
            *[res[n][2] for n in WEIGHTS], *[res[n][3] for n in WEIGHTS])
```

```python
import math

import jax
import jax.numpy as jnp
from jax import lax
from jax.experimental import pallas as pl
from jax.experimental.pallas import tpu as pltpu

D = 2048
HD = 128
MEM_W = 512
MEM_HEADS = 4
MIX_W = D - MEM_W
N_BLK = MIX_W // HD
D_FF = 5632
N_MEM = 256
RMS_EPS = 1e-6
NEG_INF = -1e30
LRU_C = 8.0
DIL_GROUPS = ((128, 1), (512, 4), (2048, 16))
Q_BLOCK = 128
SCALE = HD ** -0.5
N_CHIPS = 4
MXU_COLS = 256
ACC_CHUNK = 2 * MXU_COLS

ADAM_LR = 0.001
ADAM_B1 = 0.9
ADAM_B2 = 0.999
ADAM_EPS = 1e-08
ADAM_WD = 0.01
ADAM_STEP = 10

MXU = jnp.bfloat16
F32 = jnp.float32
VMEM_LIMIT_BYTES = 56 * 1024 * 1024

BS = pl.BlockSpec
SDS = jax.ShapeDtypeStruct
MESH = pl.DeviceIdType.MESH


def _cp(*sem):
    return pltpu.CompilerParams(dimension_semantics=sem or None, vmem_limit_bytes=VMEM_LIMIT_BYTES)


def _dot(a, b, dn=((1,), (0,))):
    return lax.dot_general(a, b, (dn, ((), ())), preferred_element_type=F32)


def _div(i, n):
    return lax.div(i, jnp.int32(n))


def _rem(i, n):
    return lax.rem(i, jnp.int32(n))


NN = ((1,), (0,))
NT = ((1,), (1,))
TN = ((0,), (0,))


def _sigmoid(z):
    return 0.5 * jnp.tanh(0.5 * z) + 0.5


def _logistic(z):
    return 1.0 / (1.0 + jnp.exp(-z))


def _log1p_pos(u):
    return jnp.where(u < 1e-2, u * (1.0 - u * (0.5 - u * (1.0 / 3.0))), jnp.log(1.0 + u))


def _neg_expm1(z):
    return jnp.where(z > -1e-2, -z * (1.0 + z * (0.5 + z * (1.0 / 6.0))), 1.0 - jnp.exp(z))


def _softplus(z):
    return jnp.maximum(z, 0.0) + _log1p_pos(jnp.exp(-jnp.abs(z)))


_GELU_C = math.sqrt(2.0 / math.pi)


def _gelu_and_grad(x):
    x2 = x * x
    t = jnp.tanh(_GELU_C * (x + 0.044715 * x * x2))
    g = 0.5 * x * (1.0 + t)
    dg = 0.5 * (1.0 + t) + 0.5 * x * (1.0 - t * t) * _GELU_C * (1.0 + 3.0 * 0.044715 * x2)
    return g, dg


def _row_tile(rows):
    return min(512, rows)


def norm_cast(x, g, name):
    rows = x.shape[0]
    tr = _row_tile(rows)

    def body(x_ref, g_ref, o_ref):
        xv = x_ref[...]
        r = lax.rsqrt(jnp.mean(xv * xv, axis=-1, keepdims=True) + RMS_EPS)
        o_ref[...] = (xv * r * g_ref[...]).astype(o_ref.dtype)

    return pl.pallas_call(
        body, grid=(rows // tr,),
        in_specs=[BS((tr, D), lambda i: (i, 0)), BS((1, D), lambda i: (0, 0))],
        out_specs=BS((tr, D), lambda i: (i, 0)),
        out_shape=SDS((rows, D), MXU), compiler_params=_cp("parallel"), name=name,
    )(x, g.reshape(1, D))


def resid_norm(h, y, g, name):
    rows = h.shape[0]
    tr = _row_tile(rows)

    def body(h_ref, y_ref, g_ref, o_ref):
        yv = y_ref[...]
        r = lax.rsqrt(jnp.mean(yv * yv, axis=-1, keepdims=True) + RMS_EPS)
        o_ref[...] = h_ref[...] + yv * r * g_ref[...]

    return pl.pallas_call(
        body, grid=(rows // tr,),
        in_specs=[BS((tr, D), lambda i: (i, 0)), BS((tr, D), lambda i: (i, 0)), BS((1, D), lambda i: (0, 0))],
        out_specs=BS((tr, D), lambda i: (i, 0)),
        out_shape=SDS((rows, D), F32), compiler_params=_cp("parallel"), name=name,
    )(h, y, g.reshape(1, D))


def resid_norm_next(h, y, g, g_next, name):
    rows = h.shape[0]
    tr = _row_tile(rows)

    def body(h_ref, y_ref, g_ref, gn_ref, o_ref, n_ref):
        yv = y_ref[...]
        r = lax.rsqrt(jnp.mean(yv * yv, axis=-1, keepdims=True) + RMS_EPS)
        hv = h_ref[...] + yv * r * g_ref[...]
        o_ref[...] = hv
        r2 = lax.rsqrt(jnp.mean(hv * hv, axis=-1, keepdims=True) + RMS_EPS)
        n_ref[...] = (hv * r2 * gn_ref[...]).astype(n_ref.dtype)

    row = BS((tr, D), lambda i: (i, 0))
    vec = BS((1, D), lambda i: (0, 0))
    return pl.pallas_call(
        body, grid=(rows // tr,), in_specs=[row, row, vec, vec], out_specs=[row, row],
        out_shape=[SDS((rows, D), F32), SDS((rows, D), MXU)], compiler_params=_cp("parallel"), name=name,
    )(h, y, g.reshape(1, D), g_next.reshape(1, D))


def _norm_bwd_rows(xv, gv, dyv):
    r = lax.rsqrt(jnp.mean(xv * xv, axis=-1, keepdims=True) + RMS_EPS)
    xhat = xv * r
    dxhat = dyv * gv
    dx = r * (dxhat - xhat * jnp.mean(dxhat * xhat, axis=-1, keepdims=True))
    return dx, jnp.sum(dyv * xhat, axis=0, keepdims=True)


def norm_bwd(x, g, dy, res, out_dtype, name, then=None):
    rows = x.shape[0]
    tr = _row_tile(rows)
    has_res = res is not None
    n_in = 3 + has_res + (2 if then else 0)

    def body(*refs):
        x_ref, g_ref, dy_ref = refs[:3]
        dx_ref, dg_ref = refs[n_in], refs[n_in + 1]
        dx, dg = _norm_bwd_rows(x_ref[...], g_ref[...], dy_ref[...].astype(F32))
        if has_res:
            dx = dx + refs[3][...]
        dx_ref[...] = dx.astype(dx_ref.dtype)
        first = pl.program_id(0) == 0

        @pl.when(first)
        def _():
            dg_ref[...] = jnp.zeros_like(dg_ref)

        dg_ref[...] += dg
        if then:
            x2_ref, g2_ref = refs[n_in - 2], refs[n_in - 1]
            dx2_ref, dg2_ref = refs[n_in + 2], refs[n_in + 3]
            dx2, dg2 = _norm_bwd_rows(x2_ref[...], g2_ref[...], dx)
            dx2_ref[...] = dx2.astype(dx2_ref.dtype)

            @pl.when(first)
            def _():
                dg2_ref[...] = jnp.zeros_like(dg2_ref)

            dg2_ref[...] += dg2

    row = BS((tr, D), lambda i: (i, 0))
    vec = BS((1, D), lambda i: (0, 0))
    ins = [x, g.reshape(1, D), dy] + ([res] if has_res else []) + ([then[0], then[1].reshape(1, D)] if then else [])
    outs = pl.pallas_call(
        body, grid=(rows // tr,),
        in_specs=[row, vec, row] + ([row] if has_res else []) + ([row, vec] if then else []),
        out_specs=[row, vec] + ([row, vec] if then else []),
        out_shape=[SDS((rows, D), out_dtype), SDS((1, D), F32)] + ([SDS((rows, D), MXU), SDS((1, D), F32)] if then else []),
        compiler_params=_cp("arbitrary"), name=name,
    )(*ins)
    if then:
        return outs[0], outs[1].reshape(D), outs[2], outs[3].reshape(D)
    return outs[0], outs[1].reshape(D)


def loss_head(y, target, name):
    rows = y.shape[0]
    tr = _row_tile(rows)

    def body(y_ref, t_ref, dy_ref, acc_ref):
        err = y_ref[...] - t_ref[...]
        dy_ref[...] = err * (1.0 / D)

        @pl.when(pl.program_id(0) == 0)
        def _():
            acc_ref[...] = jnp.zeros_like(acc_ref)

        acc_ref[...] += jnp.sum(err * err, axis=0, keepdims=True)

    row = BS((tr, D), lambda i: (i, 0))
    dy, acc = pl.pallas_call(
        body, grid=(rows // tr,), in_specs=[row, row],
        out_specs=[row, BS((1, D), lambda i: (0, 0))],
        out_shape=[SDS((rows, D), F32), SDS((1, D), F32)],
        compiler_params=_cp("arbitrary"), name=name,
    )(y, target)
    return acc, dy


def _mm_call(ins, in_specs, pick, dn, grid, o_spec, out_sds, name):
    gk = grid[2]
    n_in = len(ins)

    def body(*refs):
        o_ref = refs[n_in]
        k = pl.program_id(2)

        def step(a_ref, b_ref):
            acc = o_ref if (out_sds.dtype == F32 or gk == 1) else refs[n_in + 1]
            width = acc.shape[-1]
            if dn == TN or width <= ACC_CHUNK:
                chunks = [(0, width)]
            else:
                chunks = [(c0, min(c0 + ACC_CHUNK, width)) for c0 in range(0, width, ACC_CHUNK)]

            def sweep(first):
                a = a_ref[...]
                pending = None
                for c0, c1 in chunks:
                    p = _dot(a, b_ref[c0:c1, :] if dn == NT else b_ref[:, c0:c1], dn)
                    if pending is not None:
                        put(first, *pending)
                    pending = (c0, c1, p)
                put(first, *pending)

            def put(first, c0, c1, p):
                if first:
                    acc[:, c0:c1] = p.astype(acc.dtype)
                else:
                    acc[:, c0:c1] += p

            if gk == 1:
                sweep(True)
                return

            @pl.when(k == 0)
            def _():
                sweep(True)

            @pl.when(k > 0)
            def _():
                sweep(False)

            if acc is not o_ref:
                @pl.when(k == gk - 1)
                def _():
                    o_ref[...] = acc[...].astype(o_ref.dtype)

        pick(refs[:n_in], k, step)

    scratch = []
    if gk > 1 and out_sds.dtype != F32:
        scratch = [pltpu.VMEM(o_spec.block_shape[-2:], F32)]
    return pl.pallas_call(
        body, grid=grid, in_specs=in_specs, out_specs=o_spec, out_shape=out_sds,
        scratch_shapes=scratch, compiler_params=_cp("parallel", "parallel", "arbitrary"), name=name,
    )(*ins)


def _pick2(refs, k, step):
    step(refs[0], refs[1])


def mm_nn(a, w, *, tm, tn, tk, out_dtype, name):
    m, kdim = a.shape
    if w.ndim == 3:
        c = w.shape[2]
        n = N_CHIPS * c
        per = c // tn
        b_spec = BS((None, tk, tn), lambda i, j, k: (_div(j, per), k, _rem(j, per)))
    else:
        n = w.shape[1]
        b_spec = BS((tk, tn), lambda i, j, k: (k, j))
    grid = (m // tm, n // tn, kdim // tk)
    return _mm_call([a, w], [BS((tm, tk), lambda i, j, k: (i, k)), b_spec], _pick2, NN, grid,
                    BS((tm, tn), lambda i, j, k: (i, j)), SDS((m, n), out_dtype), name)


def mm_nt(a_list, w, *, tm, tn, tk, out_dtype, name):
    m = a_list[0].shape[0]
    ka = a_list[0].shape[1]
    n_a = len(a_list)
    kdim = ka * n_a
    if w.ndim == 3:
        c = w.shape[2]
        n = w.shape[1]
        per = c // tk
        b_spec = BS((None, tn, tk), lambda i, j, k: (_div(k, per), j, _rem(k, per)))
    else:
        n = w.shape[0]
        b_spec = BS((tn, tk), lambda i, j, k: (j, k))
    gk = kdim // tk
    half = gk // n_a
    grid = (m // tm, n // tn, gk)
    if n_a == 1:
        a_specs = [BS((tm, tk), lambda i, j, k: (i, k))]
        pick = lambda refs, k, step: step(refs[0], refs[1])
    else:
        a_specs = [BS((tm, tk), lambda i, j, k: (i, jnp.minimum(k, half - 1))),
                   BS((tm, tk), lambda i, j, k: (i, jnp.maximum(k - half, 0)))]

        def pick(refs, k, step):
            @pl.when(k < half)
            def _():
                step(refs[0], refs[2])

            @pl.when(k >= half)
            def _():
                step(refs[1], refs[2])

    return _mm_call(list(a_list) + [w], a_specs + [b_spec], pick, NT, grid,
                    BS((tm, tn), lambda i, j, k: (i, j)), SDS((m, n), out_dtype), name)


def mm_tn(a, b_list, *, t1, tn, ts, col_shards, name):
    s, k1 = a.shape
    nb = b_list[0].shape[1]
    n_b = len(b_list)
    n = nb * n_b
    gn = n // tn
    half = gn // n_b
    grid = (k1 // t1, gn, s // ts)
    if col_shards:
        c = n // N_CHIPS
        per = c // tn
        o_spec = BS((None, t1, tn), lambda i, j, k: (_div(j, per), i, _rem(j, per)))
        out_sds = SDS((N_CHIPS, k1, c), MXU)
    else:
        o_spec = BS((t1, tn), lambda i, j, k: (i, j))
        out_sds = SDS((k1, n), MXU)
    a_spec = BS((ts, t1), lambda i, j, k: (k, i))
    if n_b == 1:
        b_specs = [BS((ts, tn), lambda i, j, k: (k, j))]
        pick = lambda refs, k, step: step(refs[0], refs[1])
    else:
        b_specs = [BS((ts, tn), lambda i, j, k: (jnp.where(j < half, k, 0), jnp.minimum(j, half - 1))),
                   BS((ts, tn), lambda i, j, k: (jnp.where(j >= half, k, 0), jnp.maximum(j - half, 0)))]

        def pick(refs, k, step):
            j = pl.program_id(1)

            @pl.when(j < half)
            def _():
                step(refs[0], refs[1])

            @pl.when(j >= half)
            def _():
                step(refs[0], refs[2])

    return _mm_call([a] + list(b_list), [a_spec] + b_specs, pick, TN, grid, o_spec, out_sds, name)


def ffn_in_fwd(hn, w, name):
    s = hn.shape[0]
    tm = min(512, s)
    tn = D_FF // 4

    def tail(dag_ref, dau_ref, act_ref, c0, c1, g, u):
        sg = _sigmoid(g)
        silu = g * sg
        dag_ref[:, c0:c1] = (u * sg * (1.0 + g * (1.0 - sg))).astype(dag_ref.dtype)
        dau_ref[:, c0:c1] = silu.astype(dau_ref.dtype)
        act_ref[:, c0:c1] = (silu * u).astype(act_ref.dtype)

    def body(a_ref, wg_ref, wu_ref, dag_ref, dau_ref, act_ref):
        a = a_ref[...]
        pending = None
        for c0 in range(0, tn, ACC_CHUNK):
            c1 = min(c0 + ACC_CHUNK, tn)
            g = _dot(a, wg_ref[:, c0:c1])
            u = _dot(a, wu_ref[:, c0:c1])
            if pending is not None:
                tail(dag_ref, dau_ref, act_ref, *pending)
            pending = (c0, c1, g, u)
        tail(dag_ref, dau_ref, act_ref, *pending)

    tile = BS((tm, tn), lambda j, i: (i, j))
    return pl.pallas_call(
        body, grid=(4, s // tm),
        in_specs=[BS((tm, D), lambda j, i: (i, 0)),
                  BS((None, D, tn), lambda j, i: (_div(j, 2), 0, _rem(j, 2))),
                  BS((None, D, tn), lambda j, i: (2 + _div(j, 2), 0, _rem(j, 2)))],
        out_specs=[tile, tile, tile],
        out_shape=[SDS((s, D_FF), MXU), SDS((s, D_FF), MXU), SDS((s, D_FF), MXU)],
        compiler_params=_cp("parallel", "parallel"), name=name,
    )(hn, w, w)


def ffn_act_bwd(dy, w_out, dag, dau, name):
    s = dy.shape[0]
    tm = min(1024, s)
    tn = D_FF // 4

    def body(dy_ref, w_ref, dag_ref, dau_ref, dg_ref, du_ref):
        def tail(c0, c1, dact):
            dg_ref[:, c0:c1] = (dact * dag_ref[:, c0:c1].astype(F32)).astype(dg_ref.dtype)
            du_ref[:, c0:c1] = (dact * dau_ref[:, c0:c1].astype(F32)).astype(du_ref.dtype)

        dy = dy_ref[...]
        pending = None
        for c0 in range(0, tn, ACC_CHUNK):
            c1 = min(c0 + ACC_CHUNK, tn)
            dact = _dot(dy, w_ref[c0:c1, :], NT)
            if pending is not None:
                tail(*pending)
            pending = (c0, c1, dact)
        tail(*pending)

    tile = BS((tm, tn), lambda j, i: (i, j))
    return pl.pallas_call(
        body, grid=(4, s // tm),
        in_specs=[BS((tm, D), lambda j, i: (i, 0)), BS((tn, D), lambda j, i: (j, 0)), tile, tile],
        out_specs=[tile, tile],
        out_shape=[SDS((s, D_FF), MXU), SDS((s, D_FF), MXU)],
        compiler_params=_cp("parallel", "parallel"), name=name,
    )(dy, w_out, dag, dau)


LRU_T = 256
HALO = 8


def _shift_down(x, k, fill):
    rows = x.shape[0]
    idx = lax.broadcasted_iota(jnp.int32, x.shape, 0)
    return jnp.where(idx < k, fill, pltpu.roll(x, k, 0))


def _shift_up(x, k, fill):
    rows = x.shape[0]
    idx = lax.broadcasted_iota(jnp.int32, x.shape, 0)
    return jnp.where(idx >= rows - k, fill, pltpu.roll(x, rows - k, 0))


def _scan_block(a, b, carry, reverse):
    rows, cols = a.shape
    sub = 8
    in_group = lax.broadcasted_iota(jnp.int32, a.shape, 0) % sub
    for sh in (1, 2, 4):
        if reverse:
            a_s, b_s, ok = pltpu.roll(a, rows - sh, 0), pltpu.roll(b, rows - sh, 0), in_group < sub - sh
        else:
            a_s, b_s, ok = pltpu.roll(a, sh, 0), pltpu.roll(b, sh, 0), in_group >= sh
        b = jnp.where(ok, a * b_s + b, b)
        a = jnp.where(ok, a * a_s, a)
    groups = list(range(rows // sub))
    edge = 0 if reverse else sub - 1
    carry_in = {}
    for v in (reversed(groups) if reverse else groups):
        carry_in[v] = carry
        row = sub * v + edge
        carry = b[row:row + 1, :] + a[row:row + 1, :] * carry
    cin = jnp.concatenate([jnp.broadcast_to(carry_in[v], (sub, cols)) for v in groups], axis=0)
    return b + a * cin


def _conv_taps(xcat):
    rows = xcat.shape[0]
    taps = []
    for k in range(4):
        off = HALO - 3 + k
        taps.append(xcat[off:off + LRU_T] if off == HALO else pltpu.roll(xcat, rows - off, 0)[:LRU_T])
    return taps


def _gates(xc, wa_ref, ba, wx_ref, bx, lam, za_ref, zx_ref):
    xm = xc.astype(MXU)
    for n in range(N_BLK):
        sl = slice(n * HD, (n + 1) * HD)
        za_ref[:, sl] = _dot(xm[:, sl], wa_ref[n])
        zx_ref[:, sl] = _dot(xm[:, sl], wx_ref[n])
    ra = _logistic(za_ref[...] + ba)
    ii = _logistic(zx_ref[...] + bx)
    sp = _softplus(-lam)
    log_a = -LRU_C * ra * sp
    a = jnp.exp(log_a)
    mult = jnp.sqrt(_neg_expm1(2.0 * log_a))
    return ra, ii, sp, a, mult


def lru_fwd(proj, conv_w, conv_b, wa, ba, wx, bx, lam, name):
    s = proj.shape[0]
    c = MIX_W
    nblk = s // LRU_T
    hpb = LRU_T // HALO

    def body(x_ref, halo_ref, cw_ref, cb_ref, wa_ref, ba_ref, wx_ref, bx_ref, lam_ref,
             xc_ref, h_ref, carry, za_ref, zx_ref):
        i = pl.program_id(0)

        @pl.when(i == 0)
        def _():
            carry[...] = jnp.zeros_like(carry)

        halo = jnp.where(i == 0, 0.0, halo_ref[...])
        xcat = jnp.concatenate([halo, x_ref[...]], axis=0)
        taps = _conv_taps(xcat)
        xc = cb_ref[...] + sum(cw_ref[k:k + 1, :] * taps[k] for k in range(4))
        xc_ref[...] = xc
        _, ii, _, a, mult = _gates(xc, wa_ref, ba_ref[...], wx_ref, bx_ref[...], lam_ref[...], za_ref, zx_ref)
        h = _scan_block(a, mult * (ii * xc), carry[HALO - 1:HALO, :], False)
        h_ref[...] = h
        carry[...] = h[LRU_T - HALO:, :]

    def full(shape):
        return BS(shape, lambda i: (0,) * len(shape))

    blk = BS((LRU_T, c), lambda i: (i, 0))
    return pl.pallas_call(
        body, grid=(nblk,),
        in_specs=[blk, BS((HALO, c), lambda i: (jnp.maximum(i * hpb - 1, 0), 0)),
                  full((4, c)), full((1, c)), full((N_BLK, HD, HD)), full((1, c)),
                  full((N_BLK, HD, HD)), full((1, c)), full((1, c))],
        out_specs=[blk, blk],
        out_shape=[SDS((s, c), F32), SDS((s, c), F32)],
        scratch_shapes=[pltpu.VMEM((HALO, c), F32), pltpu.VMEM((LRU_T, c), F32), pltpu.VMEM((LRU_T, c), F32)],
        compiler_params=_cp("arbitrary"), name=name,
    )(proj, proj, conv_w, conv_b.reshape(1, c), wa.astype(MXU), ba.reshape(1, c), wx.astype(MXU),
      bx.reshape(1, c), lam.reshape(1, c))


def lru_mix_prep(h, proj, m, name):
    s = h.shape[0]
    tr = _row_tile(s)

    def body(h_ref, gb_ref, m_ref, o_ref):
        ge, _ = _gelu_and_grad(gb_ref[...])
        o_ref[:, :MIX_W] = (h_ref[...] * ge).astype(o_ref.dtype)
        o_ref[:, MIX_W:] = m_ref[...]

    return pl.pallas_call(
        body, grid=(s // tr,),
        in_specs=[BS((tr, MIX_W), lambda i: (i, 0)), BS((tr, MIX_W), lambda i: (i, 1)),
                  BS((tr, MEM_W), lambda i: (i, 0))],
        out_specs=BS((tr, D), lambda i: (i, 0)), out_shape=SDS((s, D), MXU),
        compiler_params=_cp("parallel"), name=name,
    )(h, proj, m)


def lru_bwd(dym, proj, xc, hl, dqm, conv_w, wa, ba, wx, bx, lam, name):
    s = proj.shape[0]
    c = MIX_W
    nblk = s // LRU_T
    hpb = LRU_T // HALO
    wa_m = wa.astype(MXU)
    wx_m = wx.astype(MXU)

    def body(dy_ref, x_ref, xhalo_ref, gb_ref, xc_ref, h_ref, hhalo_ref, dqm_ref,
             cw_ref, wa_ref, ba_ref, wx_ref, bx_ref, lam_ref,
             dproj_ref, dcw_ref, dcb_ref, dwa_ref, dba_ref, dwx_ref, dbx_ref, dlam_ref,
             g_next, a_next, dxc_next, za_ref, zx_ref, dxc_ref):
        i = pl.program_id(0)

        @pl.when(i == 0)
        def _():
            g_next[...] = jnp.zeros_like(g_next)
            a_next[...] = jnp.zeros_like(a_next)
            dxc_next[...] = jnp.zeros_like(dxc_next)
            for r in (dcw_ref, dcb_ref, dwa_ref, dba_ref, dwx_ref, dbx_ref, dlam_ref):
                r[...] = jnp.zeros_like(r)

        first = i == nblk - 1
        xc = xc_ref[...]
        lam = lam_ref[...]
        ra, ii, sp, a, mult = _gates(xc, wa_ref, ba_ref[...], wx_ref, bx_ref[...], lam, za_ref, zx_ref)
        hl_v = h_ref[...]
        ge, dge = _gelu_and_grad(gb_ref[...])
        dyl = dy_ref[...]
        dhl = dyl * ge
        dproj_ref[:, c:2 * c] = (dyl * hl_v * dge).astype(dproj_ref.dtype)
        dproj_ref[:, 2 * c:] = dqm_ref[...]

        an = _shift_up(a, 1, 0.0)
        last_row = lax.broadcasted_iota(jnp.int32, a.shape, 0) == LRU_T - 1
        an = jnp.where(last_row, a_next[0:1, :], an)
        g = _scan_block(an, dhl, g_next[0:1, :], True)
        g_next[...] = g[:HALO, :]
        a_next[...] = a[:HALO, :]

        hhalo = jnp.where(first, 0.0, hhalo_ref[...])
        h_prev = _shift_down(hl_v, 1, 0.0)
        first_row = lax.broadcasted_iota(jnp.int32, a.shape, 0) == 0
        h_prev = jnp.where(first_row, hhalo[HALO - 1:HALO, :], h_prev)
        da = g * h_prev
        ixc = ii * xc
        dmult = g * ixc
        dii = g * mult * xc
        dxc = g * mult * ii
        dlog_a = (da - dmult * a / mult) * a
        dra = dlog_a * (-LRU_C) * sp
        dlam_ref[...] += jnp.sum(dlog_a * ra, axis=0, keepdims=True) * (LRU_C * _logistic(-lam))
        dza = dra * ra * (1.0 - ra)
        dzx = dii * ii * (1.0 - ii)
        dba_ref[...] += jnp.sum(dza, axis=0, keepdims=True)
        dbx_ref[...] += jnp.sum(dzx, axis=0, keepdims=True)
        xm = xc.astype(MXU)
        dza_m = dza.astype(MXU)
        dzx_m = dzx.astype(MXU)
        for n in range(N_BLK):
            sl = slice(n * HD, (n + 1) * HD)
            dwa_ref[n] += _dot(xm[:, sl], dza_m[:, sl], TN)
            dwx_ref[n] += _dot(xm[:, sl], dzx_m[:, sl], TN)
            dxc_ref[:, sl] = _dot(dza_m[:, sl], wa_ref[n], NT) + _dot(dzx_m[:, sl], wx_ref[n], NT)
        dxc = dxc + dxc_ref[...]

        dcat = jnp.concatenate([dxc, dxc_next[...]], axis=0)
        rows = dcat.shape[0]
        dxb = cw_ref[3:4, :] * dxc
        for k in range(3):
            dxb = dxb + cw_ref[k:k + 1, :] * pltpu.roll(dcat, rows - (3 - k), 0)[:LRU_T]
        dproj_ref[:, :c] = dxb.astype(dproj_ref.dtype)
        dxc_next[...] = dxc[:HALO, :]

        xhalo = jnp.where(first, 0.0, xhalo_ref[...])
        taps = _conv_taps(jnp.concatenate([xhalo, x_ref[...]], axis=0))
        for k in range(4):
            dcw_ref[k:k + 1, :] += jnp.sum(dxc * taps[k], axis=0, keepdims=True)
        dcb_ref[...] += jnp.sum(dxc, axis=0, keepdims=True)

    def full(shape):
        return BS(shape, lambda i: (0,) * len(shape))

    def rev(i):
        return nblk - 1 - i

    blk0 = BS((LRU_T, c), lambda i: (rev(i), 0))
    blk1 = BS((LRU_T, c), lambda i: (rev(i), 1))
    halo = BS((HALO, c), lambda i: (jnp.maximum(rev(i) * hpb - 1, 0), 0))
    outs = pl.pallas_call(
        body, grid=(nblk,),
        in_specs=[blk0, blk0, halo, blk1, blk0, blk0, halo, BS((LRU_T, MEM_W), lambda i: (rev(i), 0)),
                  full((4, c)), full((N_BLK, HD, HD)), full((1, c)), full((N_BLK, HD, HD)), full((1, c)),
                  full((1, c))],
        out_specs=[BS((LRU_T, 2 * c + MEM_W), lambda i: (rev(i), 0)), full((4, c)), full((1, c)),
                   full((N_BLK, HD, HD)), full((1, c)), full((N_BLK, HD, HD)), full((1, c)), full((1, c))],
        out_shape=[SDS((s, 2 * c + MEM_W), MXU), SDS((4, c), F32), SDS((1, c), F32),
                   SDS((N_BLK, HD, HD), F32), SDS((1, c), F32), SDS((N_BLK, HD, HD), F32), SDS((1, c), F32),
                   SDS((1, c), F32)],
        scratch_shapes=[pltpu.VMEM((HALO, c), F32), pltpu.VMEM((HALO, c), F32), pltpu.VMEM((HALO, c), F32),
                        pltpu.VMEM((LRU_T, c), F32), pltpu.VMEM((LRU_T, c), F32), pltpu.VMEM((LRU_T, c), F32)],
        compiler_params=_cp("arbitrary"), name=name,
    )(dym, proj, proj, proj, xc, hl, hl, dqm, conv_w, wa_m, ba.reshape(1, c), wx_m, bx.reshape(1, c),
      lam.reshape(1, c))
    dproj, dcw, dcb, dwa, dba, dwx, dbx, dlam = outs
    return dproj, dcw, dcb.reshape(c), dwa, dba.reshape(c), dwx, dbx.reshape(c), dlam.reshape(c)


def _mem_probs(q, kv):
    heads = [slice(hh * HD, (hh + 1) * HD) for hh in range(MEM_HEADS)]
    sc = [_dot(q[:, sl], kv[:, sl], NT) * SCALE for sl in heads]
    e = [jnp.exp(s - jnp.max(s, axis=-1, keepdims=True)) for s in sc]
    return [x / jnp.sum(x, axis=-1, keepdims=True) for x in e]


def mem_attn_fwd(proj, q_col, kvm, name):
    s = proj.shape[0]
    tq = min(512, s)

    def body(q_ref, kv_ref, o_ref):
        q = q_ref[...].astype(MXU)
        kv = kv_ref[...]
        p = _mem_probs(q, kv)
        outs = [_dot(p[hh].astype(MXU), kv[:, MEM_W + hh * HD:MEM_W + (hh + 1) * HD]) for hh in range(MEM_HEADS)]
        o_ref[...] = jnp.concatenate(outs, axis=1).astype(o_ref.dtype)

    return pl.pallas_call(
        body, grid=(s // tq,),
        in_specs=[BS((tq, MEM_W), lambda i: (i, q_col)), BS((N_MEM, 2 * MEM_W), lambda i: (0, 0))],
        out_specs=BS((tq, MEM_W), lambda i: (i, 0)), out_shape=SDS((s, MEM_W), MXU),
        compiler_params=_cp("parallel"), name=name,
    )(proj, kvm)


def mem_attn_bwd(proj, q_col, kvm, dym, name):
    s = proj.shape[0]
    tq = min(512, s)

    def body(q_ref, kv_ref, do_ref, dq_ref, dkv_ref):
        @pl.when(pl.program_id(0) == 0)
        def _():
            dkv_ref[...] = jnp.zeros_like(dkv_ref)

        q = q_ref[...].astype(MXU)
        do = do_ref[...].astype(MXU)
        kv = kv_ref[...]
        heads = [slice(hh * HD, (hh + 1) * HD) for hh in range(MEM_HEADS)]
        p = _mem_probs(q, kv)
        dp = [_dot(do[:, sl], kv[:, MEM_W + hh * HD:MEM_W + (hh + 1) * HD], NT) for hh, sl in enumerate(heads)]
        ds = [(pp * (d - jnp.sum(pp * d, axis=-1, keepdims=True)) * SCALE).astype(MXU) for pp, d in zip(p, dp)]
        dq = [_dot(x, kv[:, sl]) for x, sl in zip(ds, heads)]
        dk = [_dot(x, q[:, sl], TN) for x, sl in zip(ds, heads)]
        dv = [_dot(pp.astype(MXU), do[:, sl], TN) for pp, sl in zip(p, heads)]
        dq_ref[...] = jnp.concatenate(dq, axis=1).astype(dq_ref.dtype)
        dkv_ref[...] += jnp.concatenate(dk + dv, axis=1)

    return pl.pallas_call(
        body, grid=(s // tq,),
        in_specs=[BS((tq, MEM_W), lambda i: (i, q_col)), BS((N_MEM, 2 * MEM_W), lambda i: (0, 0)),
                  BS((tq, MEM_W), lambda i: (i, MIX_W // MEM_W))],
        out_specs=[BS((tq, MEM_W), lambda i: (i, 0)), BS((N_MEM, 2 * MEM_W), lambda i: (0, 0))],
        out_shape=[SDS((s, MEM_W), MXU), SDS((N_MEM, 2 * MEM_W), F32)],
        compiler_params=_cp("arbitrary"), name=name,
    )(proj, kvm, dym)


def _score_prev(q, kp, n, slope_dil):
    qi = lax.broadcasted_iota(jnp.int32, (Q_BLOCK, Q_BLOCK), 0)
    ki = lax.broadcasted_iota(jnp.int32, (Q_BLOCK, Q_BLOCK), 1)
    rel = qi + Q_BLOCK - ki
    s = _dot(q, kp, NT) * SCALE - slope_dil * rel.astype(F32)
    return jnp.where((rel <= Q_BLOCK) & (n > 0), s, NEG_INF)


def _score_cur(q, kc, slope_dil):
    qi = lax.broadcasted_iota(jnp.int32, (Q_BLOCK, Q_BLOCK), 0)
    ki = lax.broadcasted_iota(jnp.int32, (Q_BLOCK, Q_BLOCK), 1)
    rel = qi - ki
    s = _dot(q, kc, NT) * SCALE - slope_dil * rel.astype(F32)
    return jnp.where(rel >= 0, s, NEG_INF)


def _dil_scores(q, kp, kc, n, slope_dil):
    return _score_prev(q, kp, n, slope_dil), _score_cur(q, kc, slope_dil)


def _slope_dil(gi, hh):
    head = 4 * gi + hh
    return DIL_GROUPS[gi][1] * 2.0 ** (-8.0 * (head + 1.0) / N_BLK)


def _dil_operands(proj, kv, gi):
    dil = DIL_GROUPS[gi][1]
    if dil == 1:
        return proj, kv, kv, (lambda r: gi), (lambda r: gi), (lambda r: MIX_W // MEM_W + gi)
    sub = proj.shape[0] // dil

    def view(a, col):
        return a[:, col:col + MEM_W].reshape(sub, dil * MEM_W)

    same = lambda r: r
    return view(proj, gi * MEM_W), view(kv, gi * MEM_W), view(kv, MIX_W + gi * MEM_W), same, same, same


def dil_attn_fwd(proj, kv, gi, name):
    dil = DIL_GROUPS[gi][1]
    s, pw = proj.shape
    sub = s // dil
    nb = sub // Q_BLOCK
    pair = 2 * Q_BLOCK

    def body(q_ref, kp_ref, kc_ref, vp_ref, vc_ref, o_ref, lse_ref):
        t = pl.program_id(1)
        q = q_ref[...].astype(MXU)
        k_prev, k_cur, v_prev, v_cur = kp_ref[...], kc_ref[...], vp_ref[...], vc_ref[...]
        units = []
        for b in range(2):
            rows = slice(b * Q_BLOCK, (b + 1) * Q_BLOCK)
            kp, vp = (k_prev, v_prev) if b == 0 else (k_cur[:Q_BLOCK], v_cur[:Q_BLOCK])
            for hh in range(4):
                sl = slice(hh * HD, (hh + 1) * HD)
                units.append((q[rows, sl], kp[:, sl], k_cur[rows, sl], vp[:, sl], v_cur[rows, sl], 2 * t + b, hh))
        sc = [_dil_scores(qh, kp, kc, n, _slope_dil(gi, hh)) for qh, kp, kc, _, _, n, hh in units]
        mx = [jnp.maximum(jnp.max(s_p, axis=-1, keepdims=True), jnp.max(s_c, axis=-1, keepdims=True))
              for s_p, s_c in sc]
        den = [jnp.sum(jnp.exp(s_p - m), axis=-1, keepdims=True) + jnp.sum(jnp.exp(s_c - m), axis=-1, keepdims=True)
               for (s_p, s_c), m in zip(sc, mx)]
        lse = [m + jnp.log(d) for m, d in zip(mx, den)]
        pr = [(jnp.exp(s_p - l).astype(MXU), jnp.exp(s_c - l).astype(MXU)) for (s_p, s_c), l in zip(sc, lse)]
        outs = [_dot(p_p, u[3]) + _dot(p_c, u[4]) for (p_p, p_c), u in zip(pr, units)]
        wide = [jnp.broadcast_to(l, (Q_BLOCK, HD)) for l in lse]
        o_ref[...] = jnp.concatenate([jnp.concatenate(outs[4 * b:4 * b + 4], axis=1) for b in range(2)], axis=0)
        lse_ref[...] = jnp.concatenate([jnp.concatenate(wide[4 * b:4 * b + 4], axis=1) for b in range(2)], axis=0)

    one, two = (Q_BLOCK, MEM_W), (pair, MEM_W)
    before = lambda t: jnp.maximum(2 * t - 1, 0)
    out = BS(two, lambda r, t: (t, r))
    qv, kview, vview, qcol, kcol, vcol = _dil_operands(proj, kv, gi)
    return pl.pallas_call(
        body, grid=(dil, nb // 2),
        in_specs=[BS(two, lambda r, t: (t, qcol(r))),
                  BS(one, lambda r, t: (before(t), kcol(r))), BS(two, lambda r, t: (t, kcol(r))),
                  BS(one, lambda r, t: (before(t), vcol(r))), BS(two, lambda r, t: (t, vcol(r)))],
        out_specs=[out, out],
        out_shape=[SDS((sub, dil * MEM_W), F32), SDS((sub, dil * MEM_W), F32)],
        compiler_params=_cp("parallel", "parallel"), name=name,
    )(qv, kview, kview, vview, vview)


def dil_attn_bwd(proj, kv, lse, do, dd, gi, name):
    dil = DIL_GROUPS[gi][1]
    sub = proj.shape[0] // dil
    nb = sub // Q_BLOCK
    pair = 2 * Q_BLOCK
    lo, hi = slice(0, Q_BLOCK), slice(Q_BLOCK, pair)

    def body(q_ref, qn_ref, kp_ref, kc_ref, vp_ref, vc_ref, lse_ref, lsen_ref, do_ref, don_ref, dd_ref, ddn_ref,
             dq_ref, dk_ref, dv_ref):
        t = pl.program_id(1)
        has_next = 2 * t + 2 < nb
        q, qn = q_ref[...].astype(MXU), qn_ref[...].astype(MXU)
        do_m, don_m = do_ref[...].astype(MXU), don_ref[...].astype(MXU)
        k_prev, k_cur, v_prev, v_cur = kp_ref[...], kc_ref[...], vp_ref[...], vc_ref[...]
        lse_v, lsen_v, dd_v, ddn_v = lse_ref[...], lsen_ref[...], dd_ref[...], ddn_ref[...]
        units = []
        for hh in range(4):
            sl = slice(hh * HD, (hh + 1) * HD)
            q0, q1, q2 = q[lo, sl], q[hi, sl], qn[:, sl]
            g0, g1, g2 = do_m[lo, sl], do_m[hi, sl], don_m[:, sl]
            l0, l1, l2 = lse_v[lo, sl], lse_v[hi, sl], lsen_v[:, sl]
            e0, e1, e2 = dd_v[lo, sl], dd_v[hi, sl], ddn_v[:, sl]
            ka, kb, kc = k_prev[:, sl], k_cur[lo, sl], k_cur[hi, sl]
            va, vb, vc = v_prev[:, sl], v_cur[lo, sl], v_cur[hi, sl]
            units += [(q0, g0, l0, e0, ka, va, 2 * t, None), (q0, g0, l0, e0, kb, vb, None, None),
                      (q1, g1, l1, e1, kb, vb, 2 * t + 1, None), (q1, g1, l1, e1, kc, vc, None, None),
                      (q2, g2, l2, e2, kc, vc, 2 * t + 2, has_next)]
        slopes = [_slope_dil(gi, hh) for hh in range(4) for _ in range(5)]
        sc = [_score_cur(u[0], u[4], sp) if u[6] is None else _score_prev(u[0], u[4], u[6], sp)
              for u, sp in zip(units, slopes)]
        dp = [_dot(u[1], u[5], NT) for u in units]
        pr = [jnp.exp(s - u[2]) if u[7] is None else jnp.where(u[7], jnp.exp(s - u[2]), 0.0) for s, u in zip(sc, units)]
        ds = [(p * (d + u[3]) * SCALE).astype(MXU) for p, d, u in zip(pr, dp, units)]
        pm = [p.astype(MXU) for p in pr]
        dq, dk, dv = [[], []], [[], []], [[], []]
        for hh in range(4):
            a, b, c, d, e = range(5 * hh, 5 * hh + 5)
            dq[0].append(_dot(ds[a], units[a][4]) + _dot(ds[b], units[b][4]))
            dq[1].append(_dot(ds[c], units[c][4]) + _dot(ds[d], units[d][4]))
        for hh in range(4):
            a, b, c, d, e = range(5 * hh, 5 * hh + 5)
            dk[0].append(_dot(ds[b], units[b][0], TN) + _dot(ds[c], units[c][0], TN))
            dk[1].append(_dot(ds[d], units[d][0], TN) + _dot(ds[e], units[e][0], TN))
            dv[0].append(_dot(pm[b], units[b][1], TN) + _dot(pm[c], units[c][1], TN))
            dv[1].append(_dot(pm[d], units[d][1], TN) + _dot(pm[e], units[e][1], TN))

        def tile(parts):
            return jnp.concatenate([jnp.concatenate(parts[0], axis=1), jnp.concatenate(parts[1], axis=1)], axis=0)

        dq_ref[...] = tile(dq).astype(dq_ref.dtype)
        dk_ref[...] = tile(dk)
        dv_ref[...] = tile(dv)

    one, two = (Q_BLOCK, MEM_W), (pair, MEM_W)
    before = lambda t: jnp.maximum(2 * t - 1, 0)
    after = lambda t: jnp.minimum(2 * t + 2, nb - 1)
    own, nxt = BS(two, lambda r, t: (t, r)), BS(one, lambda r, t: (after(t), r))
    qv, kview, vview, qcol, kcol, vcol = _dil_operands(proj, kv, gi)
    return pl.pallas_call(
        body, grid=(dil, nb // 2),
        in_specs=[BS(two, lambda r, t: (t, qcol(r))), BS(one, lambda r, t: (after(t), qcol(r))),
                  BS(one, lambda r, t: (before(t), kcol(r))), BS(two, lambda r, t: (t, kcol(r))),
                  BS(one, lambda r, t: (before(t), vcol(r))), BS(two, lambda r, t: (t, vcol(r))),
                  own, nxt, own, nxt, own, nxt],
        out_specs=[own, own, own],
        out_shape=[SDS((sub, dil * MEM_W), MXU), SDS((sub, dil * MEM_W), F32), SDS((sub, dil * MEM_W), F32)],
        compiler_params=_cp("parallel", "parallel"), name=name,
    )(qv, qv, kview, kview, vview, vview, lse, lse, do, do, dd, dd)


def _group_weights(lse_refs):
    l0, l1, l2 = (r[...] for r in lse_refs)
    mx = jnp.maximum(jnp.maximum(l0, l1), l2)
    e = [jnp.exp(l - mx) for l in (l0, l1, l2)]
    den = e[0] + e[1] + e[2]
    return [x / den for x in e]


def dil_mix_prep(o_list, lse_list, m, name):
    s = m.shape[0]
    tr = _row_tile(s)

    def body(o0, o1, o2, l0, l1, l2, m_ref, out_ref):
        w = _group_weights((l0, l1, l2))
        for g, o_ref in enumerate((o0, o1, o2)):
            out_ref[:, g * MEM_W:(g + 1) * MEM_W] = (o_ref[...] * w[g]).astype(out_ref.dtype)
        out_ref[:, MIX_W:] = m_ref[...]

    blk = BS((tr, MEM_W), lambda i: (i, 0))
    return pl.pallas_call(
        body, grid=(s // tr,), in_specs=[blk] * 7,
        out_specs=BS((tr, D), lambda i: (i, 0)), out_shape=SDS((s, D), MXU),
        compiler_params=_cp("parallel"), name=name,
    )(*o_list, *lse_list, m)


def dil_mix_bwd(dym, o_list, lse_list, name):
    s = dym.shape[0]
    tr = _row_tile(s)

    def body(da_ref, o0, o1, o2, l0, l1, l2, do0, do1, do2, dd0, dd1, dd2):
        w = _group_weights((l0, l1, l2))
        tot = None
        for g, (o_ref, do_ref) in enumerate(zip((o0, o1, o2), (do0, do1, do2))):
            da = da_ref[:, g * MEM_W:(g + 1) * MEM_W]
            do_ref[...] = da * w[g]
            x = da * o_ref[...]
            dw = jnp.concatenate(
                [jnp.broadcast_to(jnp.sum(x[:, hh * HD:(hh + 1) * HD], axis=-1, keepdims=True), (tr, HD))
                 for hh in range(4)], axis=1)
            tot = w[g] * dw if tot is None else tot + w[g] * dw
        for g, dd_ref in enumerate((dd0, dd1, dd2)):
            dd_ref[...] = -w[g] * tot

    blk = BS((tr, MEM_W), lambda i: (i, 0))
    outs = pl.pallas_call(
        body, grid=(s // tr,), in_specs=[BS((tr, MIX_W), lambda i: (i, 0))] + [blk] * 6,
        out_specs=[blk] * 6, out_shape=[SDS((s, MEM_W), F32)] * 6,
        compiler_params=_cp("parallel"), name=name,
    )(dym, *o_list, *lse_list)
    return outs[:3], outs[3:]


def sum_cast(parts, name):
    s = parts[0][0].shape[0]
    tr = _row_tile(s)
    flat = [a for p in parts for a in p]
    sizes = [len(p) for p in parts]

    def body(*refs):
        out_ref = refs[-1]
        pos = 0
        for j, n in enumerate(sizes):
            acc = refs[pos][...].astype(F32)
            for t in range(1, n):
                acc = acc + refs[pos + t][...].astype(F32)
            out_ref[:, j * MEM_W:(j + 1) * MEM_W] = acc.astype(out_ref.dtype)
            pos += n

    blk = BS((tr, MEM_W), lambda i: (i, 0))
    width = MEM_W * len(parts)
    return pl.pallas_call(
        body, grid=(s // tr,), in_specs=[blk] * len(flat),
        out_specs=BS((tr, width), lambda i: (i, 0)), out_shape=SDS((s, width), MXU),
        compiler_params=_cp("parallel"), name=name,
    )(*flat)


def add_n(arrs, name):
    rows, cols = arrs[0].shape
    tr = _row_tile(rows)

    def body(*refs):
        acc = refs[0][...]
        for r in refs[1:-1]:
            acc = acc + r[...]
        refs[-1][...] = acc

    blk = BS((tr, cols), lambda i: (i, 0))
    return pl.pallas_call(
        body, grid=(rows // tr,), in_specs=[blk] * len(arrs), out_specs=blk,
        out_shape=SDS((rows, cols), F32), compiler_params=_cp("parallel"), name=name,
    )(*arrs)


class _NoExchange:
    def hook(self, where, l, after):
        return []


def _fwd_bwd(x, mem, target, small, big, gs, gb, sched):
    s = x.shape[0]
    tm = min(1024, s)
    ts = min(2048, s)

    def after_hook(arr, where, l, after):
        toks = sched.hook(where, l, after)
        return tie(arr, toks, "tie_%s_%d" % (where, l)) if toks else arr

    h = x
    saved = []
    kv = None
    mem_n = None
    hn = norm_cast(h, small["a_pre_mix_g"][0], "pre_norm")
    for l in range(4):
        rec = l < 2
        p, j = ("a", l) if rec else ("b", l - 2)
        sv = {"h": h}
        hn = after_hook(hn, "fwd_begin", l, h)
        if mem_n is None:
            mem_n = norm_cast(mem, small["mem_norm_g"], "mem_norm")
        kvm = mm_nn(mem_n, big[p + "_w_mem_kv"][j], tm=N_MEM, tn=2 * MEM_W, tk=D, out_dtype=MXU, name="mem_kv")
        if rec:
            proj = mm_nn(hn, big["a_w_in"][j], tm=min(2 * tm, s), tn=896, tk=D, out_dtype=F32, name="rec_in")
            xc, hl = lru_fwd(proj, small["a_conv_w"][j], small["a_conv_b"][j], small["a_gate_a_w"][j],
                             small["a_gate_a_b"][j], small["a_gate_x_w"][j], small["a_gate_x_b"][j],
                             small["a_lambda"][j], "lru_fwd")
            m = mem_attn_fwd(proj, 2 * MIX_W // MEM_W, kvm, "rec_mem_attn")
            ym = lru_mix_prep(hl, proj, m, "lru_mix_prep")
            sv.update(xc=xc, hl=hl)
        else:
            proj = mm_nn(hn, big["b_w_in"][j], tm=tm, tn=D, tk=D, out_dtype=F32, name="dil_in")
            o_list, lse_list = [], []
            for gi in range(3):
                o, lse = dil_attn_fwd(proj, kv, gi, "dil_attn_fwd%d" % gi)
                o_list.append(o.reshape(s, MEM_W))
                lse_list.append(lse.reshape(s, MEM_W))
            m = mem_attn_fwd(proj, MIX_W // MEM_W, kvm, "dil_mem_attn")
            ym = dil_mix_prep(o_list, lse_list, m, "dil_mix_prep")
            sv.update(o=o_list, lse=lse_list)
        ym = after_hook(ym, "fwd_q1", l, ym)
        mix = mm_nn(ym, big[p + "_w_out"][j], tm=tm, tn=D, tk=D, out_dtype=F32, name="mix_out")
        h1, hn2 = resid_norm_next(h, mix, small[p + "_post_mix_g"][j], small[p + "_pre_ffn_g"][j], "post_pre_norm")
        hn2 = after_hook(hn2, "fwd_mid", l, mix)
        g, u, act = ffn_in_fwd(hn2, big[p + "_w_ffn_in"][j], "ffn_in")
        act = after_hook(act, "fwd_q3", l, u)
        y2 = mm_nn(act, big[p + "_w_ffn_out"][j], tm=tm // 2, tn=D, tk=D_FF // 2, out_dtype=F32, name="ffn_out")
        sv.update(kvm=kvm, hn=hn, proj=proj, ym=ym, mix=mix, h1=h1, hn2=hn2, g=g, u=u, act=act, y2=y2)
        saved.append(sv)
        if l < 3:
            pn, jn = ("a", l + 1) if l + 1 < 2 else ("b", l - 1)
            h, hn = resid_norm_next(h1, y2, small[p + "_post_ffn_g"][j], small[pn + "_pre_mix_g"][jn],
                                    "post_pre_norm")
        else:
            h = resid_norm(h1, y2, small[p + "_post_ffn_g"][j], "post_norm")
        sched.hook("fwd_end", l, h)
        if l == 1:
            h_kv = h
            kvn = norm_cast(h, small["kv_norm_g"], "pre_norm")
            kv = mm_nn(kvn, big["w_kv_shared"], tm=tm, tn=768, tk=D, out_dtype=MXU, name="kv_proj")

    loss_parts, dh = loss_head(h, target, "loss_head")

    def stack2(name, j, val):
        gs.setdefault(name, [None, None])[j] = val

    def stack2b(name, j, val):
        gb.setdefault(name, [None, None])[j] = val

    dkv_parts = []
    ahead = []
    dmem_parts = []
    dkvm = [None] * 4
    for l in (3, 2, 1, 0):
        rec = l < 2
        p, j = ("a", l) if rec else ("b", l - 2)
        sv = saved[l]
        if l == 1:
            dkv = sum_cast([(dkv_parts[0][c], dkv_parts[1][c]) for c in range(6)], "dkv_sum")
            dkvn = mm_nt([dkv], big["w_kv_shared"], tm=tm, tn=D, tk=768, out_dtype=MXU, name="kv_proj_dx")
            gb["w_kv_shared"] = mm_tn(kvn, [dkv], t1=D, tn=768, ts=ts, col_shards=True, name="kv_proj_dw")
            dh, gs["kv_norm_g"], *ahead = norm_bwd(h_kv, small["kv_norm_g"], dkvn, dh, F32, "pre_post_norm_bwd",
                                                   then=(sv["y2"], small["a_post_ffn_g"][1]))
        if ahead:
            dy2, dg = ahead
            ahead = []
        else:
            dy2, dg = norm_bwd(sv["y2"], small[p + "_post_ffn_g"][j], dh, None, MXU, "post_norm_bwd")
        dy2 = after_hook(dy2, "bwd_begin", l, dh)
        stack2(p + "_post_ffn_g", j, dg)
        dgg, dgu = ffn_act_bwd(dy2, big[p + "_w_ffn_out"][j], sv["g"], sv["u"], "ffn_act_bwd")
        dgg = after_hook(dgg, "bwd_mid1", l, dgu)
        stack2b(p + "_w_ffn_out", j, mm_tn(sv["act"], [dy2], t1=D_FF // 4, tn=D, ts=ts // 2, col_shards=False,
                                          name="ffn_out_dw"))
        dhn2 = mm_nt([dgg, dgu], big[p + "_w_ffn_in"][j], tm=tm // 2, tn=D, tk=D_FF // 2, out_dtype=MXU,
                     name="ffn_in_dx")
        stack2b(p + "_w_ffn_in", j, mm_tn(sv["hn2"], [dgg, dgu], t1=D // 2, tn=D_FF // 4, ts=ts, col_shards=True,
                                         name="ffn_in_dw"))
        dhn2 = after_hook(dhn2, "bwd_mid2", l, gb[p + "_w_ffn_in"][j])
        dh1, dg, dmix, dg_mix = norm_bwd(sv["h1"], small[p + "_pre_ffn_g"][j], dhn2, dh, F32, "pre_post_norm_bwd",
                                         then=(sv["mix"], small[p + "_post_mix_g"][j]))
        stack2(p + "_pre_ffn_g", j, dg)
        stack2(p + "_post_mix_g", j, dg_mix)
        dym = mm_nt([dmix], big[p + "_w_out"][j], tm=tm, tn=D, tk=D, out_dtype=F32, name="mix_out_dx")
        stack2b(p + "_w_out", j, mm_tn(sv["ym"], [dmix], t1=D, tn=1024, ts=ts, col_shards=False,
                                      name="mix_out_dw"))
        dym = after_hook(dym, "bwd_m1", l, gb[p + "_w_out"][j])
        if rec:
            dqm, dkvm[l] = mem_attn_bwd(sv["proj"], 2 * MIX_W // MEM_W, sv["kvm"], dym, "rec_mem_attn_bwd")
            dproj, dcw, dcb, dwa, dba, dwx, dbx, dlam = lru_bwd(
                dym, sv["proj"], sv["xc"], sv["hl"], dqm, small["a_conv_w"][j], small["a_gate_a_w"][j],
                small["a_gate_a_b"][j], small["a_gate_x_w"][j], small["a_gate_x_b"][j], small["a_lambda"][j],
                "lru_bwd")
            for nm, val in (("a_conv_w", dcw), ("a_conv_b", dcb), ("a_gate_a_w", dwa), ("a_gate_a_b", dba),
                            ("a_gate_x_w", dwx), ("a_gate_x_b", dbx), ("a_lambda", dlam)):
                stack2(nm, j, val)
            dhn = mm_nt([dproj], big["a_w_in"][j], tm=tm, tn=D, tk=896, out_dtype=MXU, name="rec_in_dx")
            stack2b("a_w_in", j, mm_tn(sv["hn"], [dproj], t1=D, tn=896, ts=ts, col_shards=True, name="rec_in_dw"))
        else:
            dqm, dkvm[l] = mem_attn_bwd(sv["proj"], MIX_W // MEM_W, sv["kvm"], dym, "dil_mem_attn_bwd")
            do_list, dd_list = dil_mix_bwd(dym, sv["o"], sv["lse"], "dil_mix_bwd")
            dq_list, dk_list, dv_list = [], [], []
            for gi in range(3):
                dil = DIL_GROUPS[gi][1]
                view = (s // dil, dil * MEM_W)
                dq, dk, dv = dil_attn_bwd(sv["proj"], kv, sv["lse"][gi].reshape(view), do_list[gi].reshape(view),
                                          dd_list[gi].reshape(view), gi, "dil_attn_bwd%d" % gi)
                dq_list.append(dq.reshape(s, MEM_W))
                dk_list.append(dk.reshape(s, MEM_W))
                dv_list.append(dv.reshape(s, MEM_W))
            dkv_parts.append(dk_list + dv_list)
            dproj = sum_cast([(a,) for a in dq_list + [dqm]], "dil_dproj")
            dhn = mm_nt([dproj], big["b_w_in"][j], tm=tm, tn=D, tk=D, out_dtype=MXU, name="dil_in_dx")
            stack2b("b_w_in", j, mm_tn(sv["hn"], [dproj], t1=D, tn=1024, ts=ts, col_shards=False, name="dil_in_dw"))
        dk_m = dkvm[l].astype(MXU)
        dmem_parts.append(mm_nt([dk_m], big[p + "_w_mem_kv"][j], tm=N_MEM, tn=D, tk=2 * MEM_W, out_dtype=F32,
                                name="mem_kv_dx"))
        stack2b(p + "_w_mem_kv", j, mm_tn(mem_n, [dk_m], t1=D, tn=2 * MEM_W, ts=N_MEM, col_shards=False,
                                         name="mem_kv_dw"))
        if l in (3, 1):
            pn, jn = ("b", 0) if l == 3 else ("a", 0)
            dh, dg, *ahead = norm_bwd(sv["h"], small[p + "_pre_mix_g"][j], dhn, dh1, F32, "pre_post_norm_bwd",
                                      then=(saved[l - 1]["y2"], small[pn + "_post_ffn_g"][jn]))
        else:
            dh, dg = norm_bwd(sv["h"], small[p + "_pre_mix_g"][j], dhn, dh1, F32, "pre_norm_bwd")
        stack2(p + "_pre_mix_g", j, dg)
        dh = after_hook(dh, "bwd_end", l, dh)

    _, gs["mem_norm_g"] = norm_bwd(mem, small["mem_norm_g"], add_n(dmem_parts, "dmem_sum"), None, F32,
                                   "mem_norm_bwd")
    return loss_parts, dh


ANY = pl.BlockSpec(memory_space=pl.ANY)
CHIP_FLIPS = (1, 2, 3)


def _coords():
    return lax.axis_index("x"), lax.axis_index("y"), lax.axis_index("c")


def _flip(x, y, m):
    return x ^ (m >> 1), y ^ (m & 1)


def _remote(src, dst, send_sems, recv_sems, k, device):
    return pltpu.make_async_remote_copy(src_ref=src, dst_ref=dst, send_sem=send_sems.at[k], recv_sem=recv_sems.at[k],
                                        device_id=device, device_id_type=MESH)


def _sum_rows_tile(rows, cols, itemsize=4):
    for tr in (512, 256, 128, 64, 32, 16):
        if rows % tr == 0 and tr * cols * itemsize <= 2 * 1024 * 1024:
            return tr
    raise ValueError((rows, cols))


def half_sum(g, got, name):
    _, r, cols = g.shape
    hr = r // 2
    tr = _sum_rows_tile(hr, cols, g.dtype.itemsize // 2)

    def my_chip():
        return 2 * lax.axis_index("x") + lax.axis_index("y")

    def body(g_ref, got_ref, o_ref, own_ref):
        p = (g_ref[...].astype(F32) + got_ref[...].astype(F32)).astype(o_ref.dtype)
        o_ref[...] = p

        @pl.when(pl.program_id(1) == my_chip())
        def _():
            own_ref[...] = p

    out = SDS((N_CHIPS, hr, cols), jnp.bfloat16)
    return pl.pallas_call(
        body, grid=(hr // tr, N_CHIPS),
        in_specs=[BS((None, None, tr, cols), lambda i, s: (s, lax.axis_index("c"), i, 0)),
                  BS((None, tr, cols), lambda i, s: (s, i, 0))],
        out_specs=[BS((None, tr, cols), lambda i, s: (s, i, 0)),
                   BS((None, tr, cols), lambda i, s: (my_chip(), i, 0))],
        out_shape=[out, out], compiler_params=_cp("parallel", "arbitrary"), name=name,
    )(g.reshape(N_CHIPS, 2, hr, cols), got)


def slot_sum(slots, name):
    _, hr, cols = slots.shape
    tr = _sum_rows_tile(hr, cols)
    nblk = hr // tr

    def body(s_ref, o_ref):
        acc = s_ref[0].astype(F32)
        for p in range(1, N_CHIPS):
            acc = acc + s_ref[p].astype(F32)
        o_ref[...] = acc

    return pl.pallas_call(
        body, grid=(nblk,), in_specs=[BS((N_CHIPS, tr, cols), lambda i: (0, i, 0))],
        out_specs=BS((tr, cols), lambda i: (lax.axis_index("c") * nblk + i, 0)),
        out_shape=SDS((2 * hr, cols), F32), compiler_params=_cp("parallel"), name=name,
    )(slots)


HBM_SPEC = pl.BlockSpec(memory_space=pltpu.HBM)
SEM_SPEC = pl.BlockSpec(memory_space=pltpu.SEMAPHORE)
EFFECT = pltpu.SideEffectType.DATAFLOW_SIDE_EFFECTING


def split_start(name, bufs, plan, n_copies):
    nb = len(bufs)

    def body(*refs):
        send_sems, recv_sems = refs[nb], refs[nb + 1]
        for k, (src, dst, dev) in enumerate(plan(refs[:nb])):
            _remote(src, dst, send_sems, recv_sems, k, dev).start()
        refs[-1][...] = jnp.zeros_like(refs[-1])

    outs = pl.pallas_call(
        body, name=name,
        out_shape=(pltpu.SemaphoreType.DMA((n_copies,)), pltpu.SemaphoreType.DMA((n_copies,)),
                   *[pltpu.HBM(b.shape, b.dtype) for b in bufs], SDS((8, LANES), F32)),
        in_specs=[HBM_SPEC] * nb, out_specs=(SEM_SPEC, SEM_SPEC, *[HBM_SPEC] * nb, VM),
        input_output_aliases={i: 2 + i for i in range(nb)},
        compiler_params=pltpu.CompilerParams(has_side_effects=EFFECT),
    )(*[pltpu.with_memory_space_constraint(b, pltpu.HBM) for b in bufs])
    return outs[0], outs[1], list(outs[2:2 + nb]), outs[-1]


def split_wait(name, send_sems, recv_sems, bufs, after, plan):
    nb = len(bufs)

    def body(*refs):
        send_ref, recv_ref = refs[nb], refs[nb + 1]
        for k, (src, dst, dev) in enumerate(plan(refs[:nb])):
            cp = _remote(src, dst, send_ref, recv_ref, k, dev)
            cp.wait_send()
            cp.wait_recv()

    outs = pl.pallas_call(
        body, name=name, out_shape=[pltpu.HBM(b.shape, b.dtype) for b in bufs],
        in_specs=[HBM_SPEC] * nb + [SEM_SPEC, SEM_SPEC, ANY], out_specs=[HBM_SPEC] * nb,
        input_output_aliases={i: i for i in range(nb)},
        compiler_params=pltpu.CompilerParams(has_side_effects=EFFECT),
    )(*bufs, send_sems, recv_sems, after)
    return list(outs)


def tie(x, tokens, name):
    def body(*refs):
        pass

    return pl.pallas_call(
        body, name=name, out_shape=SDS(x.shape, x.dtype), in_specs=[ANY] * (1 + len(tokens)), out_specs=ANY,
        input_output_aliases={0: 0},
    )(x, *tokens)


def plan_gather_ici(n, rows):
    def plan(refs):
        x, y, c = _coords()
        me = 2 * x + y
        out = []
        for i in range(n):
            hr = rows[i] // 2
            mine = pl.ds(pl.multiple_of(c * hr, 8), hr)
            out.append((refs[i], refs[n + i].at[me], (x, y, 1 - c)))
            for m in CHIP_FLIPS:
                out.append((refs[i].at[mine], refs[n + i].at[me, mine], (*_flip(x, y, m), c)))
        return out
    return plan


def plan_gather_d2d(n, rows):
    def plan(refs):
        x, y, c = _coords()
        me = 2 * x + y
        out = []
        for i in range(n):
            hr = rows[i] // 2
            mine = pl.ds(pl.multiple_of(c * hr, 8), hr)
            for m in CHIP_FLIPS:
                slot = refs[i].at[me ^ m, mine]
                out.append((slot, slot, (x, y, 1 - c)))
        return out
    return plan


def plan_swap(n, rows):
    def plan(refs):
        x, y, c = _coords()
        out = []
        for i in range(n):
            hr = rows[i] // 2
            other = pl.ds(pl.multiple_of((1 - c) * hr, 8), hr)
            out.append((refs[i].at[pl.ds(0, N_CHIPS), other], refs[n + i], (x, y, 1 - c)))
        return out
    return plan


def plan_exchange(n):
    def plan(refs):
        x, y, c = _coords()
        me = 2 * x + y
        out = []
        for i in range(n):
            for m in CHIP_FLIPS:
                out.append((refs[i].at[me ^ m], refs[n + i].at[me], (*_flip(x, y, m), c)))
        return out
    return plan


def plan_share(n, rows):
    def plan(refs):
        x, y, c = _coords()
        out = []
        for i in range(n):
            hr = rows[i] // 2
            mine = refs[i].at[pl.ds(pl.multiple_of(c * hr, 8), hr)]
            out.append((mine, mine, (x, y, 1 - c)))
        return out
    return plan


VM = pl.BlockSpec(memory_space=pltpu.VMEM)


def small_gather(v, name):
    def body(v_ref, out_ref, send_sems, recv_sems):
        x, y, c = _coords()
        me = 2 * x + y
        out_ref[me] = v_ref[...]
        cps = []
        for j, m in enumerate(CHIP_FLIPS):
            cp = _remote(v_ref, out_ref.at[me], send_sems, recv_sems, j, (*_flip(x, y, m), c))
            cp.start()
            cps.append(cp)
        for cp in cps:
            cp.wait()

    return pl.pallas_call(
        body, in_specs=[VM], out_specs=VM, out_shape=SDS((N_CHIPS,) + v.shape, v.dtype),
        scratch_shapes=[pltpu.SemaphoreType.DMA((3,)), pltpu.SemaphoreType.DMA((3,))],
        compiler_params=pltpu.CompilerParams(vmem_limit_bytes=VMEM_LIMIT_BYTES), name=name,
    )(v)


def plan_small_swap(refs):
    x, y, c = _coords()
    return [(refs[0], refs[1], (x, y, 1 - c))]


def plan_small_exchange(refs):
    x, y, c = _coords()
    me = 2 * x + y
    return [(refs[0].at[me], refs[0].at[me], (*_flip(x, y, m), c)) for m in CHIP_FLIPS]


def small_pair(v, sib, name):
    rows, cols = v.shape
    tr = _sum_rows_tile(rows, cols)

    def body(v_ref, s_ref, o_ref):
        o_ref[...] = v_ref[...] + s_ref[...]

    blk = BS((tr, cols), lambda i: (i, 0))
    return pl.pallas_call(
        body, grid=(rows // tr,), in_specs=[blk, blk],
        out_specs=BS((None, tr, cols), lambda i: (2 * lax.axis_index("x") + lax.axis_index("y"), i, 0)),
        out_shape=SDS((N_CHIPS, rows, cols), F32), compiler_params=_cp("parallel"), name=name,
    )(v, sib)


def small_total(slots, name):
    _, rows, cols = slots.shape
    tr = _sum_rows_tile(rows, cols)

    def body(s_ref, o_ref):
        o_ref[...] = (s_ref[0] + s_ref[1]) + (s_ref[2] + s_ref[3])

    return pl.pallas_call(
        body, grid=(rows // tr,), in_specs=[BS((N_CHIPS, tr, cols), lambda i: (0, i, 0))],
        out_specs=BS((tr, cols), lambda i: (i, 0)), out_shape=SDS((rows, cols), F32),
        compiler_params=_cp("parallel"), name=name,
    )(slots)


def adamw(w, g_list, m, v, name):
    nl, rows, cols = w.shape
    tr = _sum_rows_tile(rows, cols) if rows % 16 == 0 else rows
    bc1 = 1.0 - ADAM_B1 ** ADAM_STEP
    bc2 = 1.0 - ADAM_B2 ** ADAM_STEP

    def body(*refs):
        w_ref, m_ref, v_ref = refs[:3]
        g_refs = refs[3:3 + nl]
        go_ref, d_ref, mo_ref, vo_ref = refs[3 + nl:]
        layer = pl.program_id(0)
        for l in range(nl):
            @pl.when(layer == l)
            def _(l=l):
                g = g_refs[l][...]
                m_new = ADAM_B1 * m_ref[...] + (1.0 - ADAM_B1) * g
                v_new = ADAM_B2 * v_ref[...] + (1.0 - ADAM_B2) * (g * g)
                m_hat = m_new / bc1
                v_hat = v_new / bc2
                go_ref[...] = g
                d_ref[...] = -ADAM_LR * (m_hat / (jnp.sqrt(v_hat) + ADAM_EPS) + ADAM_WD * w_ref[...])
                mo_ref[...] = m_new
                vo_ref[...] = v_new

    stk = BS((None, tr, cols), lambda l, i: (l, i, 0))
    flat = BS((tr, cols), lambda l, i: (i, 0))
    out = SDS((nl, rows, cols), F32)
    return pl.pallas_call(
        body, grid=(nl, rows // tr), in_specs=[stk] * 3 + [flat] * nl, out_specs=[stk] * 4,
        out_shape=[out] * 4, compiler_params=_cp("parallel", "parallel"), name=name,
    )(w, m, v, *g_list)


WEIGHTS = ["mem_norm_g", "a_pre_mix_g", "a_post_mix_g", "a_pre_ffn_g", "a_post_ffn_g", "a_w_in", "a_conv_w",
           "a_conv_b", "a_gate_a_w", "a_gate_a_b", "a_gate_x_w", "a_gate_x_b", "a_lambda", "a_w_mem_kv", "a_w_out",
           "a_w_ffn_in", "a_w_ffn_out", "kv_norm_g", "w_kv_shared", "b_pre_mix_g", "b_post_mix_g", "b_pre_ffn_g",
           "b_post_ffn_g", "b_w_in", "b_w_mem_kv", "b_w_out", "b_w_ffn_in", "b_w_ffn_out"]
BIG = {"a_w_in": True, "a_w_mem_kv": False, "a_w_out": False, "a_w_ffn_in": True, "a_w_ffn_out": False,
       "w_kv_shared": True, "b_w_in": False, "b_w_mem_kv": False, "b_w_out": False, "b_w_ffn_in": True,
       "b_w_ffn_out": False}
SHARDED_SMALL = ["a_pre_mix_g", "a_post_mix_g", "a_pre_ffn_g", "a_post_ffn_g", "a_conv_w", "a_conv_b", "a_gate_a_b",
                 "a_gate_x_b", "a_lambda"]
REPL_SMALL = ["mem_norm_g", "kv_norm_g", "b_pre_mix_g", "b_post_mix_g", "b_pre_ffn_g", "b_post_ffn_g", "a_gate_a_w",
              "a_gate_x_w"]
LANES = 128


def _pack(arrs, row_multiple=8):
    flat = jnp.concatenate([a.reshape(-1) for a in arrs])
    pad = -flat.shape[0] % (LANES * row_multiple)
    if pad:
        flat = jnp.concatenate([flat, jnp.zeros((pad,), flat.dtype)])
    return flat.reshape(-1, LANES)


def _unpack(packed, shapes):
    flat = packed.reshape(-1)
    out, pos = [], 0
    for sh in shapes:
        size = math.prod(sh)
        out.append(flat[pos:pos + size].reshape(sh))
        pos += size
    return out


def kernel(x, mem, mem_norm_g, a_pre_mix_g, a_post_mix_g, a_pre_ffn_g, a_post_ffn_g, a_w_in, a_conv_w, a_conv_b,
           a_gate_a_w, a_gate_a_b, a_gate_x_w, a_gate_x_b, a_lambda, a_w_mem_kv, a_w_out, a_w_ffn_in, a_w_ffn_out,
           kv_norm_g, w_kv_shared, b_pre_mix_g, b_post_mix_g, b_pre_ffn_g, b_post_ffn_g, b_w_in, b_w_mem_kv, b_w_out,
           b_w_ffn_in, b_w_ffn_out, loss_target, m_mem_norm_g, m_a_pre_mix_g, m_a_post_mix_g, m_a_pre_ffn_g,
           m_a_post_ffn_g, m_a_w_in, m_a_conv_w, m_a_conv_b, m_a_gate_a_w, m_a_gate_a_b, m_a_gate_x_w, m_a_gate_x_b,
           m_a_lambda, m_a_w_mem_kv, m_a_w_out, m_a_w_ffn_in, m_a_w_ffn_out, m_kv_norm_g, m_w_kv_shared, m_b_pre_mix_g,
           m_b_post_mix_g, m_b_pre_ffn_g, m_b_post_ffn_g, m_b_w_in, m_b_w_mem_kv, m_b_w_out, m_b_w_ffn_in, m_b_w_ffn_out,
           v_mem_norm_g, v_a_pre_mix_g, v_a_post_mix_g, v_a_pre_ffn_g, v_a_post_ffn_g, v_a_w_in, v_a_conv_w, v_a_conv_b,
           v_a_gate_a_w, v_a_gate_a_b, v_a_gate_x_w, v_a_gate_x_b, v_a_lambda, v_a_w_mem_kv, v_a_w_out, v_a_w_ffn_in,
           v_a_w_ffn_out, v_kv_norm_g, v_w_kv_shared, v_b_pre_mix_g, v_b_post_mix_g, v_b_pre_ffn_g, v_b_post_ffn_g,
           v_b_w_in, v_b_w_mem_kv, v_b_w_out, v_b_w_ffn_in, v_b_w_ffn_out):
    a = dict(locals())
    xi, yi, ci = _coords()
    chip = 2 * xi + yi

    got = small_gather(_pack([a[n] for n in SHARDED_SMALL]), "small_gather")
    per_chip = [_unpack(got[s], [a[n].shape for n in SHARDED_SMALL]) for s in range(N_CHIPS)]
    small = {n: jnp.concatenate([per_chip[s][k] for s in range(N_CHIPS)], axis=-1)
             for k, n in enumerate(SHARDED_SMALL)}
    small.update({n: a[n] for n in REPL_SMALL})

    groups = []
    for l in range(4):
        p, j = ("a", l) if l < 2 else ("b", l - 2)
        groups.append([(p + "_" + n, j) for n in ("w_in", "w_mem_kv", "w_out")])
        groups.append([(p + "_" + n, j) for n in ("w_ffn_in", "w_ffn_out")])
    groups[3].append(("w_kv_shared", None))
    big = {n: [None, None] for n in BIG if n != "w_kv_shared"}
    gs, gb = {}, {}
    reduced = {n: [None, None] for n in BIG if n != "w_kv_shared"}

    def put(store, n, j, val):
        if j is None:
            store[n] = val
        else:
            store[n][j] = val

    class Exchange:
        def __init__(self):
            self.state = {}

        def gather_ici(self, g):
            shards = [(a[n] if j is None else a[n][j]).astype(MXU) for n, j in groups[g]]
            rows = [sh.shape[0] for sh in shards]
            lands = [lax.empty((N_CHIPS,) + sh.shape, sh.dtype) for sh in shards]
            plan = plan_gather_ici(len(shards), rows)
            ss, rs, bufs, tok = split_start("gather_ici_%d" % g, shards + lands, plan, 4 * len(shards))
            self.state["g", g] = (ss, rs, bufs, plan, rows)
            return tok

        def gather_d2d(self, g, after):
            ss, rs, bufs, plan, rows = self.state.pop(("g", g))
            n = len(rows)
            outs = split_wait("gather_ici_wait_%d" % g, ss, rs, bufs, after, plan)[n:]
            plan = plan_gather_d2d(n, rows)
            ss, rs, bufs, tok = split_start("gather_d2d_%d" % g, outs, plan, 3 * n)
            self.state["g", g] = (ss, rs, bufs, plan)
            return tok

        def gather_done(self, g, after):
            ss, rs, bufs, plan = self.state.pop(("g", g))
            outs = split_wait("gather_d2d_wait_%d" % g, ss, rs, bufs, after, plan)
            for (n, j), w in zip(groups[g], outs):
                put(big, n, j, w if BIG[n] else w.reshape(-1, w.shape[-1]))

        def rs_swap(self, g):
            grads = []
            for n, j in groups[g]:
                gr = gb[n] if j is None else gb[n][j]
                grads.append(gr if BIG[n] else gr.reshape(N_CHIPS, gr.shape[0] // N_CHIPS, gr.shape[1]))
            rows = [gr.shape[1] for gr in grads]
            lands = [lax.empty((N_CHIPS, gr.shape[1] // 2, gr.shape[2]), gr.dtype) for gr in grads]
            plan = plan_swap(len(grads), rows)
            ss, rs, bufs, tok = split_start("rs_swap_%d" % g, grads + lands, plan, len(grads))
            self.state["r", g] = (ss, rs, bufs, plan, rows)
            return tok

        def rs_exchange(self, g, after):
            ss, rs, bufs, plan, rows = self.state.pop(("r", g))
            n = len(rows)
            bufs = split_wait("rs_swap_wait_%d" % g, ss, rs, bufs, after, plan)
            sums = [half_sum(gr, got, "rs_half_sum") for gr, got in zip(bufs[:n], bufs[n:])]
            plan = plan_exchange(n)
            ss, rs, bufs, tok = split_start("rs_exchange_%d" % g, [p for p, _ in sums] + [s for _, s in sums], plan,
                                            3 * n)
            self.state["r", g] = (ss, rs, bufs, plan, rows)
            return tok

        def rs_share(self, g, after):
            ss, rs, bufs, plan, rows = self.state.pop(("r", g))
            n = len(rows)
            slots = split_wait("rs_exchange_wait_%d" % g, ss, rs, bufs, after, plan)[n:]
            fulls = [slot_sum(s, "rs_slot_sum") for s in slots]
            plan = plan_share(n, rows)
            ss, rs, bufs, tok = split_start("rs_share_%d" % g, fulls, plan, n)
            self.state["r", g] = (ss, rs, bufs, plan)
            return tok

        def rs_done(self, g, after):
            ss, rs, bufs, plan = self.state.pop(("r", g))
            outs = split_wait("rs_share_wait_%d" % g, ss, rs, bufs, after, plan)
            for (n, j), r in zip(groups[g], outs):
                put(reduced, n, j, r)

        def hook(self, where, l, after):
            mix, ffn = 2 * l, 2 * l + 1
            toks = []
            if where == "fwd_begin":
                if l == 0:
                    tok = self.gather_ici(mix)
                    tok = self.gather_d2d(mix, tok)
                    self.gather_done(mix, tok)
                toks.append(self.gather_ici(ffn))
            elif where == "fwd_q1":
                toks.append(self.gather_d2d(ffn, after))
            elif where == "fwd_mid":
                self.gather_done(ffn, after)
                if l < 3:
                    toks.append(self.gather_ici(mix + 2))
            elif where == "fwd_q3":
                if l < 3:
                    toks.append(self.gather_d2d(mix + 2, after))
            elif where == "fwd_end":
                if l < 3:
                    self.gather_done(mix + 2, after)
            elif where == "bwd_begin":
                if l < 3:
                    self.rs_done(ffn + 2, after)
                    toks.append(self.rs_exchange(mix + 2, after))
            elif where == "bwd_mid1":
                if l < 3:
                    toks.append(self.rs_share(mix + 2, after))
            elif where == "bwd_mid2":
                if l < 3:
                    self.rs_done(mix + 2, after)
                toks.append(self.rs_swap(ffn))
            elif where == "bwd_m1":
                toks.append(self.rs_exchange(ffn, after))
            elif where == "bwd_end":
                toks.append(self.rs_share(ffn, after))
                toks.append(self.rs_swap(mix))
                if l == 0:
                    self.rs_done(ffn, toks[0])
                    tok = self.rs_exchange(mix, toks[1])
                    tok = self.rs_share(mix, adamw_big([n for n in BIG if n.startswith("b_")], tok))
                    self.rs_done(mix, tok)
                    toks = []
            else:
                raise ValueError(where)
            return toks

    res = {}

    def adamw_big(names, token=None):
        last = None
        for n in names:
            shape = a[n].shape
            rows, cols = shape[-2], shape[-1]
            stk = (-1, rows, cols)
            grads = reduced[n] if isinstance(reduced[n], list) else [reduced[n]]
            if token is not None:
                grads = [tie(grads[0], [token], "tie_adamw_" + n)] + grads[1:]
            outs = adamw(a[n].reshape(stk), grads, a["m_" + n].reshape(stk), a["v_" + n].reshape(stk), "adamw")
            res[n] = [o.reshape(shape) for o in outs]
            last = outs[1]
            token = last if token is not None else None
        return last

    loss_parts, dx = _fwd_bwd(x[0], mem[0], loss_target[0], small, big, gs, gb, Exchange())
    loss = lax.psum(jnp.sum(loss_parts) * (0.5 / D), ("x", "y", "c"))

    def full(n):
        g = gs[n]
        return jnp.stack(g) if isinstance(g, list) else g

    order = SHARDED_SMALL + REPL_SMALL
    full_shapes = [full(n).shape for n in order]
    pack = _pack([full(n) for n in order], 512)
    ss, rs, bufs, tok = split_start("small_swap", [pack, lax.empty(pack.shape, F32)], plan_small_swap, 1)
    mine_v, sib_v = split_wait("small_swap_wait", ss, rs, bufs, tok, plan_small_swap)
    ss, rs, bufs, tok = split_start("small_exchange", [small_pair(mine_v, sib_v, "small_pair")],
                                    plan_small_exchange, 3)
    last = adamw_big([n for n in BIG if n not in res], tok)
    slots = split_wait("small_exchange_wait", ss, rs, bufs, last, plan_small_exchange)[0]
    summed = _unpack(small_total(slots, "small_total"), full_shapes)
    mine = []
    for n, g in zip(order, summed):
        if n in SHARDED_SMALL:
            width = a[n].shape[-1]
            g = lax.dynamic_slice_in_dim(g, chip * width, width, axis=g.ndim - 1)
        mine.append(g.reshape(a[n].shape))
    shapes = [a[n].shape for n in order]
    rm = 512
    outs = adamw(_pack([a[n] for n in order], rm)[None], [_pack(mine, rm)],
                 _pack([a["m_" + n] for n in order], rm)[None], _pack([a["v_" + n] for n in order], rm)[None],
                 "adamw_small")
    unpacked = [_unpack(o[0], shapes) for o in outs]
    for k, n in enumerate(order):
        res[n] = [u[k] for u in unpacked]

    return (loss, dx[None], *[res[n][0] for n in WEIGHTS], *[res[n][1] for n in WEIGHTS],
            *[res[n][2] for n in WEIGHTS], *[res[n][3] for n in WEIGHTS])
```
